```python
import math
import jax
import jax.numpy as jnp
from jax import lax
import numpy as np

D_MODEL = 2048
BATCH = 2
SEQ = 4096
DEPTH = 1

CTX_LEN = 256
GRID_W = 64
S5_WIDTH = D_MODEL // 2
S5_GROUP = 16
S5_GROUPS = S5_WIDTH // S5_GROUP
S5_STATE = 64
RWKV_WIDTH = D_MODEL // 2
RWKV_HEAD = 64
RWKV_HEADS = RWKV_WIDTH // RWKV_HEAD
DECAY_LORA = 64
ICLR_LORA = 64
GATE_LORA = 128
RWKV_SHIFT_COLS = 3 * RWKV_WIDTH + 2 * DECAY_LORA + 2 * ICLR_LORA + GATE_LORA
IN_COLS = S5_WIDTH + RWKV_SHIFT_COLS + 2 * D_MODEL
D_FF = -(-(8 * D_MODEL) // (3 * 256)) * 256
N_MOD = 6
NORM_EPS = 1e-6
GN_EPS = 64e-5
DT_MIN = 1e-3
DT_MAX = 1e-1

kernel_name = 'hybrid_s5_rwkv7_flow_block'


def rms_norm(x, w):
    xf = x.astype(jnp.float32)
    y = xf * lax.rsqrt(jnp.mean(xf * xf, axis=-1, keepdims=True) + NORM_EPS)
    return (y * w.astype(jnp.float32)).astype(x.dtype)


def modulate(h, shift, scale):
    return h * (1 + scale) + shift


def grid_neighbour_mean(z):
    bsz, length, ch = z.shape
    rows = length // GRID_W
    g = z.reshape(bsz, rows, GRID_W, ch)
    p = jnp.pad(g, ((0, 0), (1, 1), (1, 1), (0, 0)))
    s = p[:, :-2, 1:-1] + p[:, 2:, 1:-1] + p[:, 1:-1, :-2] + p[:, 1:-1, 2:]
    ri = jnp.arange(rows)
    ci = jnp.arange(GRID_W)
    cnt = ((ri > 0).astype(z.dtype) + (ri < rows - 1).astype(z.dtype))[:, None] + ((ci > 0).astype(z.dtype) + (ci < GRID_W - 1).astype(z.dtype))[None, :]
    return (s / cnt[None, :, :, None]).reshape(bsz, length, ch)


def seq_neighbour_mean(z):
    length = z.shape[1]
    p = jnp.pad(z, ((0, 0), (1, 1), (0, 0)))
    i = jnp.arange(length)
    cnt = (i > 0).astype(z.dtype) + (i < length - 1).astype(z.dtype)
    return (p[:, :-2] + p[:, 2:]) / cnt[None, :, None]


def _complex(re, im):
    return lax.complex(re.astype(jnp.float32), im.astype(jnp.float32))


def s5_scan(bu, abar, s0):
    a = jnp.broadcast_to(abar, (bu.shape[0], 1) + abar.shape)

    def combine(e1, e2):
        a1, b1 = e1
        a2, b2 = e2
        return a1 * a2, a2 * b1 + b2

    a_cum, b_cum = lax.associative_scan(combine, (a, bu), axis=0)
    return b_cum + a_cum * s0[None]


def s5_mixer(u, lp, s0):
    bsz, length, _ = u.shape
    ug = jnp.swapaxes(u.astype(jnp.float32).reshape(bsz, length, S5_GROUPS, S5_GROUP), 0, 1)
    ugc = ug.astype(jnp.complex64)
    y = lp['s5_d'].astype(jnp.float32) * ug
    finals = []
    for di in range(2):
        a = _complex(lp['s5_a_re'][di], lp['s5_a_im'][di])
        dt = jnp.exp(lp['s5_log_dt'][di].astype(jnp.float32))[:, None]
        abar = jnp.exp(a * dt)
        bbar = ((abar - 1) / a)[..., None] * _complex(lp['s5_b_re'][di], lp['s5_b_im'][di])
        bu = jnp.einsum('lbgh,gph->lbgp', ugc, bbar)
        if di == 1:
            bu = bu[::-1]
        states = s5_scan(bu, abar, s0[di])
        finals.append(states[-1])
        if di == 1:
            states = states[::-1]
        cm = _complex(lp['s5_c_re'][di], lp['s5_c_im'][di])
        y = y + jnp.real(jnp.einsum('lbgp,ghp->lbgh', states, cm))
    y = jnp.swapaxes(y, 0, 1).reshape(bsz, length, S5_WIDTH)
    return y.astype(u.dtype), (finals[0], finals[1])


def rwkv_scan(r, w, k, v, kk, a, s0):
    def step(s, inp):
        r_t, w_t, k_t, v_t, kk_t, a_t = inp
        s_kk = jnp.einsum('bhvk,bhk->bhv', s, kk_t)
        s = s * w_t[:, :, None, :] - s_kk[..., None] * (kk_t * a_t)[:, :, None, :] + v_t[..., None] * k_t[:, :, None, :]
        return s, jnp.einsum('bhvk,bhk->bhv', s, r_t)

    s_fin, y = lax.scan(step, s0, (r, w, k, v, kk, a))
    return y, s_fin


def rwkv_mixer(z, lp, s0):
    bsz, length, _ = z.shape
    zf = z.astype(jnp.float32)
    rw = RWKV_WIDTH
    r = zf[..., :rw]
    k = zf[..., rw:2 * rw]
    v = zf[..., 2 * rw:3 * rw]
    o = 3 * rw
    wd = zf[..., o:o + 2 * DECAY_LORA].reshape(bsz, length, 2, DECAY_LORA)
    o += 2 * DECAY_LORA
    ad = zf[..., o:o + 2 * ICLR_LORA].reshape(bsz, length, 2, ICLR_LORA)
    o += 2 * ICLR_LORA
    gd = zf[..., o:o + GATE_LORA]
    g = jax.nn.sigmoid(gd) @ lp['rw_g2'].astype(jnp.float32)

    def heads(t):
        return jnp.swapaxes(t.reshape(bsz, length, RWKV_HEADS, RWKV_HEAD), 0, 1)

    kk = heads(k * lp['rw_k_k'].astype(jnp.float32))
    kk = kk * lax.rsqrt(jnp.sum(kk * kk, axis=-1, keepdims=True) + 1e-12)
    r_h = heads(r)
    v_h = heads(v)
    r_k = lp['rw_r_k'].astype(jnp.float32)
    ys = []
    bonuses = []
    finals = []
    for di in range(2):
        w = -jax.nn.softplus(-(lp['rw_w0'][di].astype(jnp.float32) + jnp.tanh(wd[:, :, di]) @ lp['rw_w2'][di].astype(jnp.float32))) - 0.5
        decay = jnp.exp(-jnp.exp(w))
        a = jax.nn.sigmoid(lp['rw_a0'][di].astype(jnp.float32) + ad[:, :, di] @ lp['rw_a2'][di].astype(jnp.float32))
        k_d = heads(k * (1 + (a - 1) * lp['rw_k_a'].astype(jnp.float32)))
        seqs = (r_h, heads(decay), k_d, v_h, kk, heads(a))
        if di == 1:
            seqs = tuple(t[::-1] for t in seqs)
        y_d, s_fin = rwkv_scan(*seqs, s0[di])
        if di == 1:
            y_d = y_d[::-1]
        ys.append(y_d)
        bonuses.append(jnp.sum(r_h * k_d * r_k, axis=-1, keepdims=True) * v_h)
        finals.append(s_fin)
    y = ys[0] + ys[1]
    mu = jnp.mean(y, axis=-1, keepdims=True)
    var = jnp.mean(jnp.square(y - mu), axis=-1, keepdims=True)
    y = (y - mu) * lax.rsqrt(var + GN_EPS)
    ln_w = lp['rw_ln_w'].astype(jnp.float32).reshape(RWKV_HEADS, RWKV_HEAD)
    ln_b = lp['rw_ln_b'].astype(jnp.float32).reshape(RWKV_HEADS, RWKV_HEAD)
    y = y * ln_w + ln_b + bonuses[0] + bonuses[1]
    y = jnp.swapaxes(y, 0, 1).reshape(bsz, length, rw) * g
    return y.astype(z.dtype), (finals[0], finals[1])


def token_mix(h, neighbour_mean, s5_init, rw_init, lp, with_output):
    z = jnp.einsum('bld,dc->blc', h, lp['w_in'])
    u_s5 = z[..., :S5_WIDTH]
    z_rw = z[..., S5_WIDTH:S5_WIDTH + RWKV_SHIFT_COLS]
    z_rw = z_rw + (neighbour_mean(z_rw) - z_rw) * lp['rw_mu']
    y_s5, s5_fin = s5_mixer(u_s5, lp, s5_init)
    y_rw, rw_fin = rwkv_mixer(z_rw, lp, rw_init)
    if not with_output:
        return None, s5_fin, rw_fin
    gates = jax.nn.sigmoid(z[..., S5_WIDTH + RWKV_SHIFT_COLS:])
    glu = jax.nn.gelu(y_s5) @ lp['s5_glu_w']
    s5_out = glu[..., :D_MODEL] * jax.nn.sigmoid(glu[..., D_MODEL:])
    rw_out = y_rw @ lp['rw_proj']
    merged = gates[..., :D_MODEL] * s5_out + gates[..., D_MODEL:] * rw_out
    return merged @ lp['w_o'], s5_fin, rw_fin


def swiglu(h, w13, w2):
    z = h @ w13
    return (jax.nn.silu(z[..., :D_FF]) * z[..., D_FF:]) @ w2


def setup_inputs(seed: int = 0) -> dict:
    key = jax.random.key(seed)
    ks = jax.random.split(key, 40)

    def nrm(k, shape, s):
        return jax.random.normal(k, shape, jnp.float32) * s

    dp = DEPTH
    g_, p_, hg = S5_GROUPS, S5_STATE, S5_GROUP
    a_im_base = math.pi * jnp.arange(p_, dtype=jnp.float32)
    return {
        'x': nrm(ks[0], (BATCH, SEQ, D_MODEL), 1.0),
        'c': nrm(ks[1], (BATCH, D_MODEL), 1.0),
        'ctx': nrm(ks[2], (BATCH, CTX_LEN, D_MODEL), 1.0),
        'c_ctx': nrm(ks[3], (D_MODEL,), 1.0),
        'ada_w': nrm(ks[4], (dp, D_MODEL, N_MOD * D_MODEL), 0.5 * D_MODEL ** -0.5),
        'ada_b': nrm(ks[5], (dp, N_MOD * D_MODEL), 0.02),
        'norm1_w': 1.0 + nrm(ks[6], (dp, D_MODEL), 0.02),
        'w_in': nrm(ks[7], (dp, D_MODEL, IN_COLS), D_MODEL ** -0.5),
        'rw_mu': jax.random.uniform(ks[8], (dp, RWKV_SHIFT_COLS), jnp.float32),
        's5_a_re': -0.5 + nrm(ks[9], (dp, 2, g_, p_), 0.01),
        's5_a_im': a_im_base + nrm(ks[10], (dp, 2, g_, p_), 0.01),
        's5_log_dt': jax.random.uniform(ks[11], (dp, 2, g_), jnp.float32, math.log(DT_MIN), math.log(DT_MAX)),
        's5_b_re': nrm(ks[12], (dp, 2, g_, p_, hg), (2 * hg) ** -0.5),
        's5_b_im': nrm(ks[13], (dp, 2, g_, p_, hg), (2 * hg) ** -0.5),
        's5_c_re': nrm(ks[14], (dp, 2, g_, hg, p_), (2 * p_) ** -0.5),
        's5_c_im': nrm(ks[15], (dp, 2, g_, hg, p_), (2 * p_) ** -0.5),
        's5_d': nrm(ks[16], (dp, g_, hg), 1.0),
        's5_glu_w': nrm(ks[17], (dp, S5_WIDTH, 2 * D_MODEL), S5_WIDTH ** -0.5),
        'rw_w0': jax.random.uniform(ks[18], (dp, 2, RWKV_WIDTH), jnp.float32, -6.0, -1.0),
        'rw_w2': nrm(ks[19], (dp, 2, DECAY_LORA, RWKV_WIDTH), 0.1 * DECAY_LORA ** -0.5),
        'rw_a0': nrm(ks[20], (dp, 2, RWKV_WIDTH), 0.1),
        'rw_a2': nrm(ks[21], (dp, 2, ICLR_LORA, RWKV_WIDTH), 0.1 * ICLR_LORA ** -0.5),
        'rw_g2': nrm(ks[22], (dp, GATE_LORA, RWKV_WIDTH), GATE_LORA ** -0.5),
        'rw_k_k': 0.85 + nrm(ks[23], (dp, RWKV_WIDTH), 0.02),
        'rw_k_a': 1.0 + nrm(ks[24], (dp, RWKV_WIDTH), 0.02),
        'rw_r_k': nrm(ks[25], (dp, RWKV_HEADS, RWKV_HEAD), 0.1),
        'rw_ln_w': 1.0 + nrm(ks[26], (dp, RWKV_WIDTH), 0.02),
        'rw_ln_b': nrm(ks[27], (dp, RWKV_WIDTH), 0.02),
        'rw_proj': nrm(ks[28], (dp, RWKV_WIDTH, D_MODEL), RWKV_WIDTH ** -0.5),
        'w_o': nrm(ks[29], (dp, D_MODEL, D_MODEL), D_MODEL ** -0.5),
        'norm2_w': 1.0 + nrm(ks[30], (dp, D_MODEL), 0.02),
        'ffn_w13': nrm(ks[31], (dp, D_MODEL, 2 * D_FF), D_MODEL ** -0.5),
        'ffn_w2': nrm(ks[32], (dp, D_FF, D_MODEL), D_FF ** -0.5),
        'norm_f': 1.0 + nrm(ks[33], (D_MODEL,), 0.02),
    }


def reference(x, c, ctx, c_ctx, ada_w, ada_b, norm1_w, w_in, rw_mu, s5_a_re, s5_a_im, s5_log_dt,
              s5_b_re, s5_b_im, s5_c_re, s5_c_im, s5_d, s5_glu_w, rw_w0, rw_w2, rw_a0, rw_a2, rw_g2,
              rw_k_k, rw_k_a, rw_r_k, rw_ln_w, rw_ln_b, rw_proj, w_o, norm2_w, ffn_w13, ffn_w2, norm_f):
    bsz = x.shape[0]
    s5_zero = jnp.zeros((bsz, S5_GROUPS, S5_STATE), jnp.complex64)
    rw_zero = jnp.zeros((bsz, RWKV_HEADS, RWKV_HEAD, RWKV_HEAD), jnp.float32)
    h_lat = x
    h_ctx = ctx
    for i in range(DEPTH):
        lp = {
            'w_in': w_in[i], 'rw_mu': rw_mu[i],
            's5_a_re': s5_a_re[i], 's5_a_im': s5_a_im[i], 's5_log_dt': s5_log_dt[i],
            's5_b_re': s5_b_re[i], 's5_b_im': s5_b_im[i], 's5_c_re': s5_c_re[i], 's5_c_im': s5_c_im[i],
            's5_d': s5_d[i], 's5_glu_w': s5_glu_w[i],
            'rw_w0': rw_w0[i], 'rw_w2': rw_w2[i], 'rw_a0': rw_a0[i], 'rw_a2': rw_a2[i], 'rw_g2': rw_g2[i],
            'rw_k_k': rw_k_k[i], 'rw_k_a': rw_k_a[i], 'rw_r_k': rw_r_k[i],
            'rw_ln_w': rw_ln_w[i], 'rw_ln_b': rw_ln_b[i], 'rw_proj': rw_proj[i], 'w_o': w_o[i],
        }
        last = i == DEPTH - 1
        mod = jax.nn.silu(c) @ ada_w[i] + ada_b[i]
        mod_c = jax.nn.silu(c_ctx) @ ada_w[i] + ada_b[i]
        sh1, sc1, g1, sh2, sc2, g2 = jnp.split(mod[:, None, :], N_MOD, axis=-1)
        sh1c, sc1c, g1c, sh2c, sc2c, g2c = jnp.split(mod_c, N_MOD, axis=-1)
        ctx_out, s5_ctx, rw_ctx = token_mix(modulate(rms_norm(h_ctx, norm1_w[i]), sh1c, sc1c), seq_neighbour_mean,
                                            (s5_zero, s5_zero), (rw_zero, rw_zero), lp, not last)
        lat_out, _, _ = token_mix(modulate(rms_norm(h_lat, norm1_w[i]), sh1, sc1), grid_neighbour_mean,
                                  s5_ctx, rw_ctx, lp, True)
        h_lat = h_lat + g1 * lat_out
        h_lat = h_lat + g2 * swiglu(modulate(rms_norm(h_lat, norm2_w[i]), sh2, sc2), ffn_w13[i], ffn_w2[i])
        if not last:
            h_ctx = h_ctx + g1c * ctx_out
            h_ctx = h_ctx + g2c * swiglu(modulate(rms_norm(h_ctx, norm2_w[i]), sh2c, sc2c), ffn_w13[i], ffn_w2[i])
    return rms_norm(h_lat, norm_f)
```

```python
import functools
import math

import jax
import jax.numpy as jnp
from jax import lax
from jax.experimental import pallas as pl
from jax.experimental.pallas import tpu as pltpu

F32 = jnp.float32
BF16 = jnp.bfloat16
HIGHEST = lax.Precision.HIGHEST

D_MODEL = 2048
N_MOD = 6
NORM_EPS = 1e-6
GN_EPS = 64e-5
GRID_W = 64
S5_GROUP = 16
S5_STATE = 64
S5_CHUNK = 16
RW_HEAD = 64
RW_CHUNK = 64
LANES = 128
VMEM_LIMIT = 48 * 1024 * 1024


def _cparams(sem):
    return pltpu.CompilerParams(dimension_semantics=sem, vmem_limit_bytes=VMEM_LIMIT)


def _operands(a, b, precision):
    if precision == "bf16":
        return a.astype(BF16), b.astype(BF16), None
    return a, b, precision


def _dot(a, b, precision=None):
    a, b, precision = _operands(a, b, precision)
    return jnp.dot(a, b, preferred_element_type=F32, precision=precision)


def _dot_nt(a, b, precision=None):
    a, b, precision = _operands(a, b, precision)
    return lax.dot_general(a, b, (((1,), (1,)), ((), ())), preferred_element_type=F32, precision=precision)


def _sigmoid(x):
    return 1.0 / (1.0 + jnp.exp(-x))


def _silu(x):
    return x * _sigmoid(x)


def _gelu_tanh(x):
    c = math.sqrt(2.0 / math.pi)
    return 0.5 * x * (1.0 + jnp.tanh(c * (x + 0.044715 * (x * x * x))))


def _softplus(x):
    return jnp.maximum(x, 0.0) + jnp.log(1.0 + jnp.exp(-jnp.abs(x)))


def _mod_kernel(c_ref, w_ref, b_ref, o_ref):
    o_ref[...] = _dot(_silu(c_ref[...]), w_ref[...], HIGHEST) + b_ref[...]


def _modulation(c_rows, ada_w, ada_b):
    m, d = c_rows.shape
    n = ada_w.shape[1]
    tn = 1024
    return pl.pallas_call(
        _mod_kernel,
        out_shape=jax.ShapeDtypeStruct((m, n), F32),
        grid=(n // tn,),
        in_specs=[pl.BlockSpec((m, d), lambda j: (0, 0)),
                  pl.BlockSpec((d, tn), lambda j: (0, j)),
                  pl.BlockSpec((1, tn), lambda j: (0, j))],
        out_specs=pl.BlockSpec((m, tn), lambda j: (0, j)),
        compiler_params=_cparams(("arbitrary",)),
        name="modulation",
    )(c_rows, ada_w, ada_b.reshape(1, n))


def _lnmod_rows(x, nw, sh, sc):
    ms = jnp.mean(x * x, axis=-1, keepdims=True)
    y = x * lax.rsqrt(ms + NORM_EPS) * nw
    return y * (1.0 + sc) + sh


def _lnmod_mm_kernel(x_ref, nw_ref, sh_ref, sc_ref, w_ref, o_ref, h_scr, *, epilogue):
    @pl.when(pl.program_id(1) == 0)
    def _():
        h_scr[...] = _lnmod_rows(x_ref[...], nw_ref[...], sh_ref[0], sc_ref[0]).astype(BF16)

    z = _dot(h_scr[...], w_ref[...])
    if epilogue == "sigmoid":
        z = _sigmoid(z)
    o_ref[...] = z.astype(o_ref.dtype)


def _lnmod_swiglu_kernel(x_ref, nw_ref, sh_ref, sc_ref, w1_ref, w3_ref, o_ref, h_scr):
    @pl.when(pl.program_id(1) == 0)
    def _():
        h_scr[...] = _lnmod_rows(x_ref[...], nw_ref[...], sh_ref[0], sc_ref[0]).astype(BF16)

    h = h_scr[...]
    o_ref[...] = (_silu(_dot(h, w1_ref[...])) * _dot(h, w3_ref[...])).astype(o_ref.dtype)


def _lnmod_matmul(x2, nw, sh_tab, sc_tab, mod_row_of_block, w, *, tm, tn, out_dtype, epilogue=None, name):
    m, d = x2.shape
    n = w.shape[1]
    mod_map = lambda i, j: (mod_row_of_block(i), 0, 0)
    return pl.pallas_call(
        functools.partial(_lnmod_mm_kernel, epilogue=epilogue),
        out_shape=jax.ShapeDtypeStruct((m, n), out_dtype),
        grid=(m // tm, n // tn),
        in_specs=[pl.BlockSpec((tm, d), lambda i, j: (i, 0)),
                  pl.BlockSpec((1, d), lambda i, j: (0, 0)),
                  pl.BlockSpec((1, 1, d), mod_map),
                  pl.BlockSpec((1, 1, d), mod_map),
                  pl.BlockSpec((d, tn), lambda i, j: (0, j))],
        out_specs=pl.BlockSpec((tm, tn), lambda i, j: (i, j)),
        scratch_shapes=[pltpu.VMEM((tm, d), BF16)],
        compiler_params=_cparams(("parallel", "arbitrary")),
        name=name,
    )(x2, nw.reshape(1, d), sh_tab, sc_tab, w)


def _lnmod_swiglu(x2, nw, sh_tab, sc_tab, mod_row_of_block, w13, d_ff, *, tm, tn, name):
    m, d = x2.shape
    nj = d_ff // tn
    mod_map = lambda i, j: (mod_row_of_block(i), 0, 0)
    return pl.pallas_call(
        _lnmod_swiglu_kernel,
        out_shape=jax.ShapeDtypeStruct((m, d_ff), BF16),
        grid=(m // tm, nj),
        in_specs=[pl.BlockSpec((tm, d), lambda i, j: (i, 0)),
                  pl.BlockSpec((1, d), lambda i, j: (0, 0)),
                  pl.BlockSpec((1, 1, d), mod_map),
                  pl.BlockSpec((1, 1, d), mod_map),
                  pl.BlockSpec((d, tn), lambda i, j: (0, j)),
                  pl.BlockSpec((d, tn), lambda i, j: (0, j + nj))],
        out_specs=pl.BlockSpec((tm, tn), lambda i, j: (i, j)),
        scratch_shapes=[pltpu.VMEM((tm, d), BF16)],
        compiler_params=_cparams(("parallel", "arbitrary")),
        name=name,
    )(x2, nw.reshape(1, d), sh_tab, sc_tab, w13, w13)


def _s5_param_kernel(are_ref, aim_ref, ldt_ref, bre_ref, bim_ref, cre_ref, cim_ref,
                     e_ref, cs_ref, kt_ref, a16_ref):
    t_n, hg, p_n = S5_CHUNK, S5_GROUP, S5_STATE
    tau = lax.broadcasted_iota(jnp.int32, (t_n, 1, p_n), 0).astype(F32)
    for d in range(2):
        a_re = are_ref[0, d:d + 1, :]
        a_im = aim_ref[0, d:d + 1, :]
        dt = jnp.exp(ldt_ref[0, d:d + 1, :])
        lam = a_re * dt
        th = a_im * dt
        er = jnp.exp(lam)
        ab_re = er * jnp.cos(th)
        ab_im = er * jnp.sin(th)
        den = a_re * a_re + a_im * a_im
        x_re = ab_re - 1.0
        co_re = (x_re * a_re + ab_im * a_im) / den
        co_im = (ab_im * a_re - x_re * a_im) / den
        bt_re = bre_ref[0, d]
        bt_im = bim_ref[0, d]
        bb_re = co_re * bt_re - co_im * bt_im
        bb_im = co_re * bt_im + co_im * bt_re
        c_re = cre_ref[0, d]
        c_im = cim_ref[0, d]

        def power(tv):
            mag = jnp.exp(tv * lam)
            return mag * jnp.cos(tv * th), mag * jnp.sin(tv * th)

        pw_re, pw_im = power(tau)
        cp_re = (c_re[None] * pw_re - c_im[None] * pw_im).reshape(t_n * hg, p_n)
        cp_im = (c_re[None] * pw_im + c_im[None] * pw_re).reshape(t_n * hg, p_n)
        kt_ref[0, d] = _dot_nt(bb_re, cp_re, HIGHEST) - _dot_nt(bb_im, cp_im, HIGHEST)

        te = (t_n - 1.0 - tau) if d == 0 else tau
        pe_re, pe_im = power(te)
        e_ref[0, d, :, 0:p_n] = (pe_re * bb_re[None] - pe_im * bb_im[None]).reshape(t_n * hg, p_n)
        e_ref[0, d, :, p_n:2 * p_n] = (pe_re * bb_im[None] + pe_im * bb_re[None]).reshape(t_n * hg, p_n)

        tc = (tau + 1.0) if d == 0 else (t_n - tau)
        pc_re, pc_im = power(tc)
        cs_ref[0, d, :, 0:p_n] = (c_re[None] * pc_re - c_im[None] * pc_im).reshape(t_n * hg, p_n)
        cs_ref[0, d, :, p_n:2 * p_n] = -(c_re[None] * pc_im + c_im[None] * pc_re).reshape(t_n * hg, p_n)

        mag16 = jnp.exp(float(t_n) * lam)
        a16_ref[0, d, 0:1, :] = mag16 * jnp.cos(float(t_n) * th)
        a16_ref[0, d, 1:2, :] = mag16 * jnp.sin(float(t_n) * th)


def _s5_params(a_re, a_im, log_dt, b_re, b_im, c_re, c_im):
    g_n = a_re.shape[1]
    p_n, hg, t_n = S5_STATE, S5_GROUP, S5_CHUNK
    tr = lambda a: jnp.swapaxes(a, 0, 1)
    ldt = jnp.broadcast_to(tr(log_dt)[:, :, None], (g_n, 2, p_n))
    spec3 = pl.BlockSpec((1, 2, p_n), lambda g: (g, 0, 0))
    spec4 = pl.BlockSpec((1, 2, hg, p_n), lambda g: (g, 0, 0, 0))
    th = t_n * hg
    return pl.pallas_call(
        _s5_param_kernel,
        out_shape=(jax.ShapeDtypeStruct((g_n, 2, th, 2 * p_n), F32),
                   jax.ShapeDtypeStruct((g_n, 2, th, 2 * p_n), F32),
                   jax.ShapeDtypeStruct((g_n, 2, hg, th), F32),
                   jax.ShapeDtypeStruct((g_n, 2, 2, p_n), F32)),
        grid=(g_n,),
        in_specs=[spec3, spec3, spec3, spec4, spec4, spec4, spec4],
        out_specs=(pl.BlockSpec((1, 2, th, 2 * p_n), lambda g: (g, 0, 0, 0)),
                   pl.BlockSpec((1, 2, th, 2 * p_n), lambda g: (g, 0, 0, 0)),
                   pl.BlockSpec((1, 2, hg, th), lambda g: (g, 0, 0, 0)),
                   pl.BlockSpec((1, 2, 2, p_n), lambda g: (g, 0, 0, 0))),
        compiler_params=_cparams(("parallel",)),
        name="s5_params",
    )(tr(a_re), tr(a_im), ldt,
      jnp.transpose(b_re, (1, 0, 3, 2)), jnp.transpose(b_im, (1, 0, 3, 2)), tr(c_re), tr(c_im))


def _s5_state_in_kernel(u_ref, e_ref, o_ref):
    e = e_ref[0]
    u = u_ref[0]
    o_ref[0, :, 0:LANES] = _dot(u, e[0], HIGHEST)
    o_ref[0, :, LANES:2 * LANES] = _dot(u, e[1], HIGHEST)


def _s5_state_inputs(u_g, e):
    g_n, rows, th = u_g.shape
    return pl.pallas_call(
        _s5_state_in_kernel,
        out_shape=jax.ShapeDtypeStruct((g_n, rows, 2 * LANES), F32),
        grid=(g_n,),
        in_specs=[pl.BlockSpec((1, rows, th), lambda g: (g, 0, 0)),
                  pl.BlockSpec((1, 2, th, LANES), lambda g: (g, 0, 0, 0))],
        out_specs=pl.BlockSpec((1, rows, 2 * LANES), lambda g: (g, 0, 0)),
        compiler_params=_cparams(("parallel",)),
        name="s5_state_inputs",
    )(u_g, e)


def _s5_scan_kernel(e_ref, a_ref, o_ref, *, n_ctx, n_all):
    zero = jnp.zeros(e_ref.shape[2:], F32)

    def step(pr, pi):
        ar, ai = a_ref[pr], a_ref[pi]

        def body(c, carry):
            sr, si = carry
            o_ref[pr, c] = sr
            o_ref[pi, c] = si
            return (ar * sr - ai * si + e_ref[pr, c], ar * si + ai * sr + e_ref[pi, c])
        return body

    lax.fori_loop(0, n_all, step(0, 1), (zero, zero))
    bwd = step(2, 3)
    carry = lax.fori_loop(0, n_ctx, lambda k, cy: bwd(n_ctx - 1 - k, cy), (zero, zero))
    lax.fori_loop(0, n_all - n_ctx, lambda k, cy: bwd(n_all - 1 - k, cy), carry)


def _s5_scan(e_planes, a_planes, n_ctx, n_all):
    _, rows, r_n, _ = e_planes.shape
    bsz = rows // n_all
    sub = 8
    return pl.pallas_call(
        functools.partial(_s5_scan_kernel, n_ctx=n_ctx, n_all=n_all),
        out_shape=jax.ShapeDtypeStruct(e_planes.shape, F32),
        grid=(bsz, r_n // sub),
        in_specs=[pl.BlockSpec((4, n_all, sub, LANES), lambda b, q: (0, b, q, 0)),
                  pl.BlockSpec((4, sub, LANES), lambda b, q: (0, q, 0))],
        out_specs=pl.BlockSpec((4, n_all, sub, LANES), lambda b, q: (0, b, q, 0)),
        compiler_params=_cparams(("parallel", "parallel")),
        name="s5_scan",
    )(e_planes, a_planes)


def _s5_apply_kernel(u_ref, s_ref, mf_ref, mb_ref, d_ref, cs_ref, o_ref, *, n_ctx):
    u = u_ref[0, n_ctx:, :]
    s = s_ref[0, n_ctx:, :]
    y = _dot(u, mf_ref[0] + mb_ref[0], HIGHEST) + u * d_ref[0]
    y = y + _dot_nt(s[:, 0:LANES], cs_ref[0, 0], HIGHEST) + _dot_nt(s[:, LANES:2 * LANES], cs_ref[0, 1], HIGHEST)
    o_ref[0, 0] = y


def _s5_apply(u_g, s_g, m_f, m_b, d_t, cs, n_ctx, n_all):
    g_n, rows, th = u_g.shape
    bsz = rows // n_all
    n_lat = n_all - n_ctx
    return pl.pallas_call(
        functools.partial(_s5_apply_kernel, n_ctx=n_ctx),
        out_shape=jax.ShapeDtypeStruct((g_n, bsz, n_lat, th), F32),
        grid=(g_n, bsz),
        in_specs=[pl.BlockSpec((1, n_all, th), lambda g, b: (g, b, 0)),
                  pl.BlockSpec((1, n_all, 2 * LANES), lambda g, b: (g, b, 0)),
                  pl.BlockSpec((1, th, th), lambda g, b: (g, 0, 0)),
                  pl.BlockSpec((1, th, th), lambda g, b: (g, 0, 0)),
                  pl.BlockSpec((1, 1, th), lambda g, b: (g, 0, 0)),
                  pl.BlockSpec((1, 2, th, LANES), lambda g, b: (g, 0, 0, 0))],
        out_specs=pl.BlockSpec((1, 1, n_lat, th), lambda g, b: (g, b, 0, 0)),
        compiler_params=_cparams(("parallel", "arbitrary")),
        name="s5_apply",
    )(u_g, s_g, m_f, m_b, d_t, cs)


def _s5_branch(u_all, n_ctx_tok, a_re, a_im, log_dt, b_re, b_im, c_re, c_im, s5_d):
    bsz, l_all, width = u_all.shape
    hg, t_n, p_n = S5_GROUP, S5_CHUNK, S5_STATE
    g_n = width // hg
    n_all = l_all // t_n
    n_ctx = n_ctx_tok // t_n
    th = t_n * hg
    e, cs, kt, a16 = _s5_params(a_re, a_im, log_dt, b_re, b_im, c_re, c_im)

    kt5 = kt.reshape(g_n, 2, hg, t_n, hg)
    ii = jnp.arange(t_n)[:, None]
    jj = jnp.arange(t_n)[None, :]

    def toeplitz(k4, lag, keep):
        m = k4[:, :, jnp.clip(lag, 0, t_n - 1), :]
        m = jnp.where(keep[None, None, :, :, None], m, 0.0)
        return jnp.transpose(m, (0, 2, 1, 3, 4)).reshape(g_n, th, th)

    m_f = toeplitz(kt5[:, 0], jj - ii, jj >= ii)
    m_b = toeplitz(kt5[:, 1], ii - jj, ii >= jj)
    d_t = jnp.tile(s5_d, (1, t_n)).reshape(g_n, 1, th)

    u_g = jnp.transpose(u_all.reshape(bsz, n_all, t_n, g_n, hg), (3, 0, 1, 2, 4)).reshape(g_n, bsz * n_all, th)
    e_cat = e
    s_in = _s5_state_inputs(u_g, e_cat)
    rows = bsz * n_all
    planes = jnp.transpose(s_in.reshape(g_n, rows, 4, p_n), (2, 1, 0, 3)).reshape(4, rows, g_n * p_n // LANES, LANES)
    a_pl = jnp.transpose(a16.reshape(g_n, 4, p_n), (1, 0, 2)).reshape(4, g_n * p_n // LANES, LANES)
    st = _s5_scan(planes, a_pl, n_ctx, n_all)
    s_g = jnp.transpose(st.reshape(4, rows, g_n, p_n), (2, 1, 0, 3)).reshape(g_n, rows, 4 * p_n)
    y_g = _s5_apply(u_g, s_g, m_f, m_b, d_t, cs, n_ctx, n_all)
    n_lat = n_all - n_ctx
    y = jnp.transpose(y_g.reshape(g_n, bsz, n_lat, t_n, hg), (1, 2, 3, 0, 4))
    return y.reshape(bsz, n_lat * t_n, width)


def _rw_prep_kernel(z_ref, zp_ref, zn_ref, mu_ref, w2_ref, a2_ref, g2_ref, w0_ref, a0_ref,
                    kk_w_ref, ka_ref, rk_ref, seg_ref, segt_ref,
                    r_ref, v_ref, kk_ref, g_ref, bonus_ref, lw_ref, kd_ref, be_ref,
                    *, tm, l_lat, rw):
    j = pl.program_id(1)
    z = z_ref[0]
    lat = j > 0
    tl = lax.broadcasted_iota(jnp.int32, (tm, 1), 0)
    tok = (j - 1) * tm + tl
    col = tl % GRID_W
    m_l = jnp.where(lat, col, tl) > 0
    m_r = jnp.where(lat, col - (GRID_W - 1), tl - (tm - 1)) < 0
    m_u = jnp.logical_and(lat, tok >= GRID_W)
    m_d = jnp.logical_and(lat, tok < l_lat - GRID_W)
    z_l = pltpu.roll(z, 1, 0)
    z_r = pltpu.roll(z, tm - 1, 0)
    z_u = jnp.concatenate([zp_ref[0], z[:tm - GRID_W]], axis=0)
    z_d = jnp.concatenate([z[GRID_W:], zn_ref[0]], axis=0)
    s = (jnp.where(m_l, z_l, 0.0) + jnp.where(m_r, z_r, 0.0)
         + jnp.where(m_u, z_u, 0.0) + jnp.where(m_d, z_d, 0.0))
    cnt = (m_l.astype(F32) + m_r.astype(F32)) + (m_u.astype(F32) + m_d.astype(F32))
    zs = z + (s / cnt - z) * mu_ref[...]

    r = zs[:, 0:rw]
    k = zs[:, rw:2 * rw]
    v = zs[:, 2 * rw:3 * rw]
    o = 3 * rw
    wd = zs[:, o:o + LANES]
    ad = zs[:, o + LANES:o + 2 * LANES]
    gd = zs[:, o + 2 * LANES:o + 3 * LANES]

    seg = seg_ref[...]
    segt = segt_ref[...]

    def head_sum(t):
        return _dot(_dot(t, seg, HIGHEST), segt, HIGHEST)

    g_ref[0] = _dot(_sigmoid(gd), g2_ref[...], HIGHEST)
    kk = k * kk_w_ref[...]
    kk = kk * lax.rsqrt(head_sum(kk * kk) + 1e-12)
    wl = w0_ref[...] + _dot(jnp.tanh(wd), w2_ref[...], HIGHEST)
    al = a0_ref[...] + _dot(ad, a2_ref[...], HIGHEST)
    r_ref[0] = r
    v_ref[0] = v
    kk_ref[0] = kk
    coef = jnp.zeros_like(r)
    for d in range(2):
        w_raw = -_softplus(-wl[:, d * rw:(d + 1) * rw]) - 0.5
        a = _sigmoid(al[:, d * rw:(d + 1) * rw])
        k_d = k * (1.0 + (a - 1.0) * ka_ref[...])
        lw_ref[d, 0] = -jnp.exp(w_raw)
        kd_ref[d, 0] = k_d
        be_ref[d, 0] = kk * a
        coef = coef + head_sum(r * k_d * rk_ref[...])
    bonus_ref[0] = coef * v


def _rw_prep(z_rw, l_ctx, mu, w2bd, a2bd, g2, w0cat, a0cat, k_k, k_a, r_k_flat, seg, segt):
    bsz, l_all, cols = z_rw.shape
    tm = l_ctx
    rw = g2.shape[1]
    l_lat = l_all - l_ctx
    nblk = l_all // tm
    hb = tm // GRID_W
    n_hblk = l_all // GRID_W
    full = lambda shape: pl.BlockSpec(shape, lambda b, j: (0,) * len(shape))
    shared = jax.ShapeDtypeStruct((bsz, l_all, rw), F32)
    per_dir = jax.ShapeDtypeStruct((2, bsz, l_all, rw), F32)
    o_shared = pl.BlockSpec((1, tm, rw), lambda b, j: (b, j, 0))
    o_dir = pl.BlockSpec((2, 1, tm, rw), lambda b, j: (0, b, j, 0))
    return pl.pallas_call(
        functools.partial(_rw_prep_kernel, tm=tm, l_lat=l_lat, rw=rw),
        out_shape=(shared,) * 5 + (per_dir,) * 3,
        grid=(bsz, nblk),
        in_specs=[pl.BlockSpec((1, tm, cols), lambda b, j: (b, j, 0)),
                  pl.BlockSpec((1, GRID_W, cols), lambda b, j: (b, jnp.maximum(j * hb - 1, 0), 0)),
                  pl.BlockSpec((1, GRID_W, cols), lambda b, j: (b, jnp.minimum((j + 1) * hb, n_hblk - 1), 0)),
                  full((1, cols)), full(w2bd.shape), full(a2bd.shape), full(g2.shape),
                  full((1, 2 * rw)), full((1, 2 * rw)), full((1, rw)), full((1, rw)), full((1, rw)),
                  full(seg.shape), full(segt.shape)],
        out_specs=(o_shared,) * 5 + (o_dir,) * 3,
        compiler_params=_cparams(("parallel", "parallel")),
        name="rwkv_prep",
    )(z_rw, z_rw, z_rw, mu.reshape(1, cols), w2bd, a2bd, g2, w0cat, a0cat,
      k_k.reshape(1, rw), k_a.reshape(1, rw), r_k_flat.reshape(1, rw), seg, segt)


def _stack_heads(x, head0):
    return jnp.concatenate([jnp.where(head0, x, 0.0), jnp.where(head0, 0.0, x)], axis=0)


def _rw_chunk_kernel(r_ref, v_ref, kk_ref, lw_ref, kd_ref, be_ref, y_ref, z_scr, *, n_pairs, prec):
    c_n = RW_CHUNK
    n2 = 2 * c_n
    rev = (pl.program_id(0) % 2) == 1

    @pl.when(pl.program_id(1) == 0)
    def _():
        z_scr[...] = jnp.zeros_like(z_scr)

    ri = lax.broadcasted_iota(jnp.int32, (c_n, c_n), 0)
    ci = lax.broadcasted_iota(jnp.int32, (c_n, c_n), 1)
    tri = (jnp.where(rev, ci - ri, ri - ci) >= 0).astype(F32)
    r2 = lax.broadcasted_iota(jnp.int32, (n2, n2), 0)
    c2 = lax.broadcasted_iota(jnp.int32, (n2, n2), 1)
    t2 = r2 % c_n
    i2 = c2 % c_n
    same_head = (r2 // c_n) == (c2 // c_n)
    before = jnp.logical_and(same_head, jnp.where(rev, i2 - t2, t2 - i2) > 0)
    upto = jnp.logical_or(before, r2 == c2)
    eye = (r2 == c2).astype(F32)
    head0 = lax.broadcasted_iota(jnp.int32, (1, LANES), 1) < RW_HEAD

    def blk(s):
        return (r2 // s) == (c2 // s)

    def pair(p, carry):
        sl = pl.ds(pl.multiple_of(p * LANES, LANES), LANES)
        lw = lw_ref[0, 0, :, sl]
        r = r_ref[0, :, sl]
        v = v_ref[0, :, sl]
        kk = kk_ref[0, :, sl]
        kd = kd_ref[0, 0, :, sl]
        be = be_ref[0, 0, :, sl]
        cum = _dot(tri, lw, HIGHEST)
        tot = jnp.sum(lw, axis=0, keepdims=True)
        g_inv = jnp.exp(-cum)
        g_hat = jnp.exp(tot - cum)
        at_s = _stack_heads(kk * jnp.exp(cum - lw), head0)
        rt_s = _stack_heads(r * jnp.exp(cum), head0)
        kt_s = _stack_heads(kd * g_inv, head0)
        bt_s = _stack_heads(be * g_inv, head0)
        kh_s = _stack_heads(kd * g_hat, head0)
        bh_s = _stack_heads(be * g_hat, head0)
        v_s = _stack_heads(v, head0)

        lhs = jnp.concatenate([at_s, rt_s], axis=0)
        gk = _dot_nt(lhs, kt_s, prec)
        gb = _dot_nt(lhs, bt_s, prec)
        a_ak = jnp.where(before, gk[:n2], 0.0)
        a_rk = jnp.where(upto, gk[n2:], 0.0)
        a_ab = jnp.where(before, gb[:n2], 0.0)
        a_rb = jnp.where(upto, gb[n2:], 0.0)

        l4 = jnp.where(blk(4), a_ab, 0.0)
        t_inv = _dot(eye - l4, eye + _dot(l4, l4, prec), prec)
        s = 4
        while s < c_n:
            l_off = jnp.where(jnp.logical_and(blk(2 * s), jnp.logical_not(blk(s))), a_ab, 0.0)
            t_inv = t_inv - _dot(_dot(t_inv, l_off, prec), t_inv, prec)
            s *= 2

        av = _dot(jnp.concatenate([a_ak, a_rk], axis=0), v_s, prec)
        wu = _dot(t_inv, jnp.concatenate([at_s, av[:n2]], axis=1), prec)
        qy = jnp.concatenate([rt_s, av[n2:]], axis=1) - _dot(a_rb, wu, prec)
        bwu = _dot(bh_s.T, wu, prec)
        kv = _dot(kh_s.T, v_s, prec)
        z0 = z_scr[p]
        y_s = _dot(qy[:, :LANES], z0, prec) + qy[:, LANES:]
        y_ref[0, 0, :, sl] = y_s[:c_n] + y_s[c_n:]
        m_z = eye * jnp.exp(tot) - bwu[:, :LANES]
        z_scr[p] = _dot(m_z, z0, prec) + (kv - bwu[:, LANES:])
        return carry

    lax.fori_loop(0, n_pairs, pair, 0)


def _rw_scan(r, v, kk, lw, kd, be, l_ctx, prec):
    bsz, l_all, rw = r.shape
    c_n = RW_CHUNK
    n_all = l_all // c_n
    n_ctx = l_ctx // c_n
    n_lat = n_all - n_ctx
    n_pairs = rw // LANES

    def chunk_of(bd, s):
        fwd = s
        bwd = jnp.where(s < n_ctx, n_ctx - 1 - s, n_all + n_ctx - 1 - s)
        return jnp.where(bd % 2 == 0, fwd, bwd)

    def out_chunk(bd, s):
        c = chunk_of(bd, s)
        edge = jnp.where(bd % 2 == 0, 0, n_lat - 1)
        return jnp.where(s < n_ctx, edge, c - n_ctx)

    shared = pl.BlockSpec((1, c_n, rw), lambda bd, s: (bd // 2, chunk_of(bd, s), 0))
    per_dir = pl.BlockSpec((1, 1, c_n, rw), lambda bd, s: (bd % 2, bd // 2, chunk_of(bd, s), 0))
    return pl.pallas_call(
        functools.partial(_rw_chunk_kernel, n_pairs=n_pairs, prec=prec),
        out_shape=jax.ShapeDtypeStruct((2, bsz, n_lat * c_n, rw), F32),
        grid=(2 * bsz, n_all),
        in_specs=[shared, shared, shared, per_dir, per_dir, per_dir],
        out_specs=pl.BlockSpec((1, 1, c_n, rw), lambda bd, s: (bd % 2, bd // 2, out_chunk(bd, s), 0)),
        scratch_shapes=[pltpu.VMEM((n_pairs, LANES, LANES), F32)],
        compiler_params=_cparams(("parallel", "arbitrary")),
        name="rwkv_scan",
    )(r, v, kk, lw, kd, be)


def _s5_glu_kernel(y_ref, wa_ref, wb_ref, o_ref, h_scr):
    @pl.when(pl.program_id(1) == 0)
    def _():
        h_scr[...] = _gelu_tanh(y_ref[...]).astype(BF16)

    h = h_scr[...]
    o_ref[...] = (_dot(h, wa_ref[...]) * _sigmoid(_dot(h, wb_ref[...]))).astype(o_ref.dtype)


def _s5_glu(y2, w, *, tm, tn):
    m, k = y2.shape
    n = w.shape[1] // 2
    nj = n // tn
    return pl.pallas_call(
        _s5_glu_kernel,
        out_shape=jax.ShapeDtypeStruct((m, n), BF16),
        grid=(m // tm, nj),
        in_specs=[pl.BlockSpec((tm, k), lambda i, j: (i, 0)),
                  pl.BlockSpec((k, tn), lambda i, j: (0, j)),
                  pl.BlockSpec((k, tn), lambda i, j: (0, j + nj))],
        out_specs=pl.BlockSpec((tm, tn), lambda i, j: (i, j)),
        scratch_shapes=[pltpu.VMEM((tm, k), BF16)],
        compiler_params=_cparams(("parallel", "arbitrary")),
        name="s5_glu",
    )(y2, w, w)


def _rw_merge_kernel(yf_ref, yb_ref, bonus_ref, g_ref, lnw_ref, lnb_ref, seg_ref, segt_ref,
                     ga_ref, gb_ref, s5_ref, w_ref, o_ref, h_scr):
    @pl.when(pl.program_id(1) == 0)
    def _():
        seg = seg_ref[...]
        segt = segt_ref[...]
        inv_n = 1.0 / RW_HEAD

        def head_mean(t):
            return _dot(_dot(t, seg, HIGHEST), segt, HIGHEST) * inv_n

        y = yf_ref[0] + yb_ref[0]
        dy = y - head_mean(y)
        var = head_mean(dy * dy)
        y = dy * lax.rsqrt(var + GN_EPS) * lnw_ref[...] + lnb_ref[...] + bonus_ref[...]
        h_scr[...] = (y * g_ref[...]).astype(BF16)

    rw_out = _dot(h_scr[...], w_ref[...])
    merged = ga_ref[...].astype(F32) * s5_ref[...].astype(F32) + gb_ref[...].astype(F32) * rw_out
    o_ref[...] = merged.astype(o_ref.dtype)


def _rw_merge(y_dirs, bonus, g, ln_w, ln_b, seg, segt, gates, s5_out, w_proj, row_off, *, tm, tn):
    _, m, rw = y_dirs.shape
    n = w_proj.shape[1]
    nj = n // tn
    full = lambda shape: pl.BlockSpec(shape, lambda i, j: (0,) * len(shape))
    return pl.pallas_call(
        _rw_merge_kernel,
        out_shape=jax.ShapeDtypeStruct((m, n), BF16),
        grid=(m // tm, nj),
        in_specs=[pl.BlockSpec((1, tm, rw), lambda i, j: (0, i, 0)),
                  pl.BlockSpec((1, tm, rw), lambda i, j: (1, i, 0)),
                  pl.BlockSpec((tm, rw), lambda i, j: (row_off(i), 0)),
                  pl.BlockSpec((tm, rw), lambda i, j: (row_off(i), 0)),
                  full((1, rw)), full((1, rw)), full(seg.shape), full(segt.shape),
                  pl.BlockSpec((tm, tn), lambda i, j: (i, j)),
                  pl.BlockSpec((tm, tn), lambda i, j: (i, j + nj)),
                  pl.BlockSpec((tm, tn), lambda i, j: (i, j)),
                  pl.BlockSpec((rw, tn), lambda i, j: (0, j))],
        out_specs=pl.BlockSpec((tm, tn), lambda i, j: (i, j)),
        scratch_shapes=[pltpu.VMEM((tm, rw), BF16)],
        compiler_params=_cparams(("parallel", "arbitrary")),
        name="rwkv_merge",
    )(y_dirs, y_dirs, bonus, g, ln_w.reshape(1, rw), ln_b.reshape(1, rw), seg, segt,
      gates, gates, s5_out, w_proj)


def _resid_mm_kernel(a_ref, w_ref, x_ref, g_ref, o_ref):
    o_ref[...] = x_ref[...] + g_ref[0] * _dot(a_ref[...], w_ref[...])


def _resid_matmul(a, w, x2, g_tab, mod_row_of_block, *, tm, tn):
    m, k = a.shape
    n = w.shape[1]
    return pl.pallas_call(
        _resid_mm_kernel,
        out_shape=jax.ShapeDtypeStruct((m, n), F32),
        grid=(m // tm, n // tn),
        in_specs=[pl.BlockSpec((tm, k), lambda i, j: (i, 0)),
                  pl.BlockSpec((k, tn), lambda i, j: (0, j)),
                  pl.BlockSpec((tm, tn), lambda i, j: (i, j)),
                  pl.BlockSpec((1, 1, tn), lambda i, j: (mod_row_of_block(i), 0, j))],
        out_specs=pl.BlockSpec((tm, tn), lambda i, j: (i, j)),
        compiler_params=_cparams(("parallel", "arbitrary")),
        name="out_proj",
    )(a, w, x2, g_tab)


def _ffn_down_kernel(a_ref, w_ref, x_ref, g_ref, nf_ref, o_ref, acc_ref):
    kk = pl.program_id(1)

    @pl.when(kk == 0)
    def _():
        acc_ref[...] = jnp.zeros_like(acc_ref)

    acc_ref[...] += _dot(a_ref[...], w_ref[...])

    @pl.when(kk == pl.num_programs(1) - 1)
    def _():
        h = x_ref[...] + g_ref[0] * acc_ref[...]
        ms = jnp.mean(h * h, axis=-1, keepdims=True)
        o_ref[...] = h * lax.rsqrt(ms + NORM_EPS) * nf_ref[...]


def _ffn_down(a, w, x2, g_tab, mod_row_of_block, norm_f, *, tm, tk):
    m, k = a.shape
    n = w.shape[1]
    return pl.pallas_call(
        _ffn_down_kernel,
        out_shape=jax.ShapeDtypeStruct((m, n), F32),
        grid=(m // tm, k // tk),
        in_specs=[pl.BlockSpec((tm, tk), lambda i, kk: (i, kk)),
                  pl.BlockSpec((tk, n), lambda i, kk: (kk, 0)),
                  pl.BlockSpec((tm, n), lambda i, kk: (i, 0)),
                  pl.BlockSpec((1, 1, n), lambda i, kk: (mod_row_of_block(i), 0, 0)),
                  pl.BlockSpec((1, n), lambda i, kk: (0, 0))],
        out_specs=pl.BlockSpec((tm, n), lambda i, kk: (i, 0)),
        scratch_shapes=[pltpu.VMEM((tm, n), F32)],
        compiler_params=_cparams(("parallel", "arbitrary")),
        name="ffn_down",
    )(a, w, x2, g_tab, norm_f.reshape(1, n))


RW_PREC = HIGHEST


def kernel(x, c, ctx, c_ctx, ada_w, ada_b, norm1_w, w_in, rw_mu, s5_a_re, s5_a_im, s5_log_dt, s5_b_re, s5_b_im, s5_c_re, s5_c_im, s5_d, s5_glu_w, rw_w0, rw_w2, rw_a0, rw_a2, rw_g2, rw_k_k, rw_k_a, rw_r_k, rw_ln_w, rw_ln_b, rw_proj, w_o, norm2_w, ffn_w13, ffn_w2, norm_f):
    assert ada_w.shape[0] == 1, "single-layer block"
    bsz, l_lat, d = x.shape
    l_ctx = ctx.shape[1]
    l_all = l_ctx + l_lat
    s5w = s5_d.shape[1] * s5_d.shape[2]
    rw = rw_g2.shape[2]
    shift_cols = rw_mu.shape[1]
    d_ff = ffn_w2.shape[1]
    n_heads = rw // RW_HEAD

    c_rows = jnp.concatenate([c, c_ctx[None], jnp.zeros((8 - bsz - 1, d), F32)], axis=0)
    mod = _modulation(c_rows, ada_w[0], ada_b[0])
    tab = lambda k: mod[:, k * d:(k + 1) * d].reshape(8, 1, d)
    sh1, sc1, g1, sh2, sc2, g2 = (tab(k) for k in range(N_MOD))
    ctx_row = bsz

    tm_in = l_ctx
    blocks_per_batch = l_all // tm_in
    x_all = jnp.concatenate([ctx, x], axis=1).reshape(bsz * l_all, d)
    w_in_b = w_in[0].astype(BF16)
    n_mix = s5w + shift_cols

    def mix_mod_row(i):
        return jnp.where(i % blocks_per_batch == 0, ctx_row, i // blocks_per_batch)

    u_all = _lnmod_matmul(x_all, norm1_w[0], sh1, sc1, mix_mod_row, w_in_b[:, :s5w],
                          tm=tm_in, tn=s5w, out_dtype=F32, name="in_proj_s5").reshape(bsz, l_all, s5w)
    z_rw = _lnmod_matmul(x_all, norm1_w[0], sh1, sc1, mix_mod_row, w_in_b[:, s5w:n_mix],
                         tm=tm_in, tn=shift_cols // 3, out_dtype=F32, name="in_proj_rw").reshape(bsz, l_all, shift_cols)
    x2 = x.reshape(bsz * l_lat, d)
    tm = 512
    lat_mod_row = lambda i: i // (l_lat // tm)
    gates = _lnmod_matmul(x2, norm1_w[0], sh1, sc1, lat_mod_row, w_in_b[:, n_mix:],
                          tm=tm, tn=1024, out_dtype=BF16, epilogue="sigmoid", name="in_proj_gates")

    y_s5 = _s5_branch(u_all, l_ctx, s5_a_re[0], s5_a_im[0], s5_log_dt[0], s5_b_re[0], s5_b_im[0],
                      s5_c_re[0], s5_c_im[0], s5_d[0])
    s5_out = _s5_glu(y_s5.reshape(bsz * l_lat, s5w), s5_glu_w[0].astype(BF16), tm=tm, tn=1024)

    lora = rw_w2.shape[2]
    zl = jnp.zeros((lora, rw), F32)
    w2bd = jnp.concatenate([jnp.concatenate([rw_w2[0, 0], zl], axis=1),
                            jnp.concatenate([zl, rw_w2[0, 1]], axis=1)], axis=0)
    a2bd = jnp.concatenate([jnp.concatenate([rw_a2[0, 0], zl], axis=1),
                            jnp.concatenate([zl, rw_a2[0, 1]], axis=1)], axis=0)
    head_of = jnp.arange(rw) // RW_HEAD
    seg = (head_of[:, None] == jnp.arange(LANES)[None, :]).astype(F32)
    segt = seg.T
    r, v, kk, g, bonus, lw, kd, be = _rw_prep(
        z_rw, l_ctx, rw_mu[0], w2bd, a2bd, rw_g2[0], rw_w0[0].reshape(1, 2 * rw), rw_a0[0].reshape(1, 2 * rw),
        rw_k_k[0], rw_k_a[0], rw_r_k[0].reshape(rw), seg, segt)
    y_dirs = _rw_scan(r, v, kk, lw, kd, be, l_ctx, RW_PREC)

    tm_m = l_ctx
    per_b = l_lat // tm_m
    row_off = lambda i: (i // per_b) * (per_b + 1) + 1 + i % per_b
    merged = _rw_merge(y_dirs.reshape(2, bsz * l_lat, rw), bonus.reshape(bsz * l_all, rw),
                       g.reshape(bsz * l_all, rw), rw_ln_w[0], rw_ln_b[0], seg, segt,
                       gates, s5_out, rw_proj[0].astype(BF16), row_off, tm=tm_m, tn=1024)
    h1 = _resid_matmul(merged, w_o[0].astype(BF16), x2, g1, lat_mod_row, tm=tm, tn=1024)

    act = _lnmod_swiglu(h1, norm2_w[0], sh2, sc2, lat_mod_row, ffn_w13[0].astype(BF16), d_ff,
                        tm=tm, tn=512, name="ffn_up")
    out = _ffn_down(act, ffn_w2[0].astype(BF16), h1, g2, lat_mod_row, norm_f, tm=tm, tk=512)
    return out.reshape(bsz, l_lat, d)
```

```python
import functools
import math

import jax
import jax.numpy as jnp
from jax import lax
from jax.experimental import pallas as pl
from jax.experimental.pallas import tpu as pltpu

F32 = jnp.float32
BF16 = jnp.bfloat16
HIGHEST = lax.Precision.HIGHEST

D_MODEL = 2048
N_MOD = 6
NORM_EPS = 1e-6
GN_EPS = 64e-5
GRID_W = 64
S5_GROUP = 16
S5_STATE = 64
S5_CHUNK = 16
RW_HEAD = 64
RW_CHUNK = 64
LANES = 128
VMEM_LIMIT = 48 * 1024 * 1024


def _cparams(sem):
    return pltpu.CompilerParams(dimension_semantics=sem, vmem_limit_bytes=VMEM_LIMIT)


def _operands(a, b, precision):
    if precision == "bf16":
        return a.astype(BF16), b.astype(BF16), None
    return a, b, precision


def _dot(a, b, precision=None):
    a, b, precision = _operands(a, b, precision)
    return jnp.dot(a, b, preferred_element_type=F32, precision=precision)


def _dot_nt(a, b, precision=None):
    a, b, precision = _operands(a, b, precision)
    return lax.dot_general(a, b, (((1,), (1,)), ((), ())), preferred_element_type=F32, precision=precision)


def _sigmoid(x):
    return 1.0 / (1.0 + jnp.exp(-x))


def _silu(x):
    return x * _sigmoid(x)


def _gelu_tanh(x):
    c = math.sqrt(2.0 / math.pi)
    return 0.5 * x * (1.0 + jnp.tanh(c * (x + 0.044715 * (x * x * x))))


def _softplus(x):
    return jnp.maximum(x, 0.0) + jnp.log(1.0 + jnp.exp(-jnp.abs(x)))


def _mod_kernel(c_ref, w_ref, b_ref, o_ref):
    o_ref[...] = _dot(_silu(c_ref[...]), w_ref[...], HIGHEST) + b_ref[...]


def _modulation(c_rows, ada_w, ada_b):
    m, d = c_rows.shape
    n = ada_w.shape[1]
    tn = 1024
    return pl.pallas_call(
        _mod_kernel,
        out_shape=jax.ShapeDtypeStruct((m, n), F32),
        grid=(n // tn,),
        in_specs=[pl.BlockSpec((m, d), lambda j: (0, 0)),
                  pl.BlockSpec((d, tn), lambda j: (0, j)),
                  pl.BlockSpec((1, tn), lambda j: (0, j))],
        out_specs=pl.BlockSpec((m, tn), lambda j: (0, j)),
        compiler_params=_cparams(("arbitrary",)),
        name="modulation",
    )(c_rows, ada_w, ada_b.reshape(1, n))


def _lnmod_rows(x, nw, sh, sc):
    ms = jnp.mean(x * x, axis=-1, keepdims=True)
    y = x * lax.rsqrt(ms + NORM_EPS) * nw
    return y * (1.0 + sc) + sh


def _lnmod_mm_kernel(x_ref, nw_ref, sh_ref, sc_ref, w_ref, o_ref, h_scr, *, epilogue):
    @pl.when(pl.program_id(1) == 0)
    def _():
        h_scr[...] = _lnmod_rows(x_ref[...], nw_ref[...], sh_ref[0], sc_ref[0]).astype(BF16)

    z = _dot(h_scr[...], w_ref[...])
    if epilogue == "sigmoid":
        z = _sigmoid(z)
    o_ref[...] = z.astype(o_ref.dtype)


def _lnmod_swiglu_kernel(x_ref, nw_ref, sh_ref, sc_ref, w1_ref, w3_ref, o_ref, h_scr):
    @pl.when(pl.program_id(1) == 0)
    def _():
        h_scr[...] = _lnmod_rows(x_ref[...], nw_ref[...], sh_ref[0], sc_ref[0]).astype(BF16)

    h = h_scr[...]
    o_ref[...] = (_silu(_dot(h, w1_ref[...])) * _dot(h, w3_ref[...])).astype(o_ref.dtype)


def _lnmod_matmul(x2, nw, sh_tab, sc_tab, mod_row_of_block, w, *, tm, tn, out_dtype, epilogue=None, name):
    m, d = x2.shape
    n = w.shape[1]
    mod_map = lambda i, j: (mod_row_of_block(i), 0, 0)
    return pl.pallas_call(
        functools.partial(_lnmod_mm_kernel, epilogue=epilogue),
        out_shape=jax.ShapeDtypeStruct((m, n), out_dtype),
        grid=(m // tm, n // tn),
        in_specs=[pl.BlockSpec((tm, d), lambda i, j: (i, 0)),
                  pl.BlockSpec((1, d), lambda i, j: (0, 0)),
                  pl.BlockSpec((1, 1, d), mod_map),
                  pl.BlockSpec((1, 1, d), mod_map),
                  pl.BlockSpec((d, tn), lambda i, j: (0, j))],
        out_specs=pl.BlockSpec((tm, tn), lambda i, j: (i, j)),
        scratch_shapes=[pltpu.VMEM((tm, d), BF16)],
        compiler_params=_cparams(("parallel", "arbitrary")),
        name=name,
    )(x2, nw.reshape(1, d), sh_tab, sc_tab, w)


def _lnmod_swiglu(x2, nw, sh_tab, sc_tab, mod_row_of_block, w13, d_ff, *, tm, tn, name):
    m, d = x2.shape
    nj = d_ff // tn
    mod_map = lambda i, j: (mod_row_of_block(i), 0, 0)
    return pl.pallas_call(
        _lnmod_swiglu_kernel,
        out_shape=jax.ShapeDtypeStruct((m, d_ff), BF16),
        grid=(m // tm, nj),
        in_specs=[pl.BlockSpec((tm, d), lambda i, j: (i, 0)),
                  pl.BlockSpec((1, d), lambda i, j: (0, 0)),
                  pl.BlockSpec((1, 1, d), mod_map),
                  pl.BlockSpec((1, 1, d), mod_map),
                  pl.BlockSpec((d, tn), lambda i, j: (0, j)),
                  pl.BlockSpec((d, tn), lambda i, j: (0, j + nj))],
        out_specs=pl.BlockSpec((tm, tn), lambda i, j: (i, j)),
        scratch_shapes=[pltpu.VMEM((tm, d), BF16)],
        compiler_params=_cparams(("parallel", "arbitrary")),
        name=name,
    )(x2, nw.reshape(1, d), sh_tab, sc_tab, w13, w13)


def _s5_param_kernel(are_ref, aim_ref, ldt_ref, bre_ref, bim_ref, cre_ref, cim_ref,
                     e_ref, cs_ref, kt_ref, a16_ref):
    t_n, hg, p_n = S5_CHUNK, S5_GROUP, S5_STATE
    tau = lax.broadcasted_iota(jnp.int32, (t_n, 1, p_n), 0).astype(F32)
    for d in range(2):
        a_re = are_ref[0, d:d + 1, :]
        a_im = aim_ref[0, d:d + 1, :]
        dt = jnp.exp(ldt_ref[0, d:d + 1, :])
        lam = a_re * dt
        th = a_im * dt
        er = jnp.exp(lam)
        ab_re = er * jnp.cos(th)
        ab_im = er * jnp.sin(th)
        den = a_re * a_re + a_im * a_im
        x_re = ab_re - 1.0
        co_re = (x_re * a_re + ab_im * a_im) / den
        co_im = (ab_im * a_re - x_re * a_im) / den
        bt_re = bre_ref[0, d]
        bt_im = bim_ref[0, d]
        bb_re = co_re * bt_re - co_im * bt_im
        bb_im = co_re * bt_im + co_im * bt_re
        c_re = cre_ref[0, d]
        c_im = cim_ref[0, d]

        def power(tv):
            mag = jnp.exp(tv * lam)
            return mag * jnp.cos(tv * th), mag * jnp.sin(tv * th)

        pw_re, pw_im = power(tau)
        cp_re = (c_re[None] * pw_re - c_im[None] * pw_im).reshape(t_n * hg, p_n)
        cp_im = (c_re[None] * pw_im + c_im[None] * pw_re).reshape(t_n * hg, p_n)
        kt_ref[0, d] = _dot_nt(bb_re, cp_re, HIGHEST) - _dot_nt(bb_im, cp_im, HIGHEST)

        te = (t_n - 1.0 - tau) if d == 0 else tau
        pe_re, pe_im = power(te)
        e_ref[0, d, :, 0:p_n] = (pe_re * bb_re[None] - pe_im * bb_im[None]).reshape(t_n * hg, p_n)
        e_ref[0, d, :, p_n:2 * p_n] = (pe_re * bb_im[None] + pe_im * bb_re[None]).reshape(t_n * hg, p_n)

        tc = (tau + 1.0) if d == 0 else (t_n - tau)
        pc_re, pc_im = power(tc)
        cs_ref[0, d, :, 0:p_n] = (c_re[None] * pc_re - c_im[None] * pc_im).reshape(t_n * hg, p_n)
        cs_ref[0, d, :, p_n:2 * p_n] = -(c_re[None] * pc_im + c_im[None] * pc_re).reshape(t_n * hg, p_n)

        mag16 = jnp.exp(float(t_n) * lam)
        a16_ref[0, d, 0:1, :] = mag16 * jnp.cos(float(t_n) * th)
        a16_ref[0, d, 1:2, :] = mag16 * jnp.sin(float(t_n) * th)


def _s5_params(a_re, a_im, log_dt, b_re, b_im, c_re, c_im):
    g_n = a_re.shape[1]
    p_n, hg, t_n = S5_STATE, S5_GROUP, S5_CHUNK
    tr = lambda a: jnp.swapaxes(a, 0, 1)
    ldt = jnp.broadcast_to(tr(log_dt)[:, :, None], (g_n, 2, p_n))
    spec3 = pl.BlockSpec((1, 2, p_n), lambda g: (g, 0, 0))
    spec4 = pl.BlockSpec((1, 2, hg, p_n), lambda g: (g, 0, 0, 0))
    th = t_n * hg
    return pl.pallas_call(
        _s5_param_kernel,
        out_shape=(jax.ShapeDtypeStruct((g_n, 2, th, 2 * p_n), F32),
                   jax.ShapeDtypeStruct((g_n, 2, th, 2 * p_n), F32),
                   jax.ShapeDtypeStruct((g_n, 2, hg, th), F32),
                   jax.ShapeDtypeStruct((g_n, 2, 2, p_n), F32)),
        grid=(g_n,),
        in_specs=[spec3, spec3, spec3, spec4, spec4, spec4, spec4],
        out_specs=(pl.BlockSpec((1, 2, th, 2 * p_n), lambda g: (g, 0, 0, 0)),
                   pl.BlockSpec((1, 2, th, 2 * p_n), lambda g: (g, 0, 0, 0)),
                   pl.BlockSpec((1, 2, hg, th), lambda g: (g, 0, 0, 0)),
                   pl.BlockSpec((1, 2, 2, p_n), lambda g: (g, 0, 0, 0))),
        compiler_params=_cparams(("parallel",)),
        name="s5_params",
    )(tr(a_re), tr(a_im), ldt,
      jnp.transpose(b_re, (1, 0, 3, 2)), jnp.transpose(b_im, (1, 0, 3, 2)), tr(c_re), tr(c_im))


def _s5_state_in_kernel(u_ref, e_ref, o_ref):
    e = e_ref[0]
    u = u_ref[0]
    o_ref[0, :, 0:LANES] = _dot(u, e[0], HIGHEST)
    o_ref[0, :, LANES:2 * LANES] = _dot(u, e[1], HIGHEST)


def _s5_state_inputs(u_g, e):
    g_n, rows, th = u_g.shape
    return pl.pallas_call(
        _s5_state_in_kernel,
        out_shape=jax.ShapeDtypeStruct((g_n, rows, 2 * LANES), F32),
        grid=(g_n,),
        in_specs=[pl.BlockSpec((1, rows, th), lambda g: (g, 0, 0)),
                  pl.BlockSpec((1, 2, th, LANES), lambda g: (g, 0, 0, 0))],
        out_specs=pl.BlockSpec((1, rows, 2 * LANES), lambda g: (g, 0, 0)),
        compiler_params=_cparams(("parallel",)),
        name="s5_state_inputs",
    )(u_g, e)


def _s5_scan_kernel(e_ref, a_ref, o_ref, *, n_ctx, n_all):
    zero = jnp.zeros(e_ref.shape[2:], F32)

    def step(pr, pi):
        ar, ai = a_ref[pr], a_ref[pi]

        def body(c, carry):
            sr, si = carry
            o_ref[pr, c] = sr
            o_ref[pi, c] = si
            return (ar * sr - ai * si + e_ref[pr, c], ar * si + ai * sr + e_ref[pi, c])
        return body

    lax.fori_loop(0, n_all, step(0, 1), (zero, zero))
    bwd = step(2, 3)
    carry = lax.fori_loop(0, n_ctx, lambda k, cy: bwd(n_ctx - 1 - k, cy), (zero, zero))
    lax.fori_loop(0, n_all - n_ctx, lambda k, cy: bwd(n_all - 1 - k, cy), carry)


def _s5_scan(e_planes, a_planes, n_ctx, n_all):
    _, rows, r_n, _ = e_planes.shape
    bsz = rows // n_all
    sub = 8
    return pl.pallas_call(
        functools.partial(_s5_scan_kernel, n_ctx=n_ctx, n_all=n_all),
        out_shape=jax.ShapeDtypeStruct(e_planes.shape, F32),
        grid=(bsz, r_n // sub),
        in_specs=[pl.BlockSpec((4, n_all, sub, LANES), lambda b, q: (0, b, q, 0)),
                  pl.BlockSpec((4, sub, LANES), lambda b, q: (0, q, 0))],
        out_specs=pl.BlockSpec((4, n_all, sub, LANES), lambda b, q: (0, b, q, 0)),
        compiler_params=_cparams(("parallel", "parallel")),
        name="s5_scan",
    )(e_planes, a_planes)


def _s5_apply_kernel(u_ref, s_ref, mf_ref, mb_ref, d_ref, cs_ref, o_ref, *, n_ctx):
    u = u_ref[0, n_ctx:, :]
    s = s_ref[0, n_ctx:, :]
    y = _dot(u, mf_ref[0] + mb_ref[0], HIGHEST) + u * d_ref[0]
    y = y + _dot_nt(s[:, 0:LANES], cs_ref[0, 0], HIGHEST) + _dot_nt(s[:, LANES:2 * LANES], cs_ref[0, 1], HIGHEST)
    o_ref[0, 0] = y


def _s5_apply(u_g, s_g, m_f, m_b, d_t, cs, n_ctx, n_all):
    g_n, rows, th = u_g.shape
    bsz = rows // n_all
    n_lat = n_all - n_ctx
    return pl.pallas_call(
        functools.partial(_s5_apply_kernel, n_ctx=n_ctx),
        out_shape=jax.ShapeDtypeStruct((g_n, bsz, n_lat, th), F32),
        grid=(g_n, bsz),
        in_specs=[pl.BlockSpec((1, n_all, th), lambda g, b: (g, b, 0)),
                  pl.BlockSpec((1, n_all, 2 * LANES), lambda g, b: (g, b, 0)),
                  pl.BlockSpec((1, th, th), lambda g, b: (g, 0, 0)),
                  pl.BlockSpec((1, th, th), lambda g, b: (g, 0, 0)),
                  pl.BlockSpec((1, 1, th), lambda g, b: (g, 0, 0)),
                  pl.BlockSpec((1, 2, th, LANES), lambda g, b: (g, 0, 0, 0))],
        out_specs=pl.BlockSpec((1, 1, n_lat, th), lambda g, b: (g, b, 0, 0)),
        compiler_params=_cparams(("parallel", "arbitrary")),
        name="s5_apply",
    )(u_g, s_g, m_f, m_b, d_t, cs)


def _s5_branch(u_all, n_ctx_tok, a_re, a_im, log_dt, b_re, b_im, c_re, c_im, s5_d):
    bsz, l_all, width = u_all.shape
    hg, t_n, p_n = S5_GROUP, S5_CHUNK, S5_STATE
    g_n = width // hg
    n_all = l_all // t_n
    n_ctx = n_ctx_tok // t_n
    th = t_n * hg
    e, cs, kt, a16 = _s5_params(a_re, a_im, log_dt, b_re, b_im, c_re, c_im)

    kt5 = kt.reshape(g_n, 2, hg, t_n, hg)
    ii = jnp.arange(t_n)[:, None]
    jj = jnp.arange(t_n)[None, :]

    def toeplitz(k4, lag, keep):
        m = k4[:, :, jnp.clip(lag, 0, t_n - 1), :]
        m = jnp.where(keep[None, None, :, :, None], m, 0.0)
        return jnp.transpose(m, (0, 2, 1, 3, 4)).reshape(g_n, th, th)

    m_f = toeplitz(kt5[:, 0], jj - ii, jj >= ii)
    m_b = toeplitz(kt5[:, 1], ii - jj, ii >= jj)
    d_t = jnp.tile(s5_d, (1, t_n)).reshape(g_n, 1, th)

    u_g = jnp.transpose(u_all.reshape(bsz, n_all, t_n, g_n, hg), (3, 0, 1, 2, 4)).reshape(g_n, bsz * n_all, th)
    e_cat = e
    s_in = _s5_state_inputs(u_g, e_cat)
    rows = bsz * n_all
    planes = jnp.transpose(s_in.reshape(g_n, rows, 4, p_n), (2, 1, 0, 3)).reshape(4, rows, g_n * p_n // LANES, LANES)
    a_pl = jnp.transpose(a16.reshape(g_n, 4, p_n), (1, 0, 2)).reshape(4, g_n * p_n // LANES, LANES)
    st = _s5_scan(planes, a_pl, n_ctx, n_all)
    s_g = jnp.transpose(st.reshape(4, rows, g_n, p_n), (2, 1, 0, 3)).reshape(g_n, rows, 4 * p_n)
    y_g = _s5_apply(u_g, s_g, m_f, m_b, d_t, cs, n_ctx, n_all)
    n_lat = n_all - n_ctx
    y = jnp.transpose(y_g.reshape(g_n, bsz, n_lat, t_n, hg), (1, 2, 3, 0, 4))
    return y.reshape(bsz, n_lat * t_n, width)


def _rw_prep_kernel(z_ref, zp_ref, zn_ref, mu_ref, w2_ref, a2_ref, g2_ref, w0_ref, a0_ref,
                    kk_w_ref, ka_ref, rk_ref, seg_ref, segt_ref,
                    r_ref, v_ref, kk_ref, g_ref, bonus_ref, lw_ref, kd_ref, be_ref,
                    *, tm, l_lat, rw):
    j = pl.program_id(1)
    z = z_ref[0]
    lat = j > 0
    tl = lax.broadcasted_iota(jnp.int32, (tm, 1), 0)
    tok = (j - 1) * tm + tl
    col = tl % GRID_W
    m_l = jnp.where(lat, col, tl) > 0
    m_r = jnp.where(lat, col - (GRID_W - 1), tl - (tm - 1)) < 0
    m_u = jnp.logical_and(lat, tok >= GRID_W)
    m_d = jnp.logical_and(lat, tok < l_lat - GRID_W)
    z_l = pltpu.roll(z, 1, 0)
    z_r = pltpu.roll(z, tm - 1, 0)
    z_u = jnp.concatenate([zp_ref[0], z[:tm - GRID_W]], axis=0)
    z_d = jnp.concatenate([z[GRID_W:], zn_ref[0]], axis=0)
    s = (jnp.where(m_l, z_l, 0.0) + jnp.where(m_r, z_r, 0.0)
         + jnp.where(m_u, z_u, 0.0) + jnp.where(m_d, z_d, 0.0))
    cnt = (m_l.astype(F32) + m_r.astype(F32)) + (m_u.astype(F32) + m_d.astype(F32))
    zs = z + (s / cnt - z) * mu_ref[...]

    r = zs[:, 0:rw]
    k = zs[:, rw:2 * rw]
    v = zs[:, 2 * rw:3 * rw]
    o = 3 * rw
    wd = zs[:, o:o + LANES]
    ad = zs[:, o + LANES:o + 2 * LANES]
    gd = zs[:, o + 2 * LANES:o + 3 * LANES]

    seg = seg_ref[...]
    segt = segt_ref[...]

    def head_sum(t):
        return _dot(_dot(t, seg, HIGHEST), segt, HIGHEST)

    g_ref[0] = _dot(_sigmoid(gd), g2_ref[...], HIGHEST)
    kk = k * kk_w_ref[...]
    kk = kk * lax.rsqrt(head_sum(kk * kk) + 1e-12)
    wl = w0_ref[...] + _dot(jnp.tanh(wd), w2_ref[...], HIGHEST)
    al = a0_ref[...] + _dot(ad, a2_ref[...], HIGHEST)
    r_ref[0] = r
    v_ref[0] = v
    kk_ref[0] = kk
    coef = jnp.zeros_like(r)
    for d in range(2):
        w_raw = -_softplus(-wl[:, d * rw:(d + 1) * rw]) - 0.5
        a = _sigmoid(al[:, d * rw:(d + 1) * rw])
        k_d = k * (1.0 + (a - 1.0) * ka_ref[...])
        lw_ref[d, 0] = -jnp.exp(w_raw)
        kd_ref[d, 0] = k_d
        be_ref[d, 0] = kk * a
        coef = coef + head_sum(r * k_d * rk_ref[...])
    bonus_ref[0] = coef * v


def _rw_prep(z_rw, l_ctx, mu, w2bd, a2bd, g2, w0cat, a0cat, k_k, k_a, r_k_flat, seg, segt):
    bsz, l_all, cols = z_rw.shape
    tm = l_ctx
    rw = g2.shape[1]
    l_lat = l_all - l_ctx
    nblk = l_all // tm
    hb = tm // GRID_W
    n_hblk = l_all // GRID_W
    full = lambda shape: pl.BlockSpec(shape, lambda b, j: (0,) * len(shape))
    shared = jax.ShapeDtypeStruct((bsz, l_all, rw), F32)
    per_dir = jax.ShapeDtypeStruct((2, bsz, l_all, rw), F32)
    o_shared = pl.BlockSpec((1, tm, rw), lambda b, j: (b, j, 0))
    o_dir = pl.BlockSpec((2, 1, tm, rw), lambda b, j: (0, b, j, 0))
    return pl.pallas_call(
        functools.partial(_rw_prep_kernel, tm=tm, l_lat=l_lat, rw=rw),
        out_shape=(shared,) * 5 + (per_dir,) * 3,
        grid=(bsz, nblk),
        in_specs=[pl.BlockSpec((1, tm, cols), lambda b, j: (b, j, 0)),
                  pl.BlockSpec((1, GRID_W, cols), lambda b, j: (b, jnp.maximum(j * hb - 1, 0), 0)),
                  pl.BlockSpec((1, GRID_W, cols), lambda b, j: (b, jnp.minimum((j + 1) * hb, n_hblk - 1), 0)),
                  full((1, cols)), full(w2bd.shape), full(a2bd.shape), full(g2.shape),
                  full((1, 2 * rw)), full((1, 2 * rw)), full((1, rw)), full((1, rw)), full((1, rw)),
                  full(seg.shape), full(segt.shape)],
        out_specs=(o_shared,) * 5 + (o_dir,) * 3,
        compiler_params=_cparams(("parallel", "parallel")),
        name="rwkv_prep",
    )(z_rw, z_rw, z_rw, mu.reshape(1, cols), w2bd, a2bd, g2, w0cat, a0cat,
      k_k.reshape(1, rw), k_a.reshape(1, rw), r_k_flat.reshape(1, rw), seg, segt)


def _stack_heads(x, head0):
    return jnp.concatenate([jnp.where(head0, x, 0.0), jnp.where(head0, 0.0, x)], axis=0)


def _rw_chunk_kernel(r_ref, v_ref, kk_ref, lw_ref, kd_ref, be_ref, y_ref,
                     z_scr, lhs_s, kb_s, kht_s, bht_s, vs_s, rt_s, aab_s, akr_s, arb_s, t_s, x_s, xin_s, yc_s, wu_s,
                     *, n_pairs):
    c_n = RW_CHUNK
    n2 = 2 * c_n
    rev = (pl.program_id(0) % 2) == 1

    @pl.when(pl.program_id(1) == 0)
    def _():
        z_scr[...] = jnp.zeros_like(z_scr)

    ri = lax.broadcasted_iota(jnp.int32, (c_n, c_n), 0)
    ci = lax.broadcasted_iota(jnp.int32, (c_n, c_n), 1)
    tri = (jnp.where(rev, ci - ri, ri - ci) >= 0).astype(F32)
    r2 = lax.broadcasted_iota(jnp.int32, (n2, n2), 0)
    c2 = lax.broadcasted_iota(jnp.int32, (n2, n2), 1)
    t2 = r2 % c_n
    i2 = c2 % c_n
    same_head = (r2 // c_n) == (c2 // c_n)
    before = jnp.logical_and(same_head, jnp.where(rev, i2 - t2, t2 - i2) > 0)
    upto = jnp.logical_or(before, r2 == c2)
    eye = (r2 == c2).astype(F32)
    head0 = lax.broadcasted_iota(jnp.int32, (1, LANES), 1) < RW_HEAD

    def blk(s):
        return (r2 // s) == (c2 // s)

    pairs = range(n_pairs)
    lanes_of = lambda p: slice(p * LANES, (p + 1) * LANES)

    lw = lw_ref[0, 0]
    cum = _dot(tri, lw, HIGHEST)
    tot = jnp.sum(lw, axis=0, keepdims=True)
    g_inv = jnp.exp(-cum)
    g_hat = jnp.exp(tot - cum)
    kd = kd_ref[0, 0]
    be = be_ref[0, 0]
    at = kk_ref[0] * jnp.exp(cum - lw)
    rt = r_ref[0] * jnp.exp(cum)
    kt = kd * g_inv
    bt = be * g_inv
    kh = kd * g_hat
    bh = be * g_hat
    g_c = jnp.exp(tot)
    v = v_ref[0]
    for p in pairs:
        st = lambda x: _stack_heads(x[:, lanes_of(p)], head0)
        rt_p = st(rt)
        at_p = st(at).astype(BF16)
        lhs_s[p, :n2] = at_p
        lhs_s[p, n2:] = rt_p.astype(BF16)
        xin_s[p, :, :LANES] = at_p
        rt_s[p] = rt_p
        kb_s[p, :n2] = st(kt).astype(BF16)
        kb_s[p, n2:] = st(bt).astype(BF16)
        kht_s[p] = st(kh).T.astype(BF16)
        bht_s[p] = st(bh).T.astype(BF16)
        vs_s[p] = st(v).astype(BF16)

    for p in pairs:
        g = _dot_nt(lhs_s[p], kb_s[p])
        a_ab = jnp.where(before, g[:n2, n2:], 0.0)
        akr_s[p, :n2] = jnp.where(before, g[:n2, :n2], 0.0).astype(BF16)
        akr_s[p, n2:] = jnp.where(upto, g[n2:, :n2], 0.0).astype(BF16)
        arb_s[p] = jnp.where(upto, g[n2:, n2:], 0.0).astype(BF16)
        aab_s[p] = a_ab
        t_s[p] = eye - jnp.where(blk(2), a_ab, 0.0)

    s = 2
    while s < c_n:
        off = jnp.logical_and(blk(2 * s), jnp.logical_not(blk(s)))
        for p in pairs:
            x_s[p] = _dot(t_s[p].astype(BF16), jnp.where(off, aab_s[p], 0.0).astype(BF16)).astype(BF16)
        for p in pairs:
            t = t_s[p]
            t_s[p] = t - _dot(x_s[p], t.astype(BF16))
        s *= 2

    for p in pairs:
        av = _dot(akr_s[p], vs_s[p])
        xin_s[p, :, LANES:] = av[:n2].astype(BF16)
        yc_s[p] = av[n2:]

    for p in pairs:
        wu_s[p] = _dot(t_s[p].astype(BF16), xin_s[p]).astype(BF16)

    for p in pairs:
        wu = wu_s[p]
        q = _dot(arb_s[p], wu)
        bwu = _dot(bht_s[p], wu)
        kv = _dot(kht_s[p], vs_s[p])
        z0 = z_scr[p].astype(BF16)
        y_p = _dot((rt_s[p] - q[:, :LANES]).astype(BF16), z0) + (yc_s[p] - q[:, LANES:])
        y_ref[0, 0, :, lanes_of(p)] = y_p[:c_n] + y_p[c_n:]
        m_z = eye * g_c[:, lanes_of(p)] - bwu[:, :LANES]
        z_scr[p] = _dot(m_z.astype(BF16), z0) + (kv - bwu[:, LANES:])


def _rw_scan(r, v, kk, lw, kd, be, l_ctx):
    bsz, l_all, rw = r.shape
    c_n = RW_CHUNK
    n_all = l_all // c_n
    n_ctx = l_ctx // c_n
    n_lat = n_all - n_ctx
    n_pairs = rw // LANES
    n2 = 2 * c_n
    vm = lambda rows, cols, dt: pltpu.VMEM((n_pairs, rows, cols), dt)

    def chunk_of(bd, s):
        fwd = s
        bwd = jnp.where(s < n_ctx, n_ctx - 1 - s, n_all + n_ctx - 1 - s)
        return jnp.where(bd % 2 == 0, fwd, bwd)

    def out_chunk(bd, s):
        c = chunk_of(bd, s)
        edge = jnp.where(bd % 2 == 0, 0, n_lat - 1)
        return jnp.where(s < n_ctx, edge, c - n_ctx)

    shared = pl.BlockSpec((1, c_n, rw), lambda bd, s: (bd // 2, chunk_of(bd, s), 0))
    per_dir = pl.BlockSpec((1, 1, c_n, rw), lambda bd, s: (bd % 2, bd // 2, chunk_of(bd, s), 0))
    return pl.pallas_call(
        functools.partial(_rw_chunk_kernel, n_pairs=n_pairs),
        out_shape=jax.ShapeDtypeStruct((2, bsz, n_lat * c_n, rw), F32),
        grid=(2 * bsz, n_all),
        in_specs=[shared, shared, shared, per_dir, per_dir, per_dir],
        out_specs=pl.BlockSpec((1, 1, c_n, rw), lambda bd, s: (bd % 2, bd // 2, out_chunk(bd, s), 0)),
        scratch_shapes=[vm(n2, n2, F32),
                        vm(2 * n2, n2, BF16),
                        vm(2 * n2, n2, BF16),
                        vm(n2, n2, BF16),
                        vm(n2, n2, BF16),
                        vm(n2, n2, BF16),
                        vm(n2, n2, F32),
                        vm(n2, n2, F32),
                        vm(2 * n2, n2, BF16),
                        vm(n2, n2, BF16),
                        vm(n2, n2, F32),
                        vm(n2, n2, BF16),
                        vm(n2, 2 * n2, BF16),
                        vm(n2, n2, F32),
                        vm(n2, 2 * n2, BF16)],
        compiler_params=_cparams(("parallel", "arbitrary")),
        name="rwkv_scan",
    )(r, v, kk, lw, kd, be)


def _s5_glu_kernel(y_ref, wa_ref, wb_ref, o_ref, h_scr):
    @pl.when(pl.program_id(1) == 0)
    def _():
        h_scr[...] = _gelu_tanh(y_ref[...]).astype(BF16)

    h = h_scr[...]
    o_ref[...] = (_dot(h, wa_ref[...]) * _sigmoid(_dot(h, wb_ref[...]))).astype(o_ref.dtype)


def _s5_glu(y2, w, *, tm, tn):
    m, k = y2.shape
    n = w.shape[1] // 2
    nj = n // tn
    return pl.pallas_call(
        _s5_glu_kernel,
        out_shape=jax.ShapeDtypeStruct((m, n), BF16),
        grid=(m // tm, nj),
        in_specs=[pl.BlockSpec((tm, k), lambda i, j: (i, 0)),
                  pl.BlockSpec((k, tn), lambda i, j: (0, j)),
                  pl.BlockSpec((k, tn), lambda i, j: (0, j + nj))],
        out_specs=pl.BlockSpec((tm, tn), lambda i, j: (i, j)),
        scratch_shapes=[pltpu.VMEM((tm, k), BF16)],
        compiler_params=_cparams(("parallel", "arbitrary")),
        name="s5_glu",
    )(y2, w, w)


def _rw_merge_kernel(yf_ref, yb_ref, bonus_ref, g_ref, lnw_ref, lnb_ref, seg_ref, segt_ref,
                     ga_ref, gb_ref, s5_ref, w_ref, o_ref, h_scr):
    @pl.when(pl.program_id(1) == 0)
    def _():
        seg = seg_ref[...]
        segt = segt_ref[...]
        inv_n = 1.0 / RW_HEAD

        def head_mean(t):
            return _dot(_dot(t, seg, HIGHEST), segt, HIGHEST) * inv_n

        y = yf_ref[0] + yb_ref[0]
        dy = y - head_mean(y)
        var = head_mean(dy * dy)
        y = dy * lax.rsqrt(var + GN_EPS) * lnw_ref[...] + lnb_ref[...] + bonus_ref[...]
        h_scr[...] = (y * g_ref[...]).astype(BF16)

    rw_out = _dot(h_scr[...], w_ref[...])
    merged = ga_ref[...].astype(F32) * s5_ref[...].astype(F32) + gb_ref[...].astype(F32) * rw_out
    o_ref[...] = merged.astype(o_ref.dtype)


def _rw_merge(y_dirs, bonus, g, ln_w, ln_b, seg, segt, gates, s5_out, w_proj, row_off, *, tm, tn):
    _, m, rw = y_dirs.shape
    n = w_proj.shape[1]
    nj = n // tn
    full = lambda shape: pl.BlockSpec(shape, lambda i, j: (0,) * len(shape))
    return pl.pallas_call(
        _rw_merge_kernel,
        out_shape=jax.ShapeDtypeStruct((m, n), BF16),
        grid=(m // tm, nj),
        in_specs=[pl.BlockSpec((1, tm, rw), lambda i, j: (0, i, 0)),
                  pl.BlockSpec((1, tm, rw), lambda i, j: (1, i, 0)),
                  pl.BlockSpec((tm, rw), lambda i, j: (row_off(i), 0)),
                  pl.BlockSpec((tm, rw), lambda i, j: (row_off(i), 0)),
                  full((1, rw)), full((1, rw)), full(seg.shape), full(segt.shape),
                  pl.BlockSpec((tm, tn), lambda i, j: (i, j)),
                  pl.BlockSpec((tm, tn), lambda i, j: (i, j + nj)),
                  pl.BlockSpec((tm, tn), lambda i, j: (i, j)),
                  pl.BlockSpec((rw, tn), lambda i, j: (0, j))],
        out_specs=pl.BlockSpec((tm, tn), lambda i, j: (i, j)),
        scratch_shapes=[pltpu.VMEM((tm, rw), BF16)],
        compiler_params=_cparams(("parallel", "arbitrary")),
        name="rwkv_merge",
    )(y_dirs, y_dirs, bonus, g, ln_w.reshape(1, rw), ln_b.reshape(1, rw), seg, segt,
      gates, gates, s5_out, w_proj)


def _resid_mm_kernel(a_ref, w_ref, x_ref, g_ref, o_ref):
    o_ref[...] = x_ref[...] + g_ref[0] * _dot(a_ref[...], w_ref[...])


def _resid_matmul(a, w, x2, g_tab, mod_row_of_block, *, tm, tn):
    m, k = a.shape
    n = w.shape[1]
    return pl.pallas_call(
        _resid_mm_kernel,
        out_shape=jax.ShapeDtypeStruct((m, n), F32),
        grid=(m // tm, n // tn),
        in_specs=[pl.BlockSpec((tm, k), lambda i, j: (i, 0)),
                  pl.BlockSpec((k, tn), lambda i, j: (0, j)),
                  pl.BlockSpec((tm, tn), lambda i, j: (i, j)),
                  pl.BlockSpec((1, 1, tn), lambda i, j: (mod_row_of_block(i), 0, j))],
        out_specs=pl.BlockSpec((tm, tn), lambda i, j: (i, j)),
        compiler_params=_cparams(("parallel", "arbitrary")),
        name="out_proj",
    )(a, w, x2, g_tab)


def _ffn_down_kernel(a_ref, w_ref, x_ref, g_ref, nf_ref, o_ref, acc_ref):
    kk = pl.program_id(1)

    @pl.when(kk == 0)
    def _():
        acc_ref[...] = jnp.zeros_like(acc_ref)

    acc_ref[...] += _dot(a_ref[...], w_ref[...])

    @pl.when(kk == pl.num_programs(1) - 1)
    def _():
        h = x_ref[...] + g_ref[0] * acc_ref[...]
        ms = jnp.mean(h * h, axis=-1, keepdims=True)
        o_ref[...] = h * lax.rsqrt(ms + NORM_EPS) * nf_ref[...]


def _ffn_down(a, w, x2, g_tab, mod_row_of_block, norm_f, *, tm, tk):
    m, k = a.shape
    n = w.shape[1]
    return pl.pallas_call(
        _ffn_down_kernel,
        out_shape=jax.ShapeDtypeStruct((m, n), F32),
        grid=(m // tm, k // tk),
        in_specs=[pl.BlockSpec((tm, tk), lambda i, kk: (i, kk)),
                  pl.BlockSpec((tk, n), lambda i, kk: (kk, 0)),
                  pl.BlockSpec((tm, n), lambda i, kk: (i, 0)),
                  pl.BlockSpec((1, 1, n), lambda i, kk: (mod_row_of_block(i), 0, 0)),
                  pl.BlockSpec((1, n), lambda i, kk: (0, 0))],
        out_specs=pl.BlockSpec((tm, n), lambda i, kk: (i, 0)),
        scratch_shapes=[pltpu.VMEM((tm, n), F32)],
        compiler_params=_cparams(("parallel", "arbitrary")),
        name="ffn_down",
    )(a, w, x2, g_tab, norm_f.reshape(1, n))


def kernel(x, c, ctx, c_ctx, ada_w, ada_b, norm1_w, w_in, rw_mu, s5_a_re, s5_a_im, s5_log_dt, s5_b_re, s5_b_im, s5_c_re, s5_c_im, s5_d, s5_glu_w, rw_w0, rw_w2, rw_a0, rw_a2, rw_g2, rw_k_k, rw_k_a, rw_r_k, rw_ln_w, rw_ln_b, rw_proj, w_o, norm2_w, ffn_w13, ffn_w2, norm_f):
    assert ada_w.shape[0] == 1, "single-layer block"
    bsz, l_lat, d = x.shape
    l_ctx = ctx.shape[1]
    l_all = l_ctx + l_lat
    s5w = s5_d.shape[1] * s5_d.shape[2]
    rw = rw_g2.shape[2]
    shift_cols = rw_mu.shape[1]
    d_ff = ffn_w2.shape[1]
    n_heads = rw // RW_HEAD

    c_rows = jnp.concatenate([c, c_ctx[None], jnp.zeros((8 - bsz - 1, d), F32)], axis=0)
    mod = _modulation(c_rows, ada_w[0], ada_b[0])
    tab = lambda k: mod[:, k * d:(k + 1) * d].reshape(8, 1, d)
    sh1, sc1, g1, sh2, sc2, g2 = (tab(k) for k in range(N_MOD))
    ctx_row = bsz

    tm_in = l_ctx
    blocks_per_batch = l_all // tm_in
    x_all = jnp.concatenate([ctx, x], axis=1).reshape(bsz * l_all, d)
    w_in_b = w_in[0].astype(BF16)
    n_mix = s5w + shift_cols

    def mix_mod_row(i):
        return jnp.where(i % blocks_per_batch == 0, ctx_row, i // blocks_per_batch)

    u_all = _lnmod_matmul(x_all, norm1_w[0], sh1, sc1, mix_mod_row, w_in_b[:, :s5w],
                          tm=tm_in, tn=s5w, out_dtype=F32, name="in_proj_s5").reshape(bsz, l_all, s5w)
    z_rw = _lnmod_matmul(x_all, norm1_w[0], sh1, sc1, mix_mod_row, w_in_b[:, s5w:n_mix],
                         tm=tm_in, tn=shift_cols // 3, out_dtype=F32, name="in_proj_rw").reshape(bsz, l_all, shift_cols)
    x2 = x.reshape(bsz * l_lat, d)
    tm = 512
    lat_mod_row = lambda i: i // (l_lat // tm)
    gates = _lnmod_matmul(x2, norm1_w[0], sh1, sc1, lat_mod_row, w_in_b[:, n_mix:],
                          tm=tm, tn=1024, out_dtype=BF16, epilogue="sigmoid", name="in_proj_gates")

    y_s5 = _s5_branch(u_all, l_ctx, s5_a_re[0], s5_a_im[0], s5_log_dt[0], s5_b_re[0], s5_b_im[0],
                      s5_c_re[0], s5_c_im[0], s5_d[0])
    s5_out = _s5_glu(y_s5.reshape(bsz * l_lat, s5w), s5_glu_w[0].astype(BF16), tm=tm, tn=1024)

    lora = rw_w2.shape[2]
    zl = jnp.zeros((lora, rw), F32)
    w2bd = jnp.concatenate([jnp.concatenate([rw_w2[0, 0], zl], axis=1),
                            jnp.concatenate([zl, rw_w2[0, 1]], axis=1)], axis=0)
    a2bd = jnp.concatenate([jnp.concatenate([rw_a2[0, 0], zl], axis=1),
                            jnp.concatenate([zl, rw_a2[0, 1]], axis=1)], axis=0)
    head_of = jnp.arange(rw) // RW_HEAD
    seg = (head_of[:, None] == jnp.arange(LANES)[None, :]).astype(F32)
    segt = seg.T
    r, v, kk, g, bonus, lw, kd, be = _rw_prep(
        z_rw, l_ctx, rw_mu[0], w2bd, a2bd, rw_g2[0], rw_w0[0].reshape(1, 2 * rw), rw_a0[0].reshape(1, 2 * rw),
        rw_k_k[0], rw_k_a[0], rw_r_k[0].reshape(rw), seg, segt)
    y_dirs = _rw_scan(r, v, kk, lw, kd, be, l_ctx)

    tm_m = l_ctx
    per_b = l_lat // tm_m
    row_off = lambda i: (i // per_b) * (per_b + 1) + 1 + i % per_b
    merged = _rw_merge(y_dirs.reshape(2, bsz * l_lat, rw), bonus.reshape(bsz * l_all, rw),
                       g.reshape(bsz * l_all, rw), rw_ln_w[0], rw_ln_b[0], seg, segt,
                       gates, s5_out, rw_proj[0].astype(BF16), row_off, tm=tm_m, tn=1024)
    h1 = _resid_matmul(merged, w_o[0].astype(BF16), x2, g1, lat_mod_row, tm=tm, tn=1024)

    act = _lnmod_swiglu(h1, norm2_w[0], sh2, sc2, lat_mod_row, ffn_w13[0].astype(BF16), d_ff,
                        tm=tm, tn=512, name="ffn_up")
    out = _ffn_down(act, ffn_w2[0].astype(BF16), h1, g2, lat_mod_row, norm_f, tm=tm, tk=512)
    return out.reshape(bsz, l_lat, d)
```

```python
import functools
import math

import jax
import jax.numpy as jnp
from jax import lax
from jax.experimental import pallas as pl
from jax.experimental.pallas import tpu as pltpu

F32 = jnp.float32
BF16 = jnp.bfloat16
HIGHEST = lax.Precision.HIGHEST

D_MODEL = 2048
N_MOD = 6
NORM_EPS = 1e-6
GN_EPS = 64e-5
GRID_W = 64
S5_GROUP = 16
S5_STATE = 64
S5_CHUNK = 16
RW_HEAD = 64
RW_CHUNK = 64
LANES = 128
VMEM_LIMIT = 48 * 1024 * 1024


def _cparams(sem):
    return pltpu.CompilerParams(dimension_semantics=sem, vmem_limit_bytes=VMEM_LIMIT)


def _operands(a, b, precision):
    if precision == "bf16":
        return a.astype(BF16), b.astype(BF16), None
    return a, b, precision


def _dot(a, b, precision=None):
    a, b, precision = _operands(a, b, precision)
    return jnp.dot(a, b, preferred_element_type=F32, precision=precision)


def _dot_nt(a, b, precision=None):
    a, b, precision = _operands(a, b, precision)
    return lax.dot_general(a, b, (((1,), (1,)), ((), ())), preferred_element_type=F32, precision=precision)


def _dot_hilo(a, ind):
    hi = a.astype(BF16)
    lo = (a - hi.astype(F32)).astype(BF16)
    return _dot(hi, ind) + _dot(lo, ind)


def _sigmoid(x):
    return 1.0 / (1.0 + jnp.exp(-x))


def _silu(x):
    return x * _sigmoid(x)


def _gelu_tanh(x):
    c = math.sqrt(2.0 / math.pi)
    return 0.5 * x * (1.0 + jnp.tanh(c * (x + 0.044715 * (x * x * x))))


def _softplus(x):
    return jnp.maximum(x, 0.0) + jnp.log(1.0 + jnp.exp(-jnp.abs(x)))


def _mod_kernel(c_ref, w_ref, b_ref, o_ref):
    o_ref[...] = _dot(_silu(c_ref[...]), w_ref[...], HIGHEST) + b_ref[...]


def _modulation(c_rows, ada_w, ada_b):
    m, d = c_rows.shape
    n = ada_w.shape[1]
    tn = 1024
    return pl.pallas_call(
        _mod_kernel,
        out_shape=jax.ShapeDtypeStruct((m, n), F32),
        grid=(n // tn,),
        in_specs=[pl.BlockSpec((m, d), lambda j: (0, 0)),
                  pl.BlockSpec((d, tn), lambda j: (0, j)),
                  pl.BlockSpec((1, tn), lambda j: (0, j))],
        out_specs=pl.BlockSpec((m, tn), lambda j: (0, j)),
        compiler_params=_cparams(("arbitrary",)),
        name="modulation",
    )(c_rows, ada_w, ada_b.reshape(1, n))


def _lnmod_rows(x, nw, sh, sc):
    ms = jnp.mean(x * x, axis=-1, keepdims=True)
    y = x * lax.rsqrt(ms + NORM_EPS) * nw
    return y * (1.0 + sc) + sh


def _lnmod_mm_kernel(x_ref, nw_ref, sh_ref, sc_ref, w_ref, o_ref, h_scr, *, epilogue):
    @pl.when(pl.program_id(1) == 0)
    def _():
        h_scr[...] = _lnmod_rows(x_ref[...], nw_ref[...], sh_ref[0], sc_ref[0]).astype(BF16)

    z = _dot(h_scr[...], w_ref[...])
    if epilogue == "sigmoid":
        z = _sigmoid(z)
    o_ref[...] = z.astype(o_ref.dtype)


def _lnmod_swiglu_kernel(x_ref, nw_ref, sh_ref, sc_ref, w1_ref, w3_ref, o_ref, h_scr):
    @pl.when(pl.program_id(1) == 0)
    def _():
        h_scr[...] = _lnmod_rows(x_ref[...], nw_ref[...], sh_ref[0], sc_ref[0]).astype(BF16)

    h = h_scr[...]
    o_ref[...] = (_silu(_dot(h, w1_ref[...])) * _dot(h, w3_ref[...])).astype(o_ref.dtype)


def _lnmod_matmul(x2, nw, sh_tab, sc_tab, mod_row_of_block, w, *, tm, tn, out_dtype, epilogue=None, name):
    m, d = x2.shape
    n = w.shape[1]
    mod_map = lambda i, j: (mod_row_of_block(i), 0, 0)
    return pl.pallas_call(
        functools.partial(_lnmod_mm_kernel, epilogue=epilogue),
        out_shape=jax.ShapeDtypeStruct((m, n), out_dtype),
        grid=(m // tm, n // tn),
        in_specs=[pl.BlockSpec((tm, d), lambda i, j: (i, 0)),
                  pl.BlockSpec((1, d), lambda i, j: (0, 0)),
                  pl.BlockSpec((1, 1, d), mod_map),
                  pl.BlockSpec((1, 1, d), mod_map),
                  pl.BlockSpec((d, tn), lambda i, j: (0, j))],
        out_specs=pl.BlockSpec((tm, tn), lambda i, j: (i, j)),
        scratch_shapes=[pltpu.VMEM((tm, d), BF16)],
        compiler_params=_cparams(("parallel", "arbitrary")),
        name=name,
    )(x2, nw.reshape(1, d), sh_tab, sc_tab, w)


def _lnmod_swiglu(x2, nw, sh_tab, sc_tab, mod_row_of_block, w13, d_ff, *, tm, tn, name):
    m, d = x2.shape
    nj = d_ff // tn
    mod_map = lambda i, j: (mod_row_of_block(i), 0, 0)
    return pl.pallas_call(
        _lnmod_swiglu_kernel,
        out_shape=jax.ShapeDtypeStruct((m, d_ff), BF16),
        grid=(m // tm, nj),
        in_specs=[pl.BlockSpec((tm, d), lambda i, j: (i, 0)),
                  pl.BlockSpec((1, d), lambda i, j: (0, 0)),
                  pl.BlockSpec((1, 1, d), mod_map),
                  pl.BlockSpec((1, 1, d), mod_map),
                  pl.BlockSpec((d, tn), lambda i, j: (0, j)),
                  pl.BlockSpec((d, tn), lambda i, j: (0, j + nj))],
        out_specs=pl.BlockSpec((tm, tn), lambda i, j: (i, j)),
        scratch_shapes=[pltpu.VMEM((tm, d), BF16)],
        compiler_params=_cparams(("parallel", "arbitrary")),
        name=name,
    )(x2, nw.reshape(1, d), sh_tab, sc_tab, w13, w13)


def _s5_param_kernel(are_ref, aim_ref, ldt_ref, bre_ref, bim_ref, cre_ref, cim_ref,
                     e_ref, cs_ref, kt_ref, a16_ref):
    t_n, hg, p_n = S5_CHUNK, S5_GROUP, S5_STATE
    tau = lax.broadcasted_iota(jnp.int32, (t_n, 1, p_n), 0).astype(F32)
    for d in range(2):
        a_re = are_ref[0, d:d + 1, :]
        a_im = aim_ref[0, d:d + 1, :]
        dt = jnp.exp(ldt_ref[0, d:d + 1, :])
        lam = a_re * dt
        th = a_im * dt
        er = jnp.exp(lam)
        ab_re = er * jnp.cos(th)
        ab_im = er * jnp.sin(th)
        den = a_re * a_re + a_im * a_im
        x_re = ab_re - 1.0
        co_re = (x_re * a_re + ab_im * a_im) / den
        co_im = (ab_im * a_re - x_re * a_im) / den
        bt_re = bre_ref[0, d]
        bt_im = bim_ref[0, d]
        bb_re = co_re * bt_re - co_im * bt_im
        bb_im = co_re * bt_im + co_im * bt_re
        c_re = cre_ref[0, d]
        c_im = cim_ref[0, d]

        def power(tv):
            mag = jnp.exp(tv * lam)
            return mag * jnp.cos(tv * th), mag * jnp.sin(tv * th)

        pw_re, pw_im = power(tau)
        cp_re = (c_re[None] * pw_re - c_im[None] * pw_im).reshape(t_n * hg, p_n)
        cp_im = (c_re[None] * pw_im + c_im[None] * pw_re).reshape(t_n * hg, p_n)
        kt_ref[0, d] = _dot_nt(bb_re, cp_re, HIGHEST) - _dot_nt(bb_im, cp_im, HIGHEST)

        te = (t_n - 1.0 - tau) if d == 0 else tau
        pe_re, pe_im = power(te)
        e_ref[0, d, :, 0:p_n] = (pe_re * bb_re[None] - pe_im * bb_im[None]).reshape(t_n * hg, p_n)
        e_ref[0, d, :, p_n:2 * p_n] = (pe_re * bb_im[None] + pe_im * bb_re[None]).reshape(t_n * hg, p_n)

        tc = (tau + 1.0) if d == 0 else (t_n - tau)
        pc_re, pc_im = power(tc)
        cs_ref[0, d, :, 0:p_n] = (c_re[None] * pc_re - c_im[None] * pc_im).reshape(t_n * hg, p_n)
        cs_ref[0, d, :, p_n:2 * p_n] = -(c_re[None] * pc_im + c_im[None] * pc_re).reshape(t_n * hg, p_n)

        mag16 = jnp.exp(float(t_n) * lam)
        a16_ref[0, d, 0:1, :] = mag16 * jnp.cos(float(t_n) * th)
        a16_ref[0, d, 1:2, :] = mag16 * jnp.sin(float(t_n) * th)


def _s5_params(a_re, a_im, log_dt, b_re, b_im, c_re, c_im):
    g_n = a_re.shape[1]
    p_n, hg, t_n = S5_STATE, S5_GROUP, S5_CHUNK
    tr = lambda a: jnp.swapaxes(a, 0, 1)
    ldt = jnp.broadcast_to(tr(log_dt)[:, :, None], (g_n, 2, p_n))
    spec3 = pl.BlockSpec((1, 2, p_n), lambda g: (g, 0, 0))
    spec4 = pl.BlockSpec((1, 2, hg, p_n), lambda g: (g, 0, 0, 0))
    th = t_n * hg
    return pl.pallas_call(
        _s5_param_kernel,
        out_shape=(jax.ShapeDtypeStruct((g_n, 2, th, 2 * p_n), F32),
                   jax.ShapeDtypeStruct((g_n, 2, th, 2 * p_n), F32),
                   jax.ShapeDtypeStruct((g_n, 2, hg, th), F32),
                   jax.ShapeDtypeStruct((g_n, 2, 2, p_n), F32)),
        grid=(g_n,),
        in_specs=[spec3, spec3, spec3, spec4, spec4, spec4, spec4],
        out_specs=(pl.BlockSpec((1, 2, th, 2 * p_n), lambda g: (g, 0, 0, 0)),
                   pl.BlockSpec((1, 2, th, 2 * p_n), lambda g: (g, 0, 0, 0)),
                   pl.BlockSpec((1, 2, hg, th), lambda g: (g, 0, 0, 0)),
                   pl.BlockSpec((1, 2, 2, p_n), lambda g: (g, 0, 0, 0))),
        compiler_params=_cparams(("parallel",)),
        name="s5_params",
    )(tr(a_re), tr(a_im), ldt,
      jnp.transpose(b_re, (1, 0, 3, 2)), jnp.transpose(b_im, (1, 0, 3, 2)), tr(c_re), tr(c_im))


def _s5_state_in_kernel(u_ref, e_ref, o_ref):
    e = e_ref[0]
    u = u_ref[0]
    o_ref[0, :, 0:LANES] = _dot(u, e[0], HIGHEST)
    o_ref[0, :, LANES:2 * LANES] = _dot(u, e[1], HIGHEST)


def _s5_state_inputs(u_g, e):
    g_n, rows, th = u_g.shape
    return pl.pallas_call(
        _s5_state_in_kernel,
        out_shape=jax.ShapeDtypeStruct((g_n, rows, 2 * LANES), F32),
        grid=(g_n,),
        in_specs=[pl.BlockSpec((1, rows, th), lambda g: (g, 0, 0)),
                  pl.BlockSpec((1, 2, th, LANES), lambda g: (g, 0, 0, 0))],
        out_specs=pl.BlockSpec((1, rows, 2 * LANES), lambda g: (g, 0, 0)),
        compiler_params=_cparams(("parallel",)),
        name="s5_state_inputs",
    )(u_g, e)


def _s5_scan_kernel(e_ref, a_ref, o_ref, *, n_ctx, n_all):
    zero = jnp.zeros(e_ref.shape[2:], F32)

    def step(pr, pi):
        ar, ai = a_ref[pr], a_ref[pi]

        def body(c, carry):
            sr, si = carry
            o_ref[pr, c] = sr
            o_ref[pi, c] = si
            return (ar * sr - ai * si + e_ref[pr, c], ar * si + ai * sr + e_ref[pi, c])
        return body

    lax.fori_loop(0, n_all, step(0, 1), (zero, zero))
    bwd = step(2, 3)
    carry = lax.fori_loop(0, n_ctx, lambda k, cy: bwd(n_ctx - 1 - k, cy), (zero, zero))
    lax.fori_loop(0, n_all - n_ctx, lambda k, cy: bwd(n_all - 1 - k, cy), carry)


def _s5_scan(e_planes, a_planes, n_ctx, n_all):
    _, rows, r_n, _ = e_planes.shape
    bsz = rows // n_all
    sub = 8
    return pl.pallas_call(
        functools.partial(_s5_scan_kernel, n_ctx=n_ctx, n_all=n_all),
        out_shape=jax.ShapeDtypeStruct(e_planes.shape, F32),
        grid=(bsz, r_n // sub),
        in_specs=[pl.BlockSpec((4, n_all, sub, LANES), lambda b, q: (0, b, q, 0)),
                  pl.BlockSpec((4, sub, LANES), lambda b, q: (0, q, 0))],
        out_specs=pl.BlockSpec((4, n_all, sub, LANES), lambda b, q: (0, b, q, 0)),
        compiler_params=_cparams(("parallel", "parallel")),
        name="s5_scan",
    )(e_planes, a_planes)


def _s5_apply_kernel(u_ref, s_ref, mf_ref, mb_ref, d_ref, cs_ref, o_ref, *, n_ctx):
    u = u_ref[0, n_ctx:, :]
    s = s_ref[0, n_ctx:, :]
    y = _dot(u, mf_ref[0] + mb_ref[0], HIGHEST) + u * d_ref[0]
    y = y + _dot_nt(s[:, 0:LANES], cs_ref[0, 0], HIGHEST) + _dot_nt(s[:, LANES:2 * LANES], cs_ref[0, 1], HIGHEST)
    o_ref[0, 0] = y


def _s5_apply(u_g, s_g, m_f, m_b, d_t, cs, n_ctx, n_all):
    g_n, rows, th = u_g.shape
    bsz = rows // n_all
    n_lat = n_all - n_ctx
    return pl.pallas_call(
        functools.partial(_s5_apply_kernel, n_ctx=n_ctx),
        out_shape=jax.ShapeDtypeStruct((g_n, bsz, n_lat, th), F32),
        grid=(g_n, bsz),
        in_specs=[pl.BlockSpec((1, n_all, th), lambda g, b: (g, b, 0)),
                  pl.BlockSpec((1, n_all, 2 * LANES), lambda g, b: (g, b, 0)),
                  pl.BlockSpec((1, th, th), lambda g, b: (g, 0, 0)),
                  pl.BlockSpec((1, th, th), lambda g, b: (g, 0, 0)),
                  pl.BlockSpec((1, 1, th), lambda g, b: (g, 0, 0)),
                  pl.BlockSpec((1, 2, th, LANES), lambda g, b: (g, 0, 0, 0))],
        out_specs=pl.BlockSpec((1, 1, n_lat, th), lambda g, b: (g, b, 0, 0)),
        compiler_params=_cparams(("parallel", "arbitrary")),
        name="s5_apply",
    )(u_g, s_g, m_f, m_b, d_t, cs)


def _s5_branch(u_all, n_ctx_tok, a_re, a_im, log_dt, b_re, b_im, c_re, c_im, s5_d):
    bsz, l_all, width = u_all.shape
    hg, t_n, p_n = S5_GROUP, S5_CHUNK, S5_STATE
    g_n = width // hg
    n_all = l_all // t_n
    n_ctx = n_ctx_tok // t_n
    th = t_n * hg
    e, cs, kt, a16 = _s5_params(a_re, a_im, log_dt, b_re, b_im, c_re, c_im)

    kt5 = kt.reshape(g_n, 2, hg, t_n, hg)
    ii = jnp.arange(t_n)[:, None]
    jj = jnp.arange(t_n)[None, :]

    def toeplitz(k4, lag, keep):
        m = k4[:, :, jnp.clip(lag, 0, t_n - 1), :]
        m = jnp.where(keep[None, None, :, :, None], m, 0.0)
        return jnp.transpose(m, (0, 2, 1, 3, 4)).reshape(g_n, th, th)

    m_f = toeplitz(kt5[:, 0], jj - ii, jj >= ii)
    m_b = toeplitz(kt5[:, 1], ii - jj, ii >= jj)
    d_t = jnp.tile(s5_d, (1, t_n)).reshape(g_n, 1, th)

    u_g = jnp.transpose(u_all.reshape(bsz, n_all, t_n, g_n, hg), (3, 0, 1, 2, 4)).reshape(g_n, bsz * n_all, th)
    e_cat = e
    s_in = _s5_state_inputs(u_g, e_cat)
    rows = bsz * n_all
    planes = jnp.transpose(s_in.reshape(g_n, rows, 4, p_n), (2, 1, 0, 3)).reshape(4, rows, g_n * p_n // LANES, LANES)
    a_pl = jnp.transpose(a16.reshape(g_n, 4, p_n), (1, 0, 2)).reshape(4, g_n * p_n // LANES, LANES)
    st = _s5_scan(planes, a_pl, n_ctx, n_all)
    s_g = jnp.transpose(st.reshape(4, rows, g_n, p_n), (2, 1, 0, 3)).reshape(g_n, rows, 4 * p_n)
    y_g = _s5_apply(u_g, s_g, m_f, m_b, d_t, cs, n_ctx, n_all)
    n_lat = n_all - n_ctx
    y = jnp.transpose(y_g.reshape(g_n, bsz, n_lat, t_n, hg), (1, 2, 3, 0, 4))
    return y.reshape(bsz, n_lat * t_n, width)


def _rw_prep_kernel(z_ref, zp_ref, zn_ref, mu_ref, w2_ref, a2_ref, g2_ref, w0_ref, a0_ref,
                    kk_w_ref, ka_ref, rk_ref, seg_ref, segt_ref,
                    r_ref, v_ref, kk_ref, g_ref, bonus_ref, lw_ref, kd_ref, be_ref,
                    *, tm, l_lat, rw):
    j = pl.program_id(1)
    z = z_ref[...]
    lat = j > 0
    tl = lax.broadcasted_iota(jnp.int32, (tm, 1), 0)
    tok = (j - 1) * tm + tl
    col = tl % GRID_W
    m_l = jnp.where(lat, col, tl) > 0
    m_r = jnp.where(lat, col - (GRID_W - 1), tl - (tm - 1)) < 0
    m_u = jnp.logical_and(lat, tok >= GRID_W)
    m_d = jnp.logical_and(lat, tok < l_lat - GRID_W)
    z_l = pltpu.roll(z, 1, 0)
    z_r = pltpu.roll(z, tm - 1, 0)
    z_u = jnp.concatenate([zp_ref[...], z[:tm - GRID_W]], axis=0)
    z_d = jnp.concatenate([z[GRID_W:], zn_ref[...]], axis=0)
    s = (jnp.where(m_l, z_l, 0.0) + jnp.where(m_r, z_r, 0.0)
         + jnp.where(m_u, z_u, 0.0) + jnp.where(m_d, z_d, 0.0))
    cnt = (m_l.astype(F32) + m_r.astype(F32)) + (m_u.astype(F32) + m_d.astype(F32))
    zs = z + (s / cnt - z) * mu_ref[...]

    r = zs[:, 0:rw]
    k = zs[:, rw:2 * rw]
    v = zs[:, 2 * rw:3 * rw]
    o = 3 * rw
    wd = zs[:, o:o + LANES]
    ad = zs[:, o + LANES:o + 2 * LANES]
    gd = zs[:, o + 2 * LANES:o + 3 * LANES]

    seg = seg_ref[...]
    segt = segt_ref[...]

    def head_sum(t):
        return _dot_hilo(_dot_hilo(t, seg), segt)

    g_ref[0] = _dot(_sigmoid(gd), g2_ref[...], "bf16")
    kk = k * kk_w_ref[...]
    kk = kk * lax.rsqrt(head_sum(kk * kk) + 1e-12)
    wl = w0_ref[...] + _dot(jnp.tanh(wd), w2_ref[...], "bf16")
    al = a0_ref[...] + _dot(ad, a2_ref[...], "bf16")
    r_ref[0] = r
    v_ref[0] = v
    kk_ref[0] = kk
    coef = jnp.zeros_like(r)
    for d in range(2):
        w_raw = -_softplus(-wl[:, d * rw:(d + 1) * rw]) - 0.5
        a = _sigmoid(al[:, d * rw:(d + 1) * rw])
        k_d = k * (1.0 + (a - 1.0) * ka_ref[...])
        lw_ref[d, 0] = -jnp.exp(w_raw)
        kd_ref[d, 0] = k_d
        be_ref[d, 0] = kk * a
        coef = coef + head_sum(r * k_d * rk_ref[...])
    bonus_ref[0] = coef * v


def _rw_prep(z_rw, bsz, l_ctx, l_lat, mu, w2bd, a2bd, g2, w0cat, a0cat, k_k, k_a, r_k_flat, seg, segt):
    cols = z_rw.shape[1]
    tm = l_ctx
    rw = g2.shape[1]
    l_all = l_ctx + l_lat
    nblk = l_all // tm
    lat_blk = l_lat // tm
    hb = tm // GRID_W
    lat_hblk = l_lat // GRID_W

    def main_blk(b, j):
        return jnp.where(j == 0, bsz * lat_blk + b, b * lat_blk + j - 1)

    def prev_halo(b, j):
        return b * lat_hblk + jnp.maximum((j - 1) * hb - 1, 0)

    def next_halo(b, j):
        return b * lat_hblk + jnp.minimum(jnp.maximum(j, 1) * hb, lat_hblk - 1)

    full = lambda shape: pl.BlockSpec(shape, lambda b, j: (0,) * len(shape))
    shared = jax.ShapeDtypeStruct((bsz, l_all, rw), F32)
    lat_only = jax.ShapeDtypeStruct((bsz, l_lat, rw), F32)
    per_dir = jax.ShapeDtypeStruct((2, bsz, l_all, rw), F32)
    o_shared = pl.BlockSpec((1, tm, rw), lambda b, j: (b, j, 0))
    o_lat = pl.BlockSpec((1, tm, rw), lambda b, j: (b, jnp.maximum(j - 1, 0), 0))
    o_dir = pl.BlockSpec((2, 1, tm, rw), lambda b, j: (0, b, j, 0))
    return pl.pallas_call(
        functools.partial(_rw_prep_kernel, tm=tm, l_lat=l_lat, rw=rw),
        out_shape=(shared,) * 3 + (lat_only,) * 2 + (per_dir,) * 3,
        grid=(bsz, nblk),
        in_specs=[pl.BlockSpec((tm, cols), lambda b, j: (main_blk(b, j), 0)),
                  pl.BlockSpec((GRID_W, cols), lambda b, j: (prev_halo(b, j), 0)),
                  pl.BlockSpec((GRID_W, cols), lambda b, j: (next_halo(b, j), 0)),
                  full((1, cols)), full(w2bd.shape), full(a2bd.shape), full(g2.shape),
                  full((1, 2 * rw)), full((1, 2 * rw)), full((1, rw)), full((1, rw)), full((1, rw)),
                  full(seg.shape), full(segt.shape)],
        out_specs=(o_shared,) * 3 + (o_lat,) * 2 + (o_dir,) * 3,
        compiler_params=_cparams(("parallel", "arbitrary")),
        name="rwkv_prep",
    )(z_rw, z_rw, z_rw, mu.reshape(1, cols), w2bd, a2bd, g2, w0cat, a0cat,
      k_k.reshape(1, rw), k_a.reshape(1, rw), r_k_flat.reshape(1, rw), seg, segt)


def _stack_heads(x, head0):
    return jnp.concatenate([jnp.where(head0, x, 0.0), jnp.where(head0, 0.0, x)], axis=0)


def _rw_chunk_kernel(r_ref, v_ref, kk_ref, lw_ref, kd_ref, be_ref, y_ref,
                     z_scr, lhs_s, kb_s, kht_s, bht_s, vs_s, rt_s, aab_s, akr_s, arb_s, t_s, x_s, xin_s, yc_s, wu_s,
                     *, n_pairs):
    c_n = RW_CHUNK
    n2 = 2 * c_n
    rev = (pl.program_id(0) % 2) == 1

    @pl.when(pl.program_id(1) == 0)
    def _():
        z_scr[...] = jnp.zeros_like(z_scr)

    ri = lax.broadcasted_iota(jnp.int32, (c_n, c_n), 0)
    ci = lax.broadcasted_iota(jnp.int32, (c_n, c_n), 1)
    tri = (jnp.where(rev, ci - ri, ri - ci) >= 0).astype(F32)
    r2 = lax.broadcasted_iota(jnp.int32, (n2, n2), 0)
    c2 = lax.broadcasted_iota(jnp.int32, (n2, n2), 1)
    t2 = r2 % c_n
    i2 = c2 % c_n
    same_head = (r2 // c_n) == (c2 // c_n)
    before = jnp.logical_and(same_head, jnp.where(rev, i2 - t2, t2 - i2) > 0)
    upto = jnp.logical_or(before, r2 == c2)
    eye = (r2 == c2).astype(F32)
    head0 = lax.broadcasted_iota(jnp.int32, (1, LANES), 1) < RW_HEAD

    def blk(s):
        return (r2 // s) == (c2 // s)

    pairs = range(n_pairs)
    lanes_of = lambda p: slice(p * LANES, (p + 1) * LANES)

    lw = lw_ref[0, 0]
    cum = _dot(tri, lw, HIGHEST)
    tot = jnp.sum(lw, axis=0, keepdims=True)
    g_inv = jnp.exp(-cum)
    g_hat = jnp.exp(tot - cum)
    kd = kd_ref[0, 0]
    be = be_ref[0, 0]
    at = kk_ref[0] * jnp.exp(cum - lw)
    rt = r_ref[0] * jnp.exp(cum)
    kt = kd * g_inv
    bt = be * g_inv
    kh = kd * g_hat
    bh = be * g_hat
    g_c = jnp.exp(tot)
    v = v_ref[0]
    for p in pairs:
        st = lambda x: _stack_heads(x[:, lanes_of(p)], head0)
        rt_p = st(rt)
        at_p = st(at).astype(BF16)
        lhs_s[p, :n2] = at_p
        lhs_s[p, n2:] = rt_p.astype(BF16)
        xin_s[p, :, :LANES] = at_p
        rt_s[p] = rt_p
        kb_s[p, :n2] = st(kt).astype(BF16)
        kb_s[p, n2:] = st(bt).astype(BF16)
        kht_s[p] = st(kh).T.astype(BF16)
        bht_s[p] = st(bh).T.astype(BF16)
        vs_s[p] = st(v).astype(BF16)

    for p in pairs:
        g = _dot_nt(lhs_s[p], kb_s[p])
        a_ab = jnp.where(before, g[:n2, n2:], 0.0)
        akr_s[p, :n2] = jnp.where(before, g[:n2, :n2], 0.0).astype(BF16)
        akr_s[p, n2:] = jnp.where(upto, g[n2:, :n2], 0.0).astype(BF16)
        arb_s[p] = jnp.where(upto, g[n2:, n2:], 0.0).astype(BF16)
        aab_s[p] = a_ab
        t_s[p] = eye - jnp.where(blk(2), a_ab, 0.0)

    s = 2
    while s < c_n:
        off = jnp.logical_and(blk(2 * s), jnp.logical_not(blk(s)))
        for p in pairs:
            x_s[p] = _dot(t_s[p].astype(BF16), jnp.where(off, aab_s[p], 0.0).astype(BF16)).astype(BF16)
        for p in pairs:
            t = t_s[p]
            t_s[p] = t - _dot(x_s[p], t.astype(BF16))
        s *= 2

    for p in pairs:
        av = _dot(akr_s[p], vs_s[p])
        xin_s[p, :, LANES:] = av[:n2].astype(BF16)
        yc_s[p] = av[n2:]

    for p in pairs:
        wu_s[p] = _dot(t_s[p].astype(BF16), xin_s[p]).astype(BF16)

    for p in pairs:
        wu = wu_s[p]
        q = _dot(arb_s[p], wu)
        bwu = _dot(bht_s[p], wu)
        kv = _dot(kht_s[p], vs_s[p])
        z0 = z_scr[p].astype(BF16)
        y_p = _dot((rt_s[p] - q[:, :LANES]).astype(BF16), z0) + (yc_s[p] - q[:, LANES:])
        y_ref[0, 0, :, lanes_of(p)] = y_p[:c_n] + y_p[c_n:]
        m_z = eye * g_c[:, lanes_of(p)] - bwu[:, :LANES]
        z_scr[p] = _dot(m_z.astype(BF16), z0) + (kv - bwu[:, LANES:])


def _rw_scan(r, v, kk, lw, kd, be, l_ctx):
    bsz, l_all, rw = r.shape
    c_n = RW_CHUNK
    n_all = l_all // c_n
    n_ctx = l_ctx // c_n
    n_lat = n_all - n_ctx
    n_pairs = rw // LANES
    n2 = 2 * c_n
    vm = lambda rows, cols, dt: pltpu.VMEM((n_pairs, rows, cols), dt)

    def chunk_of(bd, s):
        fwd = s
        bwd = jnp.where(s < n_ctx, n_ctx - 1 - s, n_all + n_ctx - 1 - s)
        return jnp.where(bd % 2 == 0, fwd, bwd)

    def out_chunk(bd, s):
        c = chunk_of(bd, s)
        edge = jnp.where(bd % 2 == 0, 0, n_lat - 1)
        return jnp.where(s < n_ctx, edge, c - n_ctx)

    shared = pl.BlockSpec((1, c_n, rw), lambda bd, s: (bd // 2, chunk_of(bd, s), 0))
    per_dir = pl.BlockSpec((1, 1, c_n, rw), lambda bd, s: (bd % 2, bd // 2, chunk_of(bd, s), 0))
    return pl.pallas_call(
        functools.partial(_rw_chunk_kernel, n_pairs=n_pairs),
        out_shape=jax.ShapeDtypeStruct((2, bsz, n_lat * c_n, rw), F32),
        grid=(2 * bsz, n_all),
        in_specs=[shared, shared, shared, per_dir, per_dir, per_dir],
        out_specs=pl.BlockSpec((1, 1, c_n, rw), lambda bd, s: (bd % 2, bd // 2, out_chunk(bd, s), 0)),
        scratch_shapes=[vm(n2, n2, F32),
                        vm(2 * n2, n2, BF16),
                        vm(2 * n2, n2, BF16),
                        vm(n2, n2, BF16),
                        vm(n2, n2, BF16),
                        vm(n2, n2, BF16),
                        vm(n2, n2, F32),
                        vm(n2, n2, F32),
                        vm(2 * n2, n2, BF16),
                        vm(n2, n2, BF16),
                        vm(n2, n2, F32),
                        vm(n2, n2, BF16),
                        vm(n2, 2 * n2, BF16),
                        vm(n2, n2, F32),
                        vm(n2, 2 * n2, BF16)],
        compiler_params=_cparams(("parallel", "arbitrary")),
        name="rwkv_scan",
    )(r, v, kk, lw, kd, be)


def _s5_glu_kernel(y_ref, wa_ref, wb_ref, o_ref, h_scr):
    @pl.when(pl.program_id(1) == 0)
    def _():
        h_scr[...] = _gelu_tanh(y_ref[...]).astype(BF16)

    h = h_scr[...]
    o_ref[...] = (_dot(h, wa_ref[...]) * _sigmoid(_dot(h, wb_ref[...]))).astype(o_ref.dtype)


def _s5_glu(y2, w, *, tm, tn):
    m, k = y2.shape
    n = w.shape[1] // 2
    nj = n // tn
    return pl.pallas_call(
        _s5_glu_kernel,
        out_shape=jax.ShapeDtypeStruct((m, n), BF16),
        grid=(m // tm, nj),
        in_specs=[pl.BlockSpec((tm, k), lambda i, j: (i, 0)),
                  pl.BlockSpec((k, tn), lambda i, j: (0, j)),
                  pl.BlockSpec((k, tn), lambda i, j: (0, j + nj))],
        out_specs=pl.BlockSpec((tm, tn), lambda i, j: (i, j)),
        scratch_shapes=[pltpu.VMEM((tm, k), BF16)],
        compiler_params=_cparams(("parallel", "arbitrary")),
        name="s5_glu",
    )(y2, w, w)


def _rw_merge_kernel(yf_ref, yb_ref, bonus_ref, g_ref, lnw_ref, lnb_ref, seg_ref, segt_ref,
                     ga_ref, gb_ref, s5_ref, w_ref, o_ref, h_scr):
    @pl.when(pl.program_id(1) == 0)
    def _():
        seg = seg_ref[...]
        segt = segt_ref[...]
        inv_n = 1.0 / RW_HEAD

        def head_mean(t):
            return _dot_hilo(_dot_hilo(t, seg), segt) * inv_n

        y = yf_ref[0] + yb_ref[0]
        dy = y - head_mean(y)
        var = head_mean(dy * dy)
        y = dy * lax.rsqrt(var + GN_EPS) * lnw_ref[...] + lnb_ref[...] + bonus_ref[...]
        h_scr[...] = (y * g_ref[...]).astype(BF16)

    rw_out = _dot(h_scr[...], w_ref[...])
    merged = ga_ref[...].astype(F32) * s5_ref[...].astype(F32) + gb_ref[...].astype(F32) * rw_out
    o_ref[...] = merged.astype(o_ref.dtype)


def _rw_merge(y_dirs, bonus, g, ln_w, ln_b, seg, segt, gates, s5_out, w_proj, *, tm, tn):
    _, m, rw = y_dirs.shape
    n = w_proj.shape[1]
    nj = n // tn
    full = lambda shape: pl.BlockSpec(shape, lambda i, j: (0,) * len(shape))
    return pl.pallas_call(
        _rw_merge_kernel,
        out_shape=jax.ShapeDtypeStruct((m, n), BF16),
        grid=(m // tm, nj),
        in_specs=[pl.BlockSpec((1, tm, rw), lambda i, j: (0, i, 0)),
                  pl.BlockSpec((1, tm, rw), lambda i, j: (1, i, 0)),
                  pl.BlockSpec((tm, rw), lambda i, j: (i, 0)),
                  pl.BlockSpec((tm, rw), lambda i, j: (i, 0)),
                  full((1, rw)), full((1, rw)), full(seg.shape), full(segt.shape),
                  pl.BlockSpec((tm, tn), lambda i, j: (i, j)),
                  pl.BlockSpec((tm, tn), lambda i, j: (i, j + nj)),
                  pl.BlockSpec((tm, tn), lambda i, j: (i, j)),
                  pl.BlockSpec((rw, tn), lambda i, j: (0, j))],
        out_specs=pl.BlockSpec((tm, tn), lambda i, j: (i, j)),
        scratch_shapes=[pltpu.VMEM((tm, rw), BF16)],
        compiler_params=_cparams(("parallel", "arbitrary")),
        name="rwkv_merge",
    )(y_dirs, y_dirs, bonus, g, ln_w.reshape(1, rw), ln_b.reshape(1, rw), seg, segt,
      gates, gates, s5_out, w_proj)


def _resid_mm_kernel(a_ref, w_ref, x_ref, g_ref, o_ref):
    o_ref[...] = x_ref[...] + g_ref[0] * _dot(a_ref[...], w_ref[...])


def _resid_matmul(a, w, x2, g_tab, mod_row_of_block, *, tm, tn):
    m, k = a.shape
    n = w.shape[1]
    return pl.pallas_call(
        _resid_mm_kernel,
        out_shape=jax.ShapeDtypeStruct((m, n), F32),
        grid=(m // tm, n // tn),
        in_specs=[pl.BlockSpec((tm, k), lambda i, j: (i, 0)),
                  pl.BlockSpec((k, tn), lambda i, j: (0, j)),
                  pl.BlockSpec((tm, tn), lambda i, j: (i, j)),
                  pl.BlockSpec((1, 1, tn), lambda i, j: (mod_row_of_block(i), 0, j))],
        out_specs=pl.BlockSpec((tm, tn), lambda i, j: (i, j)),
        compiler_params=_cparams(("parallel", "arbitrary")),
        name="out_proj",
    )(a, w, x2, g_tab)


def _ffn_down_kernel(a_ref, w_ref, x_ref, g_ref, nf_ref, o_ref, acc_ref):
    kk = pl.program_id(1)

    @pl.when(kk == 0)
    def _():
        acc_ref[...] = jnp.zeros_like(acc_ref)

    acc_ref[...] += _dot(a_ref[...], w_ref[...])

    @pl.when(kk == pl.num_programs(1) - 1)
    def _():
        h = x_ref[...] + g_ref[0] * acc_ref[...]
        ms = jnp.mean(h * h, axis=-1, keepdims=True)
        o_ref[...] = h * lax.rsqrt(ms + NORM_EPS) * nf_ref[...]


def _ffn_down(a, w, x2, g_tab, mod_row_of_block, norm_f, *, tm, tk):
    m, k = a.shape
    n = w.shape[1]
    return pl.pallas_call(
        _ffn_down_kernel,
        out_shape=jax.ShapeDtypeStruct((m, n), F32),
        grid=(m // tm, k // tk),
        in_specs=[pl.BlockSpec((tm, tk), lambda i, kk: (i, kk)),
                  pl.BlockSpec((tk, n), lambda i, kk: (kk, 0)),
                  pl.BlockSpec((tm, n), lambda i, kk: (i, 0)),
                  pl.BlockSpec((1, 1, n), lambda i, kk: (mod_row_of_block(i), 0, 0)),
                  pl.BlockSpec((1, n), lambda i, kk: (0, 0))],
        out_specs=pl.BlockSpec((tm, n), lambda i, kk: (i, 0)),
        scratch_shapes=[pltpu.VMEM((tm, n), F32)],
        compiler_params=_cparams(("parallel", "arbitrary")),
        name="ffn_down",
    )(a, w, x2, g_tab, norm_f.reshape(1, n))


def kernel(x, c, ctx, c_ctx, ada_w, ada_b, norm1_w, w_in, rw_mu, s5_a_re, s5_a_im, s5_log_dt, s5_b_re, s5_b_im, s5_c_re, s5_c_im, s5_d, s5_glu_w, rw_w0, rw_w2, rw_a0, rw_a2, rw_g2, rw_k_k, rw_k_a, rw_r_k, rw_ln_w, rw_ln_b, rw_proj, w_o, norm2_w, ffn_w13, ffn_w2, norm_f):
    assert ada_w.shape[0] == 1, "single-layer block"
    bsz, l_lat, d = x.shape
    l_ctx = ctx.shape[1]
    l_all = l_ctx + l_lat
    s5w = s5_d.shape[1] * s5_d.shape[2]
    rw = rw_g2.shape[2]
    shift_cols = rw_mu.shape[1]
    d_ff = ffn_w2.shape[1]
    n_heads = rw // RW_HEAD

    c_rows = jnp.concatenate([c, c_ctx[None], jnp.zeros((8 - bsz - 1, d), F32)], axis=0)
    mod = _modulation(c_rows, ada_w[0], ada_b[0])
    tab = lambda k: mod[:, k * d:(k + 1) * d].reshape(8, 1, d)
    sh1, sc1, g1, sh2, sc2, g2 = (tab(k) for k in range(N_MOD))
    ctx_row = bsz

    tm_in = 512
    m_lat = bsz * l_lat
    x2 = x.reshape(m_lat, d)
    x_all = jnp.concatenate([x2, ctx.reshape(bsz * l_ctx, d)], axis=0)
    w_in_b = w_in[0].astype(BF16)
    n_mix = s5w + shift_cols
    lat_row = lambda t: (lambda i: i // (l_lat // t))

    def mix_mod_row(i):
        return jnp.where(i >= m_lat // tm_in, ctx_row, i // (l_lat // tm_in))

    u2 = _lnmod_matmul(x_all, norm1_w[0], sh1, sc1, mix_mod_row, w_in_b[:, :s5w],
                       tm=tm_in, tn=s5w, out_dtype=F32, name="in_proj_s5")
    z_rw = _lnmod_matmul(x_all, norm1_w[0], sh1, sc1, mix_mod_row, w_in_b[:, s5w:n_mix],
                         tm=tm_in, tn=shift_cols // 3, out_dtype=F32, name="in_proj_rw")
    tm = 1024
    gates = _lnmod_matmul(x2, norm1_w[0], sh1, sc1, lat_row(tm), w_in_b[:, n_mix:],
                          tm=tm, tn=1024, out_dtype=BF16, epilogue="sigmoid", name="in_proj_gates")

    u_all = jnp.concatenate([u2[m_lat:].reshape(bsz, l_ctx, s5w), u2[:m_lat].reshape(bsz, l_lat, s5w)], axis=1)
    y_s5 = _s5_branch(u_all, l_ctx, s5_a_re[0], s5_a_im[0], s5_log_dt[0], s5_b_re[0], s5_b_im[0],
                      s5_c_re[0], s5_c_im[0], s5_d[0])
    s5_out = _s5_glu(y_s5.reshape(m_lat, s5w), s5_glu_w[0].astype(BF16), tm=tm, tn=1024)

    lora = rw_w2.shape[2]
    zl = jnp.zeros((lora, rw), F32)
    w2bd = jnp.concatenate([jnp.concatenate([rw_w2[0, 0], zl], axis=1),
                            jnp.concatenate([zl, rw_w2[0, 1]], axis=1)], axis=0)
    a2bd = jnp.concatenate([jnp.concatenate([rw_a2[0, 0], zl], axis=1),
                            jnp.concatenate([zl, rw_a2[0, 1]], axis=1)], axis=0)
    head_of = jnp.arange(rw) // RW_HEAD
    seg = (head_of[:, None] == jnp.arange(LANES)[None, :]).astype(BF16)
    segt = seg.T
    r, v, kk, g, bonus, lw, kd, be = _rw_prep(
        z_rw, bsz, l_ctx, l_lat, rw_mu[0], w2bd, a2bd, rw_g2[0], rw_w0[0].reshape(1, 2 * rw),
        rw_a0[0].reshape(1, 2 * rw), rw_k_k[0], rw_k_a[0], rw_r_k[0].reshape(rw), seg, segt)
    y_dirs = _rw_scan(r, v, kk, lw, kd, be, l_ctx)

    merged = _rw_merge(y_dirs.reshape(2, m_lat, rw), bonus.reshape(m_lat, rw), g.reshape(m_lat, rw),
                       rw_ln_w[0], rw_ln_b[0], seg, segt, gates, s5_out, rw_proj[0].astype(BF16),
                       tm=512, tn=1024)
    h1 = _resid_matmul(merged, w_o[0].astype(BF16), x2, g1, lat_row(tm), tm=tm, tn=1024)

    act = _lnmod_swiglu(h1, norm2_w[0], sh2, sc2, lat_row(tm), ffn_w13[0].astype(BF16), d_ff,
                        tm=tm, tn=512, name="ffn_up")
    tm_dn = 512
    out = _ffn_down(act, ffn_w2[0].astype(BF16), h1, g2, lat_row(tm_dn), norm_f, tm=tm_dn, tk=d_ff // 4)
    return out.reshape(bsz, l_lat, d)
```

```python
import functools
import math

import jax
import jax.numpy as jnp
from jax import lax
from jax.experimental import pallas as pl
from jax.experimental.pallas import tpu as pltpu

F32 = jnp.float32
BF16 = jnp.bfloat16
HIGHEST = lax.Precision.HIGHEST

D_MODEL = 2048
N_MOD = 6
NORM_EPS = 1e-6
GN_EPS = 64e-5
GRID_W = 64
S5_GROUP = 16
S5_STATE = 64
S5_CHUNK = 16
RW_HEAD = 64
RW_CHUNK = 64
LANES = 128
VMEM_LIMIT = 48 * 1024 * 1024


def _cparams(sem):
    return pltpu.CompilerParams(dimension_semantics=sem, vmem_limit_bytes=VMEM_LIMIT)


def _operands(a, b, precision):
    if precision == "bf16":
        return a.astype(BF16), b.astype(BF16), None
    return a, b, precision


def _dot(a, b, precision=None):
    a, b, precision = _operands(a, b, precision)
    return jnp.dot(a, b, preferred_element_type=F32, precision=precision)


def _dot_nt(a, b, precision=None):
    a, b, precision = _operands(a, b, precision)
    return lax.dot_general(a, b, (((1,), (1,)), ((), ())), preferred_element_type=F32, precision=precision)


def _dot_hilo(a, ind):
    hi = a.astype(BF16)
    lo = (a - hi.astype(F32)).astype(BF16)
    return _dot(hi, ind) + _dot(lo, ind)


def _sigmoid(x):
    return 1.0 / (1.0 + jnp.exp(-x))


def _silu(x):
    return x * _sigmoid(x)


def _gelu_tanh(x):
    c = math.sqrt(2.0 / math.pi)
    return 0.5 * x * (1.0 + jnp.tanh(c * (x + 0.044715 * (x * x * x))))


def _softplus(x):
    return jnp.maximum(x, 0.0) + jnp.log(1.0 + jnp.exp(-jnp.abs(x)))


def _mod_kernel(c_ref, w_ref, b_ref, o_ref):
    o_ref[...] = _dot(_silu(c_ref[...]), w_ref[...], HIGHEST) + b_ref[...]


def _modulation(c_rows, ada_w, ada_b):
    m, d = c_rows.shape
    n = ada_w.shape[1]
    tn = 1024
    return pl.pallas_call(
        _mod_kernel,
        out_shape=jax.ShapeDtypeStruct((m, n), F32),
        grid=(n // tn,),
        in_specs=[pl.BlockSpec((m, d), lambda j: (0, 0)),
                  pl.BlockSpec((d, tn), lambda j: (0, j)),
                  pl.BlockSpec((1, tn), lambda j: (0, j))],
        out_specs=pl.BlockSpec((m, tn), lambda j: (0, j)),
        compiler_params=_cparams(("arbitrary",)),
        name="modulation",
    )(c_rows, ada_w, ada_b.reshape(1, n))


def _lnmod_rows(x, nw, sh, sc):
    ms = jnp.mean(x * x, axis=-1, keepdims=True)
    y = x * lax.rsqrt(ms + NORM_EPS) * nw
    return y * (1.0 + sc) + sh


def _lnmod_mm_kernel(x_ref, nw_ref, sh_ref, sc_ref, w_ref, o_ref, h_scr, *, epilogue):
    @pl.when(pl.program_id(1) == 0)
    def _():
        h_scr[...] = _lnmod_rows(x_ref[...], nw_ref[...], sh_ref[0], sc_ref[0]).astype(BF16)

    z = _dot(h_scr[...], w_ref[...])
    if epilogue == "sigmoid":
        z = _sigmoid(z)
    if epilogue == "lane_blocks":
        for jb in range(o_ref.shape[0]):
            o_ref[jb] = z[:, jb * LANES:(jb + 1) * LANES].astype(o_ref.dtype)
    else:
        o_ref[...] = z.astype(o_ref.dtype)


def _lnmod_swiglu_kernel(x_ref, nw_ref, sh_ref, sc_ref, w1_ref, w3_ref, o_ref, h_scr):
    @pl.when(pl.program_id(1) == 0)
    def _():
        h_scr[...] = _lnmod_rows(x_ref[...], nw_ref[...], sh_ref[0], sc_ref[0]).astype(BF16)

    h = h_scr[...]
    o_ref[...] = (_silu(_dot(h, w1_ref[...])) * _dot(h, w3_ref[...])).astype(o_ref.dtype)


def _lnmod_matmul(x2, nw, sh_tab, sc_tab, mod_row_of_block, w, *, tm, tn, out_dtype, epilogue=None, name):
    m, d = x2.shape
    n = w.shape[1]
    mod_map = lambda i, j: (mod_row_of_block(i), 0, 0)
    if epilogue == "lane_blocks":
        assert tn == n
        out_shape = jax.ShapeDtypeStruct((n // LANES, m, LANES), out_dtype)
        out_spec = pl.BlockSpec((n // LANES, tm, LANES), lambda i, j: (0, i, 0))
    else:
        out_shape = jax.ShapeDtypeStruct((m, n), out_dtype)
        out_spec = pl.BlockSpec((tm, tn), lambda i, j: (i, j))
    return pl.pallas_call(
        functools.partial(_lnmod_mm_kernel, epilogue=epilogue),
        out_shape=out_shape,
        grid=(m // tm, n // tn),
        in_specs=[pl.BlockSpec((tm, d), lambda i, j: (i, 0)),
                  pl.BlockSpec((1, d), lambda i, j: (0, 0)),
                  pl.BlockSpec((1, 1, d), mod_map),
                  pl.BlockSpec((1, 1, d), mod_map),
                  pl.BlockSpec((d, tn), lambda i, j: (0, j))],
        out_specs=out_spec,
        scratch_shapes=[pltpu.VMEM((tm, d), BF16)],
        compiler_params=_cparams(("parallel", "arbitrary")),
        name=name,
    )(x2, nw.reshape(1, d), sh_tab, sc_tab, w)


def _lnmod_swiglu(x2, nw, sh_tab, sc_tab, mod_row_of_block, w13, d_ff, *, tm, tn, name):
    m, d = x2.shape
    nj = d_ff // tn
    mod_map = lambda i, j: (mod_row_of_block(i), 0, 0)
    return pl.pallas_call(
        _lnmod_swiglu_kernel,
        out_shape=jax.ShapeDtypeStruct((m, d_ff), BF16),
        grid=(m // tm, nj),
        in_specs=[pl.BlockSpec((tm, d), lambda i, j: (i, 0)),
                  pl.BlockSpec((1, d), lambda i, j: (0, 0)),
                  pl.BlockSpec((1, 1, d), mod_map),
                  pl.BlockSpec((1, 1, d), mod_map),
                  pl.BlockSpec((d, tn), lambda i, j: (0, j)),
                  pl.BlockSpec((d, tn), lambda i, j: (0, j + nj))],
        out_specs=pl.BlockSpec((tm, tn), lambda i, j: (i, j)),
        scratch_shapes=[pltpu.VMEM((tm, d), BF16)],
        compiler_params=_cparams(("parallel", "arbitrary")),
        name=name,
    )(x2, nw.reshape(1, d), sh_tab, sc_tab, w13, w13)


def _s5_param_kernel(are_ref, aim_ref, ldt_ref, bre_ref, bim_ref, cre_ref, cim_ref,
                     e_ref, cs_ref, kt_ref, a16_ref):
    t_n, hg, p_n = S5_CHUNK, S5_GROUP, S5_STATE
    tau = lax.broadcasted_iota(jnp.int32, (t_n, 1, p_n), 0).astype(F32)
    for d in range(2):
        a_re = are_ref[0, d:d + 1, :]
        a_im = aim_ref[0, d:d + 1, :]
        dt = jnp.exp(ldt_ref[0, d:d + 1, :])
        lam = a_re * dt
        th = a_im * dt
        er = jnp.exp(lam)
        ab_re = er * jnp.cos(th)
        ab_im = er * jnp.sin(th)
        den = a_re * a_re + a_im * a_im
        x_re = ab_re - 1.0
        co_re = (x_re * a_re + ab_im * a_im) / den
        co_im = (ab_im * a_re - x_re * a_im) / den
        bt_re = bre_ref[0, d]
        bt_im = bim_ref[0, d]
        bb_re = co_re * bt_re - co_im * bt_im
        bb_im = co_re * bt_im + co_im * bt_re
        c_re = cre_ref[0, d]
        c_im = cim_ref[0, d]

        def power(tv):
            mag = jnp.exp(tv * lam)
            return mag * jnp.cos(tv * th), mag * jnp.sin(tv * th)

        pw_re, pw_im = power(tau)
        cp_re = (c_re[None] * pw_re - c_im[None] * pw_im).reshape(t_n * hg, p_n)
        cp_im = (c_re[None] * pw_im + c_im[None] * pw_re).reshape(t_n * hg, p_n)
        kt_ref[0, d] = _dot_nt(bb_re, cp_re, HIGHEST) - _dot_nt(bb_im, cp_im, HIGHEST)

        te = (t_n - 1.0 - tau) if d == 0 else tau
        pe_re, pe_im = power(te)
        e_ref[0, d, :, 0:p_n] = (pe_re * bb_re[None] - pe_im * bb_im[None]).reshape(t_n * hg, p_n)
        e_ref[0, d, :, p_n:2 * p_n] = (pe_re * bb_im[None] + pe_im * bb_re[None]).reshape(t_n * hg, p_n)

        tc = (tau + 1.0) if d == 0 else (t_n - tau)
        pc_re, pc_im = power(tc)
        cs_ref[0, d, :, 0:p_n] = (c_re[None] * pc_re - c_im[None] * pc_im).reshape(t_n * hg, p_n)
        cs_ref[0, d, :, p_n:2 * p_n] = -(c_re[None] * pc_im + c_im[None] * pc_re).reshape(t_n * hg, p_n)

        mag16 = jnp.exp(float(t_n) * lam)
        a16_ref[0, d, 0:1, :] = mag16 * jnp.cos(float(t_n) * th)
        a16_ref[0, d, 1:2, :] = mag16 * jnp.sin(float(t_n) * th)


def _s5_params(a_re, a_im, log_dt, b_re, b_im, c_re, c_im):
    g_n = a_re.shape[1]
    p_n, hg, t_n = S5_STATE, S5_GROUP, S5_CHUNK
    tr = lambda a: jnp.swapaxes(a, 0, 1)
    ldt = jnp.broadcast_to(tr(log_dt)[:, :, None], (g_n, 2, p_n))
    spec3 = pl.BlockSpec((1, 2, p_n), lambda g: (g, 0, 0))
    spec4 = pl.BlockSpec((1, 2, hg, p_n), lambda g: (g, 0, 0, 0))
    th = t_n * hg
    return pl.pallas_call(
        _s5_param_kernel,
        out_shape=(jax.ShapeDtypeStruct((g_n, 2, th, 2 * p_n), F32),
                   jax.ShapeDtypeStruct((g_n, 2, th, 2 * p_n), F32),
                   jax.ShapeDtypeStruct((g_n, 2, hg, th), F32),
                   jax.ShapeDtypeStruct((g_n, 2, 2, p_n), F32)),
        grid=(g_n,),
        in_specs=[spec3, spec3, spec3, spec4, spec4, spec4, spec4],
        out_specs=(pl.BlockSpec((1, 2, th, 2 * p_n), lambda g: (g, 0, 0, 0)),
                   pl.BlockSpec((1, 2, th, 2 * p_n), lambda g: (g, 0, 0, 0)),
                   pl.BlockSpec((1, 2, hg, th), lambda g: (g, 0, 0, 0)),
                   pl.BlockSpec((1, 2, 2, p_n), lambda g: (g, 0, 0, 0))),
        compiler_params=_cparams(("parallel",)),
        name="s5_params",
    )(tr(a_re), tr(a_im), ldt,
      jnp.transpose(b_re, (1, 0, 3, 2)), jnp.transpose(b_im, (1, 0, 3, 2)), tr(c_re), tr(c_im))


def _s5_state_in_kernel(u_ref, e_ref, o_ref):
    e = e_ref[0]
    u = u_ref[0]
    o_ref[0, :, 0:LANES] = _dot(u, e[0], HIGHEST)
    o_ref[0, :, LANES:2 * LANES] = _dot(u, e[1], HIGHEST)


def _s5_state_inputs(u_g, e):
    g_n, rows, th = u_g.shape
    return pl.pallas_call(
        _s5_state_in_kernel,
        out_shape=jax.ShapeDtypeStruct((g_n, rows, 2 * LANES), F32),
        grid=(g_n,),
        in_specs=[pl.BlockSpec((1, rows, th), lambda g: (g, 0, 0)),
                  pl.BlockSpec((1, 2, th, LANES), lambda g: (g, 0, 0, 0))],
        out_specs=pl.BlockSpec((1, rows, 2 * LANES), lambda g: (g, 0, 0)),
        compiler_params=_cparams(("parallel",)),
        name="s5_state_inputs",
    )(u_g, e)


def _s5_scan_kernel(e_ref, a_ref, o_ref, *, n_ctx, n_all):
    zero = jnp.zeros(e_ref.shape[2:], F32)

    def step(pr, pi):
        ar, ai = a_ref[pr], a_ref[pi]

        def body(c, carry):
            sr, si = carry
            o_ref[pr, c] = sr
            o_ref[pi, c] = si
            return (ar * sr - ai * si + e_ref[pr, c], ar * si + ai * sr + e_ref[pi, c])
        return body

    lax.fori_loop(0, n_all, step(0, 1), (zero, zero))
    bwd = step(2, 3)
    carry = lax.fori_loop(0, n_ctx, lambda k, cy: bwd(n_ctx - 1 - k, cy), (zero, zero))
    lax.fori_loop(0, n_all - n_ctx, lambda k, cy: bwd(n_all - 1 - k, cy), carry)


def _s5_scan(e_planes, a_planes, n_ctx, n_all):
    _, rows, r_n, _ = e_planes.shape
    bsz = rows // n_all
    sub = 8
    return pl.pallas_call(
        functools.partial(_s5_scan_kernel, n_ctx=n_ctx, n_all=n_all),
        out_shape=jax.ShapeDtypeStruct(e_planes.shape, F32),
        grid=(bsz, r_n // sub),
        in_specs=[pl.BlockSpec((4, n_all, sub, LANES), lambda b, q: (0, b, q, 0)),
                  pl.BlockSpec((4, sub, LANES), lambda b, q: (0, q, 0))],
        out_specs=pl.BlockSpec((4, n_all, sub, LANES), lambda b, q: (0, b, q, 0)),
        compiler_params=_cparams(("parallel", "parallel")),
        name="s5_scan",
    )(e_planes, a_planes)


def _s5_apply_kernel(u_ref, s_ref, mf_ref, mb_ref, d_ref, cs_ref, o_ref, *, n_ctx):
    u = u_ref[0, n_ctx:, :]
    s = s_ref[0, n_ctx:, :]
    y = _dot(u, mf_ref[0] + mb_ref[0], HIGHEST) + u * d_ref[0]
    y = y + _dot_nt(s[:, 0:LANES], cs_ref[0, 0], HIGHEST) + _dot_nt(s[:, LANES:2 * LANES], cs_ref[0, 1], HIGHEST)
    o_ref[0, 0] = y


def _s5_apply(u_g, s_g, m_f, m_b, d_t, cs, n_ctx, n_all):
    g_n, rows, th = u_g.shape
    bsz = rows // n_all
    n_lat = n_all - n_ctx
    return pl.pallas_call(
        functools.partial(_s5_apply_kernel, n_ctx=n_ctx),
        out_shape=jax.ShapeDtypeStruct((g_n, bsz, n_lat, th), F32),
        grid=(g_n, bsz),
        in_specs=[pl.BlockSpec((1, n_all, th), lambda g, b: (g, b, 0)),
                  pl.BlockSpec((1, n_all, 2 * LANES), lambda g, b: (g, b, 0)),
                  pl.BlockSpec((1, th, th), lambda g, b: (g, 0, 0)),
                  pl.BlockSpec((1, th, th), lambda g, b: (g, 0, 0)),
                  pl.BlockSpec((1, 1, th), lambda g, b: (g, 0, 0)),
                  pl.BlockSpec((1, 2, th, LANES), lambda g, b: (g, 0, 0, 0))],
        out_specs=pl.BlockSpec((1, 1, n_lat, th), lambda g, b: (g, b, 0, 0)),
        compiler_params=_cparams(("parallel", "arbitrary")),
        name="s5_apply",
    )(u_g, s_g, m_f, m_b, d_t, cs)


def _s5_branch(u_all, n_ctx_tok, a_re, a_im, log_dt, b_re, b_im, c_re, c_im, s5_d):
    bsz, l_all, width = u_all.shape
    hg, t_n, p_n = S5_GROUP, S5_CHUNK, S5_STATE
    g_n = width // hg
    n_all = l_all // t_n
    n_ctx = n_ctx_tok // t_n
    th = t_n * hg
    e, cs, kt, a16 = _s5_params(a_re, a_im, log_dt, b_re, b_im, c_re, c_im)

    kt5 = kt.reshape(g_n, 2, hg, t_n, hg)
    ii = jnp.arange(t_n)[:, None]
    jj = jnp.arange(t_n)[None, :]

    def toeplitz(k4, lag, keep):
        m = k4[:, :, jnp.clip(lag, 0, t_n - 1), :]
        m = jnp.where(keep[None, None, :, :, None], m, 0.0)
        return jnp.transpose(m, (0, 2, 1, 3, 4)).reshape(g_n, th, th)

    m_f = toeplitz(kt5[:, 0], jj - ii, jj >= ii)
    m_b = toeplitz(kt5[:, 1], ii - jj, ii >= jj)
    d_t = jnp.tile(s5_d, (1, t_n)).reshape(g_n, 1, th)

    u_g = jnp.transpose(u_all.reshape(bsz, n_all, t_n, g_n, hg), (3, 0, 1, 2, 4)).reshape(g_n, bsz * n_all, th)
    e_cat = e
    s_in = _s5_state_inputs(u_g, e_cat)
    rows = bsz * n_all
    planes = jnp.transpose(s_in.reshape(g_n, rows, 4, p_n), (2, 1, 0, 3)).reshape(4, rows, g_n * p_n // LANES, LANES)
    a_pl = jnp.transpose(a16.reshape(g_n, 4, p_n), (1, 0, 2)).reshape(4, g_n * p_n // LANES, LANES)
    st = _s5_scan(planes, a_pl, n_ctx, n_all)
    s_g = jnp.transpose(st.reshape(4, rows, g_n, p_n), (2, 1, 0, 3)).reshape(g_n, rows, 4 * p_n)
    y_g = _s5_apply(u_g, s_g, m_f, m_b, d_t, cs, n_ctx, n_all)
    n_lat = n_all - n_ctx
    y = jnp.transpose(y_g.reshape(g_n, bsz, n_lat, t_n, hg), (1, 2, 3, 0, 4))
    return y.reshape(bsz, n_lat * t_n, width)


def _s5_ein_kernel(u_ref, w_ref, o_ref):
    o_ref[0] = _dot(u_ref[0].astype(BF16), w_ref[0])


def _s5_chunk_inputs(u_rows, e_big):
    nb, rows, k = u_rows.shape
    return pl.pallas_call(
        _s5_ein_kernel,
        out_shape=jax.ShapeDtypeStruct((nb, rows, k), F32),
        grid=(nb,),
        in_specs=[pl.BlockSpec((1, rows, k), lambda j: (j, 0, 0)),
                  pl.BlockSpec((1, k, k), lambda j: (j, 0, 0))],
        out_specs=pl.BlockSpec((1, rows, k), lambda j: (j, 0, 0)),
        compiler_params=_cparams(("parallel",)),
        name="s5_chunk_inputs",
    )(u_rows, e_big)


def _s5_bscan_kernel(e_ref, a_ref, o_ref, *, bsz, n_ctx, n_lat):
    q = e_ref.shape[2] // 4
    planes = lambda row, d: (row[:, (2 * d) * q:(2 * d + 1) * q], row[:, (2 * d + 1) * q:(2 * d + 2) * q])
    coef = [planes(a_ref[0], d) for d in range(2)]
    ctx0 = bsz * n_lat

    def advance(state, rows):
        new = []
        for (sr, si), (b, d), row in zip(state, [(b, d) for b in range(bsz) for d in range(2)], rows):
            ar, ai = coef[d]
            er, ei = planes(e_ref[0, pl.ds(row, 1), :], d)
            new.append((ar * sr - ai * si + er, ar * si + ai * sr + ei))
        return tuple(new)

    def ctx_step(s, state):
        rows = [ctx0 + b * n_ctx + (s if d == 0 else n_ctx - 1 - s) for b in range(bsz) for d in range(2)]
        return advance(state, rows)

    def lat_step(s, state):
        rows = [b * n_lat + (s if d == 0 else n_lat - 1 - s) for b in range(bsz) for d in range(2)]
        for (sr, si), (b, d), row in zip(state, [(b, d) for b in range(bsz) for d in range(2)], rows):
            o_ref[0, pl.ds(row, 1), (2 * d) * q:(2 * d + 1) * q] = sr
            o_ref[0, pl.ds(row, 1), (2 * d + 1) * q:(2 * d + 2) * q] = si
        return advance(state, rows)

    zero = jnp.zeros((1, q), F32)
    state = tuple((zero, zero) for _ in range(2 * bsz))
    state = lax.fori_loop(0, n_ctx, ctx_step, state)
    lax.fori_loop(0, n_lat, lat_step, state)


def _s5_bscan(e_rows, a_rows, bsz, n_ctx, n_lat):
    nb, rows, k = e_rows.shape
    return pl.pallas_call(
        functools.partial(_s5_bscan_kernel, bsz=bsz, n_ctx=n_ctx, n_lat=n_lat),
        out_shape=jax.ShapeDtypeStruct((nb, bsz * n_lat, k), F32),
        grid=(nb,),
        in_specs=[pl.BlockSpec((1, rows, k), lambda j: (j, 0, 0)),
                  pl.BlockSpec((1, 1, k), lambda j: (j, 0, 0))],
        out_specs=pl.BlockSpec((1, bsz * n_lat, k), lambda j: (j, 0, 0)),
        compiler_params=_cparams(("parallel",)),
        name="s5_scan",
    )(e_rows, a_rows)


def _s5_out_kernel(u_ref, s_ref, m_ref, cs_ref, d_ref, o_ref):
    u = u_ref[0]
    y = _dot(u.astype(BF16), m_ref[0]) + _dot(s_ref[0].astype(BF16), cs_ref[0])
    o_ref[0] = y + u * d_ref[0]


def _s5_outputs(u_rows, s_rows, m_big, cs_big, d_rows):
    nb, _, k = u_rows.shape
    rows = s_rows.shape[1]
    tr = rows // 2
    return pl.pallas_call(
        _s5_out_kernel,
        out_shape=jax.ShapeDtypeStruct((nb, rows, k), F32),
        grid=(nb, rows // tr),
        in_specs=[pl.BlockSpec((1, tr, k), lambda j, i: (j, i, 0)),
                  pl.BlockSpec((1, tr, k), lambda j, i: (j, i, 0)),
                  pl.BlockSpec((1, k, k), lambda j, i: (j, 0, 0)),
                  pl.BlockSpec((1, k, k), lambda j, i: (j, 0, 0)),
                  pl.BlockSpec((1, 1, k), lambda j, i: (j, 0, 0))],
        out_specs=pl.BlockSpec((1, tr, k), lambda j, i: (j, i, 0)),
        compiler_params=_cparams(("parallel", "arbitrary")),
        name="s5_outputs",
    )(u_rows, s_rows, m_big, cs_big, d_rows)


def _s5_branch_blocked(u_blk, bsz, l_ctx, l_lat, a_re, a_im, log_dt, b_re, b_im, c_re, c_im, s5_d):
    nb, m_all, _ = u_blk.shape
    hg, t_n, p_n = S5_GROUP, S5_CHUNK, S5_STATE
    g_n = a_re.shape[1]
    gl_n = g_n // nb
    k = t_n * LANES
    n_lat = l_lat // t_n
    n_ctx = l_ctx // t_n
    e, cs, kt, a16 = _s5_params(a_re, a_im, log_dt, b_re, b_im, c_re, c_im)
    eye = jnp.eye(gl_n, dtype=F32)

    kt5 = kt.reshape(g_n, 2, hg, t_n, hg)
    ii = jnp.arange(t_n)[:, None]
    jj = jnp.arange(t_n)[None, :]

    def toeplitz(k4, lag, keep):
        m = k4[:, :, jnp.clip(lag, 0, t_n - 1), :]
        m = jnp.where(keep[None, None, :, :, None], m, 0.0)
        return jnp.transpose(m, (0, 2, 1, 3, 4))

    toep = toeplitz(kt5[:, 0], jj - ii, jj >= ii) + toeplitz(kt5[:, 1], ii - jj, ii >= jj)
    toep = jnp.transpose(toep.reshape(nb, gl_n, t_n, hg, t_n, hg), (0, 2, 1, 3, 4, 5))
    m_big = (toep[:, :, :, :, :, None, :] * eye[None, None, :, None, None, :, None]).reshape(nb, k, k).astype(BF16)

    e7 = jnp.transpose(e.reshape(nb, gl_n, 2, t_n, hg, 2, p_n), (0, 3, 1, 4, 2, 5, 6))
    e_big = (e7[..., None, :] * eye[None, None, :, None, None, None, :, None]).reshape(nb, k, k).astype(BF16)

    c7 = jnp.transpose(cs.reshape(nb, gl_n, 2, t_n, hg, 2, p_n), (0, 2, 5, 1, 6, 3, 4))
    cs_big = (c7[..., None, :] * eye[None, None, None, :, None, None, :, None]).reshape(nb, k, k).astype(BF16)

    d_rows = jnp.tile(s5_d.reshape(nb, 1, gl_n * hg), (1, 1, t_n))
    a_rows = jnp.transpose(a16.reshape(nb, gl_n, 2, 2, p_n), (0, 2, 3, 1, 4)).reshape(nb, 1, 4 * gl_n * p_n)

    u_rows = u_blk.reshape(nb, m_all // t_n, k)
    e_rows = _s5_chunk_inputs(u_rows, e_big)
    s_rows = _s5_bscan(e_rows, a_rows, bsz, n_ctx, n_lat)
    y_rows = _s5_outputs(u_rows, s_rows, m_big, cs_big, d_rows)
    return y_rows.reshape(nb, bsz * l_lat, LANES)


def _rw_prep_kernel(z_ref, zp_ref, zn_ref, mu_ref, w2_ref, a2_ref, g2_ref, w0_ref, a0_ref,
                    kk_w_ref, ka_ref, rk_ref, seg_ref, segt_ref,
                    r_ref, v_ref, kk_ref, g_ref, bonus_ref, lw_ref, kd_ref, be_ref,
                    *, tm, l_lat, rw):
    j = pl.program_id(1)
    z = z_ref[...]
    lat = j > 0
    tl = lax.broadcasted_iota(jnp.int32, (tm, 1), 0)
    tok = (j - 1) * tm + tl
    col = tl % GRID_W
    m_l = jnp.where(lat, col, tl) > 0
    m_r = jnp.where(lat, col - (GRID_W - 1), tl - (tm - 1)) < 0
    m_u = jnp.logical_and(lat, tok >= GRID_W)
    m_d = jnp.logical_and(lat, tok < l_lat - GRID_W)
    z_l = pltpu.roll(z, 1, 0)
    z_r = pltpu.roll(z, tm - 1, 0)
    z_u = jnp.concatenate([zp_ref[...], z[:tm - GRID_W]], axis=0)
    z_d = jnp.concatenate([z[GRID_W:], zn_ref[...]], axis=0)
    s = (jnp.where(m_l, z_l, 0.0) + jnp.where(m_r, z_r, 0.0)
         + jnp.where(m_u, z_u, 0.0) + jnp.where(m_d, z_d, 0.0))
    cnt = (m_l.astype(F32) + m_r.astype(F32)) + (m_u.astype(F32) + m_d.astype(F32))
    zs = z + (s / cnt - z) * mu_ref[...]

    r = zs[:, 0:rw]
    k = zs[:, rw:2 * rw]
    v = zs[:, 2 * rw:3 * rw]
    o = 3 * rw
    wd = zs[:, o:o + LANES]
    ad = zs[:, o + LANES:o + 2 * LANES]
    gd = zs[:, o + 2 * LANES:o + 3 * LANES]

    seg = seg_ref[...]
    segt = segt_ref[...]

    def head_sum(t):
        return _dot_hilo(_dot_hilo(t, seg), segt)

    g_ref[0] = _dot(_sigmoid(gd), g2_ref[...], "bf16")
    kk = k * kk_w_ref[...]
    kk = kk * lax.rsqrt(head_sum(kk * kk) + 1e-12)
    wl = w0_ref[...] + _dot(jnp.tanh(wd), w2_ref[...], "bf16")
    al = a0_ref[...] + _dot(ad, a2_ref[...], "bf16")
    r_ref[0] = r
    v_ref[0] = v
    kk_ref[0] = kk
    coef = jnp.zeros_like(r)
    for d in range(2):
        w_raw = -_softplus(-wl[:, d * rw:(d + 1) * rw]) - 0.5
        a = _sigmoid(al[:, d * rw:(d + 1) * rw])
        k_d = k * (1.0 + (a - 1.0) * ka_ref[...])
        lw_ref[d, 0] = -jnp.exp(w_raw)
        kd_ref[d, 0] = k_d
        be_ref[d, 0] = kk * a
        coef = coef + head_sum(r * k_d * rk_ref[...])
    bonus_ref[0] = coef * v


def _rw_prep(z_rw, bsz, l_ctx, l_lat, mu, w2bd, a2bd, g2, w0cat, a0cat, k_k, k_a, r_k_flat, seg, segt):
    cols = z_rw.shape[1]
    tm = l_ctx
    rw = g2.shape[1]
    l_all = l_ctx + l_lat
    nblk = l_all // tm
    lat_blk = l_lat // tm
    hb = tm // GRID_W
    lat_hblk = l_lat // GRID_W

    def main_blk(b, j):
        return jnp.where(j == 0, bsz * lat_blk + b, b * lat_blk + j - 1)

    def prev_halo(b, j):
        return b * lat_hblk + jnp.maximum((j - 1) * hb - 1, 0)

    def next_halo(b, j):
        return b * lat_hblk + jnp.minimum(jnp.maximum(j, 1) * hb, lat_hblk - 1)

    full = lambda shape: pl.BlockSpec(shape, lambda b, j: (0,) * len(shape))
    shared = jax.ShapeDtypeStruct((bsz, l_all, rw), F32)
    lat_only = jax.ShapeDtypeStruct((bsz, l_lat, rw), F32)
    per_dir = jax.ShapeDtypeStruct((2, bsz, l_all, rw), F32)
    o_shared = pl.BlockSpec((1, tm, rw), lambda b, j: (b, j, 0))
    o_lat = pl.BlockSpec((1, tm, rw), lambda b, j: (b, jnp.maximum(j - 1, 0), 0))
    o_dir = pl.BlockSpec((2, 1, tm, rw), lambda b, j: (0, b, j, 0))
    return pl.pallas_call(
        functools.partial(_rw_prep_kernel, tm=tm, l_lat=l_lat, rw=rw),
        out_shape=(shared,) * 3 + (lat_only,) * 2 + (per_dir,) * 3,
        grid=(bsz, nblk),
        in_specs=[pl.BlockSpec((tm, cols), lambda b, j: (main_blk(b, j), 0)),
                  pl.BlockSpec((GRID_W, cols), lambda b, j: (prev_halo(b, j), 0)),
                  pl.BlockSpec((GRID_W, cols), lambda b, j: (next_halo(b, j), 0)),
                  full((1, cols)), full(w2bd.shape), full(a2bd.shape), full(g2.shape),
                  full((1, 2 * rw)), full((1, 2 * rw)), full((1, rw)), full((1, rw)), full((1, rw)),
                  full(seg.shape), full(segt.shape)],
        out_specs=(o_shared,) * 3 + (o_lat,) * 2 + (o_dir,) * 3,
        compiler_params=_cparams(("parallel", "arbitrary")),
        name="rwkv_prep",
    )(z_rw, z_rw, z_rw, mu.reshape(1, cols), w2bd, a2bd, g2, w0cat, a0cat,
      k_k.reshape(1, rw), k_a.reshape(1, rw), r_k_flat.reshape(1, rw), seg, segt)


def _stack_heads(x, head0):
    return jnp.concatenate([jnp.where(head0, x, 0.0), jnp.where(head0, 0.0, x)], axis=0)


def _rw_chunk_kernel(r_ref, v_ref, kk_ref, lw_ref, kd_ref, be_ref, y_ref,
                     z_scr, lhs_s, kb_s, kht_s, bht_s, vs_s, rt_s, aab_s, akr_s, arb_s, t_s, x_s, xin_s, yc_s, wu_s,
                     *, n_pairs):
    c_n = RW_CHUNK
    n2 = 2 * c_n
    rev = (pl.program_id(0) % 2) == 1

    @pl.when(pl.program_id(1) == 0)
    def _():
        z_scr[...] = jnp.zeros_like(z_scr)

    ri = lax.broadcasted_iota(jnp.int32, (c_n, c_n), 0)
    ci = lax.broadcasted_iota(jnp.int32, (c_n, c_n), 1)
    tri = (jnp.where(rev, ci - ri, ri - ci) >= 0).astype(F32)
    r2 = lax.broadcasted_iota(jnp.int32, (n2, n2), 0)
    c2 = lax.broadcasted_iota(jnp.int32, (n2, n2), 1)
    t2 = r2 % c_n
    i2 = c2 % c_n
    same_head = (r2 // c_n) == (c2 // c_n)
    before = jnp.logical_and(same_head, jnp.where(rev, i2 - t2, t2 - i2) > 0)
    upto = jnp.logical_or(before, r2 == c2)
    eye = (r2 == c2).astype(F32)
    head0 = lax.broadcasted_iota(jnp.int32, (1, LANES), 1) < RW_HEAD

    def blk(s):
        return (r2 // s) == (c2 // s)

    pairs = range(n_pairs)
    lanes_of = lambda p: slice(p * LANES, (p + 1) * LANES)

    lw = lw_ref[0, 0]
    cum = _dot(tri, lw, HIGHEST)
    tot = jnp.sum(lw, axis=0, keepdims=True)
    g_inv = jnp.exp(-cum)
    g_hat = jnp.exp(tot - cum)
    kd = kd_ref[0, 0]
    be = be_ref[0, 0]
    at = kk_ref[0] * jnp.exp(cum - lw)
    rt = r_ref[0] * jnp.exp(cum)
    kt = kd * g_inv
    bt = be * g_inv
    kh = kd * g_hat
    bh = be * g_hat
    g_c = jnp.exp(tot)
    v = v_ref[0]
    for p in pairs:
        st = lambda x: _stack_heads(x[:, lanes_of(p)], head0)
        rt_p = st(rt)
        at_p = st(at).astype(BF16)
        lhs_s[p, :n2] = at_p
        lhs_s[p, n2:] = rt_p.astype(BF16)
        xin_s[p, :, :LANES] = at_p
        rt_s[p] = rt_p
        kb_s[p, :n2] = st(kt).astype(BF16)
        kb_s[p, n2:] = st(bt).astype(BF16)
        kht_s[p] = st(kh).T.astype(BF16)
        bht_s[p] = st(bh).T.astype(BF16)
        vs_s[p] = st(v).astype(BF16)

    for p in pairs:
        g = _dot_nt(lhs_s[p], kb_s[p])
        a_ab = jnp.where(before, g[:n2, n2:], 0.0)
        akr_s[p, :n2] = jnp.where(before, g[:n2, :n2], 0.0).astype(BF16)
        akr_s[p, n2:] = jnp.where(upto, g[n2:, :n2], 0.0).astype(BF16)
        arb_s[p] = jnp.where(upto, g[n2:, n2:], 0.0).astype(BF16)
        aab_s[p] = a_ab
        t_s[p] = eye - jnp.where(blk(2), a_ab, 0.0)

    s = 2
    while s < c_n:
        off = jnp.logical_and(blk(2 * s), jnp.logical_not(blk(s)))
        for p in pairs:
            x_s[p] = _dot(t_s[p].astype(BF16), jnp.where(off, aab_s[p], 0.0).astype(BF16)).astype(BF16)
        for p in pairs:
            t = t_s[p]
            t_s[p] = t - _dot(x_s[p], t.astype(BF16))
        s *= 2

    for p in pairs:
        av = _dot(akr_s[p], vs_s[p])
        xin_s[p, :, LANES:] = av[:n2].astype(BF16)
        yc_s[p] = av[n2:]

    for p in pairs:
        wu_s[p] = _dot(t_s[p].astype(BF16), xin_s[p]).astype(BF16)

    for p in pairs:
        wu = wu_s[p]
        q = _dot(arb_s[p], wu)
        bwu = _dot(bht_s[p], wu)
        kv = _dot(kht_s[p], vs_s[p])
        z0 = z_scr[p].astype(BF16)
        y_p = _dot((rt_s[p] - q[:, :LANES]).astype(BF16), z0) + (yc_s[p] - q[:, LANES:])
        y_ref[0, 0, :, lanes_of(p)] = y_p[:c_n] + y_p[c_n:]
        m_z = eye * g_c[:, lanes_of(p)] - bwu[:, :LANES]
        z_scr[p] = _dot(m_z.astype(BF16), z0) + (kv - bwu[:, LANES:])


def _rw_scan(r, v, kk, lw, kd, be, l_ctx):
    bsz, l_all, rw = r.shape
    c_n = RW_CHUNK
    n_all = l_all // c_n
    n_ctx = l_ctx // c_n
    n_lat = n_all - n_ctx
    n_pairs = rw // LANES
    n2 = 2 * c_n
    vm = lambda rows, cols, dt: pltpu.VMEM((n_pairs, rows, cols), dt)

    def chunk_of(bd, s):
        fwd = s
        bwd = jnp.where(s < n_ctx, n_ctx - 1 - s, n_all + n_ctx - 1 - s)
        return jnp.where(bd % 2 == 0, fwd, bwd)

    def out_chunk(bd, s):
        c = chunk_of(bd, s)
        edge = jnp.where(bd % 2 == 0, 0, n_lat - 1)
        return jnp.where(s < n_ctx, edge, c - n_ctx)

    shared = pl.BlockSpec((1, c_n, rw), lambda bd, s: (bd // 2, chunk_of(bd, s), 0))
    per_dir = pl.BlockSpec((1, 1, c_n, rw), lambda bd, s: (bd % 2, bd // 2, chunk_of(bd, s), 0))
    return pl.pallas_call(
        functools.partial(_rw_chunk_kernel, n_pairs=n_pairs),
        out_shape=jax.ShapeDtypeStruct((2, bsz, n_lat * c_n, rw), F32),
        grid=(2 * bsz, n_all),
        in_specs=[shared, shared, shared, per_dir, per_dir, per_dir],
        out_specs=pl.BlockSpec((1, 1, c_n, rw), lambda bd, s: (bd % 2, bd // 2, out_chunk(bd, s), 0)),
        scratch_shapes=[vm(n2, n2, F32),
                        vm(2 * n2, n2, BF16),
                        vm(2 * n2, n2, BF16),
                        vm(n2, n2, BF16),
                        vm(n2, n2, BF16),
                        vm(n2, n2, BF16),
                        vm(n2, n2, F32),
                        vm(n2, n2, F32),
                        vm(2 * n2, n2, BF16),
                        vm(n2, n2, BF16),
                        vm(n2, n2, F32),
                        vm(n2, n2, BF16),
                        vm(n2, 2 * n2, BF16),
                        vm(n2, n2, F32),
                        vm(n2, 2 * n2, BF16)],
        compiler_params=_cparams(("parallel", "arbitrary")),
        name="rwkv_scan",
    )(r, v, kk, lw, kd, be)


def _s5_glu_kernel(y_ref, wa_ref, wb_ref, o_ref, h_scr):
    @pl.when(pl.program_id(1) == 0)
    def _():
        for jb in range(y_ref.shape[0]):
            h_scr[:, jb * LANES:(jb + 1) * LANES] = _gelu_tanh(y_ref[jb]).astype(BF16)

    h = h_scr[...]
    o_ref[...] = (_dot(h, wa_ref[...]) * _sigmoid(_dot(h, wb_ref[...]))).astype(o_ref.dtype)


def _s5_glu(y_blk, w, *, tm, tn):
    nb, m, _ = y_blk.shape
    k = nb * LANES
    n = w.shape[1] // 2
    nj = n // tn
    return pl.pallas_call(
        _s5_glu_kernel,
        out_shape=jax.ShapeDtypeStruct((m, n), BF16),
        grid=(m // tm, nj),
        in_specs=[pl.BlockSpec((nb, tm, LANES), lambda i, j: (0, i, 0)),
                  pl.BlockSpec((k, tn), lambda i, j: (0, j)),
                  pl.BlockSpec((k, tn), lambda i, j: (0, j + nj))],
        out_specs=pl.BlockSpec((tm, tn), lambda i, j: (i, j)),
        scratch_shapes=[pltpu.VMEM((tm, k), BF16)],
        compiler_params=_cparams(("parallel", "arbitrary")),
        name="s5_glu",
    )(y_blk, w, w)


def _rw_merge_kernel(yf_ref, yb_ref, bonus_ref, g_ref, lnw_ref, lnb_ref, seg_ref, segt_ref,
                     ga_ref, gb_ref, s5_ref, w_ref, o_ref, h_scr):
    @pl.when(pl.program_id(1) == 0)
    def _():
        seg = seg_ref[...]
        segt = segt_ref[...]
        inv_n = 1.0 / RW_HEAD

        def head_mean(t):
            return _dot_hilo(_dot_hilo(t, seg), segt) * inv_n

        y = yf_ref[0] + yb_ref[0]
        dy = y - head_mean(y)
        var = head_mean(dy * dy)
        y = dy * lax.rsqrt(var + GN_EPS) * lnw_ref[...] + lnb_ref[...] + bonus_ref[...]
        h_scr[...] = (y * g_ref[...]).astype(BF16)

    rw_out = _dot(h_scr[...], w_ref[...])
    merged = ga_ref[...].astype(F32) * s5_ref[...].astype(F32) + gb_ref[...].astype(F32) * rw_out
    o_ref[...] = merged.astype(o_ref.dtype)


def _rw_merge(y_dirs, bonus, g, ln_w, ln_b, seg, segt, gates, s5_out, w_proj, *, tm, tn):
    _, m, rw = y_dirs.shape
    n = w_proj.shape[1]
    nj = n // tn
    full = lambda shape: pl.BlockSpec(shape, lambda i, j: (0,) * len(shape))
    return pl.pallas_call(
        _rw_merge_kernel,
        out_shape=jax.ShapeDtypeStruct((m, n), BF16),
        grid=(m // tm, nj),
        in_specs=[pl.BlockSpec((1, tm, rw), lambda i, j: (0, i, 0)),
                  pl.BlockSpec((1, tm, rw), lambda i, j: (1, i, 0)),
                  pl.BlockSpec((tm, rw), lambda i, j: (i, 0)),
                  pl.BlockSpec((tm, rw), lambda i, j: (i, 0)),
                  full((1, rw)), full((1, rw)), full(seg.shape), full(segt.shape),
                  pl.BlockSpec((tm, tn), lambda i, j: (i, j)),
                  pl.BlockSpec((tm, tn), lambda i, j: (i, j + nj)),
                  pl.BlockSpec((tm, tn), lambda i, j: (i, j)),
                  pl.BlockSpec((rw, tn), lambda i, j: (0, j))],
        out_specs=pl.BlockSpec((tm, tn), lambda i, j: (i, j)),
        scratch_shapes=[pltpu.VMEM((tm, rw), BF16)],
        compiler_params=_cparams(("parallel", "arbitrary")),
        name="rwkv_merge",
    )(y_dirs, y_dirs, bonus, g, ln_w.reshape(1, rw), ln_b.reshape(1, rw), seg, segt,
      gates, gates, s5_out, w_proj)


def _resid_mm_kernel(a_ref, w_ref, x_ref, g_ref, o_ref):
    o_ref[...] = x_ref[...] + g_ref[0] * _dot(a_ref[...], w_ref[...])


def _resid_matmul(a, w, x2, g_tab, mod_row_of_block, *, tm, tn):
    m, k = a.shape
    n = w.shape[1]
    return pl.pallas_call(
        _resid_mm_kernel,
        out_shape=jax.ShapeDtypeStruct((m, n), F32),
        grid=(m // tm, n // tn),
        in_specs=[pl.BlockSpec((tm, k), lambda i, j: (i, 0)),
                  pl.BlockSpec((k, tn), lambda i, j: (0, j)),
                  pl.BlockSpec((tm, tn), lambda i, j: (i, j)),
                  pl.BlockSpec((1, 1, tn), lambda i, j: (mod_row_of_block(i), 0, j))],
        out_specs=pl.BlockSpec((tm, tn), lambda i, j: (i, j)),
        compiler_params=_cparams(("parallel", "arbitrary")),
        name="out_proj",
    )(a, w, x2, g_tab)


def _ffn_down_kernel(a_ref, w_ref, x_ref, g_ref, nf_ref, o_ref, acc_ref):
    kk = pl.program_id(1)

    @pl.when(kk == 0)
    def _():
        acc_ref[...] = jnp.zeros_like(acc_ref)

    acc_ref[...] += _dot(a_ref[...], w_ref[...])

    @pl.when(kk == pl.num_programs(1) - 1)
    def _():
        h = x_ref[...] + g_ref[0] * acc_ref[...]
        ms = jnp.mean(h * h, axis=-1, keepdims=True)
        o_ref[...] = h * lax.rsqrt(ms + NORM_EPS) * nf_ref[...]


def _ffn_down(a, w, x2, g_tab, mod_row_of_block, norm_f, *, tm, tk):
    m, k = a.shape
    n = w.shape[1]
    return pl.pallas_call(
        _ffn_down_kernel,
        out_shape=jax.ShapeDtypeStruct((m, n), F32),
        grid=(m // tm, k // tk),
        in_specs=[pl.BlockSpec((tm, tk), lambda i, kk: (i, kk)),
                  pl.BlockSpec((tk, n), lambda i, kk: (kk, 0)),
                  pl.BlockSpec((tm, n), lambda i, kk: (i, 0)),
                  pl.BlockSpec((1, 1, n), lambda i, kk: (mod_row_of_block(i), 0, 0)),
                  pl.BlockSpec((1, n), lambda i, kk: (0, 0))],
        out_specs=pl.BlockSpec((tm, n), lambda i, kk: (i, 0)),
        scratch_shapes=[pltpu.VMEM((tm, n), F32)],
        compiler_params=_cparams(("parallel", "arbitrary")),
        name="ffn_down",
    )(a, w, x2, g_tab, norm_f.reshape(1, n))


def kernel(x, c, ctx, c_ctx, ada_w, ada_b, norm1_w, w_in, rw_mu, s5_a_re, s5_a_im, s5_log_dt, s5_b_re, s5_b_im, s5_c_re, s5_c_im, s5_d, s5_glu_w, rw_w0, rw_w2, rw_a0, rw_a2, rw_g2, rw_k_k, rw_k_a, rw_r_k, rw_ln_w, rw_ln_b, rw_proj, w_o, norm2_w, ffn_w13, ffn_w2, norm_f):
    assert ada_w.shape[0] == 1, "single-layer block"
    bsz, l_lat, d = x.shape
    l_ctx = ctx.shape[1]
    l_all = l_ctx + l_lat
    s5w = s5_d.shape[1] * s5_d.shape[2]
    rw = rw_g2.shape[2]
    shift_cols = rw_mu.shape[1]
    d_ff = ffn_w2.shape[1]
    n_heads = rw // RW_HEAD

    c_rows = jnp.concatenate([c, c_ctx[None], jnp.zeros((8 - bsz - 1, d), F32)], axis=0)
    mod = _modulation(c_rows, ada_w[0], ada_b[0])
    tab = lambda k: mod[:, k * d:(k + 1) * d].reshape(8, 1, d)
    sh1, sc1, g1, sh2, sc2, g2 = (tab(k) for k in range(N_MOD))
    ctx_row = bsz

    tm_in = 512
    m_lat = bsz * l_lat
    x2 = x.reshape(m_lat, d)
    x_all = jnp.concatenate([x2, ctx.reshape(bsz * l_ctx, d)], axis=0)
    w_in_b = w_in[0].astype(BF16)
    n_mix = s5w + shift_cols
    lat_row = lambda t: (lambda i: i // (l_lat // t))

    def mix_mod_row(i):
        return jnp.where(i >= m_lat // tm_in, ctx_row, i // (l_lat // tm_in))

    u_blk = _lnmod_matmul(x_all, norm1_w[0], sh1, sc1, mix_mod_row, w_in_b[:, :s5w],
                          tm=tm_in, tn=s5w, out_dtype=F32, epilogue="lane_blocks", name="in_proj_s5")
    z_rw = _lnmod_matmul(x_all, norm1_w[0], sh1, sc1, mix_mod_row, w_in_b[:, s5w:n_mix],
                         tm=tm_in, tn=shift_cols // 3, out_dtype=F32, name="in_proj_rw")
    tm = 1024
    gates = _lnmod_matmul(x2, norm1_w[0], sh1, sc1, lat_row(tm), w_in_b[:, n_mix:],
                          tm=tm, tn=1024, out_dtype=BF16, epilogue="sigmoid", name="in_proj_gates")

    y_blk = _s5_branch_blocked(u_blk, bsz, l_ctx, l_lat, s5_a_re[0], s5_a_im[0], s5_log_dt[0],
                               s5_b_re[0], s5_b_im[0], s5_c_re[0], s5_c_im[0], s5_d[0])
    s5_out = _s5_glu(y_blk, s5_glu_w[0].astype(BF16), tm=tm, tn=1024)

    lora = rw_w2.shape[2]
    zl = jnp.zeros((lora, rw), F32)
    w2bd = jnp.concatenate([jnp.concatenate([rw_w2[0, 0], zl], axis=1),
                            jnp.concatenate([zl, rw_w2[0, 1]], axis=1)], axis=0)
    a2bd = jnp.concatenate([jnp.concatenate([rw_a2[0, 0], zl], axis=1),
                            jnp.concatenate([zl, rw_a2[0, 1]], axis=1)], axis=0)
    head_of = jnp.arange(rw) // RW_HEAD
    seg = (head_of[:, None] == jnp.arange(LANES)[None, :]).astype(BF16)
    segt = seg.T
    r, v, kk, g, bonus, lw, kd, be = _rw_prep(
        z_rw, bsz, l_ctx, l_lat, rw_mu[0], w2bd, a2bd, rw_g2[0], rw_w0[0].reshape(1, 2 * rw),
        rw_a0[0].reshape(1, 2 * rw), rw_k_k[0], rw_k_a[0], rw_r_k[0].reshape(rw), seg, segt)
    y_dirs = _rw_scan(r, v, kk, lw, kd, be, l_ctx)

    merged = _rw_merge(y_dirs.reshape(2, m_lat, rw), bonus.reshape(m_lat, rw), g.reshape(m_lat, rw),
                       rw_ln_w[0], rw_ln_b[0], seg, segt, gates, s5_out, rw_proj[0].astype(BF16),
                       tm=512, tn=1024)
    h1 = _resid_matmul(merged, w_o[0].astype(BF16), x2, g1, lat_row(tm), tm=tm, tn=1024)

    act = _lnmod_swiglu(h1, norm2_w[0], sh2, sc2, lat_row(tm), ffn_w13[0].astype(BF16), d_ff,
                        tm=tm, tn=512, name="ffn_up")
    tm_dn = 512
    out = _ffn_down(act, ffn_w2[0].astype(BF16), h1, g2, lat_row(tm_dn), norm_f, tm=tm_dn, tk=d_ff // 4)
    return out.reshape(bsz, l_lat, d)
```

```python
import functools
import math

import jax
import jax.numpy as jnp
from jax import lax
from jax.experimental import pallas as pl
from jax.experimental.pallas import tpu as pltpu

F32 = jnp.float32
BF16 = jnp.bfloat16
HIGHEST = lax.Precision.HIGHEST

D_MODEL = 2048
N_MOD = 6
NORM_EPS = 1e-6
GN_EPS = 64e-5
GRID_W = 64
S5_GROUP = 16
S5_STATE = 64
S5_CHUNK = 16
RW_HEAD = 64
RW_CHUNK = 64
LANES = 128
VMEM_LIMIT = 48 * 1024 * 1024


def _cparams(sem):
    return pltpu.CompilerParams(dimension_semantics=sem, vmem_limit_bytes=VMEM_LIMIT)


def _operands(a, b, precision):
    if precision == "bf16":
        return a.astype(BF16), b.astype(BF16), None
    return a, b, precision


def _dot(a, b, precision=None):
    a, b, precision = _operands(a, b, precision)
    return jnp.dot(a, b, preferred_element_type=F32, precision=precision)


def _dot_nt(a, b, precision=None):
    a, b, precision = _operands(a, b, precision)
    return lax.dot_general(a, b, (((1,), (1,)), ((), ())), preferred_element_type=F32, precision=precision)


def _dot_hilo(a, ind):
    hi = a.astype(BF16)
    lo = (a - hi.astype(F32)).astype(BF16)
    return _dot(hi, ind) + _dot(lo, ind)


def _sigmoid(x):
    return 1.0 / (1.0 + jnp.exp(-x))


def _silu(x):
    return x * _sigmoid(x)


def _gelu_tanh(x):
    c = math.sqrt(2.0 / math.pi)
    return 0.5 * x * (1.0 + jnp.tanh(c * (x + 0.044715 * (x * x * x))))


def _softplus(x):
    return jnp.maximum(x, 0.0) + jnp.log(1.0 + jnp.exp(-jnp.abs(x)))


def _mod_kernel(c_ref, w_ref, b_ref, o_ref):
    o_ref[...] = _dot(_silu(c_ref[...]), w_ref[...], HIGHEST) + b_ref[...]


def _modulation(c_rows, ada_w, ada_b):
    m, d = c_rows.shape
    n = ada_w.shape[1]
    tn = 1024
    return pl.pallas_call(
        _mod_kernel,
        out_shape=jax.ShapeDtypeStruct((m, n), F32),
        grid=(n // tn,),
        in_specs=[pl.BlockSpec((m, d), lambda j: (0, 0)),
                  pl.BlockSpec((d, tn), lambda j: (0, j)),
                  pl.BlockSpec((1, tn), lambda j: (0, j))],
        out_specs=pl.BlockSpec((m, tn), lambda j: (0, j)),
        compiler_params=_cparams(("arbitrary",)),
        name="modulation",
    )(c_rows, ada_w, ada_b.reshape(1, n))


def _lnmod_rows(x, nw, sh, sc):
    ms = jnp.mean(x * x, axis=-1, keepdims=True)
    y = x * lax.rsqrt(ms + NORM_EPS) * nw
    return y * (1.0 + sc) + sh


def _lnmod_mm_kernel(x_ref, nw_ref, sh_ref, sc_ref, w_ref, o_ref, h_scr, *, epilogue):
    @pl.when(pl.program_id(1) == 0)
    def _():
        h_scr[...] = _lnmod_rows(x_ref[...], nw_ref[...], sh_ref[0], sc_ref[0]).astype(BF16)

    z = _dot(h_scr[...], w_ref[...])
    if epilogue == "sigmoid":
        z = _sigmoid(z)
    if epilogue == "lane_blocks":
        for jb in range(o_ref.shape[0]):
            o_ref[jb] = z[:, jb * LANES:(jb + 1) * LANES].astype(o_ref.dtype)
    else:
        o_ref[...] = z.astype(o_ref.dtype)


def _lnmod_swiglu_kernel(x_ref, nw_ref, sh_ref, sc_ref, w1_ref, w3_ref, o_ref, h_scr):
    @pl.when(pl.program_id(1) == 0)
    def _():
        h_scr[...] = _lnmod_rows(x_ref[...], nw_ref[...], sh_ref[0], sc_ref[0]).astype(BF16)

    h = h_scr[...]
    o_ref[...] = (_silu(_dot(h, w1_ref[...])) * _dot(h, w3_ref[...])).astype(o_ref.dtype)


def _lnmod_matmul(x2, nw, sh_tab, sc_tab, mod_row_of_block, w, *, tm, tn, out_dtype, epilogue=None, name):
    m, d = x2.shape
    n = w.shape[1]
    mod_map = lambda i, j: (mod_row_of_block(i), 0, 0)
    if epilogue == "lane_blocks":
        assert tn == n
        out_shape = jax.ShapeDtypeStruct((n // LANES, m, LANES), out_dtype)
        out_spec = pl.BlockSpec((n // LANES, tm, LANES), lambda i, j: (0, i, 0))
    else:
        out_shape = jax.ShapeDtypeStruct((m, n), out_dtype)
        out_spec = pl.BlockSpec((tm, tn), lambda i, j: (i, j))
    return pl.pallas_call(
        functools.partial(_lnmod_mm_kernel, epilogue=epilogue),
        out_shape=out_shape,
        grid=(m // tm, n // tn),
        in_specs=[pl.BlockSpec((tm, d), lambda i, j: (i, 0)),
                  pl.BlockSpec((1, d), lambda i, j: (0, 0)),
                  pl.BlockSpec((1, 1, d), mod_map),
                  pl.BlockSpec((1, 1, d), mod_map),
                  pl.BlockSpec((d, tn), lambda i, j: (0, j))],
        out_specs=out_spec,
        scratch_shapes=[pltpu.VMEM((tm, d), BF16)],
        compiler_params=_cparams(("parallel", "arbitrary")),
        name=name,
    )(x2, nw.reshape(1, d), sh_tab, sc_tab, w)


def _lnmod_swiglu(x2, nw, sh_tab, sc_tab, mod_row_of_block, w13, d_ff, *, tm, tn, name):
    m, d = x2.shape
    nj = d_ff // tn
    mod_map = lambda i, j: (mod_row_of_block(i), 0, 0)
    return pl.pallas_call(
        _lnmod_swiglu_kernel,
        out_shape=jax.ShapeDtypeStruct((m, d_ff), BF16),
        grid=(m // tm, nj),
        in_specs=[pl.BlockSpec((tm, d), lambda i, j: (i, 0)),
                  pl.BlockSpec((1, d), lambda i, j: (0, 0)),
                  pl.BlockSpec((1, 1, d), mod_map),
                  pl.BlockSpec((1, 1, d), mod_map),
                  pl.BlockSpec((d, tn), lambda i, j: (0, j)),
                  pl.BlockSpec((d, tn), lambda i, j: (0, j + nj))],
        out_specs=pl.BlockSpec((tm, tn), lambda i, j: (i, j)),
        scratch_shapes=[pltpu.VMEM((tm, d), BF16)],
        compiler_params=_cparams(("parallel", "arbitrary")),
        name=name,
    )(x2, nw.reshape(1, d), sh_tab, sc_tab, w13, w13)


def _s5_param_kernel(are_ref, aim_ref, ldt_ref, bre_ref, bim_ref, cre_ref, cim_ref,
                     e_ref, cs_ref, kt_ref, a16_ref):
    t_n, hg, p_n = S5_CHUNK, S5_GROUP, S5_STATE
    tau = lax.broadcasted_iota(jnp.int32, (t_n, 1, p_n), 0).astype(F32)
    for d in range(2):
        a_re = are_ref[0, d:d + 1, :]
        a_im = aim_ref[0, d:d + 1, :]
        dt = jnp.exp(ldt_ref[0, d:d + 1, :])
        lam = a_re * dt
        th = a_im * dt
        er = jnp.exp(lam)
        ab_re = er * jnp.cos(th)
        ab_im = er * jnp.sin(th)
        den = a_re * a_re + a_im * a_im
        x_re = ab_re - 1.0
        co_re = (x_re * a_re + ab_im * a_im) / den
        co_im = (ab_im * a_re - x_re * a_im) / den
        bt_re = bre_ref[0, d]
        bt_im = bim_ref[0, d]
        bb_re = co_re * bt_re - co_im * bt_im
        bb_im = co_re * bt_im + co_im * bt_re
        c_re = cre_ref[0, d]
        c_im = cim_ref[0, d]

        def power(tv):
            mag = jnp.exp(tv * lam)
            return mag * jnp.cos(tv * th), mag * jnp.sin(tv * th)

        pw_re, pw_im = power(tau)
        cp_re = (c_re[None] * pw_re - c_im[None] * pw_im).reshape(t_n * hg, p_n)
        cp_im = (c_re[None] * pw_im + c_im[None] * pw_re).reshape(t_n * hg, p_n)
        kt_ref[0, d] = _dot_nt(bb_re, cp_re, HIGHEST) - _dot_nt(bb_im, cp_im, HIGHEST)

        te = (t_n - 1.0 - tau) if d == 0 else tau
        pe_re, pe_im = power(te)
        e_ref[0, d, :, 0:p_n] = (pe_re * bb_re[None] - pe_im * bb_im[None]).reshape(t_n * hg, p_n)
        e_ref[0, d, :, p_n:2 * p_n] = (pe_re * bb_im[None] + pe_im * bb_re[None]).reshape(t_n * hg, p_n)

        tc = (tau + 1.0) if d == 0 else (t_n - tau)
        pc_re, pc_im = power(tc)
        cs_ref[0, d, :, 0:p_n] = (c_re[None] * pc_re - c_im[None] * pc_im).reshape(t_n * hg, p_n)
        cs_ref[0, d, :, p_n:2 * p_n] = -(c_re[None] * pc_im + c_im[None] * pc_re).reshape(t_n * hg, p_n)

        mag16 = jnp.exp(float(t_n) * lam)
        a16_ref[0, d, 0:1, :] = mag16 * jnp.cos(float(t_n) * th)
        a16_ref[0, d, 1:2, :] = mag16 * jnp.sin(float(t_n) * th)


def _s5_params(a_re, a_im, log_dt, b_re, b_im, c_re, c_im):
    g_n = a_re.shape[1]
    p_n, hg, t_n = S5_STATE, S5_GROUP, S5_CHUNK
    tr = lambda a: jnp.swapaxes(a, 0, 1)
    ldt = jnp.broadcast_to(tr(log_dt)[:, :, None], (g_n, 2, p_n))
    spec3 = pl.BlockSpec((1, 2, p_n), lambda g: (g, 0, 0))
    spec4 = pl.BlockSpec((1, 2, hg, p_n), lambda g: (g, 0, 0, 0))
    th = t_n * hg
    return pl.pallas_call(
        _s5_param_kernel,
        out_shape=(jax.ShapeDtypeStruct((g_n, 2, th, 2 * p_n), F32),
                   jax.ShapeDtypeStruct((g_n, 2, th, 2 * p_n), F32),
                   jax.ShapeDtypeStruct((g_n, 2, hg, th), F32),
                   jax.ShapeDtypeStruct((g_n, 2, 2, p_n), F32)),
        grid=(g_n,),
        in_specs=[spec3, spec3, spec3, spec4, spec4, spec4, spec4],
        out_specs=(pl.BlockSpec((1, 2, th, 2 * p_n), lambda g: (g, 0, 0, 0)),
                   pl.BlockSpec((1, 2, th, 2 * p_n), lambda g: (g, 0, 0, 0)),
                   pl.BlockSpec((1, 2, hg, th), lambda g: (g, 0, 0, 0)),
                   pl.BlockSpec((1, 2, 2, p_n), lambda g: (g, 0, 0, 0))),
        compiler_params=_cparams(("parallel",)),
        name="s5_params",
    )(tr(a_re), tr(a_im), ldt,
      jnp.transpose(b_re, (1, 0, 3, 2)), jnp.transpose(b_im, (1, 0, 3, 2)), tr(c_re), tr(c_im))


def _s5_state_in_kernel(u_ref, e_ref, o_ref):
    e = e_ref[0]
    u = u_ref[0]
    o_ref[0, :, 0:LANES] = _dot(u, e[0], HIGHEST)
    o_ref[0, :, LANES:2 * LANES] = _dot(u, e[1], HIGHEST)


def _s5_state_inputs(u_g, e):
    g_n, rows, th = u_g.shape
    return pl.pallas_call(
        _s5_state_in_kernel,
        out_shape=jax.ShapeDtypeStruct((g_n, rows, 2 * LANES), F32),
        grid=(g_n,),
        in_specs=[pl.BlockSpec((1, rows, th), lambda g: (g, 0, 0)),
                  pl.BlockSpec((1, 2, th, LANES), lambda g: (g, 0, 0, 0))],
        out_specs=pl.BlockSpec((1, rows, 2 * LANES), lambda g: (g, 0, 0)),
        compiler_params=_cparams(("parallel",)),
        name="s5_state_inputs",
    )(u_g, e)


def _s5_scan_kernel(e_ref, a_ref, o_ref, *, n_ctx, n_all):
    zero = jnp.zeros(e_ref.shape[2:], F32)

    def step(pr, pi):
        ar, ai = a_ref[pr], a_ref[pi]

        def body(c, carry):
            sr, si = carry
            o_ref[pr, c] = sr
            o_ref[pi, c] = si
            return (ar * sr - ai * si + e_ref[pr, c], ar * si + ai * sr + e_ref[pi, c])
        return body

    lax.fori_loop(0, n_all, step(0, 1), (zero, zero))
    bwd = step(2, 3)
    carry = lax.fori_loop(0, n_ctx, lambda k, cy: bwd(n_ctx - 1 - k, cy), (zero, zero))
    lax.fori_loop(0, n_all - n_ctx, lambda k, cy: bwd(n_all - 1 - k, cy), carry)


def _s5_scan(e_planes, a_planes, n_ctx, n_all):
    _, rows, r_n, _ = e_planes.shape
    bsz = rows // n_all
    sub = 8
    return pl.pallas_call(
        functools.partial(_s5_scan_kernel, n_ctx=n_ctx, n_all=n_all),
        out_shape=jax.ShapeDtypeStruct(e_planes.shape, F32),
        grid=(bsz, r_n // sub),
        in_specs=[pl.BlockSpec((4, n_all, sub, LANES), lambda b, q: (0, b, q, 0)),
                  pl.BlockSpec((4, sub, LANES), lambda b, q: (0, q, 0))],
        out_specs=pl.BlockSpec((4, n_all, sub, LANES), lambda b, q: (0, b, q, 0)),
        compiler_params=_cparams(("parallel", "parallel")),
        name="s5_scan",
    )(e_planes, a_planes)


def _s5_apply_kernel(u_ref, s_ref, mf_ref, mb_ref, d_ref, cs_ref, o_ref, *, n_ctx):
    u = u_ref[0, n_ctx:, :]
    s = s_ref[0, n_ctx:, :]
    y = _dot(u, mf_ref[0] + mb_ref[0], HIGHEST) + u * d_ref[0]
    y = y + _dot_nt(s[:, 0:LANES], cs_ref[0, 0], HIGHEST) + _dot_nt(s[:, LANES:2 * LANES], cs_ref[0, 1], HIGHEST)
    o_ref[0, 0] = y


def _s5_apply(u_g, s_g, m_f, m_b, d_t, cs, n_ctx, n_all):
    g_n, rows, th = u_g.shape
    bsz = rows // n_all
    n_lat = n_all - n_ctx
    return pl.pallas_call(
        functools.partial(_s5_apply_kernel, n_ctx=n_ctx),
        out_shape=jax.ShapeDtypeStruct((g_n, bsz, n_lat, th), F32),
        grid=(g_n, bsz),
        in_specs=[pl.BlockSpec((1, n_all, th), lambda g, b: (g, b, 0)),
                  pl.BlockSpec((1, n_all, 2 * LANES), lambda g, b: (g, b, 0)),
                  pl.BlockSpec((1, th, th), lambda g, b: (g, 0, 0)),
                  pl.BlockSpec((1, th, th), lambda g, b: (g, 0, 0)),
                  pl.BlockSpec((1, 1, th), lambda g, b: (g, 0, 0)),
                  pl.BlockSpec((1, 2, th, LANES), lambda g, b: (g, 0, 0, 0))],
        out_specs=pl.BlockSpec((1, 1, n_lat, th), lambda g, b: (g, b, 0, 0)),
        compiler_params=_cparams(("parallel", "arbitrary")),
        name="s5_apply",
    )(u_g, s_g, m_f, m_b, d_t, cs)


def _s5_branch(u_all, n_ctx_tok, a_re, a_im, log_dt, b_re, b_im, c_re, c_im, s5_d):
    bsz, l_all, width = u_all.shape
    hg, t_n, p_n = S5_GROUP, S5_CHUNK, S5_STATE
    g_n = width // hg
    n_all = l_all // t_n
    n_ctx = n_ctx_tok // t_n
    th = t_n * hg
    e, cs, kt, a16 = _s5_params(a_re, a_im, log_dt, b_re, b_im, c_re, c_im)

    kt5 = kt.reshape(g_n, 2, hg, t_n, hg)
    ii = jnp.arange(t_n)[:, None]
    jj = jnp.arange(t_n)[None, :]

    def toeplitz(k4, lag, keep):
        m = k4[:, :, jnp.clip(lag, 0, t_n - 1), :]
        m = jnp.where(keep[None, None, :, :, None], m, 0.0)
        return jnp.transpose(m, (0, 2, 1, 3, 4)).reshape(g_n, th, th)

    m_f = toeplitz(kt5[:, 0], jj - ii, jj >= ii)
    m_b = toeplitz(kt5[:, 1], ii - jj, ii >= jj)
    d_t = jnp.tile(s5_d, (1, t_n)).reshape(g_n, 1, th)

    u_g = jnp.transpose(u_all.reshape(bsz, n_all, t_n, g_n, hg), (3, 0, 1, 2, 4)).reshape(g_n, bsz * n_all, th)
    e_cat = e
    s_in = _s5_state_inputs(u_g, e_cat)
    rows = bsz * n_all
    planes = jnp.transpose(s_in.reshape(g_n, rows, 4, p_n), (2, 1, 0, 3)).reshape(4, rows, g_n * p_n // LANES, LANES)
    a_pl = jnp.transpose(a16.reshape(g_n, 4, p_n), (1, 0, 2)).reshape(4, g_n * p_n // LANES, LANES)
    st = _s5_scan(planes, a_pl, n_ctx, n_all)
    s_g = jnp.transpose(st.reshape(4, rows, g_n, p_n), (2, 1, 0, 3)).reshape(g_n, rows, 4 * p_n)
    y_g = _s5_apply(u_g, s_g, m_f, m_b, d_t, cs, n_ctx, n_all)
    n_lat = n_all - n_ctx
    y = jnp.transpose(y_g.reshape(g_n, bsz, n_lat, t_n, hg), (1, 2, 3, 0, 4))
    return y.reshape(bsz, n_lat * t_n, width)


def _s5_param2_kernel(are_ref, aim_ref, ldt_ref, bre_ref, bim_ref, cre_ref, cim_ref,
                      e_ref, c_ref, m_ref, a16_ref):
    t_n, hg, p_n = S5_CHUNK, S5_GROUP, S5_STATE
    tau = lax.broadcasted_iota(jnp.int32, (t_n, 1, p_n), 0).astype(F32)
    taps = []
    for d in range(2):
        a_re = are_ref[0, d:d + 1, :]
        a_im = aim_ref[0, d:d + 1, :]
        dt = jnp.exp(ldt_ref[0, d:d + 1, :])
        lam = a_re * dt
        th = a_im * dt
        er = jnp.exp(lam)
        ab_re = er * jnp.cos(th)
        ab_im = er * jnp.sin(th)
        den = a_re * a_re + a_im * a_im
        x_re = ab_re - 1.0
        co_re = (x_re * a_re + ab_im * a_im) / den
        co_im = (ab_im * a_re - x_re * a_im) / den
        bt_re = bre_ref[0, d]
        bt_im = bim_ref[0, d]
        bb_re = co_re * bt_re - co_im * bt_im
        bb_im = co_re * bt_im + co_im * bt_re
        c_re = cre_ref[0, d]
        c_im = cim_ref[0, d]

        def power(tv):
            mag = jnp.exp(tv * lam)
            return mag * jnp.cos(tv * th), mag * jnp.sin(tv * th)

        pw_re, pw_im = power(tau if d == 0 else (t_n - 1.0 - tau))
        cp_re = (c_re[None] * pw_re - c_im[None] * pw_im).reshape(t_n * hg, p_n)
        cp_im = (c_re[None] * pw_im + c_im[None] * pw_re).reshape(t_n * hg, p_n)
        taps.append(_dot_nt(bb_re, cp_re, HIGHEST) - _dot_nt(bb_im, cp_im, HIGHEST))

        lo, hi = 2 * d * p_n, (2 * d + 1) * p_n
        pe_re, pe_im = power((t_n - 1.0 - tau) if d == 0 else tau)
        e_ref[0, :, 0, :, lo:hi] = pe_re * bb_re[None] - pe_im * bb_im[None]
        e_ref[0, :, 0, :, hi:hi + p_n] = pe_re * bb_im[None] + pe_im * bb_re[None]
        pc_re, pc_im = power((tau + 1.0) if d == 0 else (t_n - tau))
        c_ref[0, :, 0, :, lo:hi] = c_re[None] * pc_re - c_im[None] * pc_im
        c_ref[0, :, 0, :, hi:hi + p_n] = -(c_re[None] * pc_im + c_im[None] * pc_re)

        mag16 = jnp.exp(float(t_n) * lam)
        a16_ref[0, d, 0:1, :] = mag16 * jnp.cos(float(t_n) * th)
        a16_ref[0, d, 1:2, :] = mag16 * jnp.sin(float(t_n) * th)

    width = t_n * hg
    lane = lax.broadcasted_iota(jnp.int32, (hg, width), 1)
    for t in range(t_n):
        sf = t * hg
        sb = (t_n - 1 - t) * hg
        f = taps[0] if sf == 0 else jnp.where(lane >= sf, pltpu.roll(taps[0], sf, 1), 0.0)
        b = taps[1] if sb == 0 else jnp.where(lane < width - sb, pltpu.roll(taps[1], width - sb, 1), 0.0)
        m_ref[0, t, 0] = f + b


def _s5_params2(a_re, a_im, log_dt, b_re, b_im, c_re, c_im, nb):
    g_n = a_re.shape[1]
    gl_n = g_n // nb
    p_n, hg, t_n = S5_STATE, S5_GROUP, S5_CHUNK
    tr = lambda a: jnp.swapaxes(a, 0, 1)
    ldt = jnp.broadcast_to(tr(log_dt)[:, :, None], (g_n, 2, p_n))
    spec3 = pl.BlockSpec((1, 2, p_n), lambda g: (g, 0, 0))
    spec4 = pl.BlockSpec((1, 2, hg, p_n), lambda g: (g, 0, 0, 0))
    wide = 4 * p_n
    comp = jax.ShapeDtypeStruct((nb, t_n, gl_n, hg, wide), F32)
    comp_spec = pl.BlockSpec((1, t_n, 1, hg, wide), lambda g: (g // gl_n, 0, g % gl_n, 0, 0))
    return pl.pallas_call(
        _s5_param2_kernel,
        out_shape=(comp, comp, comp, jax.ShapeDtypeStruct((g_n, 2, 2, p_n), F32)),
        grid=(g_n,),
        in_specs=[spec3, spec3, spec3, spec4, spec4, spec4, spec4],
        out_specs=(comp_spec, comp_spec, comp_spec, pl.BlockSpec((1, 2, 2, p_n), lambda g: (g, 0, 0, 0))),
        compiler_params=_cparams(("parallel",)),
        name="s5_params",
    )(tr(a_re), tr(a_im), ldt,
      jnp.transpose(b_re, (1, 0, 3, 2)), jnp.transpose(b_im, (1, 0, 3, 2)), tr(c_re), tr(c_im))


def _expand_block_diag(comp, rep_ref, mask_ref, w_scr):
    k = w_scr.shape[0]
    period = mask_ref.shape[1]
    cb = comp.astype(BF16)
    step = 512
    for c0 in range(0, k, step):
        blk = _dot(cb, rep_ref[:, c0:c0 + step]).astype(BF16)
        for q0 in range(0, step, period):
            w_scr[:, c0 + q0:c0 + q0 + period] = blk[:, q0:q0 + period] * mask_ref[...]


def _s5_ein_kernel(u_ref, ec_ref, rep_ref, mask_ref, o_ref, w_scr):
    _expand_block_diag(ec_ref[0], rep_ref, mask_ref, w_scr)
    o_ref[0] = _dot(u_ref[0].astype(BF16), w_scr[...])


def _s5_chunk_inputs(u_rows, e_comp, rep_e, mask_e):
    nb, rows, k = u_rows.shape
    cw = e_comp.shape[2]
    return pl.pallas_call(
        _s5_ein_kernel,
        out_shape=jax.ShapeDtypeStruct((nb, rows, k), F32),
        grid=(nb,),
        in_specs=[pl.BlockSpec((1, rows, k), lambda j: (j, 0, 0)),
                  pl.BlockSpec((1, k, cw), lambda j: (j, 0, 0)),
                  pl.BlockSpec(rep_e.shape, lambda j: (0, 0)),
                  pl.BlockSpec(mask_e.shape, lambda j: (0, 0))],
        out_specs=pl.BlockSpec((1, rows, k), lambda j: (j, 0, 0)),
        scratch_shapes=[pltpu.VMEM((k, k), BF16)],
        compiler_params=_cparams(("parallel",)),
        name="s5_chunk_inputs",
    )(u_rows, e_comp, rep_e, mask_e)


def _s5_bscan_kernel(e_ref, a_ref, o_ref, *, bsz, n_ctx, n_lat):
    q = e_ref.shape[2] // 4
    planes = lambda row, d: (row[:, (2 * d) * q:(2 * d + 1) * q], row[:, (2 * d + 1) * q:(2 * d + 2) * q])
    coef = [planes(a_ref[0], d) for d in range(2)]
    ctx0 = bsz * n_lat

    def advance(state, rows):
        new = []
        for (sr, si), (b, d), row in zip(state, [(b, d) for b in range(bsz) for d in range(2)], rows):
            ar, ai = coef[d]
            er, ei = planes(e_ref[0, pl.ds(row, 1), :], d)
            new.append((ar * sr - ai * si + er, ar * si + ai * sr + ei))
        return tuple(new)

    def ctx_step(s, state):
        rows = [ctx0 + b * n_ctx + (s if d == 0 else n_ctx - 1 - s) for b in range(bsz) for d in range(2)]
        return advance(state, rows)

    def lat_step(s, state):
        rows = [b * n_lat + (s if d == 0 else n_lat - 1 - s) for b in range(bsz) for d in range(2)]
        for (sr, si), (b, d), row in zip(state, [(b, d) for b in range(bsz) for d in range(2)], rows):
            o_ref[0, pl.ds(row, 1), (2 * d) * q:(2 * d + 1) * q] = sr
            o_ref[0, pl.ds(row, 1), (2 * d + 1) * q:(2 * d + 2) * q] = si
        return advance(state, rows)

    zero = jnp.zeros((1, q), F32)
    state = tuple((zero, zero) for _ in range(2 * bsz))
    state = lax.fori_loop(0, n_ctx, ctx_step, state)
    lax.fori_loop(0, n_lat, lat_step, state)


def _s5_bscan(e_rows, a_rows, bsz, n_ctx, n_lat):
    nb, rows, k = e_rows.shape
    return pl.pallas_call(
        functools.partial(_s5_bscan_kernel, bsz=bsz, n_ctx=n_ctx, n_lat=n_lat),
        out_shape=jax.ShapeDtypeStruct((nb, bsz * n_lat, k), F32),
        grid=(nb,),
        in_specs=[pl.BlockSpec((1, rows, k), lambda j: (j, 0, 0)),
                  pl.BlockSpec((1, 1, k), lambda j: (j, 0, 0))],
        out_specs=pl.BlockSpec((1, bsz * n_lat, k), lambda j: (j, 0, 0)),
        compiler_params=_cparams(("parallel",)),
        name="s5_scan",
    )(e_rows, a_rows)


def _s5_out_kernel(u_ref, s_ref, mc_ref, cc_ref, d_ref, rep_m_ref, mask_m_ref, rep_e_ref, mask_e_ref,
                   o_ref, wm_scr, wc_scr):
    @pl.when(pl.program_id(1) == 0)
    def _():
        _expand_block_diag(mc_ref[0], rep_m_ref, mask_m_ref, wm_scr)
        _expand_block_diag(cc_ref[0], rep_e_ref, mask_e_ref, wc_scr)

    u = u_ref[0]
    y = _dot(u.astype(BF16), wm_scr[...]) + _dot_nt(s_ref[0].astype(BF16), wc_scr[...])
    o_ref[0] = y + u * d_ref[0]


def _s5_outputs(u_rows, s_rows, m_comp, c_comp, d_rows, rep_m, mask_m, rep_e, mask_e):
    nb, _, k = u_rows.shape
    rows = s_rows.shape[1]
    cw = m_comp.shape[2]
    tr = rows // 2
    const = lambda a: pl.BlockSpec(a.shape, lambda j, i: (0, 0))
    return pl.pallas_call(
        _s5_out_kernel,
        out_shape=jax.ShapeDtypeStruct((nb, rows, k), F32),
        grid=(nb, rows // tr),
        in_specs=[pl.BlockSpec((1, tr, k), lambda j, i: (j, i, 0)),
                  pl.BlockSpec((1, tr, k), lambda j, i: (j, i, 0)),
                  pl.BlockSpec((1, k, cw), lambda j, i: (j, 0, 0)),
                  pl.BlockSpec((1, k, cw), lambda j, i: (j, 0, 0)),
                  pl.BlockSpec((1, 1, k), lambda j, i: (j, 0, 0)),
                  const(rep_m), const(mask_m), const(rep_e), const(mask_e)],
        out_specs=pl.BlockSpec((1, tr, k), lambda j, i: (j, i, 0)),
        scratch_shapes=[pltpu.VMEM((k, k), BF16), pltpu.VMEM((k, k), BF16)],
        compiler_params=_cparams(("parallel", "arbitrary")),
        name="s5_outputs",
    )(u_rows, s_rows, m_comp, c_comp, d_rows, rep_m, mask_m, rep_e, mask_e)


def _s5_branch_blocked(u_blk, bsz, l_ctx, l_lat, a_re, a_im, log_dt, b_re, b_im, c_re, c_im, s5_d):
    nb, m_all, _ = u_blk.shape
    hg, t_n, p_n = S5_GROUP, S5_CHUNK, S5_STATE
    g_n = a_re.shape[1]
    gl_n = g_n // nb
    k = t_n * LANES
    n_lat = l_lat // t_n
    n_ctx = l_ctx // t_n
    e_c, c_c, m_c, a16 = _s5_params2(a_re, a_im, log_dt, b_re, b_im, c_re, c_im, nb)
    cw = 4 * p_n
    e_comp, c_comp, m_comp = (a.reshape(nb, k, cw) for a in (e_c, c_c, m_c))

    row_gl = (jnp.arange(k) // hg) % gl_n
    col = jnp.arange(k)
    src = jnp.arange(cw)
    rep_e = ((src[:, None] // p_n == col[None, :] // (gl_n * p_n)) & (src[:, None] % p_n == col[None, :] % p_n)).astype(BF16)
    mask_e = (row_gl[:, None] == (jnp.arange(gl_n * p_n)[None, :] // p_n)).astype(BF16)
    rep_m = ((src[:, None] // hg == col[None, :] // (gl_n * hg)) & (src[:, None] % hg == col[None, :] % hg)).astype(BF16)
    mask_m = (row_gl[:, None] == (jnp.arange(gl_n * hg)[None, :] // hg)).astype(BF16)

    d_rows = jnp.tile(s5_d.reshape(nb, 1, gl_n * hg), (1, 1, t_n))
    a_rows = jnp.transpose(a16.reshape(nb, gl_n, 2, 2, p_n), (0, 2, 3, 1, 4)).reshape(nb, 1, 4 * gl_n * p_n)

    u_rows = u_blk.reshape(nb, m_all // t_n, k)
    e_rows = _s5_chunk_inputs(u_rows, e_comp, rep_e, mask_e)
    s_rows = _s5_bscan(e_rows, a_rows, bsz, n_ctx, n_lat)
    y_rows = _s5_outputs(u_rows, s_rows, m_comp, c_comp, d_rows, rep_m, mask_m, rep_e, mask_e)
    return y_rows.reshape(nb, bsz * l_lat, LANES)


def _rw_prep_kernel(z_ref, zp_ref, zn_ref, mu_ref, w2_ref, a2_ref, g2_ref, w0_ref, a0_ref,
                    kk_w_ref, ka_ref, rk_ref, seg_ref, segt_ref,
                    r_ref, v_ref, kk_ref, g_ref, bonus_ref, lw_ref, kd_ref, be_ref,
                    *, tm, l_lat, rw):
    j = pl.program_id(1)
    z = z_ref[...]
    lat = j > 0
    tl = lax.broadcasted_iota(jnp.int32, (tm, 1), 0)
    tok = (j - 1) * tm + tl
    col = tl % GRID_W
    m_l = jnp.where(lat, col, tl) > 0
    m_r = jnp.where(lat, col - (GRID_W - 1), tl - (tm - 1)) < 0
    m_u = jnp.logical_and(lat, tok >= GRID_W)
    m_d = jnp.logical_and(lat, tok < l_lat - GRID_W)
    z_l = pltpu.roll(z, 1, 0)
    z_r = pltpu.roll(z, tm - 1, 0)
    z_u = jnp.concatenate([zp_ref[...], z[:tm - GRID_W]], axis=0)
    z_d = jnp.concatenate([z[GRID_W:], zn_ref[...]], axis=0)
    s = (jnp.where(m_l, z_l, 0.0) + jnp.where(m_r, z_r, 0.0)
         + jnp.where(m_u, z_u, 0.0) + jnp.where(m_d, z_d, 0.0))
    cnt = (m_l.astype(F32) + m_r.astype(F32)) + (m_u.astype(F32) + m_d.astype(F32))
    zs = z + (s / cnt - z) * mu_ref[...]

    r = zs[:, 0:rw]
    k = zs[:, rw:2 * rw]
    v = zs[:, 2 * rw:3 * rw]
    o = 3 * rw
    wd = zs[:, o:o + LANES]
    ad = zs[:, o + LANES:o + 2 * LANES]
    gd = zs[:, o + 2 * LANES:o + 3 * LANES]

    seg = seg_ref[...]
    segt = segt_ref[...]

    def head_sum(t):
        return _dot_hilo(_dot_hilo(t, seg), segt)

    g_ref[0] = _dot(_sigmoid(gd), g2_ref[...], "bf16")
    kk = k * kk_w_ref[...]
    kk = kk * lax.rsqrt(head_sum(kk * kk) + 1e-12)
    wl = w0_ref[...] + _dot(jnp.tanh(wd), w2_ref[...], "bf16")
    al = a0_ref[...] + _dot(ad, a2_ref[...], "bf16")
    r_ref[0] = r
    v_ref[0] = v
    kk_ref[0] = kk
    coef = jnp.zeros_like(r)
    for d in range(2):
        w_raw = -_softplus(-wl[:, d * rw:(d + 1) * rw]) - 0.5
        a = _sigmoid(al[:, d * rw:(d + 1) * rw])
        k_d = k * (1.0 + (a - 1.0) * ka_ref[...])
        lw_ref[d, 0] = -jnp.exp(w_raw)
        kd_ref[d, 0] = k_d
        be_ref[d, 0] = kk * a
        coef = coef + head_sum(r * k_d * rk_ref[...])
    bonus_ref[0] = coef * v


def _rw_prep(z_rw, bsz, l_ctx, l_lat, mu, w2bd, a2bd, g2, w0cat, a0cat, k_k, k_a, r_k_flat, seg, segt):
    cols = z_rw.shape[1]
    tm = l_ctx
    rw = g2.shape[1]
    l_all = l_ctx + l_lat
    nblk = l_all // tm
    lat_blk = l_lat // tm
    hb = tm // GRID_W
    lat_hblk = l_lat // GRID_W

    def main_blk(b, j):
        return jnp.where(j == 0, bsz * lat_blk + b, b * lat_blk + j - 1)

    def prev_halo(b, j):
        return b * lat_hblk + jnp.maximum((j - 1) * hb - 1, 0)

    def next_halo(b, j):
        return b * lat_hblk + jnp.minimum(jnp.maximum(j, 1) * hb, lat_hblk - 1)

    full = lambda shape: pl.BlockSpec(shape, lambda b, j: (0,) * len(shape))
    shared = jax.ShapeDtypeStruct((bsz, l_all, rw), F32)
    lat_only = jax.ShapeDtypeStruct((bsz, l_lat, rw), F32)
    per_dir = jax.ShapeDtypeStruct((2, bsz, l_all, rw), F32)
    o_shared = pl.BlockSpec((1, tm, rw), lambda b, j: (b, j, 0))
    o_lat = pl.BlockSpec((1, tm, rw), lambda b, j: (b, jnp.maximum(j - 1, 0), 0))
    o_dir = pl.BlockSpec((2, 1, tm, rw), lambda b, j: (0, b, j, 0))
    return pl.pallas_call(
        functools.partial(_rw_prep_kernel, tm=tm, l_lat=l_lat, rw=rw),
        out_shape=(shared,) * 3 + (lat_only,) * 2 + (per_dir,) * 3,
        grid=(bsz, nblk),
        in_specs=[pl.BlockSpec((tm, cols), lambda b, j: (main_blk(b, j), 0)),
                  pl.BlockSpec((GRID_W, cols), lambda b, j: (prev_halo(b, j), 0)),
                  pl.BlockSpec((GRID_W, cols), lambda b, j: (next_halo(b, j), 0)),
                  full((1, cols)), full(w2bd.shape), full(a2bd.shape), full(g2.shape),
                  full((1, 2 * rw)), full((1, 2 * rw)), full((1, rw)), full((1, rw)), full((1, rw)),
                  full(seg.shape), full(segt.shape)],
        out_specs=(o_shared,) * 3 + (o_lat,) * 2 + (o_dir,) * 3,
        compiler_params=_cparams(("parallel", "arbitrary")),
        name="rwkv_prep",
    )(z_rw, z_rw, z_rw, mu.reshape(1, cols), w2bd, a2bd, g2, w0cat, a0cat,
      k_k.reshape(1, rw), k_a.reshape(1, rw), r_k_flat.reshape(1, rw), seg, segt)


def _stack_heads(x, head0):
    return jnp.concatenate([jnp.where(head0, x, 0.0), jnp.where(head0, 0.0, x)], axis=0)


def _rw_chunk_kernel(r_ref, v_ref, kk_ref, lw_ref, kd_ref, be_ref, y_ref,
                     z_scr, lhs_s, kb_s, kht_s, bht_s, vs_s, rt_s, aab_s, akr_s, arb_s, t_s, x_s, xin_s, yc_s, wu_s,
                     *, n_pairs):
    c_n = RW_CHUNK
    n2 = 2 * c_n
    rev = (pl.program_id(0) % 2) == 1

    @pl.when(pl.program_id(1) == 0)
    def _():
        z_scr[...] = jnp.zeros_like(z_scr)

    ri = lax.broadcasted_iota(jnp.int32, (c_n, c_n), 0)
    ci = lax.broadcasted_iota(jnp.int32, (c_n, c_n), 1)
    tri = (jnp.where(rev, ci - ri, ri - ci) >= 0).astype(F32)
    r2 = lax.broadcasted_iota(jnp.int32, (n2, n2), 0)
    c2 = lax.broadcasted_iota(jnp.int32, (n2, n2), 1)
    t2 = r2 % c_n
    i2 = c2 % c_n
    same_head = (r2 // c_n) == (c2 // c_n)
    before = jnp.logical_and(same_head, jnp.where(rev, i2 - t2, t2 - i2) > 0)
    upto = jnp.logical_or(before, r2 == c2)
    eye = (r2 == c2).astype(F32)
    head0 = lax.broadcasted_iota(jnp.int32, (1, LANES), 1) < RW_HEAD

    def blk(s):
        return (r2 // s) == (c2 // s)

    pairs = range(n_pairs)
    lanes_of = lambda p: slice(p * LANES, (p + 1) * LANES)

    lw = lw_ref[0, 0]
    cum = _dot(tri, lw, HIGHEST)
    tot = jnp.sum(lw, axis=0, keepdims=True)
    g_inv = jnp.exp(-cum)
    g_hat = jnp.exp(tot - cum)
    kd = kd_ref[0, 0]
    be = be_ref[0, 0]
    at = kk_ref[0] * jnp.exp(cum - lw)
    rt = r_ref[0] * jnp.exp(cum)
    kt = kd * g_inv
    bt = be * g_inv
    kh = kd * g_hat
    bh = be * g_hat
    g_c = jnp.exp(tot)
    v = v_ref[0]
    for p in pairs:
        st = lambda x: _stack_heads(x[:, lanes_of(p)], head0)
        rt_p = st(rt)
        at_p = st(at).astype(BF16)
        lhs_s[p, :n2] = at_p
        lhs_s[p, n2:] = rt_p.astype(BF16)
        xin_s[p, :, :LANES] = at_p
        rt_s[p] = rt_p
        kb_s[p, :n2] = st(kt).astype(BF16)
        kb_s[p, n2:] = st(bt).astype(BF16)
        kht_s[p] = st(kh).T.astype(BF16)
        bht_s[p] = st(bh).T.astype(BF16)
        vs_s[p] = st(v).astype(BF16)

    for p in pairs:
        g = _dot_nt(lhs_s[p], kb_s[p])
        a_ab = jnp.where(before, g[:n2, n2:], 0.0)
        akr_s[p, :n2] = jnp.where(before, g[:n2, :n2], 0.0).astype(BF16)
        akr_s[p, n2:] = jnp.where(upto, g[n2:, :n2], 0.0).astype(BF16)
        arb_s[p] = jnp.where(upto, g[n2:, n2:], 0.0).astype(BF16)
        aab_s[p] = a_ab
        t_s[p] = eye - jnp.where(blk(2), a_ab, 0.0)

    s = 2
    while s < c_n:
        off = jnp.logical_and(blk(2 * s), jnp.logical_not(blk(s)))
        for p in pairs:
            x_s[p] = _dot(t_s[p].astype(BF16), jnp.where(off, aab_s[p], 0.0).astype(BF16)).astype(BF16)
        for p in pairs:
            t = t_s[p]
            t_s[p] = t - _dot(x_s[p], t.astype(BF16))
        s *= 2

    for p in pairs:
        av = _dot(akr_s[p], vs_s[p])
        xin_s[p, :, LANES:] = av[:n2].astype(BF16)
        yc_s[p] = av[n2:]

    for p in pairs:
        wu_s[p] = _dot(t_s[p].astype(BF16), xin_s[p]).astype(BF16)

    for p in pairs:
        wu = wu_s[p]
        q = _dot(arb_s[p], wu)
        bwu = _dot(bht_s[p], wu)
        kv = _dot(kht_s[p], vs_s[p])
        z0 = z_scr[p].astype(BF16)
        y_p = _dot((rt_s[p] - q[:, :LANES]).astype(BF16), z0) + (yc_s[p] - q[:, LANES:])
        y_ref[0, 0, :, lanes_of(p)] = y_p[:c_n] + y_p[c_n:]
        m_z = eye * g_c[:, lanes_of(p)] - bwu[:, :LANES]
        z_scr[p] = _dot(m_z.astype(BF16), z0) + (kv - bwu[:, LANES:])


def _rw_scan(r, v, kk, lw, kd, be, l_ctx):
    bsz, l_all, rw = r.shape
    c_n = RW_CHUNK
    n_all = l_all // c_n
    n_ctx = l_ctx // c_n
    n_lat = n_all - n_ctx
    n_pairs = rw // LANES
    n2 = 2 * c_n
    vm = lambda rows, cols, dt: pltpu.VMEM((n_pairs, rows, cols), dt)

    def chunk_of(bd, s):
        fwd = s
        bwd = jnp.where(s < n_ctx, n_ctx - 1 - s, n_all + n_ctx - 1 - s)
        return jnp.where(bd % 2 == 0, fwd, bwd)

    def out_chunk(bd, s):
        c = chunk_of(bd, s)
        edge = jnp.where(bd % 2 == 0, 0, n_lat - 1)
        return jnp.where(s < n_ctx, edge, c - n_ctx)

    shared = pl.BlockSpec((1, c_n, rw), lambda bd, s: (bd // 2, chunk_of(bd, s), 0))
    per_dir = pl.BlockSpec((1, 1, c_n, rw), lambda bd, s: (bd % 2, bd // 2, chunk_of(bd, s), 0))
    return pl.pallas_call(
        functools.partial(_rw_chunk_kernel, n_pairs=n_pairs),
        out_shape=jax.ShapeDtypeStruct((2, bsz, n_lat * c_n, rw), F32),
        grid=(2 * bsz, n_all),
        in_specs=[shared, shared, shared, per_dir, per_dir, per_dir],
        out_specs=pl.BlockSpec((1, 1, c_n, rw), lambda bd, s: (bd % 2, bd // 2, out_chunk(bd, s), 0)),
        scratch_shapes=[vm(n2, n2, F32),
                        vm(2 * n2, n2, BF16),
                        vm(2 * n2, n2, BF16),
                        vm(n2, n2, BF16),
                        vm(n2, n2, BF16),
                        vm(n2, n2, BF16),
                        vm(n2, n2, F32),
                        vm(n2, n2, F32),
                        vm(2 * n2, n2, BF16),
                        vm(n2, n2, BF16),
                        vm(n2, n2, F32),
                        vm(n2, n2, BF16),
                        vm(n2, 2 * n2, BF16),
                        vm(n2, n2, F32),
                        vm(n2, 2 * n2, BF16)],
        compiler_params=_cparams(("parallel", "arbitrary")),
        name="rwkv_scan",
    )(r, v, kk, lw, kd, be)


def _s5_glu_kernel(y_ref, wa_ref, wb_ref, o_ref, h_scr):
    @pl.when(pl.program_id(1) == 0)
    def _():
        for jb in range(y_ref.shape[0]):
            h_scr[:, jb * LANES:(jb + 1) * LANES] = _gelu_tanh(y_ref[jb]).astype(BF16)

    h = h_scr[...]
    o_ref[...] = (_dot(h, wa_ref[...]) * _sigmoid(_dot(h, wb_ref[...]))).astype(o_ref.dtype)


def _s5_glu(y_blk, w, *, tm, tn):
    nb, m, _ = y_blk.shape
    k = nb * LANES
    n = w.shape[1] // 2
    nj = n // tn
    return pl.pallas_call(
        _s5_glu_kernel,
        out_shape=jax.ShapeDtypeStruct((m, n), BF16),
        grid=(m // tm, nj),
        in_specs=[pl.BlockSpec((nb, tm, LANES), lambda i, j: (0, i, 0)),
                  pl.BlockSpec((k, tn), lambda i, j: (0, j)),
                  pl.BlockSpec((k, tn), lambda i, j: (0, j + nj))],
        out_specs=pl.BlockSpec((tm, tn), lambda i, j: (i, j)),
        scratch_shapes=[pltpu.VMEM((tm, k), BF16)],
        compiler_params=_cparams(("parallel", "arbitrary")),
        name="s5_glu",
    )(y_blk, w, w)


def _rw_merge_kernel(yf_ref, yb_ref, bonus_ref, g_ref, lnw_ref, lnb_ref, seg_ref, segt_ref,
                     ga_ref, gb_ref, s5_ref, w_ref, o_ref, h_scr):
    @pl.when(pl.program_id(1) == 0)
    def _():
        seg = seg_ref[...]
        segt = segt_ref[...]
        inv_n = 1.0 / RW_HEAD

        def head_mean(t):
            return _dot_hilo(_dot_hilo(t, seg), segt) * inv_n

        y = yf_ref[0] + yb_ref[0]
        dy = y - head_mean(y)
        var = head_mean(dy * dy)
        y = dy * lax.rsqrt(var + GN_EPS) * lnw_ref[...] + lnb_ref[...] + bonus_ref[...]
        h_scr[...] = (y * g_ref[...]).astype(BF16)

    rw_out = _dot(h_scr[...], w_ref[...])
    merged = ga_ref[...].astype(F32) * s5_ref[...].astype(F32) + gb_ref[...].astype(F32) * rw_out
    o_ref[...] = merged.astype(o_ref.dtype)


def _rw_merge(y_dirs, bonus, g, ln_w, ln_b, seg, segt, gates, s5_out, w_proj, *, tm, tn):
    _, m, rw = y_dirs.shape
    n = w_proj.shape[1]
    nj = n // tn
    full = lambda shape: pl.BlockSpec(shape, lambda i, j: (0,) * len(shape))
    return pl.pallas_call(
        _rw_merge_kernel,
        out_shape=jax.ShapeDtypeStruct((m, n), BF16),
        grid=(m // tm, nj),
        in_specs=[pl.BlockSpec((1, tm, rw), lambda i, j: (0, i, 0)),
                  pl.BlockSpec((1, tm, rw), lambda i, j: (1, i, 0)),
                  pl.BlockSpec((tm, rw), lambda i, j: (i, 0)),
                  pl.BlockSpec((tm, rw), lambda i, j: (i, 0)),
                  full((1, rw)), full((1, rw)), full(seg.shape), full(segt.shape),
                  pl.BlockSpec((tm, tn), lambda i, j: (i, j)),
                  pl.BlockSpec((tm, tn), lambda i, j: (i, j + nj)),
                  pl.BlockSpec((tm, tn), lambda i, j: (i, j)),
                  pl.BlockSpec((rw, tn), lambda i, j: (0, j))],
        out_specs=pl.BlockSpec((tm, tn), lambda i, j: (i, j)),
        scratch_shapes=[pltpu.VMEM((tm, rw), BF16)],
        compiler_params=_cparams(("parallel", "arbitrary")),
        name="rwkv_merge",
    )(y_dirs, y_dirs, bonus, g, ln_w.reshape(1, rw), ln_b.reshape(1, rw), seg, segt,
      gates, gates, s5_out, w_proj)


def _resid_mm_kernel(a_ref, w_ref, x_ref, g_ref, o_ref):
    o_ref[...] = x_ref[...] + g_ref[0] * _dot(a_ref[...], w_ref[...])


def _resid_matmul(a, w, x2, g_tab, mod_row_of_block, *, tm, tn):
    m, k = a.shape
    n = w.shape[1]
    return pl.pallas_call(
        _resid_mm_kernel,
        out_shape=jax.ShapeDtypeStruct((m, n), F32),
        grid=(m // tm, n // tn),
        in_specs=[pl.BlockSpec((tm, k), lambda i, j: (i, 0)),
                  pl.BlockSpec((k, tn), lambda i, j: (0, j)),
                  pl.BlockSpec((tm, tn), lambda i, j: (i, j)),
                  pl.BlockSpec((1, 1, tn), lambda i, j: (mod_row_of_block(i), 0, j))],
        out_specs=pl.BlockSpec((tm, tn), lambda i, j: (i, j)),
        compiler_params=_cparams(("parallel", "arbitrary")),
        name="out_proj",
    )(a, w, x2, g_tab)


def _ffn_down_kernel(a_ref, w_ref, x_ref, g_ref, nf_ref, o_ref, acc_ref):
    kk = pl.program_id(1)

    @pl.when(kk == 0)
    def _():
        acc_ref[...] = jnp.zeros_like(acc_ref)

    acc_ref[...] += _dot(a_ref[...], w_ref[...])

    @pl.when(kk == pl.num_programs(1) - 1)
    def _():
        h = x_ref[...] + g_ref[0] * acc_ref[...]
        ms = jnp.mean(h * h, axis=-1, keepdims=True)
        o_ref[...] = h * lax.rsqrt(ms + NORM_EPS) * nf_ref[...]


def _ffn_down(a, w, x2, g_tab, mod_row_of_block, norm_f, *, tm, tk):
    m, k = a.shape
    n = w.shape[1]
    return pl.pallas_call(
        _ffn_down_kernel,
        out_shape=jax.ShapeDtypeStruct((m, n), F32),
        grid=(m // tm, k // tk),
        in_specs=[pl.BlockSpec((tm, tk), lambda i, kk: (i, kk)),
                  pl.BlockSpec((tk, n), lambda i, kk: (kk, 0)),
                  pl.BlockSpec((tm, n), lambda i, kk: (i, 0)),
                  pl.BlockSpec((1, 1, n), lambda i, kk: (mod_row_of_block(i), 0, 0)),
                  pl.BlockSpec((1, n), lambda i, kk: (0, 0))],
        out_specs=pl.BlockSpec((tm, n), lambda i, kk: (i, 0)),
        scratch_shapes=[pltpu.VMEM((tm, n), F32)],
        compiler_params=_cparams(("parallel", "arbitrary")),
        name="ffn_down",
    )(a, w, x2, g_tab, norm_f.reshape(1, n))


def kernel(x, c, ctx, c_ctx, ada_w, ada_b, norm1_w, w_in, rw_mu, s5_a_re, s5_a_im, s5_log_dt, s5_b_re, s5_b_im, s5_c_re, s5_c_im, s5_d, s5_glu_w, rw_w0, rw_w2, rw_a0, rw_a2, rw_g2, rw_k_k, rw_k_a, rw_r_k, rw_ln_w, rw_ln_b, rw_proj, w_o, norm2_w, ffn_w13, ffn_w2, norm_f):
    assert ada_w.shape[0] == 1, "single-layer block"
    bsz, l_lat, d = x.shape
    l_ctx = ctx.shape[1]
    l_all = l_ctx + l_lat
    s5w = s5_d.shape[1] * s5_d.shape[2]
    rw = rw_g2.shape[2]
    shift_cols = rw_mu.shape[1]
    d_ff = ffn_w2.shape[1]
    n_heads = rw // RW_HEAD

    c_rows = jnp.concatenate([c, c_ctx[None], jnp.zeros((8 - bsz - 1, d), F32)], axis=0)
    mod = _modulation(c_rows, ada_w[0], ada_b[0])
    tab = lambda k: mod[:, k * d:(k + 1) * d].reshape(8, 1, d)
    sh1, sc1, g1, sh2, sc2, g2 = (tab(k) for k in range(N_MOD))
    ctx_row = bsz

    tm_in = 512
    m_lat = bsz * l_lat
    x2 = x.reshape(m_lat, d)
    x_all = jnp.concatenate([x2, ctx.reshape(bsz * l_ctx, d)], axis=0)
    w_in_b = w_in[0].astype(BF16)
    n_mix = s5w + shift_cols
    lat_row = lambda t: (lambda i: i // (l_lat // t))

    def mix_mod_row(i):
        return jnp.where(i >= m_lat // tm_in, ctx_row, i // (l_lat // tm_in))

    u_blk = _lnmod_matmul(x_all, norm1_w[0], sh1, sc1, mix_mod_row, w_in_b[:, :s5w],
                          tm=tm_in, tn=s5w, out_dtype=F32, epilogue="lane_blocks", name="in_proj_s5")
    z_rw = _lnmod_matmul(x_all, norm1_w[0], sh1, sc1, mix_mod_row, w_in_b[:, s5w:n_mix],
                         tm=tm_in, tn=shift_cols // 3, out_dtype=F32, name="in_proj_rw")
    tm = 1024
    gates = _lnmod_matmul(x2, norm1_w[0], sh1, sc1, lat_row(tm), w_in_b[:, n_mix:],
                          tm=tm, tn=1024, out_dtype=BF16, epilogue="sigmoid", name="in_proj_gates")

    y_blk = _s5_branch_blocked(u_blk, bsz, l_ctx, l_lat, s5_a_re[0], s5_a_im[0], s5_log_dt[0],
                               s5_b_re[0], s5_b_im[0], s5_c_re[0], s5_c_im[0], s5_d[0])
    s5_out = _s5_glu(y_blk, s5_glu_w[0].astype(BF16), tm=tm, tn=1024)

    lora = rw_w2.shape[2]
    zl = jnp.zeros((lora, rw), F32)
    w2bd = jnp.concatenate([jnp.concatenate([rw_w2[0, 0], zl], axis=1),
                            jnp.concatenate([zl, rw_w2[0, 1]], axis=1)], axis=0)
    a2bd = jnp.concatenate([jnp.concatenate([rw_a2[0, 0], zl], axis=1),
                            jnp.concatenate([zl, rw_a2[0, 1]], axis=1)], axis=0)
    head_of = jnp.arange(rw) // RW_HEAD
    seg = (head_of[:, None] == jnp.arange(LANES)[None, :]).astype(BF16)
    segt = seg.T
    r, v, kk, g, bonus, lw, kd, be = _rw_prep(
        z_rw, bsz, l_ctx, l_lat, rw_mu[0], w2bd, a2bd, rw_g2[0], rw_w0[0].reshape(1, 2 * rw),
        rw_a0[0].reshape(1, 2 * rw), rw_k_k[0], rw_k_a[0], rw_r_k[0].reshape(rw), seg, segt)
    y_dirs = _rw_scan(r, v, kk, lw, kd, be, l_ctx)

    merged = _rw_merge(y_dirs.reshape(2, m_lat, rw), bonus.reshape(m_lat, rw), g.reshape(m_lat, rw),
                       rw_ln_w[0], rw_ln_b[0], seg, segt, gates, s5_out, rw_proj[0].astype(BF16),
                       tm=512, tn=1024)
    h1 = _resid_matmul(merged, w_o[0].astype(BF16), x2, g1, lat_row(tm), tm=tm, tn=1024)

    act = _lnmod_swiglu(h1, norm2_w[0], sh2, sc2, lat_row(tm), ffn_w13[0].astype(BF16), d_ff,
                        tm=tm, tn=512, name="ffn_up")
    tm_dn = 512
    out = _ffn_down(act, ffn_w2[0].astype(BF16), h1, g2, lat_row(tm_dn), norm_f, tm=tm_dn, tk=d_ff // 4)
    return out.reshape(bsz, l_lat, d)
```

```python
import functools
import math

import jax
import jax.numpy as jnp
from jax import lax
from jax.experimental import pallas as pl
from jax.experimental.pallas import tpu as pltpu

F32 = jnp.float32
BF16 = jnp.bfloat16
HIGHEST = lax.Precision.HIGHEST

D_MODEL = 2048
N_MOD = 6
NORM_EPS = 1e-6
GN_EPS = 64e-5
GRID_W = 64
S5_GROUP = 16
S5_STATE = 64
S5_CHUNK = 16
RW_HEAD = 64
RW_CHUNK = 64
RW_SUB = 2
LANES = 128
VMEM_LIMIT = 48 * 1024 * 1024


def _cparams(sem):
    return pltpu.CompilerParams(dimension_semantics=sem, vmem_limit_bytes=VMEM_LIMIT)


def _operands(a, b, precision):
    if precision == "bf16":
        return a.astype(BF16), b.astype(BF16), None
    return a, b, precision


def _dot(a, b, precision=None):
    a, b, precision = _operands(a, b, precision)
    return jnp.dot(a, b, preferred_element_type=F32, precision=precision)


def _dot_nt(a, b, precision=None):
    a, b, precision = _operands(a, b, precision)
    return lax.dot_general(a, b, (((1,), (1,)), ((), ())), preferred_element_type=F32, precision=precision)


def _dot_hilo(a, ind):
    hi = a.astype(BF16)
    lo = (a - hi.astype(F32)).astype(BF16)
    return _dot(hi, ind) + _dot(lo, ind)


def _sigmoid(x):
    return 1.0 / (1.0 + jnp.exp(-x))


def _silu(x):
    return x * _sigmoid(x)


def _gelu_tanh(x):
    c = math.sqrt(2.0 / math.pi)
    return 0.5 * x * (1.0 + jnp.tanh(c * (x + 0.044715 * (x * x * x))))


def _softplus(x):
    return jnp.maximum(x, 0.0) + jnp.log(1.0 + jnp.exp(-jnp.abs(x)))


def _mod_kernel(c_ref, w_ref, b_ref, o_ref):
    o_ref[...] = _dot(_silu(c_ref[...]), w_ref[...], HIGHEST) + b_ref[...]


def _modulation(c_rows, ada_w, ada_b):
    m, d = c_rows.shape
    n = ada_w.shape[1]
    tn = 1024
    return pl.pallas_call(
        _mod_kernel,
        out_shape=jax.ShapeDtypeStruct((m, n), F32),
        grid=(n // tn,),
        in_specs=[pl.BlockSpec((m, d), lambda j: (0, 0)),
                  pl.BlockSpec((d, tn), lambda j: (0, j)),
                  pl.BlockSpec((1, tn), lambda j: (0, j))],
        out_specs=pl.BlockSpec((m, tn), lambda j: (0, j)),
        compiler_params=_cparams(("arbitrary",)),
        name="modulation",
    )(c_rows, ada_w, ada_b.reshape(1, n))


def _lnmod_rows(x, nw, sh, sc):
    ms = jnp.mean(x * x, axis=-1, keepdims=True)
    y = x * lax.rsqrt(ms + NORM_EPS) * nw
    return y * (1.0 + sc) + sh


def _lnmod_mm_kernel(x_ref, nw_ref, sh_ref, sc_ref, w_ref, o_ref, h_scr, *, epilogue):
    @pl.when(pl.program_id(1) == 0)
    def _():
        h_scr[...] = _lnmod_rows(x_ref[...], nw_ref[...], sh_ref[0], sc_ref[0]).astype(BF16)

    z = _dot(h_scr[...], w_ref[...])
    if epilogue == "sigmoid":
        z = _sigmoid(z)
    if epilogue == "lane_blocks":
        for jb in range(o_ref.shape[0]):
            o_ref[jb] = z[:, jb * LANES:(jb + 1) * LANES].astype(o_ref.dtype)
    else:
        o_ref[...] = z.astype(o_ref.dtype)


def _lnmod_swiglu_kernel(x_ref, nw_ref, sh_ref, sc_ref, w1_ref, w3_ref, o_ref, h_scr):
    @pl.when(pl.program_id(1) == 0)
    def _():
        h_scr[...] = _lnmod_rows(x_ref[...], nw_ref[...], sh_ref[0], sc_ref[0]).astype(BF16)

    h = h_scr[...]
    o_ref[...] = (_silu(_dot(h, w1_ref[...])) * _dot(h, w3_ref[...])).astype(o_ref.dtype)


def _lnmod_matmul(x2, nw, sh_tab, sc_tab, mod_row_of_block, w, *, tm, tn, out_dtype, epilogue=None, name):
    m, d = x2.shape
    n = w.shape[1]
    mod_map = lambda i, j: (mod_row_of_block(i), 0, 0)
    if epilogue == "lane_blocks":
        assert tn == n
        out_shape = jax.ShapeDtypeStruct((n // LANES, m, LANES), out_dtype)
        out_spec = pl.BlockSpec((n // LANES, tm, LANES), lambda i, j: (0, i, 0))
    else:
        out_shape = jax.ShapeDtypeStruct((m, n), out_dtype)
        out_spec = pl.BlockSpec((tm, tn), lambda i, j: (i, j))
    return pl.pallas_call(
        functools.partial(_lnmod_mm_kernel, epilogue=epilogue),
        out_shape=out_shape,
        grid=(m // tm, n // tn),
        in_specs=[pl.BlockSpec((tm, d), lambda i, j: (i, 0)),
                  pl.BlockSpec((1, d), lambda i, j: (0, 0)),
                  pl.BlockSpec((1, 1, d), mod_map),
                  pl.BlockSpec((1, 1, d), mod_map),
                  pl.BlockSpec((d, tn), lambda i, j: (0, j))],
        out_specs=out_spec,
        scratch_shapes=[pltpu.VMEM((tm, d), BF16)],
        compiler_params=_cparams(("parallel", "arbitrary")),
        name=name,
    )(x2, nw.reshape(1, d), sh_tab, sc_tab, w)


def _lnmod_swiglu(x2, nw, sh_tab, sc_tab, mod_row_of_block, w13, d_ff, *, tm, tn, name):
    m, d = x2.shape
    nj = d_ff // tn
    mod_map = lambda i, j: (mod_row_of_block(i), 0, 0)
    return pl.pallas_call(
        _lnmod_swiglu_kernel,
        out_shape=jax.ShapeDtypeStruct((m, d_ff), BF16),
        grid=(m // tm, nj),
        in_specs=[pl.BlockSpec((tm, d), lambda i, j: (i, 0)),
                  pl.BlockSpec((1, d), lambda i, j: (0, 0)),
                  pl.BlockSpec((1, 1, d), mod_map),
                  pl.BlockSpec((1, 1, d), mod_map),
                  pl.BlockSpec((d, tn), lambda i, j: (0, j)),
                  pl.BlockSpec((d, tn), lambda i, j: (0, j + nj))],
        out_specs=pl.BlockSpec((tm, tn), lambda i, j: (i, j)),
        scratch_shapes=[pltpu.VMEM((tm, d), BF16)],
        compiler_params=_cparams(("parallel", "arbitrary")),
        name=name,
    )(x2, nw.reshape(1, d), sh_tab, sc_tab, w13, w13)


def _s5_param_kernel(are_ref, aim_ref, ldt_ref, bre_ref, bim_ref, cre_ref, cim_ref,
                     e_ref, cs_ref, kt_ref, a16_ref):
    t_n, hg, p_n = S5_CHUNK, S5_GROUP, S5_STATE
    tau = lax.broadcasted_iota(jnp.int32, (t_n, 1, p_n), 0).astype(F32)
    for d in range(2):
        a_re = are_ref[0, d:d + 1, :]
        a_im = aim_ref[0, d:d + 1, :]
        dt = jnp.exp(ldt_ref[0, d:d + 1, :])
        lam = a_re * dt
        th = a_im * dt
        er = jnp.exp(lam)
        ab_re = er * jnp.cos(th)
        ab_im = er * jnp.sin(th)
        den = a_re * a_re + a_im * a_im
        x_re = ab_re - 1.0
        co_re = (x_re * a_re + ab_im * a_im) / den
        co_im = (ab_im * a_re - x_re * a_im) / den
        bt_re = bre_ref[0, d]
        bt_im = bim_ref[0, d]
        bb_re = co_re * bt_re - co_im * bt_im
        bb_im = co_re * bt_im + co_im * bt_re
        c_re = cre_ref[0, d]
        c_im = cim_ref[0, d]

        def power(tv):
            mag = jnp.exp(tv * lam)
            return mag * jnp.cos(tv * th), mag * jnp.sin(tv * th)

        pw_re, pw_im = power(tau)
        cp_re = (c_re[None] * pw_re - c_im[None] * pw_im).reshape(t_n * hg, p_n)
        cp_im = (c_re[None] * pw_im + c_im[None] * pw_re).reshape(t_n * hg, p_n)
        kt_ref[0, d] = _dot_nt(bb_re, cp_re, HIGHEST) - _dot_nt(bb_im, cp_im, HIGHEST)

        te = (t_n - 1.0 - tau) if d == 0 else tau
        pe_re, pe_im = power(te)
        e_ref[0, d, :, 0:p_n] = (pe_re * bb_re[None] - pe_im * bb_im[None]).reshape(t_n * hg, p_n)
        e_ref[0, d, :, p_n:2 * p_n] = (pe_re * bb_im[None] + pe_im * bb_re[None]).reshape(t_n * hg, p_n)

        tc = (tau + 1.0) if d == 0 else (t_n - tau)
        pc_re, pc_im = power(tc)
        cs_ref[0, d, :, 0:p_n] = (c_re[None] * pc_re - c_im[None] * pc_im).reshape(t_n * hg, p_n)
        cs_ref[0, d, :, p_n:2 * p_n] = -(c_re[None] * pc_im + c_im[None] * pc_re).reshape(t_n * hg, p_n)

        mag16 = jnp.exp(float(t_n) * lam)
        a16_ref[0, d, 0:1, :] = mag16 * jnp.cos(float(t_n) * th)
        a16_ref[0, d, 1:2, :] = mag16 * jnp.sin(float(t_n) * th)


def _s5_params(a_re, a_im, log_dt, b_re, b_im, c_re, c_im):
    g_n = a_re.shape[1]
    p_n, hg, t_n = S5_STATE, S5_GROUP, S5_CHUNK
    tr = lambda a: jnp.swapaxes(a, 0, 1)
    ldt = jnp.broadcast_to(tr(log_dt)[:, :, None], (g_n, 2, p_n))
    spec3 = pl.BlockSpec((1, 2, p_n), lambda g: (g, 0, 0))
    spec4 = pl.BlockSpec((1, 2, hg, p_n), lambda g: (g, 0, 0, 0))
    th = t_n * hg
    return pl.pallas_call(
        _s5_param_kernel,
        out_shape=(jax.ShapeDtypeStruct((g_n, 2, th, 2 * p_n), F32),
                   jax.ShapeDtypeStruct((g_n, 2, th, 2 * p_n), F32),
                   jax.ShapeDtypeStruct((g_n, 2, hg, th), F32),
                   jax.ShapeDtypeStruct((g_n, 2, 2, p_n), F32)),
        grid=(g_n,),
        in_specs=[spec3, spec3, spec3, spec4, spec4, spec4, spec4],
        out_specs=(pl.BlockSpec((1, 2, th, 2 * p_n), lambda g: (g, 0, 0, 0)),
                   pl.BlockSpec((1, 2, th, 2 * p_n), lambda g: (g, 0, 0, 0)),
                   pl.BlockSpec((1, 2, hg, th), lambda g: (g, 0, 0, 0)),
                   pl.BlockSpec((1, 2, 2, p_n), lambda g: (g, 0, 0, 0))),
        compiler_params=_cparams(("parallel",)),
        name="s5_params",
    )(tr(a_re), tr(a_im), ldt,
      jnp.transpose(b_re, (1, 0, 3, 2)), jnp.transpose(b_im, (1, 0, 3, 2)), tr(c_re), tr(c_im))


def _s5_state_in_kernel(u_ref, e_ref, o_ref):
    e = e_ref[0]
    u = u_ref[0]
    o_ref[0, :, 0:LANES] = _dot(u, e[0], HIGHEST)
    o_ref[0, :, LANES:2 * LANES] = _dot(u, e[1], HIGHEST)


def _s5_state_inputs(u_g, e):
    g_n, rows, th = u_g.shape
    return pl.pallas_call(
        _s5_state_in_kernel,
        out_shape=jax.ShapeDtypeStruct((g_n, rows, 2 * LANES), F32),
        grid=(g_n,),
        in_specs=[pl.BlockSpec((1, rows, th), lambda g: (g, 0, 0)),
                  pl.BlockSpec((1, 2, th, LANES), lambda g: (g, 0, 0, 0))],
        out_specs=pl.BlockSpec((1, rows, 2 * LANES), lambda g: (g, 0, 0)),
        compiler_params=_cparams(("parallel",)),
        name="s5_state_inputs",
    )(u_g, e)


def _s5_scan_kernel(e_ref, a_ref, o_ref, *, n_ctx, n_all):
    zero = jnp.zeros(e_ref.shape[2:], F32)

    def step(pr, pi):
        ar, ai = a_ref[pr], a_ref[pi]

        def body(c, carry):
            sr, si = carry
            o_ref[pr, c] = sr
            o_ref[pi, c] = si
            return (ar * sr - ai * si + e_ref[pr, c], ar * si + ai * sr + e_ref[pi, c])
        return body

    lax.fori_loop(0, n_all, step(0, 1), (zero, zero))
    bwd = step(2, 3)
    carry = lax.fori_loop(0, n_ctx, lambda k, cy: bwd(n_ctx - 1 - k, cy), (zero, zero))
    lax.fori_loop(0, n_all - n_ctx, lambda k, cy: bwd(n_all - 1 - k, cy), carry)


def _s5_scan(e_planes, a_planes, n_ctx, n_all):
    _, rows, r_n, _ = e_planes.shape
    bsz = rows // n_all
    sub = 8
    return pl.pallas_call(
        functools.partial(_s5_scan_kernel, n_ctx=n_ctx, n_all=n_all),
        out_shape=jax.ShapeDtypeStruct(e_planes.shape, F32),
        grid=(bsz, r_n // sub),
        in_specs=[pl.BlockSpec((4, n_all, sub, LANES), lambda b, q: (0, b, q, 0)),
                  pl.BlockSpec((4, sub, LANES), lambda b, q: (0, q, 0))],
        out_specs=pl.BlockSpec((4, n_all, sub, LANES), lambda b, q: (0, b, q, 0)),
        compiler_params=_cparams(("parallel", "parallel")),
        name="s5_scan",
    )(e_planes, a_planes)


def _s5_apply_kernel(u_ref, s_ref, mf_ref, mb_ref, d_ref, cs_ref, o_ref, *, n_ctx):
    u = u_ref[0, n_ctx:, :]
    s = s_ref[0, n_ctx:, :]
    y = _dot(u, mf_ref[0] + mb_ref[0], HIGHEST) + u * d_ref[0]
    y = y + _dot_nt(s[:, 0:LANES], cs_ref[0, 0], HIGHEST) + _dot_nt(s[:, LANES:2 * LANES], cs_ref[0, 1], HIGHEST)
    o_ref[0, 0] = y


def _s5_apply(u_g, s_g, m_f, m_b, d_t, cs, n_ctx, n_all):
    g_n, rows, th = u_g.shape
    bsz = rows // n_all
    n_lat = n_all - n_ctx
    return pl.pallas_call(
        functools.partial(_s5_apply_kernel, n_ctx=n_ctx),
        out_shape=jax.ShapeDtypeStruct((g_n, bsz, n_lat, th), F32),
        grid=(g_n, bsz),
        in_specs=[pl.BlockSpec((1, n_all, th), lambda g, b: (g, b, 0)),
                  pl.BlockSpec((1, n_all, 2 * LANES), lambda g, b: (g, b, 0)),
                  pl.BlockSpec((1, th, th), lambda g, b: (g, 0, 0)),
                  pl.BlockSpec((1, th, th), lambda g, b: (g, 0, 0)),
                  pl.BlockSpec((1, 1, th), lambda g, b: (g, 0, 0)),
                  pl.BlockSpec((1, 2, th, LANES), lambda g, b: (g, 0, 0, 0))],
        out_specs=pl.BlockSpec((1, 1, n_lat, th), lambda g, b: (g, b, 0, 0)),
        compiler_params=_cparams(("parallel", "arbitrary")),
        name="s5_apply",
    )(u_g, s_g, m_f, m_b, d_t, cs)


def _s5_branch(u_all, n_ctx_tok, a_re, a_im, log_dt, b_re, b_im, c_re, c_im, s5_d):
    bsz, l_all, width = u_all.shape
    hg, t_n, p_n = S5_GROUP, S5_CHUNK, S5_STATE
    g_n = width // hg
    n_all = l_all // t_n
    n_ctx = n_ctx_tok // t_n
    th = t_n * hg
    e, cs, kt, a16 = _s5_params(a_re, a_im, log_dt, b_re, b_im, c_re, c_im)

    kt5 = kt.reshape(g_n, 2, hg, t_n, hg)
    ii = jnp.arange(t_n)[:, None]
    jj = jnp.arange(t_n)[None, :]

    def toeplitz(k4, lag, keep):
        m = k4[:, :, jnp.clip(lag, 0, t_n - 1), :]
        m = jnp.where(keep[None, None, :, :, None], m, 0.0)
        return jnp.transpose(m, (0, 2, 1, 3, 4)).reshape(g_n, th, th)

    m_f = toeplitz(kt5[:, 0], jj - ii, jj >= ii)
    m_b = toeplitz(kt5[:, 1], ii - jj, ii >= jj)
    d_t = jnp.tile(s5_d, (1, t_n)).reshape(g_n, 1, th)

    u_g = jnp.transpose(u_all.reshape(bsz, n_all, t_n, g_n, hg), (3, 0, 1, 2, 4)).reshape(g_n, bsz * n_all, th)
    e_cat = e
    s_in = _s5_state_inputs(u_g, e_cat)
    rows = bsz * n_all
    planes = jnp.transpose(s_in.reshape(g_n, rows, 4, p_n), (2, 1, 0, 3)).reshape(4, rows, g_n * p_n // LANES, LANES)
    a_pl = jnp.transpose(a16.reshape(g_n, 4, p_n), (1, 0, 2)).reshape(4, g_n * p_n // LANES, LANES)
    st = _s5_scan(planes, a_pl, n_ctx, n_all)
    s_g = jnp.transpose(st.reshape(4, rows, g_n, p_n), (2, 1, 0, 3)).reshape(g_n, rows, 4 * p_n)
    y_g = _s5_apply(u_g, s_g, m_f, m_b, d_t, cs, n_ctx, n_all)
    n_lat = n_all - n_ctx
    y = jnp.transpose(y_g.reshape(g_n, bsz, n_lat, t_n, hg), (1, 2, 3, 0, 4))
    return y.reshape(bsz, n_lat * t_n, width)


def _s5_param2_kernel(are_ref, aim_ref, ldt_ref, bre_ref, bim_ref, cre_ref, cim_ref,
                      e_ref, c_ref, m_ref, a16_ref):
    t_n, hg, p_n = S5_CHUNK, S5_GROUP, S5_STATE
    tau = lax.broadcasted_iota(jnp.int32, (t_n, 1, p_n), 0).astype(F32)
    taps = []
    for d in range(2):
        a_re = are_ref[0, d:d + 1, :]
        a_im = aim_ref[0, d:d + 1, :]
        dt = jnp.exp(ldt_ref[0, d:d + 1, :])
        lam = a_re * dt
        th = a_im * dt
        er = jnp.exp(lam)
        ab_re = er * jnp.cos(th)
        ab_im = er * jnp.sin(th)
        den = a_re * a_re + a_im * a_im
        x_re = ab_re - 1.0
        co_re = (x_re * a_re + ab_im * a_im) / den
        co_im = (ab_im * a_re - x_re * a_im) / den
        bt_re = bre_ref[0, d]
        bt_im = bim_ref[0, d]
        bb_re = co_re * bt_re - co_im * bt_im
        bb_im = co_re * bt_im + co_im * bt_re
        c_re = cre_ref[0, d]
        c_im = cim_ref[0, d]

        def power(tv):
            mag = jnp.exp(tv * lam)
            return mag * jnp.cos(tv * th), mag * jnp.sin(tv * th)

        pw_re, pw_im = power(tau if d == 0 else (t_n - 1.0 - tau))
        cp_re = (c_re[None] * pw_re - c_im[None] * pw_im).reshape(t_n * hg, p_n)
        cp_im = (c_re[None] * pw_im + c_im[None] * pw_re).reshape(t_n * hg, p_n)
        taps.append(_dot_nt(bb_re, cp_re, HIGHEST) - _dot_nt(bb_im, cp_im, HIGHEST))

        lo, hi = 2 * d * p_n, (2 * d + 1) * p_n
        pe_re, pe_im = power((t_n - 1.0 - tau) if d == 0 else tau)
        e_ref[0, :, 0, :, lo:hi] = pe_re * bb_re[None] - pe_im * bb_im[None]
        e_ref[0, :, 0, :, hi:hi + p_n] = pe_re * bb_im[None] + pe_im * bb_re[None]
        pc_re, pc_im = power((tau + 1.0) if d == 0 else (t_n - tau))
        c_ref[0, :, 0, :, lo:hi] = c_re[None] * pc_re - c_im[None] * pc_im
        c_ref[0, :, 0, :, hi:hi + p_n] = -(c_re[None] * pc_im + c_im[None] * pc_re)

        mag16 = jnp.exp(float(t_n) * lam)
        a16_ref[0, d, 0:1, :] = mag16 * jnp.cos(float(t_n) * th)
        a16_ref[0, d, 1:2, :] = mag16 * jnp.sin(float(t_n) * th)

    width = t_n * hg
    lane = lax.broadcasted_iota(jnp.int32, (hg, width), 1)
    for t in range(t_n):
        sf = t * hg
        sb = (t_n - 1 - t) * hg
        f = taps[0] if sf == 0 else jnp.where(lane >= sf, pltpu.roll(taps[0], sf, 1), 0.0)
        b = taps[1] if sb == 0 else jnp.where(lane < width - sb, pltpu.roll(taps[1], width - sb, 1), 0.0)
        m_ref[0, t, 0] = f + b


def _s5_params2(a_re, a_im, log_dt, b_re, b_im, c_re, c_im, nb):
    g_n = a_re.shape[1]
    gl_n = g_n // nb
    p_n, hg, t_n = S5_STATE, S5_GROUP, S5_CHUNK
    tr = lambda a: jnp.swapaxes(a, 0, 1)
    ldt = jnp.broadcast_to(tr(log_dt)[:, :, None], (g_n, 2, p_n))
    spec3 = pl.BlockSpec((1, 2, p_n), lambda g: (g, 0, 0))
    spec4 = pl.BlockSpec((1, 2, hg, p_n), lambda g: (g, 0, 0, 0))
    wide = 4 * p_n
    comp = jax.ShapeDtypeStruct((nb, t_n, gl_n, hg, wide), F32)
    comp_spec = pl.BlockSpec((1, t_n, 1, hg, wide), lambda g: (g // gl_n, 0, g % gl_n, 0, 0))
    return pl.pallas_call(
        _s5_param2_kernel,
        out_shape=(comp, comp, comp, jax.ShapeDtypeStruct((g_n, 2, 2, p_n), F32)),
        grid=(g_n,),
        in_specs=[spec3, spec3, spec3, spec4, spec4, spec4, spec4],
        out_specs=(comp_spec, comp_spec, comp_spec, pl.BlockSpec((1, 2, 2, p_n), lambda g: (g, 0, 0, 0))),
        compiler_params=_cparams(("parallel",)),
        name="s5_params",
    )(tr(a_re), tr(a_im), ldt,
      jnp.transpose(b_re, (1, 0, 3, 2)), jnp.transpose(b_im, (1, 0, 3, 2)), tr(c_re), tr(c_im))


def _expand_block_diag(comp, rep_ref, mask_ref, w_scr):
    k = w_scr.shape[0]
    period = mask_ref.shape[1]
    cb = comp.astype(BF16)
    step = 512
    for c0 in range(0, k, step):
        blk = _dot(cb, rep_ref[:, c0:c0 + step]).astype(BF16)
        for q0 in range(0, step, period):
            w_scr[:, c0 + q0:c0 + q0 + period] = blk[:, q0:q0 + period] * mask_ref[...]


def _s5_ein_kernel(u_ref, ec_ref, rep_ref, mask_ref, o_ref, w_scr):
    _expand_block_diag(ec_ref[0], rep_ref, mask_ref, w_scr)
    o_ref[0] = _dot(u_ref[0].astype(BF16), w_scr[...])


def _s5_chunk_inputs(u_rows, e_comp, rep_e, mask_e):
    nb, rows, k = u_rows.shape
    cw = e_comp.shape[2]
    return pl.pallas_call(
        _s5_ein_kernel,
        out_shape=jax.ShapeDtypeStruct((nb, rows, k), F32),
        grid=(nb,),
        in_specs=[pl.BlockSpec((1, rows, k), lambda j: (j, 0, 0)),
                  pl.BlockSpec((1, k, cw), lambda j: (j, 0, 0)),
                  pl.BlockSpec(rep_e.shape, lambda j: (0, 0)),
                  pl.BlockSpec(mask_e.shape, lambda j: (0, 0))],
        out_specs=pl.BlockSpec((1, rows, k), lambda j: (j, 0, 0)),
        scratch_shapes=[pltpu.VMEM((k, k), BF16)],
        compiler_params=_cparams(("parallel",)),
        name="s5_chunk_inputs",
    )(u_rows, e_comp, rep_e, mask_e)


def _s5_bscan_kernel(e_ref, a_ref, o_ref, *, bsz, n_ctx, n_lat):
    q = e_ref.shape[2] // 4
    planes = lambda row, d: (row[:, (2 * d) * q:(2 * d + 1) * q], row[:, (2 * d + 1) * q:(2 * d + 2) * q])
    coef = [planes(a_ref[0], d) for d in range(2)]
    ctx0 = bsz * n_lat

    def advance(state, rows):
        new = []
        for (sr, si), (b, d), row in zip(state, [(b, d) for b in range(bsz) for d in range(2)], rows):
            ar, ai = coef[d]
            er, ei = planes(e_ref[0, pl.ds(row, 1), :], d)
            new.append((ar * sr - ai * si + er, ar * si + ai * sr + ei))
        return tuple(new)

    def ctx_step(s, state):
        rows = [ctx0 + b * n_ctx + (s if d == 0 else n_ctx - 1 - s) for b in range(bsz) for d in range(2)]
        return advance(state, rows)

    def lat_step(s, state):
        rows = [b * n_lat + (s if d == 0 else n_lat - 1 - s) for b in range(bsz) for d in range(2)]
        for (sr, si), (b, d), row in zip(state, [(b, d) for b in range(bsz) for d in range(2)], rows):
            o_ref[0, pl.ds(row, 1), (2 * d) * q:(2 * d + 1) * q] = sr
            o_ref[0, pl.ds(row, 1), (2 * d + 1) * q:(2 * d + 2) * q] = si
        return advance(state, rows)

    zero = jnp.zeros((1, q), F32)
    state = tuple((zero, zero) for _ in range(2 * bsz))
    state = lax.fori_loop(0, n_ctx, ctx_step, state)
    lax.fori_loop(0, n_lat, lat_step, state)


def _s5_bscan(e_rows, a_rows, bsz, n_ctx, n_lat):
    nb, rows, k = e_rows.shape
    return pl.pallas_call(
        functools.partial(_s5_bscan_kernel, bsz=bsz, n_ctx=n_ctx, n_lat=n_lat),
        out_shape=jax.ShapeDtypeStruct((nb, bsz * n_lat, k), F32),
        grid=(nb,),
        in_specs=[pl.BlockSpec((1, rows, k), lambda j: (j, 0, 0)),
                  pl.BlockSpec((1, 1, k), lambda j: (j, 0, 0))],
        out_specs=pl.BlockSpec((1, bsz * n_lat, k), lambda j: (j, 0, 0)),
        compiler_params=_cparams(("parallel",)),
        name="s5_scan",
    )(e_rows, a_rows)


def _s5_out_kernel(u_ref, s_ref, mc_ref, cc_ref, d_ref, rep_m_ref, mask_m_ref, rep_e_ref, mask_e_ref,
                   o_ref, wm_scr, wc_scr):
    @pl.when(pl.program_id(1) == 0)
    def _():
        _expand_block_diag(mc_ref[0], rep_m_ref, mask_m_ref, wm_scr)
        _expand_block_diag(cc_ref[0], rep_e_ref, mask_e_ref, wc_scr)

    u = u_ref[0]
    y = _dot(u.astype(BF16), wm_scr[...]) + _dot_nt(s_ref[0].astype(BF16), wc_scr[...])
    o_ref[0] = y + u * d_ref[0]


def _s5_outputs(u_rows, s_rows, m_comp, c_comp, d_rows, rep_m, mask_m, rep_e, mask_e):
    nb, _, k = u_rows.shape
    rows = s_rows.shape[1]
    cw = m_comp.shape[2]
    tr = rows // 2
    const = lambda a: pl.BlockSpec(a.shape, lambda j, i: (0, 0))
    return pl.pallas_call(
        _s5_out_kernel,
        out_shape=jax.ShapeDtypeStruct((nb, rows, k), F32),
        grid=(nb, rows // tr),
        in_specs=[pl.BlockSpec((1, tr, k), lambda j, i: (j, i, 0)),
                  pl.BlockSpec((1, tr, k), lambda j, i: (j, i, 0)),
                  pl.BlockSpec((1, k, cw), lambda j, i: (j, 0, 0)),
                  pl.BlockSpec((1, k, cw), lambda j, i: (j, 0, 0)),
                  pl.BlockSpec((1, 1, k), lambda j, i: (j, 0, 0)),
                  const(rep_m), const(mask_m), const(rep_e), const(mask_e)],
        out_specs=pl.BlockSpec((1, tr, k), lambda j, i: (j, i, 0)),
        scratch_shapes=[pltpu.VMEM((k, k), BF16), pltpu.VMEM((k, k), BF16)],
        compiler_params=_cparams(("parallel", "arbitrary")),
        name="s5_outputs",
    )(u_rows, s_rows, m_comp, c_comp, d_rows, rep_m, mask_m, rep_e, mask_e)


def _s5_branch_blocked(u_blk, bsz, l_ctx, l_lat, a_re, a_im, log_dt, b_re, b_im, c_re, c_im, s5_d):
    nb, m_all, _ = u_blk.shape
    hg, t_n, p_n = S5_GROUP, S5_CHUNK, S5_STATE
    g_n = a_re.shape[1]
    gl_n = g_n // nb
    k = t_n * LANES
    n_lat = l_lat // t_n
    n_ctx = l_ctx // t_n
    e_c, c_c, m_c, a16 = _s5_params2(a_re, a_im, log_dt, b_re, b_im, c_re, c_im, nb)
    cw = 4 * p_n
    e_comp, c_comp, m_comp = (a.reshape(nb, k, cw) for a in (e_c, c_c, m_c))

    row_gl = (jnp.arange(k) // hg) % gl_n
    col = jnp.arange(k)
    src = jnp.arange(cw)
    rep_e = ((src[:, None] // p_n == col[None, :] // (gl_n * p_n)) & (src[:, None] % p_n == col[None, :] % p_n)).astype(BF16)
    mask_e = (row_gl[:, None] == (jnp.arange(gl_n * p_n)[None, :] // p_n)).astype(BF16)
    rep_m = ((src[:, None] // hg == col[None, :] // (gl_n * hg)) & (src[:, None] % hg == col[None, :] % hg)).astype(BF16)
    mask_m = (row_gl[:, None] == (jnp.arange(gl_n * hg)[None, :] // hg)).astype(BF16)

    d_rows = jnp.tile(s5_d.reshape(nb, 1, gl_n * hg), (1, 1, t_n))
    a_rows = jnp.transpose(a16.reshape(nb, gl_n, 2, 2, p_n), (0, 2, 3, 1, 4)).reshape(nb, 1, 4 * gl_n * p_n)

    u_rows = u_blk.reshape(nb, m_all // t_n, k)
    e_rows = _s5_chunk_inputs(u_rows, e_comp, rep_e, mask_e)
    s_rows = _s5_bscan(e_rows, a_rows, bsz, n_ctx, n_lat)
    y_rows = _s5_outputs(u_rows, s_rows, m_comp, c_comp, d_rows, rep_m, mask_m, rep_e, mask_e)
    return y_rows.reshape(nb, bsz * l_lat, LANES)


def _rw_prep_kernel(z_ref, zp_ref, zn_ref, mu_ref, w2_ref, a2_ref, g2_ref, w0_ref, a0_ref,
                    kk_w_ref, ka_ref, rk_ref, seg_ref, segt_ref,
                    r_ref, v_ref, kk_ref, g_ref, bonus_ref, lw_ref, kd_ref, be_ref,
                    *, tm, l_lat, rw):
    j = pl.program_id(1)
    z = z_ref[...]
    lat = j > 0
    tl = lax.broadcasted_iota(jnp.int32, (tm, 1), 0)
    tok = (j - 1) * tm + tl
    col = tl % GRID_W
    m_l = jnp.where(lat, col, tl) > 0
    m_r = jnp.where(lat, col - (GRID_W - 1), tl - (tm - 1)) < 0
    m_u = jnp.logical_and(lat, tok >= GRID_W)
    m_d = jnp.logical_and(lat, tok < l_lat - GRID_W)
    z_l = pltpu.roll(z, 1, 0)
    z_r = pltpu.roll(z, tm - 1, 0)
    z_u = jnp.concatenate([zp_ref[...], z[:tm - GRID_W]], axis=0)
    z_d = jnp.concatenate([z[GRID_W:], zn_ref[...]], axis=0)
    s = (jnp.where(m_l, z_l, 0.0) + jnp.where(m_r, z_r, 0.0)
         + jnp.where(m_u, z_u, 0.0) + jnp.where(m_d, z_d, 0.0))
    cnt = (m_l.astype(F32) + m_r.astype(F32)) + (m_u.astype(F32) + m_d.astype(F32))
    zs = z + (s / cnt - z) * mu_ref[...]

    r = zs[:, 0:rw]
    k = zs[:, rw:2 * rw]
    v = zs[:, 2 * rw:3 * rw]
    o = 3 * rw
    wd = zs[:, o:o + LANES]
    ad = zs[:, o + LANES:o + 2 * LANES]
    gd = zs[:, o + 2 * LANES:o + 3 * LANES]

    seg = seg_ref[...]
    segt = segt_ref[...]

    def head_sum(t):
        return _dot_hilo(_dot_hilo(t, seg), segt)

    g_ref[0] = _dot(_sigmoid(gd), g2_ref[...], "bf16")
    kk = k * kk_w_ref[...]
    kk = kk * lax.rsqrt(head_sum(kk * kk) + 1e-12)
    wl = w0_ref[...] + _dot(jnp.tanh(wd), w2_ref[...], "bf16")
    al = a0_ref[...] + _dot(ad, a2_ref[...], "bf16")
    r_ref[0] = r
    v_ref[0] = v
    kk_ref[0] = kk
    coef = jnp.zeros_like(r)
    for d in range(2):
        w_raw = -_softplus(-wl[:, d * rw:(d + 1) * rw]) - 0.5
        a = _sigmoid(al[:, d * rw:(d + 1) * rw])
        k_d = k * (1.0 + (a - 1.0) * ka_ref[...])
        lw_ref[d, 0] = -jnp.exp(w_raw)
        kd_ref[d, 0] = k_d
        be_ref[d, 0] = kk * a
        coef = coef + head_sum(r * k_d * rk_ref[...])
    bonus_ref[0] = coef * v


def _rw_prep(z_rw, bsz, l_ctx, l_lat, mu, w2bd, a2bd, g2, w0cat, a0cat, k_k, k_a, r_k_flat, seg, segt):
    cols = z_rw.shape[1]
    tm = l_ctx
    rw = g2.shape[1]
    l_all = l_ctx + l_lat
    nblk = l_all // tm
    lat_blk = l_lat // tm
    hb = tm // GRID_W
    lat_hblk = l_lat // GRID_W

    def main_blk(b, j):
        return jnp.where(j == 0, bsz * lat_blk + b, b * lat_blk + j - 1)

    def prev_halo(b, j):
        return b * lat_hblk + jnp.maximum((j - 1) * hb - 1, 0)

    def next_halo(b, j):
        return b * lat_hblk + jnp.minimum(jnp.maximum(j, 1) * hb, lat_hblk - 1)

    full = lambda shape: pl.BlockSpec(shape, lambda b, j: (0,) * len(shape))
    shared = jax.ShapeDtypeStruct((bsz, l_all, rw), F32)
    lat_only = jax.ShapeDtypeStruct((bsz, l_lat, rw), F32)
    per_dir = jax.ShapeDtypeStruct((2, bsz, l_all, rw), F32)
    o_shared = pl.BlockSpec((1, tm, rw), lambda b, j: (b, j, 0))
    o_lat = pl.BlockSpec((1, tm, rw), lambda b, j: (b, jnp.maximum(j - 1, 0), 0))
    o_dir = pl.BlockSpec((2, 1, tm, rw), lambda b, j: (0, b, j, 0))
    return pl.pallas_call(
        functools.partial(_rw_prep_kernel, tm=tm, l_lat=l_lat, rw=rw),
        out_shape=(shared,) * 3 + (lat_only,) * 2 + (per_dir,) * 3,
        grid=(bsz, nblk),
        in_specs=[pl.BlockSpec((tm, cols), lambda b, j: (main_blk(b, j), 0)),
                  pl.BlockSpec((GRID_W, cols), lambda b, j: (prev_halo(b, j), 0)),
                  pl.BlockSpec((GRID_W, cols), lambda b, j: (next_halo(b, j), 0)),
                  full((1, cols)), full(w2bd.shape), full(a2bd.shape), full(g2.shape),
                  full((1, 2 * rw)), full((1, 2 * rw)), full((1, rw)), full((1, rw)), full((1, rw)),
                  full(seg.shape), full(segt.shape)],
        out_specs=(o_shared,) * 3 + (o_lat,) * 2 + (o_dir,) * 3,
        compiler_params=_cparams(("parallel", "arbitrary")),
        name="rwkv_prep",
    )(z_rw, z_rw, z_rw, mu.reshape(1, cols), w2bd, a2bd, g2, w0cat, a0cat,
      k_k.reshape(1, rw), k_a.reshape(1, rw), r_k_flat.reshape(1, rw), seg, segt)


def _stack_heads(x, head0):
    return jnp.concatenate([jnp.where(head0, x, 0.0), jnp.where(head0, 0.0, x)], axis=0)


def _rw_chunk_kernel(r_ref, v_ref, kk_ref, lw_ref, kd_ref, be_ref, y_ref,
                     z_scr, gc_s, lhs_s, kb_s, kht_s, bht_s, vs_s, rt_s, aab_s, akr_s, arb_s, t_s, x_s, xin_s,
                     yc_s, qy_s, bw_s, n_s, y_s, *, n_pairs, n_sub):
    c_n = RW_CHUNK
    n2 = 2 * c_n
    rev = (pl.program_id(0) % 2) == 1

    @pl.when(pl.program_id(1) == 0)
    def _():
        z_scr[...] = jnp.zeros_like(z_scr)

    ri = lax.broadcasted_iota(jnp.int32, (c_n, c_n), 0)
    ci = lax.broadcasted_iota(jnp.int32, (c_n, c_n), 1)
    tri = (jnp.where(rev, ci - ri, ri - ci) >= 0).astype(F32)
    r2 = lax.broadcasted_iota(jnp.int32, (n2, n2), 0)
    c2 = lax.broadcasted_iota(jnp.int32, (n2, n2), 1)
    t2 = r2 % c_n
    i2 = c2 % c_n
    same_head = (r2 // c_n) == (c2 // c_n)
    before = jnp.logical_and(same_head, jnp.where(rev, i2 - t2, t2 - i2) > 0)
    upto = jnp.logical_or(before, r2 == c2)
    eye = (r2 == c2).astype(F32)
    head0 = lax.broadcasted_iota(jnp.int32, (1, LANES), 1) < RW_HEAD

    def blk(s):
        return (r2 // s) == (c2 // s)

    pairs = range(n_pairs)
    units = range(n_sub * n_pairs)
    lanes_of = lambda p: slice(p * LANES, (p + 1) * LANES)

    row_of = lambda k: pl.multiple_of(jnp.where(rev, n_sub - 1 - k, k) * c_n, c_n)
    for cc in range(n_sub):
        rows = pl.ds(row_of(cc), c_n)
        lw = lw_ref[0, 0, rows, :]
        cum = _dot(tri, lw, HIGHEST)
        tot = jnp.sum(lw, axis=0, keepdims=True)
        g_inv = jnp.exp(-cum)
        g_hat = jnp.exp(tot - cum)
        kd = kd_ref[0, 0, rows, :]
        be = be_ref[0, 0, rows, :]
        at = kk_ref[0, rows, :] * jnp.exp(cum - lw)
        rt = r_ref[0, rows, :] * jnp.exp(cum)
        kt = kd * g_inv
        bt = be * g_inv
        kh = kd * g_hat
        bh = be * g_hat
        gc_s[cc] = jnp.exp(tot)
        v = v_ref[0, rows, :]
        for p in pairs:
            u = cc * n_pairs + p
            st = lambda x: _stack_heads(x[:, lanes_of(p)], head0)
            rt_p = st(rt)
            at_p = st(at).astype(BF16)
            lhs_s[u, :n2] = at_p
            lhs_s[u, n2:] = rt_p.astype(BF16)
            xin_s[u, :, :LANES] = at_p
            rt_s[u] = rt_p
            kb_s[u, :n2] = st(kt).astype(BF16)
            kb_s[u, n2:] = st(bt).astype(BF16)
            kht_s[u] = st(kh).T.astype(BF16)
            bht_s[u] = st(bh).T.astype(BF16)
            vs_s[u] = st(v).astype(BF16)

    for u in units:
        g = _dot_nt(lhs_s[u], kb_s[u])
        a_ab = jnp.where(before, g[:n2, n2:], 0.0)
        akr_s[u, :n2] = jnp.where(before, g[:n2, :n2], 0.0).astype(BF16)
        akr_s[u, n2:] = jnp.where(upto, g[n2:, :n2], 0.0).astype(BF16)
        arb_s[u] = jnp.where(upto, g[n2:, n2:], 0.0).astype(BF16)
        aab_s[u] = a_ab
        t_s[u] = eye - jnp.where(blk(2), a_ab, 0.0)

    s = 2
    while s < c_n:
        off = jnp.logical_and(blk(2 * s), jnp.logical_not(blk(s)))
        for u in units:
            x_s[u] = _dot(t_s[u].astype(BF16), jnp.where(off, aab_s[u], 0.0).astype(BF16)).astype(BF16)
        for u in units:
            t = t_s[u]
            t_s[u] = t - _dot(x_s[u], t.astype(BF16))
        s *= 2

    for u in units:
        av = _dot(akr_s[u], vs_s[u])
        xin_s[u, :, LANES:] = av[:n2].astype(BF16)
        yc_s[u] = av[n2:]

    for u in units:
        wu = _dot(t_s[u].astype(BF16), xin_s[u]).astype(BF16)
        q = _dot(arb_s[u], wu)
        bwu = _dot(bht_s[u], wu)
        qy_s[u] = (rt_s[u] - q[:, :LANES]).astype(BF16)
        yc_s[u] = yc_s[u] - q[:, LANES:]
        bw_s[u] = bwu[:, :LANES]
        n_s[u] = _dot(kht_s[u], vs_s[u]) - bwu[:, LANES:]

    for k in range(n_sub):
        g_c = gc_s[k]
        for p in pairs:
            u = k * n_pairs + p
            z0 = z_scr[p].astype(BF16)
            y_p = _dot(qy_s[u], z0) + yc_s[u]
            y_s[k, :, lanes_of(p)] = y_p[:c_n] + y_p[c_n:]
            m_z = eye * g_c[:, lanes_of(p)] - bw_s[u]
            z_scr[p] = _dot(m_z.astype(BF16), z0) + n_s[u]
    for k in range(n_sub):
        y_ref[0, 0, pl.ds(row_of(k), c_n), :] = y_s[k]


def _rw_scan(r, v, kk, lw, kd, be, l_ctx):
    bsz, l_all, rw = r.shape
    n_sub = RW_SUB
    c_n = RW_CHUNK
    rows = n_sub * c_n
    n_all = l_all // rows
    n_ctx = l_ctx // rows
    n_lat = n_all - n_ctx
    n_pairs = rw // LANES
    n2 = 2 * c_n
    vm = lambda nr, cols, dt: pltpu.VMEM((n_sub * n_pairs, nr, cols), dt)

    def chunk_of(bd, s):
        fwd = s
        bwd = jnp.where(s < n_ctx, n_ctx - 1 - s, n_all + n_ctx - 1 - s)
        return jnp.where(bd % 2 == 0, fwd, bwd)

    def out_chunk(bd, s):
        c = chunk_of(bd, s)
        edge = jnp.where(bd % 2 == 0, 0, n_lat - 1)
        return jnp.where(s < n_ctx, edge, c - n_ctx)

    shared = pl.BlockSpec((1, rows, rw), lambda bd, s: (bd // 2, chunk_of(bd, s), 0))
    per_dir = pl.BlockSpec((1, 1, rows, rw), lambda bd, s: (bd % 2, bd // 2, chunk_of(bd, s), 0))
    return pl.pallas_call(
        functools.partial(_rw_chunk_kernel, n_pairs=n_pairs, n_sub=n_sub),
        out_shape=jax.ShapeDtypeStruct((2, bsz, n_lat * rows, rw), F32),
        grid=(2 * bsz, n_all),
        in_specs=[shared, shared, shared, per_dir, per_dir, per_dir],
        out_specs=pl.BlockSpec((1, 1, rows, rw), lambda bd, s: (bd % 2, bd // 2, out_chunk(bd, s), 0)),
        scratch_shapes=[pltpu.VMEM((n_pairs, n2, n2), F32),
                        pltpu.VMEM((n_sub, 1, rw), F32),
                        vm(2 * n2, n2, BF16),
                        vm(2 * n2, n2, BF16),
                        vm(n2, n2, BF16),
                        vm(n2, n2, BF16),
                        vm(n2, n2, BF16),
                        vm(n2, n2, F32),
                        vm(n2, n2, F32),
                        vm(2 * n2, n2, BF16),
                        vm(n2, n2, BF16),
                        vm(n2, n2, F32),
                        vm(n2, n2, BF16),
                        vm(n2, 2 * n2, BF16),
                        vm(n2, n2, F32),
                        vm(n2, n2, BF16),
                        vm(n2, n2, F32),
                        vm(n2, n2, F32),
                        pltpu.VMEM((n_sub, c_n, rw), F32)],
        compiler_params=_cparams(("parallel", "arbitrary")),
        name="rwkv_scan",
    )(r, v, kk, lw, kd, be)


def _rw_step_kernel(rf_ref, rb_ref, vf_ref, vb_ref, kkf_ref, kkb_ref, lwf_ref, lwb_ref, kdf_ref, kdb_ref,
                    bef_ref, beb_ref, yf_ref, yb_ref,
                    z_scr, gc_s, lhs_s, kb_s, kht_s, bht_s, vs_s, rt_s, aab_s, akr_s, arb_s, t_s, x_s, xin_s,
                    yc_s, qm_s, yn_s, *, n_pairs, bsz):
    c_n = RW_CHUNK
    n2 = 2 * c_n

    lanes_of = lambda p: slice(p * LANES, (p + 1) * LANES)
    unit = lambda d, b, p: (d * bsz + b) * n_pairs + p
    units = [(d, b, p) for d in range(2) for b in range(bsz) for p in range(n_pairs)]

    @pl.when(pl.program_id(0) == 0)
    def _():
        z_scr[...] = jnp.zeros_like(z_scr)
        qm_s[...] = jnp.zeros_like(qm_s)
        yn_s[...] = jnp.zeros_like(yn_s)

    for d, b, p in units:
        u = unit(d, b, p)
        y_ref = yf_ref if d == 0 else yb_ref
        yz = _dot(qm_s[u], z_scr[u].astype(BF16)) + yn_s[u]
        y_ref[b, :, lanes_of(p)] = yz[:c_n] + yz[c_n:n2]
        z_scr[u] = yz[n2:]

    ri = lax.broadcasted_iota(jnp.int32, (c_n, c_n), 0)
    ci = lax.broadcasted_iota(jnp.int32, (c_n, c_n), 1)
    r2 = lax.broadcasted_iota(jnp.int32, (n2, n2), 0)
    c2 = lax.broadcasted_iota(jnp.int32, (n2, n2), 1)
    t2 = r2 % c_n
    i2 = c2 % c_n
    same_head = (r2 // c_n) == (c2 // c_n)
    diag = r2 == c2
    eye = diag.astype(F32)
    head0 = lax.broadcasted_iota(jnp.int32, (1, LANES), 1) < RW_HEAD
    tri = [(ri >= ci).astype(F32), (ri <= ci).astype(F32)]
    before = [jnp.logical_and(same_head, i2 < t2), jnp.logical_and(same_head, i2 > t2)]
    upto = [jnp.logical_or(m, diag) for m in before]

    def blk(s):
        return (r2 // s) == (c2 // s)

    srcs = [(rf_ref, vf_ref, kkf_ref, lwf_ref, kdf_ref, bef_ref), (rb_ref, vb_ref, kkb_ref, lwb_ref, kdb_ref, beb_ref)]

    for d in range(2):
        r_ref, v_ref, kk_ref, lw_ref, kd_ref, be_ref = srcs[d]
        for b in range(bsz):
            lw = lw_ref[0, b]
            cum = _dot(tri[d], lw, HIGHEST)
            tot = jnp.sum(lw, axis=0, keepdims=True)
            g_inv = jnp.exp(-cum)
            g_hat = jnp.exp(tot - cum)
            kd = kd_ref[0, b]
            be = be_ref[0, b]
            at = kk_ref[b] * jnp.exp(cum - lw)
            rt = r_ref[b] * jnp.exp(cum)
            kt = kd * g_inv
            bt = be * g_inv
            kh = kd * g_hat
            bh = be * g_hat
            gc_s[d * bsz + b] = jnp.exp(tot)
            v = v_ref[b]
            for p in range(n_pairs):
                u = unit(d, b, p)
                st = lambda x: _stack_heads(x[:, lanes_of(p)], head0)
                rt_p = st(rt)
                at_p = st(at).astype(BF16)
                lhs_s[u, :n2] = at_p
                lhs_s[u, n2:] = rt_p.astype(BF16)
                xin_s[u, :, :LANES] = at_p
                rt_s[u] = rt_p
                kb_s[u, :n2] = st(kt).astype(BF16)
                kb_s[u, n2:] = st(bt).astype(BF16)
                kht_s[u] = st(kh).T.astype(BF16)
                bht_s[u] = st(bh).T.astype(BF16)
                vs_s[u] = st(v).astype(BF16)

    for d, b, p in units:
        u = unit(d, b, p)
        g = _dot_nt(lhs_s[u], kb_s[u])
        a_ab = jnp.where(before[d], g[:n2, n2:], 0.0)
        akr_s[u, :n2] = jnp.where(before[d], g[:n2, :n2], 0.0).astype(BF16)
        akr_s[u, n2:] = jnp.where(upto[d], g[n2:, :n2], 0.0).astype(BF16)
        arb_s[u] = jnp.where(upto[d], g[n2:, n2:], 0.0).astype(BF16)
        aab_s[u] = a_ab
        t_s[u] = eye - jnp.where(blk(2), a_ab, 0.0)

    n_units = len(units)
    s = 2
    while s < c_n:
        off = jnp.logical_and(blk(2 * s), jnp.logical_not(blk(s)))
        for u in range(n_units):
            x_s[u] = _dot(t_s[u].astype(BF16), jnp.where(off, aab_s[u], 0.0).astype(BF16)).astype(BF16)
        for u in range(n_units):
            t = t_s[u]
            t_s[u] = t - _dot(x_s[u], t.astype(BF16))
        s *= 2

    for u in range(n_units):
        av = _dot(akr_s[u], vs_s[u])
        xin_s[u, :, LANES:] = av[:n2].astype(BF16)
        yc_s[u] = av[n2:]

    for d, b, p in units:
        u = unit(d, b, p)
        wu = _dot(t_s[u].astype(BF16), xin_s[u]).astype(BF16)
        q = _dot(arb_s[u], wu)
        bwu = _dot(bht_s[u], wu)
        qm_s[u, :n2] = (rt_s[u] - q[:, :LANES]).astype(BF16)
        qm_s[u, n2:] = (eye * gc_s[d * bsz + b][:, lanes_of(p)] - bwu[:, :LANES]).astype(BF16)
        yn_s[u, :n2] = yc_s[u] - q[:, LANES:]
        yn_s[u, n2:] = _dot(kht_s[u], vs_s[u]) - bwu[:, LANES:]


def _rw_scan2(r, v, kk, lw, kd, be, l_ctx):
    bsz, l_all, rw = r.shape
    c_n = RW_CHUNK
    n_all = l_all // c_n
    n_ctx = l_ctx // c_n
    n_lat = n_all - n_ctx
    n_pairs = rw // LANES
    n_units = 2 * bsz * n_pairs
    n2 = 2 * c_n
    vm = lambda nr, cols, dt: pltpu.VMEM((n_units, nr, cols), dt)

    clamp = lambda s: jnp.minimum(s, n_all - 1)
    chunk_f = lambda s: clamp(s)
    chunk_b = lambda s: jnp.where(clamp(s) < n_ctx, n_ctx - 1 - clamp(s), n_all + n_ctx - 1 - clamp(s))
    prev = lambda s: jnp.maximum(s - 1, 0)
    out_f = lambda s: jnp.maximum(prev(s) - n_ctx, 0)
    out_b = lambda s: jnp.where(prev(s) < n_ctx, n_lat - 1, chunk_b(prev(s)) - n_ctx)

    sh_f = pl.BlockSpec((bsz, c_n, rw), lambda s: (0, chunk_f(s), 0))
    sh_b = pl.BlockSpec((bsz, c_n, rw), lambda s: (0, chunk_b(s), 0))
    pd_f = pl.BlockSpec((1, bsz, c_n, rw), lambda s: (0, 0, chunk_f(s), 0))
    pd_b = pl.BlockSpec((1, bsz, c_n, rw), lambda s: (1, 0, chunk_b(s), 0))
    y_shape = jax.ShapeDtypeStruct((bsz, n_lat * c_n, rw), F32)
    return pl.pallas_call(
        functools.partial(_rw_step_kernel, n_pairs=n_pairs, bsz=bsz),
        out_shape=(y_shape, y_shape),
        grid=(n_all + 1,),
        in_specs=[sh_f, sh_b, sh_f, sh_b, sh_f, sh_b, pd_f, pd_b, pd_f, pd_b, pd_f, pd_b],
        out_specs=(pl.BlockSpec((bsz, c_n, rw), lambda s: (0, out_f(s), 0)),
                   pl.BlockSpec((bsz, c_n, rw), lambda s: (0, out_b(s), 0))),
        scratch_shapes=[vm(n2, n2, F32),
                        pltpu.VMEM((2 * bsz, 1, rw), F32),
                        vm(2 * n2, n2, BF16),
                        vm(2 * n2, n2, BF16),
                        vm(n2, n2, BF16),
                        vm(n2, n2, BF16),
                        vm(n2, n2, BF16),
                        vm(n2, n2, F32),
                        vm(n2, n2, F32),
                        vm(2 * n2, n2, BF16),
                        vm(n2, n2, BF16),
                        vm(n2, n2, F32),
                        vm(n2, n2, BF16),
                        vm(n2, 2 * n2, BF16),
                        vm(n2, n2, F32),
                        vm(2 * n2, n2, BF16),
                        vm(2 * n2, n2, F32)],
        compiler_params=_cparams(("arbitrary",)),
        name="rwkv_scan",
    )(r, r, v, v, kk, kk, lw, lw, kd, kd, be, be)


def _s5_glu_kernel(y_ref, wa_ref, wb_ref, o_ref, h_scr):
    @pl.when(pl.program_id(1) == 0)
    def _():
        for jb in range(y_ref.shape[0]):
            h_scr[:, jb * LANES:(jb + 1) * LANES] = _gelu_tanh(y_ref[jb]).astype(BF16)

    h = h_scr[...]
    o_ref[...] = (_dot(h, wa_ref[...]) * _sigmoid(_dot(h, wb_ref[...]))).astype(o_ref.dtype)


def _s5_glu(y_blk, w, *, tm, tn):
    nb, m, _ = y_blk.shape
    k = nb * LANES
    n = w.shape[1] // 2
    nj = n // tn
    return pl.pallas_call(
        _s5_glu_kernel,
        out_shape=jax.ShapeDtypeStruct((m, n), BF16),
        grid=(m // tm, nj),
        in_specs=[pl.BlockSpec((nb, tm, LANES), lambda i, j: (0, i, 0)),
                  pl.BlockSpec((k, tn), lambda i, j: (0, j)),
                  pl.BlockSpec((k, tn), lambda i, j: (0, j + nj))],
        out_specs=pl.BlockSpec((tm, tn), lambda i, j: (i, j)),
        scratch_shapes=[pltpu.VMEM((tm, k), BF16)],
        compiler_params=_cparams(("parallel", "arbitrary")),
        name="s5_glu",
    )(y_blk, w, w)


def _rw_merge_kernel(yf_ref, yb_ref, bonus_ref, g_ref, lnw_ref, lnb_ref, seg_ref, segt_ref,
                     ga_ref, gb_ref, s5_ref, w_ref, o_ref, h_scr):
    @pl.when(pl.program_id(1) == 0)
    def _():
        seg = seg_ref[...]
        segt = segt_ref[...]
        inv_n = 1.0 / RW_HEAD

        def head_mean(t):
            return _dot_hilo(_dot_hilo(t, seg), segt) * inv_n

        y = yf_ref[...] + yb_ref[...]
        dy = y - head_mean(y)
        var = head_mean(dy * dy)
        y = dy * lax.rsqrt(var + GN_EPS) * lnw_ref[...] + lnb_ref[...] + bonus_ref[...]
        h_scr[...] = (y * g_ref[...]).astype(BF16)

    rw_out = _dot(h_scr[...], w_ref[...])
    merged = ga_ref[...].astype(F32) * s5_ref[...].astype(F32) + gb_ref[...].astype(F32) * rw_out
    o_ref[...] = merged.astype(o_ref.dtype)


def _rw_merge(y_f, y_b, bonus, g, ln_w, ln_b, seg, segt, gates, s5_out, w_proj, *, tm, tn):
    m, rw = y_f.shape
    n = w_proj.shape[1]
    nj = n // tn
    full = lambda shape: pl.BlockSpec(shape, lambda i, j: (0,) * len(shape))
    return pl.pallas_call(
        _rw_merge_kernel,
        out_shape=jax.ShapeDtypeStruct((m, n), BF16),
        grid=(m // tm, nj),
        in_specs=[pl.BlockSpec((tm, rw), lambda i, j: (i, 0)),
                  pl.BlockSpec((tm, rw), lambda i, j: (i, 0)),
                  pl.BlockSpec((tm, rw), lambda i, j: (i, 0)),
                  pl.BlockSpec((tm, rw), lambda i, j: (i, 0)),
                  full((1, rw)), full((1, rw)), full(seg.shape), full(segt.shape),
                  pl.BlockSpec((tm, tn), lambda i, j: (i, j)),
                  pl.BlockSpec((tm, tn), lambda i, j: (i, j + nj)),
                  pl.BlockSpec((tm, tn), lambda i, j: (i, j)),
                  pl.BlockSpec((rw, tn), lambda i, j: (0, j))],
        out_specs=pl.BlockSpec((tm, tn), lambda i, j: (i, j)),
        scratch_shapes=[pltpu.VMEM((tm, rw), BF16)],
        compiler_params=_cparams(("parallel", "arbitrary")),
        name="rwkv_merge",
    )(y_f, y_b, bonus, g, ln_w.reshape(1, rw), ln_b.reshape(1, rw), seg, segt,
      gates, gates, s5_out, w_proj)


def _resid_mm_kernel(a_ref, w_ref, x_ref, g_ref, o_ref):
    o_ref[...] = x_ref[...] + g_ref[0] * _dot(a_ref[...], w_ref[...])


def _resid_matmul(a, w, x2, g_tab, mod_row_of_block, *, tm, tn):
    m, k = a.shape
    n = w.shape[1]
    return pl.pallas_call(
        _resid_mm_kernel,
        out_shape=jax.ShapeDtypeStruct((m, n), F32),
        grid=(m // tm, n // tn),
        in_specs=[pl.BlockSpec((tm, k), lambda i, j: (i, 0)),
                  pl.BlockSpec((k, tn), lambda i, j: (0, j)),
                  pl.BlockSpec((tm, tn), lambda i, j: (i, j)),
                  pl.BlockSpec((1, 1, tn), lambda i, j: (mod_row_of_block(i), 0, j))],
        out_specs=pl.BlockSpec((tm, tn), lambda i, j: (i, j)),
        compiler_params=_cparams(("parallel", "arbitrary")),
        name="out_proj",
    )(a, w, x2, g_tab)


def _ffn_down_kernel(a_ref, w_ref, x_ref, g_ref, nf_ref, o_ref, acc_ref):
    kk = pl.program_id(1)

    @pl.when(kk == 0)
    def _():
        acc_ref[...] = jnp.zeros_like(acc_ref)

    acc_ref[...] += _dot(a_ref[...], w_ref[...])

    @pl.when(kk == pl.num_programs(1) - 1)
    def _():
        h = x_ref[...] + g_ref[0] * acc_ref[...]
        ms = jnp.mean(h * h, axis=-1, keepdims=True)
        o_ref[...] = h * lax.rsqrt(ms + NORM_EPS) * nf_ref[...]


def _ffn_down(a, w, x2, g_tab, mod_row_of_block, norm_f, *, tm, tk):
    m, k = a.shape
    n = w.shape[1]
    return pl.pallas_call(
        _ffn_down_kernel,
        out_shape=jax.ShapeDtypeStruct((m, n), F32),
        grid=(m // tm, k // tk),
        in_specs=[pl.BlockSpec((tm, tk), lambda i, kk: (i, kk)),
                  pl.BlockSpec((tk, n), lambda i, kk: (kk, 0)),
                  pl.BlockSpec((tm, n), lambda i, kk: (i, 0)),
                  pl.BlockSpec((1, 1, n), lambda i, kk: (mod_row_of_block(i), 0, 0)),
                  pl.BlockSpec((1, n), lambda i, kk: (0, 0))],
        out_specs=pl.BlockSpec((tm, n), lambda i, kk: (i, 0)),
        scratch_shapes=[pltpu.VMEM((tm, n), F32)],
        compiler_params=_cparams(("parallel", "arbitrary")),
        name="ffn_down",
    )(a, w, x2, g_tab, norm_f.reshape(1, n))


def kernel(x, c, ctx, c_ctx, ada_w, ada_b, norm1_w, w_in, rw_mu, s5_a_re, s5_a_im, s5_log_dt, s5_b_re, s5_b_im, s5_c_re, s5_c_im, s5_d, s5_glu_w, rw_w0, rw_w2, rw_a0, rw_a2, rw_g2, rw_k_k, rw_k_a, rw_r_k, rw_ln_w, rw_ln_b, rw_proj, w_o, norm2_w, ffn_w13, ffn_w2, norm_f):
    assert ada_w.shape[0] == 1, "single-layer block"
    bsz, l_lat, d = x.shape
    l_ctx = ctx.shape[1]
    l_all = l_ctx + l_lat
    s5w = s5_d.shape[1] * s5_d.shape[2]
    rw = rw_g2.shape[2]
    shift_cols = rw_mu.shape[1]
    d_ff = ffn_w2.shape[1]
    n_heads = rw // RW_HEAD

    c_rows = jnp.concatenate([c, c_ctx[None], jnp.zeros((8 - bsz - 1, d), F32)], axis=0)
    mod = _modulation(c_rows, ada_w[0], ada_b[0])
    tab = lambda k: mod[:, k * d:(k + 1) * d].reshape(8, 1, d)
    sh1, sc1, g1, sh2, sc2, g2 = (tab(k) for k in range(N_MOD))
    ctx_row = bsz

    tm_in = 512
    m_lat = bsz * l_lat
    x2 = x.reshape(m_lat, d)
    x_all = jnp.concatenate([x2, ctx.reshape(bsz * l_ctx, d)], axis=0)
    w_in_b = w_in[0].astype(BF16)
    n_mix = s5w + shift_cols
    lat_row = lambda t: (lambda i: i // (l_lat // t))

    def mix_mod_row(i):
        return jnp.where(i >= m_lat // tm_in, ctx_row, i // (l_lat // tm_in))

    u_blk = _lnmod_matmul(x_all, norm1_w[0], sh1, sc1, mix_mod_row, w_in_b[:, :s5w],
                          tm=tm_in, tn=s5w, out_dtype=F32, epilogue="lane_blocks", name="in_proj_s5")
    z_rw = _lnmod_matmul(x_all, norm1_w[0], sh1, sc1, mix_mod_row, w_in_b[:, s5w:n_mix],
                         tm=tm_in, tn=shift_cols // 3, out_dtype=F32, name="in_proj_rw")
    tm = 1024
    gates = _lnmod_matmul(x2, norm1_w[0], sh1, sc1, lat_row(tm), w_in_b[:, n_mix:],
                          tm=tm, tn=1024, out_dtype=BF16, epilogue="sigmoid", name="in_proj_gates")

    y_blk = _s5_branch_blocked(u_blk, bsz, l_ctx, l_lat, s5_a_re[0], s5_a_im[0], s5_log_dt[0],
                               s5_b_re[0], s5_b_im[0], s5_c_re[0], s5_c_im[0], s5_d[0])
    s5_out = _s5_glu(y_blk, s5_glu_w[0].astype(BF16), tm=tm, tn=1024)

    lora = rw_w2.shape[2]
    zl = jnp.zeros((lora, rw), F32)
    w2bd = jnp.concatenate([jnp.concatenate([rw_w2[0, 0], zl], axis=1),
                            jnp.concatenate([zl, rw_w2[0, 1]], axis=1)], axis=0)
    a2bd = jnp.concatenate([jnp.concatenate([rw_a2[0, 0], zl], axis=1),
                            jnp.concatenate([zl, rw_a2[0, 1]], axis=1)], axis=0)
    head_of = jnp.arange(rw) // RW_HEAD
    seg = (head_of[:, None] == jnp.arange(LANES)[None, :]).astype(BF16)
    segt = seg.T
    r, v, kk, g, bonus, lw, kd, be = _rw_prep(
        z_rw, bsz, l_ctx, l_lat, rw_mu[0], w2bd, a2bd, rw_g2[0], rw_w0[0].reshape(1, 2 * rw),
        rw_a0[0].reshape(1, 2 * rw), rw_k_k[0], rw_k_a[0], rw_r_k[0].reshape(rw), seg, segt)
    y_f, y_b = _rw_scan2(r, v, kk, lw, kd, be, l_ctx)

    merged = _rw_merge(y_f.reshape(m_lat, rw), y_b.reshape(m_lat, rw), bonus.reshape(m_lat, rw), g.reshape(m_lat, rw),
                       rw_ln_w[0], rw_ln_b[0], seg, segt, gates, s5_out, rw_proj[0].astype(BF16),
                       tm=512, tn=1024)
    h1 = _resid_matmul(merged, w_o[0].astype(BF16), x2, g1, lat_row(tm), tm=tm, tn=1024)

    act = _lnmod_swiglu(h1, norm2_w[0], sh2, sc2, lat_row(tm), ffn_w13[0].astype(BF16), d_ff,
                        tm=tm, tn=512, name="ffn_up")
    tm_dn = 512
    out = _ffn_down(act, ffn_w2[0].astype(BF16), h1, g2, lat_row(tm_dn), norm_f, tm=tm_dn, tk=d_ff // 4)
    return out.reshape(bsz, l_lat, d)
```

```python
import functools
import math

import jax
import jax.numpy as jnp
from jax import lax
from jax.experimental import pallas as pl
from jax.experimental.pallas import tpu as pltpu

F32 = jnp.float32
BF16 = jnp.bfloat16
HIGHEST = lax.Precision.HIGHEST

D_MODEL = 2048
N_MOD = 6
NORM_EPS = 1e-6
GN_EPS = 64e-5
GRID_W = 64
S5_GROUP = 16
S5_STATE = 64
S5_CHUNK = 16
RW_HEAD = 64
RW_CHUNK = 64
RW_SUB = 2
LANES = 128
VMEM_LIMIT = 48 * 1024 * 1024


def _cparams(sem):
    return pltpu.CompilerParams(dimension_semantics=sem, vmem_limit_bytes=VMEM_LIMIT)


def _operands(a, b, precision):
    if precision == "bf16":
        return a.astype(BF16), b.astype(BF16), None
    return a, b, precision


def _dot(a, b, precision=None):
    a, b, precision = _operands(a, b, precision)
    return jnp.dot(a, b, preferred_element_type=F32, precision=precision)


def _dot_nt(a, b, precision=None):
    a, b, precision = _operands(a, b, precision)
    return lax.dot_general(a, b, (((1,), (1,)), ((), ())), preferred_element_type=F32, precision=precision)


def _dot_hilo(a, ind):
    hi = a.astype(BF16)
    lo = (a - hi.astype(F32)).astype(BF16)
    return _dot(hi, ind) + _dot(lo, ind)


def _sigmoid(x):
    return 1.0 / (1.0 + jnp.exp(-x))


def _silu(x):
    return x * _sigmoid(x)


def _gelu_tanh(x):
    c = math.sqrt(2.0 / math.pi)
    return 0.5 * x * (1.0 + jnp.tanh(c * (x + 0.044715 * (x * x * x))))


def _softplus(x):
    return jnp.maximum(x, 0.0) + jnp.log(1.0 + jnp.exp(-jnp.abs(x)))


def _mod_kernel(c_ref, w_ref, b_ref, o_ref):
    o_ref[...] = _dot(_silu(c_ref[...]), w_ref[...], HIGHEST) + b_ref[...]


def _modulation(c_rows, ada_w, ada_b):
    m, d = c_rows.shape
    n = ada_w.shape[1]
    tn = 1024
    return pl.pallas_call(
        _mod_kernel,
        out_shape=jax.ShapeDtypeStruct((m, n), F32),
        grid=(n // tn,),
        in_specs=[pl.BlockSpec((m, d), lambda j: (0, 0)),
                  pl.BlockSpec((d, tn), lambda j: (0, j)),
                  pl.BlockSpec((1, tn), lambda j: (0, j))],
        out_specs=pl.BlockSpec((m, tn), lambda j: (0, j)),
        compiler_params=_cparams(("arbitrary",)),
        name="modulation",
    )(c_rows, ada_w, ada_b.reshape(1, n))


def _lnmod_rows(x, nw, sh, sc):
    ms = jnp.mean(x * x, axis=-1, keepdims=True)
    y = x * lax.rsqrt(ms + NORM_EPS) * nw
    return y * (1.0 + sc) + sh


def _lnmod_mm_kernel(x_ref, nw_ref, sh_ref, sc_ref, w_ref, o_ref, h_scr, *, epilogue):
    @pl.when(pl.program_id(1) == 0)
    def _():
        h_scr[...] = _lnmod_rows(x_ref[...], nw_ref[...], sh_ref[0], sc_ref[0]).astype(BF16)

    z = _dot(h_scr[...], w_ref[...])
    if epilogue == "sigmoid":
        z = _sigmoid(z)
    if epilogue == "lane_blocks":
        for jb in range(o_ref.shape[0]):
            o_ref[jb] = z[:, jb * LANES:(jb + 1) * LANES].astype(o_ref.dtype)
    else:
        o_ref[...] = z.astype(o_ref.dtype)


def _lnmod_swiglu_kernel(x_ref, nw_ref, sh_ref, sc_ref, w1_ref, w3_ref, o_ref, h_scr):
    @pl.when(pl.program_id(1) == 0)
    def _():
        h_scr[...] = _lnmod_rows(x_ref[...], nw_ref[...], sh_ref[0], sc_ref[0]).astype(BF16)

    h = h_scr[...]
    o_ref[...] = (_silu(_dot(h, w1_ref[...])) * _dot(h, w3_ref[...])).astype(o_ref.dtype)


def _lnmod_matmul(x2, nw, sh_tab, sc_tab, mod_row_of_block, w, *, tm, tn, out_dtype, epilogue=None, name):
    m, d = x2.shape
    n = w.shape[1]
    mod_map = lambda i, j: (mod_row_of_block(i), 0, 0)
    if epilogue == "lane_blocks":
        assert tn == n
        out_shape = jax.ShapeDtypeStruct((n // LANES, m, LANES), out_dtype)
        out_spec = pl.BlockSpec((n // LANES, tm, LANES), lambda i, j: (0, i, 0))
    else:
        out_shape = jax.ShapeDtypeStruct((m, n), out_dtype)
        out_spec = pl.BlockSpec((tm, tn), lambda i, j: (i, j))
    return pl.pallas_call(
        functools.partial(_lnmod_mm_kernel, epilogue=epilogue),
        out_shape=out_shape,
        grid=(m // tm, n // tn),
        in_specs=[pl.BlockSpec((tm, d), lambda i, j: (i, 0)),
                  pl.BlockSpec((1, d), lambda i, j: (0, 0)),
                  pl.BlockSpec((1, 1, d), mod_map),
                  pl.BlockSpec((1, 1, d), mod_map),
                  pl.BlockSpec((d, tn), lambda i, j: (0, j))],
        out_specs=out_spec,
        scratch_shapes=[pltpu.VMEM((tm, d), BF16)],
        compiler_params=_cparams(("parallel", "arbitrary")),
        name=name,
    )(x2, nw.reshape(1, d), sh_tab, sc_tab, w)


def _lnmod_swiglu(x2, nw, sh_tab, sc_tab, mod_row_of_block, w13, d_ff, *, tm, tn, name):
    m, d = x2.shape
    nj = d_ff // tn
    mod_map = lambda i, j: (mod_row_of_block(i), 0, 0)
    return pl.pallas_call(
        _lnmod_swiglu_kernel,
        out_shape=jax.ShapeDtypeStruct((m, d_ff), BF16),
        grid=(m // tm, nj),
        in_specs=[pl.BlockSpec((tm, d), lambda i, j: (i, 0)),
                  pl.BlockSpec((1, d), lambda i, j: (0, 0)),
                  pl.BlockSpec((1, 1, d), mod_map),
                  pl.BlockSpec((1, 1, d), mod_map),
                  pl.BlockSpec((d, tn), lambda i, j: (0, j)),
                  pl.BlockSpec((d, tn), lambda i, j: (0, j + nj))],
        out_specs=pl.BlockSpec((tm, tn), lambda i, j: (i, j)),
        scratch_shapes=[pltpu.VMEM((tm, d), BF16)],
        compiler_params=_cparams(("parallel", "arbitrary")),
        name=name,
    )(x2, nw.reshape(1, d), sh_tab, sc_tab, w13, w13)


def _s5_param_kernel(are_ref, aim_ref, ldt_ref, bre_ref, bim_ref, cre_ref, cim_ref,
                     e_ref, cs_ref, kt_ref, a16_ref):
    t_n, hg, p_n = S5_CHUNK, S5_GROUP, S5_STATE
    tau = lax.broadcasted_iota(jnp.int32, (t_n, 1, p_n), 0).astype(F32)
    for d in range(2):
        a_re = are_ref[0, d:d + 1, :]
        a_im = aim_ref[0, d:d + 1, :]
        dt = jnp.exp(ldt_ref[0, d:d + 1, :])
        lam = a_re * dt
        th = a_im * dt
        er = jnp.exp(lam)
        ab_re = er * jnp.cos(th)
        ab_im = er * jnp.sin(th)
        den = a_re * a_re + a_im * a_im
        x_re = ab_re - 1.0
        co_re = (x_re * a_re + ab_im * a_im) / den
        co_im = (ab_im * a_re - x_re * a_im) / den
        bt_re = bre_ref[0, d]
        bt_im = bim_ref[0, d]
        bb_re = co_re * bt_re - co_im * bt_im
        bb_im = co_re * bt_im + co_im * bt_re
        c_re = cre_ref[0, d]
        c_im = cim_ref[0, d]

        def power(tv):
            mag = jnp.exp(tv * lam)
            return mag * jnp.cos(tv * th), mag * jnp.sin(tv * th)

        pw_re, pw_im = power(tau)
        cp_re = (c_re[None] * pw_re - c_im[None] * pw_im).reshape(t_n * hg, p_n)
        cp_im = (c_re[None] * pw_im + c_im[None] * pw_re).reshape(t_n * hg, p_n)
        kt_ref[0, d] = _dot_nt(bb_re, cp_re, HIGHEST) - _dot_nt(bb_im, cp_im, HIGHEST)

        te = (t_n - 1.0 - tau) if d == 0 else tau
        pe_re, pe_im = power(te)
        e_ref[0, d, :, 0:p_n] = (pe_re * bb_re[None] - pe_im * bb_im[None]).reshape(t_n * hg, p_n)
        e_ref[0, d, :, p_n:2 * p_n] = (pe_re * bb_im[None] + pe_im * bb_re[None]).reshape(t_n * hg, p_n)

        tc = (tau + 1.0) if d == 0 else (t_n - tau)
        pc_re, pc_im = power(tc)
        cs_ref[0, d, :, 0:p_n] = (c_re[None] * pc_re - c_im[None] * pc_im).reshape(t_n * hg, p_n)
        cs_ref[0, d, :, p_n:2 * p_n] = -(c_re[None] * pc_im + c_im[None] * pc_re).reshape(t_n * hg, p_n)

        mag16 = jnp.exp(float(t_n) * lam)
        a16_ref[0, d, 0:1, :] = mag16 * jnp.cos(float(t_n) * th)
        a16_ref[0, d, 1:2, :] = mag16 * jnp.sin(float(t_n) * th)


def _s5_params(a_re, a_im, log_dt, b_re, b_im, c_re, c_im):
    g_n = a_re.shape[1]
    p_n, hg, t_n = S5_STATE, S5_GROUP, S5_CHUNK
    tr = lambda a: jnp.swapaxes(a, 0, 1)
    ldt = jnp.broadcast_to(tr(log_dt)[:, :, None], (g_n, 2, p_n))
    spec3 = pl.BlockSpec((1, 2, p_n), lambda g: (g, 0, 0))
    spec4 = pl.BlockSpec((1, 2, hg, p_n), lambda g: (g, 0, 0, 0))
    th = t_n * hg
    return pl.pallas_call(
        _s5_param_kernel,
        out_shape=(jax.ShapeDtypeStruct((g_n, 2, th, 2 * p_n), F32),
                   jax.ShapeDtypeStruct((g_n, 2, th, 2 * p_n), F32),
                   jax.ShapeDtypeStruct((g_n, 2, hg, th), F32),
                   jax.ShapeDtypeStruct((g_n, 2, 2, p_n), F32)),
        grid=(g_n,),
        in_specs=[spec3, spec3, spec3, spec4, spec4, spec4, spec4],
        out_specs=(pl.BlockSpec((1, 2, th, 2 * p_n), lambda g: (g, 0, 0, 0)),
                   pl.BlockSpec((1, 2, th, 2 * p_n), lambda g: (g, 0, 0, 0)),
                   pl.BlockSpec((1, 2, hg, th), lambda g: (g, 0, 0, 0)),
                   pl.BlockSpec((1, 2, 2, p_n), lambda g: (g, 0, 0, 0))),
        compiler_params=_cparams(("parallel",)),
        name="s5_params",
    )(tr(a_re), tr(a_im), ldt,
      jnp.transpose(b_re, (1, 0, 3, 2)), jnp.transpose(b_im, (1, 0, 3, 2)), tr(c_re), tr(c_im))


def _s5_state_in_kernel(u_ref, e_ref, o_ref):
    e = e_ref[0]
    u = u_ref[0]
    o_ref[0, :, 0:LANES] = _dot(u, e[0], HIGHEST)
    o_ref[0, :, LANES:2 * LANES] = _dot(u, e[1], HIGHEST)


def _s5_state_inputs(u_g, e):
    g_n, rows, th = u_g.shape
    return pl.pallas_call(
        _s5_state_in_kernel,
        out_shape=jax.ShapeDtypeStruct((g_n, rows, 2 * LANES), F32),
        grid=(g_n,),
        in_specs=[pl.BlockSpec((1, rows, th), lambda g: (g, 0, 0)),
                  pl.BlockSpec((1, 2, th, LANES), lambda g: (g, 0, 0, 0))],
        out_specs=pl.BlockSpec((1, rows, 2 * LANES), lambda g: (g, 0, 0)),
        compiler_params=_cparams(("parallel",)),
        name="s5_state_inputs",
    )(u_g, e)


def _s5_scan_kernel(e_ref, a_ref, o_ref, *, n_ctx, n_all):
    zero = jnp.zeros(e_ref.shape[2:], F32)

    def step(pr, pi):
        ar, ai = a_ref[pr], a_ref[pi]

        def body(c, carry):
            sr, si = carry
            o_ref[pr, c] = sr
            o_ref[pi, c] = si
            return (ar * sr - ai * si + e_ref[pr, c], ar * si + ai * sr + e_ref[pi, c])
        return body

    lax.fori_loop(0, n_all, step(0, 1), (zero, zero))
    bwd = step(2, 3)
    carry = lax.fori_loop(0, n_ctx, lambda k, cy: bwd(n_ctx - 1 - k, cy), (zero, zero))
    lax.fori_loop(0, n_all - n_ctx, lambda k, cy: bwd(n_all - 1 - k, cy), carry)


def _s5_scan(e_planes, a_planes, n_ctx, n_all):
    _, rows, r_n, _ = e_planes.shape
    bsz = rows // n_all
    sub = 8
    return pl.pallas_call(
        functools.partial(_s5_scan_kernel, n_ctx=n_ctx, n_all=n_all),
        out_shape=jax.ShapeDtypeStruct(e_planes.shape, F32),
        grid=(bsz, r_n // sub),
        in_specs=[pl.BlockSpec((4, n_all, sub, LANES), lambda b, q: (0, b, q, 0)),
                  pl.BlockSpec((4, sub, LANES), lambda b, q: (0, q, 0))],
        out_specs=pl.BlockSpec((4, n_all, sub, LANES), lambda b, q: (0, b, q, 0)),
        compiler_params=_cparams(("parallel", "parallel")),
        name="s5_scan",
    )(e_planes, a_planes)


def _s5_apply_kernel(u_ref, s_ref, mf_ref, mb_ref, d_ref, cs_ref, o_ref, *, n_ctx):
    u = u_ref[0, n_ctx:, :]
    s = s_ref[0, n_ctx:, :]
    y = _dot(u, mf_ref[0] + mb_ref[0], HIGHEST) + u * d_ref[0]
    y = y + _dot_nt(s[:, 0:LANES], cs_ref[0, 0], HIGHEST) + _dot_nt(s[:, LANES:2 * LANES], cs_ref[0, 1], HIGHEST)
    o_ref[0, 0] = y


def _s5_apply(u_g, s_g, m_f, m_b, d_t, cs, n_ctx, n_all):
    g_n, rows, th = u_g.shape
    bsz = rows // n_all
    n_lat = n_all - n_ctx
    return pl.pallas_call(
        functools.partial(_s5_apply_kernel, n_ctx=n_ctx),
        out_shape=jax.ShapeDtypeStruct((g_n, bsz, n_lat, th), F32),
        grid=(g_n, bsz),
        in_specs=[pl.BlockSpec((1, n_all, th), lambda g, b: (g, b, 0)),
                  pl.BlockSpec((1, n_all, 2 * LANES), lambda g, b: (g, b, 0)),
                  pl.BlockSpec((1, th, th), lambda g, b: (g, 0, 0)),
                  pl.BlockSpec((1, th, th), lambda g, b: (g, 0, 0)),
                  pl.BlockSpec((1, 1, th), lambda g, b: (g, 0, 0)),
                  pl.BlockSpec((1, 2, th, LANES), lambda g, b: (g, 0, 0, 0))],
        out_specs=pl.BlockSpec((1, 1, n_lat, th), lambda g, b: (g, b, 0, 0)),
        compiler_params=_cparams(("parallel", "arbitrary")),
        name="s5_apply",
    )(u_g, s_g, m_f, m_b, d_t, cs)


def _s5_branch(u_all, n_ctx_tok, a_re, a_im, log_dt, b_re, b_im, c_re, c_im, s5_d):
    bsz, l_all, width = u_all.shape
    hg, t_n, p_n = S5_GROUP, S5_CHUNK, S5_STATE
    g_n = width // hg
    n_all = l_all // t_n
    n_ctx = n_ctx_tok // t_n
    th = t_n * hg
    e, cs, kt, a16 = _s5_params(a_re, a_im, log_dt, b_re, b_im, c_re, c_im)

    kt5 = kt.reshape(g_n, 2, hg, t_n, hg)
    ii = jnp.arange(t_n)[:, None]
    jj = jnp.arange(t_n)[None, :]

    def toeplitz(k4, lag, keep):
        m = k4[:, :, jnp.clip(lag, 0, t_n - 1), :]
        m = jnp.where(keep[None, None, :, :, None], m, 0.0)
        return jnp.transpose(m, (0, 2, 1, 3, 4)).reshape(g_n, th, th)

    m_f = toeplitz(kt5[:, 0], jj - ii, jj >= ii)
    m_b = toeplitz(kt5[:, 1], ii - jj, ii >= jj)
    d_t = jnp.tile(s5_d, (1, t_n)).reshape(g_n, 1, th)

    u_g = jnp.transpose(u_all.reshape(bsz, n_all, t_n, g_n, hg), (3, 0, 1, 2, 4)).reshape(g_n, bsz * n_all, th)
    e_cat = e
    s_in = _s5_state_inputs(u_g, e_cat)
    rows = bsz * n_all
    planes = jnp.transpose(s_in.reshape(g_n, rows, 4, p_n), (2, 1, 0, 3)).reshape(4, rows, g_n * p_n // LANES, LANES)
    a_pl = jnp.transpose(a16.reshape(g_n, 4, p_n), (1, 0, 2)).reshape(4, g_n * p_n // LANES, LANES)
    st = _s5_scan(planes, a_pl, n_ctx, n_all)
    s_g = jnp.transpose(st.reshape(4, rows, g_n, p_n), (2, 1, 0, 3)).reshape(g_n, rows, 4 * p_n)
    y_g = _s5_apply(u_g, s_g, m_f, m_b, d_t, cs, n_ctx, n_all)
    n_lat = n_all - n_ctx
    y = jnp.transpose(y_g.reshape(g_n, bsz, n_lat, t_n, hg), (1, 2, 3, 0, 4))
    return y.reshape(bsz, n_lat * t_n, width)


def _s5_param2_kernel(are_ref, aim_ref, ldt_ref, bre_ref, bim_ref, cre_ref, cim_ref,
                      e_ref, c_ref, m_ref, a16_ref):
    t_n, hg, p_n = S5_CHUNK, S5_GROUP, S5_STATE
    tau = lax.broadcasted_iota(jnp.int32, (t_n, 1, p_n), 0).astype(F32)
    taps = []
    for d in range(2):
        a_re = are_ref[0, d:d + 1, :]
        a_im = aim_ref[0, d:d + 1, :]
        dt = jnp.exp(ldt_ref[0, d:d + 1, :])
        lam = a_re * dt
        th = a_im * dt
        er = jnp.exp(lam)
        ab_re = er * jnp.cos(th)
        ab_im = er * jnp.sin(th)
        den = a_re * a_re + a_im * a_im
        x_re = ab_re - 1.0
        co_re = (x_re * a_re + ab_im * a_im) / den
        co_im = (ab_im * a_re - x_re * a_im) / den
        bt_re = bre_ref[0, d]
        bt_im = bim_ref[0, d]
        bb_re = co_re * bt_re - co_im * bt_im
        bb_im = co_re * bt_im + co_im * bt_re
        c_re = cre_ref[0, d]
        c_im = cim_ref[0, d]

        def power(tv):
            mag = jnp.exp(tv * lam)
            return mag * jnp.cos(tv * th), mag * jnp.sin(tv * th)

        pw_re, pw_im = power(tau if d == 0 else (t_n - 1.0 - tau))
        cp_re = (c_re[None] * pw_re - c_im[None] * pw_im).reshape(t_n * hg, p_n)
        cp_im = (c_re[None] * pw_im + c_im[None] * pw_re).reshape(t_n * hg, p_n)
        taps.append(_dot_nt(bb_re, cp_re, HIGHEST) - _dot_nt(bb_im, cp_im, HIGHEST))

        lo, hi = 2 * d * p_n, (2 * d + 1) * p_n
        pe_re, pe_im = power((t_n - 1.0 - tau) if d == 0 else tau)
        e_ref[0, :, 0, :, lo:hi] = pe_re * bb_re[None] - pe_im * bb_im[None]
        e_ref[0, :, 0, :, hi:hi + p_n] = pe_re * bb_im[None] + pe_im * bb_re[None]
        pc_re, pc_im = power((tau + 1.0) if d == 0 else (t_n - tau))
        c_ref[0, :, 0, :, lo:hi] = c_re[None] * pc_re - c_im[None] * pc_im
        c_ref[0, :, 0, :, hi:hi + p_n] = -(c_re[None] * pc_im + c_im[None] * pc_re)

        mag16 = jnp.exp(float(t_n) * lam)
        a16_ref[0, d, 0:1, :] = mag16 * jnp.cos(float(t_n) * th)
        a16_ref[0, d, 1:2, :] = mag16 * jnp.sin(float(t_n) * th)

    width = t_n * hg
    lane = lax.broadcasted_iota(jnp.int32, (hg, width), 1)
    for t in range(t_n):
        sf = t * hg
        sb = (t_n - 1 - t) * hg
        f = taps[0] if sf == 0 else jnp.where(lane >= sf, pltpu.roll(taps[0], sf, 1), 0.0)
        b = taps[1] if sb == 0 else jnp.where(lane < width - sb, pltpu.roll(taps[1], width - sb, 1), 0.0)
        m_ref[0, t, 0] = f + b


def _s5_params2(a_re, a_im, log_dt, b_re, b_im, c_re, c_im, nb):
    g_n = a_re.shape[1]
    gl_n = g_n // nb
    p_n, hg, t_n = S5_STATE, S5_GROUP, S5_CHUNK
    tr = lambda a: jnp.swapaxes(a, 0, 1)
    ldt = jnp.broadcast_to(tr(log_dt)[:, :, None], (g_n, 2, p_n))
    spec3 = pl.BlockSpec((1, 2, p_n), lambda g: (g, 0, 0))
    spec4 = pl.BlockSpec((1, 2, hg, p_n), lambda g: (g, 0, 0, 0))
    wide = 4 * p_n
    comp = jax.ShapeDtypeStruct((nb, t_n, gl_n, hg, wide), F32)
    comp_spec = pl.BlockSpec((1, t_n, 1, hg, wide), lambda g: (g // gl_n, 0, g % gl_n, 0, 0))
    return pl.pallas_call(
        _s5_param2_kernel,
        out_shape=(comp, comp, comp, jax.ShapeDtypeStruct((g_n, 2, 2, p_n), F32)),
        grid=(g_n,),
        in_specs=[spec3, spec3, spec3, spec4, spec4, spec4, spec4],
        out_specs=(comp_spec, comp_spec, comp_spec, pl.BlockSpec((1, 2, 2, p_n), lambda g: (g, 0, 0, 0))),
        compiler_params=_cparams(("parallel",)),
        name="s5_params",
    )(tr(a_re), tr(a_im), ldt,
      jnp.transpose(b_re, (1, 0, 3, 2)), jnp.transpose(b_im, (1, 0, 3, 2)), tr(c_re), tr(c_im))


def _expand_block_diag(comp, rep_ref, mask_ref, w_scr):
    k = w_scr.shape[0]
    period = mask_ref.shape[1]
    cb = comp.astype(BF16)
    step = 512
    for c0 in range(0, k, step):
        blk = _dot(cb, rep_ref[:, c0:c0 + step]).astype(BF16)
        for q0 in range(0, step, period):
            w_scr[:, c0 + q0:c0 + q0 + period] = blk[:, q0:q0 + period] * mask_ref[...]


def _s5_ein_kernel(u_ref, ec_ref, rep_ref, mask_ref, o_ref, w_scr):
    _expand_block_diag(ec_ref[0], rep_ref, mask_ref, w_scr)
    o_ref[0] = _dot(u_ref[0].astype(BF16), w_scr[...])


def _s5_chunk_inputs(u_rows, e_comp, rep_e, mask_e):
    nb, rows, k = u_rows.shape
    cw = e_comp.shape[2]
    return pl.pallas_call(
        _s5_ein_kernel,
        out_shape=jax.ShapeDtypeStruct((nb, rows, k), F32),
        grid=(nb,),
        in_specs=[pl.BlockSpec((1, rows, k), lambda j: (j, 0, 0)),
                  pl.BlockSpec((1, k, cw), lambda j: (j, 0, 0)),
                  pl.BlockSpec(rep_e.shape, lambda j: (0, 0)),
                  pl.BlockSpec(mask_e.shape, lambda j: (0, 0))],
        out_specs=pl.BlockSpec((1, rows, k), lambda j: (j, 0, 0)),
        scratch_shapes=[pltpu.VMEM((k, k), BF16)],
        compiler_params=_cparams(("parallel",)),
        name="s5_chunk_inputs",
    )(u_rows, e_comp, rep_e, mask_e)


def _s5_bscan_kernel(e_ref, a_ref, o_ref, *, bsz, n_ctx, n_lat):
    q = e_ref.shape[2] // 4
    planes = lambda row, d: (row[:, (2 * d) * q:(2 * d + 1) * q], row[:, (2 * d + 1) * q:(2 * d + 2) * q])
    coef = [planes(a_ref[0], d) for d in range(2)]
    ctx0 = bsz * n_lat

    def advance(state, rows):
        new = []
        for (sr, si), (b, d), row in zip(state, [(b, d) for b in range(bsz) for d in range(2)], rows):
            ar, ai = coef[d]
            er, ei = planes(e_ref[0, pl.ds(row, 1), :], d)
            new.append((ar * sr - ai * si + er, ar * si + ai * sr + ei))
        return tuple(new)

    def ctx_step(s, state):
        rows = [ctx0 + b * n_ctx + (s if d == 0 else n_ctx - 1 - s) for b in range(bsz) for d in range(2)]
        return advance(state, rows)

    def lat_step(s, state):
        rows = [b * n_lat + (s if d == 0 else n_lat - 1 - s) for b in range(bsz) for d in range(2)]
        for (sr, si), (b, d), row in zip(state, [(b, d) for b in range(bsz) for d in range(2)], rows):
            o_ref[0, pl.ds(row, 1), (2 * d) * q:(2 * d + 1) * q] = sr
            o_ref[0, pl.ds(row, 1), (2 * d + 1) * q:(2 * d + 2) * q] = si
        return advance(state, rows)

    zero = jnp.zeros((1, q), F32)
    state = tuple((zero, zero) for _ in range(2 * bsz))
    state = lax.fori_loop(0, n_ctx, ctx_step, state)
    lax.fori_loop(0, n_lat, lat_step, state)


def _s5_bscan(e_rows, a_rows, bsz, n_ctx, n_lat):
    nb, rows, k = e_rows.shape
    return pl.pallas_call(
        functools.partial(_s5_bscan_kernel, bsz=bsz, n_ctx=n_ctx, n_lat=n_lat),
        out_shape=jax.ShapeDtypeStruct((nb, bsz * n_lat, k), F32),
        grid=(nb,),
        in_specs=[pl.BlockSpec((1, rows, k), lambda j: (j, 0, 0)),
                  pl.BlockSpec((1, 1, k), lambda j: (j, 0, 0))],
        out_specs=pl.BlockSpec((1, bsz * n_lat, k), lambda j: (j, 0, 0)),
        compiler_params=_cparams(("parallel",)),
        name="s5_scan",
    )(e_rows, a_rows)


def _s5_out_kernel(u_ref, s_ref, mc_ref, cc_ref, d_ref, rep_m_ref, mask_m_ref, rep_e_ref, mask_e_ref,
                   o_ref, wm_scr, wc_scr):
    @pl.when(pl.program_id(1) == 0)
    def _():
        _expand_block_diag(mc_ref[0], rep_m_ref, mask_m_ref, wm_scr)
        _expand_block_diag(cc_ref[0], rep_e_ref, mask_e_ref, wc_scr)

    u = u_ref[0]
    y = _dot(u.astype(BF16), wm_scr[...]) + _dot_nt(s_ref[0].astype(BF16), wc_scr[...])
    o_ref[0] = y + u * d_ref[0]


def _s5_outputs(u_rows, s_rows, m_comp, c_comp, d_rows, rep_m, mask_m, rep_e, mask_e):
    nb, _, k = u_rows.shape
    rows = s_rows.shape[1]
    cw = m_comp.shape[2]
    tr = rows // 2
    const = lambda a: pl.BlockSpec(a.shape, lambda j, i: (0, 0))
    return pl.pallas_call(
        _s5_out_kernel,
        out_shape=jax.ShapeDtypeStruct((nb, rows, k), F32),
        grid=(nb, rows // tr),
        in_specs=[pl.BlockSpec((1, tr, k), lambda j, i: (j, i, 0)),
                  pl.BlockSpec((1, tr, k), lambda j, i: (j, i, 0)),
                  pl.BlockSpec((1, k, cw), lambda j, i: (j, 0, 0)),
                  pl.BlockSpec((1, k, cw), lambda j, i: (j, 0, 0)),
                  pl.BlockSpec((1, 1, k), lambda j, i: (j, 0, 0)),
                  const(rep_m), const(mask_m), const(rep_e), const(mask_e)],
        out_specs=pl.BlockSpec((1, tr, k), lambda j, i: (j, i, 0)),
        scratch_shapes=[pltpu.VMEM((k, k), BF16), pltpu.VMEM((k, k), BF16)],
        compiler_params=_cparams(("parallel", "arbitrary")),
        name="s5_outputs",
    )(u_rows, s_rows, m_comp, c_comp, d_rows, rep_m, mask_m, rep_e, mask_e)


def _s5_branch_blocked(u_blk, bsz, l_ctx, l_lat, a_re, a_im, log_dt, b_re, b_im, c_re, c_im, s5_d):
    nb, m_all, _ = u_blk.shape
    hg, t_n, p_n = S5_GROUP, S5_CHUNK, S5_STATE
    g_n = a_re.shape[1]
    gl_n = g_n // nb
    k = t_n * LANES
    n_lat = l_lat // t_n
    n_ctx = l_ctx // t_n
    e_c, c_c, m_c, a16 = _s5_params2(a_re, a_im, log_dt, b_re, b_im, c_re, c_im, nb)
    cw = 4 * p_n
    e_comp, c_comp, m_comp = (a.reshape(nb, k, cw) for a in (e_c, c_c, m_c))

    row_gl = (jnp.arange(k) // hg) % gl_n
    col = jnp.arange(k)
    src = jnp.arange(cw)
    rep_e = ((src[:, None] // p_n == col[None, :] // (gl_n * p_n)) & (src[:, None] % p_n == col[None, :] % p_n)).astype(BF16)
    mask_e = (row_gl[:, None] == (jnp.arange(gl_n * p_n)[None, :] // p_n)).astype(BF16)
    rep_m = ((src[:, None] // hg == col[None, :] // (gl_n * hg)) & (src[:, None] % hg == col[None, :] % hg)).astype(BF16)
    mask_m = (row_gl[:, None] == (jnp.arange(gl_n * hg)[None, :] // hg)).astype(BF16)

    d_rows = jnp.tile(s5_d.reshape(nb, 1, gl_n * hg), (1, 1, t_n))
    a_rows = jnp.transpose(a16.reshape(nb, gl_n, 2, 2, p_n), (0, 2, 3, 1, 4)).reshape(nb, 1, 4 * gl_n * p_n)

    u_rows = u_blk.reshape(nb, m_all // t_n, k)
    e_rows = _s5_chunk_inputs(u_rows, e_comp, rep_e, mask_e)
    s_rows = _s5_bscan(e_rows, a_rows, bsz, n_ctx, n_lat)
    y_rows = _s5_outputs(u_rows, s_rows, m_comp, c_comp, d_rows, rep_m, mask_m, rep_e, mask_e)
    return y_rows.reshape(nb, bsz * l_lat, LANES)


def _rw_prep_kernel(z_ref, zp_ref, zn_ref, mu_ref, w2_ref, a2_ref, g2_ref, w0_ref, a0_ref,
                    kk_w_ref, ka_ref, rk_ref, seg_ref, segt_ref,
                    r_ref, v_ref, kk_ref, g_ref, bonus_ref, lw_ref, kd_ref, be_ref,
                    *, tm, l_lat, rw):
    j = pl.program_id(1)
    z = z_ref[...]
    lat = j > 0
    tl = lax.broadcasted_iota(jnp.int32, (tm, 1), 0)
    tok = (j - 1) * tm + tl
    col = tl % GRID_W
    m_l = jnp.where(lat, col, tl) > 0
    m_r = jnp.where(lat, col - (GRID_W - 1), tl - (tm - 1)) < 0
    m_u = jnp.logical_and(lat, tok >= GRID_W)
    m_d = jnp.logical_and(lat, tok < l_lat - GRID_W)
    z_l = pltpu.roll(z, 1, 0)
    z_r = pltpu.roll(z, tm - 1, 0)
    z_u = jnp.concatenate([zp_ref[...], z[:tm - GRID_W]], axis=0)
    z_d = jnp.concatenate([z[GRID_W:], zn_ref[...]], axis=0)
    s = (jnp.where(m_l, z_l, 0.0) + jnp.where(m_r, z_r, 0.0)
         + jnp.where(m_u, z_u, 0.0) + jnp.where(m_d, z_d, 0.0))
    cnt = (m_l.astype(F32) + m_r.astype(F32)) + (m_u.astype(F32) + m_d.astype(F32))
    zs = z + (s / cnt - z) * mu_ref[...]

    r = zs[:, 0:rw]
    k = zs[:, rw:2 * rw]
    v = zs[:, 2 * rw:3 * rw]
    o = 3 * rw
    wd = zs[:, o:o + LANES]
    ad = zs[:, o + LANES:o + 2 * LANES]
    gd = zs[:, o + 2 * LANES:o + 3 * LANES]

    seg = seg_ref[...]
    segt = segt_ref[...]

    def head_sum(t):
        return _dot_hilo(_dot_hilo(t, seg), segt)

    g_ref[0] = _dot(_sigmoid(gd), g2_ref[...], "bf16")
    kk = k * kk_w_ref[...]
    kk = kk * lax.rsqrt(head_sum(kk * kk) + 1e-12)
    wl = w0_ref[...] + _dot(jnp.tanh(wd), w2_ref[...], "bf16")
    al = a0_ref[...] + _dot(ad, a2_ref[...], "bf16")
    r_ref[0] = r
    v_ref[0] = v
    kk_ref[0] = kk
    coef = jnp.zeros_like(r)
    for d in range(2):
        w_raw = -_softplus(-wl[:, d * rw:(d + 1) * rw]) - 0.5
        a = _sigmoid(al[:, d * rw:(d + 1) * rw])
        k_d = k * (1.0 + (a - 1.0) * ka_ref[...])
        lw_ref[d, 0] = -jnp.exp(w_raw)
        kd_ref[d, 0] = k_d
        be_ref[d, 0] = kk * a
        coef = coef + head_sum(r * k_d * rk_ref[...])
    bonus_ref[0] = coef * v


def _rw_prep(z_rw, bsz, l_ctx, l_lat, mu, w2bd, a2bd, g2, w0cat, a0cat, k_k, k_a, r_k_flat, seg, segt):
    cols = z_rw.shape[1]
    tm = l_ctx
    rw = g2.shape[1]
    l_all = l_ctx + l_lat
    nblk = l_all // tm
    lat_blk = l_lat // tm
    hb = tm // GRID_W
    lat_hblk = l_lat // GRID_W

    def main_blk(b, j):
        return jnp.where(j == 0, bsz * lat_blk + b, b * lat_blk + j - 1)

    def prev_halo(b, j):
        return b * lat_hblk + jnp.maximum((j - 1) * hb - 1, 0)

    def next_halo(b, j):
        return b * lat_hblk + jnp.minimum(jnp.maximum(j, 1) * hb, lat_hblk - 1)

    full = lambda shape: pl.BlockSpec(shape, lambda b, j: (0,) * len(shape))
    shared = jax.ShapeDtypeStruct((bsz, l_all, rw), F32)
    lat_only = jax.ShapeDtypeStruct((bsz, l_lat, rw), F32)
    per_dir = jax.ShapeDtypeStruct((2, bsz, l_all, rw), F32)
    o_shared = pl.BlockSpec((1, tm, rw), lambda b, j: (b, j, 0))
    o_lat = pl.BlockSpec((1, tm, rw), lambda b, j: (b, jnp.maximum(j - 1, 0), 0))
    o_dir = pl.BlockSpec((2, 1, tm, rw), lambda b, j: (0, b, j, 0))
    return pl.pallas_call(
        functools.partial(_rw_prep_kernel, tm=tm, l_lat=l_lat, rw=rw),
        out_shape=(shared,) * 3 + (lat_only,) * 2 + (per_dir,) * 3,
        grid=(bsz, nblk),
        in_specs=[pl.BlockSpec((tm, cols), lambda b, j: (main_blk(b, j), 0)),
                  pl.BlockSpec((GRID_W, cols), lambda b, j: (prev_halo(b, j), 0)),
                  pl.BlockSpec((GRID_W, cols), lambda b, j: (next_halo(b, j), 0)),
                  full((1, cols)), full(w2bd.shape), full(a2bd.shape), full(g2.shape),
                  full((1, 2 * rw)), full((1, 2 * rw)), full((1, rw)), full((1, rw)), full((1, rw)),
                  full(seg.shape), full(segt.shape)],
        out_specs=(o_shared,) * 3 + (o_lat,) * 2 + (o_dir,) * 3,
        compiler_params=_cparams(("parallel", "arbitrary")),
        name="rwkv_prep",
    )(z_rw, z_rw, z_rw, mu.reshape(1, cols), w2bd, a2bd, g2, w0cat, a0cat,
      k_k.reshape(1, rw), k_a.reshape(1, rw), r_k_flat.reshape(1, rw), seg, segt)


def _stack_heads(x, head0):
    return jnp.concatenate([jnp.where(head0, x, 0.0), jnp.where(head0, 0.0, x)], axis=0)


def _rw_chunk_kernel(r_ref, v_ref, kk_ref, lw_ref, kd_ref, be_ref, y_ref,
                     z_scr, gc_s, lhs_s, kb_s, kht_s, bht_s, vs_s, rt_s, aab_s, akr_s, arb_s, t_s, x_s, xin_s,
                     yc_s, qy_s, bw_s, n_s, y_s, *, n_pairs, n_sub):
    c_n = RW_CHUNK
    n2 = 2 * c_n
    rev = (pl.program_id(0) % 2) == 1

    @pl.when(pl.program_id(1) == 0)
    def _():
        z_scr[...] = jnp.zeros_like(z_scr)

    ri = lax.broadcasted_iota(jnp.int32, (c_n, c_n), 0)
    ci = lax.broadcasted_iota(jnp.int32, (c_n, c_n), 1)
    tri = (jnp.where(rev, ci - ri, ri - ci) >= 0).astype(F32)
    r2 = lax.broadcasted_iota(jnp.int32, (n2, n2), 0)
    c2 = lax.broadcasted_iota(jnp.int32, (n2, n2), 1)
    t2 = r2 % c_n
    i2 = c2 % c_n
    same_head = (r2 // c_n) == (c2 // c_n)
    before = jnp.logical_and(same_head, jnp.where(rev, i2 - t2, t2 - i2) > 0)
    upto = jnp.logical_or(before, r2 == c2)
    eye = (r2 == c2).astype(F32)
    head0 = lax.broadcasted_iota(jnp.int32, (1, LANES), 1) < RW_HEAD

    def blk(s):
        return (r2 // s) == (c2 // s)

    pairs = range(n_pairs)
    units = range(n_sub * n_pairs)
    lanes_of = lambda p: slice(p * LANES, (p + 1) * LANES)

    row_of = lambda k: pl.multiple_of(jnp.where(rev, n_sub - 1 - k, k) * c_n, c_n)
    for cc in range(n_sub):
        rows = pl.ds(row_of(cc), c_n)
        lw = lw_ref[0, 0, rows, :]
        cum = _dot(tri, lw, HIGHEST)
        tot = jnp.sum(lw, axis=0, keepdims=True)
        g_inv = jnp.exp(-cum)
        g_hat = jnp.exp(tot - cum)
        kd = kd_ref[0, 0, rows, :]
        be = be_ref[0, 0, rows, :]
        at = kk_ref[0, rows, :] * jnp.exp(cum - lw)
        rt = r_ref[0, rows, :] * jnp.exp(cum)
        kt = kd * g_inv
        bt = be * g_inv
        kh = kd * g_hat
        bh = be * g_hat
        gc_s[cc] = jnp.exp(tot)
        v = v_ref[0, rows, :]
        for p in pairs:
            u = cc * n_pairs + p
            st = lambda x: _stack_heads(x[:, lanes_of(p)], head0)
            rt_p = st(rt)
            at_p = st(at).astype(BF16)
            lhs_s[u, :n2] = at_p
            lhs_s[u, n2:] = rt_p.astype(BF16)
            xin_s[u, :, :LANES] = at_p
            rt_s[u] = rt_p
            kb_s[u, :n2] = st(kt).astype(BF16)
            kb_s[u, n2:] = st(bt).astype(BF16)
            kht_s[u] = st(kh).T.astype(BF16)
            bht_s[u] = st(bh).T.astype(BF16)
            vs_s[u] = st(v).astype(BF16)

    for u in units:
        g = _dot_nt(lhs_s[u], kb_s[u])
        a_ab = jnp.where(before, g[:n2, n2:], 0.0)
        akr_s[u, :n2] = jnp.where(before, g[:n2, :n2], 0.0).astype(BF16)
        akr_s[u, n2:] = jnp.where(upto, g[n2:, :n2], 0.0).astype(BF16)
        arb_s[u] = jnp.where(upto, g[n2:, n2:], 0.0).astype(BF16)
        aab_s[u] = a_ab
        t_s[u] = eye - jnp.where(blk(2), a_ab, 0.0)

    s = 2
    while s < c_n:
        off = jnp.logical_and(blk(2 * s), jnp.logical_not(blk(s)))
        for u in units:
            x_s[u] = _dot(t_s[u].astype(BF16), jnp.where(off, aab_s[u], 0.0).astype(BF16)).astype(BF16)
        for u in units:
            t = t_s[u]
            t_s[u] = t - _dot(x_s[u], t.astype(BF16))
        s *= 2

    for u in units:
        av = _dot(akr_s[u], vs_s[u])
        xin_s[u, :, LANES:] = av[:n2].astype(BF16)
        yc_s[u] = av[n2:]

    for u in units:
        wu = _dot(t_s[u].astype(BF16), xin_s[u]).astype(BF16)
        q = _dot(arb_s[u], wu)
        bwu = _dot(bht_s[u], wu)
        qy_s[u] = (rt_s[u] - q[:, :LANES]).astype(BF16)
        yc_s[u] = yc_s[u] - q[:, LANES:]
        bw_s[u] = bwu[:, :LANES]
        n_s[u] = _dot(kht_s[u], vs_s[u]) - bwu[:, LANES:]

    for k in range(n_sub):
        g_c = gc_s[k]
        for p in pairs:
            u = k * n_pairs + p
            z0 = z_scr[p].astype(BF16)
            y_p = _dot(qy_s[u], z0) + yc_s[u]
            y_s[k, :, lanes_of(p)] = y_p[:c_n] + y_p[c_n:]
            m_z = eye * g_c[:, lanes_of(p)] - bw_s[u]
            z_scr[p] = _dot(m_z.astype(BF16), z0) + n_s[u]
    for k in range(n_sub):
        y_ref[0, 0, pl.ds(row_of(k), c_n), :] = y_s[k]


def _rw_scan(r, v, kk, lw, kd, be, l_ctx):
    bsz, l_all, rw = r.shape
    n_sub = RW_SUB
    c_n = RW_CHUNK
    rows = n_sub * c_n
    n_all = l_all // rows
    n_ctx = l_ctx // rows
    n_lat = n_all - n_ctx
    n_pairs = rw // LANES
    n2 = 2 * c_n
    vm = lambda nr, cols, dt: pltpu.VMEM((n_sub * n_pairs, nr, cols), dt)

    def chunk_of(bd, s):
        fwd = s
        bwd = jnp.where(s < n_ctx, n_ctx - 1 - s, n_all + n_ctx - 1 - s)
        return jnp.where(bd % 2 == 0, fwd, bwd)

    def out_chunk(bd, s):
        c = chunk_of(bd, s)
        edge = jnp.where(bd % 2 == 0, 0, n_lat - 1)
        return jnp.where(s < n_ctx, edge, c - n_ctx)

    shared = pl.BlockSpec((1, rows, rw), lambda bd, s: (bd // 2, chunk_of(bd, s), 0))
    per_dir = pl.BlockSpec((1, 1, rows, rw), lambda bd, s: (bd % 2, bd // 2, chunk_of(bd, s), 0))
    return pl.pallas_call(
        functools.partial(_rw_chunk_kernel, n_pairs=n_pairs, n_sub=n_sub),
        out_shape=jax.ShapeDtypeStruct((2, bsz, n_lat * rows, rw), F32),
        grid=(2 * bsz, n_all),
        in_specs=[shared, shared, shared, per_dir, per_dir, per_dir],
        out_specs=pl.BlockSpec((1, 1, rows, rw), lambda bd, s: (bd % 2, bd // 2, out_chunk(bd, s), 0)),
        scratch_shapes=[pltpu.VMEM((n_pairs, n2, n2), F32),
                        pltpu.VMEM((n_sub, 1, rw), F32),
                        vm(2 * n2, n2, BF16),
                        vm(2 * n2, n2, BF16),
                        vm(n2, n2, BF16),
                        vm(n2, n2, BF16),
                        vm(n2, n2, BF16),
                        vm(n2, n2, F32),
                        vm(n2, n2, F32),
                        vm(2 * n2, n2, BF16),
                        vm(n2, n2, BF16),
                        vm(n2, n2, F32),
                        vm(n2, n2, BF16),
                        vm(n2, 2 * n2, BF16),
                        vm(n2, n2, F32),
                        vm(n2, n2, BF16),
                        vm(n2, n2, F32),
                        vm(n2, n2, F32),
                        pltpu.VMEM((n_sub, c_n, rw), F32)],
        compiler_params=_cparams(("parallel", "arbitrary")),
        name="rwkv_scan",
    )(r, v, kk, lw, kd, be)


def _rw_step_kernel(rf_ref, rb_ref, vf_ref, vb_ref, kkf_ref, kkb_ref, lwf_ref, lwb_ref, kdf_ref, kdb_ref,
                    bef_ref, beb_ref, yf_ref, yb_ref,
                    z_scr, gc_s, lhs_s, kb_s, kht_s, bht_s, vs_s, rt_s, aab_s, akr_s, arb_s, t_s, x_s, xin_s,
                    yc_s, wu_s, qm_s, yn_s, *, n_pairs, bsz):
    c_n = RW_CHUNK
    n2 = 2 * c_n

    lanes_of = lambda p: slice(p * LANES, (p + 1) * LANES)
    unit = lambda d, b, p: (d * bsz + b) * n_pairs + p
    units = [(d, b, p) for d in range(2) for b in range(bsz) for p in range(n_pairs)]

    @pl.when(pl.program_id(0) == 0)
    def _():
        z_scr[...] = jnp.zeros_like(z_scr)
        qm_s[...] = jnp.zeros_like(qm_s)
        yn_s[...] = jnp.zeros_like(yn_s)

    for d, b, p in units:
        u = unit(d, b, p)
        y_ref = yf_ref if d == 0 else yb_ref
        yz = _dot(qm_s[u], z_scr[u].astype(BF16)) + yn_s[u]
        y_ref[b, :, lanes_of(p)] = yz[:c_n] + yz[c_n:n2]
        z_scr[u] = yz[n2:]

    ri = lax.broadcasted_iota(jnp.int32, (c_n, c_n), 0)
    ci = lax.broadcasted_iota(jnp.int32, (c_n, c_n), 1)
    r2 = lax.broadcasted_iota(jnp.int32, (n2, n2), 0)
    c2 = lax.broadcasted_iota(jnp.int32, (n2, n2), 1)
    t2 = r2 % c_n
    i2 = c2 % c_n
    same_head = (r2 // c_n) == (c2 // c_n)
    diag = r2 == c2
    eye = diag.astype(F32)
    head0 = lax.broadcasted_iota(jnp.int32, (1, LANES), 1) < RW_HEAD
    tri = [(ri >= ci).astype(F32), (ri <= ci).astype(F32)]
    before = [jnp.logical_and(same_head, i2 < t2), jnp.logical_and(same_head, i2 > t2)]
    upto = [jnp.logical_or(m, diag) for m in before]

    def blk(s):
        return (r2 // s) == (c2 // s)

    srcs = [(rf_ref, vf_ref, kkf_ref, lwf_ref, kdf_ref, bef_ref), (rb_ref, vb_ref, kkb_ref, lwb_ref, kdb_ref, beb_ref)]

    for d in range(2):
        r_ref, v_ref, kk_ref, lw_ref, kd_ref, be_ref = srcs[d]
        for b in range(bsz):
            lw = lw_ref[0, b]
            cum = _dot(tri[d], lw, HIGHEST)
            tot = jnp.sum(lw, axis=0, keepdims=True)
            g_inv = jnp.exp(-cum)
            g_hat = jnp.exp(tot - cum)
            kd = kd_ref[0, b]
            be = be_ref[0, b]
            at = kk_ref[b] * jnp.exp(cum - lw)
            rt = r_ref[b] * jnp.exp(cum)
            kt = kd * g_inv
            bt = be * g_inv
            kh = kd * g_hat
            bh = be * g_hat
            gc_s[d * bsz + b] = jnp.exp(tot)
            v = v_ref[b]
            for p in range(n_pairs):
                u = unit(d, b, p)
                st = lambda x: _stack_heads(x[:, lanes_of(p)], head0)
                rt_p = st(rt)
                at_p = st(at).astype(BF16)
                lhs_s[u, :n2] = at_p
                lhs_s[u, n2:] = rt_p.astype(BF16)
                xin_s[u, :, :LANES] = at_p
                rt_s[u] = rt_p
                kb_s[u, :n2] = st(kt).astype(BF16)
                kb_s[u, n2:] = st(bt).astype(BF16)
                kht_s[u] = st(kh).T.astype(BF16)
                bht_s[u] = st(bh).T.astype(BF16)
                vs_s[u] = st(v).astype(BF16)

    for d, b, p in units:
        u = unit(d, b, p)
        g = _dot_nt(lhs_s[u], kb_s[u])
        a_ab = jnp.where(before[d], g[:n2, n2:], 0.0)
        akr_s[u, :n2] = jnp.where(before[d], g[:n2, :n2], 0.0).astype(BF16)
        akr_s[u, n2:] = jnp.where(upto[d], g[n2:, :n2], 0.0).astype(BF16)
        arb_s[u] = jnp.where(upto[d], g[n2:, n2:], 0.0).astype(BF16)
        aab_s[u] = a_ab
        t_s[u] = eye - jnp.where(blk(2), a_ab, 0.0)

    n_units = len(units)
    s = 2
    while s < c_n:
        off = jnp.logical_and(blk(2 * s), jnp.logical_not(blk(s)))
        for u in range(n_units):
            x_s[u] = _dot(t_s[u].astype(BF16), jnp.where(off, aab_s[u], 0.0).astype(BF16)).astype(BF16)
        for u in range(n_units):
            t = t_s[u]
            t_s[u] = t - _dot(x_s[u], t.astype(BF16))
        s *= 2

    for u in range(n_units):
        av = _dot(akr_s[u], vs_s[u])
        xin_s[u, :, LANES:] = av[:n2].astype(BF16)
        yc_s[u] = av[n2:]

    for u in range(n_units):
        wu_s[u] = _dot(t_s[u].astype(BF16), xin_s[u]).astype(BF16)
    for d, b, p in units:
        u = unit(d, b, p)
        wu = wu_s[u]
        q = _dot(arb_s[u], wu)
        bwu = _dot(bht_s[u], wu)
        qm_s[u, :n2] = (rt_s[u] - q[:, :LANES]).astype(BF16)
        qm_s[u, n2:] = (eye * gc_s[d * bsz + b][:, lanes_of(p)] - bwu[:, :LANES]).astype(BF16)
        yn_s[u, :n2] = yc_s[u] - q[:, LANES:]
        yn_s[u, n2:] = _dot(kht_s[u], vs_s[u]) - bwu[:, LANES:]


def _rw_scan2(r, v, kk, lw, kd, be, l_ctx):
    bsz, l_all, rw = r.shape
    c_n = RW_CHUNK
    n_all = l_all // c_n
    n_ctx = l_ctx // c_n
    n_lat = n_all - n_ctx
    n_pairs = rw // LANES
    n_units = 2 * bsz * n_pairs
    n2 = 2 * c_n
    vm = lambda nr, cols, dt: pltpu.VMEM((n_units, nr, cols), dt)

    clamp = lambda s: jnp.minimum(s, n_all - 1)
    chunk_f = lambda s: clamp(s)
    chunk_b = lambda s: jnp.where(clamp(s) < n_ctx, n_ctx - 1 - clamp(s), n_all + n_ctx - 1 - clamp(s))
    prev = lambda s: jnp.maximum(s - 1, 0)
    out_f = lambda s: jnp.maximum(prev(s) - n_ctx, 0)
    out_b = lambda s: jnp.where(prev(s) < n_ctx, n_lat - 1, chunk_b(prev(s)) - n_ctx)

    sh_f = pl.BlockSpec((bsz, c_n, rw), lambda s: (0, chunk_f(s), 0))
    sh_b = pl.BlockSpec((bsz, c_n, rw), lambda s: (0, chunk_b(s), 0))
    pd_f = pl.BlockSpec((1, bsz, c_n, rw), lambda s: (0, 0, chunk_f(s), 0))
    pd_b = pl.BlockSpec((1, bsz, c_n, rw), lambda s: (1, 0, chunk_b(s), 0))
    y_shape = jax.ShapeDtypeStruct((bsz, n_lat * c_n, rw), F32)
    return pl.pallas_call(
        functools.partial(_rw_step_kernel, n_pairs=n_pairs, bsz=bsz),
        out_shape=(y_shape, y_shape),
        grid=(n_all + 1,),
        in_specs=[sh_f, sh_b, sh_f, sh_b, sh_f, sh_b, pd_f, pd_b, pd_f, pd_b, pd_f, pd_b],
        out_specs=(pl.BlockSpec((bsz, c_n, rw), lambda s: (0, out_f(s), 0)),
                   pl.BlockSpec((bsz, c_n, rw), lambda s: (0, out_b(s), 0))),
        scratch_shapes=[vm(n2, n2, F32),
                        pltpu.VMEM((2 * bsz, 1, rw), F32),
                        vm(2 * n2, n2, BF16),
                        vm(2 * n2, n2, BF16),
                        vm(n2, n2, BF16),
                        vm(n2, n2, BF16),
                        vm(n2, n2, BF16),
                        vm(n2, n2, F32),
                        vm(n2, n2, F32),
                        vm(2 * n2, n2, BF16),
                        vm(n2, n2, BF16),
                        vm(n2, n2, F32),
                        vm(n2, n2, BF16),
                        vm(n2, 2 * n2, BF16),
                        vm(n2, n2, F32),
                        vm(n2, 2 * n2, BF16),
                        vm(2 * n2, n2, BF16),
                        vm(2 * n2, n2, F32)],
        compiler_params=_cparams(("arbitrary",)),
        name="rwkv_scan",
    )(r, r, v, v, kk, kk, lw, lw, kd, kd, be, be)


def _s5_glu_kernel(y_ref, wa_ref, wb_ref, o_ref, h_scr):
    @pl.when(pl.program_id(1) == 0)
    def _():
        for jb in range(y_ref.shape[0]):
            h_scr[:, jb * LANES:(jb + 1) * LANES] = _gelu_tanh(y_ref[jb]).astype(BF16)

    h = h_scr[...]
    o_ref[...] = (_dot(h, wa_ref[...]) * _sigmoid(_dot(h, wb_ref[...]))).astype(o_ref.dtype)


def _s5_glu(y_blk, w, *, tm, tn):
    nb, m, _ = y_blk.shape
    k = nb * LANES
    n = w.shape[1] // 2
    nj = n // tn
    return pl.pallas_call(
        _s5_glu_kernel,
        out_shape=jax.ShapeDtypeStruct((m, n), BF16),
        grid=(m // tm, nj),
        in_specs=[pl.BlockSpec((nb, tm, LANES), lambda i, j: (0, i, 0)),
                  pl.BlockSpec((k, tn), lambda i, j: (0, j)),
                  pl.BlockSpec((k, tn), lambda i, j: (0, j + nj))],
        out_specs=pl.BlockSpec((tm, tn), lambda i, j: (i, j)),
        scratch_shapes=[pltpu.VMEM((tm, k), BF16)],
        compiler_params=_cparams(("parallel", "arbitrary")),
        name="s5_glu",
    )(y_blk, w, w)


def _rw_merge_kernel(yf_ref, yb_ref, bonus_ref, g_ref, lnw_ref, lnb_ref, seg_ref, segt_ref,
                     ga_ref, gb_ref, s5_ref, w_ref, o_ref, h_scr):
    @pl.when(pl.program_id(1) == 0)
    def _():
        seg = seg_ref[...]
        segt = segt_ref[...]
        inv_n = 1.0 / RW_HEAD

        def head_mean(t):
            return _dot_hilo(_dot_hilo(t, seg), segt) * inv_n

        y = yf_ref[...] + yb_ref[...]
        dy = y - head_mean(y)
        var = head_mean(dy * dy)
        y = dy * lax.rsqrt(var + GN_EPS) * lnw_ref[...] + lnb_ref[...] + bonus_ref[...]
        h_scr[...] = (y * g_ref[...]).astype(BF16)

    rw_out = _dot(h_scr[...], w_ref[...])
    merged = ga_ref[...].astype(F32) * s5_ref[...].astype(F32) + gb_ref[...].astype(F32) * rw_out
    o_ref[...] = merged.astype(o_ref.dtype)


def _rw_merge(y_f, y_b, bonus, g, ln_w, ln_b, seg, segt, gates, s5_out, w_proj, *, tm, tn):
    m, rw = y_f.shape
    n = w_proj.shape[1]
    nj = n // tn
    full = lambda shape: pl.BlockSpec(shape, lambda i, j: (0,) * len(shape))
    return pl.pallas_call(
        _rw_merge_kernel,
        out_shape=jax.ShapeDtypeStruct((m, n), BF16),
        grid=(m // tm, nj),
        in_specs=[pl.BlockSpec((tm, rw), lambda i, j: (i, 0)),
                  pl.BlockSpec((tm, rw), lambda i, j: (i, 0)),
                  pl.BlockSpec((tm, rw), lambda i, j: (i, 0)),
                  pl.BlockSpec((tm, rw), lambda i, j: (i, 0)),
                  full((1, rw)), full((1, rw)), full(seg.shape), full(segt.shape),
                  pl.BlockSpec((tm, tn), lambda i, j: (i, j)),
                  pl.BlockSpec((tm, tn), lambda i, j: (i, j + nj)),
                  pl.BlockSpec((tm, tn), lambda i, j: (i, j)),
                  pl.BlockSpec((rw, tn), lambda i, j: (0, j))],
        out_specs=pl.BlockSpec((tm, tn), lambda i, j: (i, j)),
        scratch_shapes=[pltpu.VMEM((tm, rw), BF16)],
        compiler_params=_cparams(("parallel", "arbitrary")),
        name="rwkv_merge",
    )(y_f, y_b, bonus, g, ln_w.reshape(1, rw), ln_b.reshape(1, rw), seg, segt,
      gates, gates, s5_out, w_proj)


def _resid_mm_kernel(a_ref, w_ref, x_ref, g_ref, o_ref):
    o_ref[...] = x_ref[...] + g_ref[0] * _dot(a_ref[...], w_ref[...])


def _resid_matmul(a, w, x2, g_tab, mod_row_of_block, *, tm, tn):
    m, k = a.shape
    n = w.shape[1]
    return pl.pallas_call(
        _resid_mm_kernel,
        out_shape=jax.ShapeDtypeStruct((m, n), F32),
        grid=(m // tm, n // tn),
        in_specs=[pl.BlockSpec((tm, k), lambda i, j: (i, 0)),
                  pl.BlockSpec((k, tn), lambda i, j: (0, j)),
                  pl.BlockSpec((tm, tn), lambda i, j: (i, j)),
                  pl.BlockSpec((1, 1, tn), lambda i, j: (mod_row_of_block(i), 0, j))],
        out_specs=pl.BlockSpec((tm, tn), lambda i, j: (i, j)),
        compiler_params=_cparams(("parallel", "arbitrary")),
        name="out_proj",
    )(a, w, x2, g_tab)


def _ffn_down_kernel(a_ref, w_ref, x_ref, g_ref, nf_ref, o_ref, acc_ref):
    kk = pl.program_id(1)

    @pl.when(kk == 0)
    def _():
        acc_ref[...] = jnp.zeros_like(acc_ref)

    acc_ref[...] += _dot(a_ref[...], w_ref[...])

    @pl.when(kk == pl.num_programs(1) - 1)
    def _():
        h = x_ref[...] + g_ref[0] * acc_ref[...]
        ms = jnp.mean(h * h, axis=-1, keepdims=True)
        o_ref[...] = h * lax.rsqrt(ms + NORM_EPS) * nf_ref[...]


def _ffn_down(a, w, x2, g_tab, mod_row_of_block, norm_f, *, tm, tk):
    m, k = a.shape
    n = w.shape[1]
    return pl.pallas_call(
        _ffn_down_kernel,
        out_shape=jax.ShapeDtypeStruct((m, n), F32),
        grid=(m // tm, k // tk),
        in_specs=[pl.BlockSpec((tm, tk), lambda i, kk: (i, kk)),
                  pl.BlockSpec((tk, n), lambda i, kk: (kk, 0)),
                  pl.BlockSpec((tm, n), lambda i, kk: (i, 0)),
                  pl.BlockSpec((1, 1, n), lambda i, kk: (mod_row_of_block(i), 0, 0)),
                  pl.BlockSpec((1, n), lambda i, kk: (0, 0))],
        out_specs=pl.BlockSpec((tm, n), lambda i, kk: (i, 0)),
        scratch_shapes=[pltpu.VMEM((tm, n), F32)],
        compiler_params=_cparams(("parallel", "arbitrary")),
        name="ffn_down",
    )(a, w, x2, g_tab, norm_f.reshape(1, n))


def kernel(x, c, ctx, c_ctx, ada_w, ada_b, norm1_w, w_in, rw_mu, s5_a_re, s5_a_im, s5_log_dt, s5_b_re, s5_b_im, s5_c_re, s5_c_im, s5_d, s5_glu_w, rw_w0, rw_w2, rw_a0, rw_a2, rw_g2, rw_k_k, rw_k_a, rw_r_k, rw_ln_w, rw_ln_b, rw_proj, w_o, norm2_w, ffn_w13, ffn_w2, norm_f):
    assert ada_w.shape[0] == 1, "single-layer block"
    bsz, l_lat, d = x.shape
    l_ctx = ctx.shape[1]
    l_all = l_ctx + l_lat
    s5w = s5_d.shape[1] * s5_d.shape[2]
    rw = rw_g2.shape[2]
    shift_cols = rw_mu.shape[1]
    d_ff = ffn_w2.shape[1]
    n_heads = rw // RW_HEAD

    c_rows = jnp.concatenate([c, c_ctx[None], jnp.zeros((8 - bsz - 1, d), F32)], axis=0)
    mod = _modulation(c_rows, ada_w[0], ada_b[0])
    tab = lambda k: mod[:, k * d:(k + 1) * d].reshape(8, 1, d)
    sh1, sc1, g1, sh2, sc2, g2 = (tab(k) for k in range(N_MOD))
    ctx_row = bsz

    tm_in = 512
    m_lat = bsz * l_lat
    x2 = x.reshape(m_lat, d)
    x_all = jnp.concatenate([x2, ctx.reshape(bsz * l_ctx, d)], axis=0)
    w_in_b = w_in[0].astype(BF16)
    n_mix = s5w + shift_cols
    lat_row = lambda t: (lambda i: i // (l_lat // t))

    def mix_mod_row(i):
        return jnp.where(i >= m_lat // tm_in, ctx_row, i // (l_lat // tm_in))

    u_blk = _lnmod_matmul(x_all, norm1_w[0], sh1, sc1, mix_mod_row, w_in_b[:, :s5w],
                          tm=tm_in, tn=s5w, out_dtype=F32, epilogue="lane_blocks", name="in_proj_s5")
    z_rw = _lnmod_matmul(x_all, norm1_w[0], sh1, sc1, mix_mod_row, w_in_b[:, s5w:n_mix],
                         tm=tm_in, tn=shift_cols // 3, out_dtype=F32, name="in_proj_rw")
    tm = 1024
    gates = _lnmod_matmul(x2, norm1_w[0], sh1, sc1, lat_row(tm), w_in_b[:, n_mix:],
                          tm=tm, tn=1024, out_dtype=BF16, epilogue="sigmoid", name="in_proj_gates")

    y_blk = _s5_branch_blocked(u_blk, bsz, l_ctx, l_lat, s5_a_re[0], s5_a_im[0], s5_log_dt[0],
                               s5_b_re[0], s5_b_im[0], s5_c_re[0], s5_c_im[0], s5_d[0])
    s5_out = _s5_glu(y_blk, s5_glu_w[0].astype(BF16), tm=tm, tn=1024)

    lora = rw_w2.shape[2]
    zl = jnp.zeros((lora, rw), F32)
    w2bd = jnp.concatenate([jnp.concatenate([rw_w2[0, 0], zl], axis=1),
                            jnp.concatenate([zl, rw_w2[0, 1]], axis=1)], axis=0)
    a2bd = jnp.concatenate([jnp.concatenate([rw_a2[0, 0], zl], axis=1),
                            jnp.concatenate([zl, rw_a2[0, 1]], axis=1)], axis=0)
    head_of = jnp.arange(rw) // RW_HEAD
    seg = (head_of[:, None] == jnp.arange(LANES)[None, :]).astype(BF16)
    segt = seg.T
    r, v, kk, g, bonus, lw, kd, be = _rw_prep(
        z_rw, bsz, l_ctx, l_lat, rw_mu[0], w2bd, a2bd, rw_g2[0], rw_w0[0].reshape(1, 2 * rw),
        rw_a0[0].reshape(1, 2 * rw), rw_k_k[0], rw_k_a[0], rw_r_k[0].reshape(rw), seg, segt)
    y_f, y_b = _rw_scan2(r, v, kk, lw, kd, be, l_ctx)

    merged = _rw_merge(y_f.reshape(m_lat, rw), y_b.reshape(m_lat, rw), bonus.reshape(m_lat, rw), g.reshape(m_lat, rw),
                       rw_ln_w[0], rw_ln_b[0], seg, segt, gates, s5_out, rw_proj[0].astype(BF16),
                       tm=512, tn=1024)
    h1 = _resid_matmul(merged, w_o[0].astype(BF16), x2, g1, lat_row(tm), tm=tm, tn=1024)

    act = _lnmod_swiglu(h1, norm2_w[0], sh2, sc2, lat_row(tm), ffn_w13[0].astype(BF16), d_ff,
                        tm=tm, tn=512, name="ffn_up")
    tm_dn = 512
    out = _ffn_down(act, ffn_w2[0].astype(BF16), h1, g2, lat_row(tm_dn), norm_f, tm=tm_dn, tk=d_ff // 4)
    return out.reshape(bsz, l_lat, d)
```

```python
import functools
import math

import jax
import jax.numpy as jnp
from jax import lax
from jax.experimental import pallas as pl
from jax.experimental.pallas import tpu as pltpu

F32 = jnp.float32
BF16 = jnp.bfloat16
HIGHEST = lax.Precision.HIGHEST

D_MODEL = 2048
N_MOD = 6
NORM_EPS = 1e-6
GN_EPS = 64e-5
GRID_W = 64
S5_GROUP = 16
S5_STATE = 64
S5_CHUNK = 16
RW_HEAD = 64
RW_CHUNK = 64
RW_SUB = 2
LANES = 128
VMEM_LIMIT = 48 * 1024 * 1024


def _cparams(sem):
    return pltpu.CompilerParams(dimension_semantics=sem, vmem_limit_bytes=VMEM_LIMIT)


def _operands(a, b, precision):
    if precision == "bf16":
        return a.astype(BF16), b.astype(BF16), None
    return a, b, precision


def _dot(a, b, precision=None):
    a, b, precision = _operands(a, b, precision)
    return jnp.dot(a, b, preferred_element_type=F32, precision=precision)


def _dot_nt(a, b, precision=None):
    a, b, precision = _operands(a, b, precision)
    return lax.dot_general(a, b, (((1,), (1,)), ((), ())), preferred_element_type=F32, precision=precision)


def _dot_hilo(a, ind):
    hi = a.astype(BF16)
    lo = (a - hi.astype(F32)).astype(BF16)
    return _dot(hi, ind) + _dot(lo, ind)


def _sigmoid(x):
    return 1.0 / (1.0 + jnp.exp(-x))


def _silu(x):
    return x * _sigmoid(x)


def _gelu_tanh(x):
    c = math.sqrt(2.0 / math.pi)
    return 0.5 * x * (1.0 + jnp.tanh(c * (x + 0.044715 * (x * x * x))))


def _softplus(x):
    return jnp.maximum(x, 0.0) + jnp.log(1.0 + jnp.exp(-jnp.abs(x)))


def _mod_kernel(c_ref, w_ref, b_ref, o_ref):
    o_ref[...] = _dot(_silu(c_ref[...]), w_ref[...], HIGHEST) + b_ref[...]


def _modulation(c_rows, ada_w, ada_b):
    m, d = c_rows.shape
    n = ada_w.shape[1]
    tn = 1024
    return pl.pallas_call(
        _mod_kernel,
        out_shape=jax.ShapeDtypeStruct((m, n), F32),
        grid=(n // tn,),
        in_specs=[pl.BlockSpec((m, d), lambda j: (0, 0)),
                  pl.BlockSpec((d, tn), lambda j: (0, j)),
                  pl.BlockSpec((1, tn), lambda j: (0, j))],
        out_specs=pl.BlockSpec((m, tn), lambda j: (0, j)),
        compiler_params=_cparams(("arbitrary",)),
        name="modulation",
    )(c_rows, ada_w, ada_b.reshape(1, n))


def _lnmod_rows(x, nw, sh, sc):
    ms = jnp.mean(x * x, axis=-1, keepdims=True)
    y = x * lax.rsqrt(ms + NORM_EPS) * nw
    return y * (1.0 + sc) + sh


def _lnmod_mm_kernel(x_ref, nw_ref, sh_ref, sc_ref, w_ref, o_ref, h_scr, *, epilogue):
    @pl.when(pl.program_id(1) == 0)
    def _():
        h_scr[...] = _lnmod_rows(x_ref[...], nw_ref[...], sh_ref[0], sc_ref[0]).astype(BF16)

    z = _dot(h_scr[...], w_ref[...])
    if epilogue == "sigmoid":
        z = _sigmoid(z)
    if epilogue == "lane_blocks":
        for jb in range(o_ref.shape[0]):
            o_ref[jb] = z[:, jb * LANES:(jb + 1) * LANES].astype(o_ref.dtype)
    else:
        o_ref[...] = z.astype(o_ref.dtype)


def _lnmod_swiglu_kernel(x_ref, nw_ref, sh_ref, sc_ref, w1_ref, w3_ref, o_ref, h_scr):
    @pl.when(pl.program_id(1) == 0)
    def _():
        h_scr[...] = _lnmod_rows(x_ref[...], nw_ref[...], sh_ref[0], sc_ref[0]).astype(BF16)

    h = h_scr[...]
    o_ref[...] = (_silu(_dot(h, w1_ref[...])) * _dot(h, w3_ref[...])).astype(o_ref.dtype)


def _lnmod_matmul(x2, nw, sh_tab, sc_tab, mod_row_of_block, w, *, tm, tn, out_dtype, epilogue=None, name):
    m, d = x2.shape
    n = w.shape[1]
    mod_map = lambda i, j: (mod_row_of_block(i), 0, 0)
    if epilogue == "lane_blocks":
        assert tn == n
        out_shape = jax.ShapeDtypeStruct((n // LANES, m, LANES), out_dtype)
        out_spec = pl.BlockSpec((n // LANES, tm, LANES), lambda i, j: (0, i, 0))
    else:
        out_shape = jax.ShapeDtypeStruct((m, n), out_dtype)
        out_spec = pl.BlockSpec((tm, tn), lambda i, j: (i, j))
    return pl.pallas_call(
        functools.partial(_lnmod_mm_kernel, epilogue=epilogue),
        out_shape=out_shape,
        grid=(m // tm, n // tn),
        in_specs=[pl.BlockSpec((tm, d), lambda i, j: (i, 0)),
                  pl.BlockSpec((1, d), lambda i, j: (0, 0)),
                  pl.BlockSpec((1, 1, d), mod_map),
                  pl.BlockSpec((1, 1, d), mod_map),
                  pl.BlockSpec((d, tn), lambda i, j: (0, j))],
        out_specs=out_spec,
        scratch_shapes=[pltpu.VMEM((tm, d), BF16)],
        compiler_params=_cparams(("parallel", "arbitrary")),
        name=name,
    )(x2, nw.reshape(1, d), sh_tab, sc_tab, w)


def _lnmod_swiglu(x2, nw, sh_tab, sc_tab, mod_row_of_block, w13, d_ff, *, tm, tn, name):
    m, d = x2.shape
    nj = d_ff // tn
    mod_map = lambda i, j: (mod_row_of_block(i), 0, 0)
    return pl.pallas_call(
        _lnmod_swiglu_kernel,
        out_shape=jax.ShapeDtypeStruct((m, d_ff), BF16),
        grid=(m // tm, nj),
        in_specs=[pl.BlockSpec((tm, d), lambda i, j: (i, 0)),
                  pl.BlockSpec((1, d), lambda i, j: (0, 0)),
                  pl.BlockSpec((1, 1, d), mod_map),
                  pl.BlockSpec((1, 1, d), mod_map),
                  pl.BlockSpec((d, tn), lambda i, j: (0, j)),
                  pl.BlockSpec((d, tn), lambda i, j: (0, j + nj))],
        out_specs=pl.BlockSpec((tm, tn), lambda i, j: (i, j)),
        scratch_shapes=[pltpu.VMEM((tm, d), BF16)],
        compiler_params=_cparams(("parallel", "arbitrary")),
        name=name,
    )(x2, nw.reshape(1, d), sh_tab, sc_tab, w13, w13)


def _s5_param_kernel(are_ref, aim_ref, ldt_ref, bre_ref, bim_ref, cre_ref, cim_ref,
                     e_ref, cs_ref, kt_ref, a16_ref):
    t_n, hg, p_n = S5_CHUNK, S5_GROUP, S5_STATE
    tau = lax.broadcasted_iota(jnp.int32, (t_n, 1, p_n), 0).astype(F32)
    for d in range(2):
        a_re = are_ref[0, d:d + 1, :]
        a_im = aim_ref[0, d:d + 1, :]
        dt = jnp.exp(ldt_ref[0, d:d + 1, :])
        lam = a_re * dt
        th = a_im * dt
        er = jnp.exp(lam)
        ab_re = er * jnp.cos(th)
        ab_im = er * jnp.sin(th)
        den = a_re * a_re + a_im * a_im
        x_re = ab_re - 1.0
        co_re = (x_re * a_re + ab_im * a_im) / den
        co_im = (ab_im * a_re - x_re * a_im) / den
        bt_re = bre_ref[0, d]
        bt_im = bim_ref[0, d]
        bb_re = co_re * bt_re - co_im * bt_im
        bb_im = co_re * bt_im + co_im * bt_re
        c_re = cre_ref[0, d]
        c_im = cim_ref[0, d]

        def power(tv):
            mag = jnp.exp(tv * lam)
            return mag * jnp.cos(tv * th), mag * jnp.sin(tv * th)

        pw_re, pw_im = power(tau)
        cp_re = (c_re[None] * pw_re - c_im[None] * pw_im).reshape(t_n * hg, p_n)
        cp_im = (c_re[None] * pw_im + c_im[None] * pw_re).reshape(t_n * hg, p_n)
        kt_ref[0, d] = _dot_nt(bb_re, cp_re, HIGHEST) - _dot_nt(bb_im, cp_im, HIGHEST)

        te = (t_n - 1.0 - tau) if d == 0 else tau
        pe_re, pe_im = power(te)
        e_ref[0, d, :, 0:p_n] = (pe_re * bb_re[None] - pe_im * bb_im[None]).reshape(t_n * hg, p_n)
        e_ref[0, d, :, p_n:2 * p_n] = (pe_re * bb_im[None] + pe_im * bb_re[None]).reshape(t_n * hg, p_n)

        tc = (tau + 1.0) if d == 0 else (t_n - tau)
        pc_re, pc_im = power(tc)
        cs_ref[0, d, :, 0:p_n] = (c_re[None] * pc_re - c_im[None] * pc_im).reshape(t_n * hg, p_n)
        cs_ref[0, d, :, p_n:2 * p_n] = -(c_re[None] * pc_im + c_im[None] * pc_re).reshape(t_n * hg, p_n)

        mag16 = jnp.exp(float(t_n) * lam)
        a16_ref[0, d, 0:1, :] = mag16 * jnp.cos(float(t_n) * th)
        a16_ref[0, d, 1:2, :] = mag16 * jnp.sin(float(t_n) * th)


def _s5_params(a_re, a_im, log_dt, b_re, b_im, c_re, c_im):
    g_n = a_re.shape[1]
    p_n, hg, t_n = S5_STATE, S5_GROUP, S5_CHUNK
    tr = lambda a: jnp.swapaxes(a, 0, 1)
    ldt = jnp.broadcast_to(tr(log_dt)[:, :, None], (g_n, 2, p_n))
    spec3 = pl.BlockSpec((1, 2, p_n), lambda g: (g, 0, 0))
    spec4 = pl.BlockSpec((1, 2, hg, p_n), lambda g: (g, 0, 0, 0))
    th = t_n * hg
    return pl.pallas_call(
        _s5_param_kernel,
        out_shape=(jax.ShapeDtypeStruct((g_n, 2, th, 2 * p_n), F32),
                   jax.ShapeDtypeStruct((g_n, 2, th, 2 * p_n), F32),
                   jax.ShapeDtypeStruct((g_n, 2, hg, th), F32),
                   jax.ShapeDtypeStruct((g_n, 2, 2, p_n), F32)),
        grid=(g_n,),
        in_specs=[spec3, spec3, spec3, spec4, spec4, spec4, spec4],
        out_specs=(pl.BlockSpec((1, 2, th, 2 * p_n), lambda g: (g, 0, 0, 0)),
                   pl.BlockSpec((1, 2, th, 2 * p_n), lambda g: (g, 0, 0, 0)),
                   pl.BlockSpec((1, 2, hg, th), lambda g: (g, 0, 0, 0)),
                   pl.BlockSpec((1, 2, 2, p_n), lambda g: (g, 0, 0, 0))),
        compiler_params=_cparams(("parallel",)),
        name="s5_params",
    )(tr(a_re), tr(a_im), ldt,
      jnp.transpose(b_re, (1, 0, 3, 2)), jnp.transpose(b_im, (1, 0, 3, 2)), tr(c_re), tr(c_im))


def _s5_state_in_kernel(u_ref, e_ref, o_ref):
    e = e_ref[0]
    u = u_ref[0]
    o_ref[0, :, 0:LANES] = _dot(u, e[0], HIGHEST)
    o_ref[0, :, LANES:2 * LANES] = _dot(u, e[1], HIGHEST)


def _s5_state_inputs(u_g, e):
    g_n, rows, th = u_g.shape
    return pl.pallas_call(
        _s5_state_in_kernel,
        out_shape=jax.ShapeDtypeStruct((g_n, rows, 2 * LANES), F32),
        grid=(g_n,),
        in_specs=[pl.BlockSpec((1, rows, th), lambda g: (g, 0, 0)),
                  pl.BlockSpec((1, 2, th, LANES), lambda g: (g, 0, 0, 0))],
        out_specs=pl.BlockSpec((1, rows, 2 * LANES), lambda g: (g, 0, 0)),
        compiler_params=_cparams(("parallel",)),
        name="s5_state_inputs",
    )(u_g, e)


def _s5_scan_kernel(e_ref, a_ref, o_ref, *, n_ctx, n_all):
    zero = jnp.zeros(e_ref.shape[2:], F32)

    def step(pr, pi):
        ar, ai = a_ref[pr], a_ref[pi]

        def body(c, carry):
            sr, si = carry
            o_ref[pr, c] = sr
            o_ref[pi, c] = si
            return (ar * sr - ai * si + e_ref[pr, c], ar * si + ai * sr + e_ref[pi, c])
        return body

    lax.fori_loop(0, n_all, step(0, 1), (zero, zero))
    bwd = step(2, 3)
    carry = lax.fori_loop(0, n_ctx, lambda k, cy: bwd(n_ctx - 1 - k, cy), (zero, zero))
    lax.fori_loop(0, n_all - n_ctx, lambda k, cy: bwd(n_all - 1 - k, cy), carry)


def _s5_scan(e_planes, a_planes, n_ctx, n_all):
    _, rows, r_n, _ = e_planes.shape
    bsz = rows // n_all
    sub = 8
    return pl.pallas_call(
        functools.partial(_s5_scan_kernel, n_ctx=n_ctx, n_all=n_all),
        out_shape=jax.ShapeDtypeStruct(e_planes.shape, F32),
        grid=(bsz, r_n // sub),
        in_specs=[pl.BlockSpec((4, n_all, sub, LANES), lambda b, q: (0, b, q, 0)),
                  pl.BlockSpec((4, sub, LANES), lambda b, q: (0, q, 0))],
        out_specs=pl.BlockSpec((4, n_all, sub, LANES), lambda b, q: (0, b, q, 0)),
        compiler_params=_cparams(("parallel", "parallel")),
        name="s5_scan",
    )(e_planes, a_planes)


def _s5_apply_kernel(u_ref, s_ref, mf_ref, mb_ref, d_ref, cs_ref, o_ref, *, n_ctx):
    u = u_ref[0, n_ctx:, :]
    s = s_ref[0, n_ctx:, :]
    y = _dot(u, mf_ref[0] + mb_ref[0], HIGHEST) + u * d_ref[0]
    y = y + _dot_nt(s[:, 0:LANES], cs_ref[0, 0], HIGHEST) + _dot_nt(s[:, LANES:2 * LANES], cs_ref[0, 1], HIGHEST)
    o_ref[0, 0] = y


def _s5_apply(u_g, s_g, m_f, m_b, d_t, cs, n_ctx, n_all):
    g_n, rows, th = u_g.shape
    bsz = rows // n_all
    n_lat = n_all - n_ctx
    return pl.pallas_call(
        functools.partial(_s5_apply_kernel, n_ctx=n_ctx),
        out_shape=jax.ShapeDtypeStruct((g_n, bsz, n_lat, th), F32),
        grid=(g_n, bsz),
        in_specs=[pl.BlockSpec((1, n_all, th), lambda g, b: (g, b, 0)),
                  pl.BlockSpec((1, n_all, 2 * LANES), lambda g, b: (g, b, 0)),
                  pl.BlockSpec((1, th, th), lambda g, b: (g, 0, 0)),
                  pl.BlockSpec((1, th, th), lambda g, b: (g, 0, 0)),
                  pl.BlockSpec((1, 1, th), lambda g, b: (g, 0, 0)),
                  pl.BlockSpec((1, 2, th, LANES), lambda g, b: (g, 0, 0, 0))],
        out_specs=pl.BlockSpec((1, 1, n_lat, th), lambda g, b: (g, b, 0, 0)),
        compiler_params=_cparams(("parallel", "arbitrary")),
        name="s5_apply",
    )(u_g, s_g, m_f, m_b, d_t, cs)


def _s5_branch(u_all, n_ctx_tok, a_re, a_im, log_dt, b_re, b_im, c_re, c_im, s5_d):
    bsz, l_all, width = u_all.shape
    hg, t_n, p_n = S5_GROUP, S5_CHUNK, S5_STATE
    g_n = width // hg
    n_all = l_all // t_n
    n_ctx = n_ctx_tok // t_n
    th = t_n * hg
    e, cs, kt, a16 = _s5_params(a_re, a_im, log_dt, b_re, b_im, c_re, c_im)

    kt5 = kt.reshape(g_n, 2, hg, t_n, hg)
    ii = jnp.arange(t_n)[:, None]
    jj = jnp.arange(t_n)[None, :]

    def toeplitz(k4, lag, keep):
        m = k4[:, :, jnp.clip(lag, 0, t_n - 1), :]
        m = jnp.where(keep[None, None, :, :, None], m, 0.0)
        return jnp.transpose(m, (0, 2, 1, 3, 4)).reshape(g_n, th, th)

    m_f = toeplitz(kt5[:, 0], jj - ii, jj >= ii)
    m_b = toeplitz(kt5[:, 1], ii - jj, ii >= jj)
    d_t = jnp.tile(s5_d, (1, t_n)).reshape(g_n, 1, th)

    u_g = jnp.transpose(u_all.reshape(bsz, n_all, t_n, g_n, hg), (3, 0, 1, 2, 4)).reshape(g_n, bsz * n_all, th)
    e_cat = e
    s_in = _s5_state_inputs(u_g, e_cat)
    rows = bsz * n_all
    planes = jnp.transpose(s_in.reshape(g_n, rows, 4, p_n), (2, 1, 0, 3)).reshape(4, rows, g_n * p_n // LANES, LANES)
    a_pl = jnp.transpose(a16.reshape(g_n, 4, p_n), (1, 0, 2)).reshape(4, g_n * p_n // LANES, LANES)
    st = _s5_scan(planes, a_pl, n_ctx, n_all)
    s_g = jnp.transpose(st.reshape(4, rows, g_n, p_n), (2, 1, 0, 3)).reshape(g_n, rows, 4 * p_n)
    y_g = _s5_apply(u_g, s_g, m_f, m_b, d_t, cs, n_ctx, n_all)
    n_lat = n_all - n_ctx
    y = jnp.transpose(y_g.reshape(g_n, bsz, n_lat, t_n, hg), (1, 2, 3, 0, 4))
    return y.reshape(bsz, n_lat * t_n, width)


def _s5_param2_kernel(are_ref, aim_ref, ldt_ref, bre_ref, bim_ref, cre_ref, cim_ref,
                      e_ref, c_ref, m_ref, a16_ref):
    t_n, hg, p_n = S5_CHUNK, S5_GROUP, S5_STATE
    tau = lax.broadcasted_iota(jnp.int32, (t_n, 1, p_n), 0).astype(F32)
    taps = []
    for d in range(2):
        a_re = are_ref[0, d:d + 1, :]
        a_im = aim_ref[0, d:d + 1, :]
        dt = jnp.exp(ldt_ref[0, d:d + 1, :])
        lam = a_re * dt
        th = a_im * dt
        er = jnp.exp(lam)
        ab_re = er * jnp.cos(th)
        ab_im = er * jnp.sin(th)
        den = a_re * a_re + a_im * a_im
        x_re = ab_re - 1.0
        co_re = (x_re * a_re + ab_im * a_im) / den
        co_im = (ab_im * a_re - x_re * a_im) / den
        bt_re = bre_ref[0, d]
        bt_im = bim_ref[0, d]
        bb_re = co_re * bt_re - co_im * bt_im
        bb_im = co_re * bt_im + co_im * bt_re
        c_re = cre_ref[0, d]
        c_im = cim_ref[0, d]

        def power(tv):
            mag = jnp.exp(tv * lam)
            return mag * jnp.cos(tv * th), mag * jnp.sin(tv * th)

        pw_re, pw_im = power(tau if d == 0 else (t_n - 1.0 - tau))
        cp_re = (c_re[None] * pw_re - c_im[None] * pw_im).reshape(t_n * hg, p_n)
        cp_im = (c_re[None] * pw_im + c_im[None] * pw_re).reshape(t_n * hg, p_n)
        taps.append(_dot_nt(bb_re, cp_re, HIGHEST) - _dot_nt(bb_im, cp_im, HIGHEST))

        lo, hi = 2 * d * p_n, (2 * d + 1) * p_n
        pe_re, pe_im = power((t_n - 1.0 - tau) if d == 0 else tau)
        e_ref[0, :, 0, :, lo:hi] = pe_re * bb_re[None] - pe_im * bb_im[None]
        e_ref[0, :, 0, :, hi:hi + p_n] = pe_re * bb_im[None] + pe_im * bb_re[None]
        pc_re, pc_im = power((tau + 1.0) if d == 0 else (t_n - tau))
        c_ref[0, :, 0, :, lo:hi] = c_re[None] * pc_re - c_im[None] * pc_im
        c_ref[0, :, 0, :, hi:hi + p_n] = -(c_re[None] * pc_im + c_im[None] * pc_re)

        mag16 = jnp.exp(float(t_n) * lam)
        a16_ref[0, d, 0:1, :] = mag16 * jnp.cos(float(t_n) * th)
        a16_ref[0, d, 1:2, :] = mag16 * jnp.sin(float(t_n) * th)

    width = t_n * hg
    lane = lax.broadcasted_iota(jnp.int32, (hg, width), 1)
    for t in range(t_n):
        sf = t * hg
        sb = (t_n - 1 - t) * hg
        f = taps[0] if sf == 0 else jnp.where(lane >= sf, pltpu.roll(taps[0], sf, 1), 0.0)
        b = taps[1] if sb == 0 else jnp.where(lane < width - sb, pltpu.roll(taps[1], width - sb, 1), 0.0)
        m_ref[0, t, 0] = f + b


def _s5_params2(a_re, a_im, log_dt, b_re, b_im, c_re, c_im, nb):
    g_n = a_re.shape[1]
    gl_n = g_n // nb
    p_n, hg, t_n = S5_STATE, S5_GROUP, S5_CHUNK
    tr = lambda a: jnp.swapaxes(a, 0, 1)
    ldt = jnp.broadcast_to(tr(log_dt)[:, :, None], (g_n, 2, p_n))
    spec3 = pl.BlockSpec((1, 2, p_n), lambda g: (g, 0, 0))
    spec4 = pl.BlockSpec((1, 2, hg, p_n), lambda g: (g, 0, 0, 0))
    wide = 4 * p_n
    comp = jax.ShapeDtypeStruct((nb, t_n, gl_n, hg, wide), F32)
    comp_spec = pl.BlockSpec((1, t_n, 1, hg, wide), lambda g: (g // gl_n, 0, g % gl_n, 0, 0))
    return pl.pallas_call(
        _s5_param2_kernel,
        out_shape=(comp, comp, comp, jax.ShapeDtypeStruct((g_n, 2, 2, p_n), F32)),
        grid=(g_n,),
        in_specs=[spec3, spec3, spec3, spec4, spec4, spec4, spec4],
        out_specs=(comp_spec, comp_spec, comp_spec, pl.BlockSpec((1, 2, 2, p_n), lambda g: (g, 0, 0, 0))),
        compiler_params=_cparams(("parallel",)),
        name="s5_params",
    )(tr(a_re), tr(a_im), ldt,
      jnp.transpose(b_re, (1, 0, 3, 2)), jnp.transpose(b_im, (1, 0, 3, 2)), tr(c_re), tr(c_im))


def _expand_block_diag(comp, rep_ref, mask_ref, w_scr):
    k = w_scr.shape[0]
    period = mask_ref.shape[1]
    cb = comp.astype(BF16)
    step = 512
    for c0 in range(0, k, step):
        blk = _dot(cb, rep_ref[:, c0:c0 + step]).astype(BF16)
        for q0 in range(0, step, period):
            w_scr[:, c0 + q0:c0 + q0 + period] = blk[:, q0:q0 + period] * mask_ref[...]


def _s5_ein_kernel(u_ref, ec_ref, rep_ref, mask_ref, o_ref, w_scr):
    _expand_block_diag(ec_ref[0], rep_ref, mask_ref, w_scr)
    o_ref[0] = _dot(u_ref[0].astype(BF16), w_scr[...])


def _s5_chunk_inputs(u_rows, e_comp, rep_e, mask_e):
    nb, rows, k = u_rows.shape
    cw = e_comp.shape[2]
    return pl.pallas_call(
        _s5_ein_kernel,
        out_shape=jax.ShapeDtypeStruct((nb, rows, k), F32),
        grid=(nb,),
        in_specs=[pl.BlockSpec((1, rows, k), lambda j: (j, 0, 0)),
                  pl.BlockSpec((1, k, cw), lambda j: (j, 0, 0)),
                  pl.BlockSpec(rep_e.shape, lambda j: (0, 0)),
                  pl.BlockSpec(mask_e.shape, lambda j: (0, 0))],
        out_specs=pl.BlockSpec((1, rows, k), lambda j: (j, 0, 0)),
        scratch_shapes=[pltpu.VMEM((k, k), BF16)],
        compiler_params=_cparams(("parallel",)),
        name="s5_chunk_inputs",
    )(u_rows, e_comp, rep_e, mask_e)


def _s5_bscan_kernel(e_ref, a_ref, o_ref, *, bsz, n_ctx, n_lat):
    q = e_ref.shape[2] // 4
    planes = lambda row, d: (row[:, (2 * d) * q:(2 * d + 1) * q], row[:, (2 * d + 1) * q:(2 * d + 2) * q])
    coef = [planes(a_ref[0], d) for d in range(2)]
    ctx0 = bsz * n_lat

    def advance(state, rows):
        new = []
        for (sr, si), (b, d), row in zip(state, [(b, d) for b in range(bsz) for d in range(2)], rows):
            ar, ai = coef[d]
            er, ei = planes(e_ref[0, pl.ds(row, 1), :], d)
            new.append((ar * sr - ai * si + er, ar * si + ai * sr + ei))
        return tuple(new)

    def ctx_step(s, state):
        rows = [ctx0 + b * n_ctx + (s if d == 0 else n_ctx - 1 - s) for b in range(bsz) for d in range(2)]
        return advance(state, rows)

    def lat_step(s, state):
        rows = [b * n_lat + (s if d == 0 else n_lat - 1 - s) for b in range(bsz) for d in range(2)]
        for (sr, si), (b, d), row in zip(state, [(b, d) for b in range(bsz) for d in range(2)], rows):
            o_ref[0, pl.ds(row, 1), (2 * d) * q:(2 * d + 1) * q] = sr
            o_ref[0, pl.ds(row, 1), (2 * d + 1) * q:(2 * d + 2) * q] = si
        return advance(state, rows)

    zero = jnp.zeros((1, q), F32)
    state = tuple((zero, zero) for _ in range(2 * bsz))
    state = lax.fori_loop(0, n_ctx, ctx_step, state)
    lax.fori_loop(0, n_lat, lat_step, state)


def _s5_bscan(e_rows, a_rows, bsz, n_ctx, n_lat):
    nb, rows, k = e_rows.shape
    return pl.pallas_call(
        functools.partial(_s5_bscan_kernel, bsz=bsz, n_ctx=n_ctx, n_lat=n_lat),
        out_shape=jax.ShapeDtypeStruct((nb, bsz * n_lat, k), F32),
        grid=(nb,),
        in_specs=[pl.BlockSpec((1, rows, k), lambda j: (j, 0, 0)),
                  pl.BlockSpec((1, 1, k), lambda j: (j, 0, 0))],
        out_specs=pl.BlockSpec((1, bsz * n_lat, k), lambda j: (j, 0, 0)),
        compiler_params=_cparams(("parallel",)),
        name="s5_scan",
    )(e_rows, a_rows)


def _s5_out_kernel(u_ref, s_ref, mc_ref, cc_ref, d_ref, rep_m_ref, mask_m_ref, rep_e_ref, mask_e_ref,
                   o_ref, wm_scr, wc_scr):
    @pl.when(pl.program_id(1) == 0)
    def _():
        _expand_block_diag(mc_ref[0], rep_m_ref, mask_m_ref, wm_scr)
        _expand_block_diag(cc_ref[0], rep_e_ref, mask_e_ref, wc_scr)

    u = u_ref[0]
    y = _dot(u.astype(BF16), wm_scr[...]) + _dot_nt(s_ref[0].astype(BF16), wc_scr[...])
    o_ref[0] = y + u * d_ref[0]


def _s5_outputs(u_rows, s_rows, m_comp, c_comp, d_rows, rep_m, mask_m, rep_e, mask_e):
    nb, _, k = u_rows.shape
    rows = s_rows.shape[1]
    cw = m_comp.shape[2]
    tr = rows // 2
    const = lambda a: pl.BlockSpec(a.shape, lambda j, i: (0, 0))
    return pl.pallas_call(
        _s5_out_kernel,
        out_shape=jax.ShapeDtypeStruct((nb, rows, k), F32),
        grid=(nb, rows // tr),
        in_specs=[pl.BlockSpec((1, tr, k), lambda j, i: (j, i, 0)),
                  pl.BlockSpec((1, tr, k), lambda j, i: (j, i, 0)),
                  pl.BlockSpec((1, k, cw), lambda j, i: (j, 0, 0)),
                  pl.BlockSpec((1, k, cw), lambda j, i: (j, 0, 0)),
                  pl.BlockSpec((1, 1, k), lambda j, i: (j, 0, 0)),
                  const(rep_m), const(mask_m), const(rep_e), const(mask_e)],
        out_specs=pl.BlockSpec((1, tr, k), lambda j, i: (j, i, 0)),
        scratch_shapes=[pltpu.VMEM((k, k), BF16), pltpu.VMEM((k, k), BF16)],
        compiler_params=_cparams(("parallel", "arbitrary")),
        name="s5_outputs",
    )(u_rows, s_rows, m_comp, c_comp, d_rows, rep_m, mask_m, rep_e, mask_e)


def _s5_branch_blocked(u_blk, bsz, l_ctx, l_lat, a_re, a_im, log_dt, b_re, b_im, c_re, c_im, s5_d):
    nb, m_all, _ = u_blk.shape
    hg, t_n, p_n = S5_GROUP, S5_CHUNK, S5_STATE
    g_n = a_re.shape[1]
    gl_n = g_n // nb
    k = t_n * LANES
    n_lat = l_lat // t_n
    n_ctx = l_ctx // t_n
    e_c, c_c, m_c, a16 = _s5_params2(a_re, a_im, log_dt, b_re, b_im, c_re, c_im, nb)
    cw = 4 * p_n
    e_comp, c_comp, m_comp = (a.reshape(nb, k, cw) for a in (e_c, c_c, m_c))

    row_gl = (jnp.arange(k) // hg) % gl_n
    col = jnp.arange(k)
    src = jnp.arange(cw)
    rep_e = ((src[:, None] // p_n == col[None, :] // (gl_n * p_n)) & (src[:, None] % p_n == col[None, :] % p_n)).astype(BF16)
    mask_e = (row_gl[:, None] == (jnp.arange(gl_n * p_n)[None, :] // p_n)).astype(BF16)
    rep_m = ((src[:, None] // hg == col[None, :] // (gl_n * hg)) & (src[:, None] % hg == col[None, :] % hg)).astype(BF16)
    mask_m = (row_gl[:, None] == (jnp.arange(gl_n * hg)[None, :] // hg)).astype(BF16)

    d_rows = jnp.tile(s5_d.reshape(nb, 1, gl_n * hg), (1, 1, t_n))
    a_rows = jnp.transpose(a16.reshape(nb, gl_n, 2, 2, p_n), (0, 2, 3, 1, 4)).reshape(nb, 1, 4 * gl_n * p_n)

    u_rows = u_blk.reshape(nb, m_all // t_n, k)
    e_rows = _s5_chunk_inputs(u_rows, e_comp, rep_e, mask_e)
    s_rows = _s5_bscan(e_rows, a_rows, bsz, n_ctx, n_lat)
    y_rows = _s5_outputs(u_rows, s_rows, m_comp, c_comp, d_rows, rep_m, mask_m, rep_e, mask_e)
    return y_rows.reshape(nb, bsz * l_lat, LANES)


def _rw_prep_kernel(z_ref, zp_ref, zn_ref, mu_ref, w2_ref, a2_ref, g2_ref, w0_ref, a0_ref,
                    kk_w_ref, ka_ref, rk_ref, seg_ref, segt_ref,
                    r_ref, v_ref, kk_ref, g_ref, bonus_ref, lw_ref, kd_ref, be_ref,
                    *, tm, l_lat, rw):
    j = pl.program_id(1)
    z = z_ref[...].astype(F32)
    lat = j > 0
    tl = lax.broadcasted_iota(jnp.int32, (tm, 1), 0)
    tok = (j - 1) * tm + tl
    col = tl % GRID_W
    m_l = jnp.where(lat, col, tl) > 0
    m_r = jnp.where(lat, col - (GRID_W - 1), tl - (tm - 1)) < 0
    m_u = jnp.logical_and(lat, tok >= GRID_W)
    m_d = jnp.logical_and(lat, tok < l_lat - GRID_W)
    z_l = pltpu.roll(z, 1, 0)
    z_r = pltpu.roll(z, tm - 1, 0)
    z_u = jnp.concatenate([zp_ref[...].astype(F32), z[:tm - GRID_W]], axis=0)
    z_d = jnp.concatenate([z[GRID_W:], zn_ref[...].astype(F32)], axis=0)
    s = (jnp.where(m_l, z_l, 0.0) + jnp.where(m_r, z_r, 0.0)
         + jnp.where(m_u, z_u, 0.0) + jnp.where(m_d, z_d, 0.0))
    cnt = (m_l.astype(F32) + m_r.astype(F32)) + (m_u.astype(F32) + m_d.astype(F32))
    zs = z + (s / cnt - z) * mu_ref[...]

    r = zs[:, 0:rw]
    k = zs[:, rw:2 * rw]
    v = zs[:, 2 * rw:3 * rw]
    o = 3 * rw
    wd = zs[:, o:o + LANES]
    ad = zs[:, o + LANES:o + 2 * LANES]
    gd = zs[:, o + 2 * LANES:o + 3 * LANES]

    seg = seg_ref[...]
    segt = segt_ref[...]

    def head_sum(t):
        return _dot_hilo(_dot_hilo(t, seg), segt)

    g_ref[0] = _dot(_sigmoid(gd), g2_ref[...], "bf16").astype(g_ref.dtype)
    kk = k * kk_w_ref[...]
    kk = kk * lax.rsqrt(head_sum(kk * kk) + 1e-12)
    wl = w0_ref[...] + _dot(jnp.tanh(wd), w2_ref[...], "bf16")
    al = a0_ref[...] + _dot(ad, a2_ref[...], "bf16")
    r_ref[0] = r.astype(r_ref.dtype)
    v_ref[0] = v.astype(v_ref.dtype)
    kk_ref[0] = kk.astype(kk_ref.dtype)
    coef = jnp.zeros_like(r)
    for d in range(2):
        w_raw = -_softplus(-wl[:, d * rw:(d + 1) * rw]) - 0.5
        a = _sigmoid(al[:, d * rw:(d + 1) * rw])
        k_d = k * (1.0 + (a - 1.0) * ka_ref[...])
        lw_ref[d, 0] = -jnp.exp(w_raw)
        kd_ref[d, 0] = k_d.astype(kd_ref.dtype)
        be_ref[d, 0] = (kk * a).astype(be_ref.dtype)
        coef = coef + head_sum(r * k_d * rk_ref[...])
    bonus_ref[0] = (coef * v).astype(bonus_ref.dtype)


def _rw_prep(z_rw, bsz, l_ctx, l_lat, mu, w2bd, a2bd, g2, w0cat, a0cat, k_k, k_a, r_k_flat, seg, segt):
    cols = z_rw.shape[1]
    tm = l_ctx
    rw = g2.shape[1]
    l_all = l_ctx + l_lat
    nblk = l_all // tm
    lat_blk = l_lat // tm
    hb = tm // GRID_W
    lat_hblk = l_lat // GRID_W

    def main_blk(b, j):
        return jnp.where(j == 0, bsz * lat_blk + b, b * lat_blk + j - 1)

    def prev_halo(b, j):
        return b * lat_hblk + jnp.maximum((j - 1) * hb - 1, 0)

    def next_halo(b, j):
        return b * lat_hblk + jnp.minimum(jnp.maximum(j, 1) * hb, lat_hblk - 1)

    full = lambda shape: pl.BlockSpec(shape, lambda b, j: (0,) * len(shape))
    shared = jax.ShapeDtypeStruct((bsz, l_all, rw), BF16)
    lat_only = jax.ShapeDtypeStruct((bsz, l_lat, rw), BF16)
    per_dir = jax.ShapeDtypeStruct((2, bsz, l_all, rw), BF16)
    per_dir_f32 = jax.ShapeDtypeStruct((2, bsz, l_all, rw), F32)
    o_shared = pl.BlockSpec((1, tm, rw), lambda b, j: (b, j, 0))
    o_lat = pl.BlockSpec((1, tm, rw), lambda b, j: (b, jnp.maximum(j - 1, 0), 0))
    o_dir = pl.BlockSpec((2, 1, tm, rw), lambda b, j: (0, b, j, 0))
    return pl.pallas_call(
        functools.partial(_rw_prep_kernel, tm=tm, l_lat=l_lat, rw=rw),
        out_shape=(shared,) * 3 + (lat_only,) * 2 + (per_dir_f32, per_dir, per_dir),
        grid=(bsz, nblk),
        in_specs=[pl.BlockSpec((tm, cols), lambda b, j: (main_blk(b, j), 0)),
                  pl.BlockSpec((GRID_W, cols), lambda b, j: (prev_halo(b, j), 0)),
                  pl.BlockSpec((GRID_W, cols), lambda b, j: (next_halo(b, j), 0)),
                  full((1, cols)), full(w2bd.shape), full(a2bd.shape), full(g2.shape),
                  full((1, 2 * rw)), full((1, 2 * rw)), full((1, rw)), full((1, rw)), full((1, rw)),
                  full(seg.shape), full(segt.shape)],
        out_specs=(o_shared,) * 3 + (o_lat,) * 2 + (o_dir,) * 3,
        compiler_params=_cparams(("parallel", "arbitrary")),
        name="rwkv_prep",
    )(z_rw, z_rw, z_rw, mu.reshape(1, cols), w2bd, a2bd, g2, w0cat, a0cat,
      k_k.reshape(1, rw), k_a.reshape(1, rw), r_k_flat.reshape(1, rw), seg, segt)


def _stack_heads(x, head0):
    return jnp.concatenate([jnp.where(head0, x, 0.0), jnp.where(head0, 0.0, x)], axis=0)


def _rw_chunk_kernel(r_ref, v_ref, kk_ref, lw_ref, kd_ref, be_ref, y_ref,
                     z_scr, gc_s, lhs_s, kb_s, kht_s, bht_s, vs_s, rt_s, aab_s, akr_s, arb_s, t_s, x_s, xin_s,
                     yc_s, qy_s, bw_s, n_s, y_s, *, n_pairs, n_sub):
    c_n = RW_CHUNK
    n2 = 2 * c_n
    rev = (pl.program_id(0) % 2) == 1

    @pl.when(pl.program_id(1) == 0)
    def _():
        z_scr[...] = jnp.zeros_like(z_scr)

    ri = lax.broadcasted_iota(jnp.int32, (c_n, c_n), 0)
    ci = lax.broadcasted_iota(jnp.int32, (c_n, c_n), 1)
    tri = (jnp.where(rev, ci - ri, ri - ci) >= 0).astype(F32)
    r2 = lax.broadcasted_iota(jnp.int32, (n2, n2), 0)
    c2 = lax.broadcasted_iota(jnp.int32, (n2, n2), 1)
    t2 = r2 % c_n
    i2 = c2 % c_n
    same_head = (r2 // c_n) == (c2 // c_n)
    before = jnp.logical_and(same_head, jnp.where(rev, i2 - t2, t2 - i2) > 0)
    upto = jnp.logical_or(before, r2 == c2)
    eye = (r2 == c2).astype(F32)
    head0 = lax.broadcasted_iota(jnp.int32, (1, LANES), 1) < RW_HEAD

    def blk(s):
        return (r2 // s) == (c2 // s)

    pairs = range(n_pairs)
    units = range(n_sub * n_pairs)
    lanes_of = lambda p: slice(p * LANES, (p + 1) * LANES)

    row_of = lambda k: pl.multiple_of(jnp.where(rev, n_sub - 1 - k, k) * c_n, c_n)
    for cc in range(n_sub):
        rows = pl.ds(row_of(cc), c_n)
        lw = lw_ref[0, 0, rows, :]
        cum = _dot(tri, lw, HIGHEST)
        tot = jnp.sum(lw, axis=0, keepdims=True)
        g_inv = jnp.exp(-cum)
        g_hat = jnp.exp(tot - cum)
        kd = kd_ref[0, 0, rows, :]
        be = be_ref[0, 0, rows, :]
        at = kk_ref[0, rows, :] * jnp.exp(cum - lw)
        rt = r_ref[0, rows, :] * jnp.exp(cum)
        kt = kd * g_inv
        bt = be * g_inv
        kh = kd * g_hat
        bh = be * g_hat
        gc_s[cc] = jnp.exp(tot)
        v = v_ref[0, rows, :]
        for p in pairs:
            u = cc * n_pairs + p
            st = lambda x: _stack_heads(x[:, lanes_of(p)], head0)
            rt_p = st(rt)
            at_p = st(at).astype(BF16)
            lhs_s[u, :n2] = at_p
            lhs_s[u, n2:] = rt_p.astype(BF16)
            xin_s[u, :, :LANES] = at_p
            rt_s[u] = rt_p
            kb_s[u, :n2] = st(kt).astype(BF16)
            kb_s[u, n2:] = st(bt).astype(BF16)
            kht_s[u] = st(kh).T.astype(BF16)
            bht_s[u] = st(bh).T.astype(BF16)
            vs_s[u] = st(v).astype(BF16)

    for u in units:
        g = _dot_nt(lhs_s[u], kb_s[u])
        a_ab = jnp.where(before, g[:n2, n2:], 0.0)
        akr_s[u, :n2] = jnp.where(before, g[:n2, :n2], 0.0).astype(BF16)
        akr_s[u, n2:] = jnp.where(upto, g[n2:, :n2], 0.0).astype(BF16)
        arb_s[u] = jnp.where(upto, g[n2:, n2:], 0.0).astype(BF16)
        aab_s[u] = a_ab
        t_s[u] = eye - jnp.where(blk(2), a_ab, 0.0)

    s = 2
    while s < c_n:
        off = jnp.logical_and(blk(2 * s), jnp.logical_not(blk(s)))
        for u in units:
            x_s[u] = _dot(t_s[u].astype(BF16), jnp.where(off, aab_s[u], 0.0).astype(BF16)).astype(BF16)
        for u in units:
            t = t_s[u]
            t_s[u] = t - _dot(x_s[u], t.astype(BF16))
        s *= 2

    for u in units:
        av = _dot(akr_s[u], vs_s[u])
        xin_s[u, :, LANES:] = av[:n2].astype(BF16)
        yc_s[u] = av[n2:]

    for u in units:
        wu = _dot(t_s[u].astype(BF16), xin_s[u]).astype(BF16)
        q = _dot(arb_s[u], wu)
        bwu = _dot(bht_s[u], wu)
        qy_s[u] = (rt_s[u] - q[:, :LANES]).astype(BF16)
        yc_s[u] = yc_s[u] - q[:, LANES:]
        bw_s[u] = bwu[:, :LANES]
        n_s[u] = _dot(kht_s[u], vs_s[u]) - bwu[:, LANES:]

    for k in range(n_sub):
        g_c = gc_s[k]
        for p in pairs:
            u = k * n_pairs + p
            z0 = z_scr[p].astype(BF16)
            y_p = _dot(qy_s[u], z0) + yc_s[u]
            y_s[k, :, lanes_of(p)] = y_p[:c_n] + y_p[c_n:]
            m_z = eye * g_c[:, lanes_of(p)] - bw_s[u]
            z_scr[p] = _dot(m_z.astype(BF16), z0) + n_s[u]
    for k in range(n_sub):
        y_ref[0, 0, pl.ds(row_of(k), c_n), :] = y_s[k]


def _rw_scan(r, v, kk, lw, kd, be, l_ctx):
    bsz, l_all, rw = r.shape
    n_sub = RW_SUB
    c_n = RW_CHUNK
    rows = n_sub * c_n
    n_all = l_all // rows
    n_ctx = l_ctx // rows
    n_lat = n_all - n_ctx
    n_pairs = rw // LANES
    n2 = 2 * c_n
    vm = lambda nr, cols, dt: pltpu.VMEM((n_sub * n_pairs, nr, cols), dt)

    def chunk_of(bd, s):
        fwd = s
        bwd = jnp.where(s < n_ctx, n_ctx - 1 - s, n_all + n_ctx - 1 - s)
        return jnp.where(bd % 2 == 0, fwd, bwd)

    def out_chunk(bd, s):
        c = chunk_of(bd, s)
        edge = jnp.where(bd % 2 == 0, 0, n_lat - 1)
        return jnp.where(s < n_ctx, edge, c - n_ctx)

    shared = pl.BlockSpec((1, rows, rw), lambda bd, s: (bd // 2, chunk_of(bd, s), 0))
    per_dir = pl.BlockSpec((1, 1, rows, rw), lambda bd, s: (bd % 2, bd // 2, chunk_of(bd, s), 0))
    return pl.pallas_call(
        functools.partial(_rw_chunk_kernel, n_pairs=n_pairs, n_sub=n_sub),
        out_shape=jax.ShapeDtypeStruct((2, bsz, n_lat * rows, rw), F32),
        grid=(2 * bsz, n_all),
        in_specs=[shared, shared, shared, per_dir, per_dir, per_dir],
        out_specs=pl.BlockSpec((1, 1, rows, rw), lambda bd, s: (bd % 2, bd // 2, out_chunk(bd, s), 0)),
        scratch_shapes=[pltpu.VMEM((n_pairs, n2, n2), F32),
                        pltpu.VMEM((n_sub, 1, rw), F32),
                        vm(2 * n2, n2, BF16),
                        vm(2 * n2, n2, BF16),
                        vm(n2, n2, BF16),
                        vm(n2, n2, BF16),
                        vm(n2, n2, BF16),
                        vm(n2, n2, F32),
                        vm(n2, n2, F32),
                        vm(2 * n2, n2, BF16),
                        vm(n2, n2, BF16),
                        vm(n2, n2, F32),
                        vm(n2, n2, BF16),
                        vm(n2, 2 * n2, BF16),
                        vm(n2, n2, F32),
                        vm(n2, n2, BF16),
                        vm(n2, n2, F32),
                        vm(n2, n2, F32),
                        pltpu.VMEM((n_sub, c_n, rw), F32)],
        compiler_params=_cparams(("parallel", "arbitrary")),
        name="rwkv_scan",
    )(r, v, kk, lw, kd, be)


def _rw_step_kernel(rf_ref, rb_ref, vf_ref, vb_ref, kkf_ref, kkb_ref, lwf_ref, lwb_ref, kdf_ref, kdb_ref,
                    bef_ref, beb_ref, yf_ref, yb_ref,
                    z_scr, gc_s, lhs_s, kb_s, kht_s, bht_s, vs_s, rt_s, aab_s, akr_s, arb_s, t_s, x_s, xin_s,
                    yc_s, wu_s, qm_s, yn_s, *, n_pairs, bsz):
    c_n = RW_CHUNK
    n2 = 2 * c_n

    lanes_of = lambda p: slice(p * LANES, (p + 1) * LANES)
    unit = lambda d, b, p: (d * bsz + b) * n_pairs + p
    units = [(d, b, p) for d in range(2) for b in range(bsz) for p in range(n_pairs)]

    @pl.when(pl.program_id(0) == 0)
    def _():
        z_scr[...] = jnp.zeros_like(z_scr)
        qm_s[...] = jnp.zeros_like(qm_s)
        yn_s[...] = jnp.zeros_like(yn_s)

    for d, b, p in units:
        u = unit(d, b, p)
        y_ref = yf_ref if d == 0 else yb_ref
        yz = _dot(qm_s[u], z_scr[u].astype(BF16)) + yn_s[u]
        y_ref[b, :, lanes_of(p)] = yz[:c_n] + yz[c_n:n2]
        z_scr[u] = yz[n2:]

    ri = lax.broadcasted_iota(jnp.int32, (c_n, c_n), 0)
    ci = lax.broadcasted_iota(jnp.int32, (c_n, c_n), 1)
    r2 = lax.broadcasted_iota(jnp.int32, (n2, n2), 0)
    c2 = lax.broadcasted_iota(jnp.int32, (n2, n2), 1)
    t2 = r2 % c_n
    i2 = c2 % c_n
    same_head = (r2 // c_n) == (c2 // c_n)
    diag = r2 == c2
    eye = diag.astype(F32)
    head0 = lax.broadcasted_iota(jnp.int32, (1, LANES), 1) < RW_HEAD
    tri = [(ri >= ci).astype(F32), (ri <= ci).astype(F32)]
    before = [jnp.logical_and(same_head, i2 < t2), jnp.logical_and(same_head, i2 > t2)]
    upto = [jnp.logical_or(m, diag) for m in before]

    def blk(s):
        return (r2 // s) == (c2 // s)

    srcs = [(rf_ref, vf_ref, kkf_ref, lwf_ref, kdf_ref, bef_ref), (rb_ref, vb_ref, kkb_ref, lwb_ref, kdb_ref, beb_ref)]

    for d in range(2):
        r_ref, v_ref, kk_ref, lw_ref, kd_ref, be_ref = srcs[d]
        for b in range(bsz):
            lw = lw_ref[0, b]
            cum = _dot(tri[d], lw, HIGHEST)
            tot = jnp.sum(lw, axis=0, keepdims=True)
            g_inv = jnp.exp(-cum)
            g_hat = jnp.exp(tot - cum)
            kd = kd_ref[0, b].astype(F32)
            be = be_ref[0, b].astype(F32)
            at = kk_ref[b].astype(F32) * jnp.exp(cum - lw)
            rt = r_ref[b].astype(F32) * jnp.exp(cum)
            kt = kd * g_inv
            bt = be * g_inv
            kh = kd * g_hat
            bh = be * g_hat
            gc_s[d * bsz + b] = jnp.exp(tot)
            v = v_ref[b].astype(F32)
            for p in range(n_pairs):
                u = unit(d, b, p)
                st = lambda x: _stack_heads(x[:, lanes_of(p)], head0)
                rt_p = st(rt)
                at_p = st(at).astype(BF16)
                lhs_s[u, :n2] = at_p
                lhs_s[u, n2:] = rt_p.astype(BF16)
                xin_s[u, :, :LANES] = at_p
                rt_s[u] = rt_p
                kb_s[u, :n2] = st(kt).astype(BF16)
                kb_s[u, n2:] = st(bt).astype(BF16)
                kht_s[u] = st(kh).T.astype(BF16)
                bht_s[u] = st(bh).T.astype(BF16)
                vs_s[u] = st(v).astype(BF16)

    for d, b, p in units:
        u = unit(d, b, p)
        g = _dot_nt(lhs_s[u], kb_s[u])
        a_ab = jnp.where(before[d], g[:n2, n2:], 0.0)
        akr_s[u, :n2] = jnp.where(before[d], g[:n2, :n2], 0.0).astype(BF16)
        akr_s[u, n2:] = jnp.where(upto[d], g[n2:, :n2], 0.0).astype(BF16)
        arb_s[u] = jnp.where(upto[d], g[n2:, n2:], 0.0).astype(BF16)
        aab_s[u] = a_ab
        t_s[u] = eye - jnp.where(blk(2), a_ab, 0.0)

    n_units = len(units)
    s = 2
    while s < c_n:
        off = jnp.logical_and(blk(2 * s), jnp.logical_not(blk(s)))
        for u in range(n_units):
            x_s[u] = _dot(t_s[u].astype(BF16), jnp.where(off, aab_s[u], 0.0).astype(BF16)).astype(BF16)
        for u in range(n_units):
            t = t_s[u]
            t_s[u] = t - _dot(x_s[u], t.astype(BF16))
        s *= 2

    for u in range(n_units):
        av = _dot(akr_s[u], vs_s[u])
        xin_s[u, :, LANES:] = av[:n2].astype(BF16)
        yc_s[u] = av[n2:]

    for u in range(n_units):
        wu_s[u] = _dot(t_s[u].astype(BF16), xin_s[u]).astype(BF16)
    for d, b, p in units:
        u = unit(d, b, p)
        wu = wu_s[u]
        q = _dot(arb_s[u], wu)
        bwu = _dot(bht_s[u], wu)
        qm_s[u, :n2] = (rt_s[u] - q[:, :LANES]).astype(BF16)
        qm_s[u, n2:] = (eye * gc_s[d * bsz + b][:, lanes_of(p)] - bwu[:, :LANES]).astype(BF16)
        yn_s[u, :n2] = yc_s[u] - q[:, LANES:]
        yn_s[u, n2:] = _dot(kht_s[u], vs_s[u]) - bwu[:, LANES:]


def _rw_scan2(r, v, kk, lw, kd, be, l_ctx):
    bsz, l_all, rw = r.shape
    c_n = RW_CHUNK
    n_all = l_all // c_n
    n_ctx = l_ctx // c_n
    n_lat = n_all - n_ctx
    n_pairs = rw // LANES
    n_units = 2 * bsz * n_pairs
    n2 = 2 * c_n
    vm = lambda nr, cols, dt: pltpu.VMEM((n_units, nr, cols), dt)

    clamp = lambda s: jnp.minimum(s, n_all - 1)
    chunk_f = lambda s: clamp(s)
    chunk_b = lambda s: jnp.where(clamp(s) < n_ctx, n_ctx - 1 - clamp(s), n_all + n_ctx - 1 - clamp(s))
    prev = lambda s: jnp.maximum(s - 1, 0)
    out_f = lambda s: jnp.maximum(prev(s) - n_ctx, 0)
    out_b = lambda s: jnp.where(prev(s) < n_ctx, n_lat - 1, chunk_b(prev(s)) - n_ctx)

    sh_f = pl.BlockSpec((bsz, c_n, rw), lambda s: (0, chunk_f(s), 0))
    sh_b = pl.BlockSpec((bsz, c_n, rw), lambda s: (0, chunk_b(s), 0))
    pd_f = pl.BlockSpec((1, bsz, c_n, rw), lambda s: (0, 0, chunk_f(s), 0))
    pd_b = pl.BlockSpec((1, bsz, c_n, rw), lambda s: (1, 0, chunk_b(s), 0))
    y_shape = jax.ShapeDtypeStruct((bsz, n_lat * c_n, rw), F32)
    return pl.pallas_call(
        functools.partial(_rw_step_kernel, n_pairs=n_pairs, bsz=bsz),
        out_shape=(y_shape, y_shape),
        grid=(n_all + 1,),
        in_specs=[sh_f, sh_b, sh_f, sh_b, sh_f, sh_b, pd_f, pd_b, pd_f, pd_b, pd_f, pd_b],
        out_specs=(pl.BlockSpec((bsz, c_n, rw), lambda s: (0, out_f(s), 0)),
                   pl.BlockSpec((bsz, c_n, rw), lambda s: (0, out_b(s), 0))),
        scratch_shapes=[vm(n2, n2, F32),
                        pltpu.VMEM((2 * bsz, 1, rw), F32),
                        vm(2 * n2, n2, BF16),
                        vm(2 * n2, n2, BF16),
                        vm(n2, n2, BF16),
                        vm(n2, n2, BF16),
                        vm(n2, n2, BF16),
                        vm(n2, n2, F32),
                        vm(n2, n2, F32),
                        vm(2 * n2, n2, BF16),
                        vm(n2, n2, BF16),
                        vm(n2, n2, F32),
                        vm(n2, n2, BF16),
                        vm(n2, 2 * n2, BF16),
                        vm(n2, n2, F32),
                        vm(n2, 2 * n2, BF16),
                        vm(2 * n2, n2, BF16),
                        vm(2 * n2, n2, F32)],
        compiler_params=_cparams(("arbitrary",)),
        name="rwkv_scan",
    )(r, r, v, v, kk, kk, lw, lw, kd, kd, be, be)


def _s5_glu_kernel(y_ref, wa_ref, wb_ref, o_ref, h_scr):
    @pl.when(pl.program_id(1) == 0)
    def _():
        for jb in range(y_ref.shape[0]):
            h_scr[:, jb * LANES:(jb + 1) * LANES] = _gelu_tanh(y_ref[jb]).astype(BF16)

    h = h_scr[...]
    o_ref[...] = (_dot(h, wa_ref[...]) * _sigmoid(_dot(h, wb_ref[...]))).astype(o_ref.dtype)


def _s5_glu(y_blk, w, *, tm, tn):
    nb, m, _ = y_blk.shape
    k = nb * LANES
    n = w.shape[1] // 2
    nj = n // tn
    return pl.pallas_call(
        _s5_glu_kernel,
        out_shape=jax.ShapeDtypeStruct((m, n), BF16),
        grid=(m // tm, nj),
        in_specs=[pl.BlockSpec((nb, tm, LANES), lambda i, j: (0, i, 0)),
                  pl.BlockSpec((k, tn), lambda i, j: (0, j)),
                  pl.BlockSpec((k, tn), lambda i, j: (0, j + nj))],
        out_specs=pl.BlockSpec((tm, tn), lambda i, j: (i, j)),
        scratch_shapes=[pltpu.VMEM((tm, k), BF16)],
        compiler_params=_cparams(("parallel", "arbitrary")),
        name="s5_glu",
    )(y_blk, w, w)


def _rw_merge_kernel(yf_ref, yb_ref, bonus_ref, g_ref, lnw_ref, lnb_ref, seg_ref, segt_ref,
                     ga_ref, gb_ref, s5_ref, w_ref, o_ref, h_scr):
    @pl.when(pl.program_id(1) == 0)
    def _():
        seg = seg_ref[...]
        segt = segt_ref[...]
        inv_n = 1.0 / RW_HEAD

        def head_mean(t):
            return _dot_hilo(_dot_hilo(t, seg), segt) * inv_n

        y = yf_ref[...] + yb_ref[...]
        dy = y - head_mean(y)
        var = head_mean(dy * dy)
        y = dy * lax.rsqrt(var + GN_EPS) * lnw_ref[...] + lnb_ref[...] + bonus_ref[...].astype(F32)
        h_scr[...] = (y * g_ref[...].astype(F32)).astype(BF16)

    rw_out = _dot(h_scr[...], w_ref[...])
    merged = ga_ref[...].astype(F32) * s5_ref[...].astype(F32) + gb_ref[...].astype(F32) * rw_out
    o_ref[...] = merged.astype(o_ref.dtype)


def _rw_merge(y_f, y_b, bonus, g, ln_w, ln_b, seg, segt, gates, s5_out, w_proj, *, tm, tn):
    m, rw = y_f.shape
    n = w_proj.shape[1]
    nj = n // tn
    full = lambda shape: pl.BlockSpec(shape, lambda i, j: (0,) * len(shape))
    return pl.pallas_call(
        _rw_merge_kernel,
        out_shape=jax.ShapeDtypeStruct((m, n), BF16),
        grid=(m // tm, nj),
        in_specs=[pl.BlockSpec((tm, rw), lambda i, j: (i, 0)),
                  pl.BlockSpec((tm, rw), lambda i, j: (i, 0)),
                  pl.BlockSpec((tm, rw), lambda i, j: (i, 0)),
                  pl.BlockSpec((tm, rw), lambda i, j: (i, 0)),
                  full((1, rw)), full((1, rw)), full(seg.shape), full(segt.shape),
                  pl.BlockSpec((tm, tn), lambda i, j: (i, j)),
                  pl.BlockSpec((tm, tn), lambda i, j: (i, j + nj)),
                  pl.BlockSpec((tm, tn), lambda i, j: (i, j)),
                  pl.BlockSpec((rw, tn), lambda i, j: (0, j))],
        out_specs=pl.BlockSpec((tm, tn), lambda i, j: (i, j)),
        scratch_shapes=[pltpu.VMEM((tm, rw), BF16)],
        compiler_params=_cparams(("parallel", "arbitrary")),
        name="rwkv_merge",
    )(y_f, y_b, bonus, g, ln_w.reshape(1, rw), ln_b.reshape(1, rw), seg, segt,
      gates, gates, s5_out, w_proj)


def _resid_mm_kernel(a_ref, w_ref, x_ref, g_ref, o_ref):
    o_ref[...] = x_ref[...] + g_ref[0] * _dot(a_ref[...], w_ref[...])


def _resid_matmul(a, w, x2, g_tab, mod_row_of_block, *, tm, tn):
    m, k = a.shape
    n = w.shape[1]
    return pl.pallas_call(
        _resid_mm_kernel,
        out_shape=jax.ShapeDtypeStruct((m, n), F32),
        grid=(m // tm, n // tn),
        in_specs=[pl.BlockSpec((tm, k), lambda i, j: (i, 0)),
                  pl.BlockSpec((k, tn), lambda i, j: (0, j)),
                  pl.BlockSpec((tm, tn), lambda i, j: (i, j)),
                  pl.BlockSpec((1, 1, tn), lambda i, j: (mod_row_of_block(i), 0, j))],
        out_specs=pl.BlockSpec((tm, tn), lambda i, j: (i, j)),
        compiler_params=_cparams(("parallel", "arbitrary")),
        name="out_proj",
    )(a, w, x2, g_tab)


def _ffn_down_kernel(a_ref, w_ref, x_ref, g_ref, nf_ref, o_ref, acc_ref):
    kk = pl.program_id(1)

    @pl.when(kk == 0)
    def _():
        acc_ref[...] = jnp.zeros_like(acc_ref)

    acc_ref[...] += _dot(a_ref[...], w_ref[...])

    @pl.when(kk == pl.num_programs(1) - 1)
    def _():
        h = x_ref[...] + g_ref[0] * acc_ref[...]
        ms = jnp.mean(h * h, axis=-1, keepdims=True)
        o_ref[...] = h * lax.rsqrt(ms + NORM_EPS) * nf_ref[...]


def _ffn_down(a, w, x2, g_tab, mod_row_of_block, norm_f, *, tm, tk):
    m, k = a.shape
    n = w.shape[1]
    return pl.pallas_call(
        _ffn_down_kernel,
        out_shape=jax.ShapeDtypeStruct((m, n), F32),
        grid=(m // tm, k // tk),
        in_specs=[pl.BlockSpec((tm, tk), lambda i, kk: (i, kk)),
                  pl.BlockSpec((tk, n), lambda i, kk: (kk, 0)),
                  pl.BlockSpec((tm, n), lambda i, kk: (i, 0)),
                  pl.BlockSpec((1, 1, n), lambda i, kk: (mod_row_of_block(i), 0, 0)),
                  pl.BlockSpec((1, n), lambda i, kk: (0, 0))],
        out_specs=pl.BlockSpec((tm, n), lambda i, kk: (i, 0)),
        scratch_shapes=[pltpu.VMEM((tm, n), F32)],
        compiler_params=_cparams(("parallel", "arbitrary")),
        name="ffn_down",
    )(a, w, x2, g_tab, norm_f.reshape(1, n))


def kernel(x, c, ctx, c_ctx, ada_w, ada_b, norm1_w, w_in, rw_mu, s5_a_re, s5_a_im, s5_log_dt, s5_b_re, s5_b_im, s5_c_re, s5_c_im, s5_d, s5_glu_w, rw_w0, rw_w2, rw_a0, rw_a2, rw_g2, rw_k_k, rw_k_a, rw_r_k, rw_ln_w, rw_ln_b, rw_proj, w_o, norm2_w, ffn_w13, ffn_w2, norm_f):
    assert ada_w.shape[0] == 1, "single-layer block"
    bsz, l_lat, d = x.shape
    l_ctx = ctx.shape[1]
    l_all = l_ctx + l_lat
    s5w = s5_d.shape[1] * s5_d.shape[2]
    rw = rw_g2.shape[2]
    shift_cols = rw_mu.shape[1]
    d_ff = ffn_w2.shape[1]
    n_heads = rw // RW_HEAD

    c_rows = jnp.concatenate([c, c_ctx[None], jnp.zeros((8 - bsz - 1, d), F32)], axis=0)
    mod = _modulation(c_rows, ada_w[0], ada_b[0])
    tab = lambda k: mod[:, k * d:(k + 1) * d].reshape(8, 1, d)
    sh1, sc1, g1, sh2, sc2, g2 = (tab(k) for k in range(N_MOD))
    ctx_row = bsz

    tm_in = 512
    m_lat = bsz * l_lat
    x2 = x.reshape(m_lat, d)
    x_all = jnp.concatenate([x2, ctx.reshape(bsz * l_ctx, d)], axis=0)
    w_in_b = w_in[0].astype(BF16)
    n_mix = s5w + shift_cols
    lat_row = lambda t: (lambda i: i // (l_lat // t))

    def mix_mod_row(i):
        return jnp.where(i >= m_lat // tm_in, ctx_row, i // (l_lat // tm_in))

    u_blk = _lnmod_matmul(x_all, norm1_w[0], sh1, sc1, mix_mod_row, w_in_b[:, :s5w],
                          tm=tm_in, tn=s5w, out_dtype=F32, epilogue="lane_blocks", name="in_proj_s5")
    z_rw = _lnmod_matmul(x_all, norm1_w[0], sh1, sc1, mix_mod_row, w_in_b[:, s5w:n_mix],
                         tm=tm_in, tn=shift_cols // 3, out_dtype=BF16, name="in_proj_rw")
    tm = 1024
    gates = _lnmod_matmul(x2, norm1_w[0], sh1, sc1, lat_row(tm), w_in_b[:, n_mix:],
                          tm=tm, tn=1024, out_dtype=BF16, epilogue="sigmoid", name="in_proj_gates")

    y_blk = _s5_branch_blocked(u_blk, bsz, l_ctx, l_lat, s5_a_re[0], s5_a_im[0], s5_log_dt[0],
                               s5_b_re[0], s5_b_im[0], s5_c_re[0], s5_c_im[0], s5_d[0])
    s5_out = _s5_glu(y_blk, s5_glu_w[0].astype(BF16), tm=tm, tn=1024)

    lora = rw_w2.shape[2]
    zl = jnp.zeros((lora, rw), F32)
    w2bd = jnp.concatenate([jnp.concatenate([rw_w2[0, 0], zl], axis=1),
                            jnp.concatenate([zl, rw_w2[0, 1]], axis=1)], axis=0)
    a2bd = jnp.concatenate([jnp.concatenate([rw_a2[0, 0], zl], axis=1),
                            jnp.concatenate([zl, rw_a2[0, 1]], axis=1)], axis=0)
    head_of = jnp.arange(rw) // RW_HEAD
    seg = (head_of[:, None] == jnp.arange(LANES)[None, :]).astype(BF16)
    segt = seg.T
    r, v, kk, g, bonus, lw, kd, be = _rw_prep(
        z_rw, bsz, l_ctx, l_lat, rw_mu[0], w2bd, a2bd, rw_g2[0], rw_w0[0].reshape(1, 2 * rw),
        rw_a0[0].reshape(1, 2 * rw), rw_k_k[0], rw_k_a[0], rw_r_k[0].reshape(rw), seg, segt)
    y_f, y_b = _rw_scan2(r, v, kk, lw, kd, be, l_ctx)

    merged = _rw_merge(y_f.reshape(m_lat, rw), y_b.reshape(m_lat, rw), bonus.reshape(m_lat, rw), g.reshape(m_lat, rw),
                       rw_ln_w[0], rw_ln_b[0], seg, segt, gates, s5_out, rw_proj[0].astype(BF16),
                       tm=512, tn=1024)
    h1 = _resid_matmul(merged, w_o[0].astype(BF16), x2, g1, lat_row(tm), tm=tm, tn=1024)

    act = _lnmod_swiglu(h1, norm2_w[0], sh2, sc2, lat_row(tm), ffn_w13[0].astype(BF16), d_ff,
                        tm=tm, tn=512, name="ffn_up")
    tm_dn = 512
    out = _ffn_down(act, ffn_w2[0].astype(BF16), h1, g2, lat_row(tm_dn), norm_f, tm=tm_dn, tk=d_ff // 4)
    return out.reshape(bsz, l_lat, d)
```

```python
import functools
import math

import jax
import jax.numpy as jnp
from jax import lax
from jax.experimental import pallas as pl
from jax.experimental.pallas import tpu as pltpu

F32 = jnp.float32
BF16 = jnp.bfloat16
HIGHEST = lax.Precision.HIGHEST

D_MODEL = 2048
N_MOD = 6
NORM_EPS = 1e-6
GN_EPS = 64e-5
GRID_W = 64
S5_GROUP = 16
S5_STATE = 64
S5_CHUNK = 16
RW_HEAD = 64
RW_CHUNK = 64
RW_SUB = 2
LANES = 128
VMEM_LIMIT = 48 * 1024 * 1024


def _cparams(sem):
    return pltpu.CompilerParams(dimension_semantics=sem, vmem_limit_bytes=VMEM_LIMIT)


def _operands(a, b, precision):
    if precision == "bf16":
        return a.astype(BF16), b.astype(BF16), None
    return a, b, precision


def _dot(a, b, precision=None):
    a, b, precision = _operands(a, b, precision)
    return jnp.dot(a, b, preferred_element_type=F32, precision=precision)


def _dot_nt(a, b, precision=None):
    a, b, precision = _operands(a, b, precision)
    return lax.dot_general(a, b, (((1,), (1,)), ((), ())), preferred_element_type=F32, precision=precision)


def _dot_hilo(a, ind):
    hi = a.astype(BF16)
    lo = (a - hi.astype(F32)).astype(BF16)
    return _dot(hi, ind) + _dot(lo, ind)


def _sigmoid(x):
    return 1.0 / (1.0 + jnp.exp(-x))


def _silu(x):
    return x * _sigmoid(x)


def _gelu_tanh(x):
    c = math.sqrt(2.0 / math.pi)
    return 0.5 * x * (1.0 + jnp.tanh(c * (x + 0.044715 * (x * x * x))))


def _softplus(x):
    return jnp.maximum(x, 0.0) + jnp.log(1.0 + jnp.exp(-jnp.abs(x)))


def _mod_kernel(c_ref, w_ref, b_ref, o_ref):
    o_ref[...] = _dot(_silu(c_ref[...]), w_ref[...], HIGHEST) + b_ref[...]


def _modulation(c_rows, ada_w, ada_b):
    m, d = c_rows.shape
    n = ada_w.shape[1]
    tn = 1024
    return pl.pallas_call(
        _mod_kernel,
        out_shape=jax.ShapeDtypeStruct((m, n), F32),
        grid=(n // tn,),
        in_specs=[pl.BlockSpec((m, d), lambda j: (0, 0)),
                  pl.BlockSpec((d, tn), lambda j: (0, j)),
                  pl.BlockSpec((1, tn), lambda j: (0, j))],
        out_specs=pl.BlockSpec((m, tn), lambda j: (0, j)),
        compiler_params=_cparams(("arbitrary",)),
        name="modulation",
    )(c_rows, ada_w, ada_b.reshape(1, n))


def _lnmod_rows(x, nw, sh, sc):
    ms = jnp.mean(x * x, axis=-1, keepdims=True)
    y = x * lax.rsqrt(ms + NORM_EPS) * nw
    return y * (1.0 + sc) + sh


def _lnmod_mm_kernel(x_ref, nw_ref, sh_ref, sc_ref, w_ref, o_ref, h_scr, *, epilogue):
    @pl.when(pl.program_id(1) == 0)
    def _():
        h_scr[...] = _lnmod_rows(x_ref[...], nw_ref[...], sh_ref[0], sc_ref[0]).astype(BF16)

    z = _dot(h_scr[...], w_ref[...])
    if epilogue == "sigmoid":
        z = _sigmoid(z)
    if epilogue == "lane_blocks":
        for jb in range(o_ref.shape[0]):
            o_ref[jb] = z[:, jb * LANES:(jb + 1) * LANES].astype(o_ref.dtype)
    else:
        o_ref[...] = z.astype(o_ref.dtype)


def _lnmod_swiglu_kernel(x_ref, nw_ref, sh_ref, sc_ref, w1_ref, w3_ref, o_ref, h_scr):
    @pl.when(pl.program_id(1) == 0)
    def _():
        h_scr[...] = _lnmod_rows(x_ref[...], nw_ref[...], sh_ref[0], sc_ref[0]).astype(BF16)

    h = h_scr[...]
    o_ref[...] = (_silu(_dot(h, w1_ref[...])) * _dot(h, w3_ref[...])).astype(o_ref.dtype)


def _lnmod_matmul(x2, nw, sh_tab, sc_tab, mod_row_of_block, w, *, tm, tn, out_dtype, epilogue=None, name):
    m, d = x2.shape
    n = w.shape[1]
    mod_map = lambda i, j: (mod_row_of_block(i), 0, 0)
    if epilogue == "lane_blocks":
        assert tn == n
        out_shape = jax.ShapeDtypeStruct((n // LANES, m, LANES), out_dtype)
        out_spec = pl.BlockSpec((n // LANES, tm, LANES), lambda i, j: (0, i, 0))
    else:
        out_shape = jax.ShapeDtypeStruct((m, n), out_dtype)
        out_spec = pl.BlockSpec((tm, tn), lambda i, j: (i, j))
    return pl.pallas_call(
        functools.partial(_lnmod_mm_kernel, epilogue=epilogue),
        out_shape=out_shape,
        grid=(m // tm, n // tn),
        in_specs=[pl.BlockSpec((tm, d), lambda i, j: (i, 0)),
                  pl.BlockSpec((1, d), lambda i, j: (0, 0)),
                  pl.BlockSpec((1, 1, d), mod_map),
                  pl.BlockSpec((1, 1, d), mod_map),
                  pl.BlockSpec((d, tn), lambda i, j: (0, j))],
        out_specs=out_spec,
        scratch_shapes=[pltpu.VMEM((tm, d), BF16)],
        compiler_params=_cparams(("parallel", "arbitrary")),
        name=name,
    )(x2, nw.reshape(1, d), sh_tab, sc_tab, w)


def _lnmod_swiglu(x2, nw, sh_tab, sc_tab, mod_row_of_block, w13, d_ff, *, tm, tn, name):
    m, d = x2.shape
    nj = d_ff // tn
    mod_map = lambda i, j: (mod_row_of_block(i), 0, 0)
    return pl.pallas_call(
        _lnmod_swiglu_kernel,
        out_shape=jax.ShapeDtypeStruct((m, d_ff), BF16),
        grid=(m // tm, nj),
        in_specs=[pl.BlockSpec((tm, d), lambda i, j: (i, 0)),
                  pl.BlockSpec((1, d), lambda i, j: (0, 0)),
                  pl.BlockSpec((1, 1, d), mod_map),
                  pl.BlockSpec((1, 1, d), mod_map),
                  pl.BlockSpec((d, tn), lambda i, j: (0, j)),
                  pl.BlockSpec((d, tn), lambda i, j: (0, j + nj))],
        out_specs=pl.BlockSpec((tm, tn), lambda i, j: (i, j)),
        scratch_shapes=[pltpu.VMEM((tm, d), BF16)],
        compiler_params=_cparams(("parallel", "arbitrary")),
        name=name,
    )(x2, nw.reshape(1, d), sh_tab, sc_tab, w13, w13)


def _s5_param_kernel(are_ref, aim_ref, ldt_ref, bre_ref, bim_ref, cre_ref, cim_ref,
                     e_ref, cs_ref, kt_ref, a16_ref):
    t_n, hg, p_n = S5_CHUNK, S5_GROUP, S5_STATE
    tau = lax.broadcasted_iota(jnp.int32, (t_n, 1, p_n), 0).astype(F32)
    for d in range(2):
        a_re = are_ref[0, d:d + 1, :]
        a_im = aim_ref[0, d:d + 1, :]
        dt = jnp.exp(ldt_ref[0, d:d + 1, :])
        lam = a_re * dt
        th = a_im * dt
        er = jnp.exp(lam)
        ab_re = er * jnp.cos(th)
        ab_im = er * jnp.sin(th)
        den = a_re * a_re + a_im * a_im
        x_re = ab_re - 1.0
        co_re = (x_re * a_re + ab_im * a_im) / den
        co_im = (ab_im * a_re - x_re * a_im) / den
        bt_re = bre_ref[0, d]
        bt_im = bim_ref[0, d]
        bb_re = co_re * bt_re - co_im * bt_im
        bb_im = co_re * bt_im + co_im * bt_re
        c_re = cre_ref[0, d]
        c_im = cim_ref[0, d]

        def power(tv):
            mag = jnp.exp(tv * lam)
            return mag * jnp.cos(tv * th), mag * jnp.sin(tv * th)

        pw_re, pw_im = power(tau)
        cp_re = (c_re[None] * pw_re - c_im[None] * pw_im).reshape(t_n * hg, p_n)
        cp_im = (c_re[None] * pw_im + c_im[None] * pw_re).reshape(t_n * hg, p_n)
        kt_ref[0, d] = _dot_nt(bb_re, cp_re, HIGHEST) - _dot_nt(bb_im, cp_im, HIGHEST)

        te = (t_n - 1.0 - tau) if d == 0 else tau
        pe_re, pe_im = power(te)
        e_ref[0, d, :, 0:p_n] = (pe_re * bb_re[None] - pe_im * bb_im[None]).reshape(t_n * hg, p_n)
        e_ref[0, d, :, p_n:2 * p_n] = (pe_re * bb_im[None] + pe_im * bb_re[None]).reshape(t_n * hg, p_n)

        tc = (tau + 1.0) if d == 0 else (t_n - tau)
        pc_re, pc_im = power(tc)
        cs_ref[0, d, :, 0:p_n] = (c_re[None] * pc_re - c_im[None] * pc_im).reshape(t_n * hg, p_n)
        cs_ref[0, d, :, p_n:2 * p_n] = -(c_re[None] * pc_im + c_im[None] * pc_re).reshape(t_n * hg, p_n)

        mag16 = jnp.exp(float(t_n) * lam)
        a16_ref[0, d, 0:1, :] = mag16 * jnp.cos(float(t_n) * th)
        a16_ref[0, d, 1:2, :] = mag16 * jnp.sin(float(t_n) * th)


def _s5_params(a_re, a_im, log_dt, b_re, b_im, c_re, c_im):
    g_n = a_re.shape[1]
    p_n, hg, t_n = S5_STATE, S5_GROUP, S5_CHUNK
    tr = lambda a: jnp.swapaxes(a, 0, 1)
    ldt = jnp.broadcast_to(tr(log_dt)[:, :, None], (g_n, 2, p_n))
    spec3 = pl.BlockSpec((1, 2, p_n), lambda g: (g, 0, 0))
    spec4 = pl.BlockSpec((1, 2, hg, p_n), lambda g: (g, 0, 0, 0))
    th = t_n * hg
    return pl.pallas_call(
        _s5_param_kernel,
        out_shape=(jax.ShapeDtypeStruct((g_n, 2, th, 2 * p_n), F32),
                   jax.ShapeDtypeStruct((g_n, 2, th, 2 * p_n), F32),
                   jax.ShapeDtypeStruct((g_n, 2, hg, th), F32),
                   jax.ShapeDtypeStruct((g_n, 2, 2, p_n), F32)),
        grid=(g_n,),
        in_specs=[spec3, spec3, spec3, spec4, spec4, spec4, spec4],
        out_specs=(pl.BlockSpec((1, 2, th, 2 * p_n), lambda g: (g, 0, 0, 0)),
                   pl.BlockSpec((1, 2, th, 2 * p_n), lambda g: (g, 0, 0, 0)),
                   pl.BlockSpec((1, 2, hg, th), lambda g: (g, 0, 0, 0)),
                   pl.BlockSpec((1, 2, 2, p_n), lambda g: (g, 0, 0, 0))),
        compiler_params=_cparams(("parallel",)),
        name="s5_params",
    )(tr(a_re), tr(a_im), ldt,
      jnp.transpose(b_re, (1, 0, 3, 2)), jnp.transpose(b_im, (1, 0, 3, 2)), tr(c_re), tr(c_im))


def _s5_state_in_kernel(u_ref, e_ref, o_ref):
    e = e_ref[0]
    u = u_ref[0]
    o_ref[0, :, 0:LANES] = _dot(u, e[0], HIGHEST)
    o_ref[0, :, LANES:2 * LANES] = _dot(u, e[1], HIGHEST)


def _s5_state_inputs(u_g, e):
    g_n, rows, th = u_g.shape
    return pl.pallas_call(
        _s5_state_in_kernel,
        out_shape=jax.ShapeDtypeStruct((g_n, rows, 2 * LANES), F32),
        grid=(g_n,),
        in_specs=[pl.BlockSpec((1, rows, th), lambda g: (g, 0, 0)),
                  pl.BlockSpec((1, 2, th, LANES), lambda g: (g, 0, 0, 0))],
        out_specs=pl.BlockSpec((1, rows, 2 * LANES), lambda g: (g, 0, 0)),
        compiler_params=_cparams(("parallel",)),
        name="s5_state_inputs",
    )(u_g, e)


def _s5_scan_kernel(e_ref, a_ref, o_ref, *, n_ctx, n_all):
    zero = jnp.zeros(e_ref.shape[2:], F32)

    def step(pr, pi):
        ar, ai = a_ref[pr], a_ref[pi]

        def body(c, carry):
            sr, si = carry
            o_ref[pr, c] = sr
            o_ref[pi, c] = si
            return (ar * sr - ai * si + e_ref[pr, c], ar * si + ai * sr + e_ref[pi, c])
        return body

    lax.fori_loop(0, n_all, step(0, 1), (zero, zero))
    bwd = step(2, 3)
    carry = lax.fori_loop(0, n_ctx, lambda k, cy: bwd(n_ctx - 1 - k, cy), (zero, zero))
    lax.fori_loop(0, n_all - n_ctx, lambda k, cy: bwd(n_all - 1 - k, cy), carry)


def _s5_scan(e_planes, a_planes, n_ctx, n_all):
    _, rows, r_n, _ = e_planes.shape
    bsz = rows // n_all
    sub = 8
    return pl.pallas_call(
        functools.partial(_s5_scan_kernel, n_ctx=n_ctx, n_all=n_all),
        out_shape=jax.ShapeDtypeStruct(e_planes.shape, F32),
        grid=(bsz, r_n // sub),
        in_specs=[pl.BlockSpec((4, n_all, sub, LANES), lambda b, q: (0, b, q, 0)),
                  pl.BlockSpec((4, sub, LANES), lambda b, q: (0, q, 0))],
        out_specs=pl.BlockSpec((4, n_all, sub, LANES), lambda b, q: (0, b, q, 0)),
        compiler_params=_cparams(("parallel", "parallel")),
        name="s5_scan",
    )(e_planes, a_planes)


def _s5_apply_kernel(u_ref, s_ref, mf_ref, mb_ref, d_ref, cs_ref, o_ref, *, n_ctx):
    u = u_ref[0, n_ctx:, :]
    s = s_ref[0, n_ctx:, :]
    y = _dot(u, mf_ref[0] + mb_ref[0], HIGHEST) + u * d_ref[0]
    y = y + _dot_nt(s[:, 0:LANES], cs_ref[0, 0], HIGHEST) + _dot_nt(s[:, LANES:2 * LANES], cs_ref[0, 1], HIGHEST)
    o_ref[0, 0] = y


def _s5_apply(u_g, s_g, m_f, m_b, d_t, cs, n_ctx, n_all):
    g_n, rows, th = u_g.shape
    bsz = rows // n_all
    n_lat = n_all - n_ctx
    return pl.pallas_call(
        functools.partial(_s5_apply_kernel, n_ctx=n_ctx),
        out_shape=jax.ShapeDtypeStruct((g_n, bsz, n_lat, th), F32),
        grid=(g_n, bsz),
        in_specs=[pl.BlockSpec((1, n_all, th), lambda g, b: (g, b, 0)),
                  pl.BlockSpec((1, n_all, 2 * LANES), lambda g, b: (g, b, 0)),
                  pl.BlockSpec((1, th, th), lambda g, b: (g, 0, 0)),
                  pl.BlockSpec((1, th, th), lambda g, b: (g, 0, 0)),
                  pl.BlockSpec((1, 1, th), lambda g, b: (g, 0, 0)),
                  pl.BlockSpec((1, 2, th, LANES), lambda g, b: (g, 0, 0, 0))],
        out_specs=pl.BlockSpec((1, 1, n_lat, th), lambda g, b: (g, b, 0, 0)),
        compiler_params=_cparams(("parallel", "arbitrary")),
        name="s5_apply",
    )(u_g, s_g, m_f, m_b, d_t, cs)


def _s5_branch(u_all, n_ctx_tok, a_re, a_im, log_dt, b_re, b_im, c_re, c_im, s5_d):
    bsz, l_all, width = u_all.shape
    hg, t_n, p_n = S5_GROUP, S5_CHUNK, S5_STATE
    g_n = width // hg
    n_all = l_all // t_n
    n_ctx = n_ctx_tok // t_n
    th = t_n * hg
    e, cs, kt, a16 = _s5_params(a_re, a_im, log_dt, b_re, b_im, c_re, c_im)

    kt5 = kt.reshape(g_n, 2, hg, t_n, hg)
    ii = jnp.arange(t_n)[:, None]
    jj = jnp.arange(t_n)[None, :]

    def toeplitz(k4, lag, keep):
        m = k4[:, :, jnp.clip(lag, 0, t_n - 1), :]
        m = jnp.where(keep[None, None, :, :, None], m, 0.0)
        return jnp.transpose(m, (0, 2, 1, 3, 4)).reshape(g_n, th, th)

    m_f = toeplitz(kt5[:, 0], jj - ii, jj >= ii)
    m_b = toeplitz(kt5[:, 1], ii - jj, ii >= jj)
    d_t = jnp.tile(s5_d, (1, t_n)).reshape(g_n, 1, th)

    u_g = jnp.transpose(u_all.reshape(bsz, n_all, t_n, g_n, hg), (3, 0, 1, 2, 4)).reshape(g_n, bsz * n_all, th)
    e_cat = e
    s_in = _s5_state_inputs(u_g, e_cat)
    rows = bsz * n_all
    planes = jnp.transpose(s_in.reshape(g_n, rows, 4, p_n), (2, 1, 0, 3)).reshape(4, rows, g_n * p_n // LANES, LANES)
    a_pl = jnp.transpose(a16.reshape(g_n, 4, p_n), (1, 0, 2)).reshape(4, g_n * p_n // LANES, LANES)
    st = _s5_scan(planes, a_pl, n_ctx, n_all)
    s_g = jnp.transpose(st.reshape(4, rows, g_n, p_n), (2, 1, 0, 3)).reshape(g_n, rows, 4 * p_n)
    y_g = _s5_apply(u_g, s_g, m_f, m_b, d_t, cs, n_ctx, n_all)
    n_lat = n_all - n_ctx
    y = jnp.transpose(y_g.reshape(g_n, bsz, n_lat, t_n, hg), (1, 2, 3, 0, 4))
    return y.reshape(bsz, n_lat * t_n, width)


def _s5_param2_kernel(are_ref, aim_ref, ldt_ref, bre_ref, bim_ref, cre_ref, cim_ref,
                      e_ref, c_ref, m_ref, a16_ref):
    t_n, hg, p_n = S5_CHUNK, S5_GROUP, S5_STATE
    tau = lax.broadcasted_iota(jnp.int32, (t_n, 1, p_n), 0).astype(F32)
    taps = []
    for d in range(2):
        a_re = are_ref[0, d:d + 1, :]
        a_im = aim_ref[0, d:d + 1, :]
        dt = jnp.exp(ldt_ref[0, d:d + 1, :])
        lam = a_re * dt
        th = a_im * dt
        er = jnp.exp(lam)
        ab_re = er * jnp.cos(th)
        ab_im = er * jnp.sin(th)
        den = a_re * a_re + a_im * a_im
        x_re = ab_re - 1.0
        co_re = (x_re * a_re + ab_im * a_im) / den
        co_im = (ab_im * a_re - x_re * a_im) / den
        bt_re = bre_ref[0, d]
        bt_im = bim_ref[0, d]
        bb_re = co_re * bt_re - co_im * bt_im
        bb_im = co_re * bt_im + co_im * bt_re
        c_re = cre_ref[0, d]
        c_im = cim_ref[0, d]

        def power(tv):
            mag = jnp.exp(tv * lam)
            return mag * jnp.cos(tv * th), mag * jnp.sin(tv * th)

        pw_re, pw_im = power(tau if d == 0 else (t_n - 1.0 - tau))
        cp_re = (c_re[None] * pw_re - c_im[None] * pw_im).reshape(t_n * hg, p_n)
        cp_im = (c_re[None] * pw_im + c_im[None] * pw_re).reshape(t_n * hg, p_n)
        taps.append(_dot_nt(bb_re, cp_re, HIGHEST) - _dot_nt(bb_im, cp_im, HIGHEST))

        lo, hi = 2 * d * p_n, (2 * d + 1) * p_n
        pe_re, pe_im = power((t_n - 1.0 - tau) if d == 0 else tau)
        e_ref[0, :, 0, :, lo:hi] = pe_re * bb_re[None] - pe_im * bb_im[None]
        e_ref[0, :, 0, :, hi:hi + p_n] = pe_re * bb_im[None] + pe_im * bb_re[None]
        pc_re, pc_im = power((tau + 1.0) if d == 0 else (t_n - tau))
        c_ref[0, :, 0, :, lo:hi] = c_re[None] * pc_re - c_im[None] * pc_im
        c_ref[0, :, 0, :, hi:hi + p_n] = -(c_re[None] * pc_im + c_im[None] * pc_re)

        mag16 = jnp.exp(float(t_n) * lam)
        a16_ref[0, d, 0:1, :] = mag16 * jnp.cos(float(t_n) * th)
        a16_ref[0, d, 1:2, :] = mag16 * jnp.sin(float(t_n) * th)

    width = t_n * hg
    lane = lax.broadcasted_iota(jnp.int32, (hg, width), 1)
    for t in range(t_n):
        sf = t * hg
        sb = (t_n - 1 - t) * hg
        f = taps[0] if sf == 0 else jnp.where(lane >= sf, pltpu.roll(taps[0], sf, 1), 0.0)
        b = taps[1] if sb == 0 else jnp.where(lane < width - sb, pltpu.roll(taps[1], width - sb, 1), 0.0)
        m_ref[0, t, 0] = f + b


def _s5_params2(a_re, a_im, log_dt, b_re, b_im, c_re, c_im, nb):
    g_n = a_re.shape[1]
    gl_n = g_n // nb
    p_n, hg, t_n = S5_STATE, S5_GROUP, S5_CHUNK
    tr = lambda a: jnp.swapaxes(a, 0, 1)
    ldt = jnp.broadcast_to(tr(log_dt)[:, :, None], (g_n, 2, p_n))
    spec3 = pl.BlockSpec((1, 2, p_n), lambda g: (g, 0, 0))
    spec4 = pl.BlockSpec((1, 2, hg, p_n), lambda g: (g, 0, 0, 0))
    wide = 4 * p_n
    comp = jax.ShapeDtypeStruct((nb, t_n, gl_n, hg, wide), F32)
    comp_spec = pl.BlockSpec((1, t_n, 1, hg, wide), lambda g: (g // gl_n, 0, g % gl_n, 0, 0))
    return pl.pallas_call(
        _s5_param2_kernel,
        out_shape=(comp, comp, comp, jax.ShapeDtypeStruct((g_n, 2, 2, p_n), F32)),
        grid=(g_n,),
        in_specs=[spec3, spec3, spec3, spec4, spec4, spec4, spec4],
        out_specs=(comp_spec, comp_spec, comp_spec, pl.BlockSpec((1, 2, 2, p_n), lambda g: (g, 0, 0, 0))),
        compiler_params=_cparams(("parallel",)),
        name="s5_params",
    )(tr(a_re), tr(a_im), ldt,
      jnp.transpose(b_re, (1, 0, 3, 2)), jnp.transpose(b_im, (1, 0, 3, 2)), tr(c_re), tr(c_im))


def _expand_block_diag(comp, rep_ref, mask_ref, w_scr):
    k = w_scr.shape[0]
    period = mask_ref.shape[1]
    cb = comp.astype(BF16)
    step = 512
    for c0 in range(0, k, step):
        blk = _dot(cb, rep_ref[:, c0:c0 + step]).astype(BF16)
        for q0 in range(0, step, period):
            w_scr[:, c0 + q0:c0 + q0 + period] = blk[:, q0:q0 + period] * mask_ref[...]


def _s5_ein_kernel(u_ref, ec_ref, rep_ref, mask_ref, o_ref, w_scr):
    _expand_block_diag(ec_ref[0], rep_ref, mask_ref, w_scr)
    o_ref[0] = _dot(u_ref[0].astype(BF16), w_scr[...])


def _s5_chunk_inputs(u_rows, e_comp, rep_e, mask_e):
    nb, rows, k = u_rows.shape
    cw = e_comp.shape[2]
    return pl.pallas_call(
        _s5_ein_kernel,
        out_shape=jax.ShapeDtypeStruct((nb, rows, k), F32),
        grid=(nb,),
        in_specs=[pl.BlockSpec((1, rows, k), lambda j: (j, 0, 0)),
                  pl.BlockSpec((1, k, cw), lambda j: (j, 0, 0)),
                  pl.BlockSpec(rep_e.shape, lambda j: (0, 0)),
                  pl.BlockSpec(mask_e.shape, lambda j: (0, 0))],
        out_specs=pl.BlockSpec((1, rows, k), lambda j: (j, 0, 0)),
        scratch_shapes=[pltpu.VMEM((k, k), BF16)],
        compiler_params=_cparams(("parallel",)),
        name="s5_chunk_inputs",
    )(u_rows, e_comp, rep_e, mask_e)


def _s5_bscan_kernel(e_ref, a_ref, o_ref, *, bsz, n_ctx, n_lat):
    q = e_ref.shape[2] // 4
    planes = lambda row, d: (row[:, (2 * d) * q:(2 * d + 1) * q], row[:, (2 * d + 1) * q:(2 * d + 2) * q])
    coef = [planes(a_ref[0], d) for d in range(2)]
    ctx0 = bsz * n_lat

    def advance(state, rows):
        new = []
        for (sr, si), (b, d), row in zip(state, [(b, d) for b in range(bsz) for d in range(2)], rows):
            ar, ai = coef[d]
            er, ei = planes(e_ref[0, pl.ds(row, 1), :], d)
            new.append((ar * sr - ai * si + er, ar * si + ai * sr + ei))
        return tuple(new)

    def ctx_step(s, state):
        rows = [ctx0 + b * n_ctx + (s if d == 0 else n_ctx - 1 - s) for b in range(bsz) for d in range(2)]
        return advance(state, rows)

    def lat_step(s, state):
        rows = [b * n_lat + (s if d == 0 else n_lat - 1 - s) for b in range(bsz) for d in range(2)]
        for (sr, si), (b, d), row in zip(state, [(b, d) for b in range(bsz) for d in range(2)], rows):
            o_ref[0, pl.ds(row, 1), (2 * d) * q:(2 * d + 1) * q] = sr
            o_ref[0, pl.ds(row, 1), (2 * d + 1) * q:(2 * d + 2) * q] = si
        return advance(state, rows)

    zero = jnp.zeros((1, q), F32)
    state = tuple((zero, zero) for _ in range(2 * bsz))
    state = lax.fori_loop(0, n_ctx, ctx_step, state)
    lax.fori_loop(0, n_lat, lat_step, state)


def _s5_bscan(e_rows, a_rows, bsz, n_ctx, n_lat):
    nb, rows, k = e_rows.shape
    return pl.pallas_call(
        functools.partial(_s5_bscan_kernel, bsz=bsz, n_ctx=n_ctx, n_lat=n_lat),
        out_shape=jax.ShapeDtypeStruct((nb, bsz * n_lat, k), F32),
        grid=(nb,),
        in_specs=[pl.BlockSpec((1, rows, k), lambda j: (j, 0, 0)),
                  pl.BlockSpec((1, 1, k), lambda j: (j, 0, 0))],
        out_specs=pl.BlockSpec((1, bsz * n_lat, k), lambda j: (j, 0, 0)),
        compiler_params=_cparams(("parallel",)),
        name="s5_scan",
    )(e_rows, a_rows)


def _s5_out_kernel(u_ref, s_ref, mc_ref, cc_ref, d_ref, rep_m_ref, mask_m_ref, rep_e_ref, mask_e_ref,
                   o_ref, wm_scr, wc_scr):
    @pl.when(pl.program_id(1) == 0)
    def _():
        _expand_block_diag(mc_ref[0], rep_m_ref, mask_m_ref, wm_scr)
        _expand_block_diag(cc_ref[0], rep_e_ref, mask_e_ref, wc_scr)

    u = u_ref[0]
    y = _dot(u.astype(BF16), wm_scr[...]) + _dot_nt(s_ref[0].astype(BF16), wc_scr[...])
    o_ref[0] = y + u * d_ref[0]


def _s5_outputs(u_rows, s_rows, m_comp, c_comp, d_rows, rep_m, mask_m, rep_e, mask_e):
    nb, _, k = u_rows.shape
    rows = s_rows.shape[1]
    cw = m_comp.shape[2]
    tr = rows // 2
    const = lambda a: pl.BlockSpec(a.shape, lambda j, i: (0, 0))
    return pl.pallas_call(
        _s5_out_kernel,
        out_shape=jax.ShapeDtypeStruct((nb, rows, k), F32),
        grid=(nb, rows // tr),
        in_specs=[pl.BlockSpec((1, tr, k), lambda j, i: (j, i, 0)),
                  pl.BlockSpec((1, tr, k), lambda j, i: (j, i, 0)),
                  pl.BlockSpec((1, k, cw), lambda j, i: (j, 0, 0)),
                  pl.BlockSpec((1, k, cw), lambda j, i: (j, 0, 0)),
                  pl.BlockSpec((1, 1, k), lambda j, i: (j, 0, 0)),
                  const(rep_m), const(mask_m), const(rep_e), const(mask_e)],
        out_specs=pl.BlockSpec((1, tr, k), lambda j, i: (j, i, 0)),
        scratch_shapes=[pltpu.VMEM((k, k), BF16), pltpu.VMEM((k, k), BF16)],
        compiler_params=_cparams(("parallel", "arbitrary")),
        name="s5_outputs",
    )(u_rows, s_rows, m_comp, c_comp, d_rows, rep_m, mask_m, rep_e, mask_e)


def _s5_branch_blocked(u_blk, bsz, l_ctx, l_lat, a_re, a_im, log_dt, b_re, b_im, c_re, c_im, s5_d):
    nb, m_all, _ = u_blk.shape
    hg, t_n, p_n = S5_GROUP, S5_CHUNK, S5_STATE
    g_n = a_re.shape[1]
    gl_n = g_n // nb
    k = t_n * LANES
    n_lat = l_lat // t_n
    n_ctx = l_ctx // t_n
    e_c, c_c, m_c, a16 = _s5_params2(a_re, a_im, log_dt, b_re, b_im, c_re, c_im, nb)
    cw = 4 * p_n
    e_comp, c_comp, m_comp = (a.reshape(nb, k, cw) for a in (e_c, c_c, m_c))

    row_gl = (jnp.arange(k) // hg) % gl_n
    col = jnp.arange(k)
    src = jnp.arange(cw)
    rep_e = ((src[:, None] // p_n == col[None, :] // (gl_n * p_n)) & (src[:, None] % p_n == col[None, :] % p_n)).astype(BF16)
    mask_e = (row_gl[:, None] == (jnp.arange(gl_n * p_n)[None, :] // p_n)).astype(BF16)
    rep_m = ((src[:, None] // hg == col[None, :] // (gl_n * hg)) & (src[:, None] % hg == col[None, :] % hg)).astype(BF16)
    mask_m = (row_gl[:, None] == (jnp.arange(gl_n * hg)[None, :] // hg)).astype(BF16)

    d_rows = jnp.tile(s5_d.reshape(nb, 1, gl_n * hg), (1, 1, t_n))
    a_rows = jnp.transpose(a16.reshape(nb, gl_n, 2, 2, p_n), (0, 2, 3, 1, 4)).reshape(nb, 1, 4 * gl_n * p_n)

    u_rows = u_blk.reshape(nb, m_all // t_n, k)
    e_rows = _s5_chunk_inputs(u_rows, e_comp, rep_e, mask_e)
    s_rows = _s5_bscan(e_rows, a_rows, bsz, n_ctx, n_lat)
    y_rows = _s5_outputs(u_rows, s_rows, m_comp, c_comp, d_rows, rep_m, mask_m, rep_e, mask_e)
    return y_rows.reshape(nb, bsz * l_lat, LANES)


def _rw_prep_kernel(z_ref, zp_ref, zn_ref, mu_ref, w2_ref, a2_ref, g2_ref, w0_ref, a0_ref,
                    kk_w_ref, ka_ref, rk_ref, seg_ref, segt_ref,
                    r_ref, v_ref, kk_ref, g_ref, bonus_ref, lw_ref, kd_ref, be_ref,
                    *, tm, l_lat, rw):
    j = pl.program_id(1)
    z = z_ref[...].astype(F32)
    lat = j > 0
    tl = lax.broadcasted_iota(jnp.int32, (tm, 1), 0)
    tok = (j - 1) * tm + tl
    col = tl % GRID_W
    m_l = jnp.where(lat, col, tl) > 0
    m_r = jnp.where(lat, col - (GRID_W - 1), tl - (tm - 1)) < 0
    m_u = jnp.logical_and(lat, tok >= GRID_W)
    m_d = jnp.logical_and(lat, tok < l_lat - GRID_W)
    z_ext = jnp.concatenate([zp_ref[...], z_ref[...], zn_ref[...]], axis=0)
    rel = (lax.broadcasted_iota(jnp.int32, (tm, tm + 2 * GRID_W), 1) - GRID_W
           - lax.broadcasted_iota(jnp.int32, (tm, tm + 2 * GRID_W), 0))
    pick = (jnp.logical_and(rel == -1, m_l) | jnp.logical_and(rel == 1, m_r)
            | jnp.logical_and(rel == -GRID_W, m_u) | jnp.logical_and(rel == GRID_W, m_d))
    s = _dot(jnp.where(pick, 1.0, 0.0).astype(z_ext.dtype), z_ext)
    cnt = (m_l.astype(F32) + m_r.astype(F32)) + (m_u.astype(F32) + m_d.astype(F32))
    zs = z + (s * (1.0 / cnt) - z) * mu_ref[...]

    r = zs[:, 0:rw]
    k = zs[:, rw:2 * rw]
    v = zs[:, 2 * rw:3 * rw]
    o = 3 * rw
    wd = zs[:, o:o + LANES]
    ad = zs[:, o + LANES:o + 2 * LANES]
    gd = zs[:, o + 2 * LANES:o + 3 * LANES]

    seg = seg_ref[...]
    segt = segt_ref[...]

    def head_sum(t):
        return _dot_hilo(_dot_hilo(t, seg), segt)

    g_ref[0] = _dot(_sigmoid(gd), g2_ref[...], "bf16").astype(g_ref.dtype)
    kk = k * kk_w_ref[...]
    kk = kk * lax.rsqrt(head_sum(kk * kk) + 1e-12)
    wl = w0_ref[...] + _dot(jnp.tanh(wd), w2_ref[...], "bf16")
    al = a0_ref[...] + _dot(ad, a2_ref[...], "bf16")
    r_ref[0] = r.astype(r_ref.dtype)
    v_ref[0] = v.astype(v_ref.dtype)
    kk_ref[0] = kk.astype(kk_ref.dtype)
    k_sum = jnp.zeros_like(r)
    for d in range(2):
        a = _sigmoid(al[:, d * rw:(d + 1) * rw])
        k_d = k * (1.0 + (a - 1.0) * ka_ref[...])
        k_sum = k_sum + k_d
        lw_ref[d, 0] = -math.exp(-0.5) * _sigmoid(wl[:, d * rw:(d + 1) * rw])
        kd_ref[d, 0] = k_d.astype(kd_ref.dtype)
        be_ref[d, 0] = (kk * a).astype(be_ref.dtype)
    bonus_ref[0] = (head_sum(r * rk_ref[...] * k_sum) * v).astype(bonus_ref.dtype)


def _rw_prep(z_rw, bsz, l_ctx, l_lat, mu, w2bd, a2bd, g2, w0cat, a0cat, k_k, k_a, r_k_flat, seg, segt):
    cols = z_rw.shape[1]
    tm = l_ctx
    rw = g2.shape[1]
    l_all = l_ctx + l_lat
    nblk = l_all // tm
    lat_blk = l_lat // tm
    hb = tm // GRID_W
    lat_hblk = l_lat // GRID_W

    def main_blk(b, j):
        return jnp.where(j == 0, bsz * lat_blk + b, b * lat_blk + j - 1)

    def prev_halo(b, j):
        return b * lat_hblk + jnp.maximum((j - 1) * hb - 1, 0)

    def next_halo(b, j):
        return b * lat_hblk + jnp.minimum(jnp.maximum(j, 1) * hb, lat_hblk - 1)

    full = lambda shape: pl.BlockSpec(shape, lambda b, j: (0,) * len(shape))
    shared = jax.ShapeDtypeStruct((bsz, l_all, rw), BF16)
    lat_only = jax.ShapeDtypeStruct((bsz, l_lat, rw), BF16)
    per_dir = jax.ShapeDtypeStruct((2, bsz, l_all, rw), BF16)
    per_dir_f32 = jax.ShapeDtypeStruct((2, bsz, l_all, rw), F32)
    o_shared = pl.BlockSpec((1, tm, rw), lambda b, j: (b, j, 0))
    o_lat = pl.BlockSpec((1, tm, rw), lambda b, j: (b, jnp.maximum(j - 1, 0), 0))
    o_dir = pl.BlockSpec((2, 1, tm, rw), lambda b, j: (0, b, j, 0))
    return pl.pallas_call(
        functools.partial(_rw_prep_kernel, tm=tm, l_lat=l_lat, rw=rw),
        out_shape=(shared,) * 3 + (lat_only,) * 2 + (per_dir_f32, per_dir, per_dir),
        grid=(bsz, nblk),
        in_specs=[pl.BlockSpec((tm, cols), lambda b, j: (main_blk(b, j), 0)),
                  pl.BlockSpec((GRID_W, cols), lambda b, j: (prev_halo(b, j), 0)),
                  pl.BlockSpec((GRID_W, cols), lambda b, j: (next_halo(b, j), 0)),
                  full((1, cols)), full(w2bd.shape), full(a2bd.shape), full(g2.shape),
                  full((1, 2 * rw)), full((1, 2 * rw)), full((1, rw)), full((1, rw)), full((1, rw)),
                  full(seg.shape), full(segt.shape)],
        out_specs=(o_shared,) * 3 + (o_lat,) * 2 + (o_dir,) * 3,
        compiler_params=_cparams(("parallel", "arbitrary")),
        name="rwkv_prep",
    )(z_rw, z_rw, z_rw, mu.reshape(1, cols), w2bd, a2bd, g2, w0cat, a0cat,
      k_k.reshape(1, rw), k_a.reshape(1, rw), r_k_flat.reshape(1, rw), seg, segt)


def _stack_heads(x, head0):
    return jnp.concatenate([jnp.where(head0, x, 0.0), jnp.where(head0, 0.0, x)], axis=0)


def _rw_chunk_kernel(r_ref, v_ref, kk_ref, lw_ref, kd_ref, be_ref, y_ref,
                     z_scr, gc_s, lhs_s, kb_s, kht_s, bht_s, vs_s, rt_s, aab_s, akr_s, arb_s, t_s, x_s, xin_s,
                     yc_s, qy_s, bw_s, n_s, y_s, *, n_pairs, n_sub):
    c_n = RW_CHUNK
    n2 = 2 * c_n
    rev = (pl.program_id(0) % 2) == 1

    @pl.when(pl.program_id(1) == 0)
    def _():
        z_scr[...] = jnp.zeros_like(z_scr)

    ri = lax.broadcasted_iota(jnp.int32, (c_n, c_n), 0)
    ci = lax.broadcasted_iota(jnp.int32, (c_n, c_n), 1)
    tri = (jnp.where(rev, ci - ri, ri - ci) >= 0).astype(F32)
    r2 = lax.broadcasted_iota(jnp.int32, (n2, n2), 0)
    c2 = lax.broadcasted_iota(jnp.int32, (n2, n2), 1)
    t2 = r2 % c_n
    i2 = c2 % c_n
    same_head = (r2 // c_n) == (c2 // c_n)
    before = jnp.logical_and(same_head, jnp.where(rev, i2 - t2, t2 - i2) > 0)
    upto = jnp.logical_or(before, r2 == c2)
    eye = (r2 == c2).astype(F32)
    head0 = lax.broadcasted_iota(jnp.int32, (1, LANES), 1) < RW_HEAD

    def blk(s):
        return (r2 // s) == (c2 // s)

    pairs = range(n_pairs)
    units = range(n_sub * n_pairs)
    lanes_of = lambda p: slice(p * LANES, (p + 1) * LANES)

    row_of = lambda k: pl.multiple_of(jnp.where(rev, n_sub - 1 - k, k) * c_n, c_n)
    for cc in range(n_sub):
        rows = pl.ds(row_of(cc), c_n)
        lw = lw_ref[0, 0, rows, :]
        cum = _dot(tri, lw, HIGHEST)
        tot = jnp.sum(lw, axis=0, keepdims=True)
        g_inv = jnp.exp(-cum)
        g_hat = jnp.exp(tot - cum)
        kd = kd_ref[0, 0, rows, :]
        be = be_ref[0, 0, rows, :]
        at = kk_ref[0, rows, :] * jnp.exp(cum - lw)
        rt = r_ref[0, rows, :] * jnp.exp(cum)
        kt = kd * g_inv
        bt = be * g_inv
        kh = kd * g_hat
        bh = be * g_hat
        gc_s[cc] = jnp.exp(tot)
        v = v_ref[0, rows, :]
        for p in pairs:
            u = cc * n_pairs + p
            st = lambda x: _stack_heads(x[:, lanes_of(p)], head0)
            rt_p = st(rt)
            at_p = st(at).astype(BF16)
            lhs_s[u, :n2] = at_p
            lhs_s[u, n2:] = rt_p.astype(BF16)
            xin_s[u, :, :LANES] = at_p
            rt_s[u] = rt_p
            kb_s[u, :n2] = st(kt).astype(BF16)
            kb_s[u, n2:] = st(bt).astype(BF16)
            kht_s[u] = st(kh).T.astype(BF16)
            bht_s[u] = st(bh).T.astype(BF16)
            vs_s[u] = st(v).astype(BF16)

    for u in units:
        g = _dot_nt(lhs_s[u], kb_s[u])
        a_ab = jnp.where(before, g[:n2, n2:], 0.0)
        akr_s[u, :n2] = jnp.where(before, g[:n2, :n2], 0.0).astype(BF16)
        akr_s[u, n2:] = jnp.where(upto, g[n2:, :n2], 0.0).astype(BF16)
        arb_s[u] = jnp.where(upto, g[n2:, n2:], 0.0).astype(BF16)
        aab_s[u] = a_ab
        t_s[u] = eye - jnp.where(blk(2), a_ab, 0.0)

    s = 2
    while s < c_n:
        off = jnp.logical_and(blk(2 * s), jnp.logical_not(blk(s)))
        for u in units:
            x_s[u] = _dot(t_s[u].astype(BF16), jnp.where(off, aab_s[u], 0.0).astype(BF16)).astype(BF16)
        for u in units:
            t = t_s[u]
            t_s[u] = t - _dot(x_s[u], t.astype(BF16))
        s *= 2

    for u in units:
        av = _dot(akr_s[u], vs_s[u])
        xin_s[u, :, LANES:] = av[:n2].astype(BF16)
        yc_s[u] = av[n2:]

    for u in units:
        wu = _dot(t_s[u].astype(BF16), xin_s[u]).astype(BF16)
        q = _dot(arb_s[u], wu)
        bwu = _dot(bht_s[u], wu)
        qy_s[u] = (rt_s[u] - q[:, :LANES]).astype(BF16)
        yc_s[u] = yc_s[u] - q[:, LANES:]
        bw_s[u] = bwu[:, :LANES]
        n_s[u] = _dot(kht_s[u], vs_s[u]) - bwu[:, LANES:]

    for k in range(n_sub):
        g_c = gc_s[k]
        for p in pairs:
            u = k * n_pairs + p
            z0 = z_scr[p].astype(BF16)
            y_p = _dot(qy_s[u], z0) + yc_s[u]
            y_s[k, :, lanes_of(p)] = y_p[:c_n] + y_p[c_n:]
            m_z = eye * g_c[:, lanes_of(p)] - bw_s[u]
            z_scr[p] = _dot(m_z.astype(BF16), z0) + n_s[u]
    for k in range(n_sub):
        y_ref[0, 0, pl.ds(row_of(k), c_n), :] = y_s[k]


def _rw_scan(r, v, kk, lw, kd, be, l_ctx):
    bsz, l_all, rw = r.shape
    n_sub = RW_SUB
    c_n = RW_CHUNK
    rows = n_sub * c_n
    n_all = l_all // rows
    n_ctx = l_ctx // rows
    n_lat = n_all - n_ctx
    n_pairs = rw // LANES
    n2 = 2 * c_n
    vm = lambda nr, cols, dt: pltpu.VMEM((n_sub * n_pairs, nr, cols), dt)

    def chunk_of(bd, s):
        fwd = s
        bwd = jnp.where(s < n_ctx, n_ctx - 1 - s, n_all + n_ctx - 1 - s)
        return jnp.where(bd % 2 == 0, fwd, bwd)

    def out_chunk(bd, s):
        c = chunk_of(bd, s)
        edge = jnp.where(bd % 2 == 0, 0, n_lat - 1)
        return jnp.where(s < n_ctx, edge, c - n_ctx)

    shared = pl.BlockSpec((1, rows, rw), lambda bd, s: (bd // 2, chunk_of(bd, s), 0))
    per_dir = pl.BlockSpec((1, 1, rows, rw), lambda bd, s: (bd % 2, bd // 2, chunk_of(bd, s), 0))
    return pl.pallas_call(
        functools.partial(_rw_chunk_kernel, n_pairs=n_pairs, n_sub=n_sub),
        out_shape=jax.ShapeDtypeStruct((2, bsz, n_lat * rows, rw), F32),
        grid=(2 * bsz, n_all),
        in_specs=[shared, shared, shared, per_dir, per_dir, per_dir],
        out_specs=pl.BlockSpec((1, 1, rows, rw), lambda bd, s: (bd % 2, bd // 2, out_chunk(bd, s), 0)),
        scratch_shapes=[pltpu.VMEM((n_pairs, n2, n2), F32),
                        pltpu.VMEM((n_sub, 1, rw), F32),
                        vm(2 * n2, n2, BF16),
                        vm(2 * n2, n2, BF16),
                        vm(n2, n2, BF16),
                        vm(n2, n2, BF16),
                        vm(n2, n2, BF16),
                        vm(n2, n2, F32),
                        vm(n2, n2, F32),
                        vm(2 * n2, n2, BF16),
                        vm(n2, n2, BF16),
                        vm(n2, n2, F32),
                        vm(n2, n2, BF16),
                        vm(n2, 2 * n2, BF16),
                        vm(n2, n2, F32),
                        vm(n2, n2, BF16),
                        vm(n2, n2, F32),
                        vm(n2, n2, F32),
                        pltpu.VMEM((n_sub, c_n, rw), F32)],
        compiler_params=_cparams(("parallel", "arbitrary")),
        name="rwkv_scan",
    )(r, v, kk, lw, kd, be)


def _rw_step_kernel(rf_ref, rb_ref, vf_ref, vb_ref, kkf_ref, kkb_ref, lwf_ref, lwb_ref, kdf_ref, kdb_ref,
                    bef_ref, beb_ref, yf_ref, yb_ref,
                    z_scr, gc_s, lhs_s, kb_s, kht_s, bht_s, vs_s, rt_s, aab_s, akr_s, arb_s, t_s, x_s, xin_s,
                    yc_s, wu_s, qm_s, yn_s, *, n_pairs, bsz):
    c_n = RW_CHUNK
    n2 = 2 * c_n

    lanes_of = lambda p: slice(p * LANES, (p + 1) * LANES)
    unit = lambda d, b, p: (d * bsz + b) * n_pairs + p
    units = [(d, b, p) for d in range(2) for b in range(bsz) for p in range(n_pairs)]

    @pl.when(pl.program_id(0) == 0)
    def _():
        z_scr[...] = jnp.zeros_like(z_scr)
        qm_s[...] = jnp.zeros_like(qm_s)
        yn_s[...] = jnp.zeros_like(yn_s)

    for d, b, p in units:
        u = unit(d, b, p)
        y_ref = yf_ref if d == 0 else yb_ref
        yz = _dot(qm_s[u], z_scr[u].astype(BF16)) + yn_s[u]
        y_ref[b, :, lanes_of(p)] = yz[:c_n] + yz[c_n:n2]
        z_scr[u] = yz[n2:]

    ri = lax.broadcasted_iota(jnp.int32, (c_n, c_n), 0)
    ci = lax.broadcasted_iota(jnp.int32, (c_n, c_n), 1)
    r2 = lax.broadcasted_iota(jnp.int32, (n2, n2), 0)
    c2 = lax.broadcasted_iota(jnp.int32, (n2, n2), 1)
    t2 = r2 % c_n
    i2 = c2 % c_n
    same_head = (r2 // c_n) == (c2 // c_n)
    diag = r2 == c2
    eye = diag.astype(F32)
    head0 = lax.broadcasted_iota(jnp.int32, (1, LANES), 1) < RW_HEAD
    tri = [(ri >= ci).astype(F32), (ri <= ci).astype(F32)]
    before = [jnp.logical_and(same_head, i2 < t2), jnp.logical_and(same_head, i2 > t2)]
    upto = [jnp.logical_or(m, diag) for m in before]

    def blk(s):
        return (r2 // s) == (c2 // s)

    srcs = [(rf_ref, vf_ref, kkf_ref, lwf_ref, kdf_ref, bef_ref), (rb_ref, vb_ref, kkb_ref, lwb_ref, kdb_ref, beb_ref)]

    for d in range(2):
        r_ref, v_ref, kk_ref, lw_ref, kd_ref, be_ref = srcs[d]
        for b in range(bsz):
            lw = lw_ref[0, b]
            cum = _dot(tri[d], lw, HIGHEST)
            tot = jnp.sum(lw, axis=0, keepdims=True)
            g_inv = jnp.exp(-cum)
            g_hat = jnp.exp(tot - cum)
            kd = kd_ref[0, b].astype(F32)
            be = be_ref[0, b].astype(F32)
            at = kk_ref[b].astype(F32) * jnp.exp(cum - lw)
            rt = r_ref[b].astype(F32) * jnp.exp(cum)
            kt = kd * g_inv
            bt = be * g_inv
            kh = kd * g_hat
            bh = be * g_hat
            gc_s[d * bsz + b] = jnp.exp(tot)
            v = v_ref[b].astype(F32)
            for p in range(n_pairs):
                u = unit(d, b, p)
                st = lambda x: _stack_heads(x[:, lanes_of(p)], head0)
                rt_p = st(rt)
                at_p = st(at).astype(BF16)
                lhs_s[u, :n2] = at_p
                lhs_s[u, n2:] = rt_p.astype(BF16)
                xin_s[u, :, :LANES] = at_p
                rt_s[u] = rt_p
                kb_s[u, :n2] = st(kt).astype(BF16)
                kb_s[u, n2:] = st(bt).astype(BF16)
                kht_s[u] = st(kh).T.astype(BF16)
                bht_s[u] = st(bh).T.astype(BF16)
                vs_s[u] = st(v).astype(BF16)

    for d, b, p in units:
        u = unit(d, b, p)
        g = _dot_nt(lhs_s[u], kb_s[u])
        a_ab = jnp.where(before[d], g[:n2, n2:], 0.0)
        akr_s[u, :n2] = jnp.where(before[d], g[:n2, :n2], 0.0).astype(BF16)
        akr_s[u, n2:] = jnp.where(upto[d], g[n2:, :n2], 0.0).astype(BF16)
        arb_s[u] = jnp.where(upto[d], g[n2:, n2:], 0.0).astype(BF16)
        aab_s[u] = a_ab
        t_s[u] = eye - jnp.where(blk(2), a_ab, 0.0)

    n_units = len(units)
    s = 2
    while s < c_n:
        off = jnp.logical_and(blk(2 * s), jnp.logical_not(blk(s)))
        for u in range(n_units):
            x_s[u] = _dot(t_s[u].astype(BF16), jnp.where(off, aab_s[u], 0.0).astype(BF16)).astype(BF16)
        for u in range(n_units):
            t = t_s[u]
            t_s[u] = t - _dot(x_s[u], t.astype(BF16))
        s *= 2

    for u in range(n_units):
        av = _dot(akr_s[u], vs_s[u])
        xin_s[u, :, LANES:] = av[:n2].astype(BF16)
        yc_s[u] = av[n2:]

    for u in range(n_units):
        wu_s[u] = _dot(t_s[u].astype(BF16), xin_s[u]).astype(BF16)
    for d, b, p in units:
        u = unit(d, b, p)
        wu = wu_s[u]
        q = _dot(arb_s[u], wu)
        bwu = _dot(bht_s[u], wu)
        qm_s[u, :n2] = (rt_s[u] - q[:, :LANES]).astype(BF16)
        qm_s[u, n2:] = (eye * gc_s[d * bsz + b][:, lanes_of(p)] - bwu[:, :LANES]).astype(BF16)
        yn_s[u, :n2] = yc_s[u] - q[:, LANES:]
        yn_s[u, n2:] = _dot(kht_s[u], vs_s[u]) - bwu[:, LANES:]


def _rw_scan2(r, v, kk, lw, kd, be, l_ctx):
    bsz, l_all, rw = r.shape
    c_n = RW_CHUNK
    n_all = l_all // c_n
    n_ctx = l_ctx // c_n
    n_lat = n_all - n_ctx
    n_pairs = rw // LANES
    n_units = 2 * bsz * n_pairs
    n2 = 2 * c_n
    vm = lambda nr, cols, dt: pltpu.VMEM((n_units, nr, cols), dt)

    clamp = lambda s: jnp.minimum(s, n_all - 1)
    chunk_f = lambda s: clamp(s)
    chunk_b = lambda s: jnp.where(clamp(s) < n_ctx, n_ctx - 1 - clamp(s), n_all + n_ctx - 1 - clamp(s))
    prev = lambda s: jnp.maximum(s - 1, 0)
    out_f = lambda s: jnp.maximum(prev(s) - n_ctx, 0)
    out_b = lambda s: jnp.where(prev(s) < n_ctx, n_lat - 1, chunk_b(prev(s)) - n_ctx)

    sh_f = pl.BlockSpec((bsz, c_n, rw), lambda s: (0, chunk_f(s), 0))
    sh_b = pl.BlockSpec((bsz, c_n, rw), lambda s: (0, chunk_b(s), 0))
    pd_f = pl.BlockSpec((1, bsz, c_n, rw), lambda s: (0, 0, chunk_f(s), 0))
    pd_b = pl.BlockSpec((1, bsz, c_n, rw), lambda s: (1, 0, chunk_b(s), 0))
    y_shape = jax.ShapeDtypeStruct((bsz, n_lat * c_n, rw), F32)
    return pl.pallas_call(
        functools.partial(_rw_step_kernel, n_pairs=n_pairs, bsz=bsz),
        out_shape=(y_shape, y_shape),
        grid=(n_all + 1,),
        in_specs=[sh_f, sh_b, sh_f, sh_b, sh_f, sh_b, pd_f, pd_b, pd_f, pd_b, pd_f, pd_b],
        out_specs=(pl.BlockSpec((bsz, c_n, rw), lambda s: (0, out_f(s), 0)),
                   pl.BlockSpec((bsz, c_n, rw), lambda s: (0, out_b(s), 0))),
        scratch_shapes=[vm(n2, n2, F32),
                        pltpu.VMEM((2 * bsz, 1, rw), F32),
                        vm(2 * n2, n2, BF16),
                        vm(2 * n2, n2, BF16),
                        vm(n2, n2, BF16),
                        vm(n2, n2, BF16),
                        vm(n2, n2, BF16),
                        vm(n2, n2, F32),
                        vm(n2, n2, F32),
                        vm(2 * n2, n2, BF16),
                        vm(n2, n2, BF16),
                        vm(n2, n2, F32),
                        vm(n2, n2, BF16),
                        vm(n2, 2 * n2, BF16),
                        vm(n2, n2, F32),
                        vm(n2, 2 * n2, BF16),
                        vm(2 * n2, n2, BF16),
                        vm(2 * n2, n2, F32)],
        compiler_params=_cparams(("arbitrary",)),
        name="rwkv_scan",
    )(r, r, v, v, kk, kk, lw, lw, kd, kd, be, be)


def _s5_glu_kernel(y_ref, wa_ref, wb_ref, o_ref, h_scr):
    @pl.when(pl.program_id(1) == 0)
    def _():
        for jb in range(y_ref.shape[0]):
            h_scr[:, jb * LANES:(jb + 1) * LANES] = _gelu_tanh(y_ref[jb]).astype(BF16)

    h = h_scr[...]
    o_ref[...] = (_dot(h, wa_ref[...]) * _sigmoid(_dot(h, wb_ref[...]))).astype(o_ref.dtype)


def _s5_glu(y_blk, w, *, tm, tn):
    nb, m, _ = y_blk.shape
    k = nb * LANES
    n = w.shape[1] // 2
    nj = n // tn
    return pl.pallas_call(
        _s5_glu_kernel,
        out_shape=jax.ShapeDtypeStruct((m, n), BF16),
        grid=(m // tm, nj),
        in_specs=[pl.BlockSpec((nb, tm, LANES), lambda i, j: (0, i, 0)),
                  pl.BlockSpec((k, tn), lambda i, j: (0, j)),
                  pl.BlockSpec((k, tn), lambda i, j: (0, j + nj))],
        out_specs=pl.BlockSpec((tm, tn), lambda i, j: (i, j)),
        scratch_shapes=[pltpu.VMEM((tm, k), BF16)],
        compiler_params=_cparams(("parallel", "arbitrary")),
        name="s5_glu",
    )(y_blk, w, w)


def _rw_merge_kernel(yf_ref, yb_ref, bonus_ref, g_ref, lnw_ref, lnb_ref, seg_ref, segt_ref,
                     ga_ref, gb_ref, s5_ref, w_ref, o_ref, h_scr):
    @pl.when(pl.program_id(1) == 0)
    def _():
        seg = seg_ref[...]
        segt = segt_ref[...]
        inv_n = 1.0 / RW_HEAD

        def head_mean(t):
            return _dot_hilo(_dot_hilo(t, seg), segt) * inv_n

        y = yf_ref[...] + yb_ref[...]
        dy = y - head_mean(y)
        var = head_mean(dy * dy)
        y = dy * lax.rsqrt(var + GN_EPS) * lnw_ref[...] + lnb_ref[...] + bonus_ref[...].astype(F32)
        h_scr[...] = (y * g_ref[...].astype(F32)).astype(BF16)

    rw_out = _dot(h_scr[...], w_ref[...])
    merged = ga_ref[...].astype(F32) * s5_ref[...].astype(F32) + gb_ref[...].astype(F32) * rw_out
    o_ref[...] = merged.astype(o_ref.dtype)


def _rw_merge(y_f, y_b, bonus, g, ln_w, ln_b, seg, segt, gates, s5_out, w_proj, *, tm, tn):
    m, rw = y_f.shape
    n = w_proj.shape[1]
    nj = n // tn
    full = lambda shape: pl.BlockSpec(shape, lambda i, j: (0,) * len(shape))
    return pl.pallas_call(
        _rw_merge_kernel,
        out_shape=jax.ShapeDtypeStruct((m, n), BF16),
        grid=(m // tm, nj),
        in_specs=[pl.BlockSpec((tm, rw), lambda i, j: (i, 0)),
                  pl.BlockSpec((tm, rw), lambda i, j: (i, 0)),
                  pl.BlockSpec((tm, rw), lambda i, j: (i, 0)),
                  pl.BlockSpec((tm, rw), lambda i, j: (i, 0)),
                  full((1, rw)), full((1, rw)), full(seg.shape), full(segt.shape),
                  pl.BlockSpec((tm, tn), lambda i, j: (i, j)),
                  pl.BlockSpec((tm, tn), lambda i, j: (i, j + nj)),
                  pl.BlockSpec((tm, tn), lambda i, j: (i, j)),
                  pl.BlockSpec((rw, tn), lambda i, j: (0, j))],
        out_specs=pl.BlockSpec((tm, tn), lambda i, j: (i, j)),
        scratch_shapes=[pltpu.VMEM((tm, rw), BF16)],
        compiler_params=_cparams(("parallel", "arbitrary")),
        name="rwkv_merge",
    )(y_f, y_b, bonus, g, ln_w.reshape(1, rw), ln_b.reshape(1, rw), seg, segt,
      gates, gates, s5_out, w_proj)


def _resid_mm_kernel(a_ref, w_ref, x_ref, g_ref, o_ref):
    o_ref[...] = x_ref[...] + g_ref[0] * _dot(a_ref[...], w_ref[...])


def _resid_matmul(a, w, x2, g_tab, mod_row_of_block, *, tm, tn):
    m, k = a.shape
    n = w.shape[1]
    return pl.pallas_call(
        _resid_mm_kernel,
        out_shape=jax.ShapeDtypeStruct((m, n), F32),
        grid=(m // tm, n // tn),
        in_specs=[pl.BlockSpec((tm, k), lambda i, j: (i, 0)),
                  pl.BlockSpec((k, tn), lambda i, j: (0, j)),
                  pl.BlockSpec((tm, tn), lambda i, j: (i, j)),
                  pl.BlockSpec((1, 1, tn), lambda i, j: (mod_row_of_block(i), 0, j))],
        out_specs=pl.BlockSpec((tm, tn), lambda i, j: (i, j)),
        compiler_params=_cparams(("parallel", "arbitrary")),
        name="out_proj",
    )(a, w, x2, g_tab)


def _ffn_down_kernel(a_ref, w_ref, x_ref, g_ref, nf_ref, o_ref, acc_ref):
    kk = pl.program_id(1)

    @pl.when(kk == 0)
    def _():
        acc_ref[...] = jnp.zeros_like(acc_ref)

    acc_ref[...] += _dot(a_ref[...], w_ref[...])

    @pl.when(kk == pl.num_programs(1) - 1)
    def _():
        h = x_ref[...] + g_ref[0] * acc_ref[...]
        ms = jnp.mean(h * h, axis=-1, keepdims=True)
        o_ref[...] = h * lax.rsqrt(ms + NORM_EPS) * nf_ref[...]


def _ffn_down(a, w, x2, g_tab, mod_row_of_block, norm_f, *, tm, tk):
    m, k = a.shape
    n = w.shape[1]
    return pl.pallas_call(
        _ffn_down_kernel,
        out_shape=jax.ShapeDtypeStruct((m, n), F32),
        grid=(m // tm, k // tk),
        in_specs=[pl.BlockSpec((tm, tk), lambda i, kk: (i, kk)),
                  pl.BlockSpec((tk, n), lambda i, kk: (kk, 0)),
                  pl.BlockSpec((tm, n), lambda i, kk: (i, 0)),
                  pl.BlockSpec((1, 1, n), lambda i, kk: (mod_row_of_block(i), 0, 0)),
                  pl.BlockSpec((1, n), lambda i, kk: (0, 0))],
        out_specs=pl.BlockSpec((tm, n), lambda i, kk: (i, 0)),
        scratch_shapes=[pltpu.VMEM((tm, n), F32)],
        compiler_params=_cparams(("parallel", "arbitrary")),
        name="ffn_down",
    )(a, w, x2, g_tab, norm_f.reshape(1, n))


def kernel(x, c, ctx, c_ctx, ada_w, ada_b, norm1_w, w_in, rw_mu, s5_a_re, s5_a_im, s5_log_dt, s5_b_re, s5_b_im, s5_c_re, s5_c_im, s5_d, s5_glu_w, rw_w0, rw_w2, rw_a0, rw_a2, rw_g2, rw_k_k, rw_k_a, rw_r_k, rw_ln_w, rw_ln_b, rw_proj, w_o, norm2_w, ffn_w13, ffn_w2, norm_f):
    assert ada_w.shape[0] == 1, "single-layer block"
    bsz, l_lat, d = x.shape
    l_ctx = ctx.shape[1]
    l_all = l_ctx + l_lat
    s5w = s5_d.shape[1] * s5_d.shape[2]
    rw = rw_g2.shape[2]
    shift_cols = rw_mu.shape[1]
    d_ff = ffn_w2.shape[1]
    n_heads = rw // RW_HEAD

    c_rows = jnp.concatenate([c, c_ctx[None], jnp.zeros((8 - bsz - 1, d), F32)], axis=0)
    mod = _modulation(c_rows, ada_w[0], ada_b[0])
    tab = lambda k: mod[:, k * d:(k + 1) * d].reshape(8, 1, d)
    sh1, sc1, g1, sh2, sc2, g2 = (tab(k) for k in range(N_MOD))
    ctx_row = bsz

    tm_in = 512
    m_lat = bsz * l_lat
    x2 = x.reshape(m_lat, d)
    x_all = jnp.concatenate([x2, ctx.reshape(bsz * l_ctx, d)], axis=0)
    w_in_b = w_in[0].astype(BF16)
    n_mix = s5w + shift_cols
    lat_row = lambda t: (lambda i: i // (l_lat // t))

    def mix_mod_row(i):
        return jnp.where(i >= m_lat // tm_in, ctx_row, i // (l_lat // tm_in))

    u_blk = _lnmod_matmul(x_all, norm1_w[0], sh1, sc1, mix_mod_row, w_in_b[:, :s5w],
                          tm=tm_in, tn=s5w, out_dtype=F32, epilogue="lane_blocks", name="in_proj_s5")
    z_rw = _lnmod_matmul(x_all, norm1_w[0], sh1, sc1, mix_mod_row, w_in_b[:, s5w:n_mix],
                         tm=tm_in, tn=shift_cols // 3, out_dtype=BF16, name="in_proj_rw")
    tm = 1024
    gates = _lnmod_matmul(x2, norm1_w[0], sh1, sc1, lat_row(tm), w_in_b[:, n_mix:],
                          tm=tm, tn=1024, out_dtype=BF16, epilogue="sigmoid", name="in_proj_gates")

    y_blk = _s5_branch_blocked(u_blk, bsz, l_ctx, l_lat, s5_a_re[0], s5_a_im[0], s5_log_dt[0],
                               s5_b_re[0], s5_b_im[0], s5_c_re[0], s5_c_im[0], s5_d[0])
    s5_out = _s5_glu(y_blk, s5_glu_w[0].astype(BF16), tm=tm, tn=1024)

    lora = rw_w2.shape[2]
    zl = jnp.zeros((lora, rw), F32)
    w2bd = jnp.concatenate([jnp.concatenate([rw_w2[0, 0], zl], axis=1),
                            jnp.concatenate([zl, rw_w2[0, 1]], axis=1)], axis=0)
    a2bd = jnp.concatenate([jnp.concatenate([rw_a2[0, 0], zl], axis=1),
                            jnp.concatenate([zl, rw_a2[0, 1]], axis=1)], axis=0)
    head_of = jnp.arange(rw) // RW_HEAD
    seg = (head_of[:, None] == jnp.arange(LANES)[None, :]).astype(BF16)
    segt = seg.T
    r, v, kk, g, bonus, lw, kd, be = _rw_prep(
        z_rw, bsz, l_ctx, l_lat, rw_mu[0], w2bd, a2bd, rw_g2[0], rw_w0[0].reshape(1, 2 * rw),
        rw_a0[0].reshape(1, 2 * rw), rw_k_k[0], rw_k_a[0], rw_r_k[0].reshape(rw), seg, segt)
    y_f, y_b = _rw_scan2(r, v, kk, lw, kd, be, l_ctx)

    merged = _rw_merge(y_f.reshape(m_lat, rw), y_b.reshape(m_lat, rw), bonus.reshape(m_lat, rw), g.reshape(m_lat, rw),
                       rw_ln_w[0], rw_ln_b[0], seg, segt, gates, s5_out, rw_proj[0].astype(BF16),
                       tm=512, tn=1024)
    h1 = _resid_matmul(merged, w_o[0].astype(BF16), x2, g1, lat_row(tm), tm=tm, tn=1024)

    act = _lnmod_swiglu(h1, norm2_w[0], sh2, sc2, lat_row(tm), ffn_w13[0].astype(BF16), d_ff,
                        tm=tm, tn=512, name="ffn_up")
    tm_dn = 512
    out = _ffn_down(act, ffn_w2[0].astype(BF16), h1, g2, lat_row(tm_dn), norm_f, tm=tm_dn, tk=d_ff // 4)
    return out.reshape(bsz, l_lat, d)
```

```python
import functools
import math

import jax
import jax.numpy as jnp
from jax import lax
from jax.experimental import pallas as pl
from jax.experimental.pallas import tpu as pltpu

F32 = jnp.float32
BF16 = jnp.bfloat16
HIGHEST = lax.Precision.HIGHEST

D_MODEL = 2048
N_MOD = 6
NORM_EPS = 1e-6
GN_EPS = 64e-5
GRID_W = 64
S5_GROUP = 16
S5_STATE = 64
S5_CHUNK = 16
RW_HEAD = 64
RW_CHUNK = 64
RW_SUB = 2
LANES = 128
VMEM_LIMIT = 48 * 1024 * 1024


def _cparams(sem):
    return pltpu.CompilerParams(dimension_semantics=sem, vmem_limit_bytes=VMEM_LIMIT)


def _operands(a, b, precision):
    if precision == "bf16":
        return a.astype(BF16), b.astype(BF16), None
    return a, b, precision


def _dot(a, b, precision=None):
    a, b, precision = _operands(a, b, precision)
    return jnp.dot(a, b, preferred_element_type=F32, precision=precision)


def _dot_nt(a, b, precision=None):
    a, b, precision = _operands(a, b, precision)
    return lax.dot_general(a, b, (((1,), (1,)), ((), ())), preferred_element_type=F32, precision=precision)


def _dot_hilo(a, ind):
    hi = a.astype(BF16)
    lo = (a - hi.astype(F32)).astype(BF16)
    return _dot(hi, ind) + _dot(lo, ind)


def _sigmoid(x):
    return 1.0 / (1.0 + jnp.exp(-x))


def _silu(x):
    return x * _sigmoid(x)


def _gelu_tanh(x):
    c = math.sqrt(2.0 / math.pi)
    return 0.5 * x * (1.0 + jnp.tanh(c * (x + 0.044715 * (x * x * x))))


def _softplus(x):
    return jnp.maximum(x, 0.0) + jnp.log(1.0 + jnp.exp(-jnp.abs(x)))


def _mod_kernel(c_ref, w_ref, b_ref, o_ref):
    o_ref[...] = _dot(_silu(c_ref[...]), w_ref[...], HIGHEST) + b_ref[...]


def _modulation(c_rows, ada_w, ada_b):
    m, d = c_rows.shape
    n = ada_w.shape[1]
    tn = 1024
    return pl.pallas_call(
        _mod_kernel,
        out_shape=jax.ShapeDtypeStruct((m, n), F32),
        grid=(n // tn,),
        in_specs=[pl.BlockSpec((m, d), lambda j: (0, 0)),
                  pl.BlockSpec((d, tn), lambda j: (0, j)),
                  pl.BlockSpec((1, tn), lambda j: (0, j))],
        out_specs=pl.BlockSpec((m, tn), lambda j: (0, j)),
        compiler_params=_cparams(("arbitrary",)),
        name="modulation",
    )(c_rows, ada_w, ada_b.reshape(1, n))


def _lnmod_rows(x, nw, sh, sc):
    ms = jnp.mean(x * x, axis=-1, keepdims=True)
    y = x * lax.rsqrt(ms + NORM_EPS) * nw
    return y * (1.0 + sc) + sh


def _lnmod_mm_kernel(x_ref, nw_ref, sh_ref, sc_ref, w_ref, o_ref, h_scr, *, epilogue):
    @pl.when(pl.program_id(1) == 0)
    def _():
        h_scr[...] = _lnmod_rows(x_ref[...], nw_ref[...], sh_ref[0], sc_ref[0]).astype(BF16)

    z = _dot(h_scr[...], w_ref[...])
    if epilogue == "sigmoid":
        z = _sigmoid(z)
    if epilogue == "lane_blocks":
        for jb in range(o_ref.shape[0]):
            o_ref[jb] = z[:, jb * LANES:(jb + 1) * LANES].astype(o_ref.dtype)
    else:
        o_ref[...] = z.astype(o_ref.dtype)


def _lnmod_swiglu_kernel(x_ref, nw_ref, sh_ref, sc_ref, w1_ref, w3_ref, o_ref, h_scr):
    @pl.when(pl.program_id(1) == 0)
    def _():
        h_scr[...] = _lnmod_rows(x_ref[...], nw_ref[...], sh_ref[0], sc_ref[0]).astype(BF16)

    h = h_scr[...]
    o_ref[...] = (_silu(_dot(h, w1_ref[...])) * _dot(h, w3_ref[...])).astype(o_ref.dtype)


def _lnmod_matmul(x2, nw, sh_tab, sc_tab, mod_row_of_block, w, *, tm, tn, out_dtype, epilogue=None, name):
    m, d = x2.shape
    n = w.shape[1]
    mod_map = lambda i, j: (mod_row_of_block(i), 0, 0)
    if epilogue == "lane_blocks":
        assert tn == n
        out_shape = jax.ShapeDtypeStruct((n // LANES, m, LANES), out_dtype)
        out_spec = pl.BlockSpec((n // LANES, tm, LANES), lambda i, j: (0, i, 0))
    else:
        out_shape = jax.ShapeDtypeStruct((m, n), out_dtype)
        out_spec = pl.BlockSpec((tm, tn), lambda i, j: (i, j))
    return pl.pallas_call(
        functools.partial(_lnmod_mm_kernel, epilogue=epilogue),
        out_shape=out_shape,
        grid=(m // tm, n // tn),
        in_specs=[pl.BlockSpec((tm, d), lambda i, j: (i, 0)),
                  pl.BlockSpec((1, d), lambda i, j: (0, 0)),
                  pl.BlockSpec((1, 1, d), mod_map),
                  pl.BlockSpec((1, 1, d), mod_map),
                  pl.BlockSpec((d, tn), lambda i, j: (0, j))],
        out_specs=out_spec,
        scratch_shapes=[pltpu.VMEM((tm, d), BF16)],
        compiler_params=_cparams(("parallel", "arbitrary")),
        name=name,
    )(x2, nw.reshape(1, d), sh_tab, sc_tab, w)


def _lnmod_kernel(x_ref, nw_ref, sh_ref, sc_ref, o_ref):
    o_ref[...] = _lnmod_rows(x_ref[...], nw_ref[...], sh_ref[0], sc_ref[0]).astype(o_ref.dtype)


def _lnmod(x2, nw, sh_tab, sc_tab, mod_row_of_block, *, tm):
    m, d = x2.shape
    mod_map = lambda i: (mod_row_of_block(i), 0, 0)
    return pl.pallas_call(
        _lnmod_kernel,
        out_shape=jax.ShapeDtypeStruct((m, d), BF16),
        grid=(m // tm,),
        in_specs=[pl.BlockSpec((tm, d), lambda i: (i, 0)),
                  pl.BlockSpec((1, d), lambda i: (0, 0)),
                  pl.BlockSpec((1, 1, d), mod_map),
                  pl.BlockSpec((1, 1, d), mod_map)],
        out_specs=pl.BlockSpec((tm, d), lambda i: (i, 0)),
        compiler_params=_cparams(("parallel",)),
        name="lnmod",
    )(x2, nw.reshape(1, d), sh_tab, sc_tab)


def _wres_mm_kernel(a_ref, w_ref, o_ref, w_scr, *, epilogue):
    @pl.when(pl.program_id(1) == 0)
    def _():
        w_scr[...] = w_ref[...].astype(BF16)

    z = _dot(a_ref[...], w_scr[...])
    if epilogue == "sigmoid":
        z = _sigmoid(z)
    if epilogue == "lane_blocks":
        for jb in range(o_ref.shape[0]):
            o_ref[jb] = z[:, jb * LANES:(jb + 1) * LANES].astype(o_ref.dtype)
    else:
        o_ref[...] = z.astype(o_ref.dtype)


def _wres_matmul(a, w, col0, n, *, rows, tm, tn, out_dtype, epilogue=None, name):
    k = a.shape[1]
    assert col0 % LANES == 0 and n % tn == 0 and rows % tm == 0
    if epilogue == "lane_blocks":
        assert tn == n
        out_shape = jax.ShapeDtypeStruct((n // LANES, rows, LANES), out_dtype)
        out_spec = pl.BlockSpec((n // LANES, tm, LANES), lambda j, i: (0, i, 0))
    else:
        out_shape = jax.ShapeDtypeStruct((rows, n), out_dtype)
        out_spec = pl.BlockSpec((tm, tn), lambda j, i: (i, j))
    return pl.pallas_call(
        functools.partial(_wres_mm_kernel, epilogue=epilogue),
        out_shape=out_shape,
        grid=(n // tn, rows // tm),
        in_specs=[pl.BlockSpec((tm, k), lambda j, i: (i, 0)),
                  pl.BlockSpec((pl.Element(k), pl.Element(tn)), lambda j, i: (0, pl.multiple_of(col0 + j * tn, LANES)))],
        out_specs=out_spec,
        scratch_shapes=[pltpu.VMEM((k, tn), BF16)],
        compiler_params=_cparams(("arbitrary", "arbitrary")),
        name=name,
    )(a, w)


def _lnmod_swiglu(x2, nw, sh_tab, sc_tab, mod_row_of_block, w13, d_ff, *, tm, tn, name):
    m, d = x2.shape
    nj = d_ff // tn
    mod_map = lambda i, j: (mod_row_of_block(i), 0, 0)
    return pl.pallas_call(
        _lnmod_swiglu_kernel,
        out_shape=jax.ShapeDtypeStruct((m, d_ff), BF16),
        grid=(m // tm, nj),
        in_specs=[pl.BlockSpec((tm, d), lambda i, j: (i, 0)),
                  pl.BlockSpec((1, d), lambda i, j: (0, 0)),
                  pl.BlockSpec((1, 1, d), mod_map),
                  pl.BlockSpec((1, 1, d), mod_map),
                  pl.BlockSpec((d, tn), lambda i, j: (0, j)),
                  pl.BlockSpec((d, tn), lambda i, j: (0, j + nj))],
        out_specs=pl.BlockSpec((tm, tn), lambda i, j: (i, j)),
        scratch_shapes=[pltpu.VMEM((tm, d), BF16)],
        compiler_params=_cparams(("parallel", "arbitrary")),
        name=name,
    )(x2, nw.reshape(1, d), sh_tab, sc_tab, w13, w13)


def _s5_param_kernel(are_ref, aim_ref, ldt_ref, bre_ref, bim_ref, cre_ref, cim_ref,
                     e_ref, cs_ref, kt_ref, a16_ref):
    t_n, hg, p_n = S5_CHUNK, S5_GROUP, S5_STATE
    tau = lax.broadcasted_iota(jnp.int32, (t_n, 1, p_n), 0).astype(F32)
    for d in range(2):
        a_re = are_ref[0, d:d + 1, :]
        a_im = aim_ref[0, d:d + 1, :]
        dt = jnp.exp(ldt_ref[0, d:d + 1, :])
        lam = a_re * dt
        th = a_im * dt
        er = jnp.exp(lam)
        ab_re = er * jnp.cos(th)
        ab_im = er * jnp.sin(th)
        den = a_re * a_re + a_im * a_im
        x_re = ab_re - 1.0
        co_re = (x_re * a_re + ab_im * a_im) / den
        co_im = (ab_im * a_re - x_re * a_im) / den
        bt_re = bre_ref[0, d]
        bt_im = bim_ref[0, d]
        bb_re = co_re * bt_re - co_im * bt_im
        bb_im = co_re * bt_im + co_im * bt_re
        c_re = cre_ref[0, d]
        c_im = cim_ref[0, d]

        def power(tv):
            mag = jnp.exp(tv * lam)
            return mag * jnp.cos(tv * th), mag * jnp.sin(tv * th)

        pw_re, pw_im = power(tau)
        cp_re = (c_re[None] * pw_re - c_im[None] * pw_im).reshape(t_n * hg, p_n)
        cp_im = (c_re[None] * pw_im + c_im[None] * pw_re).reshape(t_n * hg, p_n)
        kt_ref[0, d] = _dot_nt(bb_re, cp_re, HIGHEST) - _dot_nt(bb_im, cp_im, HIGHEST)

        te = (t_n - 1.0 - tau) if d == 0 else tau
        pe_re, pe_im = power(te)
        e_ref[0, d, :, 0:p_n] = (pe_re * bb_re[None] - pe_im * bb_im[None]).reshape(t_n * hg, p_n)
        e_ref[0, d, :, p_n:2 * p_n] = (pe_re * bb_im[None] + pe_im * bb_re[None]).reshape(t_n * hg, p_n)

        tc = (tau + 1.0) if d == 0 else (t_n - tau)
        pc_re, pc_im = power(tc)
        cs_ref[0, d, :, 0:p_n] = (c_re[None] * pc_re - c_im[None] * pc_im).reshape(t_n * hg, p_n)
        cs_ref[0, d, :, p_n:2 * p_n] = -(c_re[None] * pc_im + c_im[None] * pc_re).reshape(t_n * hg, p_n)

        mag16 = jnp.exp(float(t_n) * lam)
        a16_ref[0, d, 0:1, :] = mag16 * jnp.cos(float(t_n) * th)
        a16_ref[0, d, 1:2, :] = mag16 * jnp.sin(float(t_n) * th)


def _s5_params(a_re, a_im, log_dt, b_re, b_im, c_re, c_im):
    g_n = a_re.shape[1]
    p_n, hg, t_n = S5_STATE, S5_GROUP, S5_CHUNK
    tr = lambda a: jnp.swapaxes(a, 0, 1)
    ldt = jnp.broadcast_to(tr(log_dt)[:, :, None], (g_n, 2, p_n))
    spec3 = pl.BlockSpec((1, 2, p_n), lambda g: (g, 0, 0))
    spec4 = pl.BlockSpec((1, 2, hg, p_n), lambda g: (g, 0, 0, 0))
    th = t_n * hg
    return pl.pallas_call(
        _s5_param_kernel,
        out_shape=(jax.ShapeDtypeStruct((g_n, 2, th, 2 * p_n), F32),
                   jax.ShapeDtypeStruct((g_n, 2, th, 2 * p_n), F32),
                   jax.ShapeDtypeStruct((g_n, 2, hg, th), F32),
                   jax.ShapeDtypeStruct((g_n, 2, 2, p_n), F32)),
        grid=(g_n,),
        in_specs=[spec3, spec3, spec3, spec4, spec4, spec4, spec4],
        out_specs=(pl.BlockSpec((1, 2, th, 2 * p_n), lambda g: (g, 0, 0, 0)),
                   pl.BlockSpec((1, 2, th, 2 * p_n), lambda g: (g, 0, 0, 0)),
                   pl.BlockSpec((1, 2, hg, th), lambda g: (g, 0, 0, 0)),
                   pl.BlockSpec((1, 2, 2, p_n), lambda g: (g, 0, 0, 0))),
        compiler_params=_cparams(("parallel",)),
        name="s5_params",
    )(tr(a_re), tr(a_im), ldt,
      jnp.transpose(b_re, (1, 0, 3, 2)), jnp.transpose(b_im, (1, 0, 3, 2)), tr(c_re), tr(c_im))


def _s5_state_in_kernel(u_ref, e_ref, o_ref):
    e = e_ref[0]
    u = u_ref[0]
    o_ref[0, :, 0:LANES] = _dot(u, e[0], HIGHEST)
    o_ref[0, :, LANES:2 * LANES] = _dot(u, e[1], HIGHEST)


def _s5_state_inputs(u_g, e):
    g_n, rows, th = u_g.shape
    return pl.pallas_call(
        _s5_state_in_kernel,
        out_shape=jax.ShapeDtypeStruct((g_n, rows, 2 * LANES), F32),
        grid=(g_n,),
        in_specs=[pl.BlockSpec((1, rows, th), lambda g: (g, 0, 0)),
                  pl.BlockSpec((1, 2, th, LANES), lambda g: (g, 0, 0, 0))],
        out_specs=pl.BlockSpec((1, rows, 2 * LANES), lambda g: (g, 0, 0)),
        compiler_params=_cparams(("parallel",)),
        name="s5_state_inputs",
    )(u_g, e)


def _s5_scan_kernel(e_ref, a_ref, o_ref, *, n_ctx, n_all):
    zero = jnp.zeros(e_ref.shape[2:], F32)

    def step(pr, pi):
        ar, ai = a_ref[pr], a_ref[pi]

        def body(c, carry):
            sr, si = carry
            o_ref[pr, c] = sr
            o_ref[pi, c] = si
            return (ar * sr - ai * si + e_ref[pr, c], ar * si + ai * sr + e_ref[pi, c])
        return body

    lax.fori_loop(0, n_all, step(0, 1), (zero, zero))
    bwd = step(2, 3)
    carry = lax.fori_loop(0, n_ctx, lambda k, cy: bwd(n_ctx - 1 - k, cy), (zero, zero))
    lax.fori_loop(0, n_all - n_ctx, lambda k, cy: bwd(n_all - 1 - k, cy), carry)


def _s5_scan(e_planes, a_planes, n_ctx, n_all):
    _, rows, r_n, _ = e_planes.shape
    bsz = rows // n_all
    sub = 8
    return pl.pallas_call(
        functools.partial(_s5_scan_kernel, n_ctx=n_ctx, n_all=n_all),
        out_shape=jax.ShapeDtypeStruct(e_planes.shape, F32),
        grid=(bsz, r_n // sub),
        in_specs=[pl.BlockSpec((4, n_all, sub, LANES), lambda b, q: (0, b, q, 0)),
                  pl.BlockSpec((4, sub, LANES), lambda b, q: (0, q, 0))],
        out_specs=pl.BlockSpec((4, n_all, sub, LANES), lambda b, q: (0, b, q, 0)),
        compiler_params=_cparams(("parallel", "parallel")),
        name="s5_scan",
    )(e_planes, a_planes)


def _s5_apply_kernel(u_ref, s_ref, mf_ref, mb_ref, d_ref, cs_ref, o_ref, *, n_ctx):
    u = u_ref[0, n_ctx:, :]
    s = s_ref[0, n_ctx:, :]
    y = _dot(u, mf_ref[0] + mb_ref[0], HIGHEST) + u * d_ref[0]
    y = y + _dot_nt(s[:, 0:LANES], cs_ref[0, 0], HIGHEST) + _dot_nt(s[:, LANES:2 * LANES], cs_ref[0, 1], HIGHEST)
    o_ref[0, 0] = y


def _s5_apply(u_g, s_g, m_f, m_b, d_t, cs, n_ctx, n_all):
    g_n, rows, th = u_g.shape
    bsz = rows // n_all
    n_lat = n_all - n_ctx
    return pl.pallas_call(
        functools.partial(_s5_apply_kernel, n_ctx=n_ctx),
        out_shape=jax.ShapeDtypeStruct((g_n, bsz, n_lat, th), F32),
        grid=(g_n, bsz),
        in_specs=[pl.BlockSpec((1, n_all, th), lambda g, b: (g, b, 0)),
                  pl.BlockSpec((1, n_all, 2 * LANES), lambda g, b: (g, b, 0)),
                  pl.BlockSpec((1, th, th), lambda g, b: (g, 0, 0)),
                  pl.BlockSpec((1, th, th), lambda g, b: (g, 0, 0)),
                  pl.BlockSpec((1, 1, th), lambda g, b: (g, 0, 0)),
                  pl.BlockSpec((1, 2, th, LANES), lambda g, b: (g, 0, 0, 0))],
        out_specs=pl.BlockSpec((1, 1, n_lat, th), lambda g, b: (g, b, 0, 0)),
        compiler_params=_cparams(("parallel", "arbitrary")),
        name="s5_apply",
    )(u_g, s_g, m_f, m_b, d_t, cs)


def _s5_branch(u_all, n_ctx_tok, a_re, a_im, log_dt, b_re, b_im, c_re, c_im, s5_d):
    bsz, l_all, width = u_all.shape
    hg, t_n, p_n = S5_GROUP, S5_CHUNK, S5_STATE
    g_n = width // hg
    n_all = l_all // t_n
    n_ctx = n_ctx_tok // t_n
    th = t_n * hg
    e, cs, kt, a16 = _s5_params(a_re, a_im, log_dt, b_re, b_im, c_re, c_im)

    kt5 = kt.reshape(g_n, 2, hg, t_n, hg)
    ii = jnp.arange(t_n)[:, None]
    jj = jnp.arange(t_n)[None, :]

    def toeplitz(k4, lag, keep):
        m = k4[:, :, jnp.clip(lag, 0, t_n - 1), :]
        m = jnp.where(keep[None, None, :, :, None], m, 0.0)
        return jnp.transpose(m, (0, 2, 1, 3, 4)).reshape(g_n, th, th)

    m_f = toeplitz(kt5[:, 0], jj - ii, jj >= ii)
    m_b = toeplitz(kt5[:, 1], ii - jj, ii >= jj)
    d_t = jnp.tile(s5_d, (1, t_n)).reshape(g_n, 1, th)

    u_g = jnp.transpose(u_all.reshape(bsz, n_all, t_n, g_n, hg), (3, 0, 1, 2, 4)).reshape(g_n, bsz * n_all, th)
    e_cat = e
    s_in = _s5_state_inputs(u_g, e_cat)
    rows = bsz * n_all
    planes = jnp.transpose(s_in.reshape(g_n, rows, 4, p_n), (2, 1, 0, 3)).reshape(4, rows, g_n * p_n // LANES, LANES)
    a_pl = jnp.transpose(a16.reshape(g_n, 4, p_n), (1, 0, 2)).reshape(4, g_n * p_n // LANES, LANES)
    st = _s5_scan(planes, a_pl, n_ctx, n_all)
    s_g = jnp.transpose(st.reshape(4, rows, g_n, p_n), (2, 1, 0, 3)).reshape(g_n, rows, 4 * p_n)
    y_g = _s5_apply(u_g, s_g, m_f, m_b, d_t, cs, n_ctx, n_all)
    n_lat = n_all - n_ctx
    y = jnp.transpose(y_g.reshape(g_n, bsz, n_lat, t_n, hg), (1, 2, 3, 0, 4))
    return y.reshape(bsz, n_lat * t_n, width)


def _s5_param2_kernel(are_ref, aim_ref, ldt_ref, bre_ref, bim_ref, cre_ref, cim_ref,
                      e_ref, c_ref, m_ref, a16_ref):
    t_n, hg, p_n = S5_CHUNK, S5_GROUP, S5_STATE
    tau = lax.broadcasted_iota(jnp.int32, (t_n, 1, p_n), 0).astype(F32)
    taps = []
    for d in range(2):
        a_re = are_ref[0, d:d + 1, :]
        a_im = aim_ref[0, d:d + 1, :]
        dt = jnp.exp(ldt_ref[0, d:d + 1, :])
        lam = a_re * dt
        th = a_im * dt
        er = jnp.exp(lam)
        ab_re = er * jnp.cos(th)
        ab_im = er * jnp.sin(th)
        den = a_re * a_re + a_im * a_im
        x_re = ab_re - 1.0
        co_re = (x_re * a_re + ab_im * a_im) / den
        co_im = (ab_im * a_re - x_re * a_im) / den
        bt_re = bre_ref[0, d]
        bt_im = bim_ref[0, d]
        bb_re = co_re * bt_re - co_im * bt_im
        bb_im = co_re * bt_im + co_im * bt_re
        c_re = cre_ref[0, d]
        c_im = cim_ref[0, d]

        def power(tv):
            mag = jnp.exp(tv * lam)
            return mag * jnp.cos(tv * th), mag * jnp.sin(tv * th)

        pw_re, pw_im = power(tau if d == 0 else (t_n - 1.0 - tau))
        cp_re = (c_re[None] * pw_re - c_im[None] * pw_im).reshape(t_n * hg, p_n)
        cp_im = (c_re[None] * pw_im + c_im[None] * pw_re).reshape(t_n * hg, p_n)
        taps.append(_dot_nt(bb_re, cp_re, HIGHEST) - _dot_nt(bb_im, cp_im, HIGHEST))

        lo, hi = 2 * d * p_n, (2 * d + 1) * p_n
        pe_re, pe_im = power((t_n - 1.0 - tau) if d == 0 else tau)
        e_ref[0, :, 0, :, lo:hi] = pe_re * bb_re[None] - pe_im * bb_im[None]
        e_ref[0, :, 0, :, hi:hi + p_n] = pe_re * bb_im[None] + pe_im * bb_re[None]
        pc_re, pc_im = power((tau + 1.0) if d == 0 else (t_n - tau))
        c_ref[0, :, 0, :, lo:hi] = c_re[None] * pc_re - c_im[None] * pc_im
        c_ref[0, :, 0, :, hi:hi + p_n] = -(c_re[None] * pc_im + c_im[None] * pc_re)

        mag16 = jnp.exp(float(t_n) * lam)
        a16_ref[0, d, 0:1, :] = mag16 * jnp.cos(float(t_n) * th)
        a16_ref[0, d, 1:2, :] = mag16 * jnp.sin(float(t_n) * th)

    width = t_n * hg
    lane = lax.broadcasted_iota(jnp.int32, (hg, width), 1)
    for t in range(t_n):
        sf = t * hg
        sb = (t_n - 1 - t) * hg
        f = taps[0] if sf == 0 else jnp.where(lane >= sf, pltpu.roll(taps[0], sf, 1), 0.0)
        b = taps[1] if sb == 0 else jnp.where(lane < width - sb, pltpu.roll(taps[1], width - sb, 1), 0.0)
        m_ref[0, t, 0] = f + b


def _s5_params2(a_re, a_im, log_dt, b_re, b_im, c_re, c_im, nb):
    g_n = a_re.shape[1]
    gl_n = g_n // nb
    p_n, hg, t_n = S5_STATE, S5_GROUP, S5_CHUNK
    tr = lambda a: jnp.swapaxes(a, 0, 1)
    ldt = jnp.broadcast_to(tr(log_dt)[:, :, None], (g_n, 2, p_n))
    spec3 = pl.BlockSpec((1, 2, p_n), lambda g: (g, 0, 0))
    spec4 = pl.BlockSpec((1, 2, hg, p_n), lambda g: (g, 0, 0, 0))
    wide = 4 * p_n
    comp = jax.ShapeDtypeStruct((nb, t_n, gl_n, hg, wide), F32)
    comp_spec = pl.BlockSpec((1, t_n, 1, hg, wide), lambda g: (g // gl_n, 0, g % gl_n, 0, 0))
    return pl.pallas_call(
        _s5_param2_kernel,
        out_shape=(comp, comp, comp, jax.ShapeDtypeStruct((g_n, 2, 2, p_n), F32)),
        grid=(g_n,),
        in_specs=[spec3, spec3, spec3, spec4, spec4, spec4, spec4],
        out_specs=(comp_spec, comp_spec, comp_spec, pl.BlockSpec((1, 2, 2, p_n), lambda g: (g, 0, 0, 0))),
        compiler_params=_cparams(("parallel",)),
        name="s5_params",
    )(tr(a_re), tr(a_im), ldt,
      jnp.transpose(b_re, (1, 0, 3, 2)), jnp.transpose(b_im, (1, 0, 3, 2)), tr(c_re), tr(c_im))


def _expand_block_diag(comp, rep_ref, mask_ref, w_scr):
    k = w_scr.shape[0]
    period = mask_ref.shape[1]
    cb = comp.astype(BF16)
    step = 512
    for c0 in range(0, k, step):
        blk = _dot(cb, rep_ref[:, c0:c0 + step]).astype(BF16)
        for q0 in range(0, step, period):
            w_scr[:, c0 + q0:c0 + q0 + period] = blk[:, q0:q0 + period] * mask_ref[...]


def _s5_ein_kernel(u_ref, ec_ref, rep_ref, mask_ref, o_ref, w_scr):
    _expand_block_diag(ec_ref[0], rep_ref, mask_ref, w_scr)
    o_ref[0] = _dot(u_ref[0].astype(BF16), w_scr[...])


def _s5_chunk_inputs(u_rows, e_comp, rep_e, mask_e):
    nb, rows, k = u_rows.shape
    cw = e_comp.shape[2]
    return pl.pallas_call(
        _s5_ein_kernel,
        out_shape=jax.ShapeDtypeStruct((nb, rows, k), F32),
        grid=(nb,),
        in_specs=[pl.BlockSpec((1, rows, k), lambda j: (j, 0, 0)),
                  pl.BlockSpec((1, k, cw), lambda j: (j, 0, 0)),
                  pl.BlockSpec(rep_e.shape, lambda j: (0, 0)),
                  pl.BlockSpec(mask_e.shape, lambda j: (0, 0))],
        out_specs=pl.BlockSpec((1, rows, k), lambda j: (j, 0, 0)),
        scratch_shapes=[pltpu.VMEM((k, k), BF16)],
        compiler_params=_cparams(("parallel",)),
        name="s5_chunk_inputs",
    )(u_rows, e_comp, rep_e, mask_e)


def _s5_bscan_kernel(e_ref, a_ref, o_ref, *, bsz, n_ctx, n_lat):
    q = e_ref.shape[2] // 4
    planes = lambda row, d: (row[:, (2 * d) * q:(2 * d + 1) * q], row[:, (2 * d + 1) * q:(2 * d + 2) * q])
    coef = [planes(a_ref[0], d) for d in range(2)]
    ctx0 = bsz * n_lat

    def advance(state, rows):
        new = []
        for (sr, si), (b, d), row in zip(state, [(b, d) for b in range(bsz) for d in range(2)], rows):
            ar, ai = coef[d]
            er, ei = planes(e_ref[0, pl.ds(row, 1), :], d)
            new.append((ar * sr - ai * si + er, ar * si + ai * sr + ei))
        return tuple(new)

    def ctx_step(s, state):
        rows = [ctx0 + b * n_ctx + (s if d == 0 else n_ctx - 1 - s) for b in range(bsz) for d in range(2)]
        return advance(state, rows)

    def lat_step(s, state):
        rows = [b * n_lat + (s if d == 0 else n_lat - 1 - s) for b in range(bsz) for d in range(2)]
        for (sr, si), (b, d), row in zip(state, [(b, d) for b in range(bsz) for d in range(2)], rows):
            o_ref[0, pl.ds(row, 1), (2 * d) * q:(2 * d + 1) * q] = sr
            o_ref[0, pl.ds(row, 1), (2 * d + 1) * q:(2 * d + 2) * q] = si
        return advance(state, rows)

    zero = jnp.zeros((1, q), F32)
    state = tuple((zero, zero) for _ in range(2 * bsz))
    state = lax.fori_loop(0, n_ctx, ctx_step, state)
    lax.fori_loop(0, n_lat, lat_step, state)


def _s5_bscan(e_rows, a_rows, bsz, n_ctx, n_lat):
    nb, rows, k = e_rows.shape
    return pl.pallas_call(
        functools.partial(_s5_bscan_kernel, bsz=bsz, n_ctx=n_ctx, n_lat=n_lat),
        out_shape=jax.ShapeDtypeStruct((nb, bsz * n_lat, k), F32),
        grid=(nb,),
        in_specs=[pl.BlockSpec((1, rows, k), lambda j: (j, 0, 0)),
                  pl.BlockSpec((1, 1, k), lambda j: (j, 0, 0))],
        out_specs=pl.BlockSpec((1, bsz * n_lat, k), lambda j: (j, 0, 0)),
        compiler_params=_cparams(("parallel",)),
        name="s5_scan",
    )(e_rows, a_rows)


def _s5_out_kernel(u_ref, s_ref, mc_ref, cc_ref, d_ref, rep_m_ref, mask_m_ref, rep_e_ref, mask_e_ref,
                   o_ref, wm_scr, wc_scr):
    @pl.when(pl.program_id(1) == 0)
    def _():
        _expand_block_diag(mc_ref[0], rep_m_ref, mask_m_ref, wm_scr)
        _expand_block_diag(cc_ref[0], rep_e_ref, mask_e_ref, wc_scr)

    u = u_ref[0]
    y = _dot(u.astype(BF16), wm_scr[...]) + _dot_nt(s_ref[0].astype(BF16), wc_scr[...])
    o_ref[0] = y + u * d_ref[0]


def _s5_outputs(u_rows, s_rows, m_comp, c_comp, d_rows, rep_m, mask_m, rep_e, mask_e):
    nb, _, k = u_rows.shape
    rows = s_rows.shape[1]
    cw = m_comp.shape[2]
    tr = rows // 2
    const = lambda a: pl.BlockSpec(a.shape, lambda j, i: (0, 0))
    return pl.pallas_call(
        _s5_out_kernel,
        out_shape=jax.ShapeDtypeStruct((nb, rows, k), F32),
        grid=(nb, rows // tr),
        in_specs=[pl.BlockSpec((1, tr, k), lambda j, i: (j, i, 0)),
                  pl.BlockSpec((1, tr, k), lambda j, i: (j, i, 0)),
                  pl.BlockSpec((1, k, cw), lambda j, i: (j, 0, 0)),
                  pl.BlockSpec((1, k, cw), lambda j, i: (j, 0, 0)),
                  pl.BlockSpec((1, 1, k), lambda j, i: (j, 0, 0)),
                  const(rep_m), const(mask_m), const(rep_e), const(mask_e)],
        out_specs=pl.BlockSpec((1, tr, k), lambda j, i: (j, i, 0)),
        scratch_shapes=[pltpu.VMEM((k, k), BF16), pltpu.VMEM((k, k), BF16)],
        compiler_params=_cparams(("parallel", "arbitrary")),
        name="s5_outputs",
    )(u_rows, s_rows, m_comp, c_comp, d_rows, rep_m, mask_m, rep_e, mask_e)


def _s5_branch_blocked(u_blk, bsz, l_ctx, l_lat, a_re, a_im, log_dt, b_re, b_im, c_re, c_im, s5_d):
    nb, m_all, _ = u_blk.shape
    hg, t_n, p_n = S5_GROUP, S5_CHUNK, S5_STATE
    g_n = a_re.shape[1]
    gl_n = g_n // nb
    k = t_n * LANES
    n_lat = l_lat // t_n
    n_ctx = l_ctx // t_n
    e_c, c_c, m_c, a16 = _s5_params2(a_re, a_im, log_dt, b_re, b_im, c_re, c_im, nb)
    cw = 4 * p_n
    e_comp, c_comp, m_comp = (a.reshape(nb, k, cw) for a in (e_c, c_c, m_c))

    row_gl = (jnp.arange(k) // hg) % gl_n
    col = jnp.arange(k)
    src = jnp.arange(cw)
    rep_e = ((src[:, None] // p_n == col[None, :] // (gl_n * p_n)) & (src[:, None] % p_n == col[None, :] % p_n)).astype(BF16)
    mask_e = (row_gl[:, None] == (jnp.arange(gl_n * p_n)[None, :] // p_n)).astype(BF16)
    rep_m = ((src[:, None] // hg == col[None, :] // (gl_n * hg)) & (src[:, None] % hg == col[None, :] % hg)).astype(BF16)
    mask_m = (row_gl[:, None] == (jnp.arange(gl_n * hg)[None, :] // hg)).astype(BF16)

    d_rows = jnp.tile(s5_d.reshape(nb, 1, gl_n * hg), (1, 1, t_n))
    a_rows = jnp.transpose(a16.reshape(nb, gl_n, 2, 2, p_n), (0, 2, 3, 1, 4)).reshape(nb, 1, 4 * gl_n * p_n)

    u_rows = u_blk.reshape(nb, m_all // t_n, k)
    e_rows = _s5_chunk_inputs(u_rows, e_comp, rep_e, mask_e)
    s_rows = _s5_bscan(e_rows, a_rows, bsz, n_ctx, n_lat)
    y_rows = _s5_outputs(u_rows, s_rows, m_comp, c_comp, d_rows, rep_m, mask_m, rep_e, mask_e)
    return y_rows.reshape(nb, bsz * l_lat, LANES)


def _rw_prep_kernel(z_ref, zp_ref, zn_ref, mu_ref, w2_ref, a2_ref, g2_ref, w0_ref, a0_ref,
                    kk_w_ref, ka_ref, rk_ref, seg_ref, segt_ref,
                    r_ref, v_ref, kk_ref, g_ref, bonus_ref, lw_ref, kd_ref, be_ref,
                    *, tm, l_lat, rw):
    j = pl.program_id(1)
    z = z_ref[...].astype(F32)
    lat = j > 0
    tl = lax.broadcasted_iota(jnp.int32, (tm, 1), 0)
    tok = (j - 1) * tm + tl
    col = tl % GRID_W
    m_l = jnp.where(lat, col, tl) > 0
    m_r = jnp.where(lat, col - (GRID_W - 1), tl - (tm - 1)) < 0
    m_u = jnp.logical_and(lat, tok >= GRID_W)
    m_d = jnp.logical_and(lat, tok < l_lat - GRID_W)
    z_ext = jnp.concatenate([zp_ref[...], z_ref[...], zn_ref[...]], axis=0)
    rel = (lax.broadcasted_iota(jnp.int32, (tm, tm + 2 * GRID_W), 1) - GRID_W
           - lax.broadcasted_iota(jnp.int32, (tm, tm + 2 * GRID_W), 0))
    pick = (jnp.logical_and(rel == -1, m_l) | jnp.logical_and(rel == 1, m_r)
            | jnp.logical_and(rel == -GRID_W, m_u) | jnp.logical_and(rel == GRID_W, m_d))
    s = _dot(jnp.where(pick, 1.0, 0.0).astype(z_ext.dtype), z_ext)
    cnt = (m_l.astype(F32) + m_r.astype(F32)) + (m_u.astype(F32) + m_d.astype(F32))
    zs = z + (s * (1.0 / cnt) - z) * mu_ref[...]

    r = zs[:, 0:rw]
    k = zs[:, rw:2 * rw]
    v = zs[:, 2 * rw:3 * rw]
    o = 3 * rw
    wd = zs[:, o:o + LANES]
    ad = zs[:, o + LANES:o + 2 * LANES]
    gd = zs[:, o + 2 * LANES:o + 3 * LANES]

    seg = seg_ref[...]
    segt = segt_ref[...]

    def head_sum(t):
        return _dot_hilo(_dot_hilo(t, seg), segt)

    g_ref[0] = _dot(_sigmoid(gd), g2_ref[...], "bf16").astype(g_ref.dtype)
    kk = k * kk_w_ref[...]
    kk = kk * lax.rsqrt(head_sum(kk * kk) + 1e-12)
    wl = w0_ref[...] + _dot(jnp.tanh(wd), w2_ref[...], "bf16")
    al = a0_ref[...] + _dot(ad, a2_ref[...], "bf16")
    r_ref[0] = r.astype(r_ref.dtype)
    v_ref[0] = v.astype(v_ref.dtype)
    kk_ref[0] = kk.astype(kk_ref.dtype)
    k_sum = jnp.zeros_like(r)
    for d in range(2):
        a = _sigmoid(al[:, d * rw:(d + 1) * rw])
        k_d = k * (1.0 + (a - 1.0) * ka_ref[...])
        k_sum = k_sum + k_d
        lw_ref[d, 0] = -math.exp(-0.5) * _sigmoid(wl[:, d * rw:(d + 1) * rw])
        kd_ref[d, 0] = k_d.astype(kd_ref.dtype)
        be_ref[d, 0] = (kk * a).astype(be_ref.dtype)
    bonus_ref[0] = (head_sum(r * rk_ref[...] * k_sum) * v).astype(bonus_ref.dtype)


def _rw_prep(z_rw, bsz, l_ctx, l_lat, mu, w2bd, a2bd, g2, w0cat, a0cat, k_k, k_a, r_k_flat, seg, segt):
    cols = z_rw.shape[1]
    tm = l_ctx
    rw = g2.shape[1]
    l_all = l_ctx + l_lat
    nblk = l_all // tm
    lat_blk = l_lat // tm
    hb = tm // GRID_W
    lat_hblk = l_lat // GRID_W

    def main_blk(b, j):
        return jnp.where(j == 0, bsz * lat_blk + b, b * lat_blk + j - 1)

    def prev_halo(b, j):
        return b * lat_hblk + jnp.maximum((j - 1) * hb - 1, 0)

    def next_halo(b, j):
        return b * lat_hblk + jnp.minimum(jnp.maximum(j, 1) * hb, lat_hblk - 1)

    full = lambda shape: pl.BlockSpec(shape, lambda b, j: (0,) * len(shape))
    shared = jax.ShapeDtypeStruct((bsz, l_all, rw), BF16)
    lat_only = jax.ShapeDtypeStruct((bsz, l_lat, rw), BF16)
    per_dir = jax.ShapeDtypeStruct((2, bsz, l_all, rw), BF16)
    per_dir_f32 = jax.ShapeDtypeStruct((2, bsz, l_all, rw), F32)
    o_shared = pl.BlockSpec((1, tm, rw), lambda b, j: (b, j, 0))
    o_lat = pl.BlockSpec((1, tm, rw), lambda b, j: (b, jnp.maximum(j - 1, 0), 0))
    o_dir = pl.BlockSpec((2, 1, tm, rw), lambda b, j: (0, b, j, 0))
    return pl.pallas_call(
        functools.partial(_rw_prep_kernel, tm=tm, l_lat=l_lat, rw=rw),
        out_shape=(shared,) * 3 + (lat_only,) * 2 + (per_dir_f32, per_dir, per_dir),
        grid=(bsz, nblk),
        in_specs=[pl.BlockSpec((tm, cols), lambda b, j: (main_blk(b, j), 0)),
                  pl.BlockSpec((GRID_W, cols), lambda b, j: (prev_halo(b, j), 0)),
                  pl.BlockSpec((GRID_W, cols), lambda b, j: (next_halo(b, j), 0)),
                  full((1, cols)), full(w2bd.shape), full(a2bd.shape), full(g2.shape),
                  full((1, 2 * rw)), full((1, 2 * rw)), full((1, rw)), full((1, rw)), full((1, rw)),
                  full(seg.shape), full(segt.shape)],
        out_specs=(o_shared,) * 3 + (o_lat,) * 2 + (o_dir,) * 3,
        compiler_params=_cparams(("parallel", "arbitrary")),
        name="rwkv_prep",
    )(z_rw, z_rw, z_rw, mu.reshape(1, cols), w2bd, a2bd, g2, w0cat, a0cat,
      k_k.reshape(1, rw), k_a.reshape(1, rw), r_k_flat.reshape(1, rw), seg, segt)


def _stack_heads(x, head0):
    return jnp.concatenate([jnp.where(head0, x, 0.0), jnp.where(head0, 0.0, x)], axis=0)


def _rw_chunk_kernel(r_ref, v_ref, kk_ref, lw_ref, kd_ref, be_ref, y_ref,
                     z_scr, gc_s, lhs_s, kb_s, kht_s, bht_s, vs_s, rt_s, aab_s, akr_s, arb_s, t_s, x_s, xin_s,
                     yc_s, qy_s, bw_s, n_s, y_s, *, n_pairs, n_sub):
    c_n = RW_CHUNK
    n2 = 2 * c_n
    rev = (pl.program_id(0) % 2) == 1

    @pl.when(pl.program_id(1) == 0)
    def _():
        z_scr[...] = jnp.zeros_like(z_scr)

    ri = lax.broadcasted_iota(jnp.int32, (c_n, c_n), 0)
    ci = lax.broadcasted_iota(jnp.int32, (c_n, c_n), 1)
    tri = (jnp.where(rev, ci - ri, ri - ci) >= 0).astype(F32)
    r2 = lax.broadcasted_iota(jnp.int32, (n2, n2), 0)
    c2 = lax.broadcasted_iota(jnp.int32, (n2, n2), 1)
    t2 = r2 % c_n
    i2 = c2 % c_n
    same_head = (r2 // c_n) == (c2 // c_n)
    before = jnp.logical_and(same_head, jnp.where(rev, i2 - t2, t2 - i2) > 0)
    upto = jnp.logical_or(before, r2 == c2)
    eye = (r2 == c2).astype(F32)
    head0 = lax.broadcasted_iota(jnp.int32, (1, LANES), 1) < RW_HEAD

    def blk(s):
        return (r2 // s) == (c2 // s)

    pairs = range(n_pairs)
    units = range(n_sub * n_pairs)
    lanes_of = lambda p: slice(p * LANES, (p + 1) * LANES)

    row_of = lambda k: pl.multiple_of(jnp.where(rev, n_sub - 1 - k, k) * c_n, c_n)
    for cc in range(n_sub):
        rows = pl.ds(row_of(cc), c_n)
        lw = lw_ref[0, 0, rows, :]
        cum = _dot(tri, lw, HIGHEST)
        tot = jnp.sum(lw, axis=0, keepdims=True)
        g_inv = jnp.exp(-cum)
        g_hat = jnp.exp(tot - cum)
        kd = kd_ref[0, 0, rows, :]
        be = be_ref[0, 0, rows, :]
        at = kk_ref[0, rows, :] * jnp.exp(cum - lw)
        rt = r_ref[0, rows, :] * jnp.exp(cum)
        kt = kd * g_inv
        bt = be * g_inv
        kh = kd * g_hat
        bh = be * g_hat
        gc_s[cc] = jnp.exp(tot)
        v = v_ref[0, rows, :]
        for p in pairs:
            u = cc * n_pairs + p
            st = lambda x: _stack_heads(x[:, lanes_of(p)], head0)
            rt_p = st(rt)
            at_p = st(at).astype(BF16)
            lhs_s[u, :n2] = at_p
            lhs_s[u, n2:] = rt_p.astype(BF16)
            xin_s[u, :, :LANES] = at_p
            rt_s[u] = rt_p
            kb_s[u, :n2] = st(kt).astype(BF16)
            kb_s[u, n2:] = st(bt).astype(BF16)
            kht_s[u] = st(kh).T.astype(BF16)
            bht_s[u] = st(bh).T.astype(BF16)
            vs_s[u] = st(v).astype(BF16)

    for u in units:
        g = _dot_nt(lhs_s[u], kb_s[u])
        a_ab = jnp.where(before, g[:n2, n2:], 0.0)
        akr_s[u, :n2] = jnp.where(before, g[:n2, :n2], 0.0).astype(BF16)
        akr_s[u, n2:] = jnp.where(upto, g[n2:, :n2], 0.0).astype(BF16)
        arb_s[u] = jnp.where(upto, g[n2:, n2:], 0.0).astype(BF16)
        aab_s[u] = a_ab
        t_s[u] = eye - jnp.where(blk(2), a_ab, 0.0)

    s = 2
    while s < c_n:
        off = jnp.logical_and(blk(2 * s), jnp.logical_not(blk(s)))
        for u in units:
            x_s[u] = _dot(t_s[u].astype(BF16), jnp.where(off, aab_s[u], 0.0).astype(BF16)).astype(BF16)
        for u in units:
            t = t_s[u]
            t_s[u] = t - _dot(x_s[u], t.astype(BF16))
        s *= 2

    for u in units:
        av = _dot(akr_s[u], vs_s[u])
        xin_s[u, :, LANES:] = av[:n2].astype(BF16)
        yc_s[u] = av[n2:]

    for u in units:
        wu = _dot(t_s[u].astype(BF16), xin_s[u]).astype(BF16)
        q = _dot(arb_s[u], wu)
        bwu = _dot(bht_s[u], wu)
        qy_s[u] = (rt_s[u] - q[:, :LANES]).astype(BF16)
        yc_s[u] = yc_s[u] - q[:, LANES:]
        bw_s[u] = bwu[:, :LANES]
        n_s[u] = _dot(kht_s[u], vs_s[u]) - bwu[:, LANES:]

    for k in range(n_sub):
        g_c = gc_s[k]
        for p in pairs:
            u = k * n_pairs + p
            z0 = z_scr[p].astype(BF16)
            y_p = _dot(qy_s[u], z0) + yc_s[u]
            y_s[k, :, lanes_of(p)] = y_p[:c_n] + y_p[c_n:]
            m_z = eye * g_c[:, lanes_of(p)] - bw_s[u]
            z_scr[p] = _dot(m_z.astype(BF16), z0) + n_s[u]
    for k in range(n_sub):
        y_ref[0, 0, pl.ds(row_of(k), c_n), :] = y_s[k]


def _rw_scan(r, v, kk, lw, kd, be, l_ctx):
    bsz, l_all, rw = r.shape
    n_sub = RW_SUB
    c_n = RW_CHUNK
    rows = n_sub * c_n
    n_all = l_all // rows
    n_ctx = l_ctx // rows
    n_lat = n_all - n_ctx
    n_pairs = rw // LANES
    n2 = 2 * c_n
    vm = lambda nr, cols, dt: pltpu.VMEM((n_sub * n_pairs, nr, cols), dt)

    def chunk_of(bd, s):
        fwd = s
        bwd = jnp.where(s < n_ctx, n_ctx - 1 - s, n_all + n_ctx - 1 - s)
        return jnp.where(bd % 2 == 0, fwd, bwd)

    def out_chunk(bd, s):
        c = chunk_of(bd, s)
        edge = jnp.where(bd % 2 == 0, 0, n_lat - 1)
        return jnp.where(s < n_ctx, edge, c - n_ctx)

    shared = pl.BlockSpec((1, rows, rw), lambda bd, s: (bd // 2, chunk_of(bd, s), 0))
    per_dir = pl.BlockSpec((1, 1, rows, rw), lambda bd, s: (bd % 2, bd // 2, chunk_of(bd, s), 0))
    return pl.pallas_call(
        functools.partial(_rw_chunk_kernel, n_pairs=n_pairs, n_sub=n_sub),
        out_shape=jax.ShapeDtypeStruct((2, bsz, n_lat * rows, rw), F32),
        grid=(2 * bsz, n_all),
        in_specs=[shared, shared, shared, per_dir, per_dir, per_dir],
        out_specs=pl.BlockSpec((1, 1, rows, rw), lambda bd, s: (bd % 2, bd // 2, out_chunk(bd, s), 0)),
        scratch_shapes=[pltpu.VMEM((n_pairs, n2, n2), F32),
                        pltpu.VMEM((n_sub, 1, rw), F32),
                        vm(2 * n2, n2, BF16),
                        vm(2 * n2, n2, BF16),
                        vm(n2, n2, BF16),
                        vm(n2, n2, BF16),
                        vm(n2, n2, BF16),
                        vm(n2, n2, F32),
                        vm(n2, n2, F32),
                        vm(2 * n2, n2, BF16),
                        vm(n2, n2, BF16),
                        vm(n2, n2, F32),
                        vm(n2, n2, BF16),
                        vm(n2, 2 * n2, BF16),
                        vm(n2, n2, F32),
                        vm(n2, n2, BF16),
                        vm(n2, n2, F32),
                        vm(n2, n2, F32),
                        pltpu.VMEM((n_sub, c_n, rw), F32)],
        compiler_params=_cparams(("parallel", "arbitrary")),
        name="rwkv_scan",
    )(r, v, kk, lw, kd, be)


def _rw_step_kernel(rf_ref, rb_ref, vf_ref, vb_ref, kkf_ref, kkb_ref, lwf_ref, lwb_ref, kdf_ref, kdb_ref,
                    bef_ref, beb_ref, yf_ref, yb_ref,
                    z_scr, gc_s, lhs_s, kb_s, kht_s, bht_s, vs_s, rt_s, aab_s, akr_s, arb_s, t_s, x_s, xin_s,
                    yc_s, wu_s, qm_s, yn_s, *, n_pairs, bsz):
    c_n = RW_CHUNK
    n2 = 2 * c_n

    lanes_of = lambda p: slice(p * LANES, (p + 1) * LANES)
    unit = lambda d, b, p: (d * bsz + b) * n_pairs + p
    units = [(d, b, p) for d in range(2) for b in range(bsz) for p in range(n_pairs)]

    @pl.when(pl.program_id(0) == 0)
    def _():
        z_scr[...] = jnp.zeros_like(z_scr)
        qm_s[...] = jnp.zeros_like(qm_s)
        yn_s[...] = jnp.zeros_like(yn_s)

    for d, b, p in units:
        u = unit(d, b, p)
        y_ref = yf_ref if d == 0 else yb_ref
        yz = _dot(qm_s[u], z_scr[u].astype(BF16)) + yn_s[u]
        y_ref[b, :, lanes_of(p)] = yz[:c_n] + yz[c_n:n2]
        z_scr[u] = yz[n2:]

    ri = lax.broadcasted_iota(jnp.int32, (c_n, c_n), 0)
    ci = lax.broadcasted_iota(jnp.int32, (c_n, c_n), 1)
    r2 = lax.broadcasted_iota(jnp.int32, (n2, n2), 0)
    c2 = lax.broadcasted_iota(jnp.int32, (n2, n2), 1)
    t2 = r2 % c_n
    i2 = c2 % c_n
    same_head = (r2 // c_n) == (c2 // c_n)
    diag = r2 == c2
    eye = diag.astype(F32)
    head0 = lax.broadcasted_iota(jnp.int32, (1, LANES), 1) < RW_HEAD
    tri = [(ri >= ci).astype(F32), (ri <= ci).astype(F32)]
    before = [jnp.logical_and(same_head, i2 < t2), jnp.logical_and(same_head, i2 > t2)]
    upto = [jnp.logical_or(m, diag) for m in before]

    def blk(s):
        return (r2 // s) == (c2 // s)

    srcs = [(rf_ref, vf_ref, kkf_ref, lwf_ref, kdf_ref, bef_ref), (rb_ref, vb_ref, kkb_ref, lwb_ref, kdb_ref, beb_ref)]

    for d in range(2):
        r_ref, v_ref, kk_ref, lw_ref, kd_ref, be_ref = srcs[d]
        for b in range(bsz):
            lw = lw_ref[0, b]
            cum = _dot(tri[d], lw, HIGHEST)
            tot = jnp.sum(lw, axis=0, keepdims=True)
            g_inv = jnp.exp(-cum)
            g_hat = jnp.exp(tot - cum)
            kd = kd_ref[0, b].astype(F32)
            be = be_ref[0, b].astype(F32)
            at = kk_ref[b].astype(F32) * jnp.exp(cum - lw)
            rt = r_ref[b].astype(F32) * jnp.exp(cum)
            kt = kd * g_inv
            bt = be * g_inv
            kh = kd * g_hat
            bh = be * g_hat
            gc_s[d * bsz + b] = jnp.exp(tot)
            v = v_ref[b].astype(F32)
            for p in range(n_pairs):
                u = unit(d, b, p)
                st = lambda x: _stack_heads(x[:, lanes_of(p)], head0)
                rt_p = st(rt)
                at_p = st(at).astype(BF16)
                lhs_s[u, :n2] = at_p
                lhs_s[u, n2:] = rt_p.astype(BF16)
                xin_s[u, :, :LANES] = at_p
                rt_s[u] = rt_p
                kb_s[u, :n2] = st(kt).astype(BF16)
                kb_s[u, n2:] = st(bt).astype(BF16)
                kht_s[u] = st(kh).T.astype(BF16)
                bht_s[u] = st(bh).T.astype(BF16)
                vs_s[u] = st(v).astype(BF16)

    for d, b, p in units:
        u = unit(d, b, p)
        g = _dot_nt(lhs_s[u], kb_s[u])
        a_ab = jnp.where(before[d], g[:n2, n2:], 0.0)
        akr_s[u, :n2] = jnp.where(before[d], g[:n2, :n2], 0.0).astype(BF16)
        akr_s[u, n2:] = jnp.where(upto[d], g[n2:, :n2], 0.0).astype(BF16)
        arb_s[u] = jnp.where(upto[d], g[n2:, n2:], 0.0).astype(BF16)
        aab_s[u] = a_ab
        t_s[u] = eye - jnp.where(blk(2), a_ab, 0.0)

    n_units = len(units)
    s = 2
    while s < c_n:
        off = jnp.logical_and(blk(2 * s), jnp.logical_not(blk(s)))
        for u in range(n_units):
            x_s[u] = _dot(t_s[u].astype(BF16), jnp.where(off, aab_s[u], 0.0).astype(BF16)).astype(BF16)
        for u in range(n_units):
            t = t_s[u]
            t_s[u] = t - _dot(x_s[u], t.astype(BF16))
        s *= 2

    for u in range(n_units):
        av = _dot(akr_s[u], vs_s[u])
        xin_s[u, :, LANES:] = av[:n2].astype(BF16)
        yc_s[u] = av[n2:]

    for u in range(n_units):
        wu_s[u] = _dot(t_s[u].astype(BF16), xin_s[u]).astype(BF16)
    for d, b, p in units:
        u = unit(d, b, p)
        wu = wu_s[u]
        q = _dot(arb_s[u], wu)
        bwu = _dot(bht_s[u], wu)
        qm_s[u, :n2] = (rt_s[u] - q[:, :LANES]).astype(BF16)
        qm_s[u, n2:] = (eye * gc_s[d * bsz + b][:, lanes_of(p)] - bwu[:, :LANES]).astype(BF16)
        yn_s[u, :n2] = yc_s[u] - q[:, LANES:]
        yn_s[u, n2:] = _dot(kht_s[u], vs_s[u]) - bwu[:, LANES:]


def _rw_scan2(r, v, kk, lw, kd, be, l_ctx):
    bsz, l_all, rw = r.shape
    c_n = RW_CHUNK
    n_all = l_all // c_n
    n_ctx = l_ctx // c_n
    n_lat = n_all - n_ctx
    n_pairs = rw // LANES
    n_units = 2 * bsz * n_pairs
    n2 = 2 * c_n
    vm = lambda nr, cols, dt: pltpu.VMEM((n_units, nr, cols), dt)

    clamp = lambda s: jnp.minimum(s, n_all - 1)
    chunk_f = lambda s: clamp(s)
    chunk_b = lambda s: jnp.where(clamp(s) < n_ctx, n_ctx - 1 - clamp(s), n_all + n_ctx - 1 - clamp(s))
    prev = lambda s: jnp.maximum(s - 1, 0)
    out_f = lambda s: jnp.maximum(prev(s) - n_ctx, 0)
    out_b = lambda s: jnp.where(prev(s) < n_ctx, n_lat - 1, chunk_b(prev(s)) - n_ctx)

    sh_f = pl.BlockSpec((bsz, c_n, rw), lambda s: (0, chunk_f(s), 0))
    sh_b = pl.BlockSpec((bsz, c_n, rw), lambda s: (0, chunk_b(s), 0))
    pd_f = pl.BlockSpec((1, bsz, c_n, rw), lambda s: (0, 0, chunk_f(s), 0))
    pd_b = pl.BlockSpec((1, bsz, c_n, rw), lambda s: (1, 0, chunk_b(s), 0))
    y_shape = jax.ShapeDtypeStruct((bsz, n_lat * c_n, rw), F32)
    return pl.pallas_call(
        functools.partial(_rw_step_kernel, n_pairs=n_pairs, bsz=bsz),
        out_shape=(y_shape, y_shape),
        grid=(n_all + 1,),
        in_specs=[sh_f, sh_b, sh_f, sh_b, sh_f, sh_b, pd_f, pd_b, pd_f, pd_b, pd_f, pd_b],
        out_specs=(pl.BlockSpec((bsz, c_n, rw), lambda s: (0, out_f(s), 0)),
                   pl.BlockSpec((bsz, c_n, rw), lambda s: (0, out_b(s), 0))),
        scratch_shapes=[vm(n2, n2, F32),
                        pltpu.VMEM((2 * bsz, 1, rw), F32),
                        vm(2 * n2, n2, BF16),
                        vm(2 * n2, n2, BF16),
                        vm(n2, n2, BF16),
                        vm(n2, n2, BF16),
                        vm(n2, n2, BF16),
                        vm(n2, n2, F32),
                        vm(n2, n2, F32),
                        vm(2 * n2, n2, BF16),
                        vm(n2, n2, BF16),
                        vm(n2, n2, F32),
                        vm(n2, n2, BF16),
                        vm(n2, 2 * n2, BF16),
                        vm(n2, n2, F32),
                        vm(n2, 2 * n2, BF16),
                        vm(2 * n2, n2, BF16),
                        vm(2 * n2, n2, F32)],
        compiler_params=_cparams(("arbitrary",)),
        name="rwkv_scan",
    )(r, r, v, v, kk, kk, lw, lw, kd, kd, be, be)


def _s5_glu_kernel(y_ref, wa_ref, wb_ref, o_ref, h_scr):
    @pl.when(pl.program_id(1) == 0)
    def _():
        for jb in range(y_ref.shape[0]):
            h_scr[:, jb * LANES:(jb + 1) * LANES] = _gelu_tanh(y_ref[jb]).astype(BF16)

    h = h_scr[...]
    o_ref[...] = (_dot(h, wa_ref[...]) * _sigmoid(_dot(h, wb_ref[...]))).astype(o_ref.dtype)


def _s5_glu(y_blk, w, *, tm, tn):
    nb, m, _ = y_blk.shape
    k = nb * LANES
    n = w.shape[1] // 2
    nj = n // tn
    return pl.pallas_call(
        _s5_glu_kernel,
        out_shape=jax.ShapeDtypeStruct((m, n), BF16),
        grid=(m // tm, nj),
        in_specs=[pl.BlockSpec((nb, tm, LANES), lambda i, j: (0, i, 0)),
                  pl.BlockSpec((k, tn), lambda i, j: (0, j)),
                  pl.BlockSpec((k, tn), lambda i, j: (0, j + nj))],
        out_specs=pl.BlockSpec((tm, tn), lambda i, j: (i, j)),
        scratch_shapes=[pltpu.VMEM((tm, k), BF16)],
        compiler_params=_cparams(("parallel", "arbitrary")),
        name="s5_glu",
    )(y_blk, w, w)


def _rw_merge_kernel(yf_ref, yb_ref, bonus_ref, g_ref, lnw_ref, lnb_ref, seg_ref, segt_ref,
                     ga_ref, gb_ref, s5_ref, w_ref, o_ref, h_scr):
    @pl.when(pl.program_id(1) == 0)
    def _():
        seg = seg_ref[...]
        segt = segt_ref[...]
        inv_n = 1.0 / RW_HEAD

        def head_mean(t):
            return _dot_hilo(_dot_hilo(t, seg), segt) * inv_n

        y = yf_ref[...] + yb_ref[...]
        dy = y - head_mean(y)
        var = head_mean(dy * dy)
        y = dy * lax.rsqrt(var + GN_EPS) * lnw_ref[...] + lnb_ref[...] + bonus_ref[...].astype(F32)
        h_scr[...] = (y * g_ref[...].astype(F32)).astype(BF16)

    rw_out = _dot(h_scr[...], w_ref[...])
    merged = ga_ref[...].astype(F32) * s5_ref[...].astype(F32) + gb_ref[...].astype(F32) * rw_out
    o_ref[...] = merged.astype(o_ref.dtype)


def _rw_merge(y_f, y_b, bonus, g, ln_w, ln_b, seg, segt, gates, s5_out, w_proj, *, tm, tn):
    m, rw = y_f.shape
    n = w_proj.shape[1]
    nj = n // tn
    full = lambda shape: pl.BlockSpec(shape, lambda i, j: (0,) * len(shape))
    return pl.pallas_call(
        _rw_merge_kernel,
        out_shape=jax.ShapeDtypeStruct((m, n), BF16),
        grid=(m // tm, nj),
        in_specs=[pl.BlockSpec((tm, rw), lambda i, j: (i, 0)),
                  pl.BlockSpec((tm, rw), lambda i, j: (i, 0)),
                  pl.BlockSpec((tm, rw), lambda i, j: (i, 0)),
                  pl.BlockSpec((tm, rw), lambda i, j: (i, 0)),
                  full((1, rw)), full((1, rw)), full(seg.shape), full(segt.shape),
                  pl.BlockSpec((tm, tn), lambda i, j: (i, j)),
                  pl.BlockSpec((tm, tn), lambda i, j: (i, j + nj)),
                  pl.BlockSpec((tm, tn), lambda i, j: (i, j)),
                  pl.BlockSpec((rw, tn), lambda i, j: (0, j))],
        out_specs=pl.BlockSpec((tm, tn), lambda i, j: (i, j)),
        scratch_shapes=[pltpu.VMEM((tm, rw), BF16)],
        compiler_params=_cparams(("parallel", "arbitrary")),
        name="rwkv_merge",
    )(y_f, y_b, bonus, g, ln_w.reshape(1, rw), ln_b.reshape(1, rw), seg, segt,
      gates, gates, s5_out, w_proj)


def _resid_mm_kernel(a_ref, w_ref, x_ref, g_ref, o_ref):
    o_ref[...] = x_ref[...] + g_ref[0] * _dot(a_ref[...], w_ref[...])


def _resid_matmul(a, w, x2, g_tab, mod_row_of_block, *, tm, tn):
    m, k = a.shape
    n = w.shape[1]
    return pl.pallas_call(
        _resid_mm_kernel,
        out_shape=jax.ShapeDtypeStruct((m, n), F32),
        grid=(m // tm, n // tn),
        in_specs=[pl.BlockSpec((tm, k), lambda i, j: (i, 0)),
                  pl.BlockSpec((k, tn), lambda i, j: (0, j)),
                  pl.BlockSpec((tm, tn), lambda i, j: (i, j)),
                  pl.BlockSpec((1, 1, tn), lambda i, j: (mod_row_of_block(i), 0, j))],
        out_specs=pl.BlockSpec((tm, tn), lambda i, j: (i, j)),
        compiler_params=_cparams(("parallel", "arbitrary")),
        name="out_proj",
    )(a, w, x2, g_tab)


def _ffn_down_kernel(a_ref, w_ref, x_ref, g_ref, nf_ref, o_ref, acc_ref):
    kk = pl.program_id(1)

    @pl.when(kk == 0)
    def _():
        acc_ref[...] = jnp.zeros_like(acc_ref)

    acc_ref[...] += _dot(a_ref[...], w_ref[...])

    @pl.when(kk == pl.num_programs(1) - 1)
    def _():
        h = x_ref[...] + g_ref[0] * acc_ref[...]
        ms = jnp.mean(h * h, axis=-1, keepdims=True)
        o_ref[...] = h * lax.rsqrt(ms + NORM_EPS) * nf_ref[...]


def _ffn_down(a, w, x2, g_tab, mod_row_of_block, norm_f, *, tm, tk):
    m, k = a.shape
    n = w.shape[1]
    return pl.pallas_call(
        _ffn_down_kernel,
        out_shape=jax.ShapeDtypeStruct((m, n), F32),
        grid=(m // tm, k // tk),
        in_specs=[pl.BlockSpec((tm, tk), lambda i, kk: (i, kk)),
                  pl.BlockSpec((tk, n), lambda i, kk: (kk, 0)),
                  pl.BlockSpec((tm, n), lambda i, kk: (i, 0)),
                  pl.BlockSpec((1, 1, n), lambda i, kk: (mod_row_of_block(i), 0, 0)),
                  pl.BlockSpec((1, n), lambda i, kk: (0, 0))],
        out_specs=pl.BlockSpec((tm, n), lambda i, kk: (i, 0)),
        scratch_shapes=[pltpu.VMEM((tm, n), F32)],
        compiler_params=_cparams(("parallel", "arbitrary")),
        name="ffn_down",
    )(a, w, x2, g_tab, norm_f.reshape(1, n))


def kernel(x, c, ctx, c_ctx, ada_w, ada_b, norm1_w, w_in, rw_mu, s5_a_re, s5_a_im, s5_log_dt, s5_b_re, s5_b_im, s5_c_re, s5_c_im, s5_d, s5_glu_w, rw_w0, rw_w2, rw_a0, rw_a2, rw_g2, rw_k_k, rw_k_a, rw_r_k, rw_ln_w, rw_ln_b, rw_proj, w_o, norm2_w, ffn_w13, ffn_w2, norm_f):
    assert ada_w.shape[0] == 1, "single-layer block"
    bsz, l_lat, d = x.shape
    l_ctx = ctx.shape[1]
    l_all = l_ctx + l_lat
    s5w = s5_d.shape[1] * s5_d.shape[2]
    rw = rw_g2.shape[2]
    shift_cols = rw_mu.shape[1]
    d_ff = ffn_w2.shape[1]
    n_heads = rw // RW_HEAD

    c_rows = jnp.concatenate([c, c_ctx[None], jnp.zeros((8 - bsz - 1, d), F32)], axis=0)
    mod = _modulation(c_rows, ada_w[0], ada_b[0])
    tab = lambda k: mod[:, k * d:(k + 1) * d].reshape(8, 1, d)
    sh1, sc1, g1, sh2, sc2, g2 = (tab(k) for k in range(N_MOD))
    ctx_row = bsz

    tm_in = 512
    m_lat = bsz * l_lat
    x2 = x.reshape(m_lat, d)
    x_all = jnp.concatenate([x2, ctx.reshape(bsz * l_ctx, d)], axis=0)
    n_mix = s5w + shift_cols
    m_all = m_lat + bsz * l_ctx
    lat_row = lambda t: (lambda i: i // (l_lat // t))

    def mix_mod_row(i):
        return jnp.where(i >= m_lat // tm_in, ctx_row, i // (l_lat // tm_in))

    h_all = _lnmod(x_all, norm1_w[0], sh1, sc1, mix_mod_row, tm=tm_in)
    tm_all = m_all // 8
    u_blk = _wres_matmul(h_all, w_in[0], 0, s5w, rows=m_all, tm=tm_all, tn=s5w,
                         out_dtype=F32, epilogue="lane_blocks", name="in_proj_s5")
    z_rw = _wres_matmul(h_all, w_in[0], s5w, shift_cols, rows=m_all, tm=tm_all, tn=shift_cols // 3,
                        out_dtype=BF16, name="in_proj_rw")
    tm = 1024
    gates = _wres_matmul(h_all, w_in[0], n_mix, w_in.shape[2] - n_mix, rows=m_lat, tm=tm, tn=1024,
                         out_dtype=BF16, epilogue="sigmoid", name="in_proj_gates")

    y_blk = _s5_branch_blocked(u_blk, bsz, l_ctx, l_lat, s5_a_re[0], s5_a_im[0], s5_log_dt[0],
                               s5_b_re[0], s5_b_im[0], s5_c_re[0], s5_c_im[0], s5_d[0])
    s5_out = _s5_glu(y_blk, s5_glu_w[0].astype(BF16), tm=tm, tn=1024)

    lora = rw_w2.shape[2]
    zl = jnp.zeros((lora, rw), F32)
    w2bd = jnp.concatenate([jnp.concatenate([rw_w2[0, 0], zl], axis=1),
                            jnp.concatenate([zl, rw_w2[0, 1]], axis=1)], axis=0)
    a2bd = jnp.concatenate([jnp.concatenate([rw_a2[0, 0], zl], axis=1),
                            jnp.concatenate([zl, rw_a2[0, 1]], axis=1)], axis=0)
    head_of = jnp.arange(rw) // RW_HEAD
    seg = (head_of[:, None] == jnp.arange(LANES)[None, :]).astype(BF16)
    segt = seg.T
    r, v, kk, g, bonus, lw, kd, be = _rw_prep(
        z_rw, bsz, l_ctx, l_lat, rw_mu[0], w2bd, a2bd, rw_g2[0], rw_w0[0].reshape(1, 2 * rw),
        rw_a0[0].reshape(1, 2 * rw), rw_k_k[0], rw_k_a[0], rw_r_k[0].reshape(rw), seg, segt)
    y_f, y_b = _rw_scan2(r, v, kk, lw, kd, be, l_ctx)

    merged = _rw_merge(y_f.reshape(m_lat, rw), y_b.reshape(m_lat, rw), bonus.reshape(m_lat, rw), g.reshape(m_lat, rw),
                       rw_ln_w[0], rw_ln_b[0], seg, segt, gates, s5_out, rw_proj[0].astype(BF16),
                       tm=512, tn=1024)
    h1 = _resid_matmul(merged, w_o[0].astype(BF16), x2, g1, lat_row(tm), tm=tm, tn=1024)

    act = _lnmod_swiglu(h1, norm2_w[0], sh2, sc2, lat_row(tm), ffn_w13[0].astype(BF16), d_ff,
                        tm=tm, tn=512, name="ffn_up")
    tm_dn = 512
    out = _ffn_down(act, ffn_w2[0].astype(BF16), h1, g2, lat_row(tm_dn), norm_f, tm=tm_dn, tk=d_ff // 4)
    return out.reshape(bsz, l_lat, d)
```

```python
import functools
import math

import jax
import jax.numpy as jnp
from jax import lax
from jax.experimental import pallas as pl
from jax.experimental.pallas import tpu as pltpu

F32 = jnp.float32
BF16 = jnp.bfloat16
HIGHEST = lax.Precision.HIGHEST

D_MODEL = 2048
N_MOD = 6
NORM_EPS = 1e-6
GN_EPS = 64e-5
GRID_W = 64
S5_GROUP = 16
S5_STATE = 64
S5_CHUNK = 16
RW_HEAD = 64
RW_CHUNK = 64
RW_SUB = 2
LANES = 128
VMEM_LIMIT = 48 * 1024 * 1024


def _cparams(sem):
    return pltpu.CompilerParams(dimension_semantics=sem, vmem_limit_bytes=VMEM_LIMIT)


def _operands(a, b, precision):
    if precision == "bf16":
        return a.astype(BF16), b.astype(BF16), None
    return a, b, precision


def _dot(a, b, precision=None):
    a, b, precision = _operands(a, b, precision)
    return jnp.dot(a, b, preferred_element_type=F32, precision=precision)


def _dot_nt(a, b, precision=None):
    a, b, precision = _operands(a, b, precision)
    return lax.dot_general(a, b, (((1,), (1,)), ((), ())), preferred_element_type=F32, precision=precision)


def _dot_hilo(a, ind):
    hi = a.astype(BF16)
    lo = (a - hi.astype(F32)).astype(BF16)
    return _dot(hi, ind) + _dot(lo, ind)


def _sigmoid(x):
    return 1.0 / (1.0 + jnp.exp(-x))


def _silu(x):
    return x * _sigmoid(x)


def _gelu_tanh(x):
    c = math.sqrt(2.0 / math.pi)
    return 0.5 * x * (1.0 + jnp.tanh(c * (x + 0.044715 * (x * x * x))))


def _softplus(x):
    return jnp.maximum(x, 0.0) + jnp.log(1.0 + jnp.exp(-jnp.abs(x)))


def _mod_kernel(c_ref, w_ref, b_ref, o_ref):
    o_ref[...] = _dot(_silu(c_ref[...]), w_ref[...], HIGHEST) + b_ref[...]


def _modulation(c_rows, ada_w, ada_b):
    m, d = c_rows.shape
    n = ada_w.shape[1]
    tn = 1024
    return pl.pallas_call(
        _mod_kernel,
        out_shape=jax.ShapeDtypeStruct((m, n), F32),
        grid=(n // tn,),
        in_specs=[pl.BlockSpec((m, d), lambda j: (0, 0)),
                  pl.BlockSpec((d, tn), lambda j: (0, j)),
                  pl.BlockSpec((1, tn), lambda j: (0, j))],
        out_specs=pl.BlockSpec((m, tn), lambda j: (0, j)),
        compiler_params=_cparams(("arbitrary",)),
        name="modulation",
    )(c_rows, ada_w, ada_b.reshape(1, n))


def _lnmod_rows(x, nw, sh, sc):
    ms = jnp.mean(x * x, axis=-1, keepdims=True)
    y = x * lax.rsqrt(ms + NORM_EPS) * nw
    return y * (1.0 + sc) + sh


def _lnmod_mm_kernel(x_ref, nw_ref, sh_ref, sc_ref, w_ref, o_ref, h_scr, *, epilogue):
    @pl.when(pl.program_id(1) == 0)
    def _():
        h_scr[...] = _lnmod_rows(x_ref[...], nw_ref[...], sh_ref[0], sc_ref[0]).astype(BF16)

    z = _dot(h_scr[...], w_ref[...])
    if epilogue == "sigmoid":
        z = _sigmoid(z)
    if epilogue == "lane_blocks":
        for jb in range(o_ref.shape[0]):
            o_ref[jb] = z[:, jb * LANES:(jb + 1) * LANES].astype(o_ref.dtype)
    else:
        o_ref[...] = z.astype(o_ref.dtype)


def _lnmod_swiglu_kernel(x_ref, nw_ref, sh_ref, sc_ref, w1_ref, w3_ref, o_ref, h_scr):
    @pl.when(pl.program_id(1) == 0)
    def _():
        h_scr[...] = _lnmod_rows(x_ref[...], nw_ref[...], sh_ref[0], sc_ref[0]).astype(BF16)

    h = h_scr[...]
    o_ref[...] = (_silu(_dot(h, w1_ref[...])) * _dot(h, w3_ref[...])).astype(o_ref.dtype)


def _lnmod_matmul(x2, nw, sh_tab, sc_tab, mod_row_of_block, w, *, tm, tn, out_dtype, epilogue=None, name):
    m, d = x2.shape
    n = w.shape[1]
    mod_map = lambda i, j: (mod_row_of_block(i), 0, 0)
    if epilogue == "lane_blocks":
        assert tn == n
        out_shape = jax.ShapeDtypeStruct((n // LANES, m, LANES), out_dtype)
        out_spec = pl.BlockSpec((n // LANES, tm, LANES), lambda i, j: (0, i, 0))
    else:
        out_shape = jax.ShapeDtypeStruct((m, n), out_dtype)
        out_spec = pl.BlockSpec((tm, tn), lambda i, j: (i, j))
    return pl.pallas_call(
        functools.partial(_lnmod_mm_kernel, epilogue=epilogue),
        out_shape=out_shape,
        grid=(m // tm, n // tn),
        in_specs=[pl.BlockSpec((tm, d), lambda i, j: (i, 0)),
                  pl.BlockSpec((1, d), lambda i, j: (0, 0)),
                  pl.BlockSpec((1, 1, d), mod_map),
                  pl.BlockSpec((1, 1, d), mod_map),
                  pl.BlockSpec((d, tn), lambda i, j: (0, j))],
        out_specs=out_spec,
        scratch_shapes=[pltpu.VMEM((tm, d), BF16)],
        compiler_params=_cparams(("parallel", "arbitrary")),
        name=name,
    )(x2, nw.reshape(1, d), sh_tab, sc_tab, w)


def _lnmod_kernel(x_ref, c_ref, nw_ref, sh_ref, sc_ref, o_ref, *, n_lat_blocks):
    rows = jnp.where(pl.program_id(0) < n_lat_blocks, x_ref[...], c_ref[...])
    o_ref[...] = _lnmod_rows(rows, nw_ref[...], sh_ref[0], sc_ref[0]).astype(o_ref.dtype)


def _lnmod(x2, c2, nw, sh_tab, sc_tab, mod_row_of_block, *, tm):
    m, d = x2.shape
    mc = c2.shape[0]
    nx, nc = m // tm, mc // tm
    mod_map = lambda i: (mod_row_of_block(i), 0, 0)
    return pl.pallas_call(
        functools.partial(_lnmod_kernel, n_lat_blocks=nx),
        out_shape=jax.ShapeDtypeStruct((m + mc, d), BF16),
        grid=(nx + nc,),
        in_specs=[pl.BlockSpec((tm, d), lambda i: (jnp.minimum(i, nx - 1), 0)),
                  pl.BlockSpec((tm, d), lambda i: (jnp.maximum(i - nx, 0), 0)),
                  pl.BlockSpec((1, d), lambda i: (0, 0)),
                  pl.BlockSpec((1, 1, d), mod_map),
                  pl.BlockSpec((1, 1, d), mod_map)],
        out_specs=pl.BlockSpec((tm, d), lambda i: (i, 0)),
        compiler_params=_cparams(("parallel",)),
        name="lnmod",
    )(x2, c2, nw.reshape(1, d), sh_tab, sc_tab)


def _wres_mm_kernel(a_ref, w_ref, o_ref, w_scr, *, epilogue):
    @pl.when(pl.program_id(1) == 0)
    def _():
        w_scr[...] = w_ref[...].astype(BF16)

    z = _dot(a_ref[...], w_scr[...])
    if epilogue == "sigmoid":
        z = _sigmoid(z)
    if epilogue == "lane_blocks":
        for jb in range(o_ref.shape[0]):
            o_ref[jb] = z[:, jb * LANES:(jb + 1) * LANES].astype(o_ref.dtype)
    else:
        o_ref[...] = z.astype(o_ref.dtype)


def _wres_matmul(a, w, col0, n, *, rows, tm, tn, out_dtype, epilogue=None, name):
    k = a.shape[1]
    assert col0 % LANES == 0 and n % tn == 0 and rows % tm == 0
    if epilogue == "lane_blocks":
        assert tn == n
        out_shape = jax.ShapeDtypeStruct((n // LANES, rows, LANES), out_dtype)
        out_spec = pl.BlockSpec((n // LANES, tm, LANES), lambda j, i: (0, i, 0))
    else:
        out_shape = jax.ShapeDtypeStruct((rows, n), out_dtype)
        out_spec = pl.BlockSpec((tm, tn), lambda j, i: (i, j))
    return pl.pallas_call(
        functools.partial(_wres_mm_kernel, epilogue=epilogue),
        out_shape=out_shape,
        grid=(n // tn, rows // tm),
        in_specs=[pl.BlockSpec((tm, k), lambda j, i: (i, 0)),
                  pl.BlockSpec((pl.Element(k), pl.Element(tn)), lambda j, i: (0, pl.multiple_of(col0 + j * tn, LANES)))],
        out_specs=out_spec,
        scratch_shapes=[pltpu.VMEM((k, tn), BF16)],
        compiler_params=_cparams(("arbitrary", "arbitrary")),
        name=name,
    )(a, w)


def _lnmod_swiglu(x2, nw, sh_tab, sc_tab, mod_row_of_block, w13, d_ff, *, tm, tn, name):
    m, d = x2.shape
    nj = d_ff // tn
    mod_map = lambda i, j: (mod_row_of_block(i), 0, 0)
    return pl.pallas_call(
        _lnmod_swiglu_kernel,
        out_shape=jax.ShapeDtypeStruct((m, d_ff), BF16),
        grid=(m // tm, nj),
        in_specs=[pl.BlockSpec((tm, d), lambda i, j: (i, 0)),
                  pl.BlockSpec((1, d), lambda i, j: (0, 0)),
                  pl.BlockSpec((1, 1, d), mod_map),
                  pl.BlockSpec((1, 1, d), mod_map),
                  pl.BlockSpec((d, tn), lambda i, j: (0, j)),
                  pl.BlockSpec((d, tn), lambda i, j: (0, j + nj))],
        out_specs=pl.BlockSpec((tm, tn), lambda i, j: (i, j)),
        scratch_shapes=[pltpu.VMEM((tm, d), BF16)],
        compiler_params=_cparams(("parallel", "arbitrary")),
        name=name,
    )(x2, nw.reshape(1, d), sh_tab, sc_tab, w13, w13)


def _s5_param_kernel(are_ref, aim_ref, ldt_ref, bre_ref, bim_ref, cre_ref, cim_ref,
                     e_ref, cs_ref, kt_ref, a16_ref):
    t_n, hg, p_n = S5_CHUNK, S5_GROUP, S5_STATE
    tau = lax.broadcasted_iota(jnp.int32, (t_n, 1, p_n), 0).astype(F32)
    for d in range(2):
        a_re = are_ref[0, d:d + 1, :]
        a_im = aim_ref[0, d:d + 1, :]
        dt = jnp.exp(ldt_ref[0, d:d + 1, :])
        lam = a_re * dt
        th = a_im * dt
        er = jnp.exp(lam)
        ab_re = er * jnp.cos(th)
        ab_im = er * jnp.sin(th)
        den = a_re * a_re + a_im * a_im
        x_re = ab_re - 1.0
        co_re = (x_re * a_re + ab_im * a_im) / den
        co_im = (ab_im * a_re - x_re * a_im) / den
        bt_re = bre_ref[0, d]
        bt_im = bim_ref[0, d]
        bb_re = co_re * bt_re - co_im * bt_im
        bb_im = co_re * bt_im + co_im * bt_re
        c_re = cre_ref[0, d]
        c_im = cim_ref[0, d]

        def power(tv):
            mag = jnp.exp(tv * lam)
            return mag * jnp.cos(tv * th), mag * jnp.sin(tv * th)

        pw_re, pw_im = power(tau)
        cp_re = (c_re[None] * pw_re - c_im[None] * pw_im).reshape(t_n * hg, p_n)
        cp_im = (c_re[None] * pw_im + c_im[None] * pw_re).reshape(t_n * hg, p_n)
        kt_ref[0, d] = _dot_nt(bb_re, cp_re, HIGHEST) - _dot_nt(bb_im, cp_im, HIGHEST)

        te = (t_n - 1.0 - tau) if d == 0 else tau
        pe_re, pe_im = power(te)
        e_ref[0, d, :, 0:p_n] = (pe_re * bb_re[None] - pe_im * bb_im[None]).reshape(t_n * hg, p_n)
        e_ref[0, d, :, p_n:2 * p_n] = (pe_re * bb_im[None] + pe_im * bb_re[None]).reshape(t_n * hg, p_n)

        tc = (tau + 1.0) if d == 0 else (t_n - tau)
        pc_re, pc_im = power(tc)
        cs_ref[0, d, :, 0:p_n] = (c_re[None] * pc_re - c_im[None] * pc_im).reshape(t_n * hg, p_n)
        cs_ref[0, d, :, p_n:2 * p_n] = -(c_re[None] * pc_im + c_im[None] * pc_re).reshape(t_n * hg, p_n)

        mag16 = jnp.exp(float(t_n) * lam)
        a16_ref[0, d, 0:1, :] = mag16 * jnp.cos(float(t_n) * th)
        a16_ref[0, d, 1:2, :] = mag16 * jnp.sin(float(t_n) * th)


def _s5_params(a_re, a_im, log_dt, b_re, b_im, c_re, c_im):
    g_n = a_re.shape[1]
    p_n, hg, t_n = S5_STATE, S5_GROUP, S5_CHUNK
    tr = lambda a: jnp.swapaxes(a, 0, 1)
    ldt = jnp.broadcast_to(tr(log_dt)[:, :, None], (g_n, 2, p_n))
    spec3 = pl.BlockSpec((1, 2, p_n), lambda g: (g, 0, 0))
    spec4 = pl.BlockSpec((1, 2, hg, p_n), lambda g: (g, 0, 0, 0))
    th = t_n * hg
    return pl.pallas_call(
        _s5_param_kernel,
        out_shape=(jax.ShapeDtypeStruct((g_n, 2, th, 2 * p_n), F32),
                   jax.ShapeDtypeStruct((g_n, 2, th, 2 * p_n), F32),
                   jax.ShapeDtypeStruct((g_n, 2, hg, th), F32),
                   jax.ShapeDtypeStruct((g_n, 2, 2, p_n), F32)),
        grid=(g_n,),
        in_specs=[spec3, spec3, spec3, spec4, spec4, spec4, spec4],
        out_specs=(pl.BlockSpec((1, 2, th, 2 * p_n), lambda g: (g, 0, 0, 0)),
                   pl.BlockSpec((1, 2, th, 2 * p_n), lambda g: (g, 0, 0, 0)),
                   pl.BlockSpec((1, 2, hg, th), lambda g: (g, 0, 0, 0)),
                   pl.BlockSpec((1, 2, 2, p_n), lambda g: (g, 0, 0, 0))),
        compiler_params=_cparams(("parallel",)),
        name="s5_params",
    )(tr(a_re), tr(a_im), ldt,
      jnp.transpose(b_re, (1, 0, 3, 2)), jnp.transpose(b_im, (1, 0, 3, 2)), tr(c_re), tr(c_im))


def _s5_state_in_kernel(u_ref, e_ref, o_ref):
    e = e_ref[0]
    u = u_ref[0]
    o_ref[0, :, 0:LANES] = _dot(u, e[0], HIGHEST)
    o_ref[0, :, LANES:2 * LANES] = _dot(u, e[1], HIGHEST)


def _s5_state_inputs(u_g, e):
    g_n, rows, th = u_g.shape
    return pl.pallas_call(
        _s5_state_in_kernel,
        out_shape=jax.ShapeDtypeStruct((g_n, rows, 2 * LANES), F32),
        grid=(g_n,),
        in_specs=[pl.BlockSpec((1, rows, th), lambda g: (g, 0, 0)),
                  pl.BlockSpec((1, 2, th, LANES), lambda g: (g, 0, 0, 0))],
        out_specs=pl.BlockSpec((1, rows, 2 * LANES), lambda g: (g, 0, 0)),
        compiler_params=_cparams(("parallel",)),
        name="s5_state_inputs",
    )(u_g, e)


def _s5_scan_kernel(e_ref, a_ref, o_ref, *, n_ctx, n_all):
    zero = jnp.zeros(e_ref.shape[2:], F32)

    def step(pr, pi):
        ar, ai = a_ref[pr], a_ref[pi]

        def body(c, carry):
            sr, si = carry
            o_ref[pr, c] = sr
            o_ref[pi, c] = si
            return (ar * sr - ai * si + e_ref[pr, c], ar * si + ai * sr + e_ref[pi, c])
        return body

    lax.fori_loop(0, n_all, step(0, 1), (zero, zero))
    bwd = step(2, 3)
    carry = lax.fori_loop(0, n_ctx, lambda k, cy: bwd(n_ctx - 1 - k, cy), (zero, zero))
    lax.fori_loop(0, n_all - n_ctx, lambda k, cy: bwd(n_all - 1 - k, cy), carry)


def _s5_scan(e_planes, a_planes, n_ctx, n_all):
    _, rows, r_n, _ = e_planes.shape
    bsz = rows // n_all
    sub = 8
    return pl.pallas_call(
        functools.partial(_s5_scan_kernel, n_ctx=n_ctx, n_all=n_all),
        out_shape=jax.ShapeDtypeStruct(e_planes.shape, F32),
        grid=(bsz, r_n // sub),
        in_specs=[pl.BlockSpec((4, n_all, sub, LANES), lambda b, q: (0, b, q, 0)),
                  pl.BlockSpec((4, sub, LANES), lambda b, q: (0, q, 0))],
        out_specs=pl.BlockSpec((4, n_all, sub, LANES), lambda b, q: (0, b, q, 0)),
        compiler_params=_cparams(("parallel", "parallel")),
        name="s5_scan",
    )(e_planes, a_planes)


def _s5_apply_kernel(u_ref, s_ref, mf_ref, mb_ref, d_ref, cs_ref, o_ref, *, n_ctx):
    u = u_ref[0, n_ctx:, :]
    s = s_ref[0, n_ctx:, :]
    y = _dot(u, mf_ref[0] + mb_ref[0], HIGHEST) + u * d_ref[0]
    y = y + _dot_nt(s[:, 0:LANES], cs_ref[0, 0], HIGHEST) + _dot_nt(s[:, LANES:2 * LANES], cs_ref[0, 1], HIGHEST)
    o_ref[0, 0] = y


def _s5_apply(u_g, s_g, m_f, m_b, d_t, cs, n_ctx, n_all):
    g_n, rows, th = u_g.shape
    bsz = rows // n_all
    n_lat = n_all - n_ctx
    return pl.pallas_call(
        functools.partial(_s5_apply_kernel, n_ctx=n_ctx),
        out_shape=jax.ShapeDtypeStruct((g_n, bsz, n_lat, th), F32),
        grid=(g_n, bsz),
        in_specs=[pl.BlockSpec((1, n_all, th), lambda g, b: (g, b, 0)),
                  pl.BlockSpec((1, n_all, 2 * LANES), lambda g, b: (g, b, 0)),
                  pl.BlockSpec((1, th, th), lambda g, b: (g, 0, 0)),
                  pl.BlockSpec((1, th, th), lambda g, b: (g, 0, 0)),
                  pl.BlockSpec((1, 1, th), lambda g, b: (g, 0, 0)),
                  pl.BlockSpec((1, 2, th, LANES), lambda g, b: (g, 0, 0, 0))],
        out_specs=pl.BlockSpec((1, 1, n_lat, th), lambda g, b: (g, b, 0, 0)),
        compiler_params=_cparams(("parallel", "arbitrary")),
        name="s5_apply",
    )(u_g, s_g, m_f, m_b, d_t, cs)


def _s5_branch(u_all, n_ctx_tok, a_re, a_im, log_dt, b_re, b_im, c_re, c_im, s5_d):
    bsz, l_all, width = u_all.shape
    hg, t_n, p_n = S5_GROUP, S5_CHUNK, S5_STATE
    g_n = width // hg
    n_all = l_all // t_n
    n_ctx = n_ctx_tok // t_n
    th = t_n * hg
    e, cs, kt, a16 = _s5_params(a_re, a_im, log_dt, b_re, b_im, c_re, c_im)

    kt5 = kt.reshape(g_n, 2, hg, t_n, hg)
    ii = jnp.arange(t_n)[:, None]
    jj = jnp.arange(t_n)[None, :]

    def toeplitz(k4, lag, keep):
        m = k4[:, :, jnp.clip(lag, 0, t_n - 1), :]
        m = jnp.where(keep[None, None, :, :, None], m, 0.0)
        return jnp.transpose(m, (0, 2, 1, 3, 4)).reshape(g_n, th, th)

    m_f = toeplitz(kt5[:, 0], jj - ii, jj >= ii)
    m_b = toeplitz(kt5[:, 1], ii - jj, ii >= jj)
    d_t = jnp.tile(s5_d, (1, t_n)).reshape(g_n, 1, th)

    u_g = jnp.transpose(u_all.reshape(bsz, n_all, t_n, g_n, hg), (3, 0, 1, 2, 4)).reshape(g_n, bsz * n_all, th)
    e_cat = e
    s_in = _s5_state_inputs(u_g, e_cat)
    rows = bsz * n_all
    planes = jnp.transpose(s_in.reshape(g_n, rows, 4, p_n), (2, 1, 0, 3)).reshape(4, rows, g_n * p_n // LANES, LANES)
    a_pl = jnp.transpose(a16.reshape(g_n, 4, p_n), (1, 0, 2)).reshape(4, g_n * p_n // LANES, LANES)
    st = _s5_scan(planes, a_pl, n_ctx, n_all)
    s_g = jnp.transpose(st.reshape(4, rows, g_n, p_n), (2, 1, 0, 3)).reshape(g_n, rows, 4 * p_n)
    y_g = _s5_apply(u_g, s_g, m_f, m_b, d_t, cs, n_ctx, n_all)
    n_lat = n_all - n_ctx
    y = jnp.transpose(y_g.reshape(g_n, bsz, n_lat, t_n, hg), (1, 2, 3, 0, 4))
    return y.reshape(bsz, n_lat * t_n, width)


def _s5_param2_kernel(are_ref, aim_ref, ldt_ref, bre_ref, bim_ref, cre_ref, cim_ref,
                      e_ref, c_ref, m_ref, a16_ref):
    t_n, hg, p_n = S5_CHUNK, S5_GROUP, S5_STATE
    tau = lax.broadcasted_iota(jnp.int32, (t_n, 1, p_n), 0).astype(F32)
    taps = []
    for d in range(2):
        a_re = are_ref[0, d:d + 1, :]
        a_im = aim_ref[0, d:d + 1, :]
        dt = jnp.exp(ldt_ref[0, d:d + 1, :])
        lam = a_re * dt
        th = a_im * dt
        er = jnp.exp(lam)
        ab_re = er * jnp.cos(th)
        ab_im = er * jnp.sin(th)
        den = a_re * a_re + a_im * a_im
        x_re = ab_re - 1.0
        co_re = (x_re * a_re + ab_im * a_im) / den
        co_im = (ab_im * a_re - x_re * a_im) / den
        bt_re = bre_ref[0, d].T
        bt_im = bim_ref[0, d].T
        bb_re = co_re * bt_re - co_im * bt_im
        bb_im = co_re * bt_im + co_im * bt_re
        c_re = cre_ref[0, d]
        c_im = cim_ref[0, d]

        def power(tv):
            mag = jnp.exp(tv * lam)
            return mag * jnp.cos(tv * th), mag * jnp.sin(tv * th)

        pw_re, pw_im = power(tau if d == 0 else (t_n - 1.0 - tau))
        cp_re = (c_re[None] * pw_re - c_im[None] * pw_im).reshape(t_n * hg, p_n)
        cp_im = (c_re[None] * pw_im + c_im[None] * pw_re).reshape(t_n * hg, p_n)
        taps.append(_dot_nt(bb_re, cp_re, HIGHEST) - _dot_nt(bb_im, cp_im, HIGHEST))

        lo, hi = 2 * d * p_n, (2 * d + 1) * p_n
        pe_re, pe_im = power((t_n - 1.0 - tau) if d == 0 else tau)
        e_ref[0, :, 0, :, lo:hi] = pe_re * bb_re[None] - pe_im * bb_im[None]
        e_ref[0, :, 0, :, hi:hi + p_n] = pe_re * bb_im[None] + pe_im * bb_re[None]
        pc_re, pc_im = power((tau + 1.0) if d == 0 else (t_n - tau))
        c_ref[0, :, 0, :, lo:hi] = c_re[None] * pc_re - c_im[None] * pc_im
        c_ref[0, :, 0, :, hi:hi + p_n] = -(c_re[None] * pc_im + c_im[None] * pc_re)

        mag16 = jnp.exp(float(t_n) * lam)
        a16_ref[0, d, 0:1, :] = mag16 * jnp.cos(float(t_n) * th)
        a16_ref[0, d, 1:2, :] = mag16 * jnp.sin(float(t_n) * th)

    width = t_n * hg
    lane = lax.broadcasted_iota(jnp.int32, (hg, width), 1)
    for t in range(t_n):
        sf = t * hg
        sb = (t_n - 1 - t) * hg
        f = taps[0] if sf == 0 else jnp.where(lane >= sf, pltpu.roll(taps[0], sf, 1), 0.0)
        b = taps[1] if sb == 0 else jnp.where(lane < width - sb, pltpu.roll(taps[1], width - sb, 1), 0.0)
        m_ref[0, t, 0] = f + b


def _s5_params2(a_re, a_im, log_dt, b_re, b_im, c_re, c_im, nb):
    g_n = a_re.shape[1]
    gl_n = g_n // nb
    p_n, hg, t_n = S5_STATE, S5_GROUP, S5_CHUNK
    tr = lambda a: jnp.swapaxes(a, 0, 1)
    ldt = jnp.broadcast_to(tr(log_dt)[:, :, None], (g_n, 2, p_n))
    spec3 = pl.BlockSpec((1, 2, p_n), lambda g: (g, 0, 0))
    spec4 = pl.BlockSpec((1, 2, hg, p_n), lambda g: (g, 0, 0, 0))
    spec_b = pl.BlockSpec((1, 2, p_n, hg), lambda g: (g, 0, 0, 0))
    wide = 4 * p_n
    comp = jax.ShapeDtypeStruct((nb, t_n, gl_n, hg, wide), F32)
    comp_spec = pl.BlockSpec((1, t_n, 1, hg, wide), lambda g: (g // gl_n, 0, g % gl_n, 0, 0))
    return pl.pallas_call(
        _s5_param2_kernel,
        out_shape=(comp, comp, comp, jax.ShapeDtypeStruct((g_n, 2, 2, p_n), F32)),
        grid=(g_n,),
        in_specs=[spec3, spec3, spec3, spec_b, spec_b, spec4, spec4],
        out_specs=(comp_spec, comp_spec, comp_spec, pl.BlockSpec((1, 2, 2, p_n), lambda g: (g, 0, 0, 0))),
        compiler_params=_cparams(("parallel",)),
        name="s5_params",
    )(tr(a_re), tr(a_im), ldt, tr(b_re), tr(b_im), tr(c_re), tr(c_im))


def _expand_block_diag(comp, rep_ref, mask_ref, w_scr):
    k = w_scr.shape[0]
    period = mask_ref.shape[1]
    cb = comp.astype(BF16)
    step = 512
    for c0 in range(0, k, step):
        blk = _dot(cb, rep_ref[:, c0:c0 + step]).astype(BF16)
        for q0 in range(0, step, period):
            w_scr[:, c0 + q0:c0 + q0 + period] = blk[:, q0:q0 + period] * mask_ref[...]


def _s5_ein_kernel(u_ref, ec_ref, rep_ref, mask_ref, o_ref, w_scr):
    _expand_block_diag(ec_ref[0], rep_ref, mask_ref, w_scr)
    o_ref[0] = _dot(u_ref[0].astype(BF16), w_scr[...])


def _s5_chunk_inputs(u_rows, e_comp, rep_e, mask_e):
    nb, rows, k = u_rows.shape
    cw = e_comp.shape[2]
    return pl.pallas_call(
        _s5_ein_kernel,
        out_shape=jax.ShapeDtypeStruct((nb, rows, k), F32),
        grid=(nb,),
        in_specs=[pl.BlockSpec((1, rows, k), lambda j: (j, 0, 0)),
                  pl.BlockSpec((1, k, cw), lambda j: (j, 0, 0)),
                  pl.BlockSpec(rep_e.shape, lambda j: (0, 0)),
                  pl.BlockSpec(mask_e.shape, lambda j: (0, 0))],
        out_specs=pl.BlockSpec((1, rows, k), lambda j: (j, 0, 0)),
        scratch_shapes=[pltpu.VMEM((k, k), BF16)],
        compiler_params=_cparams(("parallel",)),
        name="s5_chunk_inputs",
    )(u_rows, e_comp, rep_e, mask_e)


def _s5_bscan_kernel(e_ref, a_ref, o_ref, *, bsz, n_ctx, n_lat):
    q = e_ref.shape[2] // 4
    planes = lambda row, d: (row[:, (2 * d) * q:(2 * d + 1) * q], row[:, (2 * d + 1) * q:(2 * d + 2) * q])
    coef = [planes(a_ref[0], d) for d in range(2)]
    ctx0 = bsz * n_lat

    def advance(state, rows):
        new = []
        for (sr, si), (b, d), row in zip(state, [(b, d) for b in range(bsz) for d in range(2)], rows):
            ar, ai = coef[d]
            er, ei = planes(e_ref[0, pl.ds(row, 1), :], d)
            new.append((ar * sr - ai * si + er, ar * si + ai * sr + ei))
        return tuple(new)

    def ctx_step(s, state):
        rows = [ctx0 + b * n_ctx + (s if d == 0 else n_ctx - 1 - s) for b in range(bsz) for d in range(2)]
        return advance(state, rows)

    def lat_step(s, state):
        rows = [b * n_lat + (s if d == 0 else n_lat - 1 - s) for b in range(bsz) for d in range(2)]
        for (sr, si), (b, d), row in zip(state, [(b, d) for b in range(bsz) for d in range(2)], rows):
            o_ref[0, pl.ds(row, 1), (2 * d) * q:(2 * d + 1) * q] = sr
            o_ref[0, pl.ds(row, 1), (2 * d + 1) * q:(2 * d + 2) * q] = si
        return advance(state, rows)

    zero = jnp.zeros((1, q), F32)
    state = tuple((zero, zero) for _ in range(2 * bsz))
    state = lax.fori_loop(0, n_ctx, ctx_step, state)
    lax.fori_loop(0, n_lat, lat_step, state)


def _s5_bscan(e_rows, a_rows, bsz, n_ctx, n_lat):
    nb, rows, k = e_rows.shape
    return pl.pallas_call(
        functools.partial(_s5_bscan_kernel, bsz=bsz, n_ctx=n_ctx, n_lat=n_lat),
        out_shape=jax.ShapeDtypeStruct((nb, bsz * n_lat, k), F32),
        grid=(nb,),
        in_specs=[pl.BlockSpec((1, rows, k), lambda j: (j, 0, 0)),
                  pl.BlockSpec((1, 1, k), lambda j: (j, 0, 0))],
        out_specs=pl.BlockSpec((1, bsz * n_lat, k), lambda j: (j, 0, 0)),
        compiler_params=_cparams(("parallel",)),
        name="s5_scan",
    )(e_rows, a_rows)


def _s5_out_kernel(u_ref, s_ref, mc_ref, cc_ref, d_ref, rep_m_ref, mask_m_ref, rep_e_ref, mask_e_ref,
                   o_ref, wm_scr, wc_scr):
    @pl.when(pl.program_id(1) == 0)
    def _():
        _expand_block_diag(mc_ref[0], rep_m_ref, mask_m_ref, wm_scr)
        _expand_block_diag(cc_ref[0], rep_e_ref, mask_e_ref, wc_scr)

    u = u_ref[0]
    y = _dot(u.astype(BF16), wm_scr[...]) + _dot_nt(s_ref[0].astype(BF16), wc_scr[...])
    o_ref[0] = y + u * d_ref[0]


def _s5_outputs(u_rows, s_rows, m_comp, c_comp, d_rows, rep_m, mask_m, rep_e, mask_e):
    nb, _, k = u_rows.shape
    rows = s_rows.shape[1]
    cw = m_comp.shape[2]
    tr = rows // 2
    const = lambda a: pl.BlockSpec(a.shape, lambda j, i: (0, 0))
    return pl.pallas_call(
        _s5_out_kernel,
        out_shape=jax.ShapeDtypeStruct((nb, rows, k), F32),
        grid=(nb, rows // tr),
        in_specs=[pl.BlockSpec((1, tr, k), lambda j, i: (j, i, 0)),
                  pl.BlockSpec((1, tr, k), lambda j, i: (j, i, 0)),
                  pl.BlockSpec((1, k, cw), lambda j, i: (j, 0, 0)),
                  pl.BlockSpec((1, k, cw), lambda j, i: (j, 0, 0)),
                  pl.BlockSpec((1, 1, k), lambda j, i: (j, 0, 0)),
                  const(rep_m), const(mask_m), const(rep_e), const(mask_e)],
        out_specs=pl.BlockSpec((1, tr, k), lambda j, i: (j, i, 0)),
        scratch_shapes=[pltpu.VMEM((k, k), BF16), pltpu.VMEM((k, k), BF16)],
        compiler_params=_cparams(("parallel", "arbitrary")),
        name="s5_outputs",
    )(u_rows, s_rows, m_comp, c_comp, d_rows, rep_m, mask_m, rep_e, mask_e)


def _s5_branch_blocked(u_blk, bsz, l_ctx, l_lat, a_re, a_im, log_dt, b_re, b_im, c_re, c_im, s5_d):
    nb, m_all, _ = u_blk.shape
    hg, t_n, p_n = S5_GROUP, S5_CHUNK, S5_STATE
    g_n = a_re.shape[1]
    gl_n = g_n // nb
    k = t_n * LANES
    n_lat = l_lat // t_n
    n_ctx = l_ctx // t_n
    e_c, c_c, m_c, a16 = _s5_params2(a_re, a_im, log_dt, b_re, b_im, c_re, c_im, nb)
    cw = 4 * p_n
    e_comp, c_comp, m_comp = (a.reshape(nb, k, cw) for a in (e_c, c_c, m_c))

    row_gl = (jnp.arange(k) // hg) % gl_n
    col = jnp.arange(k)
    src = jnp.arange(cw)
    rep_e = ((src[:, None] // p_n == col[None, :] // (gl_n * p_n)) & (src[:, None] % p_n == col[None, :] % p_n)).astype(BF16)
    mask_e = (row_gl[:, None] == (jnp.arange(gl_n * p_n)[None, :] // p_n)).astype(BF16)
    rep_m = ((src[:, None] // hg == col[None, :] // (gl_n * hg)) & (src[:, None] % hg == col[None, :] % hg)).astype(BF16)
    mask_m = (row_gl[:, None] == (jnp.arange(gl_n * hg)[None, :] // hg)).astype(BF16)

    d_rows = jnp.tile(s5_d.reshape(nb, 1, gl_n * hg), (1, 1, t_n))
    a_rows = jnp.transpose(a16.reshape(nb, gl_n, 2, 2, p_n), (0, 2, 3, 1, 4)).reshape(nb, 1, 4 * gl_n * p_n)

    u_rows = u_blk.reshape(nb, m_all // t_n, k)
    e_rows = _s5_chunk_inputs(u_rows, e_comp, rep_e, mask_e)
    s_rows = _s5_bscan(e_rows, a_rows, bsz, n_ctx, n_lat)
    y_rows = _s5_outputs(u_rows, s_rows, m_comp, c_comp, d_rows, rep_m, mask_m, rep_e, mask_e)
    return y_rows.reshape(nb, bsz * l_lat, LANES)


def _rw_prep_kernel(z_ref, zp_ref, zn_ref, mu_ref, w2_ref, a2_ref, g2_ref, w0_ref, a0_ref,
                    kk_w_ref, ka_ref, rk_ref, seg_ref, segt_ref,
                    r_ref, v_ref, kk_ref, g_ref, bonus_ref, lw_ref, kd_ref, be_ref,
                    *, tm, l_lat, rw):
    j = pl.program_id(1)
    z = z_ref[...].astype(F32)
    lat = j > 0
    tl = lax.broadcasted_iota(jnp.int32, (tm, 1), 0)
    tok = (j - 1) * tm + tl
    col = tl % GRID_W
    m_l = jnp.where(lat, col, tl) > 0
    m_r = jnp.where(lat, col - (GRID_W - 1), tl - (tm - 1)) < 0
    m_u = jnp.logical_and(lat, tok >= GRID_W)
    m_d = jnp.logical_and(lat, tok < l_lat - GRID_W)
    z_ext = jnp.concatenate([zp_ref[...], z_ref[...], zn_ref[...]], axis=0)
    rel = (lax.broadcasted_iota(jnp.int32, (tm, tm + 2 * GRID_W), 1) - GRID_W
           - lax.broadcasted_iota(jnp.int32, (tm, tm + 2 * GRID_W), 0))
    pick = (jnp.logical_and(rel == -1, m_l) | jnp.logical_and(rel == 1, m_r)
            | jnp.logical_and(rel == -GRID_W, m_u) | jnp.logical_and(rel == GRID_W, m_d))
    s = _dot(jnp.where(pick, 1.0, 0.0).astype(z_ext.dtype), z_ext)
    cnt = (m_l.astype(F32) + m_r.astype(F32)) + (m_u.astype(F32) + m_d.astype(F32))
    zs = z + (s * (1.0 / cnt) - z) * mu_ref[...]

    r = zs[:, 0:rw]
    k = zs[:, rw:2 * rw]
    v = zs[:, 2 * rw:3 * rw]
    o = 3 * rw
    wd = zs[:, o:o + LANES]
    ad = zs[:, o + LANES:o + 2 * LANES]
    gd = zs[:, o + 2 * LANES:o + 3 * LANES]

    seg = seg_ref[...]
    segt = segt_ref[...]

    def head_sum(t):
        return _dot_hilo(_dot_hilo(t, seg), segt)

    g_ref[0] = _dot(_sigmoid(gd), g2_ref[...], "bf16").astype(g_ref.dtype)
    kk = k * kk_w_ref[...]
    kk = kk * lax.rsqrt(head_sum(kk * kk) + 1e-12)
    wl = w0_ref[...] + _dot(jnp.tanh(wd), w2_ref[...], "bf16")
    al = a0_ref[...] + _dot(ad, a2_ref[...], "bf16")
    r_ref[0] = r.astype(r_ref.dtype)
    v_ref[0] = v.astype(v_ref.dtype)
    kk_ref[0] = kk.astype(kk_ref.dtype)
    k_sum = jnp.zeros_like(r)
    for d in range(2):
        a = _sigmoid(al[:, d * rw:(d + 1) * rw])
        k_d = k * (1.0 + (a - 1.0) * ka_ref[...])
        k_sum = k_sum + k_d
        lw_ref[d, 0] = -math.exp(-0.5) * _sigmoid(wl[:, d * rw:(d + 1) * rw])
        kd_ref[d, 0] = k_d.astype(kd_ref.dtype)
        be_ref[d, 0] = (kk * a).astype(be_ref.dtype)
    bonus_ref[0] = (head_sum(r * rk_ref[...] * k_sum) * v).astype(bonus_ref.dtype)


def _rw_prep(z_rw, bsz, l_ctx, l_lat, mu, w2bd, a2bd, g2, w0cat, a0cat, k_k, k_a, r_k_flat, seg, segt):
    cols = z_rw.shape[1]
    tm = l_ctx
    rw = g2.shape[1]
    l_all = l_ctx + l_lat
    nblk = l_all // tm
    lat_blk = l_lat // tm
    hb = tm // GRID_W
    lat_hblk = l_lat // GRID_W

    def main_blk(b, j):
        return jnp.where(j == 0, bsz * lat_blk + b, b * lat_blk + j - 1)

    def prev_halo(b, j):
        return b * lat_hblk + jnp.maximum((j - 1) * hb - 1, 0)

    def next_halo(b, j):
        return b * lat_hblk + jnp.minimum(jnp.maximum(j, 1) * hb, lat_hblk - 1)

    full = lambda shape: pl.BlockSpec(shape, lambda b, j: (0,) * len(shape))
    shared = jax.ShapeDtypeStruct((bsz, l_all, rw), BF16)
    lat_only = jax.ShapeDtypeStruct((bsz, l_lat, rw), BF16)
    per_dir = jax.ShapeDtypeStruct((2, bsz, l_all, rw), BF16)
    per_dir_f32 = jax.ShapeDtypeStruct((2, bsz, l_all, rw), F32)
    o_shared = pl.BlockSpec((1, tm, rw), lambda b, j: (b, j, 0))
    o_lat = pl.BlockSpec((1, tm, rw), lambda b, j: (b, jnp.maximum(j - 1, 0), 0))
    o_dir = pl.BlockSpec((2, 1, tm, rw), lambda b, j: (0, b, j, 0))
    return pl.pallas_call(
        functools.partial(_rw_prep_kernel, tm=tm, l_lat=l_lat, rw=rw),
        out_shape=(shared,) * 3 + (lat_only,) * 2 + (per_dir_f32, per_dir, per_dir),
        grid=(bsz, nblk),
        in_specs=[pl.BlockSpec((tm, cols), lambda b, j: (main_blk(b, j), 0)),
                  pl.BlockSpec((GRID_W, cols), lambda b, j: (prev_halo(b, j), 0)),
                  pl.BlockSpec((GRID_W, cols), lambda b, j: (next_halo(b, j), 0)),
                  full((1, cols)), full(w2bd.shape), full(a2bd.shape), full(g2.shape),
                  full((1, 2 * rw)), full((1, 2 * rw)), full((1, rw)), full((1, rw)), full((1, rw)),
                  full(seg.shape), full(segt.shape)],
        out_specs=(o_shared,) * 3 + (o_lat,) * 2 + (o_dir,) * 3,
        compiler_params=_cparams(("parallel", "arbitrary")),
        name="rwkv_prep",
    )(z_rw, z_rw, z_rw, mu.reshape(1, cols), w2bd, a2bd, g2, w0cat, a0cat,
      k_k.reshape(1, rw), k_a.reshape(1, rw), r_k_flat.reshape(1, rw), seg, segt)


def _stack_heads(x, head0):
    return jnp.concatenate([jnp.where(head0, x, 0.0), jnp.where(head0, 0.0, x)], axis=0)


def _rw_chunk_kernel(r_ref, v_ref, kk_ref, lw_ref, kd_ref, be_ref, y_ref,
                     z_scr, gc_s, lhs_s, kb_s, kht_s, bht_s, vs_s, rt_s, aab_s, akr_s, arb_s, t_s, x_s, xin_s,
                     yc_s, qy_s, bw_s, n_s, y_s, *, n_pairs, n_sub):
    c_n = RW_CHUNK
    n2 = 2 * c_n
    rev = (pl.program_id(0) % 2) == 1

    @pl.when(pl.program_id(1) == 0)
    def _():
        z_scr[...] = jnp.zeros_like(z_scr)

    ri = lax.broadcasted_iota(jnp.int32, (c_n, c_n), 0)
    ci = lax.broadcasted_iota(jnp.int32, (c_n, c_n), 1)
    tri = (jnp.where(rev, ci - ri, ri - ci) >= 0).astype(F32)
    r2 = lax.broadcasted_iota(jnp.int32, (n2, n2), 0)
    c2 = lax.broadcasted_iota(jnp.int32, (n2, n2), 1)
    t2 = r2 % c_n
    i2 = c2 % c_n
    same_head = (r2 // c_n) == (c2 // c_n)
    before = jnp.logical_and(same_head, jnp.where(rev, i2 - t2, t2 - i2) > 0)
    upto = jnp.logical_or(before, r2 == c2)
    eye = (r2 == c2).astype(F32)
    head0 = lax.broadcasted_iota(jnp.int32, (1, LANES), 1) < RW_HEAD

    def blk(s):
        return (r2 // s) == (c2 // s)

    pairs = range(n_pairs)
    units = range(n_sub * n_pairs)
    lanes_of = lambda p: slice(p * LANES, (p + 1) * LANES)

    row_of = lambda k: pl.multiple_of(jnp.where(rev, n_sub - 1 - k, k) * c_n, c_n)
    for cc in range(n_sub):
        rows = pl.ds(row_of(cc), c_n)
        lw = lw_ref[0, 0, rows, :]
        cum = _dot(tri, lw, HIGHEST)
        tot = jnp.sum(lw, axis=0, keepdims=True)
        g_inv = jnp.exp(-cum)
        g_hat = jnp.exp(tot - cum)
        kd = kd_ref[0, 0, rows, :]
        be = be_ref[0, 0, rows, :]
        at = kk_ref[0, rows, :] * jnp.exp(cum - lw)
        rt = r_ref[0, rows, :] * jnp.exp(cum)
        kt = kd * g_inv
        bt = be * g_inv
        kh = kd * g_hat
        bh = be * g_hat
        gc_s[cc] = jnp.exp(tot)
        v = v_ref[0, rows, :]
        for p in pairs:
            u = cc * n_pairs + p
            st = lambda x: _stack_heads(x[:, lanes_of(p)], head0)
            rt_p = st(rt)
            at_p = st(at).astype(BF16)
            lhs_s[u, :n2] = at_p
            lhs_s[u, n2:] = rt_p.astype(BF16)
            xin_s[u, :, :LANES] = at_p
            rt_s[u] = rt_p
            kb_s[u, :n2] = st(kt).astype(BF16)
            kb_s[u, n2:] = st(bt).astype(BF16)
            kht_s[u] = st(kh).T.astype(BF16)
            bht_s[u] = st(bh).T.astype(BF16)
            vs_s[u] = st(v).astype(BF16)

    for u in units:
        g = _dot_nt(lhs_s[u], kb_s[u])
        a_ab = jnp.where(before, g[:n2, n2:], 0.0)
        akr_s[u, :n2] = jnp.where(before, g[:n2, :n2], 0.0).astype(BF16)
        akr_s[u, n2:] = jnp.where(upto, g[n2:, :n2], 0.0).astype(BF16)
        arb_s[u] = jnp.where(upto, g[n2:, n2:], 0.0).astype(BF16)
        aab_s[u] = a_ab
        t_s[u] = eye - jnp.where(blk(2), a_ab, 0.0)

    s = 2
    while s < c_n:
        off = jnp.logical_and(blk(2 * s), jnp.logical_not(blk(s)))
        for u in units:
            x_s[u] = _dot(t_s[u].astype(BF16), jnp.where(off, aab_s[u], 0.0).astype(BF16)).astype(BF16)
        for u in units:
            t = t_s[u]
            t_s[u] = t - _dot(x_s[u], t.astype(BF16))
        s *= 2

    for u in units:
        av = _dot(akr_s[u], vs_s[u])
        xin_s[u, :, LANES:] = av[:n2].astype(BF16)
        yc_s[u] = av[n2:]

    for u in units:
        wu = _dot(t_s[u].astype(BF16), xin_s[u]).astype(BF16)
        q = _dot(arb_s[u], wu)
        bwu = _dot(bht_s[u], wu)
        qy_s[u] = (rt_s[u] - q[:, :LANES]).astype(BF16)
        yc_s[u] = yc_s[u] - q[:, LANES:]
        bw_s[u] = bwu[:, :LANES]
        n_s[u] = _dot(kht_s[u], vs_s[u]) - bwu[:, LANES:]

    for k in range(n_sub):
        g_c = gc_s[k]
        for p in pairs:
            u = k * n_pairs + p
            z0 = z_scr[p].astype(BF16)
            y_p = _dot(qy_s[u], z0) + yc_s[u]
            y_s[k, :, lanes_of(p)] = y_p[:c_n] + y_p[c_n:]
            m_z = eye * g_c[:, lanes_of(p)] - bw_s[u]
            z_scr[p] = _dot(m_z.astype(BF16), z0) + n_s[u]
    for k in range(n_sub):
        y_ref[0, 0, pl.ds(row_of(k), c_n), :] = y_s[k]


def _rw_scan(r, v, kk, lw, kd, be, l_ctx):
    bsz, l_all, rw = r.shape
    n_sub = RW_SUB
    c_n = RW_CHUNK
    rows = n_sub * c_n
    n_all = l_all // rows
    n_ctx = l_ctx // rows
    n_lat = n_all - n_ctx
    n_pairs = rw // LANES
    n2 = 2 * c_n
    vm = lambda nr, cols, dt: pltpu.VMEM((n_sub * n_pairs, nr, cols), dt)

    def chunk_of(bd, s):
        fwd = s
        bwd = jnp.where(s < n_ctx, n_ctx - 1 - s, n_all + n_ctx - 1 - s)
        return jnp.where(bd % 2 == 0, fwd, bwd)

    def out_chunk(bd, s):
        c = chunk_of(bd, s)
        edge = jnp.where(bd % 2 == 0, 0, n_lat - 1)
        return jnp.where(s < n_ctx, edge, c - n_ctx)

    shared = pl.BlockSpec((1, rows, rw), lambda bd, s: (bd // 2, chunk_of(bd, s), 0))
    per_dir = pl.BlockSpec((1, 1, rows, rw), lambda bd, s: (bd % 2, bd // 2, chunk_of(bd, s), 0))
    return pl.pallas_call(
        functools.partial(_rw_chunk_kernel, n_pairs=n_pairs, n_sub=n_sub),
        out_shape=jax.ShapeDtypeStruct((2, bsz, n_lat * rows, rw), F32),
        grid=(2 * bsz, n_all),
        in_specs=[shared, shared, shared, per_dir, per_dir, per_dir],
        out_specs=pl.BlockSpec((1, 1, rows, rw), lambda bd, s: (bd % 2, bd // 2, out_chunk(bd, s), 0)),
        scratch_shapes=[pltpu.VMEM((n_pairs, n2, n2), F32),
                        pltpu.VMEM((n_sub, 1, rw), F32),
                        vm(2 * n2, n2, BF16),
                        vm(2 * n2, n2, BF16),
                        vm(n2, n2, BF16),
                        vm(n2, n2, BF16),
                        vm(n2, n2, BF16),
                        vm(n2, n2, F32),
                        vm(n2, n2, F32),
                        vm(2 * n2, n2, BF16),
                        vm(n2, n2, BF16),
                        vm(n2, n2, F32),
                        vm(n2, n2, BF16),
                        vm(n2, 2 * n2, BF16),
                        vm(n2, n2, F32),
                        vm(n2, n2, BF16),
                        vm(n2, n2, F32),
                        vm(n2, n2, F32),
                        pltpu.VMEM((n_sub, c_n, rw), F32)],
        compiler_params=_cparams(("parallel", "arbitrary")),
        name="rwkv_scan",
    )(r, v, kk, lw, kd, be)


def _rw_step_kernel(rf_ref, rb_ref, vf_ref, vb_ref, kkf_ref, kkb_ref, lwf_ref, lwb_ref, kdf_ref, kdb_ref,
                    bef_ref, beb_ref, yf_ref, yb_ref,
                    z_scr, gc_s, lhs_s, kb_s, kht_s, bht_s, vs_s, rt_s, aab_s, akr_s, arb_s, t_s, x_s, xin_s,
                    yc_s, wu_s, qm_s, yn_s, *, n_pairs, bsz):
    c_n = RW_CHUNK
    n2 = 2 * c_n

    lanes_of = lambda p: slice(p * LANES, (p + 1) * LANES)
    unit = lambda d, b, p: (d * bsz + b) * n_pairs + p
    units = [(d, b, p) for d in range(2) for b in range(bsz) for p in range(n_pairs)]

    @pl.when(pl.program_id(0) == 0)
    def _():
        z_scr[...] = jnp.zeros_like(z_scr)
        qm_s[...] = jnp.zeros_like(qm_s)
        yn_s[...] = jnp.zeros_like(yn_s)

    for d, b, p in units:
        u = unit(d, b, p)
        y_ref = yf_ref if d == 0 else yb_ref
        yz = _dot(qm_s[u], z_scr[u].astype(BF16)) + yn_s[u]
        y_ref[b, :, lanes_of(p)] = yz[:c_n] + yz[c_n:n2]
        z_scr[u] = yz[n2:]

    ri = lax.broadcasted_iota(jnp.int32, (c_n, c_n), 0)
    ci = lax.broadcasted_iota(jnp.int32, (c_n, c_n), 1)
    r2 = lax.broadcasted_iota(jnp.int32, (n2, n2), 0)
    c2 = lax.broadcasted_iota(jnp.int32, (n2, n2), 1)
    t2 = r2 % c_n
    i2 = c2 % c_n
    same_head = (r2 // c_n) == (c2 // c_n)
    diag = r2 == c2
    eye = diag.astype(F32)
    head0 = lax.broadcasted_iota(jnp.int32, (1, LANES), 1) < RW_HEAD
    tri = [(ri >= ci).astype(F32), (ri <= ci).astype(F32)]
    before = [jnp.logical_and(same_head, i2 < t2), jnp.logical_and(same_head, i2 > t2)]
    upto = [jnp.logical_or(m, diag) for m in before]

    def blk(s):
        return (r2 // s) == (c2 // s)

    srcs = [(rf_ref, vf_ref, kkf_ref, lwf_ref, kdf_ref, bef_ref), (rb_ref, vb_ref, kkb_ref, lwb_ref, kdb_ref, beb_ref)]

    for d in range(2):
        r_ref, v_ref, kk_ref, lw_ref, kd_ref, be_ref = srcs[d]
        for b in range(bsz):
            lw = lw_ref[0, b]
            cum = _dot(tri[d], lw, HIGHEST)
            tot = jnp.sum(lw, axis=0, keepdims=True)
            g_inv = jnp.exp(-cum)
            g_hat = jnp.exp(tot - cum)
            kd = kd_ref[0, b].astype(F32)
            be = be_ref[0, b].astype(F32)
            at = kk_ref[b].astype(F32) * jnp.exp(cum - lw)
            rt = r_ref[b].astype(F32) * jnp.exp(cum)
            kt = kd * g_inv
            bt = be * g_inv
            kh = kd * g_hat
            bh = be * g_hat
            gc_s[d * bsz + b] = jnp.exp(tot)
            v = v_ref[b].astype(F32)
            for p in range(n_pairs):
                u = unit(d, b, p)
                st = lambda x: _stack_heads(x[:, lanes_of(p)], head0)
                rt_p = st(rt)
                at_p = st(at).astype(BF16)
                lhs_s[u, :n2] = at_p
                lhs_s[u, n2:] = rt_p.astype(BF16)
                xin_s[u, :, :LANES] = at_p
                rt_s[u] = rt_p
                kb_s[u, :n2] = st(kt).astype(BF16)
                kb_s[u, n2:] = st(bt).astype(BF16)
                kht_s[u] = st(kh).T.astype(BF16)
                bht_s[u] = st(bh).T.astype(BF16)
                vs_s[u] = st(v).astype(BF16)

    for d, b, p in units:
        u = unit(d, b, p)
        g = _dot_nt(lhs_s[u], kb_s[u])
        a_ab = jnp.where(before[d], g[:n2, n2:], 0.0)
        akr_s[u, :n2] = jnp.where(before[d], g[:n2, :n2], 0.0).astype(BF16)
        akr_s[u, n2:] = jnp.where(upto[d], g[n2:, :n2], 0.0).astype(BF16)
        arb_s[u] = jnp.where(upto[d], g[n2:, n2:], 0.0).astype(BF16)
        aab_s[u] = a_ab
        t_s[u] = eye - jnp.where(blk(2), a_ab, 0.0)

    n_units = len(units)
    s = 2
    while s < c_n:
        off = jnp.logical_and(blk(2 * s), jnp.logical_not(blk(s)))
        for u in range(n_units):
            x_s[u] = _dot(t_s[u].astype(BF16), jnp.where(off, aab_s[u], 0.0).astype(BF16)).astype(BF16)
        for u in range(n_units):
            t = t_s[u]
            t_s[u] = t - _dot(x_s[u], t.astype(BF16))
        s *= 2

    for u in range(n_units):
        av = _dot(akr_s[u], vs_s[u])
        xin_s[u, :, LANES:] = av[:n2].astype(BF16)
        yc_s[u] = av[n2:]

    for u in range(n_units):
        wu_s[u] = _dot(t_s[u].astype(BF16), xin_s[u]).astype(BF16)
    for d, b, p in units:
        u = unit(d, b, p)
        wu = wu_s[u]
        q = _dot(arb_s[u], wu)
        bwu = _dot(bht_s[u], wu)
        qm_s[u, :n2] = (rt_s[u] - q[:, :LANES]).astype(BF16)
        qm_s[u, n2:] = (eye * gc_s[d * bsz + b][:, lanes_of(p)] - bwu[:, :LANES]).astype(BF16)
        yn_s[u, :n2] = yc_s[u] - q[:, LANES:]
        yn_s[u, n2:] = _dot(kht_s[u], vs_s[u]) - bwu[:, LANES:]


def _rw_scan2(r, v, kk, lw, kd, be, l_ctx):
    bsz, l_all, rw = r.shape
    c_n = RW_CHUNK
    n_all = l_all // c_n
    n_ctx = l_ctx // c_n
    n_lat = n_all - n_ctx
    n_pairs = rw // LANES
    n_units = 2 * bsz * n_pairs
    n2 = 2 * c_n
    vm = lambda nr, cols, dt: pltpu.VMEM((n_units, nr, cols), dt)

    clamp = lambda s: jnp.minimum(s, n_all - 1)
    chunk_f = lambda s: clamp(s)
    chunk_b = lambda s: jnp.where(clamp(s) < n_ctx, n_ctx - 1 - clamp(s), n_all + n_ctx - 1 - clamp(s))
    prev = lambda s: jnp.maximum(s - 1, 0)
    out_f = lambda s: jnp.maximum(prev(s) - n_ctx, 0)
    out_b = lambda s: jnp.where(prev(s) < n_ctx, n_lat - 1, chunk_b(prev(s)) - n_ctx)

    sh_f = pl.BlockSpec((bsz, c_n, rw), lambda s: (0, chunk_f(s), 0))
    sh_b = pl.BlockSpec((bsz, c_n, rw), lambda s: (0, chunk_b(s), 0))
    pd_f = pl.BlockSpec((1, bsz, c_n, rw), lambda s: (0, 0, chunk_f(s), 0))
    pd_b = pl.BlockSpec((1, bsz, c_n, rw), lambda s: (1, 0, chunk_b(s), 0))
    y_shape = jax.ShapeDtypeStruct((bsz, n_lat * c_n, rw), F32)
    return pl.pallas_call(
        functools.partial(_rw_step_kernel, n_pairs=n_pairs, bsz=bsz),
        out_shape=(y_shape, y_shape),
        grid=(n_all + 1,),
        in_specs=[sh_f, sh_b, sh_f, sh_b, sh_f, sh_b, pd_f, pd_b, pd_f, pd_b, pd_f, pd_b],
        out_specs=(pl.BlockSpec((bsz, c_n, rw), lambda s: (0, out_f(s), 0)),
                   pl.BlockSpec((bsz, c_n, rw), lambda s: (0, out_b(s), 0))),
        scratch_shapes=[vm(n2, n2, F32),
                        pltpu.VMEM((2 * bsz, 1, rw), F32),
                        vm(2 * n2, n2, BF16),
                        vm(2 * n2, n2, BF16),
                        vm(n2, n2, BF16),
                        vm(n2, n2, BF16),
                        vm(n2, n2, BF16),
                        vm(n2, n2, F32),
                        vm(n2, n2, F32),
                        vm(2 * n2, n2, BF16),
                        vm(n2, n2, BF16),
                        vm(n2, n2, F32),
                        vm(n2, n2, BF16),
                        vm(n2, 2 * n2, BF16),
                        vm(n2, n2, F32),
                        vm(n2, 2 * n2, BF16),
                        vm(2 * n2, n2, BF16),
                        vm(2 * n2, n2, F32)],
        compiler_params=_cparams(("arbitrary",)),
        name="rwkv_scan",
    )(r, r, v, v, kk, kk, lw, lw, kd, kd, be, be)


def _s5_glu_kernel(y_ref, wa_ref, wb_ref, o_ref, h_scr):
    @pl.when(pl.program_id(1) == 0)
    def _():
        for jb in range(y_ref.shape[0]):
            h_scr[:, jb * LANES:(jb + 1) * LANES] = _gelu_tanh(y_ref[jb]).astype(BF16)

    h = h_scr[...]
    o_ref[...] = (_dot(h, wa_ref[...]) * _sigmoid(_dot(h, wb_ref[...]))).astype(o_ref.dtype)


def _s5_glu(y_blk, w, *, tm, tn):
    nb, m, _ = y_blk.shape
    k = nb * LANES
    n = w.shape[1] // 2
    nj = n // tn
    return pl.pallas_call(
        _s5_glu_kernel,
        out_shape=jax.ShapeDtypeStruct((m, n), BF16),
        grid=(m // tm, nj),
        in_specs=[pl.BlockSpec((nb, tm, LANES), lambda i, j: (0, i, 0)),
                  pl.BlockSpec((k, tn), lambda i, j: (0, j)),
                  pl.BlockSpec((k, tn), lambda i, j: (0, j + nj))],
        out_specs=pl.BlockSpec((tm, tn), lambda i, j: (i, j)),
        scratch_shapes=[pltpu.VMEM((tm, k), BF16)],
        compiler_params=_cparams(("parallel", "arbitrary")),
        name="s5_glu",
    )(y_blk, w, w)


def _rw_merge_kernel(yf_ref, yb_ref, bonus_ref, g_ref, lnw_ref, lnb_ref, seg_ref, segt_ref,
                     ga_ref, gb_ref, s5_ref, w_ref, o_ref, h_scr):
    @pl.when(pl.program_id(1) == 0)
    def _():
        seg = seg_ref[...]
        segt = segt_ref[...]
        inv_n = 1.0 / RW_HEAD

        def head_mean(t):
            return _dot_hilo(_dot_hilo(t, seg), segt) * inv_n

        y = yf_ref[...] + yb_ref[...]
        dy = y - head_mean(y)
        var = head_mean(dy * dy)
        y = dy * lax.rsqrt(var + GN_EPS) * lnw_ref[...] + lnb_ref[...] + bonus_ref[...].astype(F32)
        h_scr[...] = (y * g_ref[...].astype(F32)).astype(BF16)

    rw_out = _dot(h_scr[...], w_ref[...])
    merged = ga_ref[...].astype(F32) * s5_ref[...].astype(F32) + gb_ref[...].astype(F32) * rw_out
    o_ref[...] = merged.astype(o_ref.dtype)


def _rw_merge(y_f, y_b, bonus, g, ln_w, ln_b, seg, segt, gates, s5_out, w_proj, *, tm, tn):
    m, rw = y_f.shape
    n = w_proj.shape[1]
    nj = n // tn
    full = lambda shape: pl.BlockSpec(shape, lambda i, j: (0,) * len(shape))
    return pl.pallas_call(
        _rw_merge_kernel,
        out_shape=jax.ShapeDtypeStruct((m, n), BF16),
        grid=(m // tm, nj),
        in_specs=[pl.BlockSpec((tm, rw), lambda i, j: (i, 0)),
                  pl.BlockSpec((tm, rw), lambda i, j: (i, 0)),
                  pl.BlockSpec((tm, rw), lambda i, j: (i, 0)),
                  pl.BlockSpec((tm, rw), lambda i, j: (i, 0)),
                  full((1, rw)), full((1, rw)), full(seg.shape), full(segt.shape),
                  pl.BlockSpec((tm, tn), lambda i, j: (i, j)),
                  pl.BlockSpec((tm, tn), lambda i, j: (i, j + nj)),
                  pl.BlockSpec((tm, tn), lambda i, j: (i, j)),
                  pl.BlockSpec((rw, tn), lambda i, j: (0, j))],
        out_specs=pl.BlockSpec((tm, tn), lambda i, j: (i, j)),
        scratch_shapes=[pltpu.VMEM((tm, rw), BF16)],
        compiler_params=_cparams(("parallel", "arbitrary")),
        name="rwkv_merge",
    )(y_f, y_b, bonus, g, ln_w.reshape(1, rw), ln_b.reshape(1, rw), seg, segt,
      gates, gates, s5_out, w_proj)


def _resid_ln_kernel(a_ref, w_ref, x_ref, g_ref, nw_ref, sh_ref, sc_ref, h_ref, hn_ref):
    h = x_ref[...] + g_ref[0] * _dot(a_ref[...], w_ref[...])
    h_ref[...] = h
    hn_ref[...] = _lnmod_rows(h, nw_ref[...], sh_ref[0], sc_ref[0]).astype(hn_ref.dtype)


def _resid_matmul_ln(a, w, x2, g_tab, nw, sh_tab, sc_tab, mod_row_of_block, *, tm):
    m, k = a.shape
    n = w.shape[1]
    mod_map = lambda i: (mod_row_of_block(i), 0, 0)
    return pl.pallas_call(
        _resid_ln_kernel,
        out_shape=(jax.ShapeDtypeStruct((m, n), F32), jax.ShapeDtypeStruct((m, n), BF16)),
        grid=(m // tm,),
        in_specs=[pl.BlockSpec((tm, k), lambda i: (i, 0)),
                  pl.BlockSpec((k, n), lambda i: (0, 0)),
                  pl.BlockSpec((tm, n), lambda i: (i, 0)),
                  pl.BlockSpec((1, 1, n), mod_map),
                  pl.BlockSpec((1, n), lambda i: (0, 0)),
                  pl.BlockSpec((1, 1, n), mod_map),
                  pl.BlockSpec((1, 1, n), mod_map)],
        out_specs=(pl.BlockSpec((tm, n), lambda i: (i, 0)), pl.BlockSpec((tm, n), lambda i: (i, 0))),
        compiler_params=_cparams(("parallel",)),
        name="out_proj",
    )(a, w, x2, g_tab, nw.reshape(1, n), sh_tab, sc_tab)


def _wres_swiglu_kernel(a_ref, w1_ref, w3_ref, o_ref, w1_scr, w3_scr):
    @pl.when(pl.program_id(1) == 0)
    def _():
        w1_scr[...] = w1_ref[...].astype(BF16)
        w3_scr[...] = w3_ref[...].astype(BF16)

    a = a_ref[...]
    o_ref[...] = (_silu(_dot(a, w1_scr[...])) * _dot(a, w3_scr[...])).astype(o_ref.dtype)


def _wres_swiglu(a, w13, d_ff, *, tm, tn):
    m, k = a.shape
    nj = d_ff // tn
    return pl.pallas_call(
        _wres_swiglu_kernel,
        out_shape=jax.ShapeDtypeStruct((m, d_ff), BF16),
        grid=(nj, m // tm),
        in_specs=[pl.BlockSpec((tm, k), lambda j, i: (i, 0)),
                  pl.BlockSpec((k, tn), lambda j, i: (0, j)),
                  pl.BlockSpec((k, tn), lambda j, i: (0, j + nj))],
        out_specs=pl.BlockSpec((tm, tn), lambda j, i: (i, j)),
        scratch_shapes=[pltpu.VMEM((k, tn), BF16), pltpu.VMEM((k, tn), BF16)],
        compiler_params=_cparams(("arbitrary", "arbitrary")),
        name="ffn_up",
    )(a, w13, w13)


def _resid_mm_kernel(a_ref, w_ref, x_ref, g_ref, o_ref):
    o_ref[...] = x_ref[...] + g_ref[0] * _dot(a_ref[...], w_ref[...])


def _resid_matmul(a, w, x2, g_tab, mod_row_of_block, *, tm, tn):
    m, k = a.shape
    n = w.shape[1]
    return pl.pallas_call(
        _resid_mm_kernel,
        out_shape=jax.ShapeDtypeStruct((m, n), F32),
        grid=(m // tm, n // tn),
        in_specs=[pl.BlockSpec((tm, k), lambda i, j: (i, 0)),
                  pl.BlockSpec((k, tn), lambda i, j: (0, j)),
                  pl.BlockSpec((tm, tn), lambda i, j: (i, j)),
                  pl.BlockSpec((1, 1, tn), lambda i, j: (mod_row_of_block(i), 0, j))],
        out_specs=pl.BlockSpec((tm, tn), lambda i, j: (i, j)),
        compiler_params=_cparams(("parallel", "arbitrary")),
        name="out_proj",
    )(a, w, x2, g_tab)


def _ffn_down_kernel(a_ref, w_ref, x_ref, g_ref, nf_ref, o_ref, acc_ref):
    kk = pl.program_id(1)

    @pl.when(kk == 0)
    def _():
        acc_ref[...] = jnp.zeros_like(acc_ref)

    acc_ref[...] += _dot(a_ref[...], w_ref[...])

    @pl.when(kk == pl.num_programs(1) - 1)
    def _():
        h = x_ref[...] + g_ref[0] * acc_ref[...]
        ms = jnp.mean(h * h, axis=-1, keepdims=True)
        o_ref[...] = h * lax.rsqrt(ms + NORM_EPS) * nf_ref[...]


def _ffn_down(a, w, x2, g_tab, mod_row_of_block, norm_f, *, tm, tk):
    m, k = a.shape
    n = w.shape[1]
    return pl.pallas_call(
        _ffn_down_kernel,
        out_shape=jax.ShapeDtypeStruct((m, n), F32),
        grid=(m // tm, k // tk),
        in_specs=[pl.BlockSpec((tm, tk), lambda i, kk: (i, kk)),
                  pl.BlockSpec((tk, n), lambda i, kk: (kk, 0)),
                  pl.BlockSpec((tm, n), lambda i, kk: (i, 0)),
                  pl.BlockSpec((1, 1, n), lambda i, kk: (mod_row_of_block(i), 0, 0)),
                  pl.BlockSpec((1, n), lambda i, kk: (0, 0))],
        out_specs=pl.BlockSpec((tm, n), lambda i, kk: (i, 0)),
        scratch_shapes=[pltpu.VMEM((tm, n), F32)],
        compiler_params=_cparams(("parallel", "arbitrary")),
        name="ffn_down",
    )(a, w, x2, g_tab, norm_f.reshape(1, n))


def kernel(x, c, ctx, c_ctx, ada_w, ada_b, norm1_w, w_in, rw_mu, s5_a_re, s5_a_im, s5_log_dt, s5_b_re, s5_b_im, s5_c_re, s5_c_im, s5_d, s5_glu_w, rw_w0, rw_w2, rw_a0, rw_a2, rw_g2, rw_k_k, rw_k_a, rw_r_k, rw_ln_w, rw_ln_b, rw_proj, w_o, norm2_w, ffn_w13, ffn_w2, norm_f):
    assert ada_w.shape[0] == 1, "single-layer block"
    bsz, l_lat, d = x.shape
    l_ctx = ctx.shape[1]
    l_all = l_ctx + l_lat
    s5w = s5_d.shape[1] * s5_d.shape[2]
    rw = rw_g2.shape[2]
    shift_cols = rw_mu.shape[1]
    d_ff = ffn_w2.shape[1]
    n_heads = rw // RW_HEAD

    c_rows = jnp.concatenate([c, c_ctx[None], jnp.zeros((8 - bsz - 1, d), F32)], axis=0)
    mod = _modulation(c_rows, ada_w[0], ada_b[0])
    tab = lambda k: mod[:, k * d:(k + 1) * d].reshape(8, 1, d)
    sh1, sc1, g1, sh2, sc2, g2 = (tab(k) for k in range(N_MOD))
    ctx_row = bsz

    tm_in = 512
    m_lat = bsz * l_lat
    x2 = x.reshape(m_lat, d)
    n_mix = s5w + shift_cols
    m_all = m_lat + bsz * l_ctx
    lat_row = lambda t: (lambda i: i // (l_lat // t))

    def mix_mod_row(i):
        return jnp.where(i >= m_lat // tm_in, ctx_row, i // (l_lat // tm_in))

    h_all = _lnmod(x2, ctx.reshape(bsz * l_ctx, d), norm1_w[0], sh1, sc1, mix_mod_row, tm=tm_in)
    tm_all = m_all // 8
    u_blk = _wres_matmul(h_all, w_in[0], 0, s5w, rows=m_all, tm=tm_all, tn=s5w,
                         out_dtype=F32, epilogue="lane_blocks", name="in_proj_s5")
    z_rw = _wres_matmul(h_all, w_in[0], s5w, shift_cols, rows=m_all, tm=tm_all, tn=shift_cols // 3,
                        out_dtype=BF16, name="in_proj_rw")
    tm = 1024
    gates = _wres_matmul(h_all, w_in[0], n_mix, w_in.shape[2] - n_mix, rows=m_lat, tm=tm, tn=1024,
                         out_dtype=BF16, epilogue="sigmoid", name="in_proj_gates")

    y_blk = _s5_branch_blocked(u_blk, bsz, l_ctx, l_lat, s5_a_re[0], s5_a_im[0], s5_log_dt[0],
                               s5_b_re[0], s5_b_im[0], s5_c_re[0], s5_c_im[0], s5_d[0])
    s5_out = _s5_glu(y_blk, s5_glu_w[0].astype(BF16), tm=tm, tn=1024)

    lora = rw_w2.shape[2]
    zl = jnp.zeros((lora, rw), F32)
    w2bd = jnp.concatenate([jnp.concatenate([rw_w2[0, 0], zl], axis=1),
                            jnp.concatenate([zl, rw_w2[0, 1]], axis=1)], axis=0)
    a2bd = jnp.concatenate([jnp.concatenate([rw_a2[0, 0], zl], axis=1),
                            jnp.concatenate([zl, rw_a2[0, 1]], axis=1)], axis=0)
    head_of = jnp.arange(rw) // RW_HEAD
    seg = (head_of[:, None] == jnp.arange(LANES)[None, :]).astype(BF16)
    segt = seg.T
    r, v, kk, g, bonus, lw, kd, be = _rw_prep(
        z_rw, bsz, l_ctx, l_lat, rw_mu[0], w2bd, a2bd, rw_g2[0], rw_w0[0].reshape(1, 2 * rw),
        rw_a0[0].reshape(1, 2 * rw), rw_k_k[0], rw_k_a[0], rw_r_k[0].reshape(rw), seg, segt)
    y_f, y_b = _rw_scan2(r, v, kk, lw, kd, be, l_ctx)

    merged = _rw_merge(y_f.reshape(m_lat, rw), y_b.reshape(m_lat, rw), bonus.reshape(m_lat, rw), g.reshape(m_lat, rw),
                       rw_ln_w[0], rw_ln_b[0], seg, segt, gates, s5_out, rw_proj[0].astype(BF16),
                       tm=512, tn=1024)
    tm_o = 512
    h1, h1n = _resid_matmul_ln(merged, w_o[0].astype(BF16), x2, g1, norm2_w[0], sh2, sc2, lat_row(tm_o), tm=tm_o)

    act = _wres_swiglu(h1n, ffn_w13[0], d_ff, tm=tm, tn=512)
    tm_dn = 512
    out = _ffn_down(act, ffn_w2[0].astype(BF16), h1, g2, lat_row(tm_dn), norm_f, tm=tm_dn, tk=d_ff // 4)
    return out.reshape(bsz, l_lat, d)
```

```python
import functools
import math

import jax
import jax.numpy as jnp
from jax import lax
from jax.experimental import pallas as pl
from jax.experimental.pallas import tpu as pltpu

F32 = jnp.float32
BF16 = jnp.bfloat16
HIGHEST = lax.Precision.HIGHEST

D_MODEL = 2048
N_MOD = 6
NORM_EPS = 1e-6
GN_EPS = 64e-5
GRID_W = 64
S5_GROUP = 16
S5_STATE = 64
S5_CHUNK = 16
RW_HEAD = 64
RW_CHUNK = 64
RW_SUB = 2
LANES = 128
VMEM_LIMIT = 48 * 1024 * 1024


def _cparams(sem):
    return pltpu.CompilerParams(dimension_semantics=sem, vmem_limit_bytes=VMEM_LIMIT)


def _operands(a, b, precision):
    if precision == "bf16":
        return a.astype(BF16), b.astype(BF16), None
    return a, b, precision


def _dot(a, b, precision=None):
    a, b, precision = _operands(a, b, precision)
    return jnp.dot(a, b, preferred_element_type=F32, precision=precision)


def _dot_nt(a, b, precision=None):
    a, b, precision = _operands(a, b, precision)
    return lax.dot_general(a, b, (((1,), (1,)), ((), ())), preferred_element_type=F32, precision=precision)


def _dot_hilo(a, ind):
    hi = a.astype(BF16)
    lo = (a - hi.astype(F32)).astype(BF16)
    return _dot(hi, ind) + _dot(lo, ind)


def _sigmoid(x):
    return 1.0 / (1.0 + jnp.exp(-x))


def _silu(x):
    return x * _sigmoid(x)


def _gelu_tanh(x):
    c = math.sqrt(2.0 / math.pi)
    return 0.5 * x * (1.0 + jnp.tanh(c * (x + 0.044715 * (x * x * x))))


def _softplus(x):
    return jnp.maximum(x, 0.0) + jnp.log(1.0 + jnp.exp(-jnp.abs(x)))


def _mod_kernel(c_ref, w_ref, b_ref, o_ref):
    o_ref[...] = _dot_hilo(_silu(c_ref[...]), w_ref[...].astype(BF16)) + b_ref[...]


def _modulation(c_rows, ada_w, ada_b):
    m, d = c_rows.shape
    n = ada_w.shape[1]
    tn = 1024
    return pl.pallas_call(
        _mod_kernel,
        out_shape=jax.ShapeDtypeStruct((m, n), F32),
        grid=(n // tn,),
        in_specs=[pl.BlockSpec((m, d), lambda j: (0, 0)),
                  pl.BlockSpec((d, tn), lambda j: (0, j)),
                  pl.BlockSpec((1, tn), lambda j: (0, j))],
        out_specs=pl.BlockSpec((m, tn), lambda j: (0, j)),
        compiler_params=_cparams(("arbitrary",)),
        name="modulation",
    )(c_rows, ada_w, ada_b.reshape(1, n))


def _lnmod_rows(x, nw, sh, sc):
    ms = jnp.mean(x * x, axis=-1, keepdims=True)
    y = x * lax.rsqrt(ms + NORM_EPS) * nw
    return y * (1.0 + sc) + sh


def _lnmod_mm_kernel(x_ref, nw_ref, sh_ref, sc_ref, w_ref, o_ref, h_scr, *, epilogue):
    @pl.when(pl.program_id(1) == 0)
    def _():
        h_scr[...] = _lnmod_rows(x_ref[...], nw_ref[...], sh_ref[0], sc_ref[0]).astype(BF16)

    z = _dot(h_scr[...], w_ref[...])
    if epilogue == "sigmoid":
        z = _sigmoid(z)
    if epilogue == "lane_blocks":
        for jb in range(o_ref.shape[0]):
            o_ref[jb] = z[:, jb * LANES:(jb + 1) * LANES].astype(o_ref.dtype)
    else:
        o_ref[...] = z.astype(o_ref.dtype)


def _lnmod_swiglu_kernel(x_ref, nw_ref, sh_ref, sc_ref, w1_ref, w3_ref, o_ref, h_scr):
    @pl.when(pl.program_id(1) == 0)
    def _():
        h_scr[...] = _lnmod_rows(x_ref[...], nw_ref[...], sh_ref[0], sc_ref[0]).astype(BF16)

    h = h_scr[...]
    o_ref[...] = (_silu(_dot(h, w1_ref[...])) * _dot(h, w3_ref[...])).astype(o_ref.dtype)


def _lnmod_matmul(x2, nw, sh_tab, sc_tab, mod_row_of_block, w, *, tm, tn, out_dtype, epilogue=None, name):
    m, d = x2.shape
    n = w.shape[1]
    mod_map = lambda i, j: (mod_row_of_block(i), 0, 0)
    if epilogue == "lane_blocks":
        assert tn == n
        out_shape = jax.ShapeDtypeStruct((n // LANES, m, LANES), out_dtype)
        out_spec = pl.BlockSpec((n // LANES, tm, LANES), lambda i, j: (0, i, 0))
    else:
        out_shape = jax.ShapeDtypeStruct((m, n), out_dtype)
        out_spec = pl.BlockSpec((tm, tn), lambda i, j: (i, j))
    return pl.pallas_call(
        functools.partial(_lnmod_mm_kernel, epilogue=epilogue),
        out_shape=out_shape,
        grid=(m // tm, n // tn),
        in_specs=[pl.BlockSpec((tm, d), lambda i, j: (i, 0)),
                  pl.BlockSpec((1, d), lambda i, j: (0, 0)),
                  pl.BlockSpec((1, 1, d), mod_map),
                  pl.BlockSpec((1, 1, d), mod_map),
                  pl.BlockSpec((d, tn), lambda i, j: (0, j))],
        out_specs=out_spec,
        scratch_shapes=[pltpu.VMEM((tm, d), BF16)],
        compiler_params=_cparams(("parallel", "arbitrary")),
        name=name,
    )(x2, nw.reshape(1, d), sh_tab, sc_tab, w)


def _lnmod_kernel(x_ref, c_ref, nw_ref, sh_ref, sc_ref, o_ref, *, n_lat_blocks):
    rows = jnp.where(pl.program_id(0) < n_lat_blocks, x_ref[...], c_ref[...])
    o_ref[...] = _lnmod_rows(rows, nw_ref[...], sh_ref[0], sc_ref[0]).astype(o_ref.dtype)


def _lnmod(x2, c2, nw, sh_tab, sc_tab, mod_row_of_block, *, tm):
    m, d = x2.shape
    mc = c2.shape[0]
    nx, nc = m // tm, mc // tm
    mod_map = lambda i: (mod_row_of_block(i), 0, 0)
    return pl.pallas_call(
        functools.partial(_lnmod_kernel, n_lat_blocks=nx),
        out_shape=jax.ShapeDtypeStruct((m + mc, d), BF16),
        grid=(nx + nc,),
        in_specs=[pl.BlockSpec((tm, d), lambda i: (jnp.minimum(i, nx - 1), 0)),
                  pl.BlockSpec((tm, d), lambda i: (jnp.maximum(i - nx, 0), 0)),
                  pl.BlockSpec((1, d), lambda i: (0, 0)),
                  pl.BlockSpec((1, 1, d), mod_map),
                  pl.BlockSpec((1, 1, d), mod_map)],
        out_specs=pl.BlockSpec((tm, d), lambda i: (i, 0)),
        compiler_params=_cparams(("parallel",)),
        name="lnmod",
    )(x2, c2, nw.reshape(1, d), sh_tab, sc_tab)


def _wres_mm_kernel(a_ref, w_ref, o_ref, w_scr, *, epilogue):
    @pl.when(pl.program_id(1) == 0)
    def _():
        w_scr[...] = w_ref[...].astype(BF16)

    z = _dot(a_ref[...], w_scr[...])
    if epilogue == "sigmoid":
        z = _sigmoid(z)
    if epilogue == "lane_blocks":
        for jb in range(o_ref.shape[0]):
            o_ref[jb] = z[:, jb * LANES:(jb + 1) * LANES].astype(o_ref.dtype)
    else:
        o_ref[...] = z.astype(o_ref.dtype)


def _wres_matmul(a, w, col0, n, *, rows, tm, tn, out_dtype, epilogue=None, name):
    k = a.shape[1]
    assert col0 % LANES == 0 and n % tn == 0 and rows % tm == 0
    if epilogue == "lane_blocks":
        assert tn == n
        out_shape = jax.ShapeDtypeStruct((n // LANES, rows, LANES), out_dtype)
        out_spec = pl.BlockSpec((n // LANES, tm, LANES), lambda j, i: (0, i, 0))
    else:
        out_shape = jax.ShapeDtypeStruct((rows, n), out_dtype)
        out_spec = pl.BlockSpec((tm, tn), lambda j, i: (i, j))
    return pl.pallas_call(
        functools.partial(_wres_mm_kernel, epilogue=epilogue),
        out_shape=out_shape,
        grid=(n // tn, rows // tm),
        in_specs=[pl.BlockSpec((tm, k), lambda j, i: (i, 0)),
                  pl.BlockSpec((pl.Element(k), pl.Element(tn)), lambda j, i: (0, pl.multiple_of(col0 + j * tn, LANES)))],
        out_specs=out_spec,
        scratch_shapes=[pltpu.VMEM((k, tn), BF16)],
        compiler_params=_cparams(("arbitrary", "arbitrary")),
        name=name,
    )(a, w)


def _lnmod_swiglu(x2, nw, sh_tab, sc_tab, mod_row_of_block, w13, d_ff, *, tm, tn, name):
    m, d = x2.shape
    nj = d_ff // tn
    mod_map = lambda i, j: (mod_row_of_block(i), 0, 0)
    return pl.pallas_call(
        _lnmod_swiglu_kernel,
        out_shape=jax.ShapeDtypeStruct((m, d_ff), BF16),
        grid=(m // tm, nj),
        in_specs=[pl.BlockSpec((tm, d), lambda i, j: (i, 0)),
                  pl.BlockSpec((1, d), lambda i, j: (0, 0)),
                  pl.BlockSpec((1, 1, d), mod_map),
                  pl.BlockSpec((1, 1, d), mod_map),
                  pl.BlockSpec((d, tn), lambda i, j: (0, j)),
                  pl.BlockSpec((d, tn), lambda i, j: (0, j + nj))],
        out_specs=pl.BlockSpec((tm, tn), lambda i, j: (i, j)),
        scratch_shapes=[pltpu.VMEM((tm, d), BF16)],
        compiler_params=_cparams(("parallel", "arbitrary")),
        name=name,
    )(x2, nw.reshape(1, d), sh_tab, sc_tab, w13, w13)


def _s5_param_kernel(are_ref, aim_ref, ldt_ref, bre_ref, bim_ref, cre_ref, cim_ref,
                     e_ref, cs_ref, kt_ref, a16_ref):
    t_n, hg, p_n = S5_CHUNK, S5_GROUP, S5_STATE
    tau = lax.broadcasted_iota(jnp.int32, (t_n, 1, p_n), 0).astype(F32)
    for d in range(2):
        a_re = are_ref[0, d:d + 1, :]
        a_im = aim_ref[0, d:d + 1, :]
        dt = jnp.exp(ldt_ref[0, d:d + 1, :])
        lam = a_re * dt
        th = a_im * dt
        er = jnp.exp(lam)
        ab_re = er * jnp.cos(th)
        ab_im = er * jnp.sin(th)
        den = a_re * a_re + a_im * a_im
        x_re = ab_re - 1.0
        co_re = (x_re * a_re + ab_im * a_im) / den
        co_im = (ab_im * a_re - x_re * a_im) / den
        bt_re = bre_ref[0, d]
        bt_im = bim_ref[0, d]
        bb_re = co_re * bt_re - co_im * bt_im
        bb_im = co_re * bt_im + co_im * bt_re
        c_re = cre_ref[0, d]
        c_im = cim_ref[0, d]

        def power(tv):
            mag = jnp.exp(tv * lam)
            return mag * jnp.cos(tv * th), mag * jnp.sin(tv * th)

        pw_re, pw_im = power(tau)
        cp_re = (c_re[None] * pw_re - c_im[None] * pw_im).reshape(t_n * hg, p_n)
        cp_im = (c_re[None] * pw_im + c_im[None] * pw_re).reshape(t_n * hg, p_n)
        kt_ref[0, d] = _dot_nt(bb_re, cp_re, HIGHEST) - _dot_nt(bb_im, cp_im, HIGHEST)

        te = (t_n - 1.0 - tau) if d == 0 else tau
        pe_re, pe_im = power(te)
        e_ref[0, d, :, 0:p_n] = (pe_re * bb_re[None] - pe_im * bb_im[None]).reshape(t_n * hg, p_n)
        e_ref[0, d, :, p_n:2 * p_n] = (pe_re * bb_im[None] + pe_im * bb_re[None]).reshape(t_n * hg, p_n)

        tc = (tau + 1.0) if d == 0 else (t_n - tau)
        pc_re, pc_im = power(tc)
        cs_ref[0, d, :, 0:p_n] = (c_re[None] * pc_re - c_im[None] * pc_im).reshape(t_n * hg, p_n)
        cs_ref[0, d, :, p_n:2 * p_n] = -(c_re[None] * pc_im + c_im[None] * pc_re).reshape(t_n * hg, p_n)

        mag16 = jnp.exp(float(t_n) * lam)
        a16_ref[0, d, 0:1, :] = mag16 * jnp.cos(float(t_n) * th)
        a16_ref[0, d, 1:2, :] = mag16 * jnp.sin(float(t_n) * th)


def _s5_params(a_re, a_im, log_dt, b_re, b_im, c_re, c_im):
    g_n = a_re.shape[1]
    p_n, hg, t_n = S5_STATE, S5_GROUP, S5_CHUNK
    tr = lambda a: jnp.swapaxes(a, 0, 1)
    ldt = jnp.broadcast_to(tr(log_dt)[:, :, None], (g_n, 2, p_n))
    spec3 = pl.BlockSpec((1, 2, p_n), lambda g: (g, 0, 0))
    spec4 = pl.BlockSpec((1, 2, hg, p_n), lambda g: (g, 0, 0, 0))
    th = t_n * hg
    return pl.pallas_call(
        _s5_param_kernel,
        out_shape=(jax.ShapeDtypeStruct((g_n, 2, th, 2 * p_n), F32),
                   jax.ShapeDtypeStruct((g_n, 2, th, 2 * p_n), F32),
                   jax.ShapeDtypeStruct((g_n, 2, hg, th), F32),
                   jax.ShapeDtypeStruct((g_n, 2, 2, p_n), F32)),
        grid=(g_n,),
        in_specs=[spec3, spec3, spec3, spec4, spec4, spec4, spec4],
        out_specs=(pl.BlockSpec((1, 2, th, 2 * p_n), lambda g: (g, 0, 0, 0)),
                   pl.BlockSpec((1, 2, th, 2 * p_n), lambda g: (g, 0, 0, 0)),
                   pl.BlockSpec((1, 2, hg, th), lambda g: (g, 0, 0, 0)),
                   pl.BlockSpec((1, 2, 2, p_n), lambda g: (g, 0, 0, 0))),
        compiler_params=_cparams(("parallel",)),
        name="s5_params",
    )(tr(a_re), tr(a_im), ldt,
      jnp.transpose(b_re, (1, 0, 3, 2)), jnp.transpose(b_im, (1, 0, 3, 2)), tr(c_re), tr(c_im))


def _s5_state_in_kernel(u_ref, e_ref, o_ref):
    e = e_ref[0]
    u = u_ref[0]
    o_ref[0, :, 0:LANES] = _dot(u, e[0], HIGHEST)
    o_ref[0, :, LANES:2 * LANES] = _dot(u, e[1], HIGHEST)


def _s5_state_inputs(u_g, e):
    g_n, rows, th = u_g.shape
    return pl.pallas_call(
        _s5_state_in_kernel,
        out_shape=jax.ShapeDtypeStruct((g_n, rows, 2 * LANES), F32),
        grid=(g_n,),
        in_specs=[pl.BlockSpec((1, rows, th), lambda g: (g, 0, 0)),
                  pl.BlockSpec((1, 2, th, LANES), lambda g: (g, 0, 0, 0))],
        out_specs=pl.BlockSpec((1, rows, 2 * LANES), lambda g: (g, 0, 0)),
        compiler_params=_cparams(("parallel",)),
        name="s5_state_inputs",
    )(u_g, e)


def _s5_scan_kernel(e_ref, a_ref, o_ref, *, n_ctx, n_all):
    zero = jnp.zeros(e_ref.shape[2:], F32)

    def step(pr, pi):
        ar, ai = a_ref[pr], a_ref[pi]

        def body(c, carry):
            sr, si = carry
            o_ref[pr, c] = sr
            o_ref[pi, c] = si
            return (ar * sr - ai * si + e_ref[pr, c], ar * si + ai * sr + e_ref[pi, c])
        return body

    lax.fori_loop(0, n_all, step(0, 1), (zero, zero))
    bwd = step(2, 3)
    carry = lax.fori_loop(0, n_ctx, lambda k, cy: bwd(n_ctx - 1 - k, cy), (zero, zero))
    lax.fori_loop(0, n_all - n_ctx, lambda k, cy: bwd(n_all - 1 - k, cy), carry)


def _s5_scan(e_planes, a_planes, n_ctx, n_all):
    _, rows, r_n, _ = e_planes.shape
    bsz = rows // n_all
    sub = 8
    return pl.pallas_call(
        functools.partial(_s5_scan_kernel, n_ctx=n_ctx, n_all=n_all),
        out_shape=jax.ShapeDtypeStruct(e_planes.shape, F32),
        grid=(bsz, r_n // sub),
        in_specs=[pl.BlockSpec((4, n_all, sub, LANES), lambda b, q: (0, b, q, 0)),
                  pl.BlockSpec((4, sub, LANES), lambda b, q: (0, q, 0))],
        out_specs=pl.BlockSpec((4, n_all, sub, LANES), lambda b, q: (0, b, q, 0)),
        compiler_params=_cparams(("parallel", "parallel")),
        name="s5_scan",
    )(e_planes, a_planes)


def _s5_apply_kernel(u_ref, s_ref, mf_ref, mb_ref, d_ref, cs_ref, o_ref, *, n_ctx):
    u = u_ref[0, n_ctx:, :]
    s = s_ref[0, n_ctx:, :]
    y = _dot(u, mf_ref[0] + mb_ref[0], HIGHEST) + u * d_ref[0]
    y = y + _dot_nt(s[:, 0:LANES], cs_ref[0, 0], HIGHEST) + _dot_nt(s[:, LANES:2 * LANES], cs_ref[0, 1], HIGHEST)
    o_ref[0, 0] = y


def _s5_apply(u_g, s_g, m_f, m_b, d_t, cs, n_ctx, n_all):
    g_n, rows, th = u_g.shape
    bsz = rows // n_all
    n_lat = n_all - n_ctx
    return pl.pallas_call(
        functools.partial(_s5_apply_kernel, n_ctx=n_ctx),
        out_shape=jax.ShapeDtypeStruct((g_n, bsz, n_lat, th), F32),
        grid=(g_n, bsz),
        in_specs=[pl.BlockSpec((1, n_all, th), lambda g, b: (g, b, 0)),
                  pl.BlockSpec((1, n_all, 2 * LANES), lambda g, b: (g, b, 0)),
                  pl.BlockSpec((1, th, th), lambda g, b: (g, 0, 0)),
                  pl.BlockSpec((1, th, th), lambda g, b: (g, 0, 0)),
                  pl.BlockSpec((1, 1, th), lambda g, b: (g, 0, 0)),
                  pl.BlockSpec((1, 2, th, LANES), lambda g, b: (g, 0, 0, 0))],
        out_specs=pl.BlockSpec((1, 1, n_lat, th), lambda g, b: (g, b, 0, 0)),
        compiler_params=_cparams(("parallel", "arbitrary")),
        name="s5_apply",
    )(u_g, s_g, m_f, m_b, d_t, cs)


def _s5_branch(u_all, n_ctx_tok, a_re, a_im, log_dt, b_re, b_im, c_re, c_im, s5_d):
    bsz, l_all, width = u_all.shape
    hg, t_n, p_n = S5_GROUP, S5_CHUNK, S5_STATE
    g_n = width // hg
    n_all = l_all // t_n
    n_ctx = n_ctx_tok // t_n
    th = t_n * hg
    e, cs, kt, a16 = _s5_params(a_re, a_im, log_dt, b_re, b_im, c_re, c_im)

    kt5 = kt.reshape(g_n, 2, hg, t_n, hg)
    ii = jnp.arange(t_n)[:, None]
    jj = jnp.arange(t_n)[None, :]

    def toeplitz(k4, lag, keep):
        m = k4[:, :, jnp.clip(lag, 0, t_n - 1), :]
        m = jnp.where(keep[None, None, :, :, None], m, 0.0)
        return jnp.transpose(m, (0, 2, 1, 3, 4)).reshape(g_n, th, th)

    m_f = toeplitz(kt5[:, 0], jj - ii, jj >= ii)
    m_b = toeplitz(kt5[:, 1], ii - jj, ii >= jj)
    d_t = jnp.tile(s5_d, (1, t_n)).reshape(g_n, 1, th)

    u_g = jnp.transpose(u_all.reshape(bsz, n_all, t_n, g_n, hg), (3, 0, 1, 2, 4)).reshape(g_n, bsz * n_all, th)
    e_cat = e
    s_in = _s5_state_inputs(u_g, e_cat)
    rows = bsz * n_all
    planes = jnp.transpose(s_in.reshape(g_n, rows, 4, p_n), (2, 1, 0, 3)).reshape(4, rows, g_n * p_n // LANES, LANES)
    a_pl = jnp.transpose(a16.reshape(g_n, 4, p_n), (1, 0, 2)).reshape(4, g_n * p_n // LANES, LANES)
    st = _s5_scan(planes, a_pl, n_ctx, n_all)
    s_g = jnp.transpose(st.reshape(4, rows, g_n, p_n), (2, 1, 0, 3)).reshape(g_n, rows, 4 * p_n)
    y_g = _s5_apply(u_g, s_g, m_f, m_b, d_t, cs, n_ctx, n_all)
    n_lat = n_all - n_ctx
    y = jnp.transpose(y_g.reshape(g_n, bsz, n_lat, t_n, hg), (1, 2, 3, 0, 4))
    return y.reshape(bsz, n_lat * t_n, width)


def _s5_param2_kernel(are_ref, aim_ref, ldt_ref, bre_ref, bim_ref, cre_ref, cim_ref,
                      e_ref, c_ref, m_ref, a16_ref):
    t_n, hg, p_n = S5_CHUNK, S5_GROUP, S5_STATE
    tau = lax.broadcasted_iota(jnp.int32, (t_n, 1, p_n), 0).astype(F32)
    taps = []
    for d in range(2):
        a_re = are_ref[0, d:d + 1, :]
        a_im = aim_ref[0, d:d + 1, :]
        dt = jnp.exp(ldt_ref[0, d:d + 1, :])
        lam = a_re * dt
        th = a_im * dt
        er = jnp.exp(lam)
        ab_re = er * jnp.cos(th)
        ab_im = er * jnp.sin(th)
        den = a_re * a_re + a_im * a_im
        x_re = ab_re - 1.0
        co_re = (x_re * a_re + ab_im * a_im) / den
        co_im = (ab_im * a_re - x_re * a_im) / den
        bt_re = bre_ref[0, d].T
        bt_im = bim_ref[0, d].T
        bb_re = co_re * bt_re - co_im * bt_im
        bb_im = co_re * bt_im + co_im * bt_re
        c_re = cre_ref[0, d]
        c_im = cim_ref[0, d]

        def power(tv):
            mag = jnp.exp(tv * lam)
            return mag * jnp.cos(tv * th), mag * jnp.sin(tv * th)

        pw_re, pw_im = power(tau if d == 0 else (t_n - 1.0 - tau))
        cp_re = (c_re[None] * pw_re - c_im[None] * pw_im).reshape(t_n * hg, p_n)
        cp_im = (c_re[None] * pw_im + c_im[None] * pw_re).reshape(t_n * hg, p_n)
        taps.append(_dot_nt(bb_re, cp_re, HIGHEST) - _dot_nt(bb_im, cp_im, HIGHEST))

        lo, hi = 2 * d * p_n, (2 * d + 1) * p_n
        pe_re, pe_im = power((t_n - 1.0 - tau) if d == 0 else tau)
        e_ref[0, :, 0, :, lo:hi] = pe_re * bb_re[None] - pe_im * bb_im[None]
        e_ref[0, :, 0, :, hi:hi + p_n] = pe_re * bb_im[None] + pe_im * bb_re[None]
        pc_re, pc_im = power((tau + 1.0) if d == 0 else (t_n - tau))
        c_ref[0, :, 0, :, lo:hi] = c_re[None] * pc_re - c_im[None] * pc_im
        c_ref[0, :, 0, :, hi:hi + p_n] = -(c_re[None] * pc_im + c_im[None] * pc_re)

        mag16 = jnp.exp(float(t_n) * lam)
        a16_ref[0, d, 0:1, :] = mag16 * jnp.cos(float(t_n) * th)
        a16_ref[0, d, 1:2, :] = mag16 * jnp.sin(float(t_n) * th)

    width = t_n * hg
    lane = lax.broadcasted_iota(jnp.int32, (hg, width), 1)
    for t in range(t_n):
        sf = t * hg
        sb = (t_n - 1 - t) * hg
        f = taps[0] if sf == 0 else jnp.where(lane >= sf, pltpu.roll(taps[0], sf, 1), 0.0)
        b = taps[1] if sb == 0 else jnp.where(lane < width - sb, pltpu.roll(taps[1], width - sb, 1), 0.0)
        m_ref[0, t, 0] = f + b


def _s5_params2(a_re, a_im, log_dt, b_re, b_im, c_re, c_im, nb):
    g_n = a_re.shape[1]
    gl_n = g_n // nb
    p_n, hg, t_n = S5_STATE, S5_GROUP, S5_CHUNK
    tr = lambda a: jnp.swapaxes(a, 0, 1)
    ldt = jnp.broadcast_to(tr(log_dt)[:, :, None], (g_n, 2, p_n))
    spec3 = pl.BlockSpec((1, 2, p_n), lambda g: (g, 0, 0))
    spec4 = pl.BlockSpec((1, 2, hg, p_n), lambda g: (g, 0, 0, 0))
    spec_b = pl.BlockSpec((1, 2, p_n, hg), lambda g: (g, 0, 0, 0))
    wide = 4 * p_n
    comp = jax.ShapeDtypeStruct((nb, t_n, gl_n, hg, wide), F32)
    comp_spec = pl.BlockSpec((1, t_n, 1, hg, wide), lambda g: (g // gl_n, 0, g % gl_n, 0, 0))
    return pl.pallas_call(
        _s5_param2_kernel,
        out_shape=(comp, comp, comp, jax.ShapeDtypeStruct((g_n, 2, 2, p_n), F32)),
        grid=(g_n,),
        in_specs=[spec3, spec3, spec3, spec_b, spec_b, spec4, spec4],
        out_specs=(comp_spec, comp_spec, comp_spec, pl.BlockSpec((1, 2, 2, p_n), lambda g: (g, 0, 0, 0))),
        compiler_params=_cparams(("parallel",)),
        name="s5_params",
    )(tr(a_re), tr(a_im), ldt, tr(b_re), tr(b_im), tr(c_re), tr(c_im))


def _expand_block_diag(comp, rep_ref, mask_ref, w_scr):
    k = w_scr.shape[0]
    period = mask_ref.shape[1]
    cb = comp.astype(BF16)
    step = 512
    for c0 in range(0, k, step):
        blk = _dot(cb, rep_ref[:, c0:c0 + step]).astype(BF16)
        for q0 in range(0, step, period):
            w_scr[:, c0 + q0:c0 + q0 + period] = blk[:, q0:q0 + period] * mask_ref[...]


def _gather_chunk_rows(u_ref, rows, dst):
    t_n = S5_CHUNK
    for t in range(t_n):
        dst[:, t * LANES:(t + 1) * LANES] = u_ref[0, pl.ds(t, rows, stride=t_n), :].astype(dst.dtype)


def _s5_ein_kernel(u_ref, ec_ref, rep_ref, mask_ref, o_ref, w_scr, u_scr):
    _expand_block_diag(ec_ref[0], rep_ref, mask_ref, w_scr)
    _gather_chunk_rows(u_ref, u_scr.shape[0], u_scr)
    o_ref[0] = _dot(u_scr[...], w_scr[...])


def _s5_chunk_inputs(u_blk, e_comp, rep_e, mask_e):
    nb, m_all, _ = u_blk.shape
    k, cw = e_comp.shape[1:]
    rows = m_all // S5_CHUNK
    return pl.pallas_call(
        _s5_ein_kernel,
        out_shape=jax.ShapeDtypeStruct((nb, rows, k), F32),
        grid=(nb,),
        in_specs=[pl.BlockSpec((1, m_all, LANES), lambda j: (j, 0, 0)),
                  pl.BlockSpec((1, k, cw), lambda j: (j, 0, 0)),
                  pl.BlockSpec(rep_e.shape, lambda j: (0, 0)),
                  pl.BlockSpec(mask_e.shape, lambda j: (0, 0))],
        out_specs=pl.BlockSpec((1, rows, k), lambda j: (j, 0, 0)),
        scratch_shapes=[pltpu.VMEM((k, k), BF16), pltpu.VMEM((rows, k), BF16)],
        compiler_params=_cparams(("parallel",)),
        name="s5_chunk_inputs",
    )(u_blk, e_comp, rep_e, mask_e)


def _s5_bscan_kernel(e_ref, a_ref, o_ref, *, bsz, n_ctx, n_lat):
    q = e_ref.shape[2] // 4
    planes = lambda row, d: (row[:, (2 * d) * q:(2 * d + 1) * q], row[:, (2 * d + 1) * q:(2 * d + 2) * q])
    coef = [planes(a_ref[0], d) for d in range(2)]
    ctx0 = bsz * n_lat

    def advance(state, rows):
        new = []
        for (sr, si), (b, d), row in zip(state, [(b, d) for b in range(bsz) for d in range(2)], rows):
            ar, ai = coef[d]
            er, ei = planes(e_ref[0, pl.ds(row, 1), :], d)
            new.append((ar * sr - ai * si + er, ar * si + ai * sr + ei))
        return tuple(new)

    def ctx_step(s, state):
        rows = [ctx0 + b * n_ctx + (s if d == 0 else n_ctx - 1 - s) for b in range(bsz) for d in range(2)]
        return advance(state, rows)

    def lat_step(s, state):
        rows = [b * n_lat + (s if d == 0 else n_lat - 1 - s) for b in range(bsz) for d in range(2)]
        for (sr, si), (b, d), row in zip(state, [(b, d) for b in range(bsz) for d in range(2)], rows):
            o_ref[0, pl.ds(row, 1), (2 * d) * q:(2 * d + 1) * q] = sr
            o_ref[0, pl.ds(row, 1), (2 * d + 1) * q:(2 * d + 2) * q] = si
        return advance(state, rows)

    zero = jnp.zeros((1, q), F32)
    state = tuple((zero, zero) for _ in range(2 * bsz))
    state = lax.fori_loop(0, n_ctx, ctx_step, state)
    lax.fori_loop(0, n_lat, lat_step, state)


def _s5_bscan(e_rows, a_rows, bsz, n_ctx, n_lat):
    nb, rows, k = e_rows.shape
    return pl.pallas_call(
        functools.partial(_s5_bscan_kernel, bsz=bsz, n_ctx=n_ctx, n_lat=n_lat),
        out_shape=jax.ShapeDtypeStruct((nb, bsz * n_lat, k), F32),
        grid=(nb,),
        in_specs=[pl.BlockSpec((1, rows, k), lambda j: (j, 0, 0)),
                  pl.BlockSpec((1, 1, k), lambda j: (j, 0, 0))],
        out_specs=pl.BlockSpec((1, bsz * n_lat, k), lambda j: (j, 0, 0)),
        compiler_params=_cparams(("parallel",)),
        name="s5_scan",
    )(e_rows, a_rows)


def _s5_out_kernel(u_ref, s_ref, mc_ref, cc_ref, d_ref, rep_m_ref, mask_m_ref, rep_e_ref, mask_e_ref,
                   o_ref, wm_scr, wc_scr, u_scr):
    @pl.when(pl.program_id(1) == 0)
    def _():
        _expand_block_diag(mc_ref[0], rep_m_ref, mask_m_ref, wm_scr)
        _expand_block_diag(cc_ref[0], rep_e_ref, mask_e_ref, wc_scr)

    t_n = S5_CHUNK
    rows = s_ref.shape[1]
    _gather_chunk_rows(u_ref, rows, u_scr)
    u = u_scr[...]
    y = _dot(u.astype(BF16), wm_scr[...]) + _dot_nt(s_ref[0].astype(BF16), wc_scr[...])
    y = y + u * d_ref[0]
    for t in range(t_n):
        o_ref[0, pl.ds(t, rows, stride=t_n), :] = y[:, t * LANES:(t + 1) * LANES]


def _s5_outputs(u_blk, s_rows, m_comp, c_comp, d_rows, rep_m, mask_m, rep_e, mask_e):
    nb = u_blk.shape[0]
    rows, k = s_rows.shape[1:]
    cw = m_comp.shape[2]
    tr = rows // 2
    tok = tr * S5_CHUNK
    const = lambda a: pl.BlockSpec(a.shape, lambda j, i: (0, 0))
    return pl.pallas_call(
        _s5_out_kernel,
        out_shape=jax.ShapeDtypeStruct((nb, rows * S5_CHUNK, LANES), F32),
        grid=(nb, rows // tr),
        in_specs=[pl.BlockSpec((1, tok, LANES), lambda j, i: (j, i, 0)),
                  pl.BlockSpec((1, tr, k), lambda j, i: (j, i, 0)),
                  pl.BlockSpec((1, k, cw), lambda j, i: (j, 0, 0)),
                  pl.BlockSpec((1, k, cw), lambda j, i: (j, 0, 0)),
                  pl.BlockSpec((1, 1, k), lambda j, i: (j, 0, 0)),
                  const(rep_m), const(mask_m), const(rep_e), const(mask_e)],
        out_specs=pl.BlockSpec((1, tok, LANES), lambda j, i: (j, i, 0)),
        scratch_shapes=[pltpu.VMEM((k, k), BF16), pltpu.VMEM((k, k), BF16), pltpu.VMEM((tr, k), F32)],
        compiler_params=_cparams(("parallel", "arbitrary")),
        name="s5_outputs",
    )(u_blk, s_rows, m_comp, c_comp, d_rows, rep_m, mask_m, rep_e, mask_e)


def _s5_branch_blocked(u_blk, bsz, l_ctx, l_lat, a_re, a_im, log_dt, b_re, b_im, c_re, c_im, s5_d):
    nb, m_all, _ = u_blk.shape
    hg, t_n, p_n = S5_GROUP, S5_CHUNK, S5_STATE
    g_n = a_re.shape[1]
    gl_n = g_n // nb
    k = t_n * LANES
    n_lat = l_lat // t_n
    n_ctx = l_ctx // t_n
    e_c, c_c, m_c, a16 = _s5_params2(a_re, a_im, log_dt, b_re, b_im, c_re, c_im, nb)
    cw = 4 * p_n
    e_comp, c_comp, m_comp = (a.reshape(nb, k, cw) for a in (e_c, c_c, m_c))

    row_gl = (jnp.arange(k) // hg) % gl_n
    col = jnp.arange(k)
    src = jnp.arange(cw)
    rep_e = ((src[:, None] // p_n == col[None, :] // (gl_n * p_n)) & (src[:, None] % p_n == col[None, :] % p_n)).astype(BF16)
    mask_e = (row_gl[:, None] == (jnp.arange(gl_n * p_n)[None, :] // p_n)).astype(BF16)
    rep_m = ((src[:, None] // hg == col[None, :] // (gl_n * hg)) & (src[:, None] % hg == col[None, :] % hg)).astype(BF16)
    mask_m = (row_gl[:, None] == (jnp.arange(gl_n * hg)[None, :] // hg)).astype(BF16)

    d_rows = jnp.tile(s5_d.reshape(nb, 1, gl_n * hg), (1, 1, t_n))
    a_rows = jnp.transpose(a16.reshape(nb, gl_n, 2, 2, p_n), (0, 2, 3, 1, 4)).reshape(nb, 1, 4 * gl_n * p_n)

    e_rows = _s5_chunk_inputs(u_blk, e_comp, rep_e, mask_e)
    s_rows = _s5_bscan(e_rows, a_rows, bsz, n_ctx, n_lat)
    return _s5_outputs(u_blk, s_rows, m_comp, c_comp, d_rows, rep_m, mask_m, rep_e, mask_e)


def _rw_prep_kernel(z_ref, zp_ref, zn_ref, mu_ref, w2_ref, a2_ref, g2_ref, w0_ref, a0_ref,
                    kk_w_ref, ka_ref, rk_ref, seg_ref, segt_ref,
                    r_ref, v_ref, kk_ref, g_ref, bonus_ref, lw_ref, kd_ref, be_ref,
                    *, tm, l_lat, rw):
    j = pl.program_id(1)
    z = z_ref[...].astype(F32)
    lat = j > 0
    tl = lax.broadcasted_iota(jnp.int32, (tm, 1), 0)
    tok = (j - 1) * tm + tl
    col = tl % GRID_W
    m_l = jnp.where(lat, col, tl) > 0
    m_r = jnp.where(lat, col - (GRID_W - 1), tl - (tm - 1)) < 0
    m_u = jnp.logical_and(lat, tok >= GRID_W)
    m_d = jnp.logical_and(lat, tok < l_lat - GRID_W)
    z_ext = jnp.concatenate([zp_ref[...], z_ref[...], zn_ref[...]], axis=0)
    rel = (lax.broadcasted_iota(jnp.int32, (tm, tm + 2 * GRID_W), 1) - GRID_W
           - lax.broadcasted_iota(jnp.int32, (tm, tm + 2 * GRID_W), 0))
    pick = (jnp.logical_and(rel == -1, m_l) | jnp.logical_and(rel == 1, m_r)
            | jnp.logical_and(rel == -GRID_W, m_u) | jnp.logical_and(rel == GRID_W, m_d))
    s = _dot(jnp.where(pick, 1.0, 0.0).astype(z_ext.dtype), z_ext)
    cnt = (m_l.astype(F32) + m_r.astype(F32)) + (m_u.astype(F32) + m_d.astype(F32))
    zs = z + (s * (1.0 / cnt) - z) * mu_ref[...]

    r = zs[:, 0:rw]
    k = zs[:, rw:2 * rw]
    v = zs[:, 2 * rw:3 * rw]
    o = 3 * rw
    wd = zs[:, o:o + LANES]
    ad = zs[:, o + LANES:o + 2 * LANES]
    gd = zs[:, o + 2 * LANES:o + 3 * LANES]

    seg = seg_ref[...]
    segt = segt_ref[...]

    def head_sum(t):
        return _dot_hilo(_dot_hilo(t, seg), segt)

    g_ref[0] = _dot(_sigmoid(gd), g2_ref[...], "bf16").astype(g_ref.dtype)
    kk = k * kk_w_ref[...]
    kk = kk * lax.rsqrt(head_sum(kk * kk) + 1e-12)
    wl = w0_ref[...] + _dot(jnp.tanh(wd), w2_ref[...], "bf16")
    al = a0_ref[...] + _dot(ad, a2_ref[...], "bf16")
    r_ref[0] = r.astype(r_ref.dtype)
    v_ref[0] = v.astype(v_ref.dtype)
    kk_ref[0] = kk.astype(kk_ref.dtype)
    k_sum = jnp.zeros_like(r)
    for d in range(2):
        a = _sigmoid(al[:, d * rw:(d + 1) * rw])
        k_d = k * (1.0 + (a - 1.0) * ka_ref[...])
        k_sum = k_sum + k_d
        lw_ref[d, 0] = -math.exp(-0.5) * _sigmoid(wl[:, d * rw:(d + 1) * rw])
        kd_ref[d, 0] = k_d.astype(kd_ref.dtype)
        be_ref[d, 0] = (kk * a).astype(be_ref.dtype)
    bonus_ref[0] = (head_sum(r * rk_ref[...] * k_sum) * v).astype(bonus_ref.dtype)


def _rw_prep(z_rw, bsz, l_ctx, l_lat, mu, w2bd, a2bd, g2, w0cat, a0cat, k_k, k_a, r_k_flat, seg, segt):
    cols = z_rw.shape[1]
    tm = l_ctx
    rw = g2.shape[1]
    l_all = l_ctx + l_lat
    nblk = l_all // tm
    lat_blk = l_lat // tm
    hb = tm // GRID_W
    lat_hblk = l_lat // GRID_W

    def main_blk(b, j):
        return jnp.where(j == 0, bsz * lat_blk + b, b * lat_blk + j - 1)

    def prev_halo(b, j):
        return b * lat_hblk + jnp.maximum((j - 1) * hb - 1, 0)

    def next_halo(b, j):
        return b * lat_hblk + jnp.minimum(jnp.maximum(j, 1) * hb, lat_hblk - 1)

    full = lambda shape: pl.BlockSpec(shape, lambda b, j: (0,) * len(shape))
    shared = jax.ShapeDtypeStruct((bsz, l_all, rw), BF16)
    lat_only = jax.ShapeDtypeStruct((bsz, l_lat, rw), BF16)
    per_dir = jax.ShapeDtypeStruct((2, bsz, l_all, rw), BF16)
    per_dir_f32 = jax.ShapeDtypeStruct((2, bsz, l_all, rw), F32)
    o_shared = pl.BlockSpec((1, tm, rw), lambda b, j: (b, j, 0))
    o_lat = pl.BlockSpec((1, tm, rw), lambda b, j: (b, jnp.maximum(j - 1, 0), 0))
    o_dir = pl.BlockSpec((2, 1, tm, rw), lambda b, j: (0, b, j, 0))
    return pl.pallas_call(
        functools.partial(_rw_prep_kernel, tm=tm, l_lat=l_lat, rw=rw),
        out_shape=(shared,) * 3 + (lat_only,) * 2 + (per_dir_f32, per_dir, per_dir),
        grid=(bsz, nblk),
        in_specs=[pl.BlockSpec((tm, cols), lambda b, j: (main_blk(b, j), 0)),
                  pl.BlockSpec((GRID_W, cols), lambda b, j: (prev_halo(b, j), 0)),
                  pl.BlockSpec((GRID_W, cols), lambda b, j: (next_halo(b, j), 0)),
                  full((1, cols)), full(w2bd.shape), full(a2bd.shape), full(g2.shape),
                  full((1, 2 * rw)), full((1, 2 * rw)), full((1, rw)), full((1, rw)), full((1, rw)),
                  full(seg.shape), full(segt.shape)],
        out_specs=(o_shared,) * 3 + (o_lat,) * 2 + (o_dir,) * 3,
        compiler_params=_cparams(("parallel", "arbitrary")),
        name="rwkv_prep",
    )(z_rw, z_rw, z_rw, mu.reshape(1, cols), w2bd, a2bd, g2, w0cat, a0cat,
      k_k.reshape(1, rw), k_a.reshape(1, rw), r_k_flat.reshape(1, rw), seg, segt)


def _stack_heads(x, head0):
    return jnp.concatenate([jnp.where(head0, x, 0.0), jnp.where(head0, 0.0, x)], axis=0)


def _rw_chunk_kernel(r_ref, v_ref, kk_ref, lw_ref, kd_ref, be_ref, y_ref,
                     z_scr, gc_s, lhs_s, kb_s, kht_s, bht_s, vs_s, rt_s, aab_s, akr_s, arb_s, t_s, x_s, xin_s,
                     yc_s, qy_s, bw_s, n_s, y_s, *, n_pairs, n_sub):
    c_n = RW_CHUNK
    n2 = 2 * c_n
    rev = (pl.program_id(0) % 2) == 1

    @pl.when(pl.program_id(1) == 0)
    def _():
        z_scr[...] = jnp.zeros_like(z_scr)

    ri = lax.broadcasted_iota(jnp.int32, (c_n, c_n), 0)
    ci = lax.broadcasted_iota(jnp.int32, (c_n, c_n), 1)
    tri = (jnp.where(rev, ci - ri, ri - ci) >= 0).astype(F32)
    r2 = lax.broadcasted_iota(jnp.int32, (n2, n2), 0)
    c2 = lax.broadcasted_iota(jnp.int32, (n2, n2), 1)
    t2 = r2 % c_n
    i2 = c2 % c_n
    same_head = (r2 // c_n) == (c2 // c_n)
    before = jnp.logical_and(same_head, jnp.where(rev, i2 - t2, t2 - i2) > 0)
    upto = jnp.logical_or(before, r2 == c2)
    eye = (r2 == c2).astype(F32)
    head0 = lax.broadcasted_iota(jnp.int32, (1, LANES), 1) < RW_HEAD

    def blk(s):
        return (r2 // s) == (c2 // s)

    pairs = range(n_pairs)
    units = range(n_sub * n_pairs)
    lanes_of = lambda p: slice(p * LANES, (p + 1) * LANES)

    row_of = lambda k: pl.multiple_of(jnp.where(rev, n_sub - 1 - k, k) * c_n, c_n)
    for cc in range(n_sub):
        rows = pl.ds(row_of(cc), c_n)
        lw = lw_ref[0, 0, rows, :]
        cum = _dot(tri, lw, HIGHEST)
        tot = jnp.sum(lw, axis=0, keepdims=True)
        g_inv = jnp.exp(-cum)
        g_hat = jnp.exp(tot - cum)
        kd = kd_ref[0, 0, rows, :]
        be = be_ref[0, 0, rows, :]
        at = kk_ref[0, rows, :] * jnp.exp(cum - lw)
        rt = r_ref[0, rows, :] * jnp.exp(cum)
        kt = kd * g_inv
        bt = be * g_inv
        kh = kd * g_hat
        bh = be * g_hat
        gc_s[cc] = jnp.exp(tot)
        v = v_ref[0, rows, :]
        for p in pairs:
            u = cc * n_pairs + p
            st = lambda x: _stack_heads(x[:, lanes_of(p)], head0)
            rt_p = st(rt)
            at_p = st(at).astype(BF16)
            lhs_s[u, :n2] = at_p
            lhs_s[u, n2:] = rt_p.astype(BF16)
            xin_s[u, :, :LANES] = at_p
            rt_s[u] = rt_p
            kb_s[u, :n2] = st(kt).astype(BF16)
            kb_s[u, n2:] = st(bt).astype(BF16)
            kht_s[u] = st(kh).T.astype(BF16)
            bht_s[u] = st(bh).T.astype(BF16)
            vs_s[u] = st(v).astype(BF16)

    for u in units:
        g = _dot_nt(lhs_s[u], kb_s[u])
        a_ab = jnp.where(before, g[:n2, n2:], 0.0)
        akr_s[u, :n2] = jnp.where(before, g[:n2, :n2], 0.0).astype(BF16)
        akr_s[u, n2:] = jnp.where(upto, g[n2:, :n2], 0.0).astype(BF16)
        arb_s[u] = jnp.where(upto, g[n2:, n2:], 0.0).astype(BF16)
        aab_s[u] = a_ab
        t_s[u] = eye - jnp.where(blk(2), a_ab, 0.0)

    s = 2
    while s < c_n:
        off = jnp.logical_and(blk(2 * s), jnp.logical_not(blk(s)))
        for u in units:
            x_s[u] = _dot(t_s[u].astype(BF16), jnp.where(off, aab_s[u], 0.0).astype(BF16)).astype(BF16)
        for u in units:
            t = t_s[u]
            t_s[u] = t - _dot(x_s[u], t.astype(BF16))
        s *= 2

    for u in units:
        av = _dot(akr_s[u], vs_s[u])
        xin_s[u, :, LANES:] = av[:n2].astype(BF16)
        yc_s[u] = av[n2:]

    for u in units:
        wu = _dot(t_s[u].astype(BF16), xin_s[u]).astype(BF16)
        q = _dot(arb_s[u], wu)
        bwu = _dot(bht_s[u], wu)
        qy_s[u] = (rt_s[u] - q[:, :LANES]).astype(BF16)
        yc_s[u] = yc_s[u] - q[:, LANES:]
        bw_s[u] = bwu[:, :LANES]
        n_s[u] = _dot(kht_s[u], vs_s[u]) - bwu[:, LANES:]

    for k in range(n_sub):
        g_c = gc_s[k]
        for p in pairs:
            u = k * n_pairs + p
            z0 = z_scr[p].astype(BF16)
            y_p = _dot(qy_s[u], z0) + yc_s[u]
            y_s[k, :, lanes_of(p)] = y_p[:c_n] + y_p[c_n:]
            m_z = eye * g_c[:, lanes_of(p)] - bw_s[u]
            z_scr[p] = _dot(m_z.astype(BF16), z0) + n_s[u]
    for k in range(n_sub):
        y_ref[0, 0, pl.ds(row_of(k), c_n), :] = y_s[k]


def _rw_scan(r, v, kk, lw, kd, be, l_ctx):
    bsz, l_all, rw = r.shape
    n_sub = RW_SUB
    c_n = RW_CHUNK
    rows = n_sub * c_n
    n_all = l_all // rows
    n_ctx = l_ctx // rows
    n_lat = n_all - n_ctx
    n_pairs = rw // LANES
    n2 = 2 * c_n
    vm = lambda nr, cols, dt: pltpu.VMEM((n_sub * n_pairs, nr, cols), dt)

    def chunk_of(bd, s):
        fwd = s
        bwd = jnp.where(s < n_ctx, n_ctx - 1 - s, n_all + n_ctx - 1 - s)
        return jnp.where(bd % 2 == 0, fwd, bwd)

    def out_chunk(bd, s):
        c = chunk_of(bd, s)
        edge = jnp.where(bd % 2 == 0, 0, n_lat - 1)
        return jnp.where(s < n_ctx, edge, c - n_ctx)

    shared = pl.BlockSpec((1, rows, rw), lambda bd, s: (bd // 2, chunk_of(bd, s), 0))
    per_dir = pl.BlockSpec((1, 1, rows, rw), lambda bd, s: (bd % 2, bd // 2, chunk_of(bd, s), 0))
    return pl.pallas_call(
        functools.partial(_rw_chunk_kernel, n_pairs=n_pairs, n_sub=n_sub),
        out_shape=jax.ShapeDtypeStruct((2, bsz, n_lat * rows, rw), F32),
        grid=(2 * bsz, n_all),
        in_specs=[shared, shared, shared, per_dir, per_dir, per_dir],
        out_specs=pl.BlockSpec((1, 1, rows, rw), lambda bd, s: (bd % 2, bd // 2, out_chunk(bd, s), 0)),
        scratch_shapes=[pltpu.VMEM((n_pairs, n2, n2), F32),
                        pltpu.VMEM((n_sub, 1, rw), F32),
                        vm(2 * n2, n2, BF16),
                        vm(2 * n2, n2, BF16),
                        vm(n2, n2, BF16),
                        vm(n2, n2, BF16),
                        vm(n2, n2, BF16),
                        vm(n2, n2, F32),
                        vm(n2, n2, F32),
                        vm(2 * n2, n2, BF16),
                        vm(n2, n2, BF16),
                        vm(n2, n2, F32),
                        vm(n2, n2, BF16),
                        vm(n2, 2 * n2, BF16),
                        vm(n2, n2, F32),
                        vm(n2, n2, BF16),
                        vm(n2, n2, F32),
                        vm(n2, n2, F32),
                        pltpu.VMEM((n_sub, c_n, rw), F32)],
        compiler_params=_cparams(("parallel", "arbitrary")),
        name="rwkv_scan",
    )(r, v, kk, lw, kd, be)


def _rw_step_kernel(rf_ref, rb_ref, vf_ref, vb_ref, kkf_ref, kkb_ref, lwf_ref, lwb_ref, kdf_ref, kdb_ref,
                    bef_ref, beb_ref, yf_ref, yb_ref,
                    z_scr, gc_s, lhs_s, kb_s, kht_s, bht_s, vs_s, rt_s, aab_s, akr_s, arb_s, t_s, x_s, xin_s,
                    yc_s, wu_s, qm_s, yn_s, *, n_pairs, bsz):
    c_n = RW_CHUNK
    n2 = 2 * c_n

    lanes_of = lambda p: slice(p * LANES, (p + 1) * LANES)
    unit = lambda d, b, p: (d * bsz + b) * n_pairs + p
    units = [(d, b, p) for d in range(2) for b in range(bsz) for p in range(n_pairs)]

    @pl.when(pl.program_id(0) == 0)
    def _():
        z_scr[...] = jnp.zeros_like(z_scr)
        qm_s[...] = jnp.zeros_like(qm_s)
        yn_s[...] = jnp.zeros_like(yn_s)

    for d, b, p in units:
        u = unit(d, b, p)
        y_ref = yf_ref if d == 0 else yb_ref
        yz = _dot(qm_s[u], z_scr[u].astype(BF16)) + yn_s[u]
        y_ref[b, :, lanes_of(p)] = yz[:c_n] + yz[c_n:n2]
        z_scr[u] = yz[n2:]

    ri = lax.broadcasted_iota(jnp.int32, (c_n, c_n), 0)
    ci = lax.broadcasted_iota(jnp.int32, (c_n, c_n), 1)
    r2 = lax.broadcasted_iota(jnp.int32, (n2, n2), 0)
    c2 = lax.broadcasted_iota(jnp.int32, (n2, n2), 1)
    t2 = r2 % c_n
    i2 = c2 % c_n
    same_head = (r2 // c_n) == (c2 // c_n)
    diag = r2 == c2
    eye = diag.astype(F32)
    head0 = lax.broadcasted_iota(jnp.int32, (1, LANES), 1) < RW_HEAD
    tri = [(ri >= ci).astype(F32), (ri <= ci).astype(F32)]
    before = [jnp.logical_and(same_head, i2 < t2), jnp.logical_and(same_head, i2 > t2)]
    upto = [jnp.logical_or(m, diag) for m in before]

    def blk(s):
        return (r2 // s) == (c2 // s)

    srcs = [(rf_ref, vf_ref, kkf_ref, lwf_ref, kdf_ref, bef_ref), (rb_ref, vb_ref, kkb_ref, lwb_ref, kdb_ref, beb_ref)]

    for d in range(2):
        r_ref, v_ref, kk_ref, lw_ref, kd_ref, be_ref = srcs[d]
        for b in range(bsz):
            lw = lw_ref[0, b]
            cum = _dot(tri[d], lw, HIGHEST)
            tot = jnp.sum(lw, axis=0, keepdims=True)
            g_inv = jnp.exp(-cum)
            g_hat = jnp.exp(tot - cum)
            kd = kd_ref[0, b].astype(F32)
            be = be_ref[0, b].astype(F32)
            at = kk_ref[b].astype(F32) * jnp.exp(cum - lw)
            rt = r_ref[b].astype(F32) * jnp.exp(cum)
            kt = kd * g_inv
            bt = be * g_inv
            kh = kd * g_hat
            bh = be * g_hat
            gc_s[d * bsz + b] = jnp.exp(tot)
            v = v_ref[b].astype(F32)
            for p in range(n_pairs):
                u = unit(d, b, p)
                st = lambda x: _stack_heads(x[:, lanes_of(p)], head0)
                rt_p = st(rt)
                at_p = st(at).astype(BF16)
                lhs_s[u, :n2] = at_p
                lhs_s[u, n2:] = rt_p.astype(BF16)
                xin_s[u, :, :LANES] = at_p
                rt_s[u] = rt_p
                kb_s[u, :n2] = st(kt).astype(BF16)
                kb_s[u, n2:] = st(bt).astype(BF16)
                kht_s[u] = st(kh).T.astype(BF16)
                bht_s[u] = st(bh).T.astype(BF16)
                vs_s[u] = st(v).astype(BF16)

    for d, b, p in units:
        u = unit(d, b, p)
        g = _dot_nt(lhs_s[u], kb_s[u])
        a_ab = jnp.where(before[d], g[:n2, n2:], 0.0)
        akr_s[u, :n2] = jnp.where(before[d], g[:n2, :n2], 0.0).astype(BF16)
        akr_s[u, n2:] = jnp.where(upto[d], g[n2:, :n2], 0.0).astype(BF16)
        arb_s[u] = jnp.where(upto[d], g[n2:, n2:], 0.0).astype(BF16)
        aab_s[u] = a_ab
        t_s[u] = eye - jnp.where(blk(2), a_ab, 0.0)

    n_units = len(units)
    s = 2
    while s < c_n:
        off = jnp.logical_and(blk(2 * s), jnp.logical_not(blk(s)))
        for u in range(n_units):
            x_s[u] = _dot(t_s[u].astype(BF16), jnp.where(off, aab_s[u], 0.0).astype(BF16)).astype(BF16)
        for u in range(n_units):
            t = t_s[u]
            t_s[u] = t - _dot(x_s[u], t.astype(BF16))
        s *= 2

    for u in range(n_units):
        av = _dot(akr_s[u], vs_s[u])
        xin_s[u, :, LANES:] = av[:n2].astype(BF16)
        yc_s[u] = av[n2:]

    for u in range(n_units):
        wu_s[u] = _dot(t_s[u].astype(BF16), xin_s[u]).astype(BF16)
    for d, b, p in units:
        u = unit(d, b, p)
        wu = wu_s[u]
        q = _dot(arb_s[u], wu)
        bwu = _dot(bht_s[u], wu)
        qm_s[u, :n2] = (rt_s[u] - q[:, :LANES]).astype(BF16)
        qm_s[u, n2:] = (eye * gc_s[d * bsz + b][:, lanes_of(p)] - bwu[:, :LANES]).astype(BF16)
        yn_s[u, :n2] = yc_s[u] - q[:, LANES:]
        yn_s[u, n2:] = _dot(kht_s[u], vs_s[u]) - bwu[:, LANES:]


def _rw_scan2(r, v, kk, lw, kd, be, l_ctx):
    bsz, l_all, rw = r.shape
    c_n = RW_CHUNK
    n_all = l_all // c_n
    n_ctx = l_ctx // c_n
    n_lat = n_all - n_ctx
    n_pairs = rw // LANES
    n_units = 2 * bsz * n_pairs
    n2 = 2 * c_n
    vm = lambda nr, cols, dt: pltpu.VMEM((n_units, nr, cols), dt)

    clamp = lambda s: jnp.minimum(s, n_all - 1)
    chunk_f = lambda s: clamp(s)
    chunk_b = lambda s: jnp.where(clamp(s) < n_ctx, n_ctx - 1 - clamp(s), n_all + n_ctx - 1 - clamp(s))
    prev = lambda s: jnp.maximum(s - 1, 0)
    out_f = lambda s: jnp.maximum(prev(s) - n_ctx, 0)
    out_b = lambda s: jnp.where(prev(s) < n_ctx, n_lat - 1, chunk_b(prev(s)) - n_ctx)

    sh_f = pl.BlockSpec((bsz, c_n, rw), lambda s: (0, chunk_f(s), 0))
    sh_b = pl.BlockSpec((bsz, c_n, rw), lambda s: (0, chunk_b(s), 0))
    pd_f = pl.BlockSpec((1, bsz, c_n, rw), lambda s: (0, 0, chunk_f(s), 0))
    pd_b = pl.BlockSpec((1, bsz, c_n, rw), lambda s: (1, 0, chunk_b(s), 0))
    y_shape = jax.ShapeDtypeStruct((bsz, n_lat * c_n, rw), F32)
    return pl.pallas_call(
        functools.partial(_rw_step_kernel, n_pairs=n_pairs, bsz=bsz),
        out_shape=(y_shape, y_shape),
        grid=(n_all + 1,),
        in_specs=[sh_f, sh_b, sh_f, sh_b, sh_f, sh_b, pd_f, pd_b, pd_f, pd_b, pd_f, pd_b],
        out_specs=(pl.BlockSpec((bsz, c_n, rw), lambda s: (0, out_f(s), 0)),
                   pl.BlockSpec((bsz, c_n, rw), lambda s: (0, out_b(s), 0))),
        scratch_shapes=[vm(n2, n2, F32),
                        pltpu.VMEM((2 * bsz, 1, rw), F32),
                        vm(2 * n2, n2, BF16),
                        vm(2 * n2, n2, BF16),
                        vm(n2, n2, BF16),
                        vm(n2, n2, BF16),
                        vm(n2, n2, BF16),
                        vm(n2, n2, F32),
                        vm(n2, n2, F32),
                        vm(2 * n2, n2, BF16),
                        vm(n2, n2, BF16),
                        vm(n2, n2, F32),
                        vm(n2, n2, BF16),
                        vm(n2, 2 * n2, BF16),
                        vm(n2, n2, F32),
                        vm(n2, 2 * n2, BF16),
                        vm(2 * n2, n2, BF16),
                        vm(2 * n2, n2, F32)],
        compiler_params=_cparams(("arbitrary",)),
        name="rwkv_scan",
    )(r, r, v, v, kk, kk, lw, lw, kd, kd, be, be)


def _s5_glu_kernel(y_ref, wa_ref, wb_ref, o_ref, h_scr):
    @pl.when(pl.program_id(1) == 0)
    def _():
        for jb in range(y_ref.shape[0]):
            h_scr[:, jb * LANES:(jb + 1) * LANES] = _gelu_tanh(y_ref[jb]).astype(BF16)

    h = h_scr[...]
    o_ref[...] = (_dot(h, wa_ref[...]) * _sigmoid(_dot(h, wb_ref[...]))).astype(o_ref.dtype)


def _s5_glu(y_blk, w, *, tm, tn):
    nb, m, _ = y_blk.shape
    k = nb * LANES
    n = w.shape[1] // 2
    nj = n // tn
    return pl.pallas_call(
        _s5_glu_kernel,
        out_shape=jax.ShapeDtypeStruct((m, n), BF16),
        grid=(m // tm, nj),
        in_specs=[pl.BlockSpec((nb, tm, LANES), lambda i, j: (0, i, 0)),
                  pl.BlockSpec((k, tn), lambda i, j: (0, j)),
                  pl.BlockSpec((k, tn), lambda i, j: (0, j + nj))],
        out_specs=pl.BlockSpec((tm, tn), lambda i, j: (i, j)),
        scratch_shapes=[pltpu.VMEM((tm, k), BF16)],
        compiler_params=_cparams(("parallel", "arbitrary")),
        name="s5_glu",
    )(y_blk, w, w)


def _rw_merge_kernel(yf_ref, yb_ref, bonus_ref, g_ref, lnw_ref, lnb_ref, seg_ref, segt_ref,
                     ga_ref, gb_ref, s5_ref, w_ref, o_ref, h_scr):
    @pl.when(pl.program_id(1) == 0)
    def _():
        seg = seg_ref[...]
        segt = segt_ref[...]
        inv_n = 1.0 / RW_HEAD

        def head_mean(t):
            return _dot_hilo(_dot_hilo(t, seg), segt) * inv_n

        y = yf_ref[...] + yb_ref[...]
        dy = y - head_mean(y)
        var = head_mean(dy * dy)
        y = dy * lax.rsqrt(var + GN_EPS) * lnw_ref[...] + lnb_ref[...] + bonus_ref[...].astype(F32)
        h_scr[...] = (y * g_ref[...].astype(F32)).astype(BF16)

    rw_out = _dot(h_scr[...], w_ref[...])
    merged = ga_ref[...].astype(F32) * s5_ref[...].astype(F32) + gb_ref[...].astype(F32) * rw_out
    o_ref[...] = merged.astype(o_ref.dtype)


def _rw_merge(y_f, y_b, bonus, g, ln_w, ln_b, seg, segt, gates, s5_out, w_proj, *, tm, tn):
    m, rw = y_f.shape
    n = w_proj.shape[1]
    nj = n // tn
    full = lambda shape: pl.BlockSpec(shape, lambda i, j: (0,) * len(shape))
    return pl.pallas_call(
        _rw_merge_kernel,
        out_shape=jax.ShapeDtypeStruct((m, n), BF16),
        grid=(m // tm, nj),
        in_specs=[pl.BlockSpec((tm, rw), lambda i, j: (i, 0)),
                  pl.BlockSpec((tm, rw), lambda i, j: (i, 0)),
                  pl.BlockSpec((tm, rw), lambda i, j: (i, 0)),
                  pl.BlockSpec((tm, rw), lambda i, j: (i, 0)),
                  full((1, rw)), full((1, rw)), full(seg.shape), full(segt.shape),
                  pl.BlockSpec((tm, tn), lambda i, j: (i, j)),
                  pl.BlockSpec((tm, tn), lambda i, j: (i, j + nj)),
                  pl.BlockSpec((tm, tn), lambda i, j: (i, j)),
                  pl.BlockSpec((rw, tn), lambda i, j: (0, j))],
        out_specs=pl.BlockSpec((tm, tn), lambda i, j: (i, j)),
        scratch_shapes=[pltpu.VMEM((tm, rw), BF16)],
        compiler_params=_cparams(("parallel", "arbitrary")),
        name="rwkv_merge",
    )(y_f, y_b, bonus, g, ln_w.reshape(1, rw), ln_b.reshape(1, rw), seg, segt,
      gates, gates, s5_out, w_proj)


def _resid_ln_kernel(a_ref, w_ref, x_ref, g_ref, nw_ref, sh_ref, sc_ref, h_ref, hn_ref):
    h = x_ref[...] + g_ref[0] * _dot(a_ref[...], w_ref[...])
    h_ref[...] = h
    hn_ref[...] = _lnmod_rows(h, nw_ref[...], sh_ref[0], sc_ref[0]).astype(hn_ref.dtype)


def _resid_matmul_ln(a, w, x2, g_tab, nw, sh_tab, sc_tab, mod_row_of_block, *, tm):
    m, k = a.shape
    n = w.shape[1]
    mod_map = lambda i: (mod_row_of_block(i), 0, 0)
    return pl.pallas_call(
        _resid_ln_kernel,
        out_shape=(jax.ShapeDtypeStruct((m, n), F32), jax.ShapeDtypeStruct((m, n), BF16)),
        grid=(m // tm,),
        in_specs=[pl.BlockSpec((tm, k), lambda i: (i, 0)),
                  pl.BlockSpec((k, n), lambda i: (0, 0)),
                  pl.BlockSpec((tm, n), lambda i: (i, 0)),
                  pl.BlockSpec((1, 1, n), mod_map),
                  pl.BlockSpec((1, n), lambda i: (0, 0)),
                  pl.BlockSpec((1, 1, n), mod_map),
                  pl.BlockSpec((1, 1, n), mod_map)],
        out_specs=(pl.BlockSpec((tm, n), lambda i: (i, 0)), pl.BlockSpec((tm, n), lambda i: (i, 0))),
        compiler_params=_cparams(("parallel",)),
        name="out_proj",
    )(a, w, x2, g_tab, nw.reshape(1, n), sh_tab, sc_tab)


def _wres_swiglu_kernel(a_ref, w1_ref, w3_ref, o_ref, w1_scr, w3_scr):
    @pl.when(pl.program_id(1) == 0)
    def _():
        w1_scr[...] = w1_ref[...].astype(BF16)
        w3_scr[...] = w3_ref[...].astype(BF16)

    a = a_ref[...]
    o_ref[...] = (_silu(_dot(a, w1_scr[...])) * _dot(a, w3_scr[...])).astype(o_ref.dtype)


def _wres_swiglu(a, w13, d_ff, *, tm, tn):
    m, k = a.shape
    nj = d_ff // tn
    return pl.pallas_call(
        _wres_swiglu_kernel,
        out_shape=jax.ShapeDtypeStruct((m, d_ff), BF16),
        grid=(nj, m // tm),
        in_specs=[pl.BlockSpec((tm, k), lambda j, i: (i, 0)),
                  pl.BlockSpec((k, tn), lambda j, i: (0, j)),
                  pl.BlockSpec((k, tn), lambda j, i: (0, j + nj))],
        out_specs=pl.BlockSpec((tm, tn), lambda j, i: (i, j)),
        scratch_shapes=[pltpu.VMEM((k, tn), BF16), pltpu.VMEM((k, tn), BF16)],
        compiler_params=_cparams(("arbitrary", "arbitrary")),
        name="ffn_up",
    )(a, w13, w13)


def _resid_mm_kernel(a_ref, w_ref, x_ref, g_ref, o_ref):
    o_ref[...] = x_ref[...] + g_ref[0] * _dot(a_ref[...], w_ref[...])


def _resid_matmul(a, w, x2, g_tab, mod_row_of_block, *, tm, tn):
    m, k = a.shape
    n = w.shape[1]
    return pl.pallas_call(
        _resid_mm_kernel,
        out_shape=jax.ShapeDtypeStruct((m, n), F32),
        grid=(m // tm, n // tn),
        in_specs=[pl.BlockSpec((tm, k), lambda i, j: (i, 0)),
                  pl.BlockSpec((k, tn), lambda i, j: (0, j)),
                  pl.BlockSpec((tm, tn), lambda i, j: (i, j)),
                  pl.BlockSpec((1, 1, tn), lambda i, j: (mod_row_of_block(i), 0, j))],
        out_specs=pl.BlockSpec((tm, tn), lambda i, j: (i, j)),
        compiler_params=_cparams(("parallel", "arbitrary")),
        name="out_proj",
    )(a, w, x2, g_tab)


def _ffn_down_kernel(a_ref, w_ref, x_ref, g_ref, nf_ref, o_ref, acc_ref):
    kk = pl.program_id(1)

    @pl.when(kk == 0)
    def _():
        acc_ref[...] = jnp.zeros_like(acc_ref)

    acc_ref[...] += _dot(a_ref[...], w_ref[...])

    @pl.when(kk == pl.num_programs(1) - 1)
    def _():
        h = x_ref[...] + g_ref[0] * acc_ref[...]
        ms = jnp.mean(h * h, axis=-1, keepdims=True)
        o_ref[...] = h * lax.rsqrt(ms + NORM_EPS) * nf_ref[...]


def _ffn_down(a, w, x2, g_tab, mod_row_of_block, norm_f, *, tm, tk):
    m, k = a.shape
    n = w.shape[1]
    return pl.pallas_call(
        _ffn_down_kernel,
        out_shape=jax.ShapeDtypeStruct((m, n), F32),
        grid=(m // tm, k // tk),
        in_specs=[pl.BlockSpec((tm, tk), lambda i, kk: (i, kk)),
                  pl.BlockSpec((tk, n), lambda i, kk: (kk, 0)),
                  pl.BlockSpec((tm, n), lambda i, kk: (i, 0)),
                  pl.BlockSpec((1, 1, n), lambda i, kk: (mod_row_of_block(i), 0, 0)),
                  pl.BlockSpec((1, n), lambda i, kk: (0, 0))],
        out_specs=pl.BlockSpec((tm, n), lambda i, kk: (i, 0)),
        scratch_shapes=[pltpu.VMEM((tm, n), F32)],
        compiler_params=_cparams(("parallel", "arbitrary")),
        name="ffn_down",
    )(a, w, x2, g_tab, norm_f.reshape(1, n))


def kernel(x, c, ctx, c_ctx, ada_w, ada_b, norm1_w, w_in, rw_mu, s5_a_re, s5_a_im, s5_log_dt, s5_b_re, s5_b_im, s5_c_re, s5_c_im, s5_d, s5_glu_w, rw_w0, rw_w2, rw_a0, rw_a2, rw_g2, rw_k_k, rw_k_a, rw_r_k, rw_ln_w, rw_ln_b, rw_proj, w_o, norm2_w, ffn_w13, ffn_w2, norm_f):
    assert ada_w.shape[0] == 1, "single-layer block"
    bsz, l_lat, d = x.shape
    l_ctx = ctx.shape[1]
    l_all = l_ctx + l_lat
    s5w = s5_d.shape[1] * s5_d.shape[2]
    rw = rw_g2.shape[2]
    shift_cols = rw_mu.shape[1]
    d_ff = ffn_w2.shape[1]
    n_heads = rw // RW_HEAD

    c_rows = jnp.concatenate([c, c_ctx[None], jnp.zeros((8 - bsz - 1, d), F32)], axis=0)
    mod = _modulation(c_rows, ada_w[0], ada_b[0])
    tab = lambda k: mod[:, k * d:(k + 1) * d].reshape(8, 1, d)
    sh1, sc1, g1, sh2, sc2, g2 = (tab(k) for k in range(N_MOD))
    ctx_row = bsz

    tm_in = 512
    m_lat = bsz * l_lat
    x2 = x.reshape(m_lat, d)
    n_mix = s5w + shift_cols
    m_all = m_lat + bsz * l_ctx
    lat_row = lambda t: (lambda i: i // (l_lat // t))

    def mix_mod_row(i):
        return jnp.where(i >= m_lat // tm_in, ctx_row, i // (l_lat // tm_in))

    h_all = _lnmod(x2, ctx.reshape(bsz * l_ctx, d), norm1_w[0], sh1, sc1, mix_mod_row, tm=tm_in)
    tm_all = m_all // 8
    u_blk = _wres_matmul(h_all, w_in[0], 0, s5w, rows=m_all, tm=tm_all, tn=s5w,
                         out_dtype=F32, epilogue="lane_blocks", name="in_proj_s5")
    z_rw = _wres_matmul(h_all, w_in[0], s5w, shift_cols, rows=m_all, tm=tm_all, tn=shift_cols // 3,
                        out_dtype=BF16, name="in_proj_rw")
    tm = 1024
    gates = _wres_matmul(h_all, w_in[0], n_mix, w_in.shape[2] - n_mix, rows=m_lat, tm=tm, tn=1024,
                         out_dtype=BF16, epilogue="sigmoid", name="in_proj_gates")

    y_blk = _s5_branch_blocked(u_blk, bsz, l_ctx, l_lat, s5_a_re[0], s5_a_im[0], s5_log_dt[0],
                               s5_b_re[0], s5_b_im[0], s5_c_re[0], s5_c_im[0], s5_d[0])
    s5_out = _s5_glu(y_blk, s5_glu_w[0].astype(BF16), tm=tm, tn=1024)

    lora = rw_w2.shape[2]
    zl = jnp.zeros((lora, rw), F32)
    w2bd = jnp.concatenate([jnp.concatenate([rw_w2[0, 0], zl], axis=1),
                            jnp.concatenate([zl, rw_w2[0, 1]], axis=1)], axis=0)
    a2bd = jnp.concatenate([jnp.concatenate([rw_a2[0, 0], zl], axis=1),
                            jnp.concatenate([zl, rw_a2[0, 1]], axis=1)], axis=0)
    head_of = jnp.arange(rw) // RW_HEAD
    seg = (head_of[:, None] == jnp.arange(LANES)[None, :]).astype(BF16)
    segt = seg.T
    r, v, kk, g, bonus, lw, kd, be = _rw_prep(
        z_rw, bsz, l_ctx, l_lat, rw_mu[0], w2bd, a2bd, rw_g2[0], rw_w0[0].reshape(1, 2 * rw),
        rw_a0[0].reshape(1, 2 * rw), rw_k_k[0], rw_k_a[0], rw_r_k[0].reshape(rw), seg, segt)
    y_f, y_b = _rw_scan2(r, v, kk, lw, kd, be, l_ctx)

    merged = _rw_merge(y_f.reshape(m_lat, rw), y_b.reshape(m_lat, rw), bonus.reshape(m_lat, rw), g.reshape(m_lat, rw),
                       rw_ln_w[0], rw_ln_b[0], seg, segt, gates, s5_out, rw_proj[0].astype(BF16),
                       tm=512, tn=1024)
    tm_o = 512
    h1, h1n = _resid_matmul_ln(merged, w_o[0].astype(BF16), x2, g1, norm2_w[0], sh2, sc2, lat_row(tm_o), tm=tm_o)

    act = _wres_swiglu(h1n, ffn_w13[0], d_ff, tm=tm, tn=512)
    tm_dn = 512
    out = _ffn_down(act, ffn_w2[0].astype(BF16), h1, g2, lat_row(tm_dn), norm_f, tm=tm_dn, tk=d_ff // 4)
    return out.reshape(bsz, l_lat, d)
```

```python
import functools
import math

import jax
import jax.numpy as jnp
from jax import lax
from jax.experimental import pallas as pl
from jax.experimental.pallas import tpu as pltpu

F32 = jnp.float32
BF16 = jnp.bfloat16
HIGHEST = lax.Precision.HIGHEST

D_MODEL = 2048
N_MOD = 6
NORM_EPS = 1e-6
GN_EPS = 64e-5
GRID_W = 64
S5_GROUP = 16
S5_STATE = 64
S5_CHUNK = 16
RW_HEAD = 64
RW_CHUNK = 64
RW_SUB = 2
LANES = 128
VMEM_LIMIT = 48 * 1024 * 1024


def _cparams(sem, vmem=VMEM_LIMIT):
    return pltpu.CompilerParams(dimension_semantics=sem, vmem_limit_bytes=vmem)


def _operands(a, b, precision):
    if precision == "bf16":
        return a.astype(BF16), b.astype(BF16), None
    return a, b, precision


def _dot(a, b, precision=None):
    a, b, precision = _operands(a, b, precision)
    return jnp.dot(a, b, preferred_element_type=F32, precision=precision)


def _dot_nt(a, b, precision=None):
    a, b, precision = _operands(a, b, precision)
    return lax.dot_general(a, b, (((1,), (1,)), ((), ())), preferred_element_type=F32, precision=precision)


def _dot_hilo(a, ind):
    hi = a.astype(BF16)
    lo = (a - hi.astype(F32)).astype(BF16)
    return _dot(hi, ind) + _dot(lo, ind)


def _sigmoid(x):
    return 1.0 / (1.0 + jnp.exp(-x))


def _silu(x):
    return x * _sigmoid(x)


def _gelu_tanh(x):
    c = math.sqrt(2.0 / math.pi)
    return 0.5 * x * (1.0 + jnp.tanh(c * (x + 0.044715 * (x * x * x))))


def _softplus(x):
    return jnp.maximum(x, 0.0) + jnp.log(1.0 + jnp.exp(-jnp.abs(x)))


def _mod_kernel(c_ref, w_ref, b_ref, o_ref):
    o_ref[...] = _dot_hilo(_silu(c_ref[...]), w_ref[...].astype(BF16)) + b_ref[...]


def _modulation(c_rows, ada_w, ada_b):
    m, d = c_rows.shape
    n = ada_w.shape[1]
    tn = 1024
    return pl.pallas_call(
        _mod_kernel,
        out_shape=jax.ShapeDtypeStruct((m, n), F32),
        grid=(n // tn,),
        in_specs=[pl.BlockSpec((m, d), lambda j: (0, 0)),
                  pl.BlockSpec((d, tn), lambda j: (0, j)),
                  pl.BlockSpec((1, tn), lambda j: (0, j))],
        out_specs=pl.BlockSpec((m, tn), lambda j: (0, j)),
        compiler_params=_cparams(("arbitrary",)),
        name="modulation",
    )(c_rows, ada_w, ada_b.reshape(1, n))


def _lnmod_rows(x, nw, sh, sc):
    ms = jnp.mean(x * x, axis=-1, keepdims=True)
    y = x * lax.rsqrt(ms + NORM_EPS) * nw
    return y * (1.0 + sc) + sh


def _lnmod_mm_kernel(x_ref, nw_ref, sh_ref, sc_ref, w_ref, o_ref, h_scr, *, epilogue):
    @pl.when(pl.program_id(1) == 0)
    def _():
        h_scr[...] = _lnmod_rows(x_ref[...], nw_ref[...], sh_ref[0], sc_ref[0]).astype(BF16)

    z = _dot(h_scr[...], w_ref[...])
    if epilogue == "sigmoid":
        z = _sigmoid(z)
    if epilogue == "lane_blocks":
        for jb in range(o_ref.shape[0]):
            o_ref[jb] = z[:, jb * LANES:(jb + 1) * LANES].astype(o_ref.dtype)
    else:
        o_ref[...] = z.astype(o_ref.dtype)


def _lnmod_swiglu_kernel(x_ref, nw_ref, sh_ref, sc_ref, w1_ref, w3_ref, o_ref, h_scr):
    @pl.when(pl.program_id(1) == 0)
    def _():
        h_scr[...] = _lnmod_rows(x_ref[...], nw_ref[...], sh_ref[0], sc_ref[0]).astype(BF16)

    h = h_scr[...]
    o_ref[...] = (_silu(_dot(h, w1_ref[...])) * _dot(h, w3_ref[...])).astype(o_ref.dtype)


def _lnmod_matmul(x2, nw, sh_tab, sc_tab, mod_row_of_block, w, *, tm, tn, out_dtype, epilogue=None, name):
    m, d = x2.shape
    n = w.shape[1]
    mod_map = lambda i, j: (mod_row_of_block(i), 0, 0)
    if epilogue == "lane_blocks":
        assert tn == n
        out_shape = jax.ShapeDtypeStruct((n // LANES, m, LANES), out_dtype)
        out_spec = pl.BlockSpec((n // LANES, tm, LANES), lambda i, j: (0, i, 0))
    else:
        out_shape = jax.ShapeDtypeStruct((m, n), out_dtype)
        out_spec = pl.BlockSpec((tm, tn), lambda i, j: (i, j))
    return pl.pallas_call(
        functools.partial(_lnmod_mm_kernel, epilogue=epilogue),
        out_shape=out_shape,
        grid=(m // tm, n // tn),
        in_specs=[pl.BlockSpec((tm, d), lambda i, j: (i, 0)),
                  pl.BlockSpec((1, d), lambda i, j: (0, 0)),
                  pl.BlockSpec((1, 1, d), mod_map),
                  pl.BlockSpec((1, 1, d), mod_map),
                  pl.BlockSpec((d, tn), lambda i, j: (0, j))],
        out_specs=out_spec,
        scratch_shapes=[pltpu.VMEM((tm, d), BF16)],
        compiler_params=_cparams(("parallel", "arbitrary")),
        name=name,
    )(x2, nw.reshape(1, d), sh_tab, sc_tab, w)


def _lnmod_kernel(x_ref, c_ref, nw_ref, sh_ref, sc_ref, o_ref, *, n_lat_blocks):
    rows = jnp.where(pl.program_id(0) < n_lat_blocks, x_ref[...], c_ref[...])
    o_ref[...] = _lnmod_rows(rows, nw_ref[...], sh_ref[0], sc_ref[0]).astype(o_ref.dtype)


def _lnmod(x2, c2, nw, sh_tab, sc_tab, mod_row_of_block, *, tm):
    m, d = x2.shape
    mc = c2.shape[0]
    nx, nc = m // tm, mc // tm
    mod_map = lambda i: (mod_row_of_block(i), 0, 0)
    return pl.pallas_call(
        functools.partial(_lnmod_kernel, n_lat_blocks=nx),
        out_shape=jax.ShapeDtypeStruct((m + mc, d), BF16),
        grid=(nx + nc,),
        in_specs=[pl.BlockSpec((tm, d), lambda i: (jnp.minimum(i, nx - 1), 0)),
                  pl.BlockSpec((tm, d), lambda i: (jnp.maximum(i - nx, 0), 0)),
                  pl.BlockSpec((1, d), lambda i: (0, 0)),
                  pl.BlockSpec((1, 1, d), mod_map),
                  pl.BlockSpec((1, 1, d), mod_map)],
        out_specs=pl.BlockSpec((tm, d), lambda i: (i, 0)),
        compiler_params=_cparams(("parallel",)),
        name="lnmod",
    )(x2, c2, nw.reshape(1, d), sh_tab, sc_tab)


def _wres_mm_kernel(a_ref, w_ref, o_ref, w_scr, *, epilogue):
    @pl.when(pl.program_id(1) == 0)
    def _():
        w_scr[...] = w_ref[...].astype(BF16)

    z = _dot(a_ref[...], w_scr[...])
    if epilogue == "sigmoid":
        z = _sigmoid(z)
    if epilogue == "lane_blocks":
        for jb in range(o_ref.shape[0]):
            o_ref[jb] = z[:, jb * LANES:(jb + 1) * LANES].astype(o_ref.dtype)
    else:
        o_ref[...] = z.astype(o_ref.dtype)


def _wres_matmul(a, w, col0, n, *, rows, tm, tn, out_dtype, epilogue=None, name):
    k = a.shape[1]
    assert col0 % LANES == 0 and n % tn == 0 and rows % tm == 0
    if epilogue == "lane_blocks":
        assert tn == n
        out_shape = jax.ShapeDtypeStruct((n // LANES, rows, LANES), out_dtype)
        out_spec = pl.BlockSpec((n // LANES, tm, LANES), lambda j, i: (0, i, 0))
    else:
        out_shape = jax.ShapeDtypeStruct((rows, n), out_dtype)
        out_spec = pl.BlockSpec((tm, tn), lambda j, i: (i, j))
    return pl.pallas_call(
        functools.partial(_wres_mm_kernel, epilogue=epilogue),
        out_shape=out_shape,
        grid=(n // tn, rows // tm),
        in_specs=[pl.BlockSpec((tm, k), lambda j, i: (i, 0)),
                  pl.BlockSpec((pl.Element(k), pl.Element(tn)), lambda j, i: (0, pl.multiple_of(col0 + j * tn, LANES)))],
        out_specs=out_spec,
        scratch_shapes=[pltpu.VMEM((k, tn), BF16)],
        compiler_params=_cparams(("arbitrary", "arbitrary")),
        name=name,
    )(a, w)


def _lnmod_swiglu(x2, nw, sh_tab, sc_tab, mod_row_of_block, w13, d_ff, *, tm, tn, name):
    m, d = x2.shape
    nj = d_ff // tn
    mod_map = lambda i, j: (mod_row_of_block(i), 0, 0)
    return pl.pallas_call(
        _lnmod_swiglu_kernel,
        out_shape=jax.ShapeDtypeStruct((m, d_ff), BF16),
        grid=(m // tm, nj),
        in_specs=[pl.BlockSpec((tm, d), lambda i, j: (i, 0)),
                  pl.BlockSpec((1, d), lambda i, j: (0, 0)),
                  pl.BlockSpec((1, 1, d), mod_map),
                  pl.BlockSpec((1, 1, d), mod_map),
                  pl.BlockSpec((d, tn), lambda i, j: (0, j)),
                  pl.BlockSpec((d, tn), lambda i, j: (0, j + nj))],
        out_specs=pl.BlockSpec((tm, tn), lambda i, j: (i, j)),
        scratch_shapes=[pltpu.VMEM((tm, d), BF16)],
        compiler_params=_cparams(("parallel", "arbitrary")),
        name=name,
    )(x2, nw.reshape(1, d), sh_tab, sc_tab, w13, w13)


def _s5_param_kernel(are_ref, aim_ref, ldt_ref, bre_ref, bim_ref, cre_ref, cim_ref,
                     e_ref, cs_ref, kt_ref, a16_ref):
    t_n, hg, p_n = S5_CHUNK, S5_GROUP, S5_STATE
    tau = lax.broadcasted_iota(jnp.int32, (t_n, 1, p_n), 0).astype(F32)
    for d in range(2):
        a_re = are_ref[0, d:d + 1, :]
        a_im = aim_ref[0, d:d + 1, :]
        dt = jnp.exp(ldt_ref[0, d:d + 1, :])
        lam = a_re * dt
        th = a_im * dt
        er = jnp.exp(lam)
        ab_re = er * jnp.cos(th)
        ab_im = er * jnp.sin(th)
        den = a_re * a_re + a_im * a_im
        x_re = ab_re - 1.0
        co_re = (x_re * a_re + ab_im * a_im) / den
        co_im = (ab_im * a_re - x_re * a_im) / den
        bt_re = bre_ref[0, d]
        bt_im = bim_ref[0, d]
        bb_re = co_re * bt_re - co_im * bt_im
        bb_im = co_re * bt_im + co_im * bt_re
        c_re = cre_ref[0, d]
        c_im = cim_ref[0, d]

        def power(tv):
            mag = jnp.exp(tv * lam)
            return mag * jnp.cos(tv * th), mag * jnp.sin(tv * th)

        pw_re, pw_im = power(tau)
        cp_re = (c_re[None] * pw_re - c_im[None] * pw_im).reshape(t_n * hg, p_n)
        cp_im = (c_re[None] * pw_im + c_im[None] * pw_re).reshape(t_n * hg, p_n)
        kt_ref[0, d] = _dot_nt(bb_re, cp_re, HIGHEST) - _dot_nt(bb_im, cp_im, HIGHEST)

        te = (t_n - 1.0 - tau) if d == 0 else tau
        pe_re, pe_im = power(te)
        e_ref[0, d, :, 0:p_n] = (pe_re * bb_re[None] - pe_im * bb_im[None]).reshape(t_n * hg, p_n)
        e_ref[0, d, :, p_n:2 * p_n] = (pe_re * bb_im[None] + pe_im * bb_re[None]).reshape(t_n * hg, p_n)

        tc = (tau + 1.0) if d == 0 else (t_n - tau)
        pc_re, pc_im = power(tc)
        cs_ref[0, d, :, 0:p_n] = (c_re[None] * pc_re - c_im[None] * pc_im).reshape(t_n * hg, p_n)
        cs_ref[0, d, :, p_n:2 * p_n] = -(c_re[None] * pc_im + c_im[None] * pc_re).reshape(t_n * hg, p_n)

        mag16 = jnp.exp(float(t_n) * lam)
        a16_ref[0, d, 0:1, :] = mag16 * jnp.cos(float(t_n) * th)
        a16_ref[0, d, 1:2, :] = mag16 * jnp.sin(float(t_n) * th)


def _s5_params(a_re, a_im, log_dt, b_re, b_im, c_re, c_im):
    g_n = a_re.shape[1]
    p_n, hg, t_n = S5_STATE, S5_GROUP, S5_CHUNK
    tr = lambda a: jnp.swapaxes(a, 0, 1)
    ldt = jnp.broadcast_to(tr(log_dt)[:, :, None], (g_n, 2, p_n))
    spec3 = pl.BlockSpec((1, 2, p_n), lambda g: (g, 0, 0))
    spec4 = pl.BlockSpec((1, 2, hg, p_n), lambda g: (g, 0, 0, 0))
    th = t_n * hg
    return pl.pallas_call(
        _s5_param_kernel,
        out_shape=(jax.ShapeDtypeStruct((g_n, 2, th, 2 * p_n), F32),
                   jax.ShapeDtypeStruct((g_n, 2, th, 2 * p_n), F32),
                   jax.ShapeDtypeStruct((g_n, 2, hg, th), F32),
                   jax.ShapeDtypeStruct((g_n, 2, 2, p_n), F32)),
        grid=(g_n,),
        in_specs=[spec3, spec3, spec3, spec4, spec4, spec4, spec4],
        out_specs=(pl.BlockSpec((1, 2, th, 2 * p_n), lambda g: (g, 0, 0, 0)),
                   pl.BlockSpec((1, 2, th, 2 * p_n), lambda g: (g, 0, 0, 0)),
                   pl.BlockSpec((1, 2, hg, th), lambda g: (g, 0, 0, 0)),
                   pl.BlockSpec((1, 2, 2, p_n), lambda g: (g, 0, 0, 0))),
        compiler_params=_cparams(("parallel",)),
        name="s5_params",
    )(tr(a_re), tr(a_im), ldt,
      jnp.transpose(b_re, (1, 0, 3, 2)), jnp.transpose(b_im, (1, 0, 3, 2)), tr(c_re), tr(c_im))


def _s5_state_in_kernel(u_ref, e_ref, o_ref):
    e = e_ref[0]
    u = u_ref[0]
    o_ref[0, :, 0:LANES] = _dot(u, e[0], HIGHEST)
    o_ref[0, :, LANES:2 * LANES] = _dot(u, e[1], HIGHEST)


def _s5_state_inputs(u_g, e):
    g_n, rows, th = u_g.shape
    return pl.pallas_call(
        _s5_state_in_kernel,
        out_shape=jax.ShapeDtypeStruct((g_n, rows, 2 * LANES), F32),
        grid=(g_n,),
        in_specs=[pl.BlockSpec((1, rows, th), lambda g: (g, 0, 0)),
                  pl.BlockSpec((1, 2, th, LANES), lambda g: (g, 0, 0, 0))],
        out_specs=pl.BlockSpec((1, rows, 2 * LANES), lambda g: (g, 0, 0)),
        compiler_params=_cparams(("parallel",)),
        name="s5_state_inputs",
    )(u_g, e)


def _s5_scan_kernel(e_ref, a_ref, o_ref, *, n_ctx, n_all):
    zero = jnp.zeros(e_ref.shape[2:], F32)

    def step(pr, pi):
        ar, ai = a_ref[pr], a_ref[pi]

        def body(c, carry):
            sr, si = carry
            o_ref[pr, c] = sr
            o_ref[pi, c] = si
            return (ar * sr - ai * si + e_ref[pr, c], ar * si + ai * sr + e_ref[pi, c])
        return body

    lax.fori_loop(0, n_all, step(0, 1), (zero, zero))
    bwd = step(2, 3)
    carry = lax.fori_loop(0, n_ctx, lambda k, cy: bwd(n_ctx - 1 - k, cy), (zero, zero))
    lax.fori_loop(0, n_all - n_ctx, lambda k, cy: bwd(n_all - 1 - k, cy), carry)


def _s5_scan(e_planes, a_planes, n_ctx, n_all):
    _, rows, r_n, _ = e_planes.shape
    bsz = rows // n_all
    sub = 8
    return pl.pallas_call(
        functools.partial(_s5_scan_kernel, n_ctx=n_ctx, n_all=n_all),
        out_shape=jax.ShapeDtypeStruct(e_planes.shape, F32),
        grid=(bsz, r_n // sub),
        in_specs=[pl.BlockSpec((4, n_all, sub, LANES), lambda b, q: (0, b, q, 0)),
                  pl.BlockSpec((4, sub, LANES), lambda b, q: (0, q, 0))],
        out_specs=pl.BlockSpec((4, n_all, sub, LANES), lambda b, q: (0, b, q, 0)),
        compiler_params=_cparams(("parallel", "parallel")),
        name="s5_scan",
    )(e_planes, a_planes)


def _s5_apply_kernel(u_ref, s_ref, mf_ref, mb_ref, d_ref, cs_ref, o_ref, *, n_ctx):
    u = u_ref[0, n_ctx:, :]
    s = s_ref[0, n_ctx:, :]
    y = _dot(u, mf_ref[0] + mb_ref[0], HIGHEST) + u * d_ref[0]
    y = y + _dot_nt(s[:, 0:LANES], cs_ref[0, 0], HIGHEST) + _dot_nt(s[:, LANES:2 * LANES], cs_ref[0, 1], HIGHEST)
    o_ref[0, 0] = y


def _s5_apply(u_g, s_g, m_f, m_b, d_t, cs, n_ctx, n_all):
    g_n, rows, th = u_g.shape
    bsz = rows // n_all
    n_lat = n_all - n_ctx
    return pl.pallas_call(
        functools.partial(_s5_apply_kernel, n_ctx=n_ctx),
        out_shape=jax.ShapeDtypeStruct((g_n, bsz, n_lat, th), F32),
        grid=(g_n, bsz),
        in_specs=[pl.BlockSpec((1, n_all, th), lambda g, b: (g, b, 0)),
                  pl.BlockSpec((1, n_all, 2 * LANES), lambda g, b: (g, b, 0)),
                  pl.BlockSpec((1, th, th), lambda g, b: (g, 0, 0)),
                  pl.BlockSpec((1, th, th), lambda g, b: (g, 0, 0)),
                  pl.BlockSpec((1, 1, th), lambda g, b: (g, 0, 0)),
                  pl.BlockSpec((1, 2, th, LANES), lambda g, b: (g, 0, 0, 0))],
        out_specs=pl.BlockSpec((1, 1, n_lat, th), lambda g, b: (g, b, 0, 0)),
        compiler_params=_cparams(("parallel", "arbitrary")),
        name="s5_apply",
    )(u_g, s_g, m_f, m_b, d_t, cs)


def _s5_branch(u_all, n_ctx_tok, a_re, a_im, log_dt, b_re, b_im, c_re, c_im, s5_d):
    bsz, l_all, width = u_all.shape
    hg, t_n, p_n = S5_GROUP, S5_CHUNK, S5_STATE
    g_n = width // hg
    n_all = l_all // t_n
    n_ctx = n_ctx_tok // t_n
    th = t_n * hg
    e, cs, kt, a16 = _s5_params(a_re, a_im, log_dt, b_re, b_im, c_re, c_im)

    kt5 = kt.reshape(g_n, 2, hg, t_n, hg)
    ii = jnp.arange(t_n)[:, None]
    jj = jnp.arange(t_n)[None, :]

    def toeplitz(k4, lag, keep):
        m = k4[:, :, jnp.clip(lag, 0, t_n - 1), :]
        m = jnp.where(keep[None, None, :, :, None], m, 0.0)
        return jnp.transpose(m, (0, 2, 1, 3, 4)).reshape(g_n, th, th)

    m_f = toeplitz(kt5[:, 0], jj - ii, jj >= ii)
    m_b = toeplitz(kt5[:, 1], ii - jj, ii >= jj)
    d_t = jnp.tile(s5_d, (1, t_n)).reshape(g_n, 1, th)

    u_g = jnp.transpose(u_all.reshape(bsz, n_all, t_n, g_n, hg), (3, 0, 1, 2, 4)).reshape(g_n, bsz * n_all, th)
    e_cat = e
    s_in = _s5_state_inputs(u_g, e_cat)
    rows = bsz * n_all
    planes = jnp.transpose(s_in.reshape(g_n, rows, 4, p_n), (2, 1, 0, 3)).reshape(4, rows, g_n * p_n // LANES, LANES)
    a_pl = jnp.transpose(a16.reshape(g_n, 4, p_n), (1, 0, 2)).reshape(4, g_n * p_n // LANES, LANES)
    st = _s5_scan(planes, a_pl, n_ctx, n_all)
    s_g = jnp.transpose(st.reshape(4, rows, g_n, p_n), (2, 1, 0, 3)).reshape(g_n, rows, 4 * p_n)
    y_g = _s5_apply(u_g, s_g, m_f, m_b, d_t, cs, n_ctx, n_all)
    n_lat = n_all - n_ctx
    y = jnp.transpose(y_g.reshape(g_n, bsz, n_lat, t_n, hg), (1, 2, 3, 0, 4))
    return y.reshape(bsz, n_lat * t_n, width)


def _s5_param2_kernel(are_ref, aim_ref, ldt_ref, bre_ref, bim_ref, cre_ref, cim_ref,
                      e_ref, c_ref, m_ref, a16_ref):
    t_n, hg, p_n = S5_CHUNK, S5_GROUP, S5_STATE
    gl_n = are_ref.shape[1]
    tau = lax.broadcasted_iota(jnp.int32, (t_n, gl_n, p_n), 0).astype(F32)
    taps = {}
    for d in range(2):
        a_re = are_ref[d]
        a_im = aim_ref[d]
        dt = jnp.exp(ldt_ref[d])
        lam = a_re * dt
        th = a_im * dt
        er = jnp.exp(lam)
        ab_re = er * jnp.cos(th)
        ab_im = er * jnp.sin(th)
        den = a_re * a_re + a_im * a_im
        x_re = ab_re - 1.0
        co_re = (x_re * a_re + ab_im * a_im) / den
        co_im = (ab_im * a_re - x_re * a_im) / den

        def power(tv):
            mag = jnp.exp(tv * lam)
            return mag * jnp.cos(tv * th), mag * jnp.sin(tv * th)

        pw_re, pw_im = power(tau if d == 0 else (t_n - 1.0 - tau))
        pe_re, pe_im = power((t_n - 1.0 - tau) if d == 0 else tau)
        pc_re, pc_im = power((tau + 1.0) if d == 0 else (t_n - tau))
        mag16 = jnp.exp(float(t_n) * lam)
        a16_ref[d, 0] = mag16 * jnp.cos(float(t_n) * th)
        a16_ref[d, 1] = mag16 * jnp.sin(float(t_n) * th)

        lo, hi = 2 * d * p_n, (2 * d + 1) * p_n
        for gl in range(gl_n):
            of = lambda x: x[:, gl:gl + 1, :]
            bt_re = bre_ref[d, gl].T
            bt_im = bim_ref[d, gl].T
            bb_re = co_re[gl:gl + 1] * bt_re - co_im[gl:gl + 1] * bt_im
            bb_im = co_re[gl:gl + 1] * bt_im + co_im[gl:gl + 1] * bt_re
            c_re = cre_ref[d, gl]
            c_im = cim_ref[d, gl]
            cp_re = (c_re[None] * of(pw_re) - c_im[None] * of(pw_im)).reshape(t_n * hg, p_n)
            cp_im = (c_re[None] * of(pw_im) + c_im[None] * of(pw_re)).reshape(t_n * hg, p_n)
            taps[d, gl] = _dot_nt(bb_re, cp_re, HIGHEST) - _dot_nt(bb_im, cp_im, HIGHEST)
            e_ref[0, :, gl, :, lo:hi] = of(pe_re) * bb_re[None] - of(pe_im) * bb_im[None]
            e_ref[0, :, gl, :, hi:hi + p_n] = of(pe_re) * bb_im[None] + of(pe_im) * bb_re[None]
            c_ref[0, :, gl, :, lo:hi] = c_re[None] * of(pc_re) - c_im[None] * of(pc_im)
            c_ref[0, :, gl, :, hi:hi + p_n] = -(c_re[None] * of(pc_im) + c_im[None] * of(pc_re))

    width = t_n * hg
    lane = lax.broadcasted_iota(jnp.int32, (hg, width), 1)
    for gl in range(gl_n):
        tf, tb = taps[0, gl], taps[1, gl]
        for t in range(t_n):
            sf = t * hg
            sb = (t_n - 1 - t) * hg
            f = tf if sf == 0 else jnp.where(lane >= sf, pltpu.roll(tf, sf, 1), 0.0)
            b = tb if sb == 0 else jnp.where(lane < width - sb, pltpu.roll(tb, width - sb, 1), 0.0)
            m_ref[0, t, gl] = f + b


def _s5_params2(a_re, a_im, log_dt, b_re, b_im, c_re, c_im, nb):
    g_n = a_re.shape[1]
    gl_n = g_n // nb
    p_n, hg, t_n = S5_STATE, S5_GROUP, S5_CHUNK
    ldt = jnp.broadcast_to(log_dt[:, :, None], (2, g_n, p_n))
    spec3 = pl.BlockSpec((2, gl_n, p_n), lambda j: (0, j, 0))
    spec4 = pl.BlockSpec((2, gl_n, hg, p_n), lambda j: (0, j, 0, 0))
    spec_b = pl.BlockSpec((2, gl_n, p_n, hg), lambda j: (0, j, 0, 0))
    wide = 4 * p_n
    comp = jax.ShapeDtypeStruct((nb, t_n, gl_n, hg, wide), F32)
    comp_spec = pl.BlockSpec((1, t_n, gl_n, hg, wide), lambda j: (j, 0, 0, 0, 0))
    return pl.pallas_call(
        _s5_param2_kernel,
        out_shape=(comp, comp, comp, jax.ShapeDtypeStruct((2, 2, g_n, p_n), F32)),
        grid=(nb,),
        in_specs=[spec3, spec3, spec3, spec_b, spec_b, spec4, spec4],
        out_specs=(comp_spec, comp_spec, comp_spec, pl.BlockSpec((2, 2, gl_n, p_n), lambda j: (0, 0, j, 0))),
        compiler_params=_cparams(("parallel",)),
        name="s5_params",
    )(a_re, a_im, ldt, b_re, b_im, c_re, c_im)


def _expand_block_diag(comp, rep_ref, mask_ref, w_scr):
    k = w_scr.shape[0]
    period = mask_ref.shape[1]
    cb = comp.astype(BF16)
    step = 512
    for c0 in range(0, k, step):
        blk = _dot(cb, rep_ref[:, c0:c0 + step]).astype(BF16)
        for q0 in range(0, step, period):
            w_scr[:, c0 + q0:c0 + q0 + period] = blk[:, q0:q0 + period] * mask_ref[...]


def _gather_chunk_rows(u_ref, rows, dst):
    t_n = S5_CHUNK
    for t in range(t_n):
        dst[:, t * LANES:(t + 1) * LANES] = u_ref[0, pl.ds(t, rows, stride=t_n), :].astype(dst.dtype)


def _s5_ein_kernel(u_ref, ec_ref, rep_ref, mask_ref, o_ref, w_scr, u_scr):
    _expand_block_diag(ec_ref[0], rep_ref, mask_ref, w_scr)
    _gather_chunk_rows(u_ref, u_scr.shape[0], u_scr)
    o_ref[0] = _dot(u_scr[...], w_scr[...])


def _s5_chunk_inputs(u_blk, e_comp, rep_e, mask_e):
    nb, m_all, _ = u_blk.shape
    k, cw = e_comp.shape[1:]
    rows = m_all // S5_CHUNK
    return pl.pallas_call(
        _s5_ein_kernel,
        out_shape=jax.ShapeDtypeStruct((nb, rows, k), F32),
        grid=(nb,),
        in_specs=[pl.BlockSpec((1, m_all, LANES), lambda j: (j, 0, 0)),
                  pl.BlockSpec((1, k, cw), lambda j: (j, 0, 0)),
                  pl.BlockSpec(rep_e.shape, lambda j: (0, 0)),
                  pl.BlockSpec(mask_e.shape, lambda j: (0, 0))],
        out_specs=pl.BlockSpec((1, rows, k), lambda j: (j, 0, 0)),
        scratch_shapes=[pltpu.VMEM((k, k), BF16), pltpu.VMEM((rows, k), BF16)],
        compiler_params=_cparams(("parallel",)),
        name="s5_chunk_inputs",
    )(u_blk, e_comp, rep_e, mask_e)


def _s5_bscan_kernel(e_ref, a_ref, o_ref, *, bsz, n_ctx, n_lat):
    q = e_ref.shape[2] // 4
    planes = lambda row, d: (row[:, (2 * d) * q:(2 * d + 1) * q], row[:, (2 * d + 1) * q:(2 * d + 2) * q])
    coef = [planes(a_ref[0], d) for d in range(2)]
    ctx0 = bsz * n_lat

    def advance(state, rows):
        new = []
        for (sr, si), (b, d), row in zip(state, [(b, d) for b in range(bsz) for d in range(2)], rows):
            ar, ai = coef[d]
            er, ei = planes(e_ref[0, pl.ds(row, 1), :], d)
            new.append((ar * sr - ai * si + er, ar * si + ai * sr + ei))
        return tuple(new)

    def ctx_step(s, state):
        rows = [ctx0 + b * n_ctx + (s if d == 0 else n_ctx - 1 - s) for b in range(bsz) for d in range(2)]
        return advance(state, rows)

    def lat_step(s, state):
        rows = [b * n_lat + (s if d == 0 else n_lat - 1 - s) for b in range(bsz) for d in range(2)]
        for (sr, si), (b, d), row in zip(state, [(b, d) for b in range(bsz) for d in range(2)], rows):
            o_ref[0, pl.ds(row, 1), (2 * d) * q:(2 * d + 1) * q] = sr
            o_ref[0, pl.ds(row, 1), (2 * d + 1) * q:(2 * d + 2) * q] = si
        return advance(state, rows)

    zero = jnp.zeros((1, q), F32)
    state = tuple((zero, zero) for _ in range(2 * bsz))
    state = lax.fori_loop(0, n_ctx, ctx_step, state)
    lax.fori_loop(0, n_lat, lat_step, state)


def _s5_bscan(e_rows, a_rows, bsz, n_ctx, n_lat):
    nb, rows, k = e_rows.shape
    return pl.pallas_call(
        functools.partial(_s5_bscan_kernel, bsz=bsz, n_ctx=n_ctx, n_lat=n_lat),
        out_shape=jax.ShapeDtypeStruct((nb, bsz * n_lat, k), F32),
        grid=(nb,),
        in_specs=[pl.BlockSpec((1, rows, k), lambda j: (j, 0, 0)),
                  pl.BlockSpec((1, 1, k), lambda j: (j, 0, 0))],
        out_specs=pl.BlockSpec((1, bsz * n_lat, k), lambda j: (j, 0, 0)),
        compiler_params=_cparams(("parallel",)),
        name="s5_scan",
    )(e_rows, a_rows)


def _s5_out_kernel(u_ref, s_ref, mc_ref, cc_ref, d_ref, rep_m_ref, mask_m_ref, rep_e_ref, mask_e_ref,
                   o_ref, wm_scr, wc_scr, u_scr):
    @pl.when(pl.program_id(1) == 0)
    def _():
        _expand_block_diag(mc_ref[0], rep_m_ref, mask_m_ref, wm_scr)
        _expand_block_diag(cc_ref[0], rep_e_ref, mask_e_ref, wc_scr)

    t_n = S5_CHUNK
    rows = s_ref.shape[1]
    _gather_chunk_rows(u_ref, rows, u_scr)
    u = u_scr[...]
    y = _dot(u.astype(BF16), wm_scr[...]) + _dot_nt(s_ref[0].astype(BF16), wc_scr[...])
    y = y + u * d_ref[0]
    for t in range(t_n):
        o_ref[0, pl.ds(t, rows, stride=t_n), :] = y[:, t * LANES:(t + 1) * LANES]


def _s5_outputs(u_blk, s_rows, m_comp, c_comp, d_rows, rep_m, mask_m, rep_e, mask_e):
    nb = u_blk.shape[0]
    rows, k = s_rows.shape[1:]
    cw = m_comp.shape[2]
    tr = rows // 2
    tok = tr * S5_CHUNK
    const = lambda a: pl.BlockSpec(a.shape, lambda j, i: (0, 0))
    return pl.pallas_call(
        _s5_out_kernel,
        out_shape=jax.ShapeDtypeStruct((nb, rows * S5_CHUNK, LANES), F32),
        grid=(nb, rows // tr),
        in_specs=[pl.BlockSpec((1, tok, LANES), lambda j, i: (j, i, 0)),
                  pl.BlockSpec((1, tr, k), lambda j, i: (j, i, 0)),
                  pl.BlockSpec((1, k, cw), lambda j, i: (j, 0, 0)),
                  pl.BlockSpec((1, k, cw), lambda j, i: (j, 0, 0)),
                  pl.BlockSpec((1, 1, k), lambda j, i: (j, 0, 0)),
                  const(rep_m), const(mask_m), const(rep_e), const(mask_e)],
        out_specs=pl.BlockSpec((1, tok, LANES), lambda j, i: (j, i, 0)),
        scratch_shapes=[pltpu.VMEM((k, k), BF16), pltpu.VMEM((k, k), BF16), pltpu.VMEM((tr, k), F32)],
        compiler_params=_cparams(("parallel", "arbitrary")),
        name="s5_outputs",
    )(u_blk, s_rows, m_comp, c_comp, d_rows, rep_m, mask_m, rep_e, mask_e)


def _s5_branch_blocked(u_blk, bsz, l_ctx, l_lat, a_re, a_im, log_dt, b_re, b_im, c_re, c_im, s5_d):
    nb, m_all, _ = u_blk.shape
    hg, t_n, p_n = S5_GROUP, S5_CHUNK, S5_STATE
    g_n = a_re.shape[1]
    gl_n = g_n // nb
    k = t_n * LANES
    n_lat = l_lat // t_n
    n_ctx = l_ctx // t_n
    e_c, c_c, m_c, a16 = _s5_params2(a_re, a_im, log_dt, b_re, b_im, c_re, c_im, nb)
    cw = 4 * p_n
    e_comp, c_comp, m_comp = (a.reshape(nb, k, cw) for a in (e_c, c_c, m_c))

    row_gl = (jnp.arange(k) // hg) % gl_n
    col = jnp.arange(k)
    src = jnp.arange(cw)
    rep_e = ((src[:, None] // p_n == col[None, :] // (gl_n * p_n)) & (src[:, None] % p_n == col[None, :] % p_n)).astype(BF16)
    mask_e = (row_gl[:, None] == (jnp.arange(gl_n * p_n)[None, :] // p_n)).astype(BF16)
    rep_m = ((src[:, None] // hg == col[None, :] // (gl_n * hg)) & (src[:, None] % hg == col[None, :] % hg)).astype(BF16)
    mask_m = (row_gl[:, None] == (jnp.arange(gl_n * hg)[None, :] // hg)).astype(BF16)

    d_rows = jnp.tile(s5_d.reshape(nb, 1, gl_n * hg), (1, 1, t_n))
    a_rows = jnp.transpose(a16.reshape(2, 2, nb, gl_n * p_n), (2, 0, 1, 3)).reshape(nb, 1, 4 * gl_n * p_n)

    e_rows = _s5_chunk_inputs(u_blk, e_comp, rep_e, mask_e)
    s_rows = _s5_bscan(e_rows, a_rows, bsz, n_ctx, n_lat)
    return _s5_outputs(u_blk, s_rows, m_comp, c_comp, d_rows, rep_m, mask_m, rep_e, mask_e)


def _rw_prep_kernel(z_ref, zp_ref, zn_ref, mu_ref, w2_ref, a2_ref, g2_ref, w0_ref, a0_ref,
                    kk_w_ref, ka_ref, rk_ref, seg_ref, segt_ref,
                    r_ref, v_ref, kk_ref, g_ref, bonus_ref, lw_ref, kd_ref, be_ref,
                    *, tm, l_lat, rw):
    j = pl.program_id(1)
    z = z_ref[...].astype(F32)
    lat = j > 0
    tl = lax.broadcasted_iota(jnp.int32, (tm, 1), 0)
    tok = (j - 1) * tm + tl
    col = tl % GRID_W
    m_l = jnp.where(lat, col, tl) > 0
    m_r = jnp.where(lat, col - (GRID_W - 1), tl - (tm - 1)) < 0
    m_u = jnp.logical_and(lat, tok >= GRID_W)
    m_d = jnp.logical_and(lat, tok < l_lat - GRID_W)
    z_ext = jnp.concatenate([zp_ref[...], z_ref[...], zn_ref[...]], axis=0)
    rel = (lax.broadcasted_iota(jnp.int32, (tm, tm + 2 * GRID_W), 1) - GRID_W
           - lax.broadcasted_iota(jnp.int32, (tm, tm + 2 * GRID_W), 0))
    pick = (jnp.logical_and(rel == -1, m_l) | jnp.logical_and(rel == 1, m_r)
            | jnp.logical_and(rel == -GRID_W, m_u) | jnp.logical_and(rel == GRID_W, m_d))
    s = _dot(jnp.where(pick, 1.0, 0.0).astype(z_ext.dtype), z_ext)
    cnt = (m_l.astype(F32) + m_r.astype(F32)) + (m_u.astype(F32) + m_d.astype(F32))
    zs = z + (s * (1.0 / cnt) - z) * mu_ref[...]

    r = zs[:, 0:rw]
    k = zs[:, rw:2 * rw]
    v = zs[:, 2 * rw:3 * rw]
    o = 3 * rw
    wd = zs[:, o:o + LANES]
    ad = zs[:, o + LANES:o + 2 * LANES]
    gd = zs[:, o + 2 * LANES:o + 3 * LANES]

    seg = seg_ref[...]
    segt = segt_ref[...]

    def head_sum(t):
        return _dot_hilo(_dot_hilo(t, seg), segt)

    g_ref[0] = _dot(_sigmoid(gd), g2_ref[...], "bf16").astype(g_ref.dtype)
    kk = k * kk_w_ref[...]
    kk = kk * lax.rsqrt(head_sum(kk * kk) + 1e-12)
    wl = w0_ref[...] + _dot(jnp.tanh(wd), w2_ref[...], "bf16")
    al = a0_ref[...] + _dot(ad, a2_ref[...], "bf16")
    r_ref[0] = r.astype(r_ref.dtype)
    v_ref[0] = v.astype(v_ref.dtype)
    kk_ref[0] = kk.astype(kk_ref.dtype)
    k_sum = jnp.zeros_like(r)
    for d in range(2):
        a = _sigmoid(al[:, d * rw:(d + 1) * rw])
        k_d = k * (1.0 + (a - 1.0) * ka_ref[...])
        k_sum = k_sum + k_d
        lw_ref[d, 0] = -math.exp(-0.5) * _sigmoid(wl[:, d * rw:(d + 1) * rw])
        kd_ref[d, 0] = k_d.astype(kd_ref.dtype)
        be_ref[d, 0] = (kk * a).astype(be_ref.dtype)
    bonus_ref[0] = (head_sum(r * rk_ref[...] * k_sum) * v).astype(bonus_ref.dtype)


def _rw_prep(z_rw, bsz, l_ctx, l_lat, mu, w2bd, a2bd, g2, w0cat, a0cat, k_k, k_a, r_k_flat, seg, segt):
    cols = z_rw.shape[1]
    tm = l_ctx
    rw = g2.shape[1]
    l_all = l_ctx + l_lat
    nblk = l_all // tm
    lat_blk = l_lat // tm
    hb = tm // GRID_W
    lat_hblk = l_lat // GRID_W

    def main_blk(b, j):
        return jnp.where(j == 0, bsz * lat_blk + b, b * lat_blk + j - 1)

    def prev_halo(b, j):
        return b * lat_hblk + jnp.maximum((j - 1) * hb - 1, 0)

    def next_halo(b, j):
        return b * lat_hblk + jnp.minimum(jnp.maximum(j, 1) * hb, lat_hblk - 1)

    full = lambda shape: pl.BlockSpec(shape, lambda b, j: (0,) * len(shape))
    shared = jax.ShapeDtypeStruct((bsz, l_all, rw), BF16)
    lat_only = jax.ShapeDtypeStruct((bsz, l_lat, rw), BF16)
    per_dir = jax.ShapeDtypeStruct((2, bsz, l_all, rw), BF16)
    per_dir_f32 = jax.ShapeDtypeStruct((2, bsz, l_all, rw), F32)
    o_shared = pl.BlockSpec((1, tm, rw), lambda b, j: (b, j, 0))
    o_lat = pl.BlockSpec((1, tm, rw), lambda b, j: (b, jnp.maximum(j - 1, 0), 0))
    o_dir = pl.BlockSpec((2, 1, tm, rw), lambda b, j: (0, b, j, 0))
    return pl.pallas_call(
        functools.partial(_rw_prep_kernel, tm=tm, l_lat=l_lat, rw=rw),
        out_shape=(shared,) * 3 + (lat_only,) * 2 + (per_dir_f32, per_dir, per_dir),
        grid=(bsz, nblk),
        in_specs=[pl.BlockSpec((tm, cols), lambda b, j: (main_blk(b, j), 0)),
                  pl.BlockSpec((GRID_W, cols), lambda b, j: (prev_halo(b, j), 0)),
                  pl.BlockSpec((GRID_W, cols), lambda b, j: (next_halo(b, j), 0)),
                  full((1, cols)), full(w2bd.shape), full(a2bd.shape), full(g2.shape),
                  full((1, 2 * rw)), full((1, 2 * rw)), full((1, rw)), full((1, rw)), full((1, rw)),
                  full(seg.shape), full(segt.shape)],
        out_specs=(o_shared,) * 3 + (o_lat,) * 2 + (o_dir,) * 3,
        compiler_params=_cparams(("parallel", "arbitrary")),
        name="rwkv_prep",
    )(z_rw, z_rw, z_rw, mu.reshape(1, cols), w2bd, a2bd, g2, w0cat, a0cat,
      k_k.reshape(1, rw), k_a.reshape(1, rw), r_k_flat.reshape(1, rw), seg, segt)


def _stack_heads(x, head0):
    return jnp.concatenate([jnp.where(head0, x, 0.0), jnp.where(head0, 0.0, x)], axis=0)


def _rw_chunk_kernel(r_ref, v_ref, kk_ref, lw_ref, kd_ref, be_ref, y_ref,
                     z_scr, gc_s, lhs_s, kb_s, kht_s, bht_s, vs_s, rt_s, aab_s, akr_s, arb_s, t_s, x_s, xin_s,
                     yc_s, qy_s, bw_s, n_s, y_s, *, n_pairs, n_sub):
    c_n = RW_CHUNK
    n2 = 2 * c_n
    rev = (pl.program_id(0) % 2) == 1

    @pl.when(pl.program_id(1) == 0)
    def _():
        z_scr[...] = jnp.zeros_like(z_scr)

    ri = lax.broadcasted_iota(jnp.int32, (c_n, c_n), 0)
    ci = lax.broadcasted_iota(jnp.int32, (c_n, c_n), 1)
    tri = (jnp.where(rev, ci - ri, ri - ci) >= 0).astype(F32)
    r2 = lax.broadcasted_iota(jnp.int32, (n2, n2), 0)
    c2 = lax.broadcasted_iota(jnp.int32, (n2, n2), 1)
    t2 = r2 % c_n
    i2 = c2 % c_n
    same_head = (r2 // c_n) == (c2 // c_n)
    before = jnp.logical_and(same_head, jnp.where(rev, i2 - t2, t2 - i2) > 0)
    upto = jnp.logical_or(before, r2 == c2)
    eye = (r2 == c2).astype(F32)
    head0 = lax.broadcasted_iota(jnp.int32, (1, LANES), 1) < RW_HEAD

    def blk(s):
        return (r2 // s) == (c2 // s)

    pairs = range(n_pairs)
    units = range(n_sub * n_pairs)
    lanes_of = lambda p: slice(p * LANES, (p + 1) * LANES)

    row_of = lambda k: pl.multiple_of(jnp.where(rev, n_sub - 1 - k, k) * c_n, c_n)
    for cc in range(n_sub):
        rows = pl.ds(row_of(cc), c_n)
        lw = lw_ref[0, 0, rows, :]
        cum = _dot(tri, lw, HIGHEST)
        tot = jnp.sum(lw, axis=0, keepdims=True)
        g_inv = jnp.exp(-cum)
        g_hat = jnp.exp(tot - cum)
        kd = kd_ref[0, 0, rows, :]
        be = be_ref[0, 0, rows, :]
        at = kk_ref[0, rows, :] * jnp.exp(cum - lw)
        rt = r_ref[0, rows, :] * jnp.exp(cum)
        kt = kd * g_inv
        bt = be * g_inv
        kh = kd * g_hat
        bh = be * g_hat
        gc_s[cc] = jnp.exp(tot)
        v = v_ref[0, rows, :]
        for p in pairs:
            u = cc * n_pairs + p
            st = lambda x: _stack_heads(x[:, lanes_of(p)], head0)
            rt_p = st(rt)
            at_p = st(at).astype(BF16)
            lhs_s[u, :n2] = at_p
            lhs_s[u, n2:] = rt_p.astype(BF16)
            xin_s[u, :, :LANES] = at_p
            rt_s[u] = rt_p
            kb_s[u, :n2] = st(kt).astype(BF16)
            kb_s[u, n2:] = st(bt).astype(BF16)
            kht_s[u] = st(kh).T.astype(BF16)
            bht_s[u] = st(bh).T.astype(BF16)
            vs_s[u] = st(v).astype(BF16)

    for u in units:
        g = _dot_nt(lhs_s[u], kb_s[u])
        a_ab = jnp.where(before, g[:n2, n2:], 0.0)
        akr_s[u, :n2] = jnp.where(before, g[:n2, :n2], 0.0).astype(BF16)
        akr_s[u, n2:] = jnp.where(upto, g[n2:, :n2], 0.0).astype(BF16)
        arb_s[u] = jnp.where(upto, g[n2:, n2:], 0.0).astype(BF16)
        aab_s[u] = a_ab
        t_s[u] = eye - jnp.where(blk(2), a_ab, 0.0)

    s = 2
    while s < c_n:
        off = jnp.logical_and(blk(2 * s), jnp.logical_not(blk(s)))
        for u in units:
            x_s[u] = _dot(t_s[u].astype(BF16), jnp.where(off, aab_s[u], 0.0).astype(BF16)).astype(BF16)
        for u in units:
            t = t_s[u]
            t_s[u] = t - _dot(x_s[u], t.astype(BF16))
        s *= 2

    for u in units:
        av = _dot(akr_s[u], vs_s[u])
        xin_s[u, :, LANES:] = av[:n2].astype(BF16)
        yc_s[u] = av[n2:]

    for u in units:
        wu = _dot(t_s[u].astype(BF16), xin_s[u]).astype(BF16)
        q = _dot(arb_s[u], wu)
        bwu = _dot(bht_s[u], wu)
        qy_s[u] = (rt_s[u] - q[:, :LANES]).astype(BF16)
        yc_s[u] = yc_s[u] - q[:, LANES:]
        bw_s[u] = bwu[:, :LANES]
        n_s[u] = _dot(kht_s[u], vs_s[u]) - bwu[:, LANES:]

    for k in range(n_sub):
        g_c = gc_s[k]
        for p in pairs:
            u = k * n_pairs + p
            z0 = z_scr[p].astype(BF16)
            y_p = _dot(qy_s[u], z0) + yc_s[u]
            y_s[k, :, lanes_of(p)] = y_p[:c_n] + y_p[c_n:]
            m_z = eye * g_c[:, lanes_of(p)] - bw_s[u]
            z_scr[p] = _dot(m_z.astype(BF16), z0) + n_s[u]
    for k in range(n_sub):
        y_ref[0, 0, pl.ds(row_of(k), c_n), :] = y_s[k]


def _rw_scan(r, v, kk, lw, kd, be, l_ctx):
    bsz, l_all, rw = r.shape
    n_sub = RW_SUB
    c_n = RW_CHUNK
    rows = n_sub * c_n
    n_all = l_all // rows
    n_ctx = l_ctx // rows
    n_lat = n_all - n_ctx
    n_pairs = rw // LANES
    n2 = 2 * c_n
    vm = lambda nr, cols, dt: pltpu.VMEM((n_sub * n_pairs, nr, cols), dt)

    def chunk_of(bd, s):
        fwd = s
        bwd = jnp.where(s < n_ctx, n_ctx - 1 - s, n_all + n_ctx - 1 - s)
        return jnp.where(bd % 2 == 0, fwd, bwd)

    def out_chunk(bd, s):
        c = chunk_of(bd, s)
        edge = jnp.where(bd % 2 == 0, 0, n_lat - 1)
        return jnp.where(s < n_ctx, edge, c - n_ctx)

    shared = pl.BlockSpec((1, rows, rw), lambda bd, s: (bd // 2, chunk_of(bd, s), 0))
    per_dir = pl.BlockSpec((1, 1, rows, rw), lambda bd, s: (bd % 2, bd // 2, chunk_of(bd, s), 0))
    return pl.pallas_call(
        functools.partial(_rw_chunk_kernel, n_pairs=n_pairs, n_sub=n_sub),
        out_shape=jax.ShapeDtypeStruct((2, bsz, n_lat * rows, rw), F32),
        grid=(2 * bsz, n_all),
        in_specs=[shared, shared, shared, per_dir, per_dir, per_dir],
        out_specs=pl.BlockSpec((1, 1, rows, rw), lambda bd, s: (bd % 2, bd // 2, out_chunk(bd, s), 0)),
        scratch_shapes=[pltpu.VMEM((n_pairs, n2, n2), F32),
                        pltpu.VMEM((n_sub, 1, rw), F32),
                        vm(2 * n2, n2, BF16),
                        vm(2 * n2, n2, BF16),
                        vm(n2, n2, BF16),
                        vm(n2, n2, BF16),
                        vm(n2, n2, BF16),
                        vm(n2, n2, F32),
                        vm(n2, n2, F32),
                        vm(2 * n2, n2, BF16),
                        vm(n2, n2, BF16),
                        vm(n2, n2, F32),
                        vm(n2, n2, BF16),
                        vm(n2, 2 * n2, BF16),
                        vm(n2, n2, F32),
                        vm(n2, n2, BF16),
                        vm(n2, n2, F32),
                        vm(n2, n2, F32),
                        pltpu.VMEM((n_sub, c_n, rw), F32)],
        compiler_params=_cparams(("parallel", "arbitrary")),
        name="rwkv_scan",
    )(r, v, kk, lw, kd, be)


def _rw_step_kernel(rf_ref, rb_ref, vf_ref, vb_ref, kkf_ref, kkb_ref, lwf_ref, lwb_ref, kdf_ref, kdb_ref,
                    bef_ref, beb_ref, yf_ref, yb_ref,
                    z_scr, gc_s, lhs_s, kb_s, kht_s, bht_s, vs_s, rt_s, aab_s, akr_s, arb_s, t_s, x_s, xin_s,
                    yc_s, wu_s, qm_s, yn_s, *, n_pairs, bsz):
    c_n = RW_CHUNK
    n2 = 2 * c_n

    lanes_of = lambda p: slice(p * LANES, (p + 1) * LANES)
    unit = lambda d, b, p: (d * bsz + b) * n_pairs + p
    units = [(d, b, p) for d in range(2) for b in range(bsz) for p in range(n_pairs)]

    @pl.when(pl.program_id(0) == 0)
    def _():
        z_scr[...] = jnp.zeros_like(z_scr)
        qm_s[...] = jnp.zeros_like(qm_s)
        yn_s[...] = jnp.zeros_like(yn_s)

    for d, b, p in units:
        u = unit(d, b, p)
        y_ref = yf_ref if d == 0 else yb_ref
        yz = _dot(qm_s[u], z_scr[u].astype(BF16)) + yn_s[u]
        y_ref[b, :, lanes_of(p)] = yz[:c_n] + yz[c_n:n2]
        z_scr[u] = yz[n2:]

    ri = lax.broadcasted_iota(jnp.int32, (c_n, c_n), 0)
    ci = lax.broadcasted_iota(jnp.int32, (c_n, c_n), 1)
    r2 = lax.broadcasted_iota(jnp.int32, (n2, n2), 0)
    c2 = lax.broadcasted_iota(jnp.int32, (n2, n2), 1)
    t2 = r2 % c_n
    i2 = c2 % c_n
    same_head = (r2 // c_n) == (c2 // c_n)
    diag = r2 == c2
    eye = diag.astype(F32)
    head0 = lax.broadcasted_iota(jnp.int32, (1, LANES), 1) < RW_HEAD
    tri = [(ri >= ci).astype(F32), (ri <= ci).astype(F32)]
    before = [jnp.logical_and(same_head, i2 < t2), jnp.logical_and(same_head, i2 > t2)]
    upto = [jnp.logical_or(m, diag) for m in before]

    def blk(s):
        return (r2 // s) == (c2 // s)

    srcs = [(rf_ref, vf_ref, kkf_ref, lwf_ref, kdf_ref, bef_ref), (rb_ref, vb_ref, kkb_ref, lwb_ref, kdb_ref, beb_ref)]

    for d in range(2):
        r_ref, v_ref, kk_ref, lw_ref, kd_ref, be_ref = srcs[d]
        for b in range(bsz):
            lw = lw_ref[0, b]
            cum = _dot(tri[d], lw, HIGHEST)
            tot = jnp.sum(lw, axis=0, keepdims=True)
            g_inv = jnp.exp(-cum)
            g_hat = jnp.exp(tot - cum)
            kd = kd_ref[0, b].astype(F32)
            be = be_ref[0, b].astype(F32)
            at = kk_ref[b].astype(F32) * jnp.exp(cum - lw)
            rt = r_ref[b].astype(F32) * jnp.exp(cum)
            kt = kd * g_inv
            bt = be * g_inv
            kh = kd * g_hat
            bh = be * g_hat
            gc_s[d * bsz + b] = jnp.exp(tot)
            v = v_ref[b].astype(F32)
            for p in range(n_pairs):
                u = unit(d, b, p)
                st = lambda x: _stack_heads(x[:, lanes_of(p)], head0)
                rt_p = st(rt)
                at_p = st(at).astype(BF16)
                lhs_s[u, :n2] = at_p
                lhs_s[u, n2:] = rt_p.astype(BF16)
                xin_s[u, :, :LANES] = at_p
                rt_s[u] = rt_p
                kb_s[u, :n2] = st(kt).astype(BF16)
                kb_s[u, n2:] = st(bt).astype(BF16)
                kht_s[u] = st(kh).T.astype(BF16)
                bht_s[u] = st(bh).T.astype(BF16)
                vs_s[u] = st(v).astype(BF16)

    for d, b, p in units:
        u = unit(d, b, p)
        g = _dot_nt(lhs_s[u], kb_s[u])
        a_ab = jnp.where(before[d], g[:n2, n2:], 0.0)
        akr_s[u, :n2] = jnp.where(before[d], g[:n2, :n2], 0.0).astype(BF16)
        akr_s[u, n2:] = jnp.where(upto[d], g[n2:, :n2], 0.0).astype(BF16)
        arb_s[u] = jnp.where(upto[d], g[n2:, n2:], 0.0).astype(BF16)
        aab_s[u] = a_ab
        t_s[u] = eye - jnp.where(blk(2), a_ab, 0.0)

    n_units = len(units)
    s = 2
    while s < c_n:
        off = jnp.logical_and(blk(2 * s), jnp.logical_not(blk(s)))
        for u in range(n_units):
            x_s[u] = _dot(t_s[u].astype(BF16), jnp.where(off, aab_s[u], 0.0).astype(BF16)).astype(BF16)
        for u in range(n_units):
            t = t_s[u]
            t_s[u] = t - _dot(x_s[u], t.astype(BF16))
        s *= 2

    for u in range(n_units):
        av = _dot(akr_s[u], vs_s[u])
        xin_s[u, :, LANES:] = av[:n2].astype(BF16)
        yc_s[u] = av[n2:]

    for u in range(n_units):
        wu_s[u] = _dot(t_s[u].astype(BF16), xin_s[u]).astype(BF16)
    for d, b, p in units:
        u = unit(d, b, p)
        wu = wu_s[u]
        q = _dot(arb_s[u], wu)
        bwu = _dot(bht_s[u], wu)
        qm_s[u, :n2] = (rt_s[u] - q[:, :LANES]).astype(BF16)
        qm_s[u, n2:] = (eye * gc_s[d * bsz + b][:, lanes_of(p)] - bwu[:, :LANES]).astype(BF16)
        yn_s[u, :n2] = yc_s[u] - q[:, LANES:]
        yn_s[u, n2:] = _dot(kht_s[u], vs_s[u]) - bwu[:, LANES:]


def _rw_scan2(r, v, kk, lw, kd, be, l_ctx):
    bsz, l_all, rw = r.shape
    c_n = RW_CHUNK
    n_all = l_all // c_n
    n_ctx = l_ctx // c_n
    n_lat = n_all - n_ctx
    n_pairs = rw // LANES
    n_units = 2 * bsz * n_pairs
    n2 = 2 * c_n
    vm = lambda nr, cols, dt: pltpu.VMEM((n_units, nr, cols), dt)

    clamp = lambda s: jnp.minimum(s, n_all - 1)
    chunk_f = lambda s: clamp(s)
    chunk_b = lambda s: jnp.where(clamp(s) < n_ctx, n_ctx - 1 - clamp(s), n_all + n_ctx - 1 - clamp(s))
    prev = lambda s: jnp.maximum(s - 1, 0)
    out_f = lambda s: jnp.maximum(prev(s) - n_ctx, 0)
    out_b = lambda s: jnp.where(prev(s) < n_ctx, n_lat - 1, chunk_b(prev(s)) - n_ctx)

    sh_f = pl.BlockSpec((bsz, c_n, rw), lambda s: (0, chunk_f(s), 0))
    sh_b = pl.BlockSpec((bsz, c_n, rw), lambda s: (0, chunk_b(s), 0))
    pd_f = pl.BlockSpec((1, bsz, c_n, rw), lambda s: (0, 0, chunk_f(s), 0))
    pd_b = pl.BlockSpec((1, bsz, c_n, rw), lambda s: (1, 0, chunk_b(s), 0))
    y_shape = jax.ShapeDtypeStruct((bsz, n_lat * c_n, rw), F32)
    return pl.pallas_call(
        functools.partial(_rw_step_kernel, n_pairs=n_pairs, bsz=bsz),
        out_shape=(y_shape, y_shape),
        grid=(n_all + 1,),
        in_specs=[sh_f, sh_b, sh_f, sh_b, sh_f, sh_b, pd_f, pd_b, pd_f, pd_b, pd_f, pd_b],
        out_specs=(pl.BlockSpec((bsz, c_n, rw), lambda s: (0, out_f(s), 0)),
                   pl.BlockSpec((bsz, c_n, rw), lambda s: (0, out_b(s), 0))),
        scratch_shapes=[vm(n2, n2, F32),
                        pltpu.VMEM((2 * bsz, 1, rw), F32),
                        vm(2 * n2, n2, BF16),
                        vm(2 * n2, n2, BF16),
                        vm(n2, n2, BF16),
                        vm(n2, n2, BF16),
                        vm(n2, n2, BF16),
                        vm(n2, n2, F32),
                        vm(n2, n2, F32),
                        vm(2 * n2, n2, BF16),
                        vm(n2, n2, BF16),
                        vm(n2, n2, F32),
                        vm(n2, n2, BF16),
                        vm(n2, 2 * n2, BF16),
                        vm(n2, n2, F32),
                        vm(n2, 2 * n2, BF16),
                        vm(2 * n2, n2, BF16),
                        vm(2 * n2, n2, F32)],
        compiler_params=_cparams(("arbitrary",)),
        name="rwkv_scan",
    )(r, r, v, v, kk, kk, lw, lw, kd, kd, be, be)


def _s5_glu_kernel(y_ref, wa_ref, wb_ref, o_ref, h_scr):
    @pl.when(pl.program_id(1) == 0)
    def _():
        for jb in range(y_ref.shape[0]):
            h_scr[:, jb * LANES:(jb + 1) * LANES] = _gelu_tanh(y_ref[jb]).astype(BF16)

    h = h_scr[...]
    o_ref[...] = (_dot(h, wa_ref[...]) * _sigmoid(_dot(h, wb_ref[...]))).astype(o_ref.dtype)


def _s5_glu(y_blk, w, *, tm, tn):
    nb, m, _ = y_blk.shape
    k = nb * LANES
    n = w.shape[1] // 2
    nj = n // tn
    return pl.pallas_call(
        _s5_glu_kernel,
        out_shape=jax.ShapeDtypeStruct((m, n), BF16),
        grid=(m // tm, nj),
        in_specs=[pl.BlockSpec((nb, tm, LANES), lambda i, j: (0, i, 0)),
                  pl.BlockSpec((k, tn), lambda i, j: (0, j)),
                  pl.BlockSpec((k, tn), lambda i, j: (0, j + nj))],
        out_specs=pl.BlockSpec((tm, tn), lambda i, j: (i, j)),
        scratch_shapes=[pltpu.VMEM((tm, k), BF16)],
        compiler_params=_cparams(("parallel", "arbitrary")),
        name="s5_glu",
    )(y_blk, w, w)


def _rw_merge_kernel(yf_ref, yb_ref, bonus_ref, g_ref, lnw_ref, lnb_ref, seg_ref, segt_ref,
                     ga_ref, gb_ref, s5_ref, w_ref, o_ref, h_scr):
    @pl.when(pl.program_id(1) == 0)
    def _():
        seg = seg_ref[...]
        segt = segt_ref[...]
        inv_n = 1.0 / RW_HEAD

        def head_mean(t):
            return _dot_hilo(_dot_hilo(t, seg), segt) * inv_n

        y = yf_ref[...] + yb_ref[...]
        dy = y - head_mean(y)
        var = head_mean(dy * dy)
        y = dy * lax.rsqrt(var + GN_EPS) * lnw_ref[...] + lnb_ref[...] + bonus_ref[...].astype(F32)
        h_scr[...] = (y * g_ref[...].astype(F32)).astype(BF16)

    rw_out = _dot(h_scr[...], w_ref[...])
    merged = ga_ref[...].astype(F32) * s5_ref[...].astype(F32) + gb_ref[...].astype(F32) * rw_out
    o_ref[...] = merged.astype(o_ref.dtype)


def _rw_merge(y_f, y_b, bonus, g, ln_w, ln_b, seg, segt, gates, s5_out, w_proj, *, tm, tn):
    m, rw = y_f.shape
    n = w_proj.shape[1]
    nj = n // tn
    full = lambda shape: pl.BlockSpec(shape, lambda i, j: (0,) * len(shape))
    return pl.pallas_call(
        _rw_merge_kernel,
        out_shape=jax.ShapeDtypeStruct((m, n), BF16),
        grid=(m // tm, nj),
        in_specs=[pl.BlockSpec((tm, rw), lambda i, j: (i, 0)),
                  pl.BlockSpec((tm, rw), lambda i, j: (i, 0)),
                  pl.BlockSpec((tm, rw), lambda i, j: (i, 0)),
                  pl.BlockSpec((tm, rw), lambda i, j: (i, 0)),
                  full((1, rw)), full((1, rw)), full(seg.shape), full(segt.shape),
                  pl.BlockSpec((tm, tn), lambda i, j: (i, j)),
                  pl.BlockSpec((tm, tn), lambda i, j: (i, j + nj)),
                  pl.BlockSpec((tm, tn), lambda i, j: (i, j)),
                  pl.BlockSpec((rw, tn), lambda i, j: (0, j))],
        out_specs=pl.BlockSpec((tm, tn), lambda i, j: (i, j)),
        scratch_shapes=[pltpu.VMEM((tm, rw), BF16)],
        compiler_params=_cparams(("parallel", "arbitrary")),
        name="rwkv_merge",
    )(y_f, y_b, bonus, g, ln_w.reshape(1, rw), ln_b.reshape(1, rw), seg, segt,
      gates, gates, s5_out, w_proj)


def _resid_ln_kernel(a_ref, w_ref, x_ref, g_ref, nw_ref, sh_ref, sc_ref, h_ref, hn_ref):
    h = x_ref[...] + g_ref[0] * _dot(a_ref[...], w_ref[...])
    h_ref[...] = h
    hn_ref[...] = _lnmod_rows(h, nw_ref[...], sh_ref[0], sc_ref[0]).astype(hn_ref.dtype)


def _resid_matmul_ln(a, w, x2, g_tab, nw, sh_tab, sc_tab, mod_row_of_block, *, tm):
    m, k = a.shape
    n = w.shape[1]
    mod_map = lambda i: (mod_row_of_block(i), 0, 0)
    return pl.pallas_call(
        _resid_ln_kernel,
        out_shape=(jax.ShapeDtypeStruct((m, n), F32), jax.ShapeDtypeStruct((m, n), BF16)),
        grid=(m // tm,),
        in_specs=[pl.BlockSpec((tm, k), lambda i: (i, 0)),
                  pl.BlockSpec((k, n), lambda i: (0, 0)),
                  pl.BlockSpec((tm, n), lambda i: (i, 0)),
                  pl.BlockSpec((1, 1, n), mod_map),
                  pl.BlockSpec((1, n), lambda i: (0, 0)),
                  pl.BlockSpec((1, 1, n), mod_map),
                  pl.BlockSpec((1, 1, n), mod_map)],
        out_specs=(pl.BlockSpec((tm, n), lambda i: (i, 0)), pl.BlockSpec((tm, n), lambda i: (i, 0))),
        compiler_params=_cparams(("parallel",)),
        name="out_proj",
    )(a, w, x2, g_tab, nw.reshape(1, n), sh_tab, sc_tab)


def _wres_swiglu_kernel(a_ref, w1_ref, w3_ref, o_ref, w1_scr, w3_scr):
    @pl.when(pl.program_id(1) == 0)
    def _():
        w1_scr[...] = w1_ref[...].astype(BF16)
        w3_scr[...] = w3_ref[...].astype(BF16)

    a = a_ref[...]
    o_ref[...] = (_silu(_dot(a, w1_scr[...])) * _dot(a, w3_scr[...])).astype(o_ref.dtype)


def _wres_swiglu(a, w13, d_ff, *, tm, tn):
    m, k = a.shape
    nj = d_ff // tn
    return pl.pallas_call(
        _wres_swiglu_kernel,
        out_shape=jax.ShapeDtypeStruct((m, d_ff), BF16),
        grid=(nj, m // tm),
        in_specs=[pl.BlockSpec((tm, k), lambda j, i: (i, 0)),
                  pl.BlockSpec((k, tn), lambda j, i: (0, j)),
                  pl.BlockSpec((k, tn), lambda j, i: (0, j + nj))],
        out_specs=pl.BlockSpec((tm, tn), lambda j, i: (i, j)),
        scratch_shapes=[pltpu.VMEM((k, tn), BF16), pltpu.VMEM((k, tn), BF16)],
        compiler_params=_cparams(("arbitrary", "arbitrary")),
        name="ffn_up",
    )(a, w13, w13)


def _resid_mm_kernel(a_ref, w_ref, x_ref, g_ref, o_ref):
    o_ref[...] = x_ref[...] + g_ref[0] * _dot(a_ref[...], w_ref[...])


def _resid_matmul(a, w, x2, g_tab, mod_row_of_block, *, tm, tn):
    m, k = a.shape
    n = w.shape[1]
    return pl.pallas_call(
        _resid_mm_kernel,
        out_shape=jax.ShapeDtypeStruct((m, n), F32),
        grid=(m // tm, n // tn),
        in_specs=[pl.BlockSpec((tm, k), lambda i, j: (i, 0)),
                  pl.BlockSpec((k, tn), lambda i, j: (0, j)),
                  pl.BlockSpec((tm, tn), lambda i, j: (i, j)),
                  pl.BlockSpec((1, 1, tn), lambda i, j: (mod_row_of_block(i), 0, j))],
        out_specs=pl.BlockSpec((tm, tn), lambda i, j: (i, j)),
        compiler_params=_cparams(("parallel", "arbitrary")),
        name="out_proj",
    )(a, w, x2, g_tab)


def _ffn_down_kernel(a_ref, w_ref, x_ref, g_ref, nf_ref, o_ref):
    kk = pl.program_id(1)
    part = _dot(a_ref[...], w_ref[...])

    @pl.when(kk == 0)
    def _():
        o_ref[...] = part

    @pl.when(jnp.logical_and(kk > 0, kk < pl.num_programs(1) - 1))
    def _():
        o_ref[...] += part

    @pl.when(kk == pl.num_programs(1) - 1)
    def _():
        h = x_ref[...] + g_ref[0] * (o_ref[...] + part)
        ms = jnp.mean(h * h, axis=-1, keepdims=True)
        o_ref[...] = h * lax.rsqrt(ms + NORM_EPS) * nf_ref[...]


def _ffn_down(a, w, x2, g_tab, mod_row_of_block, norm_f, *, tm, tk):
    m, k = a.shape
    n = w.shape[1]
    return pl.pallas_call(
        _ffn_down_kernel,
        out_shape=jax.ShapeDtypeStruct((m, n), F32),
        grid=(m // tm, k // tk),
        in_specs=[pl.BlockSpec((tm, tk), lambda i, kk: (i, kk)),
                  pl.BlockSpec((tk, n), lambda i, kk: (kk, 0)),
                  pl.BlockSpec((tm, n), lambda i, kk: (i, 0)),
                  pl.BlockSpec((1, 1, n), lambda i, kk: (mod_row_of_block(i), 0, 0)),
                  pl.BlockSpec((1, n), lambda i, kk: (0, 0))],
        out_specs=pl.BlockSpec((tm, n), lambda i, kk: (i, 0)),
        compiler_params=_cparams(("parallel", "arbitrary"), vmem=VMEM_LIMIT + 8 * 1024 * 1024),
        name="ffn_down",
    )(a, w, x2, g_tab, norm_f.reshape(1, n))


def kernel(x, c, ctx, c_ctx, ada_w, ada_b, norm1_w, w_in, rw_mu, s5_a_re, s5_a_im, s5_log_dt, s5_b_re, s5_b_im, s5_c_re, s5_c_im, s5_d, s5_glu_w, rw_w0, rw_w2, rw_a0, rw_a2, rw_g2, rw_k_k, rw_k_a, rw_r_k, rw_ln_w, rw_ln_b, rw_proj, w_o, norm2_w, ffn_w13, ffn_w2, norm_f):
    assert ada_w.shape[0] == 1, "single-layer block"
    bsz, l_lat, d = x.shape
    l_ctx = ctx.shape[1]
    l_all = l_ctx + l_lat
    s5w = s5_d.shape[1] * s5_d.shape[2]
    rw = rw_g2.shape[2]
    shift_cols = rw_mu.shape[1]
    d_ff = ffn_w2.shape[1]
    n_heads = rw // RW_HEAD

    c_rows = jnp.concatenate([c, c_ctx[None], jnp.zeros((8 - bsz - 1, d), F32)], axis=0)
    mod = _modulation(c_rows, ada_w[0], ada_b[0])
    tab = lambda k: mod[:, k * d:(k + 1) * d].reshape(8, 1, d)
    sh1, sc1, g1, sh2, sc2, g2 = (tab(k) for k in range(N_MOD))
    ctx_row = bsz

    tm_in = 512
    m_lat = bsz * l_lat
    x2 = x.reshape(m_lat, d)
    n_mix = s5w + shift_cols
    m_all = m_lat + bsz * l_ctx
    lat_row = lambda t: (lambda i: i // (l_lat // t))

    def mix_mod_row(i):
        return jnp.where(i >= m_lat // tm_in, ctx_row, i // (l_lat // tm_in))

    h_all = _lnmod(x2, ctx.reshape(bsz * l_ctx, d), norm1_w[0], sh1, sc1, mix_mod_row, tm=tm_in)
    tm_all = m_all // 8
    u_blk = _wres_matmul(h_all, w_in[0], 0, s5w, rows=m_all, tm=tm_all, tn=s5w,
                         out_dtype=F32, epilogue="lane_blocks", name="in_proj_s5")
    z_rw = _wres_matmul(h_all, w_in[0], s5w, shift_cols, rows=m_all, tm=tm_all, tn=shift_cols // 3,
                        out_dtype=BF16, name="in_proj_rw")
    tm = 1024
    gates = _wres_matmul(h_all, w_in[0], n_mix, w_in.shape[2] - n_mix, rows=m_lat, tm=tm, tn=1024,
                         out_dtype=BF16, epilogue="sigmoid", name="in_proj_gates")

    y_blk = _s5_branch_blocked(u_blk, bsz, l_ctx, l_lat, s5_a_re[0], s5_a_im[0], s5_log_dt[0],
                               s5_b_re[0], s5_b_im[0], s5_c_re[0], s5_c_im[0], s5_d[0])
    s5_out = _s5_glu(y_blk, s5_glu_w[0].astype(BF16), tm=tm, tn=1024)

    lora = rw_w2.shape[2]
    zl = jnp.zeros((lora, rw), F32)
    w2bd = jnp.concatenate([jnp.concatenate([rw_w2[0, 0], zl], axis=1),
                            jnp.concatenate([zl, rw_w2[0, 1]], axis=1)], axis=0)
    a2bd = jnp.concatenate([jnp.concatenate([rw_a2[0, 0], zl], axis=1),
                            jnp.concatenate([zl, rw_a2[0, 1]], axis=1)], axis=0)
    head_of = jnp.arange(rw) // RW_HEAD
    seg = (head_of[:, None] == jnp.arange(LANES)[None, :]).astype(BF16)
    segt = seg.T
    r, v, kk, g, bonus, lw, kd, be = _rw_prep(
        z_rw, bsz, l_ctx, l_lat, rw_mu[0], w2bd, a2bd, rw_g2[0], rw_w0[0].reshape(1, 2 * rw),
        rw_a0[0].reshape(1, 2 * rw), rw_k_k[0], rw_k_a[0], rw_r_k[0].reshape(rw), seg, segt)
    y_f, y_b = _rw_scan2(r, v, kk, lw, kd, be, l_ctx)

    merged = _rw_merge(y_f.reshape(m_lat, rw), y_b.reshape(m_lat, rw), bonus.reshape(m_lat, rw), g.reshape(m_lat, rw),
                       rw_ln_w[0], rw_ln_b[0], seg, segt, gates, s5_out, rw_proj[0].astype(BF16),
                       tm=512, tn=2048)
    tm_o = 512
    h1, h1n = _resid_matmul_ln(merged, w_o[0].astype(BF16), x2, g1, norm2_w[0], sh2, sc2, lat_row(tm_o), tm=tm_o)

    act = _wres_swiglu(h1n, ffn_w13[0], d_ff, tm=tm, tn=512)
    tm_dn = 512
    out = _ffn_down(act, ffn_w2[0].astype(BF16), h1, g2, lat_row(tm_dn), norm_f, tm=tm_dn, tk=d_ff // 2)
    return out.reshape(bsz, l_lat, d)
```

```python
import functools
import math

import jax
import jax.numpy as jnp
from jax import lax
from jax.experimental import pallas as pl
from jax.experimental.pallas import tpu as pltpu

F32 = jnp.float32
BF16 = jnp.bfloat16
HIGHEST = lax.Precision.HIGHEST

N_MOD = 6
NORM_EPS = 1e-6
GN_EPS = 64e-5
GRID_W = 64
S5_GROUP = 16
S5_STATE = 64
S5_CHUNK = 16
RW_HEAD = 64
RW_CHUNK = 64
LANES = 128
VMEM_LIMIT = 48 * 1024 * 1024


def _cparams(sem, vmem=VMEM_LIMIT):
    return pltpu.CompilerParams(dimension_semantics=sem, vmem_limit_bytes=vmem)


def _operands(a, b, precision):
    if precision == "bf16":
        return a.astype(BF16), b.astype(BF16), None
    return a, b, precision


def _dot(a, b, precision=None):
    a, b, precision = _operands(a, b, precision)
    return jnp.dot(a, b, preferred_element_type=F32, precision=precision)


def _dot_nt(a, b, precision=None):
    a, b, precision = _operands(a, b, precision)
    return lax.dot_general(a, b, (((1,), (1,)), ((), ())), preferred_element_type=F32, precision=precision)


def _dot_hilo(a, ind):
    hi = a.astype(BF16)
    lo = (a - hi.astype(F32)).astype(BF16)
    return _dot(hi, ind) + _dot(lo, ind)


def _sigmoid(x):
    return 1.0 / (1.0 + jnp.exp(-x))


def _silu(x):
    return x * _sigmoid(x)


def _gelu_tanh(x):
    c = math.sqrt(2.0 / math.pi)
    return 0.5 * x * (1.0 + jnp.tanh(c * (x + 0.044715 * (x * x * x))))


def _mod_kernel(c_ref, w_ref, b_ref, o_ref):
    o_ref[...] = _dot_hilo(_silu(c_ref[...]), w_ref[...].astype(BF16)) + b_ref[...]


def _modulation(c_rows, ada_w, ada_b):
    m, d = c_rows.shape
    n = ada_w.shape[1]
    tn = 1024
    return pl.pallas_call(
        _mod_kernel,
        out_shape=jax.ShapeDtypeStruct((m, n), F32),
        grid=(n // tn,),
        in_specs=[pl.BlockSpec((m, d), lambda j: (0, 0)),
                  pl.BlockSpec((d, tn), lambda j: (0, j)),
                  pl.BlockSpec((1, tn), lambda j: (0, j))],
        out_specs=pl.BlockSpec((m, tn), lambda j: (0, j)),
        compiler_params=_cparams(("arbitrary",)),
        name="modulation",
    )(c_rows, ada_w, ada_b.reshape(1, n))


def _lnmod_rows(x, nw, sh, sc):
    ms = jnp.mean(x * x, axis=-1, keepdims=True)
    y = x * lax.rsqrt(ms + NORM_EPS) * nw
    return y * (1.0 + sc) + sh


def _lnmod_kernel(x_ref, c_ref, nw_ref, sh_ref, sc_ref, o_ref, *, n_lat_blocks):
    rows = jnp.where(pl.program_id(0) < n_lat_blocks, x_ref[...], c_ref[...])
    o_ref[...] = _lnmod_rows(rows, nw_ref[...], sh_ref[0], sc_ref[0]).astype(o_ref.dtype)


def _lnmod(x2, c2, nw, sh_tab, sc_tab, mod_row_of_block, *, tm):
    m, d = x2.shape
    mc = c2.shape[0]
    nx, nc = m // tm, mc // tm
    mod_map = lambda i: (mod_row_of_block(i), 0, 0)
    return pl.pallas_call(
        functools.partial(_lnmod_kernel, n_lat_blocks=nx),
        out_shape=jax.ShapeDtypeStruct((m + mc, d), BF16),
        grid=(nx + nc,),
        in_specs=[pl.BlockSpec((tm, d), lambda i: (jnp.minimum(i, nx - 1), 0)),
                  pl.BlockSpec((tm, d), lambda i: (jnp.maximum(i - nx, 0), 0)),
                  pl.BlockSpec((1, d), lambda i: (0, 0)),
                  pl.BlockSpec((1, 1, d), mod_map),
                  pl.BlockSpec((1, 1, d), mod_map)],
        out_specs=pl.BlockSpec((tm, d), lambda i: (i, 0)),
        compiler_params=_cparams(("parallel",)),
        name="lnmod",
    )(x2, c2, nw.reshape(1, d), sh_tab, sc_tab)


def _wres_mm_kernel(a_ref, w_ref, o_ref, w_scr, *, epilogue):
    @pl.when(pl.program_id(1) == 0)
    def _():
        w_scr[...] = w_ref[...].astype(BF16)

    z = _dot(a_ref[...], w_scr[...])
    if epilogue == "sigmoid":
        z = _sigmoid(z)
    if epilogue == "lane_blocks":
        for jb in range(o_ref.shape[0]):
            o_ref[jb] = z[:, jb * LANES:(jb + 1) * LANES].astype(o_ref.dtype)
    else:
        o_ref[...] = z.astype(o_ref.dtype)


def _wres_matmul(a, w, col0, n, *, rows, tm, tn, out_dtype, epilogue=None, name):
    k = a.shape[1]
    assert col0 % LANES == 0 and n % tn == 0 and rows % tm == 0
    if epilogue == "lane_blocks":
        assert tn == n
        out_shape = jax.ShapeDtypeStruct((n // LANES, rows, LANES), out_dtype)
        out_spec = pl.BlockSpec((n // LANES, tm, LANES), lambda j, i: (0, i, 0))
    else:
        out_shape = jax.ShapeDtypeStruct((rows, n), out_dtype)
        out_spec = pl.BlockSpec((tm, tn), lambda j, i: (i, j))
    return pl.pallas_call(
        functools.partial(_wres_mm_kernel, epilogue=epilogue),
        out_shape=out_shape,
        grid=(n // tn, rows // tm),
        in_specs=[pl.BlockSpec((tm, k), lambda j, i: (i, 0)),
                  pl.BlockSpec((pl.Element(k), pl.Element(tn)), lambda j, i: (0, pl.multiple_of(col0 + j * tn, LANES)))],
        out_specs=out_spec,
        scratch_shapes=[pltpu.VMEM((k, tn), BF16)],
        compiler_params=_cparams(("arbitrary", "arbitrary")),
        name=name,
    )(a, w)


def _s5_param_kernel(are_ref, aim_ref, ldt_ref, bre_ref, bim_ref, cre_ref, cim_ref,
                      e_ref, c_ref, m_ref, a16_ref):
    t_n, hg, p_n = S5_CHUNK, S5_GROUP, S5_STATE
    gl_n = are_ref.shape[1]
    tau = lax.broadcasted_iota(jnp.int32, (t_n, gl_n, p_n), 0).astype(F32)
    taps = {}
    for d in range(2):
        a_re = are_ref[d]
        a_im = aim_ref[d]
        dt = jnp.exp(ldt_ref[d])
        lam = a_re * dt
        th = a_im * dt
        er = jnp.exp(lam)
        ab_re = er * jnp.cos(th)
        ab_im = er * jnp.sin(th)
        den = a_re * a_re + a_im * a_im
        x_re = ab_re - 1.0
        co_re = (x_re * a_re + ab_im * a_im) / den
        co_im = (ab_im * a_re - x_re * a_im) / den

        def power(tv):
            mag = jnp.exp(tv * lam)
            return mag * jnp.cos(tv * th), mag * jnp.sin(tv * th)

        pw_re, pw_im = power(tau if d == 0 else (t_n - 1.0 - tau))
        pe_re, pe_im = power((t_n - 1.0 - tau) if d == 0 else tau)
        pc_re, pc_im = power((tau + 1.0) if d == 0 else (t_n - tau))
        mag16 = jnp.exp(float(t_n) * lam)
        a16_ref[d, 0] = mag16 * jnp.cos(float(t_n) * th)
        a16_ref[d, 1] = mag16 * jnp.sin(float(t_n) * th)

        lo, hi = 2 * d * p_n, (2 * d + 1) * p_n
        for gl in range(gl_n):
            of = lambda x: x[:, gl:gl + 1, :]
            bt_re = bre_ref[d, gl].T
            bt_im = bim_ref[d, gl].T
            bb_re = co_re[gl:gl + 1] * bt_re - co_im[gl:gl + 1] * bt_im
            bb_im = co_re[gl:gl + 1] * bt_im + co_im[gl:gl + 1] * bt_re
            c_re = cre_ref[d, gl]
            c_im = cim_ref[d, gl]
            cp_re = (c_re[None] * of(pw_re) - c_im[None] * of(pw_im)).reshape(t_n * hg, p_n)
            cp_im = (c_re[None] * of(pw_im) + c_im[None] * of(pw_re)).reshape(t_n * hg, p_n)
            taps[d, gl] = _dot_nt(bb_re, cp_re, HIGHEST) - _dot_nt(bb_im, cp_im, HIGHEST)
            e_ref[0, :, gl, :, lo:hi] = of(pe_re) * bb_re[None] - of(pe_im) * bb_im[None]
            e_ref[0, :, gl, :, hi:hi + p_n] = of(pe_re) * bb_im[None] + of(pe_im) * bb_re[None]
            c_ref[0, :, gl, :, lo:hi] = c_re[None] * of(pc_re) - c_im[None] * of(pc_im)
            c_ref[0, :, gl, :, hi:hi + p_n] = -(c_re[None] * of(pc_im) + c_im[None] * of(pc_re))

    width = t_n * hg
    lane = lax.broadcasted_iota(jnp.int32, (hg, width), 1)
    for gl in range(gl_n):
        tf, tb = taps[0, gl], taps[1, gl]
        for t in range(t_n):
            sf = t * hg
            sb = (t_n - 1 - t) * hg
            f = tf if sf == 0 else jnp.where(lane >= sf, pltpu.roll(tf, sf, 1), 0.0)
            b = tb if sb == 0 else jnp.where(lane < width - sb, pltpu.roll(tb, width - sb, 1), 0.0)
            m_ref[0, t, gl] = f + b


def _s5_params(a_re, a_im, log_dt, b_re, b_im, c_re, c_im, nb):
    g_n = a_re.shape[1]
    gl_n = g_n // nb
    p_n, hg, t_n = S5_STATE, S5_GROUP, S5_CHUNK
    ldt = jnp.broadcast_to(log_dt[:, :, None], (2, g_n, p_n))
    spec3 = pl.BlockSpec((2, gl_n, p_n), lambda j: (0, j, 0))
    spec4 = pl.BlockSpec((2, gl_n, hg, p_n), lambda j: (0, j, 0, 0))
    spec_b = pl.BlockSpec((2, gl_n, p_n, hg), lambda j: (0, j, 0, 0))
    wide = 4 * p_n
    comp = jax.ShapeDtypeStruct((nb, t_n, gl_n, hg, wide), F32)
    comp_spec = pl.BlockSpec((1, t_n, gl_n, hg, wide), lambda j: (j, 0, 0, 0, 0))
    return pl.pallas_call(
        _s5_param_kernel,
        out_shape=(comp, comp, comp, jax.ShapeDtypeStruct((2, 2, g_n, p_n), F32)),
        grid=(nb,),
        in_specs=[spec3, spec3, spec3, spec_b, spec_b, spec4, spec4],
        out_specs=(comp_spec, comp_spec, comp_spec, pl.BlockSpec((2, 2, gl_n, p_n), lambda j: (0, 0, j, 0))),
        compiler_params=_cparams(("parallel",)),
        name="s5_params",
    )(a_re, a_im, ldt, b_re, b_im, c_re, c_im)


def _expand_block_diag(comp, rep_ref, mask_ref, w_scr):
    k = w_scr.shape[0]
    period = mask_ref.shape[1]
    cb = comp.astype(BF16)
    step = 512
    for c0 in range(0, k, step):
        blk = _dot(cb, rep_ref[:, c0:c0 + step]).astype(BF16)
        for q0 in range(0, step, period):
            w_scr[:, c0 + q0:c0 + q0 + period] = blk[:, q0:q0 + period] * mask_ref[...]


def _gather_chunk_rows(u_ref, rows, dst):
    t_n = S5_CHUNK
    for t in range(t_n):
        dst[:, t * LANES:(t + 1) * LANES] = u_ref[0, pl.ds(t, rows, stride=t_n), :].astype(dst.dtype)


def _s5_ein_kernel(u_ref, ec_ref, rep_ref, mask_ref, o_ref, w_scr, u_scr):
    _expand_block_diag(ec_ref[0], rep_ref, mask_ref, w_scr)
    _gather_chunk_rows(u_ref, u_scr.shape[0], u_scr)
    o_ref[0] = _dot(u_scr[...], w_scr[...])


def _s5_chunk_inputs(u_blk, e_comp, rep_e, mask_e):
    nb, m_all, _ = u_blk.shape
    k, cw = e_comp.shape[1:]
    rows = m_all // S5_CHUNK
    return pl.pallas_call(
        _s5_ein_kernel,
        out_shape=jax.ShapeDtypeStruct((nb, rows, k), F32),
        grid=(nb,),
        in_specs=[pl.BlockSpec((1, m_all, LANES), lambda j: (j, 0, 0)),
                  pl.BlockSpec((1, k, cw), lambda j: (j, 0, 0)),
                  pl.BlockSpec(rep_e.shape, lambda j: (0, 0)),
                  pl.BlockSpec(mask_e.shape, lambda j: (0, 0))],
        out_specs=pl.BlockSpec((1, rows, k), lambda j: (j, 0, 0)),
        scratch_shapes=[pltpu.VMEM((k, k), BF16), pltpu.VMEM((rows, k), BF16)],
        compiler_params=_cparams(("parallel",)),
        name="s5_chunk_inputs",
    )(u_blk, e_comp, rep_e, mask_e)


def _s5_bscan_kernel(e_ref, a_ref, o_ref, *, bsz, n_ctx, n_lat):
    q = e_ref.shape[2] // 4
    planes = lambda row, d: (row[:, (2 * d) * q:(2 * d + 1) * q], row[:, (2 * d + 1) * q:(2 * d + 2) * q])
    coef = [planes(a_ref[0], d) for d in range(2)]
    ctx0 = bsz * n_lat

    def advance(state, rows):
        new = []
        for (sr, si), (b, d), row in zip(state, [(b, d) for b in range(bsz) for d in range(2)], rows):
            ar, ai = coef[d]
            er, ei = planes(e_ref[0, pl.ds(row, 1), :], d)
            new.append((ar * sr - ai * si + er, ar * si + ai * sr + ei))
        return tuple(new)

    def ctx_step(s, state):
        rows = [ctx0 + b * n_ctx + (s if d == 0 else n_ctx - 1 - s) for b in range(bsz) for d in range(2)]
        return advance(state, rows)

    def lat_step(s, state):
        rows = [b * n_lat + (s if d == 0 else n_lat - 1 - s) for b in range(bsz) for d in range(2)]
        for (sr, si), (b, d), row in zip(state, [(b, d) for b in range(bsz) for d in range(2)], rows):
            o_ref[0, pl.ds(row, 1), (2 * d) * q:(2 * d + 1) * q] = sr
            o_ref[0, pl.ds(row, 1), (2 * d + 1) * q:(2 * d + 2) * q] = si
        return advance(state, rows)

    zero = jnp.zeros((1, q), F32)
    state = tuple((zero, zero) for _ in range(2 * bsz))
    state = lax.fori_loop(0, n_ctx, ctx_step, state)
    lax.fori_loop(0, n_lat, lat_step, state)


def _s5_bscan(e_rows, a_rows, bsz, n_ctx, n_lat):
    nb, rows, k = e_rows.shape
    return pl.pallas_call(
        functools.partial(_s5_bscan_kernel, bsz=bsz, n_ctx=n_ctx, n_lat=n_lat),
        out_shape=jax.ShapeDtypeStruct((nb, bsz * n_lat, k), F32),
        grid=(nb,),
        in_specs=[pl.BlockSpec((1, rows, k), lambda j: (j, 0, 0)),
                  pl.BlockSpec((1, 1, k), lambda j: (j, 0, 0))],
        out_specs=pl.BlockSpec((1, bsz * n_lat, k), lambda j: (j, 0, 0)),
        compiler_params=_cparams(("parallel",)),
        name="s5_scan",
    )(e_rows, a_rows)


def _s5_out_kernel(u_ref, s_ref, mc_ref, cc_ref, d_ref, rep_m_ref, mask_m_ref, rep_e_ref, mask_e_ref,
                   o_ref, wm_scr, wc_scr, u_scr):
    @pl.when(pl.program_id(1) == 0)
    def _():
        _expand_block_diag(mc_ref[0], rep_m_ref, mask_m_ref, wm_scr)
        _expand_block_diag(cc_ref[0], rep_e_ref, mask_e_ref, wc_scr)

    t_n = S5_CHUNK
    rows = s_ref.shape[1]
    _gather_chunk_rows(u_ref, rows, u_scr)
    u = u_scr[...]
    y = _dot(u.astype(BF16), wm_scr[...]) + _dot_nt(s_ref[0].astype(BF16), wc_scr[...])
    y = y + u * d_ref[0]
    for t in range(t_n):
        o_ref[0, pl.ds(t, rows, stride=t_n), :] = y[:, t * LANES:(t + 1) * LANES]


def _s5_outputs(u_blk, s_rows, m_comp, c_comp, d_rows, rep_m, mask_m, rep_e, mask_e):
    nb = u_blk.shape[0]
    rows, k = s_rows.shape[1:]
    cw = m_comp.shape[2]
    tr = rows // 2
    tok = tr * S5_CHUNK
    const = lambda a: pl.BlockSpec(a.shape, lambda j, i: (0, 0))
    return pl.pallas_call(
        _s5_out_kernel,
        out_shape=jax.ShapeDtypeStruct((nb, rows * S5_CHUNK, LANES), F32),
        grid=(nb, rows // tr),
        in_specs=[pl.BlockSpec((1, tok, LANES), lambda j, i: (j, i, 0)),
                  pl.BlockSpec((1, tr, k), lambda j, i: (j, i, 0)),
                  pl.BlockSpec((1, k, cw), lambda j, i: (j, 0, 0)),
                  pl.BlockSpec((1, k, cw), lambda j, i: (j, 0, 0)),
                  pl.BlockSpec((1, 1, k), lambda j, i: (j, 0, 0)),
                  const(rep_m), const(mask_m), const(rep_e), const(mask_e)],
        out_specs=pl.BlockSpec((1, tok, LANES), lambda j, i: (j, i, 0)),
        scratch_shapes=[pltpu.VMEM((k, k), BF16), pltpu.VMEM((k, k), BF16), pltpu.VMEM((tr, k), F32)],
        compiler_params=_cparams(("parallel", "arbitrary")),
        name="s5_outputs",
    )(u_blk, s_rows, m_comp, c_comp, d_rows, rep_m, mask_m, rep_e, mask_e)


def _s5_branch(u_blk, bsz, l_ctx, l_lat, a_re, a_im, log_dt, b_re, b_im, c_re, c_im, s5_d):
    nb, m_all, _ = u_blk.shape
    hg, t_n, p_n = S5_GROUP, S5_CHUNK, S5_STATE
    g_n = a_re.shape[1]
    gl_n = g_n // nb
    k = t_n * LANES
    n_lat = l_lat // t_n
    n_ctx = l_ctx // t_n
    e_c, c_c, m_c, a16 = _s5_params(a_re, a_im, log_dt, b_re, b_im, c_re, c_im, nb)
    cw = 4 * p_n
    e_comp, c_comp, m_comp = (a.reshape(nb, k, cw) for a in (e_c, c_c, m_c))

    row_gl = (jnp.arange(k) // hg) % gl_n
    col = jnp.arange(k)
    src = jnp.arange(cw)
    rep_e = ((src[:, None] // p_n == col[None, :] // (gl_n * p_n)) & (src[:, None] % p_n == col[None, :] % p_n)).astype(BF16)
    mask_e = (row_gl[:, None] == (jnp.arange(gl_n * p_n)[None, :] // p_n)).astype(BF16)
    rep_m = ((src[:, None] // hg == col[None, :] // (gl_n * hg)) & (src[:, None] % hg == col[None, :] % hg)).astype(BF16)
    mask_m = (row_gl[:, None] == (jnp.arange(gl_n * hg)[None, :] // hg)).astype(BF16)

    d_rows = jnp.tile(s5_d.reshape(nb, 1, gl_n * hg), (1, 1, t_n))
    a_rows = jnp.transpose(a16.reshape(2, 2, nb, gl_n * p_n), (2, 0, 1, 3)).reshape(nb, 1, 4 * gl_n * p_n)

    e_rows = _s5_chunk_inputs(u_blk, e_comp, rep_e, mask_e)
    s_rows = _s5_bscan(e_rows, a_rows, bsz, n_ctx, n_lat)
    return _s5_outputs(u_blk, s_rows, m_comp, c_comp, d_rows, rep_m, mask_m, rep_e, mask_e)


def _rw_prep_kernel(z_ref, zp_ref, zn_ref, mu_ref, w2_ref, a2_ref, g2_ref, w0_ref, a0_ref,
                    kk_w_ref, ka_ref, rk_ref, seg_ref, segt_ref,
                    r_ref, v_ref, kk_ref, g_ref, bonus_ref, lw_ref, kd_ref, be_ref,
                    *, tm, l_lat, rw):
    j = pl.program_id(1)
    z = z_ref[...].astype(F32)
    lat = j > 0
    tl = lax.broadcasted_iota(jnp.int32, (tm, 1), 0)
    tok = (j - 1) * tm + tl
    col = tl % GRID_W
    m_l = jnp.where(lat, col, tl) > 0
    m_r = jnp.where(lat, col - (GRID_W - 1), tl - (tm - 1)) < 0
    m_u = jnp.logical_and(lat, tok >= GRID_W)
    m_d = jnp.logical_and(lat, tok < l_lat - GRID_W)
    z_ext = jnp.concatenate([zp_ref[...], z_ref[...], zn_ref[...]], axis=0)
    rel = (lax.broadcasted_iota(jnp.int32, (tm, tm + 2 * GRID_W), 1) - GRID_W
           - lax.broadcasted_iota(jnp.int32, (tm, tm + 2 * GRID_W), 0))
    pick = (jnp.logical_and(rel == -1, m_l) | jnp.logical_and(rel == 1, m_r)
            | jnp.logical_and(rel == -GRID_W, m_u) | jnp.logical_and(rel == GRID_W, m_d))
    s = _dot(jnp.where(pick, 1.0, 0.0).astype(z_ext.dtype), z_ext)
    cnt = (m_l.astype(F32) + m_r.astype(F32)) + (m_u.astype(F32) + m_d.astype(F32))
    zs = z + (s * (1.0 / cnt) - z) * mu_ref[...]

    r = zs[:, 0:rw]
    k = zs[:, rw:2 * rw]
    v = zs[:, 2 * rw:3 * rw]
    o = 3 * rw
    wd = zs[:, o:o + LANES]
    ad = zs[:, o + LANES:o + 2 * LANES]
    gd = zs[:, o + 2 * LANES:o + 3 * LANES]

    seg = seg_ref[...]
    segt = segt_ref[...]

    def head_sum(t):
        return _dot_hilo(_dot_hilo(t, seg), segt)

    g_ref[0] = _dot(_sigmoid(gd), g2_ref[...], "bf16").astype(g_ref.dtype)
    kk = k * kk_w_ref[...]
    kk = kk * lax.rsqrt(head_sum(kk * kk) + 1e-12)
    wl = w0_ref[...] + _dot(jnp.tanh(wd), w2_ref[...], "bf16")
    al = a0_ref[...] + _dot(ad, a2_ref[...], "bf16")
    r_ref[0] = r.astype(r_ref.dtype)
    v_ref[0] = v.astype(v_ref.dtype)
    kk_ref[0] = kk.astype(kk_ref.dtype)
    k_sum = jnp.zeros_like(r)
    for d in range(2):
        a = _sigmoid(al[:, d * rw:(d + 1) * rw])
        k_d = k * (1.0 + (a - 1.0) * ka_ref[...])
        k_sum = k_sum + k_d
        lw_ref[d, 0] = -math.exp(-0.5) * _sigmoid(wl[:, d * rw:(d + 1) * rw])
        kd_ref[d, 0] = k_d.astype(kd_ref.dtype)
        be_ref[d, 0] = (kk * a).astype(be_ref.dtype)
    bonus_ref[0] = (head_sum(r * rk_ref[...] * k_sum) * v).astype(bonus_ref.dtype)


def _rw_prep(z_rw, bsz, l_ctx, l_lat, mu, w2bd, a2bd, g2, w0cat, a0cat, k_k, k_a, r_k_flat, seg, segt):
    cols = z_rw.shape[1]
    tm = l_ctx
    rw = g2.shape[1]
    l_all = l_ctx + l_lat
    nblk = l_all // tm
    lat_blk = l_lat // tm
    hb = tm // GRID_W
    lat_hblk = l_lat // GRID_W

    def main_blk(b, j):
        return jnp.where(j == 0, bsz * lat_blk + b, b * lat_blk + j - 1)

    def prev_halo(b, j):
        return b * lat_hblk + jnp.maximum((j - 1) * hb - 1, 0)

    def next_halo(b, j):
        return b * lat_hblk + jnp.minimum(jnp.maximum(j, 1) * hb, lat_hblk - 1)

    full = lambda shape: pl.BlockSpec(shape, lambda b, j: (0,) * len(shape))
    shared = jax.ShapeDtypeStruct((bsz, l_all, rw), BF16)
    lat_only = jax.ShapeDtypeStruct((bsz, l_lat, rw), BF16)
    per_dir = jax.ShapeDtypeStruct((2, bsz, l_all, rw), BF16)
    per_dir_f32 = jax.ShapeDtypeStruct((2, bsz, l_all, rw), F32)
    o_shared = pl.BlockSpec((1, tm, rw), lambda b, j: (b, j, 0))
    o_lat = pl.BlockSpec((1, tm, rw), lambda b, j: (b, jnp.maximum(j - 1, 0), 0))
    o_dir = pl.BlockSpec((2, 1, tm, rw), lambda b, j: (0, b, j, 0))
    return pl.pallas_call(
        functools.partial(_rw_prep_kernel, tm=tm, l_lat=l_lat, rw=rw),
        out_shape=(shared,) * 3 + (lat_only,) * 2 + (per_dir_f32, per_dir, per_dir),
        grid=(bsz, nblk),
        in_specs=[pl.BlockSpec((tm, cols), lambda b, j: (main_blk(b, j), 0)),
                  pl.BlockSpec((GRID_W, cols), lambda b, j: (prev_halo(b, j), 0)),
                  pl.BlockSpec((GRID_W, cols), lambda b, j: (next_halo(b, j), 0)),
                  full((1, cols)), full(w2bd.shape), full(a2bd.shape), full(g2.shape),
                  full((1, 2 * rw)), full((1, 2 * rw)), full((1, rw)), full((1, rw)), full((1, rw)),
                  full(seg.shape), full(segt.shape)],
        out_specs=(o_shared,) * 3 + (o_lat,) * 2 + (o_dir,) * 3,
        compiler_params=_cparams(("parallel", "arbitrary")),
        name="rwkv_prep",
    )(z_rw, z_rw, z_rw, mu.reshape(1, cols), w2bd, a2bd, g2, w0cat, a0cat,
      k_k.reshape(1, rw), k_a.reshape(1, rw), r_k_flat.reshape(1, rw), seg, segt)


def _stack_heads(x, head0):
    return jnp.concatenate([jnp.where(head0, x, 0.0), jnp.where(head0, 0.0, x)], axis=0)


def _rw_step_kernel(rf_ref, rb_ref, vf_ref, vb_ref, kkf_ref, kkb_ref, lwf_ref, lwb_ref, kdf_ref, kdb_ref,
                    bef_ref, beb_ref, yf_ref, yb_ref,
                    z_scr, gc_s, lhs_s, kb_s, kht_s, bht_s, vs_s, rt_s, aab_s, akr_s, arb_s, t_s, x_s, xin_s,
                    yc_s, wu_s, qm_s, yn_s, *, n_pairs, bsz):
    c_n = RW_CHUNK
    n2 = 2 * c_n

    lanes_of = lambda p: slice(p * LANES, (p + 1) * LANES)
    unit = lambda d, b, p: (d * bsz + b) * n_pairs + p
    units = [(d, b, p) for d in range(2) for b in range(bsz) for p in range(n_pairs)]

    @pl.when(pl.program_id(0) == 0)
    def _():
        z_scr[...] = jnp.zeros_like(z_scr)
        qm_s[...] = jnp.zeros_like(qm_s)
        yn_s[...] = jnp.zeros_like(yn_s)

    for d, b, p in units:
        u = unit(d, b, p)
        y_ref = yf_ref if d == 0 else yb_ref
        yz = _dot(qm_s[u], z_scr[u].astype(BF16)) + yn_s[u]
        y_ref[b, :, lanes_of(p)] = yz[:c_n] + yz[c_n:n2]
        z_scr[u] = yz[n2:]

    ri = lax.broadcasted_iota(jnp.int32, (c_n, c_n), 0)
    ci = lax.broadcasted_iota(jnp.int32, (c_n, c_n), 1)
    r2 = lax.broadcasted_iota(jnp.int32, (n2, n2), 0)
    c2 = lax.broadcasted_iota(jnp.int32, (n2, n2), 1)
    t2 = r2 % c_n
    i2 = c2 % c_n
    same_head = (r2 // c_n) == (c2 // c_n)
    diag = r2 == c2
    eye = diag.astype(F32)
    head0 = lax.broadcasted_iota(jnp.int32, (1, LANES), 1) < RW_HEAD
    tri = [(ri >= ci).astype(F32), (ri <= ci).astype(F32)]
    before = [jnp.logical_and(same_head, i2 < t2), jnp.logical_and(same_head, i2 > t2)]
    upto = [jnp.logical_or(m, diag) for m in before]

    def blk(s):
        return (r2 // s) == (c2 // s)

    srcs = [(rf_ref, vf_ref, kkf_ref, lwf_ref, kdf_ref, bef_ref), (rb_ref, vb_ref, kkb_ref, lwb_ref, kdb_ref, beb_ref)]

    for d in range(2):
        r_ref, v_ref, kk_ref, lw_ref, kd_ref, be_ref = srcs[d]
        for b in range(bsz):
            lw = lw_ref[0, b]
            cum = _dot(tri[d], lw, HIGHEST)
            tot = jnp.sum(lw, axis=0, keepdims=True)
            g_inv = jnp.exp(-cum)
            g_hat = jnp.exp(tot - cum)
            kd = kd_ref[0, b].astype(F32)
            be = be_ref[0, b].astype(F32)
            at = kk_ref[b].astype(F32) * jnp.exp(cum - lw)
            rt = r_ref[b].astype(F32) * jnp.exp(cum)
            kt = kd * g_inv
            bt = be * g_inv
            kh = kd * g_hat
            bh = be * g_hat
            gc_s[d * bsz + b] = jnp.exp(tot)
            v = v_ref[b].astype(F32)
            for p in range(n_pairs):
                u = unit(d, b, p)
                st = lambda x: _stack_heads(x[:, lanes_of(p)], head0)
                rt_p = st(rt)
                at_p = st(at).astype(BF16)
                lhs_s[u, :n2] = at_p
                lhs_s[u, n2:] = rt_p.astype(BF16)
                xin_s[u, :, :LANES] = at_p
                rt_s[u] = rt_p
                kb_s[u, :n2] = st(kt).astype(BF16)
                kb_s[u, n2:] = st(bt).astype(BF16)
                kht_s[u] = st(kh).T.astype(BF16)
                bht_s[u] = st(bh).T.astype(BF16)
                vs_s[u] = st(v).astype(BF16)

    for d, b, p in units:
        u = unit(d, b, p)
        g = _dot_nt(lhs_s[u], kb_s[u])
        a_ab = jnp.where(before[d], g[:n2, n2:], 0.0)
        akr_s[u, :n2] = jnp.where(before[d], g[:n2, :n2], 0.0).astype(BF16)
        akr_s[u, n2:] = jnp.where(upto[d], g[n2:, :n2], 0.0).astype(BF16)
        arb_s[u] = jnp.where(upto[d], g[n2:, n2:], 0.0).astype(BF16)
        aab_s[u] = a_ab
        t_s[u] = eye - jnp.where(blk(2), a_ab, 0.0)

    n_units = len(units)
    s = 2
    while s < c_n:
        off = jnp.logical_and(blk(2 * s), jnp.logical_not(blk(s)))
        for u in range(n_units):
            x_s[u] = _dot(t_s[u].astype(BF16), jnp.where(off, aab_s[u], 0.0).astype(BF16)).astype(BF16)
        for u in range(n_units):
            t = t_s[u]
            t_s[u] = t - _dot(x_s[u], t.astype(BF16))
        s *= 2

    for u in range(n_units):
        av = _dot(akr_s[u], vs_s[u])
        xin_s[u, :, LANES:] = av[:n2].astype(BF16)
        yc_s[u] = av[n2:]

    for u in range(n_units):
        wu_s[u] = _dot(t_s[u].astype(BF16), xin_s[u]).astype(BF16)
    for d, b, p in units:
        u = unit(d, b, p)
        wu = wu_s[u]
        q = _dot(arb_s[u], wu)
        bwu = _dot(bht_s[u], wu)
        qm_s[u, :n2] = (rt_s[u] - q[:, :LANES]).astype(BF16)
        qm_s[u, n2:] = (eye * gc_s[d * bsz + b][:, lanes_of(p)] - bwu[:, :LANES]).astype(BF16)
        yn_s[u, :n2] = yc_s[u] - q[:, LANES:]
        yn_s[u, n2:] = _dot(kht_s[u], vs_s[u]) - bwu[:, LANES:]


def _rw_scan(r, v, kk, lw, kd, be, l_ctx):
    bsz, l_all, rw = r.shape
    c_n = RW_CHUNK
    n_all = l_all // c_n
    n_ctx = l_ctx // c_n
    n_lat = n_all - n_ctx
    n_pairs = rw // LANES
    n_units = 2 * bsz * n_pairs
    n2 = 2 * c_n
    vm = lambda nr, cols, dt: pltpu.VMEM((n_units, nr, cols), dt)

    clamp = lambda s: jnp.minimum(s, n_all - 1)
    chunk_f = lambda s: clamp(s)
    chunk_b = lambda s: jnp.where(clamp(s) < n_ctx, n_ctx - 1 - clamp(s), n_all + n_ctx - 1 - clamp(s))
    prev = lambda s: jnp.maximum(s - 1, 0)
    out_f = lambda s: jnp.maximum(prev(s) - n_ctx, 0)
    out_b = lambda s: jnp.where(prev(s) < n_ctx, n_lat - 1, chunk_b(prev(s)) - n_ctx)

    sh_f = pl.BlockSpec((bsz, c_n, rw), lambda s: (0, chunk_f(s), 0))
    sh_b = pl.BlockSpec((bsz, c_n, rw), lambda s: (0, chunk_b(s), 0))
    pd_f = pl.BlockSpec((1, bsz, c_n, rw), lambda s: (0, 0, chunk_f(s), 0))
    pd_b = pl.BlockSpec((1, bsz, c_n, rw), lambda s: (1, 0, chunk_b(s), 0))
    y_shape = jax.ShapeDtypeStruct((bsz, n_lat * c_n, rw), F32)
    return pl.pallas_call(
        functools.partial(_rw_step_kernel, n_pairs=n_pairs, bsz=bsz),
        out_shape=(y_shape, y_shape),
        grid=(n_all + 1,),
        in_specs=[sh_f, sh_b, sh_f, sh_b, sh_f, sh_b, pd_f, pd_b, pd_f, pd_b, pd_f, pd_b],
        out_specs=(pl.BlockSpec((bsz, c_n, rw), lambda s: (0, out_f(s), 0)),
                   pl.BlockSpec((bsz, c_n, rw), lambda s: (0, out_b(s), 0))),
        scratch_shapes=[vm(n2, n2, F32),
                        pltpu.VMEM((2 * bsz, 1, rw), F32),
                        vm(2 * n2, n2, BF16),
                        vm(2 * n2, n2, BF16),
                        vm(n2, n2, BF16),
                        vm(n2, n2, BF16),
                        vm(n2, n2, BF16),
                        vm(n2, n2, F32),
                        vm(n2, n2, F32),
                        vm(2 * n2, n2, BF16),
                        vm(n2, n2, BF16),
                        vm(n2, n2, F32),
                        vm(n2, n2, BF16),
                        vm(n2, 2 * n2, BF16),
                        vm(n2, n2, F32),
                        vm(n2, 2 * n2, BF16),
                        vm(2 * n2, n2, BF16),
                        vm(2 * n2, n2, F32)],
        compiler_params=_cparams(("arbitrary",)),
        name="rwkv_scan",
    )(r, r, v, v, kk, kk, lw, lw, kd, kd, be, be)


def _s5_glu_kernel(y_ref, wa_ref, wb_ref, o_ref, h_scr):
    @pl.when(pl.program_id(1) == 0)
    def _():
        for jb in range(y_ref.shape[0]):
            h_scr[:, jb * LANES:(jb + 1) * LANES] = _gelu_tanh(y_ref[jb]).astype(BF16)

    h = h_scr[...]
    o_ref[...] = (_dot(h, wa_ref[...]) * _sigmoid(_dot(h, wb_ref[...]))).astype(o_ref.dtype)


def _s5_glu(y_blk, w, *, tm, tn):
    nb, m, _ = y_blk.shape
    k = nb * LANES
    n = w.shape[1] // 2
    nj = n // tn
    return pl.pallas_call(
        _s5_glu_kernel,
        out_shape=jax.ShapeDtypeStruct((m, n), BF16),
        grid=(m // tm, nj),
        in_specs=[pl.BlockSpec((nb, tm, LANES), lambda i, j: (0, i, 0)),
                  pl.BlockSpec((k, tn), lambda i, j: (0, j)),
                  pl.BlockSpec((k, tn), lambda i, j: (0, j + nj))],
        out_specs=pl.BlockSpec((tm, tn), lambda i, j: (i, j)),
        scratch_shapes=[pltpu.VMEM((tm, k), BF16)],
        compiler_params=_cparams(("parallel", "arbitrary")),
        name="s5_glu",
    )(y_blk, w, w)


def _rw_merge_kernel(yf_ref, yb_ref, bonus_ref, g_ref, lnw_ref, lnb_ref, seg_ref, segt_ref,
                     ga_ref, gb_ref, s5_ref, w_ref, o_ref, h_scr):
    @pl.when(pl.program_id(1) == 0)
    def _():
        seg = seg_ref[...]
        segt = segt_ref[...]
        inv_n = 1.0 / RW_HEAD

        def head_mean(t):
            return _dot_hilo(_dot_hilo(t, seg), segt) * inv_n

        y = yf_ref[...] + yb_ref[...]
        dy = y - head_mean(y)
        var = head_mean(dy * dy)
        y = dy * lax.rsqrt(var + GN_EPS) * lnw_ref[...] + lnb_ref[...] + bonus_ref[...].astype(F32)
        h_scr[...] = (y * g_ref[...].astype(F32)).astype(BF16)

    rw_out = _dot(h_scr[...], w_ref[...])
    merged = ga_ref[...].astype(F32) * s5_ref[...].astype(F32) + gb_ref[...].astype(F32) * rw_out
    o_ref[...] = merged.astype(o_ref.dtype)


def _rw_merge(y_f, y_b, bonus, g, ln_w, ln_b, seg, segt, gates, s5_out, w_proj, *, tm, tn):
    m, rw = y_f.shape
    n = w_proj.shape[1]
    nj = n // tn
    full = lambda shape: pl.BlockSpec(shape, lambda i, j: (0,) * len(shape))
    return pl.pallas_call(
        _rw_merge_kernel,
        out_shape=jax.ShapeDtypeStruct((m, n), BF16),
        grid=(m // tm, nj),
        in_specs=[pl.BlockSpec((tm, rw), lambda i, j: (i, 0)),
                  pl.BlockSpec((tm, rw), lambda i, j: (i, 0)),
                  pl.BlockSpec((tm, rw), lambda i, j: (i, 0)),
                  pl.BlockSpec((tm, rw), lambda i, j: (i, 0)),
                  full((1, rw)), full((1, rw)), full(seg.shape), full(segt.shape),
                  pl.BlockSpec((tm, tn), lambda i, j: (i, j)),
                  pl.BlockSpec((tm, tn), lambda i, j: (i, j + nj)),
                  pl.BlockSpec((tm, tn), lambda i, j: (i, j)),
                  pl.BlockSpec((rw, tn), lambda i, j: (0, j))],
        out_specs=pl.BlockSpec((tm, tn), lambda i, j: (i, j)),
        scratch_shapes=[pltpu.VMEM((tm, rw), BF16)],
        compiler_params=_cparams(("parallel", "arbitrary")),
        name="rwkv_merge",
    )(y_f, y_b, bonus, g, ln_w.reshape(1, rw), ln_b.reshape(1, rw), seg, segt,
      gates, gates, s5_out, w_proj)


def _resid_ln_kernel(a_ref, w_ref, x_ref, g_ref, nw_ref, sh_ref, sc_ref, h_ref, hn_ref):
    h = x_ref[...] + g_ref[0] * _dot(a_ref[...], w_ref[...])
    h_ref[...] = h
    hn_ref[...] = _lnmod_rows(h, nw_ref[...], sh_ref[0], sc_ref[0]).astype(hn_ref.dtype)


def _resid_matmul_ln(a, w, x2, g_tab, nw, sh_tab, sc_tab, mod_row_of_block, *, tm):
    m, k = a.shape
    n = w.shape[1]
    mod_map = lambda i: (mod_row_of_block(i), 0, 0)
    return pl.pallas_call(
        _resid_ln_kernel,
        out_shape=(jax.ShapeDtypeStruct((m, n), F32), jax.ShapeDtypeStruct((m, n), BF16)),
        grid=(m // tm,),
        in_specs=[pl.BlockSpec((tm, k), lambda i: (i, 0)),
                  pl.BlockSpec((k, n), lambda i: (0, 0)),
                  pl.BlockSpec((tm, n), lambda i: (i, 0)),
                  pl.BlockSpec((1, 1, n), mod_map),
                  pl.BlockSpec((1, n), lambda i: (0, 0)),
                  pl.BlockSpec((1, 1, n), mod_map),
                  pl.BlockSpec((1, 1, n), mod_map)],
        out_specs=(pl.BlockSpec((tm, n), lambda i: (i, 0)), pl.BlockSpec((tm, n), lambda i: (i, 0))),
        compiler_params=_cparams(("parallel",)),
        name="out_proj",
    )(a, w, x2, g_tab, nw.reshape(1, n), sh_tab, sc_tab)


def _wres_swiglu_kernel(a_ref, w1_ref, w3_ref, o_ref, w1_scr, w3_scr):
    @pl.when(pl.program_id(1) == 0)
    def _():
        w1_scr[...] = w1_ref[...].astype(BF16)
        w3_scr[...] = w3_ref[...].astype(BF16)

    a = a_ref[...]
    o_ref[...] = (_silu(_dot(a, w1_scr[...])) * _dot(a, w3_scr[...])).astype(o_ref.dtype)


def _wres_swiglu(a, w13, d_ff, *, tm, tn):
    m, k = a.shape
    nj = d_ff // tn
    return pl.pallas_call(
        _wres_swiglu_kernel,
        out_shape=jax.ShapeDtypeStruct((m, d_ff), BF16),
        grid=(nj, m // tm),
        in_specs=[pl.BlockSpec((tm, k), lambda j, i: (i, 0)),
                  pl.BlockSpec((k, tn), lambda j, i: (0, j)),
                  pl.BlockSpec((k, tn), lambda j, i: (0, j + nj))],
        out_specs=pl.BlockSpec((tm, tn), lambda j, i: (i, j)),
        scratch_shapes=[pltpu.VMEM((k, tn), BF16), pltpu.VMEM((k, tn), BF16)],
        compiler_params=_cparams(("arbitrary", "arbitrary")),
        name="ffn_up",
    )(a, w13, w13)


def _ffn_down_kernel(a_ref, w_ref, x_ref, g_ref, nf_ref, o_ref):
    kk = pl.program_id(1)
    part = _dot(a_ref[...], w_ref[...])

    @pl.when(kk == 0)
    def _():
        o_ref[...] = part

    @pl.when(jnp.logical_and(kk > 0, kk < pl.num_programs(1) - 1))
    def _():
        o_ref[...] += part

    @pl.when(kk == pl.num_programs(1) - 1)
    def _():
        h = x_ref[...] + g_ref[0] * (o_ref[...] + part)
        ms = jnp.mean(h * h, axis=-1, keepdims=True)
        o_ref[...] = h * lax.rsqrt(ms + NORM_EPS) * nf_ref[...]


def _ffn_down(a, w, x2, g_tab, mod_row_of_block, norm_f, *, tm, tk):
    m, k = a.shape
    n = w.shape[1]
    return pl.pallas_call(
        _ffn_down_kernel,
        out_shape=jax.ShapeDtypeStruct((m, n), F32),
        grid=(m // tm, k // tk),
        in_specs=[pl.BlockSpec((tm, tk), lambda i, kk: (i, kk)),
                  pl.BlockSpec((tk, n), lambda i, kk: (kk, 0)),
                  pl.BlockSpec((tm, n), lambda i, kk: (i, 0)),
                  pl.BlockSpec((1, 1, n), lambda i, kk: (mod_row_of_block(i), 0, 0)),
                  pl.BlockSpec((1, n), lambda i, kk: (0, 0))],
        out_specs=pl.BlockSpec((tm, n), lambda i, kk: (i, 0)),
        compiler_params=_cparams(("parallel", "arbitrary"), vmem=VMEM_LIMIT + 8 * 1024 * 1024),
        name="ffn_down",
    )(a, w, x2, g_tab, norm_f.reshape(1, n))


def kernel(x, c, ctx, c_ctx, ada_w, ada_b, norm1_w, w_in, rw_mu, s5_a_re, s5_a_im, s5_log_dt, s5_b_re, s5_b_im, s5_c_re, s5_c_im, s5_d, s5_glu_w, rw_w0, rw_w2, rw_a0, rw_a2, rw_g2, rw_k_k, rw_k_a, rw_r_k, rw_ln_w, rw_ln_b, rw_proj, w_o, norm2_w, ffn_w13, ffn_w2, norm_f):
    assert ada_w.shape[0] == 1, "single-layer block"
    bsz, l_lat, d = x.shape
    l_ctx = ctx.shape[1]
    l_all = l_ctx + l_lat
    s5w = s5_d.shape[1] * s5_d.shape[2]
    rw = rw_g2.shape[2]
    shift_cols = rw_mu.shape[1]
    d_ff = ffn_w2.shape[1]

    c_rows = jnp.concatenate([c, c_ctx[None], jnp.zeros((8 - bsz - 1, d), F32)], axis=0)
    mod = _modulation(c_rows, ada_w[0], ada_b[0])
    tab = lambda k: mod[:, k * d:(k + 1) * d].reshape(8, 1, d)
    sh1, sc1, g1, sh2, sc2, g2 = (tab(k) for k in range(N_MOD))
    ctx_row = bsz

    tm_in = 512
    m_lat = bsz * l_lat
    x2 = x.reshape(m_lat, d)
    n_mix = s5w + shift_cols
    m_all = m_lat + bsz * l_ctx
    lat_row = lambda t: (lambda i: i // (l_lat // t))

    def mix_mod_row(i):
        return jnp.where(i >= m_lat // tm_in, ctx_row, i // (l_lat // tm_in))

    h_all = _lnmod(x2, ctx.reshape(bsz * l_ctx, d), norm1_w[0], sh1, sc1, mix_mod_row, tm=tm_in)
    tm_all = m_all // 8
    u_blk = _wres_matmul(h_all, w_in[0], 0, s5w, rows=m_all, tm=tm_all, tn=s5w,
                         out_dtype=F32, epilogue="lane_blocks", name="in_proj_s5")
    z_rw = _wres_matmul(h_all, w_in[0], s5w, shift_cols, rows=m_all, tm=tm_all, tn=shift_cols // 3,
                        out_dtype=BF16, name="in_proj_rw")
    tm = 1024
    gates = _wres_matmul(h_all, w_in[0], n_mix, w_in.shape[2] - n_mix, rows=m_lat, tm=tm, tn=1024,
                         out_dtype=BF16, epilogue="sigmoid", name="in_proj_gates")

    y_blk = _s5_branch(u_blk, bsz, l_ctx, l_lat, s5_a_re[0], s5_a_im[0], s5_log_dt[0],
                               s5_b_re[0], s5_b_im[0], s5_c_re[0], s5_c_im[0], s5_d[0])
    s5_out = _s5_glu(y_blk, s5_glu_w[0].astype(BF16), tm=tm, tn=1024)

    lora = rw_w2.shape[2]
    zl = jnp.zeros((lora, rw), F32)
    w2bd = jnp.concatenate([jnp.concatenate([rw_w2[0, 0], zl], axis=1),
                            jnp.concatenate([zl, rw_w2[0, 1]], axis=1)], axis=0)
    a2bd = jnp.concatenate([jnp.concatenate([rw_a2[0, 0], zl], axis=1),
                            jnp.concatenate([zl, rw_a2[0, 1]], axis=1)], axis=0)
    head_of = jnp.arange(rw) // RW_HEAD
    seg = (head_of[:, None] == jnp.arange(LANES)[None, :]).astype(BF16)
    segt = seg.T
    r, v, kk, g, bonus, lw, kd, be = _rw_prep(
        z_rw, bsz, l_ctx, l_lat, rw_mu[0], w2bd, a2bd, rw_g2[0], rw_w0[0].reshape(1, 2 * rw),
        rw_a0[0].reshape(1, 2 * rw), rw_k_k[0], rw_k_a[0], rw_r_k[0].reshape(rw), seg, segt)
    y_f, y_b = _rw_scan(r, v, kk, lw, kd, be, l_ctx)

    merged = _rw_merge(y_f.reshape(m_lat, rw), y_b.reshape(m_lat, rw), bonus.reshape(m_lat, rw), g.reshape(m_lat, rw),
                       rw_ln_w[0], rw_ln_b[0], seg, segt, gates, s5_out, rw_proj[0].astype(BF16),
                       tm=512, tn=2048)
    tm_o = 512
    h1, h1n = _resid_matmul_ln(merged, w_o[0].astype(BF16), x2, g1, norm2_w[0], sh2, sc2, lat_row(tm_o), tm=tm_o)

    act = _wres_swiglu(h1n, ffn_w13[0], d_ff, tm=tm, tn=512)
    tm_dn = 512
    out = _ffn_down(act, ffn_w2[0].astype(BF16), h1, g2, lat_row(tm_dn), norm_f, tm=tm_dn, tk=d_ff // 2)
    return out.reshape(bsz, l_lat, d)
```

```python
import functools
import math

import jax
import jax.numpy as jnp
from jax import lax
from jax.experimental import pallas as pl
from jax.experimental.pallas import tpu as pltpu

F32 = jnp.float32
BF16 = jnp.bfloat16
HIGHEST = lax.Precision.HIGHEST

N_MOD = 6
NORM_EPS = 1e-6
GN_EPS = 64e-5
GRID_W = 64
S5_GROUP = 16
S5_STATE = 64
S5_CHUNK = 16
RW_HEAD = 64
RW_CHUNK = 64
LANES = 128
VMEM_LIMIT = 48 * 1024 * 1024


def _cparams(sem, vmem=VMEM_LIMIT):
    return pltpu.CompilerParams(dimension_semantics=sem, vmem_limit_bytes=vmem)


def _operands(a, b, precision):
    if precision == "bf16":
        return a.astype(BF16), b.astype(BF16), None
    return a, b, precision


def _dot(a, b, precision=None):
    a, b, precision = _operands(a, b, precision)
    return jnp.dot(a, b, preferred_element_type=F32, precision=precision)


def _dot_nt(a, b, precision=None):
    a, b, precision = _operands(a, b, precision)
    return lax.dot_general(a, b, (((1,), (1,)), ((), ())), preferred_element_type=F32, precision=precision)


def _dot_hilo(a, ind):
    hi = a.astype(BF16)
    lo = (a - hi.astype(F32)).astype(BF16)
    return _dot(hi, ind) + _dot(lo, ind)


def _sigmoid(x):
    return 1.0 / (1.0 + jnp.exp(-x))


def _silu(x):
    return x * _sigmoid(x)


def _gelu_tanh(x):
    c = math.sqrt(2.0 / math.pi)
    return 0.5 * x * (1.0 + jnp.tanh(c * (x + 0.044715 * (x * x * x))))


def _mod_kernel(c_ref, w_ref, b_ref, o_ref):
    o_ref[...] = _dot_hilo(_silu(c_ref[...]), w_ref[...].astype(BF16)) + b_ref[...]


def _modulation(c_rows, ada_w, ada_b):
    m, d = c_rows.shape
    n = ada_w.shape[1]
    tn = 1024
    return pl.pallas_call(
        _mod_kernel,
        out_shape=jax.ShapeDtypeStruct((m, n), F32),
        grid=(n // tn,),
        in_specs=[pl.BlockSpec((m, d), lambda j: (0, 0)),
                  pl.BlockSpec((d, tn), lambda j: (0, j)),
                  pl.BlockSpec((1, tn), lambda j: (0, j))],
        out_specs=pl.BlockSpec((m, tn), lambda j: (0, j)),
        compiler_params=_cparams(("arbitrary",)),
        name="modulation",
    )(c_rows, ada_w, ada_b.reshape(1, n))


def _lnmod_rows(x, nw, sh, sc):
    ms = jnp.mean(x * x, axis=-1, keepdims=True)
    y = x * lax.rsqrt(ms + NORM_EPS) * nw
    return y * (1.0 + sc) + sh


def _lnmod_proj_kernel(x_ref, c_ref, nw_ref, sh_ref, sc_ref, w_ref, h_ref, u_ref, w_scr, *, n_lat_blocks):
    @pl.when(pl.program_id(0) == 0)
    def _():
        w_scr[...] = w_ref[...].astype(BF16)

    rows = jnp.where(pl.program_id(0) < n_lat_blocks, x_ref[...], c_ref[...])
    h = _lnmod_rows(rows, nw_ref[...], sh_ref[0], sc_ref[0]).astype(BF16)
    h_ref[...] = h
    z = _dot(h, w_scr[...])
    for jb in range(u_ref.shape[0]):
        u_ref[jb] = z[:, jb * LANES:(jb + 1) * LANES]


def _lnmod_proj(x2, c2, nw, sh_tab, sc_tab, mod_row_of_block, w, n, *, tm):
    m, d = x2.shape
    mc = c2.shape[0]
    nx, nc = m // tm, mc // tm
    mod_map = lambda i: (mod_row_of_block(i), 0, 0)
    return pl.pallas_call(
        functools.partial(_lnmod_proj_kernel, n_lat_blocks=nx),
        out_shape=(jax.ShapeDtypeStruct((m + mc, d), BF16),
                   jax.ShapeDtypeStruct((n // LANES, m + mc, LANES), F32)),
        grid=(nx + nc,),
        in_specs=[pl.BlockSpec((tm, d), lambda i: (jnp.minimum(i, nx - 1), 0)),
                  pl.BlockSpec((tm, d), lambda i: (jnp.maximum(i - nx, 0), 0)),
                  pl.BlockSpec((1, d), lambda i: (0, 0)),
                  pl.BlockSpec((1, 1, d), mod_map),
                  pl.BlockSpec((1, 1, d), mod_map),
                  pl.BlockSpec((d, n), lambda i: (0, 0))],
        out_specs=(pl.BlockSpec((tm, d), lambda i: (i, 0)),
                   pl.BlockSpec((n // LANES, tm, LANES), lambda i: (0, i, 0))),
        scratch_shapes=[pltpu.VMEM((d, n), BF16)],
        compiler_params=_cparams(("arbitrary",)),
        name="lnmod_in_proj_s5",
    )(x2, c2, nw.reshape(1, d), sh_tab, sc_tab, w)


def _wres_mm_kernel(a_ref, w_ref, o_ref, w_scr, *, epilogue):
    @pl.when(pl.program_id(1) == 0)
    def _():
        w_scr[...] = w_ref[...].astype(BF16)

    z = _dot(a_ref[...], w_scr[...])
    if epilogue == "sigmoid":
        z = _sigmoid(z)
    o_ref[...] = z.astype(o_ref.dtype)


def _wres_matmul(a, w, col0, n, *, rows, tm, tn, out_dtype, epilogue=None, name):
    k = a.shape[1]
    assert col0 % LANES == 0 and n % tn == 0 and rows % tm == 0
    return pl.pallas_call(
        functools.partial(_wres_mm_kernel, epilogue=epilogue),
        out_shape=jax.ShapeDtypeStruct((rows, n), out_dtype),
        grid=(n // tn, rows // tm),
        in_specs=[pl.BlockSpec((tm, k), lambda j, i: (i, 0)),
                  pl.BlockSpec((pl.Element(k), pl.Element(tn)), lambda j, i: (0, pl.multiple_of(col0 + j * tn, LANES)))],
        out_specs=pl.BlockSpec((tm, tn), lambda j, i: (i, j)),
        scratch_shapes=[pltpu.VMEM((k, tn), BF16)],
        compiler_params=_cparams(("arbitrary", "arbitrary")),
        name=name,
    )(a, w)


def _s5_param_kernel(are_ref, aim_ref, ldt_ref, bre_ref, bim_ref, cre_ref, cim_ref,
                      e_ref, c_ref, m_ref, a16_ref):
    t_n, hg, p_n = S5_CHUNK, S5_GROUP, S5_STATE
    gl_n = are_ref.shape[1]
    tau = lax.broadcasted_iota(jnp.int32, (t_n, gl_n, p_n), 0).astype(F32)
    taps = {}
    for d in range(2):
        a_re = are_ref[d]
        a_im = aim_ref[d]
        dt = jnp.exp(ldt_ref[d])
        lam = a_re * dt
        th = a_im * dt
        er = jnp.exp(lam)
        ab_re = er * jnp.cos(th)
        ab_im = er * jnp.sin(th)
        den = a_re * a_re + a_im * a_im
        x_re = ab_re - 1.0
        co_re = (x_re * a_re + ab_im * a_im) / den
        co_im = (ab_im * a_re - x_re * a_im) / den

        def power(tv):
            mag = jnp.exp(tv * lam)
            return mag * jnp.cos(tv * th), mag * jnp.sin(tv * th)

        pw_re, pw_im = power(tau if d == 0 else (t_n - 1.0 - tau))
        pe_re, pe_im = power((t_n - 1.0 - tau) if d == 0 else tau)
        pc_re, pc_im = power((tau + 1.0) if d == 0 else (t_n - tau))
        mag16 = jnp.exp(float(t_n) * lam)
        a16_ref[d, 0] = mag16 * jnp.cos(float(t_n) * th)
        a16_ref[d, 1] = mag16 * jnp.sin(float(t_n) * th)

        lo, hi = 2 * d * p_n, (2 * d + 1) * p_n
        for gl in range(gl_n):
            of = lambda x: x[:, gl:gl + 1, :]
            bt_re = bre_ref[d, gl].T
            bt_im = bim_ref[d, gl].T
            bb_re = co_re[gl:gl + 1] * bt_re - co_im[gl:gl + 1] * bt_im
            bb_im = co_re[gl:gl + 1] * bt_im + co_im[gl:gl + 1] * bt_re
            c_re = cre_ref[d, gl]
            c_im = cim_ref[d, gl]
            cp_re = (c_re[None] * of(pw_re) - c_im[None] * of(pw_im)).reshape(t_n * hg, p_n)
            cp_im = (c_re[None] * of(pw_im) + c_im[None] * of(pw_re)).reshape(t_n * hg, p_n)
            taps[d, gl] = _dot_nt(bb_re, cp_re, HIGHEST) - _dot_nt(bb_im, cp_im, HIGHEST)
            e_ref[0, :, gl, :, lo:hi] = of(pe_re) * bb_re[None] - of(pe_im) * bb_im[None]
            e_ref[0, :, gl, :, hi:hi + p_n] = of(pe_re) * bb_im[None] + of(pe_im) * bb_re[None]
            c_ref[0, :, gl, :, lo:hi] = c_re[None] * of(pc_re) - c_im[None] * of(pc_im)
            c_ref[0, :, gl, :, hi:hi + p_n] = -(c_re[None] * of(pc_im) + c_im[None] * of(pc_re))

    width = t_n * hg
    lane = lax.broadcasted_iota(jnp.int32, (hg, width), 1)
    for gl in range(gl_n):
        tf, tb = taps[0, gl], taps[1, gl]
        for t in range(t_n):
            sf = t * hg
            sb = (t_n - 1 - t) * hg
            f = tf if sf == 0 else jnp.where(lane >= sf, pltpu.roll(tf, sf, 1), 0.0)
            b = tb if sb == 0 else jnp.where(lane < width - sb, pltpu.roll(tb, width - sb, 1), 0.0)
            m_ref[0, t, gl] = f + b


def _s5_params(a_re, a_im, log_dt, b_re, b_im, c_re, c_im, nb):
    g_n = a_re.shape[1]
    gl_n = g_n // nb
    p_n, hg, t_n = S5_STATE, S5_GROUP, S5_CHUNK
    ldt = jnp.broadcast_to(log_dt[:, :, None], (2, g_n, p_n))
    spec3 = pl.BlockSpec((2, gl_n, p_n), lambda j: (0, j, 0))
    spec4 = pl.BlockSpec((2, gl_n, hg, p_n), lambda j: (0, j, 0, 0))
    spec_b = pl.BlockSpec((2, gl_n, p_n, hg), lambda j: (0, j, 0, 0))
    wide = 4 * p_n
    comp = jax.ShapeDtypeStruct((nb, t_n, gl_n, hg, wide), F32)
    comp_spec = pl.BlockSpec((1, t_n, gl_n, hg, wide), lambda j: (j, 0, 0, 0, 0))
    return pl.pallas_call(
        _s5_param_kernel,
        out_shape=(comp, comp, comp, jax.ShapeDtypeStruct((2, 2, g_n, p_n), F32)),
        grid=(nb,),
        in_specs=[spec3, spec3, spec3, spec_b, spec_b, spec4, spec4],
        out_specs=(comp_spec, comp_spec, comp_spec, pl.BlockSpec((2, 2, gl_n, p_n), lambda j: (0, 0, j, 0))),
        compiler_params=_cparams(("parallel",)),
        name="s5_params",
    )(a_re, a_im, ldt, b_re, b_im, c_re, c_im)


def _expand_block_diag(comp, rep_ref, mask_ref, w_scr):
    k = w_scr.shape[0]
    period = mask_ref.shape[1]
    cb = comp.astype(BF16)
    step = 512
    for c0 in range(0, k, step):
        blk = _dot(cb, rep_ref[:, c0:c0 + step]).astype(BF16)
        for q0 in range(0, step, period):
            w_scr[:, c0 + q0:c0 + q0 + period] = blk[:, q0:q0 + period] * mask_ref[...]


def _gather_chunk_rows(u_ref, rows, dst):
    t_n = S5_CHUNK
    for t in range(t_n):
        dst[:, t * LANES:(t + 1) * LANES] = u_ref[0, pl.ds(t, rows, stride=t_n), :].astype(dst.dtype)


def _s5_ein_kernel(u_ref, ec_ref, rep_ref, mask_ref, o_ref, w_scr, u_scr):
    _expand_block_diag(ec_ref[0], rep_ref, mask_ref, w_scr)
    _gather_chunk_rows(u_ref, u_scr.shape[0], u_scr)
    o_ref[0] = _dot(u_scr[...], w_scr[...])


def _s5_chunk_inputs(u_blk, e_comp, rep_e, mask_e):
    nb, m_all, _ = u_blk.shape
    k, cw = e_comp.shape[1:]
    rows = m_all // S5_CHUNK
    return pl.pallas_call(
        _s5_ein_kernel,
        out_shape=jax.ShapeDtypeStruct((nb, rows, k), F32),
        grid=(nb,),
        in_specs=[pl.BlockSpec((1, m_all, LANES), lambda j: (j, 0, 0)),
                  pl.BlockSpec((1, k, cw), lambda j: (j, 0, 0)),
                  pl.BlockSpec(rep_e.shape, lambda j: (0, 0)),
                  pl.BlockSpec(mask_e.shape, lambda j: (0, 0))],
        out_specs=pl.BlockSpec((1, rows, k), lambda j: (j, 0, 0)),
        scratch_shapes=[pltpu.VMEM((k, k), BF16), pltpu.VMEM((rows, k), BF16)],
        compiler_params=_cparams(("parallel",)),
        name="s5_chunk_inputs",
    )(u_blk, e_comp, rep_e, mask_e)


def _s5_bscan_kernel(e_ref, a_ref, o_ref, *, bsz, n_ctx, n_lat):
    q = e_ref.shape[2] // 4
    planes = lambda row, d: (row[:, (2 * d) * q:(2 * d + 1) * q], row[:, (2 * d + 1) * q:(2 * d + 2) * q])
    coef = [planes(a_ref[0], d) for d in range(2)]
    ctx0 = bsz * n_lat

    def advance(state, rows):
        new = []
        for (sr, si), (b, d), row in zip(state, [(b, d) for b in range(bsz) for d in range(2)], rows):
            ar, ai = coef[d]
            er, ei = planes(e_ref[0, pl.ds(row, 1), :], d)
            new.append((ar * sr - ai * si + er, ar * si + ai * sr + ei))
        return tuple(new)

    def ctx_step(s, state):
        rows = [ctx0 + b * n_ctx + (s if d == 0 else n_ctx - 1 - s) for b in range(bsz) for d in range(2)]
        return advance(state, rows)

    def lat_step(s, state):
        rows = [b * n_lat + (s if d == 0 else n_lat - 1 - s) for b in range(bsz) for d in range(2)]
        for (sr, si), (b, d), row in zip(state, [(b, d) for b in range(bsz) for d in range(2)], rows):
            o_ref[0, pl.ds(row, 1), (2 * d) * q:(2 * d + 1) * q] = sr
            o_ref[0, pl.ds(row, 1), (2 * d + 1) * q:(2 * d + 2) * q] = si
        return advance(state, rows)

    zero = jnp.zeros((1, q), F32)
    state = tuple((zero, zero) for _ in range(2 * bsz))
    state = lax.fori_loop(0, n_ctx, ctx_step, state)
    lax.fori_loop(0, n_lat, lat_step, state)


def _s5_bscan(e_rows, a_rows, bsz, n_ctx, n_lat):
    nb, rows, k = e_rows.shape
    return pl.pallas_call(
        functools.partial(_s5_bscan_kernel, bsz=bsz, n_ctx=n_ctx, n_lat=n_lat),
        out_shape=jax.ShapeDtypeStruct((nb, bsz * n_lat, k), F32),
        grid=(nb,),
        in_specs=[pl.BlockSpec((1, rows, k), lambda j: (j, 0, 0)),
                  pl.BlockSpec((1, 1, k), lambda j: (j, 0, 0))],
        out_specs=pl.BlockSpec((1, bsz * n_lat, k), lambda j: (j, 0, 0)),
        compiler_params=_cparams(("parallel",)),
        name="s5_scan",
    )(e_rows, a_rows)


def _s5_out_kernel(u_ref, s_ref, mc_ref, cc_ref, d_ref, rep_m_ref, mask_m_ref, rep_e_ref, mask_e_ref,
                   o_ref, wm_scr, wc_scr, u_scr):
    @pl.when(pl.program_id(1) == 0)
    def _():
        _expand_block_diag(mc_ref[0], rep_m_ref, mask_m_ref, wm_scr)
        _expand_block_diag(cc_ref[0], rep_e_ref, mask_e_ref, wc_scr)

    t_n = S5_CHUNK
    rows = s_ref.shape[1]
    _gather_chunk_rows(u_ref, rows, u_scr)
    u = u_scr[...]
    y = _dot(u.astype(BF16), wm_scr[...]) + _dot_nt(s_ref[0].astype(BF16), wc_scr[...])
    y = y + u * d_ref[0]
    for t in range(t_n):
        o_ref[0, pl.ds(t, rows, stride=t_n), :] = y[:, t * LANES:(t + 1) * LANES]


def _s5_outputs(u_blk, s_rows, m_comp, c_comp, d_rows, rep_m, mask_m, rep_e, mask_e):
    nb = u_blk.shape[0]
    rows, k = s_rows.shape[1:]
    cw = m_comp.shape[2]
    tr = rows // 2
    tok = tr * S5_CHUNK
    const = lambda a: pl.BlockSpec(a.shape, lambda j, i: (0, 0))
    return pl.pallas_call(
        _s5_out_kernel,
        out_shape=jax.ShapeDtypeStruct((nb, rows * S5_CHUNK, LANES), F32),
        grid=(nb, rows // tr),
        in_specs=[pl.BlockSpec((1, tok, LANES), lambda j, i: (j, i, 0)),
                  pl.BlockSpec((1, tr, k), lambda j, i: (j, i, 0)),
                  pl.BlockSpec((1, k, cw), lambda j, i: (j, 0, 0)),
                  pl.BlockSpec((1, k, cw), lambda j, i: (j, 0, 0)),
                  pl.BlockSpec((1, 1, k), lambda j, i: (j, 0, 0)),
                  const(rep_m), const(mask_m), const(rep_e), const(mask_e)],
        out_specs=pl.BlockSpec((1, tok, LANES), lambda j, i: (j, i, 0)),
        scratch_shapes=[pltpu.VMEM((k, k), BF16), pltpu.VMEM((k, k), BF16), pltpu.VMEM((tr, k), F32)],
        compiler_params=_cparams(("parallel", "arbitrary")),
        name="s5_outputs",
    )(u_blk, s_rows, m_comp, c_comp, d_rows, rep_m, mask_m, rep_e, mask_e)


def _s5_branch(u_blk, bsz, l_ctx, l_lat, a_re, a_im, log_dt, b_re, b_im, c_re, c_im, s5_d):
    nb, m_all, _ = u_blk.shape
    hg, t_n, p_n = S5_GROUP, S5_CHUNK, S5_STATE
    g_n = a_re.shape[1]
    gl_n = g_n // nb
    k = t_n * LANES
    n_lat = l_lat // t_n
    n_ctx = l_ctx // t_n
    e_c, c_c, m_c, a16 = _s5_params(a_re, a_im, log_dt, b_re, b_im, c_re, c_im, nb)
    cw = 4 * p_n
    e_comp, c_comp, m_comp = (a.reshape(nb, k, cw) for a in (e_c, c_c, m_c))

    row_gl = (jnp.arange(k) // hg) % gl_n
    col = jnp.arange(k)
    src = jnp.arange(cw)
    rep_e = ((src[:, None] // p_n == col[None, :] // (gl_n * p_n)) & (src[:, None] % p_n == col[None, :] % p_n)).astype(BF16)
    mask_e = (row_gl[:, None] == (jnp.arange(gl_n * p_n)[None, :] // p_n)).astype(BF16)
    rep_m = ((src[:, None] // hg == col[None, :] // (gl_n * hg)) & (src[:, None] % hg == col[None, :] % hg)).astype(BF16)
    mask_m = (row_gl[:, None] == (jnp.arange(gl_n * hg)[None, :] // hg)).astype(BF16)

    d_rows = jnp.tile(s5_d.reshape(nb, 1, gl_n * hg), (1, 1, t_n))
    a_rows = jnp.transpose(a16.reshape(2, 2, nb, gl_n * p_n), (2, 0, 1, 3)).reshape(nb, 1, 4 * gl_n * p_n)

    e_rows = _s5_chunk_inputs(u_blk, e_comp, rep_e, mask_e)
    s_rows = _s5_bscan(e_rows, a_rows, bsz, n_ctx, n_lat)
    return _s5_outputs(u_blk, s_rows, m_comp, c_comp, d_rows, rep_m, mask_m, rep_e, mask_e)


def _rw_prep_kernel(z_ref, zp_ref, zn_ref, mu_ref, w2_ref, a2_ref, g2_ref, w0_ref, a0_ref,
                    kk_w_ref, ka_ref, rk_ref, seg_ref, segt_ref,
                    r_ref, v_ref, kk_ref, g_ref, bonus_ref, lw_ref, kd_ref, be_ref,
                    *, tm, l_lat, rw):
    j = pl.program_id(1)
    z = z_ref[...].astype(F32)
    lat = j > 0
    tl = lax.broadcasted_iota(jnp.int32, (tm, 1), 0)
    tok = (j - 1) * tm + tl
    col = tl % GRID_W
    m_l = jnp.where(lat, col, tl) > 0
    m_r = jnp.where(lat, col - (GRID_W - 1), tl - (tm - 1)) < 0
    m_u = jnp.logical_and(lat, tok >= GRID_W)
    m_d = jnp.logical_and(lat, tok < l_lat - GRID_W)
    z_ext = jnp.concatenate([zp_ref[...], z_ref[...], zn_ref[...]], axis=0)
    rel = (lax.broadcasted_iota(jnp.int32, (tm, tm + 2 * GRID_W), 1) - GRID_W
           - lax.broadcasted_iota(jnp.int32, (tm, tm + 2 * GRID_W), 0))
    pick = (jnp.logical_and(rel == -1, m_l) | jnp.logical_and(rel == 1, m_r)
            | jnp.logical_and(rel == -GRID_W, m_u) | jnp.logical_and(rel == GRID_W, m_d))
    s = _dot(jnp.where(pick, 1.0, 0.0).astype(z_ext.dtype), z_ext)
    cnt = (m_l.astype(F32) + m_r.astype(F32)) + (m_u.astype(F32) + m_d.astype(F32))
    zs = z + (s * (1.0 / cnt) - z) * mu_ref[...]

    r = zs[:, 0:rw]
    k = zs[:, rw:2 * rw]
    v = zs[:, 2 * rw:3 * rw]
    o = 3 * rw
    wd = zs[:, o:o + LANES]
    ad = zs[:, o + LANES:o + 2 * LANES]
    gd = zs[:, o + 2 * LANES:o + 3 * LANES]

    seg = seg_ref[...]
    segt = segt_ref[...]

    def head_sum(t):
        return _dot_hilo(_dot_hilo(t, seg), segt)

    g_ref[0] = _dot(_sigmoid(gd), g2_ref[...], "bf16").astype(g_ref.dtype)
    kk = k * kk_w_ref[...]
    kk = kk * lax.rsqrt(head_sum(kk * kk) + 1e-12)
    wl = w0_ref[...] + _dot(jnp.tanh(wd), w2_ref[...], "bf16")
    al = a0_ref[...] + _dot(ad, a2_ref[...], "bf16")
    r_ref[0] = r.astype(r_ref.dtype)
    v_ref[0] = v.astype(v_ref.dtype)
    kk_ref[0] = kk.astype(kk_ref.dtype)
    k_sum = jnp.zeros_like(r)
    for d in range(2):
        a = _sigmoid(al[:, d * rw:(d + 1) * rw])
        k_d = k * (1.0 + (a - 1.0) * ka_ref[...])
        k_sum = k_sum + k_d
        lw_ref[d, 0] = -math.exp(-0.5) * _sigmoid(wl[:, d * rw:(d + 1) * rw])
        kd_ref[d, 0] = k_d.astype(kd_ref.dtype)
        be_ref[d, 0] = (kk * a).astype(be_ref.dtype)
    bonus_ref[0] = (head_sum(r * rk_ref[...] * k_sum) * v).astype(bonus_ref.dtype)


def _rw_prep(z_rw, bsz, l_ctx, l_lat, mu, w2bd, a2bd, g2, w0cat, a0cat, k_k, k_a, r_k_flat, seg, segt):
    cols = z_rw.shape[1]
    tm = l_ctx
    rw = g2.shape[1]
    l_all = l_ctx + l_lat
    nblk = l_all // tm
    lat_blk = l_lat // tm
    hb = tm // GRID_W
    lat_hblk = l_lat // GRID_W

    def main_blk(b, j):
        return jnp.where(j == 0, bsz * lat_blk + b, b * lat_blk + j - 1)

    def prev_halo(b, j):
        return b * lat_hblk + jnp.maximum((j - 1) * hb - 1, 0)

    def next_halo(b, j):
        return b * lat_hblk + jnp.minimum(jnp.maximum(j, 1) * hb, lat_hblk - 1)

    full = lambda shape: pl.BlockSpec(shape, lambda b, j: (0,) * len(shape))
    shared = jax.ShapeDtypeStruct((bsz, l_all, rw), BF16)
    lat_only = jax.ShapeDtypeStruct((bsz, l_lat, rw), BF16)
    per_dir = jax.ShapeDtypeStruct((2, bsz, l_all, rw), BF16)
    per_dir_f32 = jax.ShapeDtypeStruct((2, bsz, l_all, rw), F32)
    o_shared = pl.BlockSpec((1, tm, rw), lambda b, j: (b, j, 0))
    o_lat = pl.BlockSpec((1, tm, rw), lambda b, j: (b, jnp.maximum(j - 1, 0), 0))
    o_dir = pl.BlockSpec((2, 1, tm, rw), lambda b, j: (0, b, j, 0))
    return pl.pallas_call(
        functools.partial(_rw_prep_kernel, tm=tm, l_lat=l_lat, rw=rw),
        out_shape=(shared,) * 3 + (lat_only,) * 2 + (per_dir_f32, per_dir, per_dir),
        grid=(bsz, nblk),
        in_specs=[pl.BlockSpec((tm, cols), lambda b, j: (main_blk(b, j), 0)),
                  pl.BlockSpec((GRID_W, cols), lambda b, j: (prev_halo(b, j), 0)),
                  pl.BlockSpec((GRID_W, cols), lambda b, j: (next_halo(b, j), 0)),
                  full((1, cols)), full(w2bd.shape), full(a2bd.shape), full(g2.shape),
                  full((1, 2 * rw)), full((1, 2 * rw)), full((1, rw)), full((1, rw)), full((1, rw)),
                  full(seg.shape), full(segt.shape)],
        out_specs=(o_shared,) * 3 + (o_lat,) * 2 + (o_dir,) * 3,
        compiler_params=_cparams(("parallel", "arbitrary")),
        name="rwkv_prep",
    )(z_rw, z_rw, z_rw, mu.reshape(1, cols), w2bd, a2bd, g2, w0cat, a0cat,
      k_k.reshape(1, rw), k_a.reshape(1, rw), r_k_flat.reshape(1, rw), seg, segt)


def _stack_heads(x, head0):
    return jnp.concatenate([jnp.where(head0, x, 0.0), jnp.where(head0, 0.0, x)], axis=0)


def _rw_step_kernel(rf_ref, rb_ref, vf_ref, vb_ref, kkf_ref, kkb_ref, lwf_ref, lwb_ref, kdf_ref, kdb_ref,
                    bef_ref, beb_ref, yf_ref, yb_ref,
                    z_scr, gc_s, lhs_s, kb_s, kht_s, bht_s, vs_s, rt_s, aab_s, akr_s, arb_s, t_s, x_s, xin_s,
                    yc_s, wu_s, qm_s, yn_s, *, n_pairs, bsz):
    c_n = RW_CHUNK
    n2 = 2 * c_n

    lanes_of = lambda p: slice(p * LANES, (p + 1) * LANES)
    unit = lambda d, b, p: (d * bsz + b) * n_pairs + p
    units = [(d, b, p) for d in range(2) for b in range(bsz) for p in range(n_pairs)]

    @pl.when(pl.program_id(0) == 0)
    def _():
        z_scr[...] = jnp.zeros_like(z_scr)
        qm_s[...] = jnp.zeros_like(qm_s)
        yn_s[...] = jnp.zeros_like(yn_s)

    for d, b, p in units:
        u = unit(d, b, p)
        y_ref = yf_ref if d == 0 else yb_ref
        yz = _dot(qm_s[u], z_scr[u].astype(BF16)) + yn_s[u]
        y_ref[b, :, lanes_of(p)] = yz[:c_n] + yz[c_n:n2]
        z_scr[u] = yz[n2:]

    ri = lax.broadcasted_iota(jnp.int32, (c_n, c_n), 0)
    ci = lax.broadcasted_iota(jnp.int32, (c_n, c_n), 1)
    r2 = lax.broadcasted_iota(jnp.int32, (n2, n2), 0)
    c2 = lax.broadcasted_iota(jnp.int32, (n2, n2), 1)
    t2 = r2 % c_n
    i2 = c2 % c_n
    same_head = (r2 // c_n) == (c2 // c_n)
    diag = r2 == c2
    eye = diag.astype(F32)
    head0 = lax.broadcasted_iota(jnp.int32, (1, LANES), 1) < RW_HEAD
    tri = [(ri >= ci).astype(F32), (ri <= ci).astype(F32)]
    before = [jnp.logical_and(same_head, i2 < t2), jnp.logical_and(same_head, i2 > t2)]
    upto = [jnp.logical_or(m, diag) for m in before]

    def blk(s):
        return (r2 // s) == (c2 // s)

    srcs = [(rf_ref, vf_ref, kkf_ref, lwf_ref, kdf_ref, bef_ref), (rb_ref, vb_ref, kkb_ref, lwb_ref, kdb_ref, beb_ref)]

    for d in range(2):
        r_ref, v_ref, kk_ref, lw_ref, kd_ref, be_ref = srcs[d]
        for b in range(bsz):
            lw = lw_ref[0, b]
            cum = _dot(tri[d], lw, HIGHEST)
            tot = jnp.sum(lw, axis=0, keepdims=True)
            g_inv = jnp.exp(-cum)
            g_hat = jnp.exp(tot - cum)
            kd = kd_ref[0, b].astype(F32)
            be = be_ref[0, b].astype(F32)
            at = kk_ref[b].astype(F32) * jnp.exp(cum - lw)
            rt = r_ref[b].astype(F32) * jnp.exp(cum)
            kt = kd * g_inv
            bt = be * g_inv
            kh = kd * g_hat
            bh = be * g_hat
            gc_s[d * bsz + b] = jnp.exp(tot)
            v = v_ref[b].astype(F32)
            for p in range(n_pairs):
                u = unit(d, b, p)
                st = lambda x: _stack_heads(x[:, lanes_of(p)], head0)
                rt_p = st(rt)
                at_p = st(at).astype(BF16)
                lhs_s[u, :n2] = at_p
                lhs_s[u, n2:] = rt_p.astype(BF16)
                xin_s[u, :, :LANES] = at_p
                rt_s[u] = rt_p
                kb_s[u, :n2] = st(kt).astype(BF16)
                kb_s[u, n2:] = st(bt).astype(BF16)
                kht_s[u] = st(kh).T.astype(BF16)
                bht_s[u] = st(bh).T.astype(BF16)
                vs_s[u] = st(v).astype(BF16)

    for d, b, p in units:
        u = unit(d, b, p)
        g = _dot_nt(lhs_s[u], kb_s[u])
        a_ab = jnp.where(before[d], g[:n2, n2:], 0.0)
        akr_s[u, :n2] = jnp.where(before[d], g[:n2, :n2], 0.0).astype(BF16)
        akr_s[u, n2:] = jnp.where(upto[d], g[n2:, :n2], 0.0).astype(BF16)
        arb_s[u] = jnp.where(upto[d], g[n2:, n2:], 0.0).astype(BF16)
        aab_s[u] = a_ab
        t_s[u] = eye - jnp.where(blk(2), a_ab, 0.0)

    n_units = len(units)
    s = 2
    while s < c_n:
        off = jnp.logical_and(blk(2 * s), jnp.logical_not(blk(s)))
        for u in range(n_units):
            x_s[u] = _dot(t_s[u].astype(BF16), jnp.where(off, aab_s[u], 0.0).astype(BF16)).astype(BF16)
        for u in range(n_units):
            t = t_s[u]
            t_s[u] = t - _dot(x_s[u], t.astype(BF16))
        s *= 2

    for u in range(n_units):
        av = _dot(akr_s[u], vs_s[u])
        xin_s[u, :, LANES:] = av[:n2].astype(BF16)
        yc_s[u] = av[n2:]

    for u in range(n_units):
        wu_s[u] = _dot(t_s[u].astype(BF16), xin_s[u]).astype(BF16)
    for d, b, p in units:
        u = unit(d, b, p)
        wu = wu_s[u]
        q = _dot(arb_s[u], wu)
        bwu = _dot(bht_s[u], wu)
        qm_s[u, :n2] = (rt_s[u] - q[:, :LANES]).astype(BF16)
        qm_s[u, n2:] = (eye * gc_s[d * bsz + b][:, lanes_of(p)] - bwu[:, :LANES]).astype(BF16)
        yn_s[u, :n2] = yc_s[u] - q[:, LANES:]
        yn_s[u, n2:] = _dot(kht_s[u], vs_s[u]) - bwu[:, LANES:]


def _rw_scan(r, v, kk, lw, kd, be, l_ctx):
    bsz, l_all, rw = r.shape
    c_n = RW_CHUNK
    n_all = l_all // c_n
    n_ctx = l_ctx // c_n
    n_lat = n_all - n_ctx
    n_pairs = rw // LANES
    n_units = 2 * bsz * n_pairs
    n2 = 2 * c_n
    vm = lambda nr, cols, dt: pltpu.VMEM((n_units, nr, cols), dt)

    clamp = lambda s: jnp.minimum(s, n_all - 1)
    chunk_f = lambda s: clamp(s)
    chunk_b = lambda s: jnp.where(clamp(s) < n_ctx, n_ctx - 1 - clamp(s), n_all + n_ctx - 1 - clamp(s))
    prev = lambda s: jnp.maximum(s - 1, 0)
    out_f = lambda s: jnp.maximum(prev(s) - n_ctx, 0)
    out_b = lambda s: jnp.where(prev(s) < n_ctx, n_lat - 1, chunk_b(prev(s)) - n_ctx)

    sh_f = pl.BlockSpec((bsz, c_n, rw), lambda s: (0, chunk_f(s), 0))
    sh_b = pl.BlockSpec((bsz, c_n, rw), lambda s: (0, chunk_b(s), 0))
    pd_f = pl.BlockSpec((1, bsz, c_n, rw), lambda s: (0, 0, chunk_f(s), 0))
    pd_b = pl.BlockSpec((1, bsz, c_n, rw), lambda s: (1, 0, chunk_b(s), 0))
    y_shape = jax.ShapeDtypeStruct((bsz, n_lat * c_n, rw), F32)
    return pl.pallas_call(
        functools.partial(_rw_step_kernel, n_pairs=n_pairs, bsz=bsz),
        out_shape=(y_shape, y_shape),
        grid=(n_all + 1,),
        in_specs=[sh_f, sh_b, sh_f, sh_b, sh_f, sh_b, pd_f, pd_b, pd_f, pd_b, pd_f, pd_b],
        out_specs=(pl.BlockSpec((bsz, c_n, rw), lambda s: (0, out_f(s), 0)),
                   pl.BlockSpec((bsz, c_n, rw), lambda s: (0, out_b(s), 0))),
        scratch_shapes=[vm(n2, n2, F32),
                        pltpu.VMEM((2 * bsz, 1, rw), F32),
                        vm(2 * n2, n2, BF16),
                        vm(2 * n2, n2, BF16),
                        vm(n2, n2, BF16),
                        vm(n2, n2, BF16),
                        vm(n2, n2, BF16),
                        vm(n2, n2, F32),
                        vm(n2, n2, F32),
                        vm(2 * n2, n2, BF16),
                        vm(n2, n2, BF16),
                        vm(n2, n2, F32),
                        vm(n2, n2, BF16),
                        vm(n2, 2 * n2, BF16),
                        vm(n2, n2, F32),
                        vm(n2, 2 * n2, BF16),
                        vm(2 * n2, n2, BF16),
                        vm(2 * n2, n2, F32)],
        compiler_params=_cparams(("arbitrary",)),
        name="rwkv_scan",
    )(r, r, v, v, kk, kk, lw, lw, kd, kd, be, be)


def _s5_glu_kernel(y_ref, wa_ref, wb_ref, o_ref, h_scr):
    @pl.when(pl.program_id(1) == 0)
    def _():
        for jb in range(y_ref.shape[0]):
            h_scr[:, jb * LANES:(jb + 1) * LANES] = _gelu_tanh(y_ref[jb]).astype(BF16)

    h = h_scr[...]
    o_ref[...] = (_dot(h, wa_ref[...]) * _sigmoid(_dot(h, wb_ref[...]))).astype(o_ref.dtype)


def _s5_glu(y_blk, w, *, tm, tn):
    nb, m, _ = y_blk.shape
    k = nb * LANES
    n = w.shape[1] // 2
    nj = n // tn
    return pl.pallas_call(
        _s5_glu_kernel,
        out_shape=jax.ShapeDtypeStruct((m, n), BF16),
        grid=(m // tm, nj),
        in_specs=[pl.BlockSpec((nb, tm, LANES), lambda i, j: (0, i, 0)),
                  pl.BlockSpec((k, tn), lambda i, j: (0, j)),
                  pl.BlockSpec((k, tn), lambda i, j: (0, j + nj))],
        out_specs=pl.BlockSpec((tm, tn), lambda i, j: (i, j)),
        scratch_shapes=[pltpu.VMEM((tm, k), BF16)],
        compiler_params=_cparams(("parallel", "arbitrary")),
        name="s5_glu",
    )(y_blk, w, w)


def _rw_merge_kernel(yf_ref, yb_ref, bonus_ref, g_ref, lnw_ref, lnb_ref, seg_ref, segt_ref,
                     ga_ref, gb_ref, s5_ref, w_ref, o_ref, h_scr):
    @pl.when(pl.program_id(1) == 0)
    def _():
        seg = seg_ref[...]
        segt = segt_ref[...]
        inv_n = 1.0 / RW_HEAD

        def head_mean(t):
            return _dot_hilo(_dot_hilo(t, seg), segt) * inv_n

        y = yf_ref[...] + yb_ref[...]
        dy = y - head_mean(y)
        var = head_mean(dy * dy)
        y = dy * lax.rsqrt(var + GN_EPS) * lnw_ref[...] + lnb_ref[...] + bonus_ref[...].astype(F32)
        h_scr[...] = (y * g_ref[...].astype(F32)).astype(BF16)

    rw_out = _dot(h_scr[...], w_ref[...])
    merged = ga_ref[...].astype(F32) * s5_ref[...].astype(F32) + gb_ref[...].astype(F32) * rw_out
    o_ref[...] = merged.astype(o_ref.dtype)


def _rw_merge(y_f, y_b, bonus, g, ln_w, ln_b, seg, segt, gates, s5_out, w_proj, *, tm, tn):
    m, rw = y_f.shape
    n = w_proj.shape[1]
    nj = n // tn
    full = lambda shape: pl.BlockSpec(shape, lambda i, j: (0,) * len(shape))
    return pl.pallas_call(
        _rw_merge_kernel,
        out_shape=jax.ShapeDtypeStruct((m, n), BF16),
        grid=(m // tm, nj),
        in_specs=[pl.BlockSpec((tm, rw), lambda i, j: (i, 0)),
                  pl.BlockSpec((tm, rw), lambda i, j: (i, 0)),
                  pl.BlockSpec((tm, rw), lambda i, j: (i, 0)),
                  pl.BlockSpec((tm, rw), lambda i, j: (i, 0)),
                  full((1, rw)), full((1, rw)), full(seg.shape), full(segt.shape),
                  pl.BlockSpec((tm, tn), lambda i, j: (i, j)),
                  pl.BlockSpec((tm, tn), lambda i, j: (i, j + nj)),
                  pl.BlockSpec((tm, tn), lambda i, j: (i, j)),
                  pl.BlockSpec((rw, tn), lambda i, j: (0, j))],
        out_specs=pl.BlockSpec((tm, tn), lambda i, j: (i, j)),
        scratch_shapes=[pltpu.VMEM((tm, rw), BF16)],
        compiler_params=_cparams(("parallel", "arbitrary")),
        name="rwkv_merge",
    )(y_f, y_b, bonus, g, ln_w.reshape(1, rw), ln_b.reshape(1, rw), seg, segt,
      gates, gates, s5_out, w_proj)


def _resid_ln_kernel(a_ref, w_ref, x_ref, g_ref, nw_ref, sh_ref, sc_ref, h_ref, hn_ref):
    h = x_ref[...] + g_ref[0] * _dot(a_ref[...], w_ref[...])
    h_ref[...] = h
    hn_ref[...] = _lnmod_rows(h, nw_ref[...], sh_ref[0], sc_ref[0]).astype(hn_ref.dtype)


def _resid_matmul_ln(a, w, x2, g_tab, nw, sh_tab, sc_tab, mod_row_of_block, *, tm):
    m, k = a.shape
    n = w.shape[1]
    mod_map = lambda i: (mod_row_of_block(i), 0, 0)
    return pl.pallas_call(
        _resid_ln_kernel,
        out_shape=(jax.ShapeDtypeStruct((m, n), F32), jax.ShapeDtypeStruct((m, n), BF16)),
        grid=(m // tm,),
        in_specs=[pl.BlockSpec((tm, k), lambda i: (i, 0)),
                  pl.BlockSpec((k, n), lambda i: (0, 0)),
                  pl.BlockSpec((tm, n), lambda i: (i, 0)),
                  pl.BlockSpec((1, 1, n), mod_map),
                  pl.BlockSpec((1, n), lambda i: (0, 0)),
                  pl.BlockSpec((1, 1, n), mod_map),
                  pl.BlockSpec((1, 1, n), mod_map)],
        out_specs=(pl.BlockSpec((tm, n), lambda i: (i, 0)), pl.BlockSpec((tm, n), lambda i: (i, 0))),
        compiler_params=_cparams(("parallel",)),
        name="out_proj",
    )(a, w, x2, g_tab, nw.reshape(1, n), sh_tab, sc_tab)


def _wres_swiglu_kernel(a_ref, w1_ref, w3_ref, o_ref, w1_scr, w3_scr):
    @pl.when(pl.program_id(1) == 0)
    def _():
        w1_scr[...] = w1_ref[...].astype(BF16)
        w3_scr[...] = w3_ref[...].astype(BF16)

    a = a_ref[...]
    o_ref[...] = (_silu(_dot(a, w1_scr[...])) * _dot(a, w3_scr[...])).astype(o_ref.dtype)


def _wres_swiglu(a, w13, d_ff, *, tm, tn):
    m, k = a.shape
    nj = d_ff // tn
    return pl.pallas_call(
        _wres_swiglu_kernel,
        out_shape=jax.ShapeDtypeStruct((m, d_ff), BF16),
        grid=(nj, m // tm),
        in_specs=[pl.BlockSpec((tm, k), lambda j, i: (i, 0)),
                  pl.BlockSpec((k, tn), lambda j, i: (0, j)),
                  pl.BlockSpec((k, tn), lambda j, i: (0, j + nj))],
        out_specs=pl.BlockSpec((tm, tn), lambda j, i: (i, j)),
        scratch_shapes=[pltpu.VMEM((k, tn), BF16), pltpu.VMEM((k, tn), BF16)],
        compiler_params=_cparams(("arbitrary", "arbitrary")),
        name="ffn_up",
    )(a, w13, w13)


def _ffn_down_kernel(a_ref, w_ref, x_ref, g_ref, nf_ref, o_ref):
    kk = pl.program_id(1)
    part = _dot(a_ref[...], w_ref[...])

    @pl.when(kk == 0)
    def _():
        o_ref[...] = part

    @pl.when(jnp.logical_and(kk > 0, kk < pl.num_programs(1) - 1))
    def _():
        o_ref[...] += part

    @pl.when(kk == pl.num_programs(1) - 1)
    def _():
        h = x_ref[...] + g_ref[0] * (o_ref[...] + part)
        ms = jnp.mean(h * h, axis=-1, keepdims=True)
        o_ref[...] = h * lax.rsqrt(ms + NORM_EPS) * nf_ref[...]


def _ffn_down(a, w, x2, g_tab, mod_row_of_block, norm_f, *, tm, tk):
    m, k = a.shape
    n = w.shape[1]
    return pl.pallas_call(
        _ffn_down_kernel,
        out_shape=jax.ShapeDtypeStruct((m, n), F32),
        grid=(m // tm, k // tk),
        in_specs=[pl.BlockSpec((tm, tk), lambda i, kk: (i, kk)),
                  pl.BlockSpec((tk, n), lambda i, kk: (kk, 0)),
                  pl.BlockSpec((tm, n), lambda i, kk: (i, 0)),
                  pl.BlockSpec((1, 1, n), lambda i, kk: (mod_row_of_block(i), 0, 0)),
                  pl.BlockSpec((1, n), lambda i, kk: (0, 0))],
        out_specs=pl.BlockSpec((tm, n), lambda i, kk: (i, 0)),
        compiler_params=_cparams(("parallel", "arbitrary"), vmem=VMEM_LIMIT + 8 * 1024 * 1024),
        name="ffn_down",
    )(a, w, x2, g_tab, norm_f.reshape(1, n))


def kernel(x, c, ctx, c_ctx, ada_w, ada_b, norm1_w, w_in, rw_mu, s5_a_re, s5_a_im, s5_log_dt, s5_b_re, s5_b_im, s5_c_re, s5_c_im, s5_d, s5_glu_w, rw_w0, rw_w2, rw_a0, rw_a2, rw_g2, rw_k_k, rw_k_a, rw_r_k, rw_ln_w, rw_ln_b, rw_proj, w_o, norm2_w, ffn_w13, ffn_w2, norm_f):
    assert ada_w.shape[0] == 1, "single-layer block"
    bsz, l_lat, d = x.shape
    l_ctx = ctx.shape[1]
    l_all = l_ctx + l_lat
    s5w = s5_d.shape[1] * s5_d.shape[2]
    rw = rw_g2.shape[2]
    shift_cols = rw_mu.shape[1]
    d_ff = ffn_w2.shape[1]

    c_rows = jnp.concatenate([c, c_ctx[None], jnp.zeros((8 - bsz - 1, d), F32)], axis=0)
    mod = _modulation(c_rows, ada_w[0], ada_b[0])
    tab = lambda k: mod[:, k * d:(k + 1) * d].reshape(8, 1, d)
    sh1, sc1, g1, sh2, sc2, g2 = (tab(k) for k in range(N_MOD))
    ctx_row = bsz

    tm_in = 512
    m_lat = bsz * l_lat
    x2 = x.reshape(m_lat, d)
    n_mix = s5w + shift_cols
    m_all = m_lat + bsz * l_ctx
    lat_row = lambda t: (lambda i: i // (l_lat // t))

    def mix_mod_row(i):
        return jnp.where(i >= m_lat // tm_in, ctx_row, i // (l_lat // tm_in))

    h_all, u_blk = _lnmod_proj(x2, ctx.reshape(bsz * l_ctx, d), norm1_w[0], sh1, sc1, mix_mod_row,
                               w_in[0], s5w, tm=tm_in)
    tm_all = m_all // 8
    z_rw = _wres_matmul(h_all, w_in[0], s5w, shift_cols, rows=m_all, tm=tm_all, tn=shift_cols // 3,
                        out_dtype=BF16, name="in_proj_rw")
    tm = 1024
    gates = _wres_matmul(h_all, w_in[0], n_mix, w_in.shape[2] - n_mix, rows=m_lat, tm=tm, tn=1024,
                         out_dtype=BF16, epilogue="sigmoid", name="in_proj_gates")

    y_blk = _s5_branch(u_blk, bsz, l_ctx, l_lat, s5_a_re[0], s5_a_im[0], s5_log_dt[0],
                               s5_b_re[0], s5_b_im[0], s5_c_re[0], s5_c_im[0], s5_d[0])
    s5_out = _s5_glu(y_blk, s5_glu_w[0].astype(BF16), tm=tm, tn=1024)

    lora = rw_w2.shape[2]
    zl = jnp.zeros((lora, rw), F32)
    w2bd = jnp.concatenate([jnp.concatenate([rw_w2[0, 0], zl], axis=1),
                            jnp.concatenate([zl, rw_w2[0, 1]], axis=1)], axis=0)
    a2bd = jnp.concatenate([jnp.concatenate([rw_a2[0, 0], zl], axis=1),
                            jnp.concatenate([zl, rw_a2[0, 1]], axis=1)], axis=0)
    head_of = jnp.arange(rw) // RW_HEAD
    seg = (head_of[:, None] == jnp.arange(LANES)[None, :]).astype(BF16)
    segt = seg.T
    r, v, kk, g, bonus, lw, kd, be = _rw_prep(
        z_rw, bsz, l_ctx, l_lat, rw_mu[0], w2bd, a2bd, rw_g2[0], rw_w0[0].reshape(1, 2 * rw),
        rw_a0[0].reshape(1, 2 * rw), rw_k_k[0], rw_k_a[0], rw_r_k[0].reshape(rw), seg, segt)
    y_f, y_b = _rw_scan(r, v, kk, lw, kd, be, l_ctx)

    merged = _rw_merge(y_f.reshape(m_lat, rw), y_b.reshape(m_lat, rw), bonus.reshape(m_lat, rw), g.reshape(m_lat, rw),
                       rw_ln_w[0], rw_ln_b[0], seg, segt, gates, s5_out, rw_proj[0].astype(BF16),
                       tm=512, tn=2048)
    tm_o = 512
    h1, h1n = _resid_matmul_ln(merged, w_o[0].astype(BF16), x2, g1, norm2_w[0], sh2, sc2, lat_row(tm_o), tm=tm_o)

    act = _wres_swiglu(h1n, ffn_w13[0], d_ff, tm=tm, tn=512)
    tm_dn = 512
    out = _ffn_down(act, ffn_w2[0].astype(BF16), h1, g2, lat_row(tm_dn), norm_f, tm=tm_dn, tk=d_ff // 2)
    return out.reshape(bsz, l_lat, d)
```

```python
import functools
import math

import jax
import jax.numpy as jnp
from jax import lax
from jax.experimental import pallas as pl
from jax.experimental.pallas import tpu as pltpu

F32 = jnp.float32
BF16 = jnp.bfloat16
HIGHEST = lax.Precision.HIGHEST

N_MOD = 6
NORM_EPS = 1e-6
GN_EPS = 64e-5
GRID_W = 64
S5_GROUP = 16
S5_STATE = 64
S5_CHUNK = 16
RW_HEAD = 64
RW_CHUNK = 64
LANES = 128
VMEM_LIMIT = 48 * 1024 * 1024


def _cparams(sem, vmem=VMEM_LIMIT):
    return pltpu.CompilerParams(dimension_semantics=sem, vmem_limit_bytes=vmem)


def _operands(a, b, precision):
    if precision == "bf16":
        return a.astype(BF16), b.astype(BF16), None
    return a, b, precision


def _dot(a, b, precision=None):
    a, b, precision = _operands(a, b, precision)
    return jnp.dot(a, b, preferred_element_type=F32, precision=precision)


def _dot_nt(a, b, precision=None):
    a, b, precision = _operands(a, b, precision)
    return lax.dot_general(a, b, (((1,), (1,)), ((), ())), preferred_element_type=F32, precision=precision)


def _dot_hilo(a, ind):
    hi = a.astype(BF16)
    lo = (a - hi.astype(F32)).astype(BF16)
    return _dot(hi, ind) + _dot(lo, ind)


def _ind_dot(ind, x):
    hi = x.astype(BF16)
    r1 = x - hi.astype(F32)
    mid = r1.astype(BF16)
    lo = (r1 - mid.astype(F32)).astype(BF16)
    return _dot(ind, hi) + _dot(ind, mid) + _dot(ind, lo)


def _sigmoid(x):
    return 1.0 / (1.0 + jnp.exp(-x))


def _silu(x):
    return x * _sigmoid(x)


def _gelu_tanh(x):
    c = math.sqrt(2.0 / math.pi)
    return 0.5 * x * (1.0 + jnp.tanh(c * (x + 0.044715 * (x * x * x))))


def _mod_kernel(c_ref, w_ref, b_ref, o_ref):
    o_ref[...] = _dot_hilo(_silu(c_ref[...]), w_ref[...].astype(BF16)) + b_ref[...]


def _modulation(c_rows, ada_w, ada_b):
    m, d = c_rows.shape
    n = ada_w.shape[1]
    tn = 1024
    return pl.pallas_call(
        _mod_kernel,
        out_shape=jax.ShapeDtypeStruct((m, n), F32),
        grid=(n // tn,),
        in_specs=[pl.BlockSpec((m, d), lambda j: (0, 0)),
                  pl.BlockSpec((d, tn), lambda j: (0, j)),
                  pl.BlockSpec((1, tn), lambda j: (0, j))],
        out_specs=pl.BlockSpec((m, tn), lambda j: (0, j)),
        compiler_params=_cparams(("arbitrary",)),
        name="modulation",
    )(c_rows, ada_w, ada_b.reshape(1, n))


def _lnmod_rows(x, nw, sh, sc):
    ms = jnp.mean(x * x, axis=-1, keepdims=True)
    y = x * lax.rsqrt(ms + NORM_EPS) * nw
    return y * (1.0 + sc) + sh


def _lnmod_proj_kernel(x_ref, c_ref, nw_ref, sh_ref, sc_ref, w_ref, h_ref, u_ref, w_scr, *, n_lat_blocks):
    @pl.when(pl.program_id(0) == 0)
    def _():
        w_scr[...] = w_ref[...].astype(BF16)

    rows = jnp.where(pl.program_id(0) < n_lat_blocks, x_ref[...], c_ref[...])
    h = _lnmod_rows(rows, nw_ref[...], sh_ref[0], sc_ref[0]).astype(BF16)
    h_ref[...] = h
    z = _dot(h, w_scr[...])
    for jb in range(u_ref.shape[0]):
        u_ref[jb] = z[:, jb * LANES:(jb + 1) * LANES]


def _lnmod_proj(x2, c2, nw, sh_tab, sc_tab, mod_row_of_block, w, n, *, tm):
    m, d = x2.shape
    mc = c2.shape[0]
    nx, nc = m // tm, mc // tm
    mod_map = lambda i: (mod_row_of_block(i), 0, 0)
    return pl.pallas_call(
        functools.partial(_lnmod_proj_kernel, n_lat_blocks=nx),
        out_shape=(jax.ShapeDtypeStruct((m + mc, d), BF16),
                   jax.ShapeDtypeStruct((n // LANES, m + mc, LANES), F32)),
        grid=(nx + nc,),
        in_specs=[pl.BlockSpec((tm, d), lambda i: (jnp.minimum(i, nx - 1), 0)),
                  pl.BlockSpec((tm, d), lambda i: (jnp.maximum(i - nx, 0), 0)),
                  pl.BlockSpec((1, d), lambda i: (0, 0)),
                  pl.BlockSpec((1, 1, d), mod_map),
                  pl.BlockSpec((1, 1, d), mod_map),
                  pl.BlockSpec((d, n), lambda i: (0, 0))],
        out_specs=(pl.BlockSpec((tm, d), lambda i: (i, 0)),
                   pl.BlockSpec((n // LANES, tm, LANES), lambda i: (0, i, 0))),
        scratch_shapes=[pltpu.VMEM((d, n), BF16)],
        compiler_params=_cparams(("arbitrary",)),
        name="lnmod_in_proj_s5",
    )(x2, c2, nw.reshape(1, d), sh_tab, sc_tab, w)


def _wres_mm_kernel(a_ref, w_ref, o_ref, w_scr, *, epilogue):
    @pl.when(pl.program_id(1) == 0)
    def _():
        w_scr[...] = w_ref[...].astype(BF16)

    z = _dot(a_ref[...], w_scr[...])
    if epilogue == "sigmoid":
        z = _sigmoid(z)
    o_ref[...] = z.astype(o_ref.dtype)


def _wres_matmul(a, w, col0, n, *, rows, tm, tn, out_dtype, epilogue=None, name):
    k = a.shape[1]
    assert col0 % LANES == 0 and n % tn == 0 and rows % tm == 0
    return pl.pallas_call(
        functools.partial(_wres_mm_kernel, epilogue=epilogue),
        out_shape=jax.ShapeDtypeStruct((rows, n), out_dtype),
        grid=(n // tn, rows // tm),
        in_specs=[pl.BlockSpec((tm, k), lambda j, i: (i, 0)),
                  pl.BlockSpec((pl.Element(k), pl.Element(tn)), lambda j, i: (0, pl.multiple_of(col0 + j * tn, LANES)))],
        out_specs=pl.BlockSpec((tm, tn), lambda j, i: (i, j)),
        scratch_shapes=[pltpu.VMEM((k, tn), BF16)],
        compiler_params=_cparams(("arbitrary", "arbitrary")),
        name=name,
    )(a, w)


def _s5_param_kernel(are_ref, aim_ref, ldt_ref, bre_ref, bim_ref, cre_ref, cim_ref,
                      e_ref, c_ref, m_ref, a16_ref):
    t_n, hg, p_n = S5_CHUNK, S5_GROUP, S5_STATE
    gl_n = are_ref.shape[1]
    tau = lax.broadcasted_iota(jnp.int32, (t_n, gl_n, p_n), 0).astype(F32)
    taps = {}
    for d in range(2):
        a_re = are_ref[d]
        a_im = aim_ref[d]
        dt = jnp.exp(ldt_ref[d])
        lam = a_re * dt
        th = a_im * dt
        er = jnp.exp(lam)
        ab_re = er * jnp.cos(th)
        ab_im = er * jnp.sin(th)
        den = a_re * a_re + a_im * a_im
        x_re = ab_re - 1.0
        co_re = (x_re * a_re + ab_im * a_im) / den
        co_im = (ab_im * a_re - x_re * a_im) / den

        def power(tv):
            mag = jnp.exp(tv * lam)
            return mag * jnp.cos(tv * th), mag * jnp.sin(tv * th)

        pw_re, pw_im = power(tau if d == 0 else (t_n - 1.0 - tau))
        pe_re, pe_im = power((t_n - 1.0 - tau) if d == 0 else tau)
        pc_re = pw_re * ab_re - pw_im * ab_im
        pc_im = pw_re * ab_im + pw_im * ab_re
        mag16 = jnp.exp(float(t_n) * lam)
        a16_ref[d, 0] = mag16 * jnp.cos(float(t_n) * th)
        a16_ref[d, 1] = mag16 * jnp.sin(float(t_n) * th)

        lo, hi = 2 * d * p_n, (2 * d + 1) * p_n
        for gl in range(gl_n):
            of = lambda x: x[:, gl:gl + 1, :]
            bt_re = bre_ref[d, gl].T
            bt_im = bim_ref[d, gl].T
            bb_re = co_re[gl:gl + 1] * bt_re - co_im[gl:gl + 1] * bt_im
            bb_im = co_re[gl:gl + 1] * bt_im + co_im[gl:gl + 1] * bt_re
            c_re = cre_ref[d, gl]
            c_im = cim_ref[d, gl]
            cp_re = (c_re[None] * of(pw_re) - c_im[None] * of(pw_im)).reshape(t_n * hg, p_n)
            cp_im = (c_re[None] * of(pw_im) + c_im[None] * of(pw_re)).reshape(t_n * hg, p_n)
            taps[d, gl] = _dot_nt(bb_re, cp_re, HIGHEST) - _dot_nt(bb_im, cp_im, HIGHEST)
            e_ref[0, :, gl, :, lo:hi] = of(pe_re) * bb_re[None] - of(pe_im) * bb_im[None]
            e_ref[0, :, gl, :, hi:hi + p_n] = of(pe_re) * bb_im[None] + of(pe_im) * bb_re[None]
            c_ref[0, :, gl, :, lo:hi] = c_re[None] * of(pc_re) - c_im[None] * of(pc_im)
            c_ref[0, :, gl, :, hi:hi + p_n] = -(c_re[None] * of(pc_im) + c_im[None] * of(pc_re))

    width = t_n * hg
    lane = lax.broadcasted_iota(jnp.int32, (hg, width), 1)
    for gl in range(gl_n):
        tf, tb = taps[0, gl], taps[1, gl]
        for t in range(t_n):
            sf = t * hg
            sb = (t_n - 1 - t) * hg
            f = tf if sf == 0 else jnp.where(lane >= sf, pltpu.roll(tf, sf, 1), 0.0)
            b = tb if sb == 0 else jnp.where(lane < width - sb, pltpu.roll(tb, width - sb, 1), 0.0)
            m_ref[0, t, gl] = f + b


def _s5_params(a_re, a_im, log_dt, b_re, b_im, c_re, c_im, nb):
    g_n = a_re.shape[1]
    gl_n = g_n // nb
    p_n, hg, t_n = S5_STATE, S5_GROUP, S5_CHUNK
    ldt = jnp.broadcast_to(log_dt[:, :, None], (2, g_n, p_n))
    spec3 = pl.BlockSpec((2, gl_n, p_n), lambda j: (0, j, 0))
    spec4 = pl.BlockSpec((2, gl_n, hg, p_n), lambda j: (0, j, 0, 0))
    spec_b = pl.BlockSpec((2, gl_n, p_n, hg), lambda j: (0, j, 0, 0))
    wide = 4 * p_n
    comp = jax.ShapeDtypeStruct((nb, t_n, gl_n, hg, wide), F32)
    comp_spec = pl.BlockSpec((1, t_n, gl_n, hg, wide), lambda j: (j, 0, 0, 0, 0))
    return pl.pallas_call(
        _s5_param_kernel,
        out_shape=(comp, comp, comp, jax.ShapeDtypeStruct((2, 2, g_n, p_n), F32)),
        grid=(nb,),
        in_specs=[spec3, spec3, spec3, spec_b, spec_b, spec4, spec4],
        out_specs=(comp_spec, comp_spec, comp_spec, pl.BlockSpec((2, 2, gl_n, p_n), lambda j: (0, 0, j, 0))),
        compiler_params=_cparams(("parallel",)),
        name="s5_params",
    )(a_re, a_im, ldt, b_re, b_im, c_re, c_im)


def _expand_block_diag(comp, rep_ref, mask_ref, w_scr):
    k = w_scr.shape[0]
    period = mask_ref.shape[1]
    cb = comp.astype(BF16)
    step = 512
    for c0 in range(0, k, step):
        blk = _dot(cb, rep_ref[:, c0:c0 + step]).astype(BF16)
        for q0 in range(0, step, period):
            w_scr[:, c0 + q0:c0 + q0 + period] = blk[:, q0:q0 + period] * mask_ref[...]


def _gather_chunk_rows(u_ref, rows, dst):
    t_n = S5_CHUNK
    for t in range(t_n):
        dst[:, t * LANES:(t + 1) * LANES] = u_ref[0, pl.ds(t, rows, stride=t_n), :].astype(dst.dtype)


def _s5_ein_kernel(u_ref, ec_ref, rep_ref, mask_ref, o_ref, w_scr, u_scr):
    _expand_block_diag(ec_ref[0], rep_ref, mask_ref, w_scr)
    _gather_chunk_rows(u_ref, u_scr.shape[0], u_scr)
    o_ref[0] = _dot(u_scr[...], w_scr[...])


def _s5_chunk_inputs(u_blk, e_comp, rep_e, mask_e):
    nb, m_all, _ = u_blk.shape
    k, cw = e_comp.shape[1:]
    rows = m_all // S5_CHUNK
    return pl.pallas_call(
        _s5_ein_kernel,
        out_shape=jax.ShapeDtypeStruct((nb, rows, k), F32),
        grid=(nb,),
        in_specs=[pl.BlockSpec((1, m_all, LANES), lambda j: (j, 0, 0)),
                  pl.BlockSpec((1, k, cw), lambda j: (j, 0, 0)),
                  pl.BlockSpec(rep_e.shape, lambda j: (0, 0)),
                  pl.BlockSpec(mask_e.shape, lambda j: (0, 0))],
        out_specs=pl.BlockSpec((1, rows, k), lambda j: (j, 0, 0)),
        scratch_shapes=[pltpu.VMEM((k, k), BF16), pltpu.VMEM((rows, k), BF16)],
        compiler_params=_cparams(("parallel",)),
        name="s5_chunk_inputs",
    )(u_blk, e_comp, rep_e, mask_e)


def _s5_bscan_kernel(e_ref, a_ref, o_ref, *, bsz, n_ctx, n_lat):
    q = e_ref.shape[2] // 4
    planes = lambda row, d: (row[:, (2 * d) * q:(2 * d + 1) * q], row[:, (2 * d + 1) * q:(2 * d + 2) * q])
    coef = [planes(a_ref[0], d) for d in range(2)]
    ctx0 = bsz * n_lat

    def advance(state, rows):
        new = []
        for (sr, si), (b, d), row in zip(state, [(b, d) for b in range(bsz) for d in range(2)], rows):
            ar, ai = coef[d]
            er, ei = planes(e_ref[0, pl.ds(row, 1), :], d)
            new.append((ar * sr - ai * si + er, ar * si + ai * sr + ei))
        return tuple(new)

    def ctx_step(s, state):
        rows = [ctx0 + b * n_ctx + (s if d == 0 else n_ctx - 1 - s) for b in range(bsz) for d in range(2)]
        return advance(state, rows)

    def lat_step(s, state):
        rows = [b * n_lat + (s if d == 0 else n_lat - 1 - s) for b in range(bsz) for d in range(2)]
        for (sr, si), (b, d), row in zip(state, [(b, d) for b in range(bsz) for d in range(2)], rows):
            o_ref[0, pl.ds(row, 1), (2 * d) * q:(2 * d + 1) * q] = sr
            o_ref[0, pl.ds(row, 1), (2 * d + 1) * q:(2 * d + 2) * q] = si
        return advance(state, rows)

    zero = jnp.zeros((1, q), F32)
    state = tuple((zero, zero) for _ in range(2 * bsz))
    state = lax.fori_loop(0, n_ctx, ctx_step, state)
    lax.fori_loop(0, n_lat, lat_step, state)


def _s5_bscan(e_rows, a_rows, bsz, n_ctx, n_lat):
    nb, rows, k = e_rows.shape
    return pl.pallas_call(
        functools.partial(_s5_bscan_kernel, bsz=bsz, n_ctx=n_ctx, n_lat=n_lat),
        out_shape=jax.ShapeDtypeStruct((nb, bsz * n_lat, k), F32),
        grid=(nb,),
        in_specs=[pl.BlockSpec((1, rows, k), lambda j: (j, 0, 0)),
                  pl.BlockSpec((1, 1, k), lambda j: (j, 0, 0))],
        out_specs=pl.BlockSpec((1, bsz * n_lat, k), lambda j: (j, 0, 0)),
        compiler_params=_cparams(("parallel",)),
        name="s5_scan",
    )(e_rows, a_rows)


def _s5_out_kernel(u_ref, s_ref, mc_ref, cc_ref, d_ref, rep_m_ref, mask_m_ref, rep_e_ref, mask_e_ref,
                   o_ref, wm_scr, wc_scr, u_scr):
    @pl.when(pl.program_id(1) == 0)
    def _():
        _expand_block_diag(mc_ref[0], rep_m_ref, mask_m_ref, wm_scr)
        _expand_block_diag(cc_ref[0], rep_e_ref, mask_e_ref, wc_scr)

    t_n = S5_CHUNK
    rows = s_ref.shape[1]
    _gather_chunk_rows(u_ref, rows, u_scr)
    u = u_scr[...]
    y = _dot(u.astype(BF16), wm_scr[...]) + _dot_nt(s_ref[0].astype(BF16), wc_scr[...])
    y = y + u * d_ref[0]
    for t in range(t_n):
        o_ref[0, pl.ds(t, rows, stride=t_n), :] = y[:, t * LANES:(t + 1) * LANES]


def _s5_outputs(u_blk, s_rows, m_comp, c_comp, d_rows, rep_m, mask_m, rep_e, mask_e):
    nb = u_blk.shape[0]
    rows, k = s_rows.shape[1:]
    cw = m_comp.shape[2]
    tr = rows // 2
    tok = tr * S5_CHUNK
    const = lambda a: pl.BlockSpec(a.shape, lambda j, i: (0, 0))
    return pl.pallas_call(
        _s5_out_kernel,
        out_shape=jax.ShapeDtypeStruct((nb, rows * S5_CHUNK, LANES), F32),
        grid=(nb, rows // tr),
        in_specs=[pl.BlockSpec((1, tok, LANES), lambda j, i: (j, i, 0)),
                  pl.BlockSpec((1, tr, k), lambda j, i: (j, i, 0)),
                  pl.BlockSpec((1, k, cw), lambda j, i: (j, 0, 0)),
                  pl.BlockSpec((1, k, cw), lambda j, i: (j, 0, 0)),
                  pl.BlockSpec((1, 1, k), lambda j, i: (j, 0, 0)),
                  const(rep_m), const(mask_m), const(rep_e), const(mask_e)],
        out_specs=pl.BlockSpec((1, tok, LANES), lambda j, i: (j, i, 0)),
        scratch_shapes=[pltpu.VMEM((k, k), BF16), pltpu.VMEM((k, k), BF16), pltpu.VMEM((tr, k), F32)],
        compiler_params=_cparams(("parallel", "arbitrary")),
        name="s5_outputs",
    )(u_blk, s_rows, m_comp, c_comp, d_rows, rep_m, mask_m, rep_e, mask_e)


def _s5_branch(u_blk, bsz, l_ctx, l_lat, a_re, a_im, log_dt, b_re, b_im, c_re, c_im, s5_d):
    nb, m_all, _ = u_blk.shape
    hg, t_n, p_n = S5_GROUP, S5_CHUNK, S5_STATE
    g_n = a_re.shape[1]
    gl_n = g_n // nb
    k = t_n * LANES
    n_lat = l_lat // t_n
    n_ctx = l_ctx // t_n
    e_c, c_c, m_c, a16 = _s5_params(a_re, a_im, log_dt, b_re, b_im, c_re, c_im, nb)
    cw = 4 * p_n
    e_comp, c_comp, m_comp = (a.reshape(nb, k, cw) for a in (e_c, c_c, m_c))

    row_gl = (jnp.arange(k) // hg) % gl_n
    col = jnp.arange(k)
    src = jnp.arange(cw)
    rep_e = ((src[:, None] // p_n == col[None, :] // (gl_n * p_n)) & (src[:, None] % p_n == col[None, :] % p_n)).astype(BF16)
    mask_e = (row_gl[:, None] == (jnp.arange(gl_n * p_n)[None, :] // p_n)).astype(BF16)
    rep_m = ((src[:, None] // hg == col[None, :] // (gl_n * hg)) & (src[:, None] % hg == col[None, :] % hg)).astype(BF16)
    mask_m = (row_gl[:, None] == (jnp.arange(gl_n * hg)[None, :] // hg)).astype(BF16)

    d_rows = jnp.tile(s5_d.reshape(nb, 1, gl_n * hg), (1, 1, t_n))
    a_rows = jnp.transpose(a16.reshape(2, 2, nb, gl_n * p_n), (2, 0, 1, 3)).reshape(nb, 1, 4 * gl_n * p_n)

    e_rows = _s5_chunk_inputs(u_blk, e_comp, rep_e, mask_e)
    s_rows = _s5_bscan(e_rows, a_rows, bsz, n_ctx, n_lat)
    return _s5_outputs(u_blk, s_rows, m_comp, c_comp, d_rows, rep_m, mask_m, rep_e, mask_e)


def _rw_prep_kernel(z_ref, zp_ref, zn_ref, mu_ref, w2_ref, a2_ref, g2_ref, w0_ref, a0_ref,
                    kk_w_ref, ka_ref, rk_ref, seg_ref, segt_ref,
                    r_ref, v_ref, kk_ref, g_ref, bonus_ref, lw_ref, kd_ref, be_ref,
                    *, tm, l_lat, rw):
    j = pl.program_id(1)
    z = z_ref[...].astype(F32)
    lat = j > 0
    tl = lax.broadcasted_iota(jnp.int32, (tm, 1), 0)
    tok = (j - 1) * tm + tl
    col = tl % GRID_W
    m_l = jnp.where(lat, col, tl) > 0
    m_r = jnp.where(lat, col - (GRID_W - 1), tl - (tm - 1)) < 0
    m_u = jnp.logical_and(lat, tok >= GRID_W)
    m_d = jnp.logical_and(lat, tok < l_lat - GRID_W)
    z_ext = jnp.concatenate([zp_ref[...], z_ref[...], zn_ref[...]], axis=0)
    rel = (lax.broadcasted_iota(jnp.int32, (tm, tm + 2 * GRID_W), 1) - GRID_W
           - lax.broadcasted_iota(jnp.int32, (tm, tm + 2 * GRID_W), 0))
    pick = (jnp.logical_and(rel == -1, m_l) | jnp.logical_and(rel == 1, m_r)
            | jnp.logical_and(rel == -GRID_W, m_u) | jnp.logical_and(rel == GRID_W, m_d))
    s = _dot(jnp.where(pick, 1.0, 0.0).astype(z_ext.dtype), z_ext)
    cnt = (m_l.astype(F32) + m_r.astype(F32)) + (m_u.astype(F32) + m_d.astype(F32))
    zs = z + (s * (1.0 / cnt) - z) * mu_ref[...]

    r = zs[:, 0:rw]
    k = zs[:, rw:2 * rw]
    v = zs[:, 2 * rw:3 * rw]
    o = 3 * rw
    wd = zs[:, o:o + LANES]
    ad = zs[:, o + LANES:o + 2 * LANES]
    gd = zs[:, o + 2 * LANES:o + 3 * LANES]

    seg = seg_ref[...]
    segt = segt_ref[...]

    def head_sum(t):
        return _dot_hilo(_dot_hilo(t, seg), segt)

    g_ref[0] = _dot(_sigmoid(gd), g2_ref[...], "bf16").astype(g_ref.dtype)
    kk = k * kk_w_ref[...]
    kk = kk * lax.rsqrt(head_sum(kk * kk) + 1e-12)
    wl = w0_ref[...] + _dot(jnp.tanh(wd), w2_ref[...], "bf16")
    al = a0_ref[...] + _dot(ad, a2_ref[...], "bf16")
    r_ref[0] = r.astype(r_ref.dtype)
    v_ref[0] = v.astype(v_ref.dtype)
    kk_ref[0] = kk.astype(kk_ref.dtype)
    k_sum = jnp.zeros_like(r)
    for d in range(2):
        a = _sigmoid(al[:, d * rw:(d + 1) * rw])
        k_d = k * (1.0 + (a - 1.0) * ka_ref[...])
        k_sum = k_sum + k_d
        lw_ref[d, 0] = -math.exp(-0.5) * _sigmoid(wl[:, d * rw:(d + 1) * rw])
        kd_ref[d, 0] = k_d.astype(kd_ref.dtype)
        be_ref[d, 0] = (kk * a).astype(be_ref.dtype)
    bonus_ref[0] = (head_sum(r * rk_ref[...] * k_sum) * v).astype(bonus_ref.dtype)


def _rw_prep(z_rw, bsz, l_ctx, l_lat, mu, w2bd, a2bd, g2, w0cat, a0cat, k_k, k_a, r_k_flat, seg, segt):
    cols = z_rw.shape[1]
    tm = l_ctx
    rw = g2.shape[1]
    l_all = l_ctx + l_lat
    nblk = l_all // tm
    lat_blk = l_lat // tm
    hb = tm // GRID_W
    lat_hblk = l_lat // GRID_W

    def main_blk(b, j):
        return jnp.where(j == 0, bsz * lat_blk + b, b * lat_blk + j - 1)

    def prev_halo(b, j):
        return b * lat_hblk + jnp.maximum((j - 1) * hb - 1, 0)

    def next_halo(b, j):
        return b * lat_hblk + jnp.minimum(jnp.maximum(j, 1) * hb, lat_hblk - 1)

    full = lambda shape: pl.BlockSpec(shape, lambda b, j: (0,) * len(shape))
    shared = jax.ShapeDtypeStruct((bsz, l_all, rw), BF16)
    lat_only = jax.ShapeDtypeStruct((bsz, l_lat, rw), BF16)
    per_dir = jax.ShapeDtypeStruct((2, bsz, l_all, rw), BF16)
    per_dir_f32 = jax.ShapeDtypeStruct((2, bsz, l_all, rw), F32)
    o_shared = pl.BlockSpec((1, tm, rw), lambda b, j: (b, j, 0))
    o_lat = pl.BlockSpec((1, tm, rw), lambda b, j: (b, jnp.maximum(j - 1, 0), 0))
    o_dir = pl.BlockSpec((2, 1, tm, rw), lambda b, j: (0, b, j, 0))
    return pl.pallas_call(
        functools.partial(_rw_prep_kernel, tm=tm, l_lat=l_lat, rw=rw),
        out_shape=(shared,) * 3 + (lat_only,) * 2 + (per_dir_f32, per_dir, per_dir),
        grid=(bsz, nblk),
        in_specs=[pl.BlockSpec((tm, cols), lambda b, j: (main_blk(b, j), 0)),
                  pl.BlockSpec((GRID_W, cols), lambda b, j: (prev_halo(b, j), 0)),
                  pl.BlockSpec((GRID_W, cols), lambda b, j: (next_halo(b, j), 0)),
                  full((1, cols)), full(w2bd.shape), full(a2bd.shape), full(g2.shape),
                  full((1, 2 * rw)), full((1, 2 * rw)), full((1, rw)), full((1, rw)), full((1, rw)),
                  full(seg.shape), full(segt.shape)],
        out_specs=(o_shared,) * 3 + (o_lat,) * 2 + (o_dir,) * 3,
        compiler_params=_cparams(("parallel", "arbitrary")),
        name="rwkv_prep",
    )(z_rw, z_rw, z_rw, mu.reshape(1, cols), w2bd, a2bd, g2, w0cat, a0cat,
      k_k.reshape(1, rw), k_a.reshape(1, rw), r_k_flat.reshape(1, rw), seg, segt)


def _stack_heads(x, head0):
    return jnp.concatenate([jnp.where(head0, x, 0.0), jnp.where(head0, 0.0, x)], axis=0)


def _rw_step_kernel(rf_ref, rb_ref, vf_ref, vb_ref, kkf_ref, kkb_ref, lwf_ref, lwb_ref, kdf_ref, kdb_ref,
                    bef_ref, beb_ref, yf_ref, yb_ref,
                    z_scr, gc_s, lhs_s, kb_s, kht_s, bht_s, vs_s, rt_s, aab_s, akr_s, arb_s, t_s, x_s, xin_s,
                    yc_s, wu_s, qm_s, yn_s, *, n_pairs, bsz):
    c_n = RW_CHUNK
    n2 = 2 * c_n

    lanes_of = lambda p: slice(p * LANES, (p + 1) * LANES)
    unit = lambda d, b, p: (d * bsz + b) * n_pairs + p
    units = [(d, b, p) for d in range(2) for b in range(bsz) for p in range(n_pairs)]

    @pl.when(pl.program_id(0) == 0)
    def _():
        z_scr[...] = jnp.zeros_like(z_scr)
        qm_s[...] = jnp.zeros_like(qm_s)
        yn_s[...] = jnp.zeros_like(yn_s)

    for d, b, p in units:
        u = unit(d, b, p)
        y_ref = yf_ref if d == 0 else yb_ref
        yz = _dot(qm_s[u], z_scr[u].astype(BF16)) + yn_s[u]
        y_ref[b, :, lanes_of(p)] = yz[:c_n] + yz[c_n:n2]
        z_scr[u] = yz[n2:]

    ri = lax.broadcasted_iota(jnp.int32, (c_n, c_n), 0)
    ci = lax.broadcasted_iota(jnp.int32, (c_n, c_n), 1)
    r2 = lax.broadcasted_iota(jnp.int32, (n2, n2), 0)
    c2 = lax.broadcasted_iota(jnp.int32, (n2, n2), 1)
    t2 = r2 % c_n
    i2 = c2 % c_n
    same_head = (r2 // c_n) == (c2 // c_n)
    diag = r2 == c2
    eye = diag.astype(F32)
    head0 = lax.broadcasted_iota(jnp.int32, (1, LANES), 1) < RW_HEAD
    tri = [jnp.where(ri >= ci, 1.0, 0.0).astype(BF16), jnp.where(ri <= ci, 1.0, 0.0).astype(BF16)]
    before = [jnp.logical_and(same_head, i2 < t2), jnp.logical_and(same_head, i2 > t2)]
    upto = [jnp.logical_or(m, diag) for m in before]

    def blk(s):
        return (r2 // s) == (c2 // s)

    srcs = [(rf_ref, vf_ref, kkf_ref, lwf_ref, kdf_ref, bef_ref), (rb_ref, vb_ref, kkb_ref, lwb_ref, kdb_ref, beb_ref)]

    for d in range(2):
        r_ref, v_ref, kk_ref, lw_ref, kd_ref, be_ref = srcs[d]
        for b in range(bsz):
            lw = lw_ref[0, b]
            cum = _ind_dot(tri[d], lw)
            tot = jnp.sum(lw, axis=0, keepdims=True)
            g_inv = jnp.exp(-cum)
            g_hat = jnp.exp(tot - cum)
            kd = kd_ref[0, b].astype(F32)
            be = be_ref[0, b].astype(F32)
            at = kk_ref[b].astype(F32) * jnp.exp(cum - lw)
            rt = r_ref[b].astype(F32) * jnp.exp(cum)
            kt = kd * g_inv
            bt = be * g_inv
            kh = kd * g_hat
            bh = be * g_hat
            gc_s[d * bsz + b] = jnp.exp(tot)
            v = v_ref[b].astype(F32)
            for p in range(n_pairs):
                u = unit(d, b, p)
                st = lambda x: _stack_heads(x[:, lanes_of(p)], head0)
                rt_p = st(rt)
                at_p = st(at).astype(BF16)
                lhs_s[u, :n2] = at_p
                lhs_s[u, n2:] = rt_p.astype(BF16)
                xin_s[u, :, :LANES] = at_p
                rt_s[u] = rt_p
                kb_s[u, :n2] = st(kt).astype(BF16)
                kb_s[u, n2:] = st(bt).astype(BF16)
                kht_s[u] = st(kh).T.astype(BF16)
                bht_s[u] = st(bh).T.astype(BF16)
                vs_s[u] = st(v).astype(BF16)

    for d, b, p in units:
        u = unit(d, b, p)
        g = _dot_nt(lhs_s[u], kb_s[u])
        a_ab = jnp.where(before[d], g[:n2, n2:], 0.0)
        akr_s[u, :n2] = jnp.where(before[d], g[:n2, :n2], 0.0).astype(BF16)
        akr_s[u, n2:] = jnp.where(upto[d], g[n2:, :n2], 0.0).astype(BF16)
        arb_s[u] = jnp.where(upto[d], g[n2:, n2:], 0.0).astype(BF16)
        aab_s[u] = a_ab
        t_s[u] = eye - jnp.where(blk(2), a_ab, 0.0)

    n_units = len(units)
    s = 2
    while s < c_n:
        off = jnp.logical_and(blk(2 * s), jnp.logical_not(blk(s)))
        for u in range(n_units):
            x_s[u] = _dot(t_s[u].astype(BF16), jnp.where(off, aab_s[u], 0.0).astype(BF16)).astype(BF16)
        for u in range(n_units):
            t = t_s[u]
            t_s[u] = t - _dot(x_s[u], t.astype(BF16))
        s *= 2

    for u in range(n_units):
        av = _dot(akr_s[u], vs_s[u])
        xin_s[u, :, LANES:] = av[:n2].astype(BF16)
        yc_s[u] = av[n2:]

    for u in range(n_units):
        wu_s[u] = _dot(t_s[u].astype(BF16), xin_s[u]).astype(BF16)
    for d, b, p in units:
        u = unit(d, b, p)
        wu = wu_s[u]
        q = _dot(arb_s[u], wu)
        bwu = _dot(bht_s[u], wu)
        qm_s[u, :n2] = (rt_s[u] - q[:, :LANES]).astype(BF16)
        qm_s[u, n2:] = (eye * gc_s[d * bsz + b][:, lanes_of(p)] - bwu[:, :LANES]).astype(BF16)
        yn_s[u, :n2] = yc_s[u] - q[:, LANES:]
        yn_s[u, n2:] = _dot(kht_s[u], vs_s[u]) - bwu[:, LANES:]


def _rw_scan(r, v, kk, lw, kd, be, l_ctx):
    bsz, l_all, rw = r.shape
    c_n = RW_CHUNK
    n_all = l_all // c_n
    n_ctx = l_ctx // c_n
    n_lat = n_all - n_ctx
    n_pairs = rw // LANES
    n_units = 2 * bsz * n_pairs
    n2 = 2 * c_n
    vm = lambda nr, cols, dt: pltpu.VMEM((n_units, nr, cols), dt)

    clamp = lambda s: jnp.minimum(s, n_all - 1)
    chunk_f = lambda s: clamp(s)
    chunk_b = lambda s: jnp.where(clamp(s) < n_ctx, n_ctx - 1 - clamp(s), n_all + n_ctx - 1 - clamp(s))
    prev = lambda s: jnp.maximum(s - 1, 0)
    out_f = lambda s: jnp.maximum(prev(s) - n_ctx, 0)
    out_b = lambda s: jnp.where(prev(s) < n_ctx, n_lat - 1, chunk_b(prev(s)) - n_ctx)

    sh_f = pl.BlockSpec((bsz, c_n, rw), lambda s: (0, chunk_f(s), 0))
    sh_b = pl.BlockSpec((bsz, c_n, rw), lambda s: (0, chunk_b(s), 0))
    pd_f = pl.BlockSpec((1, bsz, c_n, rw), lambda s: (0, 0, chunk_f(s), 0))
    pd_b = pl.BlockSpec((1, bsz, c_n, rw), lambda s: (1, 0, chunk_b(s), 0))
    y_shape = jax.ShapeDtypeStruct((bsz, n_lat * c_n, rw), F32)
    return pl.pallas_call(
        functools.partial(_rw_step_kernel, n_pairs=n_pairs, bsz=bsz),
        out_shape=(y_shape, y_shape),
        grid=(n_all + 1,),
        in_specs=[sh_f, sh_b, sh_f, sh_b, sh_f, sh_b, pd_f, pd_b, pd_f, pd_b, pd_f, pd_b],
        out_specs=(pl.BlockSpec((bsz, c_n, rw), lambda s: (0, out_f(s), 0)),
                   pl.BlockSpec((bsz, c_n, rw), lambda s: (0, out_b(s), 0))),
        scratch_shapes=[vm(n2, n2, F32),
                        pltpu.VMEM((2 * bsz, 1, rw), F32),
                        vm(2 * n2, n2, BF16),
                        vm(2 * n2, n2, BF16),
                        vm(n2, n2, BF16),
                        vm(n2, n2, BF16),
                        vm(n2, n2, BF16),
                        vm(n2, n2, F32),
                        vm(n2, n2, F32),
                        vm(2 * n2, n2, BF16),
                        vm(n2, n2, BF16),
                        vm(n2, n2, F32),
                        vm(n2, n2, BF16),
                        vm(n2, 2 * n2, BF16),
                        vm(n2, n2, F32),
                        vm(n2, 2 * n2, BF16),
                        vm(2 * n2, n2, BF16),
                        vm(2 * n2, n2, F32)],
        compiler_params=_cparams(("arbitrary",)),
        name="rwkv_scan",
    )(r, r, v, v, kk, kk, lw, lw, kd, kd, be, be)


def _s5_glu_kernel(y_ref, wa_ref, wb_ref, o_ref, h_scr):
    @pl.when(pl.program_id(1) == 0)
    def _():
        for jb in range(y_ref.shape[0]):
            h_scr[:, jb * LANES:(jb + 1) * LANES] = _gelu_tanh(y_ref[jb]).astype(BF16)

    h = h_scr[...]
    o_ref[...] = (_dot(h, wa_ref[...]) * _sigmoid(_dot(h, wb_ref[...]))).astype(o_ref.dtype)


def _s5_glu(y_blk, w, *, tm, tn):
    nb, m, _ = y_blk.shape
    k = nb * LANES
    n = w.shape[1] // 2
    nj = n // tn
    return pl.pallas_call(
        _s5_glu_kernel,
        out_shape=jax.ShapeDtypeStruct((m, n), BF16),
        grid=(m // tm, nj),
        in_specs=[pl.BlockSpec((nb, tm, LANES), lambda i, j: (0, i, 0)),
                  pl.BlockSpec((k, tn), lambda i, j: (0, j)),
                  pl.BlockSpec((k, tn), lambda i, j: (0, j + nj))],
        out_specs=pl.BlockSpec((tm, tn), lambda i, j: (i, j)),
        scratch_shapes=[pltpu.VMEM((tm, k), BF16)],
        compiler_params=_cparams(("parallel", "arbitrary")),
        name="s5_glu",
    )(y_blk, w, w)


def _rw_merge_kernel(yf_ref, yb_ref, bonus_ref, g_ref, lnw_ref, lnb_ref, seg_ref, segt_ref,
                     ga_ref, gb_ref, s5_ref, w_ref, o_ref, h_scr):
    @pl.when(pl.program_id(1) == 0)
    def _():
        seg = seg_ref[...]
        segt = segt_ref[...]
        inv_n = 1.0 / RW_HEAD

        def head_mean(t):
            return _dot_hilo(_dot_hilo(t, seg), segt) * inv_n

        y = yf_ref[...] + yb_ref[...]
        dy = y - head_mean(y)
        var = head_mean(dy * dy)
        y = dy * lax.rsqrt(var + GN_EPS) * lnw_ref[...] + lnb_ref[...] + bonus_ref[...].astype(F32)
        h_scr[...] = (y * g_ref[...].astype(F32)).astype(BF16)

    rw_out = _dot(h_scr[...], w_ref[...])
    merged = ga_ref[...].astype(F32) * s5_ref[...].astype(F32) + gb_ref[...].astype(F32) * rw_out
    o_ref[...] = merged.astype(o_ref.dtype)


def _rw_merge(y_f, y_b, bonus, g, ln_w, ln_b, seg, segt, gates, s5_out, w_proj, *, tm, tn):
    m, rw = y_f.shape
    n = w_proj.shape[1]
    nj = n // tn
    full = lambda shape: pl.BlockSpec(shape, lambda i, j: (0,) * len(shape))
    return pl.pallas_call(
        _rw_merge_kernel,
        out_shape=jax.ShapeDtypeStruct((m, n), BF16),
        grid=(m // tm, nj),
        in_specs=[pl.BlockSpec((tm, rw), lambda i, j: (i, 0)),
                  pl.BlockSpec((tm, rw), lambda i, j: (i, 0)),
                  pl.BlockSpec((tm, rw), lambda i, j: (i, 0)),
                  pl.BlockSpec((tm, rw), lambda i, j: (i, 0)),
                  full((1, rw)), full((1, rw)), full(seg.shape), full(segt.shape),
                  pl.BlockSpec((tm, tn), lambda i, j: (i, j)),
                  pl.BlockSpec((tm, tn), lambda i, j: (i, j + nj)),
                  pl.BlockSpec((tm, tn), lambda i, j: (i, j)),
                  pl.BlockSpec((rw, tn), lambda i, j: (0, j))],
        out_specs=pl.BlockSpec((tm, tn), lambda i, j: (i, j)),
        scratch_shapes=[pltpu.VMEM((tm, rw), BF16)],
        compiler_params=_cparams(("parallel", "arbitrary")),
        name="rwkv_merge",
    )(y_f, y_b, bonus, g, ln_w.reshape(1, rw), ln_b.reshape(1, rw), seg, segt,
      gates, gates, s5_out, w_proj)


def _resid_ln_kernel(a_ref, w_ref, x_ref, g_ref, nw_ref, sh_ref, sc_ref, h_ref, hn_ref):
    h = x_ref[...] + g_ref[0] * _dot(a_ref[...], w_ref[...])
    h_ref[...] = h
    hn_ref[...] = _lnmod_rows(h, nw_ref[...], sh_ref[0], sc_ref[0]).astype(hn_ref.dtype)


def _resid_matmul_ln(a, w, x2, g_tab, nw, sh_tab, sc_tab, mod_row_of_block, *, tm):
    m, k = a.shape
    n = w.shape[1]
    mod_map = lambda i: (mod_row_of_block(i), 0, 0)
    return pl.pallas_call(
        _resid_ln_kernel,
        out_shape=(jax.ShapeDtypeStruct((m, n), F32), jax.ShapeDtypeStruct((m, n), BF16)),
        grid=(m // tm,),
        in_specs=[pl.BlockSpec((tm, k), lambda i: (i, 0)),
                  pl.BlockSpec((k, n), lambda i: (0, 0)),
                  pl.BlockSpec((tm, n), lambda i: (i, 0)),
                  pl.BlockSpec((1, 1, n), mod_map),
                  pl.BlockSpec((1, n), lambda i: (0, 0)),
                  pl.BlockSpec((1, 1, n), mod_map),
                  pl.BlockSpec((1, 1, n), mod_map)],
        out_specs=(pl.BlockSpec((tm, n), lambda i: (i, 0)), pl.BlockSpec((tm, n), lambda i: (i, 0))),
        compiler_params=_cparams(("parallel",)),
        name="out_proj",
    )(a, w, x2, g_tab, nw.reshape(1, n), sh_tab, sc_tab)


def _wres_swiglu_kernel(a_ref, w1_ref, w3_ref, o_ref, w1_scr, w3_scr):
    @pl.when(pl.program_id(1) == 0)
    def _():
        w1_scr[...] = w1_ref[...].astype(BF16)
        w3_scr[...] = w3_ref[...].astype(BF16)

    a = a_ref[...]
    o_ref[...] = (_silu(_dot(a, w1_scr[...])) * _dot(a, w3_scr[...])).astype(o_ref.dtype)


def _wres_swiglu(a, w13, d_ff, *, tm, tn):
    m, k = a.shape
    nj = d_ff // tn
    return pl.pallas_call(
        _wres_swiglu_kernel,
        out_shape=jax.ShapeDtypeStruct((m, d_ff), BF16),
        grid=(nj, m // tm),
        in_specs=[pl.BlockSpec((tm, k), lambda j, i: (i, 0)),
                  pl.BlockSpec((k, tn), lambda j, i: (0, j)),
                  pl.BlockSpec((k, tn), lambda j, i: (0, j + nj))],
        out_specs=pl.BlockSpec((tm, tn), lambda j, i: (i, j)),
        scratch_shapes=[pltpu.VMEM((k, tn), BF16), pltpu.VMEM((k, tn), BF16)],
        compiler_params=_cparams(("arbitrary", "arbitrary")),
        name="ffn_up",
    )(a, w13, w13)


def _ffn_down_kernel(a_ref, w_ref, x_ref, g_ref, nf_ref, o_ref):
    kk = pl.program_id(1)
    part = _dot(a_ref[...], w_ref[...])

    @pl.when(kk == 0)
    def _():
        o_ref[...] = part

    @pl.when(jnp.logical_and(kk > 0, kk < pl.num_programs(1) - 1))
    def _():
        o_ref[...] += part

    @pl.when(kk == pl.num_programs(1) - 1)
    def _():
        h = x_ref[...] + g_ref[0] * (o_ref[...] + part)
        ms = jnp.mean(h * h, axis=-1, keepdims=True)
        o_ref[...] = h * lax.rsqrt(ms + NORM_EPS) * nf_ref[...]


def _ffn_down(a, w, x2, g_tab, mod_row_of_block, norm_f, *, tm, tk):
    m, k = a.shape
    n = w.shape[1]
    return pl.pallas_call(
        _ffn_down_kernel,
        out_shape=jax.ShapeDtypeStruct((m, n), F32),
        grid=(m // tm, k // tk),
        in_specs=[pl.BlockSpec((tm, tk), lambda i, kk: (i, kk)),
                  pl.BlockSpec((tk, n), lambda i, kk: (kk, 0)),
                  pl.BlockSpec((tm, n), lambda i, kk: (i, 0)),
                  pl.BlockSpec((1, 1, n), lambda i, kk: (mod_row_of_block(i), 0, 0)),
                  pl.BlockSpec((1, n), lambda i, kk: (0, 0))],
        out_specs=pl.BlockSpec((tm, n), lambda i, kk: (i, 0)),
        compiler_params=_cparams(("parallel", "arbitrary"), vmem=VMEM_LIMIT + 8 * 1024 * 1024),
        name="ffn_down",
    )(a, w, x2, g_tab, norm_f.reshape(1, n))


def kernel(x, c, ctx, c_ctx, ada_w, ada_b, norm1_w, w_in, rw_mu, s5_a_re, s5_a_im, s5_log_dt, s5_b_re, s5_b_im, s5_c_re, s5_c_im, s5_d, s5_glu_w, rw_w0, rw_w2, rw_a0, rw_a2, rw_g2, rw_k_k, rw_k_a, rw_r_k, rw_ln_w, rw_ln_b, rw_proj, w_o, norm2_w, ffn_w13, ffn_w2, norm_f):
    assert ada_w.shape[0] == 1, "single-layer block"
    bsz, l_lat, d = x.shape
    l_ctx = ctx.shape[1]
    l_all = l_ctx + l_lat
    s5w = s5_d.shape[1] * s5_d.shape[2]
    rw = rw_g2.shape[2]
    shift_cols = rw_mu.shape[1]
    d_ff = ffn_w2.shape[1]

    c_rows = jnp.concatenate([c, c_ctx[None], jnp.zeros((8 - bsz - 1, d), F32)], axis=0)
    mod = _modulation(c_rows, ada_w[0], ada_b[0])
    tab = lambda k: mod[:, k * d:(k + 1) * d].reshape(8, 1, d)
    sh1, sc1, g1, sh2, sc2, g2 = (tab(k) for k in range(N_MOD))
    ctx_row = bsz

    tm_in = 512
    m_lat = bsz * l_lat
    x2 = x.reshape(m_lat, d)
    n_mix = s5w + shift_cols
    m_all = m_lat + bsz * l_ctx
    lat_row = lambda t: (lambda i: i // (l_lat // t))

    def mix_mod_row(i):
        return jnp.where(i >= m_lat // tm_in, ctx_row, i // (l_lat // tm_in))

    h_all, u_blk = _lnmod_proj(x2, ctx.reshape(bsz * l_ctx, d), norm1_w[0], sh1, sc1, mix_mod_row,
                               w_in[0], s5w, tm=tm_in)
    tm_all = m_all // 8
    z_rw = _wres_matmul(h_all, w_in[0], s5w, shift_cols, rows=m_all, tm=tm_all, tn=shift_cols // 3,
                        out_dtype=BF16, name="in_proj_rw")
    tm = 1024
    gates = _wres_matmul(h_all, w_in[0], n_mix, w_in.shape[2] - n_mix, rows=m_lat, tm=tm, tn=1024,
                         out_dtype=BF16, epilogue="sigmoid", name="in_proj_gates")

    y_blk = _s5_branch(u_blk, bsz, l_ctx, l_lat, s5_a_re[0], s5_a_im[0], s5_log_dt[0],
                               s5_b_re[0], s5_b_im[0], s5_c_re[0], s5_c_im[0], s5_d[0])
    s5_out = _s5_glu(y_blk, s5_glu_w[0].astype(BF16), tm=tm, tn=1024)

    lora = rw_w2.shape[2]
    zl = jnp.zeros((lora, rw), F32)
    w2bd = jnp.concatenate([jnp.concatenate([rw_w2[0, 0], zl], axis=1),
                            jnp.concatenate([zl, rw_w2[0, 1]], axis=1)], axis=0)
    a2bd = jnp.concatenate([jnp.concatenate([rw_a2[0, 0], zl], axis=1),
                            jnp.concatenate([zl, rw_a2[0, 1]], axis=1)], axis=0)
    head_of = jnp.arange(rw) // RW_HEAD
    seg = (head_of[:, None] == jnp.arange(LANES)[None, :]).astype(BF16)
    segt = seg.T
    r, v, kk, g, bonus, lw, kd, be = _rw_prep(
        z_rw, bsz, l_ctx, l_lat, rw_mu[0], w2bd, a2bd, rw_g2[0], rw_w0[0].reshape(1, 2 * rw),
        rw_a0[0].reshape(1, 2 * rw), rw_k_k[0], rw_k_a[0], rw_r_k[0].reshape(rw), seg, segt)
    y_f, y_b = _rw_scan(r, v, kk, lw, kd, be, l_ctx)

    merged = _rw_merge(y_f.reshape(m_lat, rw), y_b.reshape(m_lat, rw), bonus.reshape(m_lat, rw), g.reshape(m_lat, rw),
                       rw_ln_w[0], rw_ln_b[0], seg, segt, gates, s5_out, rw_proj[0].astype(BF16),
                       tm=512, tn=2048)
    tm_o = 512
    h1, h1n = _resid_matmul_ln(merged, w_o[0].astype(BF16), x2, g1, norm2_w[0], sh2, sc2, lat_row(tm_o), tm=tm_o)

    act = _wres_swiglu(h1n, ffn_w13[0], d_ff, tm=tm, tn=512)
    tm_dn = 512
    out = _ffn_down(act, ffn_w2[0].astype(BF16), h1, g2, lat_row(tm_dn), norm_f, tm=tm_dn, tk=d_ff // 2)
    return out.reshape(bsz, l_lat, d)
```

```python
import functools
import math

import jax
import jax.numpy as jnp
from jax import lax
from jax.experimental import pallas as pl
from jax.experimental.pallas import tpu as pltpu

F32 = jnp.float32
BF16 = jnp.bfloat16
HIGHEST = lax.Precision.HIGHEST

N_MOD = 6
NORM_EPS = 1e-6
GN_EPS = 64e-5
GRID_W = 64
S5_GROUP = 16
S5_STATE = 64
S5_CHUNK = 16
RW_HEAD = 64
RW_CHUNK = 64
LANES = 128
VMEM_LIMIT = 48 * 1024 * 1024


def _cparams(sem, vmem=VMEM_LIMIT):
    return pltpu.CompilerParams(dimension_semantics=sem, vmem_limit_bytes=vmem)


def _operands(a, b, precision):
    if precision == "bf16":
        return a.astype(BF16), b.astype(BF16), None
    return a, b, precision


def _dot(a, b, precision=None):
    a, b, precision = _operands(a, b, precision)
    return jnp.dot(a, b, preferred_element_type=F32, precision=precision)


def _dot_nt(a, b, precision=None):
    a, b, precision = _operands(a, b, precision)
    return lax.dot_general(a, b, (((1,), (1,)), ((), ())), preferred_element_type=F32, precision=precision)


def _dot_hilo(a, ind):
    hi = a.astype(BF16)
    lo = (a - hi.astype(F32)).astype(BF16)
    return _dot(hi, ind) + _dot(lo, ind)


def _sigmoid(x):
    return 1.0 / (1.0 + jnp.exp(-x))


def _silu(x):
    return x * _sigmoid(x)


def _gelu_tanh(x):
    c = math.sqrt(2.0 / math.pi)
    return 0.5 * x * (1.0 + jnp.tanh(c * (x + 0.044715 * (x * x * x))))


def _mod_kernel(c_ref, w_ref, b_ref, o_ref):
    o_ref[...] = _dot_hilo(_silu(c_ref[...]), w_ref[...].astype(BF16)) + b_ref[...]


def _modulation(c_rows, ada_w, ada_b):
    m, d = c_rows.shape
    n = ada_w.shape[1]
    tn = 1024
    return pl.pallas_call(
        _mod_kernel,
        out_shape=jax.ShapeDtypeStruct((m, n), F32),
        grid=(n // tn,),
        in_specs=[pl.BlockSpec((m, d), lambda j: (0, 0)),
                  pl.BlockSpec((d, tn), lambda j: (0, j)),
                  pl.BlockSpec((1, tn), lambda j: (0, j))],
        out_specs=pl.BlockSpec((m, tn), lambda j: (0, j)),
        compiler_params=_cparams(("arbitrary",)),
        name="modulation",
    )(c_rows, ada_w, ada_b.reshape(1, n))


def _lnmod_rows(x, nw, sh, sc):
    ms = jnp.mean(x * x, axis=-1, keepdims=True)
    y = x * lax.rsqrt(ms + NORM_EPS) * nw
    return y * (1.0 + sc) + sh


def _lnmod_proj_kernel(x_ref, c_ref, nw_ref, sh_ref, sc_ref, w_ref, h_ref, u_ref, w_scr, *, n_lat_blocks):
    @pl.when(pl.program_id(0) == 0)
    def _():
        w_scr[...] = w_ref[...].astype(BF16)

    rows = jnp.where(pl.program_id(0) < n_lat_blocks, x_ref[...], c_ref[...])
    h = _lnmod_rows(rows, nw_ref[...], sh_ref[0], sc_ref[0]).astype(BF16)
    h_ref[...] = h
    z = _dot(h, w_scr[...])
    for jb in range(u_ref.shape[0]):
        u_ref[jb] = z[:, jb * LANES:(jb + 1) * LANES]


def _lnmod_proj(x2, c2, nw, sh_tab, sc_tab, mod_row_of_block, w, n, *, tm):
    m, d = x2.shape
    mc = c2.shape[0]
    nx, nc = m // tm, mc // tm
    mod_map = lambda i: (mod_row_of_block(i), 0, 0)
    return pl.pallas_call(
        functools.partial(_lnmod_proj_kernel, n_lat_blocks=nx),
        out_shape=(jax.ShapeDtypeStruct((m + mc, d), BF16),
                   jax.ShapeDtypeStruct((n // LANES, m + mc, LANES), F32)),
        grid=(nx + nc,),
        in_specs=[pl.BlockSpec((tm, d), lambda i: (jnp.minimum(i, nx - 1), 0)),
                  pl.BlockSpec((tm, d), lambda i: (jnp.maximum(i - nx, 0), 0)),
                  pl.BlockSpec((1, d), lambda i: (0, 0)),
                  pl.BlockSpec((1, 1, d), mod_map),
                  pl.BlockSpec((1, 1, d), mod_map),
                  pl.BlockSpec((d, n), lambda i: (0, 0))],
        out_specs=(pl.BlockSpec((tm, d), lambda i: (i, 0)),
                   pl.BlockSpec((n // LANES, tm, LANES), lambda i: (0, i, 0))),
        scratch_shapes=[pltpu.VMEM((d, n), BF16)],
        compiler_params=_cparams(("arbitrary",)),
        name="lnmod_in_proj_s5",
    )(x2, c2, nw.reshape(1, d), sh_tab, sc_tab, w)


def _wres_mm_kernel(a_ref, w_ref, o_ref, w_scr, *, epilogue):
    @pl.when(pl.program_id(1) == 0)
    def _():
        w_scr[...] = w_ref[...].astype(BF16)

    z = _dot(a_ref[...], w_scr[...])
    if epilogue == "sigmoid":
        z = _sigmoid(z)
    o_ref[...] = z.astype(o_ref.dtype)


def _wres_matmul(a, w, col0, n, *, rows, tm, tn, out_dtype, epilogue=None, name):
    k = a.shape[1]
    assert col0 % LANES == 0 and n % tn == 0 and rows % tm == 0
    return pl.pallas_call(
        functools.partial(_wres_mm_kernel, epilogue=epilogue),
        out_shape=jax.ShapeDtypeStruct((rows, n), out_dtype),
        grid=(n // tn, rows // tm),
        in_specs=[pl.BlockSpec((tm, k), lambda j, i: (i, 0)),
                  pl.BlockSpec((pl.Element(k), pl.Element(tn)), lambda j, i: (0, pl.multiple_of(col0 + j * tn, LANES)))],
        out_specs=pl.BlockSpec((tm, tn), lambda j, i: (i, j)),
        scratch_shapes=[pltpu.VMEM((k, tn), BF16)],
        compiler_params=_cparams(("arbitrary", "arbitrary")),
        name=name,
    )(a, w)


def _s5_param_kernel(are_ref, aim_ref, ldt_ref, bre_ref, bim_ref, cre_ref, cim_ref,
                      e_ref, c_ref, m_ref, a16_ref):
    t_n, hg, p_n = S5_CHUNK, S5_GROUP, S5_STATE
    gl_n = are_ref.shape[1]
    tau = lax.broadcasted_iota(jnp.int32, (t_n, gl_n, p_n), 0).astype(F32)
    taps = {}
    for d in range(2):
        a_re = are_ref[d]
        a_im = aim_ref[d]
        dt = jnp.exp(ldt_ref[d])
        lam = a_re * dt
        th = a_im * dt
        er = jnp.exp(lam)
        ab_re = er * jnp.cos(th)
        ab_im = er * jnp.sin(th)
        den = a_re * a_re + a_im * a_im
        x_re = ab_re - 1.0
        co_re = (x_re * a_re + ab_im * a_im) / den
        co_im = (ab_im * a_re - x_re * a_im) / den

        def power(tv):
            mag = jnp.exp(tv * lam)
            return mag * jnp.cos(tv * th), mag * jnp.sin(tv * th)

        pw_re, pw_im = power(tau if d == 0 else (t_n - 1.0 - tau))
        pe_re, pe_im = power((t_n - 1.0 - tau) if d == 0 else tau)
        pc_re, pc_im = power((tau + 1.0) if d == 0 else (t_n - tau))
        mag16 = jnp.exp(float(t_n) * lam)
        a16_ref[d, 0] = mag16 * jnp.cos(float(t_n) * th)
        a16_ref[d, 1] = mag16 * jnp.sin(float(t_n) * th)

        lo, hi = 2 * d * p_n, (2 * d + 1) * p_n
        for gl in range(gl_n):
            of = lambda x: x[:, gl:gl + 1, :]
            bt_re = bre_ref[d, gl].T
            bt_im = bim_ref[d, gl].T
            bb_re = co_re[gl:gl + 1] * bt_re - co_im[gl:gl + 1] * bt_im
            bb_im = co_re[gl:gl + 1] * bt_im + co_im[gl:gl + 1] * bt_re
            c_re = cre_ref[d, gl]
            c_im = cim_ref[d, gl]
            cp_re = (c_re[None] * of(pw_re) - c_im[None] * of(pw_im)).reshape(t_n * hg, p_n)
            cp_im = (c_re[None] * of(pw_im) + c_im[None] * of(pw_re)).reshape(t_n * hg, p_n)
            taps[d, gl] = _dot_nt(bb_re, cp_re, HIGHEST) - _dot_nt(bb_im, cp_im, HIGHEST)
            e_ref[0, :, gl, :, lo:hi] = of(pe_re) * bb_re[None] - of(pe_im) * bb_im[None]
            e_ref[0, :, gl, :, hi:hi + p_n] = of(pe_re) * bb_im[None] + of(pe_im) * bb_re[None]
            c_ref[0, :, gl, :, lo:hi] = c_re[None] * of(pc_re) - c_im[None] * of(pc_im)
            c_ref[0, :, gl, :, hi:hi + p_n] = -(c_re[None] * of(pc_im) + c_im[None] * of(pc_re))

    width = t_n * hg
    lane = lax.broadcasted_iota(jnp.int32, (hg, width), 1)
    for gl in range(gl_n):
        tf, tb = taps[0, gl], taps[1, gl]
        for t in range(t_n):
            sf = t * hg
            sb = (t_n - 1 - t) * hg
            f = tf if sf == 0 else jnp.where(lane >= sf, pltpu.roll(tf, sf, 1), 0.0)
            b = tb if sb == 0 else jnp.where(lane < width - sb, pltpu.roll(tb, width - sb, 1), 0.0)
            m_ref[0, t, gl] = f + b


def _s5_params(a_re, a_im, log_dt, b_re, b_im, c_re, c_im, nb):
    g_n = a_re.shape[1]
    gl_n = g_n // nb
    p_n, hg, t_n = S5_STATE, S5_GROUP, S5_CHUNK
    ldt = jnp.broadcast_to(log_dt[:, :, None], (2, g_n, p_n))
    spec3 = pl.BlockSpec((2, gl_n, p_n), lambda j: (0, j, 0))
    spec4 = pl.BlockSpec((2, gl_n, hg, p_n), lambda j: (0, j, 0, 0))
    spec_b = pl.BlockSpec((2, gl_n, p_n, hg), lambda j: (0, j, 0, 0))
    wide = 4 * p_n
    comp = jax.ShapeDtypeStruct((nb, t_n, gl_n, hg, wide), F32)
    comp_spec = pl.BlockSpec((1, t_n, gl_n, hg, wide), lambda j: (j, 0, 0, 0, 0))
    return pl.pallas_call(
        _s5_param_kernel,
        out_shape=(comp, comp, comp, jax.ShapeDtypeStruct((2, 2, g_n, p_n), F32)),
        grid=(nb,),
        in_specs=[spec3, spec3, spec3, spec_b, spec_b, spec4, spec4],
        out_specs=(comp_spec, comp_spec, comp_spec, pl.BlockSpec((2, 2, gl_n, p_n), lambda j: (0, 0, j, 0))),
        compiler_params=_cparams(("parallel",)),
        name="s5_params",
    )(a_re, a_im, ldt, b_re, b_im, c_re, c_im)


def _expand_block_diag(comp, rep_ref, mask_ref, w_scr):
    k = w_scr.shape[0]
    period = mask_ref.shape[1]
    cb = comp.astype(BF16)
    step = 512
    for c0 in range(0, k, step):
        blk = _dot(cb, rep_ref[:, c0:c0 + step]).astype(BF16)
        for q0 in range(0, step, period):
            w_scr[:, c0 + q0:c0 + q0 + period] = blk[:, q0:q0 + period] * mask_ref[...]


def _gather_chunk_rows(u_ref, rows, dst):
    t_n = S5_CHUNK
    for t in range(t_n):
        dst[:, t * LANES:(t + 1) * LANES] = u_ref[0, pl.ds(t, rows, stride=t_n), :].astype(dst.dtype)


def _s5_ein_kernel(u_ref, ec_ref, rep_ref, mask_ref, o_ref, w_scr, u_scr):
    _expand_block_diag(ec_ref[0], rep_ref, mask_ref, w_scr)
    _gather_chunk_rows(u_ref, u_scr.shape[0], u_scr)
    o_ref[0] = _dot(u_scr[...], w_scr[...])


def _s5_chunk_inputs(u_blk, e_comp, rep_e, mask_e):
    nb, m_all, _ = u_blk.shape
    k, cw = e_comp.shape[1:]
    rows = m_all // S5_CHUNK
    return pl.pallas_call(
        _s5_ein_kernel,
        out_shape=jax.ShapeDtypeStruct((nb, rows, k), F32),
        grid=(nb,),
        in_specs=[pl.BlockSpec((1, m_all, LANES), lambda j: (j, 0, 0)),
                  pl.BlockSpec((1, k, cw), lambda j: (j, 0, 0)),
                  pl.BlockSpec(rep_e.shape, lambda j: (0, 0)),
                  pl.BlockSpec(mask_e.shape, lambda j: (0, 0))],
        out_specs=pl.BlockSpec((1, rows, k), lambda j: (j, 0, 0)),
        scratch_shapes=[pltpu.VMEM((k, k), BF16), pltpu.VMEM((rows, k), BF16)],
        compiler_params=_cparams(("parallel",)),
        name="s5_chunk_inputs",
    )(u_blk, e_comp, rep_e, mask_e)


def _s5_bscan_kernel(e_ref, a_ref, o_ref, *, bsz, n_ctx, n_lat):
    q = e_ref.shape[2] // 4
    planes = lambda row, d: (row[:, (2 * d) * q:(2 * d + 1) * q], row[:, (2 * d + 1) * q:(2 * d + 2) * q])
    coef = [planes(a_ref[0], d) for d in range(2)]
    ctx0 = bsz * n_lat

    def advance(state, rows):
        new = []
        for (sr, si), (b, d), row in zip(state, [(b, d) for b in range(bsz) for d in range(2)], rows):
            ar, ai = coef[d]
            er, ei = planes(e_ref[0, pl.ds(row, 1), :], d)
            new.append((ar * sr - ai * si + er, ar * si + ai * sr + ei))
        return tuple(new)

    def ctx_step(s, state):
        rows = [ctx0 + b * n_ctx + (s if d == 0 else n_ctx - 1 - s) for b in range(bsz) for d in range(2)]
        return advance(state, rows)

    def lat_step(s, state):
        rows = [b * n_lat + (s if d == 0 else n_lat - 1 - s) for b in range(bsz) for d in range(2)]
        for (sr, si), (b, d), row in zip(state, [(b, d) for b in range(bsz) for d in range(2)], rows):
            o_ref[0, pl.ds(row, 1), (2 * d) * q:(2 * d + 1) * q] = sr
            o_ref[0, pl.ds(row, 1), (2 * d + 1) * q:(2 * d + 2) * q] = si
        return advance(state, rows)

    zero = jnp.zeros((1, q), F32)
    state = tuple((zero, zero) for _ in range(2 * bsz))
    state = lax.fori_loop(0, n_ctx, ctx_step, state)
    lax.fori_loop(0, n_lat, lat_step, state)


def _s5_bscan(e_rows, a_rows, bsz, n_ctx, n_lat):
    nb, rows, k = e_rows.shape
    return pl.pallas_call(
        functools.partial(_s5_bscan_kernel, bsz=bsz, n_ctx=n_ctx, n_lat=n_lat),
        out_shape=jax.ShapeDtypeStruct((nb, bsz * n_lat, k), F32),
        grid=(nb,),
        in_specs=[pl.BlockSpec((1, rows, k), lambda j: (j, 0, 0)),
                  pl.BlockSpec((1, 1, k), lambda j: (j, 0, 0))],
        out_specs=pl.BlockSpec((1, bsz * n_lat, k), lambda j: (j, 0, 0)),
        compiler_params=_cparams(("parallel",)),
        name="s5_scan",
    )(e_rows, a_rows)


def _s5_out_kernel(u_ref, s_ref, mc_ref, cc_ref, d_ref, rep_m_ref, mask_m_ref, rep_e_ref, mask_e_ref,
                   o_ref, wm_scr, wc_scr, u_scr):
    @pl.when(pl.program_id(1) == 0)
    def _():
        _expand_block_diag(mc_ref[0], rep_m_ref, mask_m_ref, wm_scr)
        _expand_block_diag(cc_ref[0], rep_e_ref, mask_e_ref, wc_scr)

    t_n = S5_CHUNK
    rows = s_ref.shape[1]
    _gather_chunk_rows(u_ref, rows, u_scr)
    u = u_scr[...]
    y = _dot(u.astype(BF16), wm_scr[...]) + _dot_nt(s_ref[0].astype(BF16), wc_scr[...])
    y = y + u * d_ref[0]
    for t in range(t_n):
        o_ref[0, pl.ds(t, rows, stride=t_n), :] = y[:, t * LANES:(t + 1) * LANES]


def _s5_outputs(u_blk, s_rows, m_comp, c_comp, d_rows, rep_m, mask_m, rep_e, mask_e):
    nb = u_blk.shape[0]
    rows, k = s_rows.shape[1:]
    cw = m_comp.shape[2]
    tr = rows // 2
    tok = tr * S5_CHUNK
    const = lambda a: pl.BlockSpec(a.shape, lambda j, i: (0, 0))
    return pl.pallas_call(
        _s5_out_kernel,
        out_shape=jax.ShapeDtypeStruct((nb, rows * S5_CHUNK, LANES), F32),
        grid=(nb, rows // tr),
        in_specs=[pl.BlockSpec((1, tok, LANES), lambda j, i: (j, i, 0)),
                  pl.BlockSpec((1, tr, k), lambda j, i: (j, i, 0)),
                  pl.BlockSpec((1, k, cw), lambda j, i: (j, 0, 0)),
                  pl.BlockSpec((1, k, cw), lambda j, i: (j, 0, 0)),
                  pl.BlockSpec((1, 1, k), lambda j, i: (j, 0, 0)),
                  const(rep_m), const(mask_m), const(rep_e), const(mask_e)],
        out_specs=pl.BlockSpec((1, tok, LANES), lambda j, i: (j, i, 0)),
        scratch_shapes=[pltpu.VMEM((k, k), BF16), pltpu.VMEM((k, k), BF16), pltpu.VMEM((tr, k), F32)],
        compiler_params=_cparams(("parallel", "arbitrary")),
        name="s5_outputs",
    )(u_blk, s_rows, m_comp, c_comp, d_rows, rep_m, mask_m, rep_e, mask_e)


def _s5_branch(u_blk, bsz, l_ctx, l_lat, a_re, a_im, log_dt, b_re, b_im, c_re, c_im, s5_d):
    nb, m_all, _ = u_blk.shape
    hg, t_n, p_n = S5_GROUP, S5_CHUNK, S5_STATE
    g_n = a_re.shape[1]
    gl_n = g_n // nb
    k = t_n * LANES
    n_lat = l_lat // t_n
    n_ctx = l_ctx // t_n
    e_c, c_c, m_c, a16 = _s5_params(a_re, a_im, log_dt, b_re, b_im, c_re, c_im, nb)
    cw = 4 * p_n
    e_comp, c_comp, m_comp = (a.reshape(nb, k, cw) for a in (e_c, c_c, m_c))

    row_gl = (jnp.arange(k) // hg) % gl_n
    col = jnp.arange(k)
    src = jnp.arange(cw)
    rep_e = ((src[:, None] // p_n == col[None, :] // (gl_n * p_n)) & (src[:, None] % p_n == col[None, :] % p_n)).astype(BF16)
    mask_e = (row_gl[:, None] == (jnp.arange(gl_n * p_n)[None, :] // p_n)).astype(BF16)
    rep_m = ((src[:, None] // hg == col[None, :] // (gl_n * hg)) & (src[:, None] % hg == col[None, :] % hg)).astype(BF16)
    mask_m = (row_gl[:, None] == (jnp.arange(gl_n * hg)[None, :] // hg)).astype(BF16)

    d_rows = jnp.tile(s5_d.reshape(nb, 1, gl_n * hg), (1, 1, t_n))
    a_rows = jnp.transpose(a16.reshape(2, 2, nb, gl_n * p_n), (2, 0, 1, 3)).reshape(nb, 1, 4 * gl_n * p_n)

    e_rows = _s5_chunk_inputs(u_blk, e_comp, rep_e, mask_e)
    s_rows = _s5_bscan(e_rows, a_rows, bsz, n_ctx, n_lat)
    return _s5_outputs(u_blk, s_rows, m_comp, c_comp, d_rows, rep_m, mask_m, rep_e, mask_e)


def _rw_prep_kernel(z_ref, zp_ref, zn_ref, mu_ref, w2_ref, a2_ref, g2_ref, w0_ref, a0_ref,
                    kk_w_ref, ka_ref, rk_ref, seg_ref, segt_ref,
                    r_ref, v_ref, kk_ref, g_ref, bonus_ref, lw_ref, kd_ref, be_ref,
                    *, tm, l_lat, rw):
    j = pl.program_id(1)
    z = z_ref[...].astype(F32)
    lat = j > 0
    tl = lax.broadcasted_iota(jnp.int32, (tm, 1), 0)
    tok = (j - 1) * tm + tl
    col = tl % GRID_W
    m_l = jnp.where(lat, col, tl) > 0
    m_r = jnp.where(lat, col - (GRID_W - 1), tl - (tm - 1)) < 0
    m_u = jnp.logical_and(lat, tok >= GRID_W)
    m_d = jnp.logical_and(lat, tok < l_lat - GRID_W)
    z_ext = jnp.concatenate([zp_ref[...], z_ref[...], zn_ref[...]], axis=0)
    rel = (lax.broadcasted_iota(jnp.int32, (tm, tm + 2 * GRID_W), 1) - GRID_W
           - lax.broadcasted_iota(jnp.int32, (tm, tm + 2 * GRID_W), 0))
    pick = (jnp.logical_and(rel == -1, m_l) | jnp.logical_and(rel == 1, m_r)
            | jnp.logical_and(rel == -GRID_W, m_u) | jnp.logical_and(rel == GRID_W, m_d))
    s = _dot(jnp.where(pick, 1.0, 0.0).astype(z_ext.dtype), z_ext)
    cnt = (m_l.astype(F32) + m_r.astype(F32)) + (m_u.astype(F32) + m_d.astype(F32))
    zs = z + (s * (1.0 / cnt) - z) * mu_ref[...]

    r = zs[:, 0:rw]
    k = zs[:, rw:2 * rw]
    v = zs[:, 2 * rw:3 * rw]
    o = 3 * rw
    wd = zs[:, o:o + LANES]
    ad = zs[:, o + LANES:o + 2 * LANES]
    gd = zs[:, o + 2 * LANES:o + 3 * LANES]

    seg = seg_ref[...]
    segt = segt_ref[...]

    def head_sum(t):
        return _dot_hilo(_dot_hilo(t, seg), segt)

    g_ref[0] = _dot(_sigmoid(gd), g2_ref[...], "bf16").astype(g_ref.dtype)
    kk = k * kk_w_ref[...]
    kk = kk * lax.rsqrt(head_sum(kk * kk) + 1e-12)
    wl = w0_ref[...] + _dot(jnp.tanh(wd), w2_ref[...], "bf16")
    al = a0_ref[...] + _dot(ad, a2_ref[...], "bf16")
    r_ref[0] = r.astype(r_ref.dtype)
    v_ref[0] = v.astype(v_ref.dtype)
    kk_ref[0] = kk.astype(kk_ref.dtype)
    k_sum = jnp.zeros_like(r)
    for d in range(2):
        a = _sigmoid(al[:, d * rw:(d + 1) * rw])
        k_d = k * (1.0 + (a - 1.0) * ka_ref[...])
        k_sum = k_sum + k_d
        lw_ref[d, 0] = -math.exp(-0.5) * _sigmoid(wl[:, d * rw:(d + 1) * rw])
        kd_ref[d, 0] = k_d.astype(kd_ref.dtype)
        be_ref[d, 0] = (kk * a).astype(be_ref.dtype)
    bonus_ref[0] = (head_sum(r * rk_ref[...] * k_sum) * v).astype(bonus_ref.dtype)


def _rw_prep(z_rw, bsz, l_ctx, l_lat, mu, w2bd, a2bd, g2, w0cat, a0cat, k_k, k_a, r_k_flat, seg, segt):
    cols = z_rw.shape[1]
    tm = l_ctx
    rw = g2.shape[1]
    l_all = l_ctx + l_lat
    nblk = l_all // tm
    lat_blk = l_lat // tm
    hb = tm // GRID_W
    lat_hblk = l_lat // GRID_W

    def main_blk(b, j):
        return jnp.where(j == 0, bsz * lat_blk + b, b * lat_blk + j - 1)

    def prev_halo(b, j):
        return b * lat_hblk + jnp.maximum((j - 1) * hb - 1, 0)

    def next_halo(b, j):
        return b * lat_hblk + jnp.minimum(jnp.maximum(j, 1) * hb, lat_hblk - 1)

    full = lambda shape: pl.BlockSpec(shape, lambda b, j: (0,) * len(shape))
    shared = jax.ShapeDtypeStruct((bsz, l_all, rw), BF16)
    lat_only = jax.ShapeDtypeStruct((bsz, l_lat, rw), BF16)
    per_dir = jax.ShapeDtypeStruct((2, bsz, l_all, rw), BF16)
    per_dir_f32 = jax.ShapeDtypeStruct((2, bsz, l_all, rw), F32)
    o_shared = pl.BlockSpec((1, tm, rw), lambda b, j: (b, j, 0))
    o_lat = pl.BlockSpec((1, tm, rw), lambda b, j: (b, jnp.maximum(j - 1, 0), 0))
    o_dir = pl.BlockSpec((2, 1, tm, rw), lambda b, j: (0, b, j, 0))
    return pl.pallas_call(
        functools.partial(_rw_prep_kernel, tm=tm, l_lat=l_lat, rw=rw),
        out_shape=(shared,) * 3 + (lat_only,) * 2 + (per_dir_f32, per_dir, per_dir),
        grid=(bsz, nblk),
        in_specs=[pl.BlockSpec((tm, cols), lambda b, j: (main_blk(b, j), 0)),
                  pl.BlockSpec((GRID_W, cols), lambda b, j: (prev_halo(b, j), 0)),
                  pl.BlockSpec((GRID_W, cols), lambda b, j: (next_halo(b, j), 0)),
                  full((1, cols)), full(w2bd.shape), full(a2bd.shape), full(g2.shape),
                  full((1, 2 * rw)), full((1, 2 * rw)), full((1, rw)), full((1, rw)), full((1, rw)),
                  full(seg.shape), full(segt.shape)],
        out_specs=(o_shared,) * 3 + (o_lat,) * 2 + (o_dir,) * 3,
        compiler_params=_cparams(("parallel", "arbitrary")),
        name="rwkv_prep",
    )(z_rw, z_rw, z_rw, mu.reshape(1, cols), w2bd, a2bd, g2, w0cat, a0cat,
      k_k.reshape(1, rw), k_a.reshape(1, rw), r_k_flat.reshape(1, rw), seg, segt)


def _stack_heads(x, head0):
    return jnp.concatenate([jnp.where(head0, x, 0.0), jnp.where(head0, 0.0, x)], axis=0)


def _rw_step_kernel(rf_ref, rb_ref, vf_ref, vb_ref, kkf_ref, kkb_ref, lwf_ref, lwb_ref, kdf_ref, kdb_ref,
                    bef_ref, beb_ref, yf_ref, yb_ref,
                    z_scr, gc_s, lhs_s, kb_s, kht_s, bht_s, vs_s, rt_s, aab_s, akr_s, arb_s, t_s, x_s, xin_s,
                    yc_s, wu_s, qm_s, yn_s, *, n_pairs, bsz):
    c_n = RW_CHUNK
    n2 = 2 * c_n

    lanes_of = lambda p: slice(p * LANES, (p + 1) * LANES)
    unit = lambda d, b, p: (d * bsz + b) * n_pairs + p
    units = [(d, b, p) for d in range(2) for b in range(bsz) for p in range(n_pairs)]

    @pl.when(pl.program_id(0) == 0)
    def _():
        z_scr[...] = jnp.zeros_like(z_scr)
        qm_s[...] = jnp.zeros_like(qm_s)
        yn_s[...] = jnp.zeros_like(yn_s)

    for d, b, p in units:
        u = unit(d, b, p)
        y_ref = yf_ref if d == 0 else yb_ref
        yz = _dot(qm_s[u], z_scr[u].astype(BF16)) + yn_s[u]
        y_ref[b, :, lanes_of(p)] = yz[:c_n] + yz[c_n:n2]
        z_scr[u] = yz[n2:]

    ri = lax.broadcasted_iota(jnp.int32, (c_n, c_n), 0)
    ci = lax.broadcasted_iota(jnp.int32, (c_n, c_n), 1)
    r2 = lax.broadcasted_iota(jnp.int32, (n2, n2), 0)
    c2 = lax.broadcasted_iota(jnp.int32, (n2, n2), 1)
    t2 = r2 % c_n
    i2 = c2 % c_n
    same_head = (r2 // c_n) == (c2 // c_n)
    diag = r2 == c2
    eye = diag.astype(F32)
    head0 = lax.broadcasted_iota(jnp.int32, (1, LANES), 1) < RW_HEAD
    tri = [(ri >= ci).astype(F32), (ri <= ci).astype(F32)]
    before = [jnp.logical_and(same_head, i2 < t2), jnp.logical_and(same_head, i2 > t2)]
    upto = [jnp.logical_or(m, diag) for m in before]

    def blk(s):
        return (r2 // s) == (c2 // s)

    srcs = [(rf_ref, vf_ref, kkf_ref, lwf_ref, kdf_ref, bef_ref), (rb_ref, vb_ref, kkb_ref, lwb_ref, kdb_ref, beb_ref)]

    for d in range(2):
        r_ref, v_ref, kk_ref, lw_ref, kd_ref, be_ref = srcs[d]
        for b in range(bsz):
            lw = lw_ref[0, b]
            cum = _dot(tri[d], lw, HIGHEST)
            tot = jnp.sum(lw, axis=0, keepdims=True)
            g_inv = jnp.exp(-cum)
            g_hat = jnp.exp(tot - cum)
            kd = kd_ref[0, b].astype(F32)
            be = be_ref[0, b].astype(F32)
            at = kk_ref[b].astype(F32) * jnp.exp(cum - lw)
            rt = r_ref[b].astype(F32) * jnp.exp(cum)
            kt = kd * g_inv
            bt = be * g_inv
            kh = kd * g_hat
            bh = be * g_hat
            gc_s[d * bsz + b] = jnp.exp(tot)
            v = v_ref[b].astype(F32)
            for p in range(n_pairs):
                u = unit(d, b, p)
                st = lambda x: _stack_heads(x[:, lanes_of(p)], head0)
                rt_p = st(rt)
                at_p = st(at).astype(BF16)
                lhs_s[u, :n2] = at_p
                lhs_s[u, n2:] = rt_p.astype(BF16)
                xin_s[u, :, :LANES] = at_p
                rt_s[u] = rt_p
                kb_s[u, :n2] = st(kt).astype(BF16)
                kb_s[u, n2:] = st(bt).astype(BF16)
                kht_s[u] = st(kh).T.astype(BF16)
                bht_s[u] = st(bh).T.astype(BF16)
                vs_s[u] = st(v).astype(BF16)

    for d, b, p in units:
        u = unit(d, b, p)
        g = _dot_nt(lhs_s[u], kb_s[u])
        a_ab = jnp.where(before[d], g[:n2, n2:], 0.0)
        akr_s[u, :n2] = jnp.where(before[d], g[:n2, :n2], 0.0).astype(BF16)
        akr_s[u, n2:] = jnp.where(upto[d], g[n2:, :n2], 0.0).astype(BF16)
        arb_s[u] = jnp.where(upto[d], g[n2:, n2:], 0.0).astype(BF16)
        aab_s[u] = a_ab
        t_s[u] = eye - jnp.where(blk(2), a_ab, 0.0)

    n_units = len(units)
    s = 2
    while s < c_n:
        off = jnp.logical_and(blk(2 * s), jnp.logical_not(blk(s)))
        for u in range(n_units):
            x_s[u] = _dot(t_s[u].astype(BF16), jnp.where(off, aab_s[u], 0.0).astype(BF16)).astype(BF16)
        for u in range(n_units):
            t = t_s[u]
            t_s[u] = t - _dot(x_s[u], t.astype(BF16))
        s *= 2

    for u in range(n_units):
        av = _dot(akr_s[u], vs_s[u])
        xin_s[u, :, LANES:] = av[:n2].astype(BF16)
        yc_s[u] = av[n2:]

    for u in range(n_units):
        wu_s[u] = _dot(t_s[u].astype(BF16), xin_s[u]).astype(BF16)
    for d, b, p in units:
        u = unit(d, b, p)
        wu = wu_s[u]
        q = _dot(arb_s[u], wu)
        bwu = _dot(bht_s[u], wu)
        qm_s[u, :n2] = (rt_s[u] - q[:, :LANES]).astype(BF16)
        qm_s[u, n2:] = (eye * gc_s[d * bsz + b][:, lanes_of(p)] - bwu[:, :LANES]).astype(BF16)
        yn_s[u, :n2] = yc_s[u] - q[:, LANES:]
        yn_s[u, n2:] = _dot(kht_s[u], vs_s[u]) - bwu[:, LANES:]


def _rw_scan(r, v, kk, lw, kd, be, l_ctx):
    bsz, l_all, rw = r.shape
    c_n = RW_CHUNK
    n_all = l_all // c_n
    n_ctx = l_ctx // c_n
    n_lat = n_all - n_ctx
    n_pairs = rw // LANES
    n_units = 2 * bsz * n_pairs
    n2 = 2 * c_n
    vm = lambda nr, cols, dt: pltpu.VMEM((n_units, nr, cols), dt)

    clamp = lambda s: jnp.minimum(s, n_all - 1)
    chunk_f = lambda s: clamp(s)
    chunk_b = lambda s: jnp.where(clamp(s) < n_ctx, n_ctx - 1 - clamp(s), n_all + n_ctx - 1 - clamp(s))
    prev = lambda s: jnp.maximum(s - 1, 0)
    out_f = lambda s: jnp.maximum(prev(s) - n_ctx, 0)
    out_b = lambda s: jnp.where(prev(s) < n_ctx, n_lat - 1, chunk_b(prev(s)) - n_ctx)

    sh_f = pl.BlockSpec((bsz, c_n, rw), lambda s: (0, chunk_f(s), 0))
    sh_b = pl.BlockSpec((bsz, c_n, rw), lambda s: (0, chunk_b(s), 0))
    pd_f = pl.BlockSpec((1, bsz, c_n, rw), lambda s: (0, 0, chunk_f(s), 0))
    pd_b = pl.BlockSpec((1, bsz, c_n, rw), lambda s: (1, 0, chunk_b(s), 0))
    y_shape = jax.ShapeDtypeStruct((bsz, n_lat * c_n, rw), F32)
    return pl.pallas_call(
        functools.partial(_rw_step_kernel, n_pairs=n_pairs, bsz=bsz),
        out_shape=(y_shape, y_shape),
        grid=(n_all + 1,),
        in_specs=[sh_f, sh_b, sh_f, sh_b, sh_f, sh_b, pd_f, pd_b, pd_f, pd_b, pd_f, pd_b],
        out_specs=(pl.BlockSpec((bsz, c_n, rw), lambda s: (0, out_f(s), 0)),
                   pl.BlockSpec((bsz, c_n, rw), lambda s: (0, out_b(s), 0))),
        scratch_shapes=[vm(n2, n2, F32),
                        pltpu.VMEM((2 * bsz, 1, rw), F32),
                        vm(2 * n2, n2, BF16),
                        vm(2 * n2, n2, BF16),
                        vm(n2, n2, BF16),
                        vm(n2, n2, BF16),
                        vm(n2, n2, BF16),
                        vm(n2, n2, F32),
                        vm(n2, n2, F32),
                        vm(2 * n2, n2, BF16),
                        vm(n2, n2, BF16),
                        vm(n2, n2, F32),
                        vm(n2, n2, BF16),
                        vm(n2, 2 * n2, BF16),
                        vm(n2, n2, F32),
                        vm(n2, 2 * n2, BF16),
                        vm(2 * n2, n2, BF16),
                        vm(2 * n2, n2, F32)],
        compiler_params=_cparams(("arbitrary",)),
        name="rwkv_scan",
    )(r, r, v, v, kk, kk, lw, lw, kd, kd, be, be)


def _s5_glu_kernel(y_ref, wa_ref, wb_ref, o_ref, h_scr):
    @pl.when(pl.program_id(1) == 0)
    def _():
        for jb in range(y_ref.shape[0]):
            h_scr[:, jb * LANES:(jb + 1) * LANES] = _gelu_tanh(y_ref[jb]).astype(BF16)

    h = h_scr[...]
    o_ref[...] = (_dot(h, wa_ref[...]) * _sigmoid(_dot(h, wb_ref[...]))).astype(o_ref.dtype)


def _s5_glu(y_blk, w, *, tm, tn):
    nb, m, _ = y_blk.shape
    k = nb * LANES
    n = w.shape[1] // 2
    nj = n // tn
    return pl.pallas_call(
        _s5_glu_kernel,
        out_shape=jax.ShapeDtypeStruct((m, n), BF16),
        grid=(m // tm, nj),
        in_specs=[pl.BlockSpec((nb, tm, LANES), lambda i, j: (0, i, 0)),
                  pl.BlockSpec((k, tn), lambda i, j: (0, j)),
                  pl.BlockSpec((k, tn), lambda i, j: (0, j + nj))],
        out_specs=pl.BlockSpec((tm, tn), lambda i, j: (i, j)),
        scratch_shapes=[pltpu.VMEM((tm, k), BF16)],
        compiler_params=_cparams(("parallel", "arbitrary")),
        name="s5_glu",
    )(y_blk, w, w)


def _merge_out_kernel(yf_ref, yb_ref, bonus_ref, g_ref, lnw_ref, lnb_ref, seg_ref, segt_ref,
                      ga_ref, gb_ref, s5_ref, wp_ref, wo_ref, x_ref, g1_ref, nw_ref, sh_ref, sc_ref,
                      h_ref, hn_ref):
    seg = seg_ref[...]
    segt = segt_ref[...]
    inv_n = 1.0 / RW_HEAD
    tm = yf_ref.shape[0]
    parts = [slice(0, tm // 2), slice(tm // 2, tm)]

    def head_mean(vals):
        sums = [_dot_hilo(v, seg) for v in vals]
        return [_dot_hilo(s, segt) * inv_n for s in sums]

    y = [yf_ref[p, :] + yb_ref[p, :] for p in parts]
    dy = [a - m for a, m in zip(y, head_mean(y))]
    var = head_mean([a * a for a in dy])
    y_rw = []
    for p, a, v in zip(parts, dy, var):
        t = a * lax.rsqrt(v + GN_EPS) * lnw_ref[...] + lnb_ref[...] + bonus_ref[p, :].astype(F32)
        y_rw.append((t * g_ref[p, :].astype(F32)).astype(BF16))
    rw_out = [_dot(a, wp_ref[...]) for a in y_rw]
    merged = [(ga_ref[p, :].astype(F32) * s5_ref[p, :].astype(F32) + gb_ref[p, :].astype(F32) * r).astype(BF16)
              for p, r in zip(parts, rw_out)]
    proj = [_dot(a, wo_ref[...]) for a in merged]
    for p, a in zip(parts, proj):
        h = x_ref[p, :] + g1_ref[0] * a
        h_ref[p, :] = h
        hn_ref[p, :] = _lnmod_rows(h, nw_ref[...], sh_ref[0], sc_ref[0]).astype(hn_ref.dtype)


def _merge_out(y_f, y_b, bonus, g, ln_w, ln_b, seg, segt, gates, s5_out, w_proj, w_o, x2, g_tab, nw,
               sh_tab, sc_tab, mod_row_of_block, *, tm):
    m, rw = y_f.shape
    n = w_proj.shape[1]
    full = lambda shape: pl.BlockSpec(shape, lambda i: (0,) * len(shape))
    rows = lambda width: pl.BlockSpec((tm, width), lambda i: (i, 0))
    mod_map = lambda i: (mod_row_of_block(i), 0, 0)
    return pl.pallas_call(
        _merge_out_kernel,
        out_shape=(jax.ShapeDtypeStruct((m, n), F32), jax.ShapeDtypeStruct((m, n), BF16)),
        grid=(m // tm,),
        in_specs=[rows(rw), rows(rw), rows(rw), rows(rw),
                  full((1, rw)), full((1, rw)), full(seg.shape), full(segt.shape),
                  pl.BlockSpec((tm, n), lambda i: (i, 0)), pl.BlockSpec((tm, n), lambda i: (i, 1)), rows(n),
                  full(w_proj.shape), full(w_o.shape), rows(n),
                  pl.BlockSpec((1, 1, n), mod_map), full((1, n)),
                  pl.BlockSpec((1, 1, n), mod_map), pl.BlockSpec((1, 1, n), mod_map)],
        out_specs=(rows(n), rows(n)),
        compiler_params=_cparams(("parallel",), vmem=VMEM_LIMIT + 8 * 1024 * 1024),
        name="merge_out_proj",
    )(y_f, y_b, bonus, g, ln_w.reshape(1, rw), ln_b.reshape(1, rw), seg, segt,
      gates, gates, s5_out, w_proj, w_o, x2, g_tab, nw.reshape(1, n), sh_tab, sc_tab)


def _rw_merge_kernel(yf_ref, yb_ref, bonus_ref, g_ref, lnw_ref, lnb_ref, seg_ref, segt_ref,
                     ga_ref, gb_ref, s5_ref, w_ref, o_ref, h_scr):
    @pl.when(pl.program_id(1) == 0)
    def _():
        seg = seg_ref[...]
        segt = segt_ref[...]
        inv_n = 1.0 / RW_HEAD

        def head_mean(t):
            return _dot_hilo(_dot_hilo(t, seg), segt) * inv_n

        y = yf_ref[...] + yb_ref[...]
        dy = y - head_mean(y)
        var = head_mean(dy * dy)
        y = dy * lax.rsqrt(var + GN_EPS) * lnw_ref[...] + lnb_ref[...] + bonus_ref[...].astype(F32)
        h_scr[...] = (y * g_ref[...].astype(F32)).astype(BF16)

    rw_out = _dot(h_scr[...], w_ref[...])
    merged = ga_ref[...].astype(F32) * s5_ref[...].astype(F32) + gb_ref[...].astype(F32) * rw_out
    o_ref[...] = merged.astype(o_ref.dtype)


def _rw_merge(y_f, y_b, bonus, g, ln_w, ln_b, seg, segt, gates, s5_out, w_proj, *, tm, tn):
    m, rw = y_f.shape
    n = w_proj.shape[1]
    nj = n // tn
    full = lambda shape: pl.BlockSpec(shape, lambda i, j: (0,) * len(shape))
    return pl.pallas_call(
        _rw_merge_kernel,
        out_shape=jax.ShapeDtypeStruct((m, n), BF16),
        grid=(m // tm, nj),
        in_specs=[pl.BlockSpec((tm, rw), lambda i, j: (i, 0)),
                  pl.BlockSpec((tm, rw), lambda i, j: (i, 0)),
                  pl.BlockSpec((tm, rw), lambda i, j: (i, 0)),
                  pl.BlockSpec((tm, rw), lambda i, j: (i, 0)),
                  full((1, rw)), full((1, rw)), full(seg.shape), full(segt.shape),
                  pl.BlockSpec((tm, tn), lambda i, j: (i, j)),
                  pl.BlockSpec((tm, tn), lambda i, j: (i, j + nj)),
                  pl.BlockSpec((tm, tn), lambda i, j: (i, j)),
                  pl.BlockSpec((rw, tn), lambda i, j: (0, j))],
        out_specs=pl.BlockSpec((tm, tn), lambda i, j: (i, j)),
        scratch_shapes=[pltpu.VMEM((tm, rw), BF16)],
        compiler_params=_cparams(("parallel", "arbitrary")),
        name="rwkv_merge",
    )(y_f, y_b, bonus, g, ln_w.reshape(1, rw), ln_b.reshape(1, rw), seg, segt,
      gates, gates, s5_out, w_proj)


def _resid_ln_kernel(a_ref, w_ref, x_ref, g_ref, nw_ref, sh_ref, sc_ref, h_ref, hn_ref):
    h = x_ref[...] + g_ref[0] * _dot(a_ref[...], w_ref[...])
    h_ref[...] = h
    hn_ref[...] = _lnmod_rows(h, nw_ref[...], sh_ref[0], sc_ref[0]).astype(hn_ref.dtype)


def _resid_matmul_ln(a, w, x2, g_tab, nw, sh_tab, sc_tab, mod_row_of_block, *, tm):
    m, k = a.shape
    n = w.shape[1]
    mod_map = lambda i: (mod_row_of_block(i), 0, 0)
    return pl.pallas_call(
        _resid_ln_kernel,
        out_shape=(jax.ShapeDtypeStruct((m, n), F32), jax.ShapeDtypeStruct((m, n), BF16)),
        grid=(m // tm,),
        in_specs=[pl.BlockSpec((tm, k), lambda i: (i, 0)),
                  pl.BlockSpec((k, n), lambda i: (0, 0)),
                  pl.BlockSpec((tm, n), lambda i: (i, 0)),
                  pl.BlockSpec((1, 1, n), mod_map),
                  pl.BlockSpec((1, n), lambda i: (0, 0)),
                  pl.BlockSpec((1, 1, n), mod_map),
                  pl.BlockSpec((1, 1, n), mod_map)],
        out_specs=(pl.BlockSpec((tm, n), lambda i: (i, 0)), pl.BlockSpec((tm, n), lambda i: (i, 0))),
        compiler_params=_cparams(("parallel",)),
        name="out_proj",
    )(a, w, x2, g_tab, nw.reshape(1, n), sh_tab, sc_tab)


def _wres_swiglu_kernel(a_ref, w1_ref, w3_ref, o_ref, w1_scr, w3_scr):
    @pl.when(pl.program_id(1) == 0)
    def _():
        w1_scr[...] = w1_ref[...].astype(BF16)
        w3_scr[...] = w3_ref[...].astype(BF16)

    a = a_ref[...]
    o_ref[...] = (_silu(_dot(a, w1_scr[...])) * _dot(a, w3_scr[...])).astype(o_ref.dtype)


def _wres_swiglu(a, w13, d_ff, *, tm, tn):
    m, k = a.shape
    nj = d_ff // tn
    return pl.pallas_call(
        _wres_swiglu_kernel,
        out_shape=jax.ShapeDtypeStruct((m, d_ff), BF16),
        grid=(nj, m // tm),
        in_specs=[pl.BlockSpec((tm, k), lambda j, i: (i, 0)),
                  pl.BlockSpec((k, tn), lambda j, i: (0, j)),
                  pl.BlockSpec((k, tn), lambda j, i: (0, j + nj))],
        out_specs=pl.BlockSpec((tm, tn), lambda j, i: (i, j)),
        scratch_shapes=[pltpu.VMEM((k, tn), BF16), pltpu.VMEM((k, tn), BF16)],
        compiler_params=_cparams(("arbitrary", "arbitrary")),
        name="ffn_up",
    )(a, w13, w13)


def _ffn_down_kernel(a_ref, w_ref, x_ref, g_ref, nf_ref, o_ref):
    kk = pl.program_id(1)
    part = _dot(a_ref[...], w_ref[...])

    @pl.when(kk == 0)
    def _():
        o_ref[...] = part

    @pl.when(jnp.logical_and(kk > 0, kk < pl.num_programs(1) - 1))
    def _():
        o_ref[...] += part

    @pl.when(kk == pl.num_programs(1) - 1)
    def _():
        h = x_ref[...] + g_ref[0] * (o_ref[...] + part)
        ms = jnp.mean(h * h, axis=-1, keepdims=True)
        o_ref[...] = h * lax.rsqrt(ms + NORM_EPS) * nf_ref[...]


def _ffn_down(a, w, x2, g_tab, mod_row_of_block, norm_f, *, tm, tk):
    m, k = a.shape
    n = w.shape[1]
    return pl.pallas_call(
        _ffn_down_kernel,
        out_shape=jax.ShapeDtypeStruct((m, n), F32),
        grid=(m // tm, k // tk),
        in_specs=[pl.BlockSpec((tm, tk), lambda i, kk: (i, kk)),
                  pl.BlockSpec((tk, n), lambda i, kk: (kk, 0)),
                  pl.BlockSpec((tm, n), lambda i, kk: (i, 0)),
                  pl.BlockSpec((1, 1, n), lambda i, kk: (mod_row_of_block(i), 0, 0)),
                  pl.BlockSpec((1, n), lambda i, kk: (0, 0))],
        out_specs=pl.BlockSpec((tm, n), lambda i, kk: (i, 0)),
        compiler_params=_cparams(("parallel", "arbitrary"), vmem=VMEM_LIMIT + 8 * 1024 * 1024),
        name="ffn_down",
    )(a, w, x2, g_tab, norm_f.reshape(1, n))


def kernel(x, c, ctx, c_ctx, ada_w, ada_b, norm1_w, w_in, rw_mu, s5_a_re, s5_a_im, s5_log_dt, s5_b_re, s5_b_im, s5_c_re, s5_c_im, s5_d, s5_glu_w, rw_w0, rw_w2, rw_a0, rw_a2, rw_g2, rw_k_k, rw_k_a, rw_r_k, rw_ln_w, rw_ln_b, rw_proj, w_o, norm2_w, ffn_w13, ffn_w2, norm_f):
    assert ada_w.shape[0] == 1, "single-layer block"
    bsz, l_lat, d = x.shape
    l_ctx = ctx.shape[1]
    l_all = l_ctx + l_lat
    s5w = s5_d.shape[1] * s5_d.shape[2]
    rw = rw_g2.shape[2]
    shift_cols = rw_mu.shape[1]
    d_ff = ffn_w2.shape[1]

    c_rows = jnp.concatenate([c, c_ctx[None], jnp.zeros((8 - bsz - 1, d), F32)], axis=0)
    mod = _modulation(c_rows, ada_w[0], ada_b[0])
    tab = lambda k: mod[:, k * d:(k + 1) * d].reshape(8, 1, d)
    sh1, sc1, g1, sh2, sc2, g2 = (tab(k) for k in range(N_MOD))
    ctx_row = bsz

    tm_in = 512
    m_lat = bsz * l_lat
    x2 = x.reshape(m_lat, d)
    n_mix = s5w + shift_cols
    m_all = m_lat + bsz * l_ctx
    lat_row = lambda t: (lambda i: i // (l_lat // t))

    def mix_mod_row(i):
        return jnp.where(i >= m_lat // tm_in, ctx_row, i // (l_lat // tm_in))

    h_all, u_blk = _lnmod_proj(x2, ctx.reshape(bsz * l_ctx, d), norm1_w[0], sh1, sc1, mix_mod_row,
                               w_in[0], s5w, tm=tm_in)
    tm_all = m_all // 8
    z_rw = _wres_matmul(h_all, w_in[0], s5w, shift_cols, rows=m_all, tm=tm_all, tn=shift_cols // 3,
                        out_dtype=BF16, name="in_proj_rw")
    tm = 1024
    gates = _wres_matmul(h_all, w_in[0], n_mix, w_in.shape[2] - n_mix, rows=m_lat, tm=tm, tn=1024,
                         out_dtype=BF16, epilogue="sigmoid", name="in_proj_gates")

    y_blk = _s5_branch(u_blk, bsz, l_ctx, l_lat, s5_a_re[0], s5_a_im[0], s5_log_dt[0],
                               s5_b_re[0], s5_b_im[0], s5_c_re[0], s5_c_im[0], s5_d[0])
    s5_out = _s5_glu(y_blk, s5_glu_w[0].astype(BF16), tm=tm, tn=1024)

    lora = rw_w2.shape[2]
    zl = jnp.zeros((lora, rw), F32)
    w2bd = jnp.concatenate([jnp.concatenate([rw_w2[0, 0], zl], axis=1),
                            jnp.concatenate([zl, rw_w2[0, 1]], axis=1)], axis=0)
    a2bd = jnp.concatenate([jnp.concatenate([rw_a2[0, 0], zl], axis=1),
                            jnp.concatenate([zl, rw_a2[0, 1]], axis=1)], axis=0)
    head_of = jnp.arange(rw) // RW_HEAD
    seg = (head_of[:, None] == jnp.arange(LANES)[None, :]).astype(BF16)
    segt = seg.T
    r, v, kk, g, bonus, lw, kd, be = _rw_prep(
        z_rw, bsz, l_ctx, l_lat, rw_mu[0], w2bd, a2bd, rw_g2[0], rw_w0[0].reshape(1, 2 * rw),
        rw_a0[0].reshape(1, 2 * rw), rw_k_k[0], rw_k_a[0], rw_r_k[0].reshape(rw), seg, segt)
    y_f, y_b = _rw_scan(r, v, kk, lw, kd, be, l_ctx)

    tm_o = 256
    h1, h1n = _merge_out(y_f.reshape(m_lat, rw), y_b.reshape(m_lat, rw), bonus.reshape(m_lat, rw),
                         g.reshape(m_lat, rw), rw_ln_w[0], rw_ln_b[0], seg, segt, gates, s5_out,
                         rw_proj[0].astype(BF16), w_o[0].astype(BF16), x2, g1, norm2_w[0], sh2, sc2,
                         lat_row(tm_o), tm=tm_o)

    act = _wres_swiglu(h1n, ffn_w13[0], d_ff, tm=tm, tn=512)
    tm_dn = 512
    out = _ffn_down(act, ffn_w2[0].astype(BF16), h1, g2, lat_row(tm_dn), norm_f, tm=tm_dn, tk=d_ff // 2)
    return out.reshape(bsz, l_lat, d)
```

```python
import functools
import math

import jax
import jax.numpy as jnp
from jax import lax
from jax.experimental import pallas as pl
from jax.experimental.pallas import tpu as pltpu

F32 = jnp.float32
BF16 = jnp.bfloat16
HIGHEST = lax.Precision.HIGHEST

N_MOD = 6
NORM_EPS = 1e-6
GN_EPS = 64e-5
GRID_W = 64
S5_GROUP = 16
S5_STATE = 64
S5_CHUNK = 16
RW_HEAD = 64
RW_CHUNK = 64
LANES = 128
VMEM_LIMIT = 48 * 1024 * 1024


def _cparams(sem, vmem=VMEM_LIMIT):
    return pltpu.CompilerParams(dimension_semantics=sem, vmem_limit_bytes=vmem)


def _operands(a, b, precision):
    if precision == "bf16":
        return a.astype(BF16), b.astype(BF16), None
    return a, b, precision


def _dot(a, b, precision=None):
    a, b, precision = _operands(a, b, precision)
    return jnp.dot(a, b, preferred_element_type=F32, precision=precision)


def _dot_nt(a, b, precision=None):
    a, b, precision = _operands(a, b, precision)
    return lax.dot_general(a, b, (((1,), (1,)), ((), ())), preferred_element_type=F32, precision=precision)


def _dot_hilo(a, ind):
    hi = a.astype(BF16)
    lo = (a - hi.astype(F32)).astype(BF16)
    return _dot(hi, ind) + _dot(lo, ind)


def _sigmoid(x):
    return 1.0 / (1.0 + jnp.exp(-x))


def _silu(x):
    return x * _sigmoid(x)


def _gelu_tanh(x):
    c = math.sqrt(2.0 / math.pi)
    return 0.5 * x * (1.0 + jnp.tanh(c * (x + 0.044715 * (x * x * x))))


def _mod_kernel(c_ref, w_ref, b_ref, o_ref):
    o_ref[...] = _dot_hilo(_silu(c_ref[...]), w_ref[...].astype(BF16)) + b_ref[...]


def _modulation(c_rows, ada_w, ada_b):
    m, d = c_rows.shape
    n = ada_w.shape[1]
    tn = 1024
    return pl.pallas_call(
        _mod_kernel,
        out_shape=jax.ShapeDtypeStruct((m, n), F32),
        grid=(n // tn,),
        in_specs=[pl.BlockSpec((m, d), lambda j: (0, 0)),
                  pl.BlockSpec((d, tn), lambda j: (0, j)),
                  pl.BlockSpec((1, tn), lambda j: (0, j))],
        out_specs=pl.BlockSpec((m, tn), lambda j: (0, j)),
        compiler_params=_cparams(("arbitrary",)),
        name="modulation",
    )(c_rows, ada_w, ada_b.reshape(1, n))


def _lnmod_rows(x, nw, sh, sc):
    ms = jnp.mean(x * x, axis=-1, keepdims=True)
    y = x * lax.rsqrt(ms + NORM_EPS) * nw
    return y * (1.0 + sc) + sh


def _lnmod_proj_kernel(x_ref, c_ref, nw_ref, sh_ref, sc_ref, w_ref, h_ref, u_ref, w_scr, *, n_lat_blocks):
    @pl.when(pl.program_id(0) == 0)
    def _():
        w_scr[...] = w_ref[...].astype(BF16)

    rows = jnp.where(pl.program_id(0) < n_lat_blocks, x_ref[...], c_ref[...])
    h = _lnmod_rows(rows, nw_ref[...], sh_ref[0], sc_ref[0]).astype(BF16)
    h_ref[...] = h
    z = _dot(h, w_scr[...])
    for jb in range(u_ref.shape[0]):
        u_ref[jb] = z[:, jb * LANES:(jb + 1) * LANES]


def _lnmod_proj(x2, c2, nw, sh_tab, sc_tab, mod_row_of_block, w, n, *, tm):
    m, d = x2.shape
    mc = c2.shape[0]
    nx, nc = m // tm, mc // tm
    mod_map = lambda i: (mod_row_of_block(i), 0, 0)
    return pl.pallas_call(
        functools.partial(_lnmod_proj_kernel, n_lat_blocks=nx),
        out_shape=(jax.ShapeDtypeStruct((m + mc, d), BF16),
                   jax.ShapeDtypeStruct((n // LANES, m + mc, LANES), F32)),
        grid=(nx + nc,),
        in_specs=[pl.BlockSpec((tm, d), lambda i: (jnp.minimum(i, nx - 1), 0)),
                  pl.BlockSpec((tm, d), lambda i: (jnp.maximum(i - nx, 0), 0)),
                  pl.BlockSpec((1, d), lambda i: (0, 0)),
                  pl.BlockSpec((1, 1, d), mod_map),
                  pl.BlockSpec((1, 1, d), mod_map),
                  pl.BlockSpec((d, n), lambda i: (0, 0))],
        out_specs=(pl.BlockSpec((tm, d), lambda i: (i, 0)),
                   pl.BlockSpec((n // LANES, tm, LANES), lambda i: (0, i, 0))),
        scratch_shapes=[pltpu.VMEM((d, n), BF16)],
        compiler_params=_cparams(("arbitrary",)),
        name="lnmod_in_proj_s5",
    )(x2, c2, nw.reshape(1, d), sh_tab, sc_tab, w)


def _wres_mm_kernel(a_ref, w_ref, o_ref, w_scr, *, epilogue):
    @pl.when(pl.program_id(1) == 0)
    def _():
        w_scr[...] = w_ref[...].astype(BF16)

    z = _dot(a_ref[...], w_scr[...])
    if epilogue == "sigmoid":
        z = _sigmoid(z)
    o_ref[...] = z.astype(o_ref.dtype)


def _wres_matmul(a, w, col0, n, *, rows, tm, tn, out_dtype, epilogue=None, name):
    k = a.shape[1]
    assert col0 % LANES == 0 and n % tn == 0 and rows % tm == 0
    return pl.pallas_call(
        functools.partial(_wres_mm_kernel, epilogue=epilogue),
        out_shape=jax.ShapeDtypeStruct((rows, n), out_dtype),
        grid=(n // tn, rows // tm),
        in_specs=[pl.BlockSpec((tm, k), lambda j, i: (i, 0)),
                  pl.BlockSpec((pl.Element(k), pl.Element(tn)), lambda j, i: (0, pl.multiple_of(col0 + j * tn, LANES)))],
        out_specs=pl.BlockSpec((tm, tn), lambda j, i: (i, j)),
        scratch_shapes=[pltpu.VMEM((k, tn), BF16)],
        compiler_params=_cparams(("arbitrary", "arbitrary")),
        name=name,
    )(a, w)


def _s5_param_kernel(are_ref, aim_ref, ldt_ref, bre_ref, bim_ref, cre_ref, cim_ref,
                      e_ref, c_ref, m_ref, a16_ref):
    t_n, hg, p_n = S5_CHUNK, S5_GROUP, S5_STATE
    gl_n = are_ref.shape[1]
    tau = lax.broadcasted_iota(jnp.int32, (t_n, gl_n, p_n), 0).astype(F32)
    taps = {}
    for d in range(2):
        a_re = are_ref[d]
        a_im = aim_ref[d]
        dt = jnp.exp(ldt_ref[d])
        lam = a_re * dt
        th = a_im * dt
        er = jnp.exp(lam)
        ab_re = er * jnp.cos(th)
        ab_im = er * jnp.sin(th)
        den = a_re * a_re + a_im * a_im
        x_re = ab_re - 1.0
        co_re = (x_re * a_re + ab_im * a_im) / den
        co_im = (ab_im * a_re - x_re * a_im) / den

        def power(tv):
            mag = jnp.exp(tv * lam)
            return mag * jnp.cos(tv * th), mag * jnp.sin(tv * th)

        pw_re, pw_im = power(tau if d == 0 else (t_n - 1.0 - tau))
        pe_re, pe_im = power((t_n - 1.0 - tau) if d == 0 else tau)
        pc_re, pc_im = power((tau + 1.0) if d == 0 else (t_n - tau))
        mag16 = jnp.exp(float(t_n) * lam)
        a16_ref[d, 0] = mag16 * jnp.cos(float(t_n) * th)
        a16_ref[d, 1] = mag16 * jnp.sin(float(t_n) * th)

        lo, hi = 2 * d * p_n, (2 * d + 1) * p_n
        for gl in range(gl_n):
            of = lambda x: x[:, gl:gl + 1, :]
            bt_re = bre_ref[d, gl].T
            bt_im = bim_ref[d, gl].T
            bb_re = co_re[gl:gl + 1] * bt_re - co_im[gl:gl + 1] * bt_im
            bb_im = co_re[gl:gl + 1] * bt_im + co_im[gl:gl + 1] * bt_re
            c_re = cre_ref[d, gl]
            c_im = cim_ref[d, gl]
            cp_re = (c_re[None] * of(pw_re) - c_im[None] * of(pw_im)).reshape(t_n * hg, p_n)
            cp_im = (c_re[None] * of(pw_im) + c_im[None] * of(pw_re)).reshape(t_n * hg, p_n)
            taps[d, gl] = _dot_nt(bb_re, cp_re, HIGHEST) - _dot_nt(bb_im, cp_im, HIGHEST)
            e_ref[0, :, gl, :, lo:hi] = of(pe_re) * bb_re[None] - of(pe_im) * bb_im[None]
            e_ref[0, :, gl, :, hi:hi + p_n] = of(pe_re) * bb_im[None] + of(pe_im) * bb_re[None]
            c_ref[0, :, gl, :, lo:hi] = c_re[None] * of(pc_re) - c_im[None] * of(pc_im)
            c_ref[0, :, gl, :, hi:hi + p_n] = -(c_re[None] * of(pc_im) + c_im[None] * of(pc_re))

    width = t_n * hg
    lane = lax.broadcasted_iota(jnp.int32, (hg, width), 1)
    for gl in range(gl_n):
        tf, tb = taps[0, gl], taps[1, gl]
        for t in range(t_n):
            sf = t * hg
            sb = (t_n - 1 - t) * hg
            f = tf if sf == 0 else jnp.where(lane >= sf, pltpu.roll(tf, sf, 1), 0.0)
            b = tb if sb == 0 else jnp.where(lane < width - sb, pltpu.roll(tb, width - sb, 1), 0.0)
            m_ref[0, t, gl] = f + b


def _s5_params(a_re, a_im, log_dt, b_re, b_im, c_re, c_im, nb):
    g_n = a_re.shape[1]
    gl_n = g_n // nb
    p_n, hg, t_n = S5_STATE, S5_GROUP, S5_CHUNK
    ldt = jnp.broadcast_to(log_dt[:, :, None], (2, g_n, p_n))
    spec3 = pl.BlockSpec((2, gl_n, p_n), lambda j: (0, j, 0))
    spec4 = pl.BlockSpec((2, gl_n, hg, p_n), lambda j: (0, j, 0, 0))
    spec_b = pl.BlockSpec((2, gl_n, p_n, hg), lambda j: (0, j, 0, 0))
    wide = 4 * p_n
    comp = jax.ShapeDtypeStruct((nb, t_n, gl_n, hg, wide), F32)
    comp_spec = pl.BlockSpec((1, t_n, gl_n, hg, wide), lambda j: (j, 0, 0, 0, 0))
    return pl.pallas_call(
        _s5_param_kernel,
        out_shape=(comp, comp, comp, jax.ShapeDtypeStruct((2, 2, g_n, p_n), F32)),
        grid=(nb,),
        in_specs=[spec3, spec3, spec3, spec_b, spec_b, spec4, spec4],
        out_specs=(comp_spec, comp_spec, comp_spec, pl.BlockSpec((2, 2, gl_n, p_n), lambda j: (0, 0, j, 0))),
        compiler_params=_cparams(("parallel",)),
        name="s5_params",
    )(a_re, a_im, ldt, b_re, b_im, c_re, c_im)


def _expand_block_diag(comp, rep_ref, mask_ref, w_scr):
    k = w_scr.shape[0]
    period = mask_ref.shape[1]
    cb = comp.astype(BF16)
    step = 512
    for c0 in range(0, k, step):
        blk = _dot(cb, rep_ref[:, c0:c0 + step]).astype(BF16)
        for q0 in range(0, step, period):
            w_scr[:, c0 + q0:c0 + q0 + period] = blk[:, q0:q0 + period] * mask_ref[...]


def _gather_chunk_rows(u_ref, rows, dst):
    t_n = S5_CHUNK
    for t in range(t_n):
        dst[:, t * LANES:(t + 1) * LANES] = u_ref[0, pl.ds(t, rows, stride=t_n), :].astype(dst.dtype)


def _s5_ein_kernel(u_ref, ec_ref, rep_ref, mask_ref, o_ref, w_scr, u_scr):
    _expand_block_diag(ec_ref[0], rep_ref, mask_ref, w_scr)
    _gather_chunk_rows(u_ref, u_scr.shape[0], u_scr)
    o_ref[0] = _dot(u_scr[...], w_scr[...])


def _s5_chunk_inputs(u_blk, e_comp, rep_e, mask_e):
    nb, m_all, _ = u_blk.shape
    k, cw = e_comp.shape[1:]
    rows = m_all // S5_CHUNK
    return pl.pallas_call(
        _s5_ein_kernel,
        out_shape=jax.ShapeDtypeStruct((nb, rows, k), F32),
        grid=(nb,),
        in_specs=[pl.BlockSpec((1, m_all, LANES), lambda j: (j, 0, 0)),
                  pl.BlockSpec((1, k, cw), lambda j: (j, 0, 0)),
                  pl.BlockSpec(rep_e.shape, lambda j: (0, 0)),
                  pl.BlockSpec(mask_e.shape, lambda j: (0, 0))],
        out_specs=pl.BlockSpec((1, rows, k), lambda j: (j, 0, 0)),
        scratch_shapes=[pltpu.VMEM((k, k), BF16), pltpu.VMEM((rows, k), BF16)],
        compiler_params=_cparams(("parallel",)),
        name="s5_chunk_inputs",
    )(u_blk, e_comp, rep_e, mask_e)


def _s5_bscan_kernel(e_ref, a_ref, o_ref, *, bsz, n_ctx, n_lat):
    q = e_ref.shape[2] // 4
    planes = lambda row, d: (row[:, (2 * d) * q:(2 * d + 1) * q], row[:, (2 * d + 1) * q:(2 * d + 2) * q])
    coef = [planes(a_ref[0], d) for d in range(2)]
    ctx0 = bsz * n_lat

    def advance(state, rows):
        new = []
        for (sr, si), (b, d), row in zip(state, [(b, d) for b in range(bsz) for d in range(2)], rows):
            ar, ai = coef[d]
            er, ei = planes(e_ref[0, pl.ds(row, 1), :], d)
            new.append((ar * sr - ai * si + er, ar * si + ai * sr + ei))
        return tuple(new)

    def ctx_step(s, state):
        rows = [ctx0 + b * n_ctx + (s if d == 0 else n_ctx - 1 - s) for b in range(bsz) for d in range(2)]
        return advance(state, rows)

    def lat_step(s, state):
        rows = [b * n_lat + (s if d == 0 else n_lat - 1 - s) for b in range(bsz) for d in range(2)]
        for (sr, si), (b, d), row in zip(state, [(b, d) for b in range(bsz) for d in range(2)], rows):
            o_ref[0, pl.ds(row, 1), (2 * d) * q:(2 * d + 1) * q] = sr
            o_ref[0, pl.ds(row, 1), (2 * d + 1) * q:(2 * d + 2) * q] = si
        return advance(state, rows)

    zero = jnp.zeros((1, q), F32)
    state = tuple((zero, zero) for _ in range(2 * bsz))
    state = lax.fori_loop(0, n_ctx, ctx_step, state)
    lax.fori_loop(0, n_lat, lat_step, state)


def _s5_bscan(e_rows, a_rows, bsz, n_ctx, n_lat):
    nb, rows, k = e_rows.shape
    return pl.pallas_call(
        functools.partial(_s5_bscan_kernel, bsz=bsz, n_ctx=n_ctx, n_lat=n_lat),
        out_shape=jax.ShapeDtypeStruct((nb, bsz * n_lat, k), F32),
        grid=(nb,),
        in_specs=[pl.BlockSpec((1, rows, k), lambda j: (j, 0, 0)),
                  pl.BlockSpec((1, 1, k), lambda j: (j, 0, 0))],
        out_specs=pl.BlockSpec((1, bsz * n_lat, k), lambda j: (j, 0, 0)),
        compiler_params=_cparams(("parallel",)),
        name="s5_scan",
    )(e_rows, a_rows)


def _s5_out_kernel(u_ref, s_ref, mc_ref, cc_ref, d_ref, rep_m_ref, mask_m_ref, rep_e_ref, mask_e_ref,
                   o_ref, wm_scr, wc_scr, u_scr):
    @pl.when(pl.program_id(1) == 0)
    def _():
        _expand_block_diag(mc_ref[0], rep_m_ref, mask_m_ref, wm_scr)
        _expand_block_diag(cc_ref[0], rep_e_ref, mask_e_ref, wc_scr)

    t_n = S5_CHUNK
    rows = s_ref.shape[1]
    _gather_chunk_rows(u_ref, rows, u_scr)
    u = u_scr[...]
    y = _dot(u.astype(BF16), wm_scr[...]) + _dot_nt(s_ref[0].astype(BF16), wc_scr[...])
    y = y + u * d_ref[0]
    for t in range(t_n):
        o_ref[0, pl.ds(t, rows, stride=t_n), :] = y[:, t * LANES:(t + 1) * LANES]


def _s5_outputs(u_blk, s_rows, m_comp, c_comp, d_rows, rep_m, mask_m, rep_e, mask_e):
    nb = u_blk.shape[0]
    rows, k = s_rows.shape[1:]
    cw = m_comp.shape[2]
    tr = rows // 2
    tok = tr * S5_CHUNK
    const = lambda a: pl.BlockSpec(a.shape, lambda j, i: (0, 0))
    return pl.pallas_call(
        _s5_out_kernel,
        out_shape=jax.ShapeDtypeStruct((nb, rows * S5_CHUNK, LANES), F32),
        grid=(nb, rows // tr),
        in_specs=[pl.BlockSpec((1, tok, LANES), lambda j, i: (j, i, 0)),
                  pl.BlockSpec((1, tr, k), lambda j, i: (j, i, 0)),
                  pl.BlockSpec((1, k, cw), lambda j, i: (j, 0, 0)),
                  pl.BlockSpec((1, k, cw), lambda j, i: (j, 0, 0)),
                  pl.BlockSpec((1, 1, k), lambda j, i: (j, 0, 0)),
                  const(rep_m), const(mask_m), const(rep_e), const(mask_e)],
        out_specs=pl.BlockSpec((1, tok, LANES), lambda j, i: (j, i, 0)),
        scratch_shapes=[pltpu.VMEM((k, k), BF16), pltpu.VMEM((k, k), BF16), pltpu.VMEM((tr, k), F32)],
        compiler_params=_cparams(("parallel", "arbitrary")),
        name="s5_outputs",
    )(u_blk, s_rows, m_comp, c_comp, d_rows, rep_m, mask_m, rep_e, mask_e)


def _s5_branch(u_blk, bsz, l_ctx, l_lat, a_re, a_im, log_dt, b_re, b_im, c_re, c_im, s5_d):
    nb, m_all, _ = u_blk.shape
    hg, t_n, p_n = S5_GROUP, S5_CHUNK, S5_STATE
    g_n = a_re.shape[1]
    gl_n = g_n // nb
    k = t_n * LANES
    n_lat = l_lat // t_n
    n_ctx = l_ctx // t_n
    e_c, c_c, m_c, a16 = _s5_params(a_re, a_im, log_dt, b_re, b_im, c_re, c_im, nb)
    cw = 4 * p_n
    e_comp, c_comp, m_comp = (a.reshape(nb, k, cw) for a in (e_c, c_c, m_c))

    row_gl = (jnp.arange(k) // hg) % gl_n
    col = jnp.arange(k)
    src = jnp.arange(cw)
    rep_e = ((src[:, None] // p_n == col[None, :] // (gl_n * p_n)) & (src[:, None] % p_n == col[None, :] % p_n)).astype(BF16)
    mask_e = (row_gl[:, None] == (jnp.arange(gl_n * p_n)[None, :] // p_n)).astype(BF16)
    rep_m = ((src[:, None] // hg == col[None, :] // (gl_n * hg)) & (src[:, None] % hg == col[None, :] % hg)).astype(BF16)
    mask_m = (row_gl[:, None] == (jnp.arange(gl_n * hg)[None, :] // hg)).astype(BF16)

    d_rows = jnp.tile(s5_d.reshape(nb, 1, gl_n * hg), (1, 1, t_n))
    a_rows = jnp.transpose(a16.reshape(2, 2, nb, gl_n * p_n), (2, 0, 1, 3)).reshape(nb, 1, 4 * gl_n * p_n)

    e_rows = _s5_chunk_inputs(u_blk, e_comp, rep_e, mask_e)
    s_rows = _s5_bscan(e_rows, a_rows, bsz, n_ctx, n_lat)
    return _s5_outputs(u_blk, s_rows, m_comp, c_comp, d_rows, rep_m, mask_m, rep_e, mask_e)


def _rw_prep_kernel(z_ref, zp_ref, zn_ref, mu_ref, w2_ref, a2_ref, g2_ref, w0_ref, a0_ref,
                    kk_w_ref, ka_ref, rk_ref, seg_ref, segt_ref,
                    r_ref, v_ref, kk_ref, g_ref, bonus_ref, lw_ref, kd_ref, be_ref,
                    *, tm, l_lat, rw):
    j = pl.program_id(1)
    z = z_ref[...].astype(F32)
    lat = j > 0
    tl = lax.broadcasted_iota(jnp.int32, (tm, 1), 0)
    tok = (j - 1) * tm + tl
    col = tl % GRID_W
    m_l = jnp.where(lat, col, tl) > 0
    m_r = jnp.where(lat, col - (GRID_W - 1), tl - (tm - 1)) < 0
    m_u = jnp.logical_and(lat, tok >= GRID_W)
    m_d = jnp.logical_and(lat, tok < l_lat - GRID_W)
    z_ext = jnp.concatenate([zp_ref[...], z_ref[...], zn_ref[...]], axis=0)
    rel = (lax.broadcasted_iota(jnp.int32, (tm, tm + 2 * GRID_W), 1) - GRID_W
           - lax.broadcasted_iota(jnp.int32, (tm, tm + 2 * GRID_W), 0))
    pick = (jnp.logical_and(rel == -1, m_l) | jnp.logical_and(rel == 1, m_r)
            | jnp.logical_and(rel == -GRID_W, m_u) | jnp.logical_and(rel == GRID_W, m_d))
    s = _dot(jnp.where(pick, 1.0, 0.0).astype(z_ext.dtype), z_ext)
    cnt = (m_l.astype(F32) + m_r.astype(F32)) + (m_u.astype(F32) + m_d.astype(F32))
    zs = z + (s * (1.0 / cnt) - z) * mu_ref[...]

    r = zs[:, 0:rw]
    k = zs[:, rw:2 * rw]
    v = zs[:, 2 * rw:3 * rw]
    o = 3 * rw
    wd = zs[:, o:o + LANES]
    ad = zs[:, o + LANES:o + 2 * LANES]
    gd = zs[:, o + 2 * LANES:o + 3 * LANES]

    seg = seg_ref[...]
    segt = segt_ref[...]

    def head_sum(t):
        return _dot_hilo(_dot_hilo(t, seg), segt)

    g_ref[0] = _dot(_sigmoid(gd), g2_ref[...], "bf16").astype(g_ref.dtype)
    kk = k * kk_w_ref[...]
    kk = kk * lax.rsqrt(head_sum(kk * kk) + 1e-12)
    wl = w0_ref[...] + _dot(jnp.tanh(wd), w2_ref[...], "bf16")
    al = a0_ref[...] + _dot(ad, a2_ref[...], "bf16")
    r_ref[0] = r.astype(r_ref.dtype)
    v_ref[0] = v.astype(v_ref.dtype)
    kk_ref[0] = kk.astype(kk_ref.dtype)
    k_sum = jnp.zeros_like(r)
    for d in range(2):
        a = _sigmoid(al[:, d * rw:(d + 1) * rw])
        k_d = k * (1.0 + (a - 1.0) * ka_ref[...])
        k_sum = k_sum + k_d
        lw_ref[d, 0] = -math.exp(-0.5) * _sigmoid(wl[:, d * rw:(d + 1) * rw])
        kd_ref[d, 0] = k_d.astype(kd_ref.dtype)
        be_ref[d, 0] = (kk * a).astype(be_ref.dtype)
    bonus_ref[0] = (head_sum(r * rk_ref[...] * k_sum) * v).astype(bonus_ref.dtype)


def _rw_prep(z_rw, bsz, l_ctx, l_lat, mu, w2bd, a2bd, g2, w0cat, a0cat, k_k, k_a, r_k_flat, seg, segt):
    cols = z_rw.shape[1]
    tm = l_ctx
    rw = g2.shape[1]
    l_all = l_ctx + l_lat
    nblk = l_all // tm
    lat_blk = l_lat // tm
    hb = tm // GRID_W
    lat_hblk = l_lat // GRID_W

    def main_blk(b, j):
        return jnp.where(j == 0, bsz * lat_blk + b, b * lat_blk + j - 1)

    def prev_halo(b, j):
        return b * lat_hblk + jnp.maximum((j - 1) * hb - 1, 0)

    def next_halo(b, j):
        return b * lat_hblk + jnp.minimum(jnp.maximum(j, 1) * hb, lat_hblk - 1)

    full = lambda shape: pl.BlockSpec(shape, lambda b, j: (0,) * len(shape))
    shared = jax.ShapeDtypeStruct((bsz, l_all, rw), BF16)
    lat_only = jax.ShapeDtypeStruct((bsz, l_lat, rw), BF16)
    per_dir = jax.ShapeDtypeStruct((2, bsz, l_all, rw), BF16)
    per_dir_f32 = jax.ShapeDtypeStruct((2, bsz, l_all, rw), F32)
    o_shared = pl.BlockSpec((1, tm, rw), lambda b, j: (b, j, 0))
    o_lat = pl.BlockSpec((1, tm, rw), lambda b, j: (b, jnp.maximum(j - 1, 0), 0))
    o_dir = pl.BlockSpec((2, 1, tm, rw), lambda b, j: (0, b, j, 0))
    return pl.pallas_call(
        functools.partial(_rw_prep_kernel, tm=tm, l_lat=l_lat, rw=rw),
        out_shape=(shared,) * 3 + (lat_only,) * 2 + (per_dir_f32, per_dir, per_dir),
        grid=(bsz, nblk),
        in_specs=[pl.BlockSpec((tm, cols), lambda b, j: (main_blk(b, j), 0)),
                  pl.BlockSpec((GRID_W, cols), lambda b, j: (prev_halo(b, j), 0)),
                  pl.BlockSpec((GRID_W, cols), lambda b, j: (next_halo(b, j), 0)),
                  full((1, cols)), full(w2bd.shape), full(a2bd.shape), full(g2.shape),
                  full((1, 2 * rw)), full((1, 2 * rw)), full((1, rw)), full((1, rw)), full((1, rw)),
                  full(seg.shape), full(segt.shape)],
        out_specs=(o_shared,) * 3 + (o_lat,) * 2 + (o_dir,) * 3,
        compiler_params=_cparams(("parallel", "arbitrary")),
        name="rwkv_prep",
    )(z_rw, z_rw, z_rw, mu.reshape(1, cols), w2bd, a2bd, g2, w0cat, a0cat,
      k_k.reshape(1, rw), k_a.reshape(1, rw), r_k_flat.reshape(1, rw), seg, segt)


def _stack_heads(x, head0):
    return jnp.concatenate([jnp.where(head0, x, 0.0), jnp.where(head0, 0.0, x)], axis=0)


def _rw_step_kernel(rf_ref, rb_ref, vf_ref, vb_ref, kkf_ref, kkb_ref, lwf_ref, lwb_ref, kdf_ref, kdb_ref,
                    bef_ref, beb_ref, yf_ref, yb_ref,
                    z_scr, gc_s, lhs_s, kb_s, kht_s, bht_s, vs_s, rt_s, aab_s, akr_s, arb_s, t_s, x_s, xin_s,
                    yc_s, wu_s, qm_s, yn_s, *, n_pairs, bsz):
    c_n = RW_CHUNK
    n2 = 2 * c_n

    lanes_of = lambda p: slice(p * LANES, (p + 1) * LANES)
    unit = lambda d, b, p: (d * bsz + b) * n_pairs + p
    units = [(d, b, p) for d in range(2) for b in range(bsz) for p in range(n_pairs)]

    @pl.when(pl.program_id(0) == 0)
    def _():
        z_scr[...] = jnp.zeros_like(z_scr)
        qm_s[...] = jnp.zeros_like(qm_s)
        yn_s[...] = jnp.zeros_like(yn_s)

    for d, b, p in units:
        u = unit(d, b, p)
        y_ref = yf_ref if d == 0 else yb_ref
        yz = _dot(qm_s[u], z_scr[u].astype(BF16)) + yn_s[u]
        y_ref[b, :, lanes_of(p)] = yz[:c_n] + yz[c_n:n2]
        z_scr[u] = yz[n2:]

    ri = lax.broadcasted_iota(jnp.int32, (c_n, c_n), 0)
    ci = lax.broadcasted_iota(jnp.int32, (c_n, c_n), 1)
    r2 = lax.broadcasted_iota(jnp.int32, (n2, n2), 0)
    c2 = lax.broadcasted_iota(jnp.int32, (n2, n2), 1)
    t2 = r2 % c_n
    i2 = c2 % c_n
    same_head = (r2 // c_n) == (c2 // c_n)
    diag = r2 == c2
    eye = diag.astype(F32)
    head0 = lax.broadcasted_iota(jnp.int32, (1, LANES), 1) < RW_HEAD
    tri = [(ri >= ci).astype(F32), (ri <= ci).astype(F32)]
    before = [jnp.logical_and(same_head, i2 < t2), jnp.logical_and(same_head, i2 > t2)]
    upto = [jnp.logical_or(m, diag) for m in before]

    def blk(s):
        return (r2 // s) == (c2 // s)

    srcs = [(rf_ref, vf_ref, kkf_ref, lwf_ref, kdf_ref, bef_ref), (rb_ref, vb_ref, kkb_ref, lwb_ref, kdb_ref, beb_ref)]

    for d in range(2):
        r_ref, v_ref, kk_ref, lw_ref, kd_ref, be_ref = srcs[d]
        for b in range(bsz):
            lw = lw_ref[0, b]
            cum = _dot(tri[d], lw, HIGHEST)
            tot = jnp.sum(lw, axis=0, keepdims=True)
            g_inv = jnp.exp(-cum)
            g_hat = jnp.exp(tot - cum)
            kd = kd_ref[0, b].astype(F32)
            be = be_ref[0, b].astype(F32)
            at = kk_ref[b].astype(F32) * jnp.exp(cum - lw)
            rt = r_ref[b].astype(F32) * jnp.exp(cum)
            kt = kd * g_inv
            bt = be * g_inv
            kh = kd * g_hat
            bh = be * g_hat
            gc_s[d * bsz + b] = jnp.exp(tot)
            v = v_ref[b].astype(F32)
            for p in range(n_pairs):
                u = unit(d, b, p)
                st = lambda x: _stack_heads(x[:, lanes_of(p)], head0)
                rt_p = st(rt)
                at_p = st(at).astype(BF16)
                lhs_s[u, :n2] = at_p
                lhs_s[u, n2:] = rt_p.astype(BF16)
                xin_s[u, :, :LANES] = at_p
                rt_s[u] = rt_p
                kb_s[u, :n2] = st(kt).astype(BF16)
                kb_s[u, n2:] = st(bt).astype(BF16)
                kht_s[u] = st(kh).T.astype(BF16)
                bht_s[u] = st(bh).T.astype(BF16)
                vs_s[u] = st(v).astype(BF16)

    for d, b, p in units:
        u = unit(d, b, p)
        g = _dot_nt(lhs_s[u], kb_s[u])
        a_ab = jnp.where(before[d], g[:n2, n2:], 0.0)
        akr_s[u, :n2] = jnp.where(before[d], g[:n2, :n2], 0.0).astype(BF16)
        akr_s[u, n2:] = jnp.where(upto[d], g[n2:, :n2], 0.0).astype(BF16)
        arb_s[u] = jnp.where(upto[d], g[n2:, n2:], 0.0).astype(BF16)
        aab_s[u] = a_ab
        t_s[u] = eye - jnp.where(blk(2), a_ab, 0.0)

    n_units = len(units)
    s = 2
    while s < c_n:
        off = jnp.logical_and(blk(2 * s), jnp.logical_not(blk(s)))
        for u in range(n_units):
            x_s[u] = _dot(t_s[u].astype(BF16), jnp.where(off, aab_s[u], 0.0).astype(BF16)).astype(BF16)
        for u in range(n_units):
            t = t_s[u]
            t_s[u] = t - _dot(x_s[u], t.astype(BF16))
        s *= 2

    for u in range(n_units):
        av = _dot(akr_s[u], vs_s[u])
        xin_s[u, :, LANES:] = av[:n2].astype(BF16)
        yc_s[u] = av[n2:]

    for u in range(n_units):
        wu_s[u] = _dot(t_s[u].astype(BF16), xin_s[u]).astype(BF16)
    for d, b, p in units:
        u = unit(d, b, p)
        wu = wu_s[u]
        q = _dot(arb_s[u], wu)
        bwu = _dot(bht_s[u], wu)
        qm_s[u, :n2] = (rt_s[u] - q[:, :LANES]).astype(BF16)
        qm_s[u, n2:] = (eye * gc_s[d * bsz + b][:, lanes_of(p)] - bwu[:, :LANES]).astype(BF16)
        yn_s[u, :n2] = yc_s[u] - q[:, LANES:]
        yn_s[u, n2:] = _dot(kht_s[u], vs_s[u]) - bwu[:, LANES:]


def _rw_scan(r, v, kk, lw, kd, be, l_ctx):
    bsz, l_all, rw = r.shape
    c_n = RW_CHUNK
    n_all = l_all // c_n
    n_ctx = l_ctx // c_n
    n_lat = n_all - n_ctx
    n_pairs = rw // LANES
    n_units = 2 * bsz * n_pairs
    n2 = 2 * c_n
    vm = lambda nr, cols, dt: pltpu.VMEM((n_units, nr, cols), dt)

    clamp = lambda s: jnp.minimum(s, n_all - 1)
    chunk_f = lambda s: clamp(s)
    chunk_b = lambda s: jnp.where(clamp(s) < n_ctx, n_ctx - 1 - clamp(s), n_all + n_ctx - 1 - clamp(s))
    prev = lambda s: jnp.maximum(s - 1, 0)
    out_f = lambda s: jnp.maximum(prev(s) - n_ctx, 0)
    out_b = lambda s: jnp.where(prev(s) < n_ctx, n_lat - 1, chunk_b(prev(s)) - n_ctx)

    sh_f = pl.BlockSpec((bsz, c_n, rw), lambda s: (0, chunk_f(s), 0))
    sh_b = pl.BlockSpec((bsz, c_n, rw), lambda s: (0, chunk_b(s), 0))
    pd_f = pl.BlockSpec((1, bsz, c_n, rw), lambda s: (0, 0, chunk_f(s), 0))
    pd_b = pl.BlockSpec((1, bsz, c_n, rw), lambda s: (1, 0, chunk_b(s), 0))
    y_shape = jax.ShapeDtypeStruct((bsz, n_lat * c_n, rw), F32)
    return pl.pallas_call(
        functools.partial(_rw_step_kernel, n_pairs=n_pairs, bsz=bsz),
        out_shape=(y_shape, y_shape),
        grid=(n_all + 1,),
        in_specs=[sh_f, sh_b, sh_f, sh_b, sh_f, sh_b, pd_f, pd_b, pd_f, pd_b, pd_f, pd_b],
        out_specs=(pl.BlockSpec((bsz, c_n, rw), lambda s: (0, out_f(s), 0)),
                   pl.BlockSpec((bsz, c_n, rw), lambda s: (0, out_b(s), 0))),
        scratch_shapes=[vm(n2, n2, F32),
                        pltpu.VMEM((2 * bsz, 1, rw), F32),
                        vm(2 * n2, n2, BF16),
                        vm(2 * n2, n2, BF16),
                        vm(n2, n2, BF16),
                        vm(n2, n2, BF16),
                        vm(n2, n2, BF16),
                        vm(n2, n2, F32),
                        vm(n2, n2, F32),
                        vm(2 * n2, n2, BF16),
                        vm(n2, n2, BF16),
                        vm(n2, n2, F32),
                        vm(n2, n2, BF16),
                        vm(n2, 2 * n2, BF16),
                        vm(n2, n2, F32),
                        vm(n2, 2 * n2, BF16),
                        vm(2 * n2, n2, BF16),
                        vm(2 * n2, n2, F32)],
        compiler_params=_cparams(("arbitrary",)),
        name="rwkv_scan",
    )(r, r, v, v, kk, kk, lw, lw, kd, kd, be, be)


def _s5_glu_kernel(y_ref, wa_ref, wb_ref, o_ref, h_scr):
    @pl.when(pl.program_id(1) == 0)
    def _():
        for jb in range(y_ref.shape[0]):
            h_scr[:, jb * LANES:(jb + 1) * LANES] = _gelu_tanh(y_ref[jb]).astype(BF16)

    h = h_scr[...]
    o_ref[...] = (_dot(h, wa_ref[...]) * _sigmoid(_dot(h, wb_ref[...]))).astype(o_ref.dtype)


def _s5_glu(y_blk, w, *, tm, tn):
    nb, m, _ = y_blk.shape
    k = nb * LANES
    n = w.shape[1] // 2
    nj = n // tn
    return pl.pallas_call(
        _s5_glu_kernel,
        out_shape=jax.ShapeDtypeStruct((m, n), BF16),
        grid=(m // tm, nj),
        in_specs=[pl.BlockSpec((nb, tm, LANES), lambda i, j: (0, i, 0)),
                  pl.BlockSpec((k, tn), lambda i, j: (0, j)),
                  pl.BlockSpec((k, tn), lambda i, j: (0, j + nj))],
        out_specs=pl.BlockSpec((tm, tn), lambda i, j: (i, j)),
        scratch_shapes=[pltpu.VMEM((tm, k), BF16)],
        compiler_params=_cparams(("parallel", "arbitrary")),
        name="s5_glu",
    )(y_blk, w, w)


def _merge_out_kernel(yf_ref, yb_ref, bonus_ref, g_ref, lnw_ref, lnb_ref, seg_ref, segt_ref,
                      ga_ref, gb_ref, s5_ref, wp_ref, wo_ref, x_ref, g1_ref, nw_ref, sh_ref, sc_ref,
                      h_ref, hn_ref):
    seg = seg_ref[...]
    segt = segt_ref[...]
    inv_n = 1.0 / RW_HEAD
    tm = yf_ref.shape[0]
    parts = [slice(0, tm // 2), slice(tm // 2, tm)]

    def head_mean(vals):
        sums = [_dot_hilo(v, seg) for v in vals]
        return [_dot_hilo(s, segt) * inv_n for s in sums]

    y = [yf_ref[p, :] + yb_ref[p, :] for p in parts]
    dy = [a - m for a, m in zip(y, head_mean(y))]
    var = head_mean([a * a for a in dy])
    y_rw = []
    for p, a, v in zip(parts, dy, var):
        t = a * lax.rsqrt(v + GN_EPS) * lnw_ref[...] + lnb_ref[...] + bonus_ref[p, :].astype(F32)
        y_rw.append((t * g_ref[p, :].astype(F32)).astype(BF16))
    rw_out = [_dot(a, wp_ref[...]) for a in y_rw]
    merged = [(ga_ref[p, :].astype(F32) * s5_ref[p, :].astype(F32) + gb_ref[p, :].astype(F32) * r).astype(BF16)
              for p, r in zip(parts, rw_out)]
    proj = [_dot(a, wo_ref[...]) for a in merged]
    for p, a in zip(parts, proj):
        h = x_ref[p, :] + g1_ref[0] * a
        h_ref[p, :] = h
        hn_ref[p, :] = _lnmod_rows(h, nw_ref[...], sh_ref[0], sc_ref[0]).astype(hn_ref.dtype)


def _merge_out(y_f, y_b, bonus, g, ln_w, ln_b, seg, segt, gates, s5_out, w_proj, w_o, x2, g_tab, nw,
               sh_tab, sc_tab, mod_row_of_block, *, tm):
    m, rw = y_f.shape
    n = w_proj.shape[1]
    full = lambda shape: pl.BlockSpec(shape, lambda i: (0,) * len(shape))
    rows = lambda width: pl.BlockSpec((tm, width), lambda i: (i, 0))
    mod_map = lambda i: (mod_row_of_block(i), 0, 0)
    return pl.pallas_call(
        _merge_out_kernel,
        out_shape=(jax.ShapeDtypeStruct((m, n), F32), jax.ShapeDtypeStruct((m, n), BF16)),
        grid=(m // tm,),
        in_specs=[rows(rw), rows(rw), rows(rw), rows(rw),
                  full((1, rw)), full((1, rw)), full(seg.shape), full(segt.shape),
                  pl.BlockSpec((tm, n), lambda i: (i, 0)), pl.BlockSpec((tm, n), lambda i: (i, 1)), rows(n),
                  full(w_proj.shape), full(w_o.shape), rows(n),
                  pl.BlockSpec((1, 1, n), mod_map), full((1, n)),
                  pl.BlockSpec((1, 1, n), mod_map), pl.BlockSpec((1, 1, n), mod_map)],
        out_specs=(rows(n), rows(n)),
        compiler_params=_cparams(("parallel",), vmem=VMEM_LIMIT + 8 * 1024 * 1024),
        name="merge_out_proj",
    )(y_f, y_b, bonus, g, ln_w.reshape(1, rw), ln_b.reshape(1, rw), seg, segt,
      gates, gates, s5_out, w_proj, w_o, x2, g_tab, nw.reshape(1, n), sh_tab, sc_tab)


def _wres_swiglu_kernel(a_ref, w1_ref, w3_ref, o_ref, w1_scr, w3_scr):
    @pl.when(pl.program_id(1) == 0)
    def _():
        w1_scr[...] = w1_ref[...].astype(BF16)
        w3_scr[...] = w3_ref[...].astype(BF16)

    a = a_ref[...]
    o_ref[...] = (_silu(_dot(a, w1_scr[...])) * _dot(a, w3_scr[...])).astype(o_ref.dtype)


def _wres_swiglu(a, w13, d_ff, *, tm, tn):
    m, k = a.shape
    nj = d_ff // tn
    return pl.pallas_call(
        _wres_swiglu_kernel,
        out_shape=jax.ShapeDtypeStruct((m, d_ff), BF16),
        grid=(nj, m // tm),
        in_specs=[pl.BlockSpec((tm, k), lambda j, i: (i, 0)),
                  pl.BlockSpec((k, tn), lambda j, i: (0, j)),
                  pl.BlockSpec((k, tn), lambda j, i: (0, j + nj))],
        out_specs=pl.BlockSpec((tm, tn), lambda j, i: (i, j)),
        scratch_shapes=[pltpu.VMEM((k, tn), BF16), pltpu.VMEM((k, tn), BF16)],
        compiler_params=_cparams(("arbitrary", "arbitrary")),
        name="ffn_up",
    )(a, w13, w13)


def _ffn_down_kernel(a_ref, w_ref, x_ref, g_ref, nf_ref, o_ref):
    kk = pl.program_id(1)
    part = _dot(a_ref[...], w_ref[...])

    @pl.when(kk == 0)
    def _():
        o_ref[...] = part

    @pl.when(jnp.logical_and(kk > 0, kk < pl.num_programs(1) - 1))
    def _():
        o_ref[...] += part

    @pl.when(kk == pl.num_programs(1) - 1)
    def _():
        h = x_ref[...] + g_ref[0] * (o_ref[...] + part)
        ms = jnp.mean(h * h, axis=-1, keepdims=True)
        o_ref[...] = h * lax.rsqrt(ms + NORM_EPS) * nf_ref[...]


def _ffn_down(a, w, x2, g_tab, mod_row_of_block, norm_f, *, tm, tk):
    m, k = a.shape
    n = w.shape[1]
    return pl.pallas_call(
        _ffn_down_kernel,
        out_shape=jax.ShapeDtypeStruct((m, n), F32),
        grid=(m // tm, k // tk),
        in_specs=[pl.BlockSpec((tm, tk), lambda i, kk: (i, kk)),
                  pl.BlockSpec((tk, n), lambda i, kk: (kk, 0)),
                  pl.BlockSpec((tm, n), lambda i, kk: (i, 0)),
                  pl.BlockSpec((1, 1, n), lambda i, kk: (mod_row_of_block(i), 0, 0)),
                  pl.BlockSpec((1, n), lambda i, kk: (0, 0))],
        out_specs=pl.BlockSpec((tm, n), lambda i, kk: (i, 0)),
        compiler_params=_cparams(("parallel", "arbitrary"), vmem=VMEM_LIMIT + 8 * 1024 * 1024),
        name="ffn_down",
    )(a, w, x2, g_tab, norm_f.reshape(1, n))


def kernel(x, c, ctx, c_ctx, ada_w, ada_b, norm1_w, w_in, rw_mu, s5_a_re, s5_a_im, s5_log_dt, s5_b_re, s5_b_im, s5_c_re, s5_c_im, s5_d, s5_glu_w, rw_w0, rw_w2, rw_a0, rw_a2, rw_g2, rw_k_k, rw_k_a, rw_r_k, rw_ln_w, rw_ln_b, rw_proj, w_o, norm2_w, ffn_w13, ffn_w2, norm_f):
    assert ada_w.shape[0] == 1, "single-layer block"
    bsz, l_lat, d = x.shape
    l_ctx = ctx.shape[1]
    l_all = l_ctx + l_lat
    s5w = s5_d.shape[1] * s5_d.shape[2]
    rw = rw_g2.shape[2]
    shift_cols = rw_mu.shape[1]
    d_ff = ffn_w2.shape[1]

    c_rows = jnp.concatenate([c, c_ctx[None], jnp.zeros((8 - bsz - 1, d), F32)], axis=0)
    mod = _modulation(c_rows, ada_w[0], ada_b[0])
    tab = lambda k: mod[:, k * d:(k + 1) * d].reshape(8, 1, d)
    sh1, sc1, g1, sh2, sc2, g2 = (tab(k) for k in range(N_MOD))
    ctx_row = bsz

    tm_in = 512
    m_lat = bsz * l_lat
    x2 = x.reshape(m_lat, d)
    n_mix = s5w + shift_cols
    m_all = m_lat + bsz * l_ctx
    lat_row = lambda t: (lambda i: i // (l_lat // t))

    def mix_mod_row(i):
        return jnp.where(i >= m_lat // tm_in, ctx_row, i // (l_lat // tm_in))

    h_all, u_blk = _lnmod_proj(x2, ctx.reshape(bsz * l_ctx, d), norm1_w[0], sh1, sc1, mix_mod_row,
                               w_in[0], s5w, tm=tm_in)
    tm_all = m_all // 8
    z_rw = _wres_matmul(h_all, w_in[0], s5w, shift_cols, rows=m_all, tm=tm_all, tn=shift_cols // 3,
                        out_dtype=BF16, name="in_proj_rw")
    tm = 1024
    gates = _wres_matmul(h_all, w_in[0], n_mix, w_in.shape[2] - n_mix, rows=m_lat, tm=tm, tn=1024,
                         out_dtype=BF16, epilogue="sigmoid", name="in_proj_gates")

    y_blk = _s5_branch(u_blk, bsz, l_ctx, l_lat, s5_a_re[0], s5_a_im[0], s5_log_dt[0],
                               s5_b_re[0], s5_b_im[0], s5_c_re[0], s5_c_im[0], s5_d[0])
    s5_out = _s5_glu(y_blk, s5_glu_w[0].astype(BF16), tm=tm, tn=1024)

    lora = rw_w2.shape[2]
    zl = jnp.zeros((lora, rw), F32)
    w2bd = jnp.concatenate([jnp.concatenate([rw_w2[0, 0], zl], axis=1),
                            jnp.concatenate([zl, rw_w2[0, 1]], axis=1)], axis=0)
    a2bd = jnp.concatenate([jnp.concatenate([rw_a2[0, 0], zl], axis=1),
                            jnp.concatenate([zl, rw_a2[0, 1]], axis=1)], axis=0)
    head_of = jnp.arange(rw) // RW_HEAD
    seg = (head_of[:, None] == jnp.arange(LANES)[None, :]).astype(BF16)
    segt = seg.T
    r, v, kk, g, bonus, lw, kd, be = _rw_prep(
        z_rw, bsz, l_ctx, l_lat, rw_mu[0], w2bd, a2bd, rw_g2[0], rw_w0[0].reshape(1, 2 * rw),
        rw_a0[0].reshape(1, 2 * rw), rw_k_k[0], rw_k_a[0], rw_r_k[0].reshape(rw), seg, segt)
    y_f, y_b = _rw_scan(r, v, kk, lw, kd, be, l_ctx)

    tm_o = 256
    h1, h1n = _merge_out(y_f.reshape(m_lat, rw), y_b.reshape(m_lat, rw), bonus.reshape(m_lat, rw),
                         g.reshape(m_lat, rw), rw_ln_w[0], rw_ln_b[0], seg, segt, gates, s5_out,
                         rw_proj[0].astype(BF16), w_o[0].astype(BF16), x2, g1, norm2_w[0], sh2, sc2,
                         lat_row(tm_o), tm=tm_o)

    act = _wres_swiglu(h1n, ffn_w13[0], d_ff, tm=tm, tn=512)
    tm_dn = 512
    out = _ffn_down(act, ffn_w2[0].astype(BF16), h1, g2, lat_row(tm_dn), norm_f, tm=tm_dn, tk=d_ff // 2)
    return out.reshape(bsz, l_lat, d)
```

```python
import functools
import math

import jax
import jax.numpy as jnp
from jax import lax
from jax.experimental import pallas as pl
from jax.experimental.pallas import tpu as pltpu

F32 = jnp.float32
BF16 = jnp.bfloat16
HIGHEST = lax.Precision.HIGHEST

N_MOD = 6
NORM_EPS = 1e-6
GN_EPS = 64e-5
GRID_W = 64
S5_GROUP = 16
S5_STATE = 64
S5_CHUNK = 16
RW_HEAD = 64
RW_CHUNK = 64
LANES = 128
VMEM_LIMIT = 48 * 1024 * 1024


def _cparams(sem, vmem=VMEM_LIMIT):
    return pltpu.CompilerParams(dimension_semantics=sem, vmem_limit_bytes=vmem)


def _operands(a, b, precision):
    if precision == "bf16":
        return a.astype(BF16), b.astype(BF16), None
    return a, b, precision


def _dot(a, b, precision=None):
    a, b, precision = _operands(a, b, precision)
    return jnp.dot(a, b, preferred_element_type=F32, precision=precision)


def _dot_nt(a, b, precision=None):
    a, b, precision = _operands(a, b, precision)
    return lax.dot_general(a, b, (((1,), (1,)), ((), ())), preferred_element_type=F32, precision=precision)


def _dot_hilo(a, ind):
    hi = a.astype(BF16)
    lo = (a - hi.astype(F32)).astype(BF16)
    return _dot(hi, ind) + _dot(lo, ind)


def _sigmoid(x):
    return 1.0 / (1.0 + jnp.exp(-x))


def _silu(x):
    return x * _sigmoid(x)


def _gelu_tanh(x):
    c = math.sqrt(2.0 / math.pi)
    return 0.5 * x * (1.0 + jnp.tanh(c * (x + 0.044715 * (x * x * x))))


def _mod_kernel(c_ref, w_ref, b_ref, o_ref):
    o_ref[...] = _dot_hilo(_silu(c_ref[...]), w_ref[...].astype(BF16)) + b_ref[...]


def _modulation(c_rows, ada_w, ada_b):
    m, d = c_rows.shape
    n = ada_w.shape[1]
    tn = 1024
    return pl.pallas_call(
        _mod_kernel,
        out_shape=jax.ShapeDtypeStruct((m, n), F32),
        grid=(n // tn,),
        in_specs=[pl.BlockSpec((m, d), lambda j: (0, 0)),
                  pl.BlockSpec((d, tn), lambda j: (0, j)),
                  pl.BlockSpec((1, tn), lambda j: (0, j))],
        out_specs=pl.BlockSpec((m, tn), lambda j: (0, j)),
        compiler_params=_cparams(("arbitrary",)),
        name="modulation",
    )(c_rows, ada_w, ada_b.reshape(1, n))


def _lnmod_rows(x, nw, sh, sc):
    ms = jnp.mean(x * x, axis=-1, keepdims=True)
    y = x * lax.rsqrt(ms + NORM_EPS) * nw
    return y * (1.0 + sc) + sh


def _lnmod_proj_kernel(x_ref, c_ref, nw_ref, sh_ref, sc_ref, w_ref, h_ref, u_ref, w_scr, *, n_lat_blocks):
    @pl.when(pl.program_id(0) == 0)
    def _():
        w_scr[...] = w_ref[...].astype(BF16)

    rows = jnp.where(pl.program_id(0) < n_lat_blocks, x_ref[...], c_ref[...])
    h = _lnmod_rows(rows, nw_ref[...], sh_ref[0], sc_ref[0]).astype(BF16)
    h_ref[...] = h
    z = _dot(h, w_scr[...])
    for jb in range(u_ref.shape[0]):
        u_ref[jb] = z[:, jb * LANES:(jb + 1) * LANES]


def _lnmod_proj(x2, c2, nw, sh_tab, sc_tab, mod_row_of_block, w, n, *, tm):
    m, d = x2.shape
    mc = c2.shape[0]
    nx, nc = m // tm, mc // tm
    mod_map = lambda i: (mod_row_of_block(i), 0, 0)
    return pl.pallas_call(
        functools.partial(_lnmod_proj_kernel, n_lat_blocks=nx),
        out_shape=(jax.ShapeDtypeStruct((m + mc, d), BF16),
                   jax.ShapeDtypeStruct((n // LANES, m + mc, LANES), F32)),
        grid=(nx + nc,),
        in_specs=[pl.BlockSpec((tm, d), lambda i: (jnp.minimum(i, nx - 1), 0)),
                  pl.BlockSpec((tm, d), lambda i: (jnp.maximum(i - nx, 0), 0)),
                  pl.BlockSpec((1, d), lambda i: (0, 0)),
                  pl.BlockSpec((1, 1, d), mod_map),
                  pl.BlockSpec((1, 1, d), mod_map),
                  pl.BlockSpec((d, n), lambda i: (0, 0))],
        out_specs=(pl.BlockSpec((tm, d), lambda i: (i, 0)),
                   pl.BlockSpec((n // LANES, tm, LANES), lambda i: (0, i, 0))),
        scratch_shapes=[pltpu.VMEM((d, n), BF16)],
        compiler_params=_cparams(("arbitrary",)),
        name="lnmod_in_proj_s5",
    )(x2, c2, nw.reshape(1, d), sh_tab, sc_tab, w)


def _wres_mm_kernel(a_ref, w_ref, o_ref, w_scr, *, epilogue):
    @pl.when(pl.program_id(1) == 0)
    def _():
        w_scr[...] = w_ref[...].astype(BF16)

    z = _dot(a_ref[...], w_scr[...])
    if epilogue == "sigmoid":
        z = _sigmoid(z)
    o_ref[...] = z.astype(o_ref.dtype)


def _wres_matmul(a, w, col0, n, *, rows, tm, tn, out_dtype, epilogue=None, name):
    k = a.shape[1]
    assert col0 % LANES == 0 and n % tn == 0 and rows % tm == 0
    return pl.pallas_call(
        functools.partial(_wres_mm_kernel, epilogue=epilogue),
        out_shape=jax.ShapeDtypeStruct((rows, n), out_dtype),
        grid=(n // tn, rows // tm),
        in_specs=[pl.BlockSpec((tm, k), lambda j, i: (i, 0)),
                  pl.BlockSpec((pl.Element(k), pl.Element(tn)), lambda j, i: (0, pl.multiple_of(col0 + j * tn, LANES)))],
        out_specs=pl.BlockSpec((tm, tn), lambda j, i: (i, j)),
        scratch_shapes=[pltpu.VMEM((k, tn), BF16)],
        compiler_params=_cparams(("arbitrary", "arbitrary")),
        name=name,
    )(a, w)


def _s5_param_kernel(are_ref, aim_ref, ldt_ref, bre_ref, bim_ref, cre_ref, cim_ref,
                      e_ref, c_ref, m_ref, a16_ref):
    t_n, hg, p_n = S5_CHUNK, S5_GROUP, S5_STATE
    gl_n = are_ref.shape[1]
    tau = lax.broadcasted_iota(jnp.int32, (t_n, gl_n, p_n), 0).astype(F32)
    taps = {}
    for d in range(2):
        a_re = are_ref[d]
        a_im = aim_ref[d]
        dt = jnp.exp(ldt_ref[d])
        lam = a_re * dt
        th = a_im * dt
        er = jnp.exp(lam)
        ab_re = er * jnp.cos(th)
        ab_im = er * jnp.sin(th)
        den = a_re * a_re + a_im * a_im
        x_re = ab_re - 1.0
        co_re = (x_re * a_re + ab_im * a_im) / den
        co_im = (ab_im * a_re - x_re * a_im) / den

        def power(tv):
            mag = jnp.exp(tv * lam)
            return mag * jnp.cos(tv * th), mag * jnp.sin(tv * th)

        pw_re, pw_im = power(tau if d == 0 else (t_n - 1.0 - tau))
        pe_re, pe_im = power((t_n - 1.0 - tau) if d == 0 else tau)
        pc_re, pc_im = power((tau + 1.0) if d == 0 else (t_n - tau))
        mag16 = jnp.exp(float(t_n) * lam)
        a16_ref[d, 0] = mag16 * jnp.cos(float(t_n) * th)
        a16_ref[d, 1] = mag16 * jnp.sin(float(t_n) * th)

        lo, hi = 2 * d * p_n, (2 * d + 1) * p_n
        for gl in range(gl_n):
            of = lambda x: x[:, gl:gl + 1, :]
            bt_re = bre_ref[d, gl].T
            bt_im = bim_ref[d, gl].T
            bb_re = co_re[gl:gl + 1] * bt_re - co_im[gl:gl + 1] * bt_im
            bb_im = co_re[gl:gl + 1] * bt_im + co_im[gl:gl + 1] * bt_re
            c_re = cre_ref[d, gl]
            c_im = cim_ref[d, gl]
            cp_re = (c_re[None] * of(pw_re) - c_im[None] * of(pw_im)).reshape(t_n * hg, p_n)
            cp_im = (c_re[None] * of(pw_im) + c_im[None] * of(pw_re)).reshape(t_n * hg, p_n)
            taps[d, gl] = _dot_nt(bb_re, cp_re, HIGHEST) - _dot_nt(bb_im, cp_im, HIGHEST)
            e_ref[0, :, gl, :, lo:hi] = of(pe_re) * bb_re[None] - of(pe_im) * bb_im[None]
            e_ref[0, :, gl, :, hi:hi + p_n] = of(pe_re) * bb_im[None] + of(pe_im) * bb_re[None]
            c_ref[0, :, gl, :, lo:hi] = c_re[None] * of(pc_re) - c_im[None] * of(pc_im)
            c_ref[0, :, gl, :, hi:hi + p_n] = -(c_re[None] * of(pc_im) + c_im[None] * of(pc_re))

    width = t_n * hg
    lane = lax.broadcasted_iota(jnp.int32, (hg, width), 1)
    for gl in range(gl_n):
        tf, tb = taps[0, gl], taps[1, gl]
        for t in range(t_n):
            sf = t * hg
            sb = (t_n - 1 - t) * hg
            f = tf if sf == 0 else jnp.where(lane >= sf, pltpu.roll(tf, sf, 1), 0.0)
            b = tb if sb == 0 else jnp.where(lane < width - sb, pltpu.roll(tb, width - sb, 1), 0.0)
            m_ref[0, t, gl] = f + b


def _s5_params(a_re, a_im, log_dt, b_re, b_im, c_re, c_im, nb):
    g_n = a_re.shape[1]
    gl_n = g_n // nb
    p_n, hg, t_n = S5_STATE, S5_GROUP, S5_CHUNK
    ldt = jnp.broadcast_to(log_dt[:, :, None], (2, g_n, p_n))
    spec3 = pl.BlockSpec((2, gl_n, p_n), lambda j: (0, j, 0))
    spec4 = pl.BlockSpec((2, gl_n, hg, p_n), lambda j: (0, j, 0, 0))
    spec_b = pl.BlockSpec((2, gl_n, p_n, hg), lambda j: (0, j, 0, 0))
    wide = 4 * p_n
    comp = jax.ShapeDtypeStruct((nb, t_n, gl_n, hg, wide), F32)
    comp_spec = pl.BlockSpec((1, t_n, gl_n, hg, wide), lambda j: (j, 0, 0, 0, 0))
    return pl.pallas_call(
        _s5_param_kernel,
        out_shape=(comp, comp, comp, jax.ShapeDtypeStruct((2, 2, g_n, p_n), F32)),
        grid=(nb,),
        in_specs=[spec3, spec3, spec3, spec_b, spec_b, spec4, spec4],
        out_specs=(comp_spec, comp_spec, comp_spec, pl.BlockSpec((2, 2, gl_n, p_n), lambda j: (0, 0, j, 0))),
        compiler_params=_cparams(("parallel",)),
        name="s5_params",
    )(a_re, a_im, ldt, b_re, b_im, c_re, c_im)


def _expand_block_diag(comp, rep_ref, mask_ref, w_scr):
    k = w_scr.shape[0]
    period = mask_ref.shape[1]
    cb = comp.astype(BF16)
    step = 512
    for c0 in range(0, k, step):
        blk = _dot(cb, rep_ref[:, c0:c0 + step]).astype(BF16)
        for q0 in range(0, step, period):
            w_scr[:, c0 + q0:c0 + q0 + period] = blk[:, q0:q0 + period] * mask_ref[...]


def _gather_chunk_rows(u_ref, rows, dst):
    t_n = S5_CHUNK
    for t in range(t_n):
        dst[:, t * LANES:(t + 1) * LANES] = u_ref[0, pl.ds(t, rows, stride=t_n), :].astype(dst.dtype)


def _s5_ein_kernel(u_ref, ec_ref, rep_ref, mask_ref, o_ref, w_scr, u_scr):
    _expand_block_diag(ec_ref[0], rep_ref, mask_ref, w_scr)
    _gather_chunk_rows(u_ref, u_scr.shape[0], u_scr)
    o_ref[0] = _dot(u_scr[...], w_scr[...])


def _s5_chunk_inputs(u_blk, e_comp, rep_e, mask_e):
    nb, m_all, _ = u_blk.shape
    k, cw = e_comp.shape[1:]
    rows = m_all // S5_CHUNK
    return pl.pallas_call(
        _s5_ein_kernel,
        out_shape=jax.ShapeDtypeStruct((nb, rows, k), F32),
        grid=(nb,),
        in_specs=[pl.BlockSpec((1, m_all, LANES), lambda j: (j, 0, 0)),
                  pl.BlockSpec((1, k, cw), lambda j: (j, 0, 0)),
                  pl.BlockSpec(rep_e.shape, lambda j: (0, 0)),
                  pl.BlockSpec(mask_e.shape, lambda j: (0, 0))],
        out_specs=pl.BlockSpec((1, rows, k), lambda j: (j, 0, 0)),
        scratch_shapes=[pltpu.VMEM((k, k), BF16), pltpu.VMEM((rows, k), BF16)],
        compiler_params=_cparams(("parallel",)),
        name="s5_chunk_inputs",
    )(u_blk, e_comp, rep_e, mask_e)


def _s5_bscan_kernel(e_ref, a_ref, o_ref, *, bsz, n_ctx, n_lat):
    q = e_ref.shape[2] // 4
    planes = lambda row, d: (row[:, (2 * d) * q:(2 * d + 1) * q], row[:, (2 * d + 1) * q:(2 * d + 2) * q])
    coef = [planes(a_ref[0], d) for d in range(2)]
    ctx0 = bsz * n_lat

    def advance(state, rows):
        new = []
        for (sr, si), (b, d), row in zip(state, [(b, d) for b in range(bsz) for d in range(2)], rows):
            ar, ai = coef[d]
            er, ei = planes(e_ref[0, pl.ds(row, 1), :], d)
            new.append((ar * sr - ai * si + er, ar * si + ai * sr + ei))
        return tuple(new)

    def ctx_step(s, state):
        rows = [ctx0 + b * n_ctx + (s if d == 0 else n_ctx - 1 - s) for b in range(bsz) for d in range(2)]
        return advance(state, rows)

    def lat_step(s, state):
        rows = [b * n_lat + (s if d == 0 else n_lat - 1 - s) for b in range(bsz) for d in range(2)]
        for (sr, si), (b, d), row in zip(state, [(b, d) for b in range(bsz) for d in range(2)], rows):
            o_ref[0, pl.ds(row, 1), (2 * d) * q:(2 * d + 1) * q] = sr
            o_ref[0, pl.ds(row, 1), (2 * d + 1) * q:(2 * d + 2) * q] = si
        return advance(state, rows)

    zero = jnp.zeros((1, q), F32)
    state = tuple((zero, zero) for _ in range(2 * bsz))
    state = lax.fori_loop(0, n_ctx, ctx_step, state)
    lax.fori_loop(0, n_lat, lat_step, state)


def _s5_bscan(e_rows, a_rows, bsz, n_ctx, n_lat):
    nb, rows, k = e_rows.shape
    return pl.pallas_call(
        functools.partial(_s5_bscan_kernel, bsz=bsz, n_ctx=n_ctx, n_lat=n_lat),
        out_shape=jax.ShapeDtypeStruct((nb, bsz * n_lat, k), F32),
        grid=(nb,),
        in_specs=[pl.BlockSpec((1, rows, k), lambda j: (j, 0, 0)),
                  pl.BlockSpec((1, 1, k), lambda j: (j, 0, 0))],
        out_specs=pl.BlockSpec((1, bsz * n_lat, k), lambda j: (j, 0, 0)),
        compiler_params=_cparams(("parallel",)),
        name="s5_scan",
    )(e_rows, a_rows)


def _s5_out_kernel(u_ref, s_ref, mc_ref, cc_ref, d_ref, rep_m_ref, mask_m_ref, rep_e_ref, mask_e_ref,
                   o_ref, wm_scr, wc_scr, u_scr):
    @pl.when(pl.program_id(1) == 0)
    def _():
        _expand_block_diag(mc_ref[0], rep_m_ref, mask_m_ref, wm_scr)
        _expand_block_diag(cc_ref[0], rep_e_ref, mask_e_ref, wc_scr)

    t_n = S5_CHUNK
    rows = s_ref.shape[1]
    _gather_chunk_rows(u_ref, rows, u_scr)
    u = u_scr[...]
    y = _dot(u.astype(BF16), wm_scr[...]) + _dot_nt(s_ref[0].astype(BF16), wc_scr[...])
    y = y + u * d_ref[0]
    for t in range(t_n):
        o_ref[0, pl.ds(t, rows, stride=t_n), :] = y[:, t * LANES:(t + 1) * LANES]


def _s5_outputs(u_blk, s_rows, m_comp, c_comp, d_rows, rep_m, mask_m, rep_e, mask_e):
    nb = u_blk.shape[0]
    rows, k = s_rows.shape[1:]
    cw = m_comp.shape[2]
    tr = rows // 2
    tok = tr * S5_CHUNK
    const = lambda a: pl.BlockSpec(a.shape, lambda j, i: (0, 0))
    return pl.pallas_call(
        _s5_out_kernel,
        out_shape=jax.ShapeDtypeStruct((nb, rows * S5_CHUNK, LANES), F32),
        grid=(nb, rows // tr),
        in_specs=[pl.BlockSpec((1, tok, LANES), lambda j, i: (j, i, 0)),
                  pl.BlockSpec((1, tr, k), lambda j, i: (j, i, 0)),
                  pl.BlockSpec((1, k, cw), lambda j, i: (j, 0, 0)),
                  pl.BlockSpec((1, k, cw), lambda j, i: (j, 0, 0)),
                  pl.BlockSpec((1, 1, k), lambda j, i: (j, 0, 0)),
                  const(rep_m), const(mask_m), const(rep_e), const(mask_e)],
        out_specs=pl.BlockSpec((1, tok, LANES), lambda j, i: (j, i, 0)),
        scratch_shapes=[pltpu.VMEM((k, k), BF16), pltpu.VMEM((k, k), BF16), pltpu.VMEM((tr, k), F32)],
        compiler_params=_cparams(("parallel", "arbitrary")),
        name="s5_outputs",
    )(u_blk, s_rows, m_comp, c_comp, d_rows, rep_m, mask_m, rep_e, mask_e)


def _s5_branch(u_blk, bsz, l_ctx, l_lat, a_re, a_im, log_dt, b_re, b_im, c_re, c_im, s5_d):
    nb, m_all, _ = u_blk.shape
    hg, t_n, p_n = S5_GROUP, S5_CHUNK, S5_STATE
    g_n = a_re.shape[1]
    gl_n = g_n // nb
    k = t_n * LANES
    n_lat = l_lat // t_n
    n_ctx = l_ctx // t_n
    e_c, c_c, m_c, a16 = _s5_params(a_re, a_im, log_dt, b_re, b_im, c_re, c_im, nb)
    cw = 4 * p_n
    e_comp, c_comp, m_comp = (a.reshape(nb, k, cw) for a in (e_c, c_c, m_c))

    row_gl = (jnp.arange(k) // hg) % gl_n
    col = jnp.arange(k)
    src = jnp.arange(cw)
    rep_e = ((src[:, None] // p_n == col[None, :] // (gl_n * p_n)) & (src[:, None] % p_n == col[None, :] % p_n)).astype(BF16)
    mask_e = (row_gl[:, None] == (jnp.arange(gl_n * p_n)[None, :] // p_n)).astype(BF16)
    rep_m = ((src[:, None] // hg == col[None, :] // (gl_n * hg)) & (src[:, None] % hg == col[None, :] % hg)).astype(BF16)
    mask_m = (row_gl[:, None] == (jnp.arange(gl_n * hg)[None, :] // hg)).astype(BF16)

    d_rows = jnp.tile(s5_d.reshape(nb, 1, gl_n * hg), (1, 1, t_n))
    a_rows = jnp.transpose(a16.reshape(2, 2, nb, gl_n * p_n), (2, 0, 1, 3)).reshape(nb, 1, 4 * gl_n * p_n)

    e_rows = _s5_chunk_inputs(u_blk, e_comp, rep_e, mask_e)
    s_rows = _s5_bscan(e_rows, a_rows, bsz, n_ctx, n_lat)
    return _s5_outputs(u_blk, s_rows, m_comp, c_comp, d_rows, rep_m, mask_m, rep_e, mask_e)


def _rw_prep_kernel(z_ref, zp_ref, zn_ref, mu_ref, w2_ref, a2_ref, g2_ref, w0_ref, a0_ref,
                    kk_w_ref, ka_ref, rk_ref, seg_ref, segt_ref,
                    r_ref, v_ref, kk_ref, g_ref, bonus_ref, lw_ref, kd_ref, be_ref,
                    *, tm, l_lat, rw):
    j = pl.program_id(1)
    z = z_ref[...].astype(F32)
    lat = j > 0
    tl = lax.broadcasted_iota(jnp.int32, (tm, 1), 0)
    tok = (j - 1) * tm + tl
    col = tl % GRID_W
    m_l = jnp.where(lat, col, tl) > 0
    m_r = jnp.where(lat, col - (GRID_W - 1), tl - (tm - 1)) < 0
    m_u = jnp.logical_and(lat, tok >= GRID_W)
    m_d = jnp.logical_and(lat, tok < l_lat - GRID_W)
    z_ext = jnp.concatenate([zp_ref[...], z_ref[...], zn_ref[...]], axis=0)
    rel = (lax.broadcasted_iota(jnp.int32, (tm, tm + 2 * GRID_W), 1) - GRID_W
           - lax.broadcasted_iota(jnp.int32, (tm, tm + 2 * GRID_W), 0))
    pick = (jnp.logical_and(rel == -1, m_l) | jnp.logical_and(rel == 1, m_r)
            | jnp.logical_and(rel == -GRID_W, m_u) | jnp.logical_and(rel == GRID_W, m_d))
    s = _dot(jnp.where(pick, 1.0, 0.0).astype(z_ext.dtype), z_ext)
    cnt = (m_l.astype(F32) + m_r.astype(F32)) + (m_u.astype(F32) + m_d.astype(F32))
    zs = z + (s * (1.0 / cnt) - z) * mu_ref[...]

    r = zs[:, 0:rw]
    k = zs[:, rw:2 * rw]
    v = zs[:, 2 * rw:3 * rw]
    o = 3 * rw
    wd = zs[:, o:o + LANES]
    ad = zs[:, o + LANES:o + 2 * LANES]
    gd = zs[:, o + 2 * LANES:o + 3 * LANES]

    seg = seg_ref[...]
    segt = segt_ref[...]

    def head_sum(t):
        return _dot(_dot(t.astype(BF16), seg).astype(BF16), segt)

    g_ref[0] = _dot(_sigmoid(gd), g2_ref[...], "bf16").astype(g_ref.dtype)
    kk = k * kk_w_ref[...]
    kk = kk * lax.rsqrt(head_sum(kk * kk) + 1e-12)
    wl = w0_ref[...] + _dot(jnp.tanh(wd), w2_ref[...], "bf16")
    al = a0_ref[...] + _dot(ad, a2_ref[...], "bf16")
    r_ref[0] = r.astype(r_ref.dtype)
    v_ref[0] = v.astype(v_ref.dtype)
    kk_ref[0] = kk.astype(kk_ref.dtype)
    k_sum = jnp.zeros_like(r)
    for d in range(2):
        a = _sigmoid(al[:, d * rw:(d + 1) * rw])
        k_d = k * (1.0 + (a - 1.0) * ka_ref[...])
        k_sum = k_sum + k_d
        lw_ref[d, 0] = -math.exp(-0.5) * _sigmoid(wl[:, d * rw:(d + 1) * rw])
        kd_ref[d, 0] = k_d.astype(kd_ref.dtype)
        be_ref[d, 0] = (kk * a).astype(be_ref.dtype)
    bonus_ref[0] = (head_sum(r * rk_ref[...] * k_sum) * v).astype(bonus_ref.dtype)


def _rw_prep(z_rw, bsz, l_ctx, l_lat, mu, w2bd, a2bd, g2, w0cat, a0cat, k_k, k_a, r_k_flat, seg, segt):
    cols = z_rw.shape[1]
    tm = l_ctx
    rw = g2.shape[1]
    l_all = l_ctx + l_lat
    nblk = l_all // tm
    lat_blk = l_lat // tm
    hb = tm // GRID_W
    lat_hblk = l_lat // GRID_W

    def main_blk(b, j):
        return jnp.where(j == 0, bsz * lat_blk + b, b * lat_blk + j - 1)

    def prev_halo(b, j):
        return b * lat_hblk + jnp.maximum((j - 1) * hb - 1, 0)

    def next_halo(b, j):
        return b * lat_hblk + jnp.minimum(jnp.maximum(j, 1) * hb, lat_hblk - 1)

    full = lambda shape: pl.BlockSpec(shape, lambda b, j: (0,) * len(shape))
    shared = jax.ShapeDtypeStruct((bsz, l_all, rw), BF16)
    lat_only = jax.ShapeDtypeStruct((bsz, l_lat, rw), BF16)
    per_dir = jax.ShapeDtypeStruct((2, bsz, l_all, rw), BF16)
    per_dir_f32 = jax.ShapeDtypeStruct((2, bsz, l_all, rw), F32)
    o_shared = pl.BlockSpec((1, tm, rw), lambda b, j: (b, j, 0))
    o_lat = pl.BlockSpec((1, tm, rw), lambda b, j: (b, jnp.maximum(j - 1, 0), 0))
    o_dir = pl.BlockSpec((2, 1, tm, rw), lambda b, j: (0, b, j, 0))
    return pl.pallas_call(
        functools.partial(_rw_prep_kernel, tm=tm, l_lat=l_lat, rw=rw),
        out_shape=(shared,) * 3 + (lat_only,) * 2 + (per_dir_f32, per_dir, per_dir),
        grid=(bsz, nblk),
        in_specs=[pl.BlockSpec((tm, cols), lambda b, j: (main_blk(b, j), 0)),
                  pl.BlockSpec((GRID_W, cols), lambda b, j: (prev_halo(b, j), 0)),
                  pl.BlockSpec((GRID_W, cols), lambda b, j: (next_halo(b, j), 0)),
                  full((1, cols)), full(w2bd.shape), full(a2bd.shape), full(g2.shape),
                  full((1, 2 * rw)), full((1, 2 * rw)), full((1, rw)), full((1, rw)), full((1, rw)),
                  full(seg.shape), full(segt.shape)],
        out_specs=(o_shared,) * 3 + (o_lat,) * 2 + (o_dir,) * 3,
        compiler_params=_cparams(("parallel", "arbitrary")),
        name="rwkv_prep",
    )(z_rw, z_rw, z_rw, mu.reshape(1, cols), w2bd, a2bd, g2, w0cat, a0cat,
      k_k.reshape(1, rw), k_a.reshape(1, rw), r_k_flat.reshape(1, rw), seg, segt)


def _stack_heads(x, head0):
    return jnp.concatenate([jnp.where(head0, x, 0.0), jnp.where(head0, 0.0, x)], axis=0)


def _rw_step_kernel(rf_ref, rb_ref, vf_ref, vb_ref, kkf_ref, kkb_ref, lwf_ref, lwb_ref, kdf_ref, kdb_ref,
                    bef_ref, beb_ref, yf_ref, yb_ref,
                    z_scr, gc_s, lhs_s, kb_s, kht_s, bht_s, vs_s, rt_s, aab_s, akr_s, arb_s, t_s, x_s, xin_s,
                    yc_s, wu_s, qm_s, yn_s, *, n_pairs, bsz):
    c_n = RW_CHUNK
    n2 = 2 * c_n

    lanes_of = lambda p: slice(p * LANES, (p + 1) * LANES)
    unit = lambda d, b, p: (d * bsz + b) * n_pairs + p
    units = [(d, b, p) for d in range(2) for b in range(bsz) for p in range(n_pairs)]

    @pl.when(pl.program_id(0) == 0)
    def _():
        z_scr[...] = jnp.zeros_like(z_scr)
        qm_s[...] = jnp.zeros_like(qm_s)
        yn_s[...] = jnp.zeros_like(yn_s)

    for d, b, p in units:
        u = unit(d, b, p)
        y_ref = yf_ref if d == 0 else yb_ref
        yz = _dot(qm_s[u], z_scr[u].astype(BF16)) + yn_s[u]
        y_ref[b, :, lanes_of(p)] = yz[:c_n] + yz[c_n:n2]
        z_scr[u] = yz[n2:]

    ri = lax.broadcasted_iota(jnp.int32, (c_n, c_n), 0)
    ci = lax.broadcasted_iota(jnp.int32, (c_n, c_n), 1)
    r2 = lax.broadcasted_iota(jnp.int32, (n2, n2), 0)
    c2 = lax.broadcasted_iota(jnp.int32, (n2, n2), 1)
    t2 = r2 % c_n
    i2 = c2 % c_n
    same_head = (r2 // c_n) == (c2 // c_n)
    diag = r2 == c2
    eye = diag.astype(F32)
    head0 = lax.broadcasted_iota(jnp.int32, (1, LANES), 1) < RW_HEAD
    tri = [(ri >= ci).astype(F32), (ri <= ci).astype(F32)]
    before = [jnp.logical_and(same_head, i2 < t2), jnp.logical_and(same_head, i2 > t2)]
    upto = [jnp.logical_or(m, diag) for m in before]

    def blk(s):
        return (r2 // s) == (c2 // s)

    srcs = [(rf_ref, vf_ref, kkf_ref, lwf_ref, kdf_ref, bef_ref), (rb_ref, vb_ref, kkb_ref, lwb_ref, kdb_ref, beb_ref)]

    for d in range(2):
        r_ref, v_ref, kk_ref, lw_ref, kd_ref, be_ref = srcs[d]
        for b in range(bsz):
            lw = lw_ref[0, b]
            cum = _dot(tri[d], lw, HIGHEST)
            tot = jnp.sum(lw, axis=0, keepdims=True)
            g_inv = jnp.exp(-cum)
            g_hat = jnp.exp(tot - cum)
            kd = kd_ref[0, b].astype(F32)
            be = be_ref[0, b].astype(F32)
            at = kk_ref[b].astype(F32) * jnp.exp(cum - lw)
            rt = r_ref[b].astype(F32) * jnp.exp(cum)
            kt = kd * g_inv
            bt = be * g_inv
            kh = kd * g_hat
            bh = be * g_hat
            gc_s[d * bsz + b] = jnp.exp(tot)
            v = v_ref[b].astype(F32)
            for p in range(n_pairs):
                u = unit(d, b, p)
                st = lambda x: _stack_heads(x[:, lanes_of(p)], head0)
                rt_p = st(rt)
                at_p = st(at).astype(BF16)
                lhs_s[u, :n2] = at_p
                lhs_s[u, n2:] = rt_p.astype(BF16)
                xin_s[u, :, :LANES] = at_p
                rt_s[u] = rt_p
                kb_s[u, :n2] = st(kt).astype(BF16)
                kb_s[u, n2:] = st(bt).astype(BF16)
                kht_s[u] = st(kh).T.astype(BF16)
                bht_s[u] = st(bh).T.astype(BF16)
                vs_s[u] = st(v).astype(BF16)

    for d, b, p in units:
        u = unit(d, b, p)
        g = _dot_nt(lhs_s[u], kb_s[u])
        a_ab = jnp.where(before[d], g[:n2, n2:], 0.0)
        akr_s[u, :n2] = jnp.where(before[d], g[:n2, :n2], 0.0).astype(BF16)
        akr_s[u, n2:] = jnp.where(upto[d], g[n2:, :n2], 0.0).astype(BF16)
        arb_s[u] = jnp.where(upto[d], g[n2:, n2:], 0.0).astype(BF16)
        aab_s[u] = a_ab
        t_s[u] = eye - jnp.where(blk(2), a_ab, 0.0)

    n_units = len(units)
    s = 2
    while s < c_n:
        off = jnp.logical_and(blk(2 * s), jnp.logical_not(blk(s)))
        for u in range(n_units):
            x_s[u] = _dot(t_s[u].astype(BF16), jnp.where(off, aab_s[u], 0.0).astype(BF16)).astype(BF16)
        for u in range(n_units):
            t = t_s[u]
            t_s[u] = t - _dot(x_s[u], t.astype(BF16))
        s *= 2

    for u in range(n_units):
        av = _dot(akr_s[u], vs_s[u])
        xin_s[u, :, LANES:] = av[:n2].astype(BF16)
        yc_s[u] = av[n2:]

    for u in range(n_units):
        wu_s[u] = _dot(t_s[u].astype(BF16), xin_s[u]).astype(BF16)
    for d, b, p in units:
        u = unit(d, b, p)
        wu = wu_s[u]
        q = _dot(arb_s[u], wu)
        bwu = _dot(bht_s[u], wu)
        qm_s[u, :n2] = (rt_s[u] - q[:, :LANES]).astype(BF16)
        qm_s[u, n2:] = (eye * gc_s[d * bsz + b][:, lanes_of(p)] - bwu[:, :LANES]).astype(BF16)
        yn_s[u, :n2] = yc_s[u] - q[:, LANES:]
        yn_s[u, n2:] = _dot(kht_s[u], vs_s[u]) - bwu[:, LANES:]


def _rw_scan(r, v, kk, lw, kd, be, l_ctx):
    bsz, l_all, rw = r.shape
    c_n = RW_CHUNK
    n_all = l_all // c_n
    n_ctx = l_ctx // c_n
    n_lat = n_all - n_ctx
    n_pairs = rw // LANES
    n_units = 2 * bsz * n_pairs
    n2 = 2 * c_n
    vm = lambda nr, cols, dt: pltpu.VMEM((n_units, nr, cols), dt)

    clamp = lambda s: jnp.minimum(s, n_all - 1)
    chunk_f = lambda s: clamp(s)
    chunk_b = lambda s: jnp.where(clamp(s) < n_ctx, n_ctx - 1 - clamp(s), n_all + n_ctx - 1 - clamp(s))
    prev = lambda s: jnp.maximum(s - 1, 0)
    out_f = lambda s: jnp.maximum(prev(s) - n_ctx, 0)
    out_b = lambda s: jnp.where(prev(s) < n_ctx, n_lat - 1, chunk_b(prev(s)) - n_ctx)

    sh_f = pl.BlockSpec((bsz, c_n, rw), lambda s: (0, chunk_f(s), 0))
    sh_b = pl.BlockSpec((bsz, c_n, rw), lambda s: (0, chunk_b(s), 0))
    pd_f = pl.BlockSpec((1, bsz, c_n, rw), lambda s: (0, 0, chunk_f(s), 0))
    pd_b = pl.BlockSpec((1, bsz, c_n, rw), lambda s: (1, 0, chunk_b(s), 0))
    y_shape = jax.ShapeDtypeStruct((bsz, n_lat * c_n, rw), F32)
    return pl.pallas_call(
        functools.partial(_rw_step_kernel, n_pairs=n_pairs, bsz=bsz),
        out_shape=(y_shape, y_shape),
        grid=(n_all + 1,),
        in_specs=[sh_f, sh_b, sh_f, sh_b, sh_f, sh_b, pd_f, pd_b, pd_f, pd_b, pd_f, pd_b],
        out_specs=(pl.BlockSpec((bsz, c_n, rw), lambda s: (0, out_f(s), 0)),
                   pl.BlockSpec((bsz, c_n, rw), lambda s: (0, out_b(s), 0))),
        scratch_shapes=[vm(n2, n2, F32),
                        pltpu.VMEM((2 * bsz, 1, rw), F32),
                        vm(2 * n2, n2, BF16),
                        vm(2 * n2, n2, BF16),
                        vm(n2, n2, BF16),
                        vm(n2, n2, BF16),
                        vm(n2, n2, BF16),
                        vm(n2, n2, F32),
                        vm(n2, n2, F32),
                        vm(2 * n2, n2, BF16),
                        vm(n2, n2, BF16),
                        vm(n2, n2, F32),
                        vm(n2, n2, BF16),
                        vm(n2, 2 * n2, BF16),
                        vm(n2, n2, F32),
                        vm(n2, 2 * n2, BF16),
                        vm(2 * n2, n2, BF16),
                        vm(2 * n2, n2, F32)],
        compiler_params=_cparams(("arbitrary",)),
        name="rwkv_scan",
    )(r, r, v, v, kk, kk, lw, lw, kd, kd, be, be)


def _s5_glu_kernel(y_ref, wa_ref, wb_ref, o_ref, h_scr):
    @pl.when(pl.program_id(1) == 0)
    def _():
        for jb in range(y_ref.shape[0]):
            h_scr[:, jb * LANES:(jb + 1) * LANES] = _gelu_tanh(y_ref[jb]).astype(BF16)

    h = h_scr[...]
    o_ref[...] = (_dot(h, wa_ref[...]) * _sigmoid(_dot(h, wb_ref[...]))).astype(o_ref.dtype)


def _s5_glu(y_blk, w, *, tm, tn):
    nb, m, _ = y_blk.shape
    k = nb * LANES
    n = w.shape[1] // 2
    nj = n // tn
    return pl.pallas_call(
        _s5_glu_kernel,
        out_shape=jax.ShapeDtypeStruct((m, n), BF16),
        grid=(m // tm, nj),
        in_specs=[pl.BlockSpec((nb, tm, LANES), lambda i, j: (0, i, 0)),
                  pl.BlockSpec((k, tn), lambda i, j: (0, j)),
                  pl.BlockSpec((k, tn), lambda i, j: (0, j + nj))],
        out_specs=pl.BlockSpec((tm, tn), lambda i, j: (i, j)),
        scratch_shapes=[pltpu.VMEM((tm, k), BF16)],
        compiler_params=_cparams(("parallel", "arbitrary")),
        name="s5_glu",
    )(y_blk, w, w)


def _merge_out_kernel(yf_ref, yb_ref, bonus_ref, g_ref, lnw_ref, lnb_ref, seg_ref, segt_ref,
                      ga_ref, gb_ref, s5_ref, wp_ref, wo_ref, x_ref, g1_ref, nw_ref, sh_ref, sc_ref,
                      h_ref, hn_ref):
    seg = seg_ref[...]
    segt = segt_ref[...]
    inv_n = 1.0 / RW_HEAD
    tm = yf_ref.shape[0]
    parts = [slice(0, tm // 2), slice(tm // 2, tm)]

    def head_mean(vals, precise):
        red = _dot_hilo if precise else (lambda a, ind: _dot(a.astype(BF16), ind))
        sums = [red(v, seg) for v in vals]
        return [red(s, segt) * inv_n for s in sums]

    y = [yf_ref[p, :] + yb_ref[p, :] for p in parts]
    dy = [a - m for a, m in zip(y, head_mean(y, True))]
    var = head_mean([a * a for a in dy], False)
    y_rw = []
    for p, a, v in zip(parts, dy, var):
        t = a * lax.rsqrt(v + GN_EPS) * lnw_ref[...] + lnb_ref[...] + bonus_ref[p, :].astype(F32)
        y_rw.append((t * g_ref[p, :].astype(F32)).astype(BF16))
    rw_out = [_dot(a, wp_ref[...]) for a in y_rw]
    merged = [(ga_ref[p, :].astype(F32) * s5_ref[p, :].astype(F32) + gb_ref[p, :].astype(F32) * r).astype(BF16)
              for p, r in zip(parts, rw_out)]
    proj = [_dot(a, wo_ref[...]) for a in merged]
    for p, a in zip(parts, proj):
        h = x_ref[p, :] + g1_ref[0] * a
        h_ref[p, :] = h
        hn_ref[p, :] = _lnmod_rows(h, nw_ref[...], sh_ref[0], sc_ref[0]).astype(hn_ref.dtype)


def _merge_out(y_f, y_b, bonus, g, ln_w, ln_b, seg, segt, gates, s5_out, w_proj, w_o, x2, g_tab, nw,
               sh_tab, sc_tab, mod_row_of_block, *, tm):
    m, rw = y_f.shape
    n = w_proj.shape[1]
    full = lambda shape: pl.BlockSpec(shape, lambda i: (0,) * len(shape))
    rows = lambda width: pl.BlockSpec((tm, width), lambda i: (i, 0))
    mod_map = lambda i: (mod_row_of_block(i), 0, 0)
    return pl.pallas_call(
        _merge_out_kernel,
        out_shape=(jax.ShapeDtypeStruct((m, n), F32), jax.ShapeDtypeStruct((m, n), BF16)),
        grid=(m // tm,),
        in_specs=[rows(rw), rows(rw), rows(rw), rows(rw),
                  full((1, rw)), full((1, rw)), full(seg.shape), full(segt.shape),
                  pl.BlockSpec((tm, n), lambda i: (i, 0)), pl.BlockSpec((tm, n), lambda i: (i, 1)), rows(n),
                  full(w_proj.shape), full(w_o.shape), rows(n),
                  pl.BlockSpec((1, 1, n), mod_map), full((1, n)),
                  pl.BlockSpec((1, 1, n), mod_map), pl.BlockSpec((1, 1, n), mod_map)],
        out_specs=(rows(n), rows(n)),
        compiler_params=_cparams(("parallel",), vmem=VMEM_LIMIT + 8 * 1024 * 1024),
        name="merge_out_proj",
    )(y_f, y_b, bonus, g, ln_w.reshape(1, rw), ln_b.reshape(1, rw), seg, segt,
      gates, gates, s5_out, w_proj, w_o, x2, g_tab, nw.reshape(1, n), sh_tab, sc_tab)


def _wres_swiglu_kernel(a_ref, w1_ref, w3_ref, o_ref, w1_scr, w3_scr):
    @pl.when(pl.program_id(1) == 0)
    def _():
        w1_scr[...] = w1_ref[...].astype(BF16)
        w3_scr[...] = w3_ref[...].astype(BF16)

    a = a_ref[...]
    o_ref[...] = (_silu(_dot(a, w1_scr[...])) * _dot(a, w3_scr[...])).astype(o_ref.dtype)


def _wres_swiglu(a, w13, d_ff, *, tm, tn):
    m, k = a.shape
    nj = d_ff // tn
    return pl.pallas_call(
        _wres_swiglu_kernel,
        out_shape=jax.ShapeDtypeStruct((m, d_ff), BF16),
        grid=(nj, m // tm),
        in_specs=[pl.BlockSpec((tm, k), lambda j, i: (i, 0)),
                  pl.BlockSpec((k, tn), lambda j, i: (0, j)),
                  pl.BlockSpec((k, tn), lambda j, i: (0, j + nj))],
        out_specs=pl.BlockSpec((tm, tn), lambda j, i: (i, j)),
        scratch_shapes=[pltpu.VMEM((k, tn), BF16), pltpu.VMEM((k, tn), BF16)],
        compiler_params=_cparams(("arbitrary", "arbitrary")),
        name="ffn_up",
    )(a, w13, w13)


def _ffn_down_kernel(a_ref, w_ref, x_ref, g_ref, nf_ref, o_ref):
    kk = pl.program_id(1)
    part = _dot(a_ref[...], w_ref[...])

    @pl.when(kk == 0)
    def _():
        o_ref[...] = part

    @pl.when(jnp.logical_and(kk > 0, kk < pl.num_programs(1) - 1))
    def _():
        o_ref[...] += part

    @pl.when(kk == pl.num_programs(1) - 1)
    def _():
        h = x_ref[...] + g_ref[0] * (o_ref[...] + part)
        ms = jnp.mean(h * h, axis=-1, keepdims=True)
        o_ref[...] = h * lax.rsqrt(ms + NORM_EPS) * nf_ref[...]


def _ffn_down(a, w, x2, g_tab, mod_row_of_block, norm_f, *, tm, tk):
    m, k = a.shape
    n = w.shape[1]
    return pl.pallas_call(
        _ffn_down_kernel,
        out_shape=jax.ShapeDtypeStruct((m, n), F32),
        grid=(m // tm, k // tk),
        in_specs=[pl.BlockSpec((tm, tk), lambda i, kk: (i, kk)),
                  pl.BlockSpec((tk, n), lambda i, kk: (kk, 0)),
                  pl.BlockSpec((tm, n), lambda i, kk: (i, 0)),
                  pl.BlockSpec((1, 1, n), lambda i, kk: (mod_row_of_block(i), 0, 0)),
                  pl.BlockSpec((1, n), lambda i, kk: (0, 0))],
        out_specs=pl.BlockSpec((tm, n), lambda i, kk: (i, 0)),
        compiler_params=_cparams(("parallel", "arbitrary"), vmem=VMEM_LIMIT + 8 * 1024 * 1024),
        name="ffn_down",
    )(a, w, x2, g_tab, norm_f.reshape(1, n))


def kernel(x, c, ctx, c_ctx, ada_w, ada_b, norm1_w, w_in, rw_mu, s5_a_re, s5_a_im, s5_log_dt, s5_b_re, s5_b_im, s5_c_re, s5_c_im, s5_d, s5_glu_w, rw_w0, rw_w2, rw_a0, rw_a2, rw_g2, rw_k_k, rw_k_a, rw_r_k, rw_ln_w, rw_ln_b, rw_proj, w_o, norm2_w, ffn_w13, ffn_w2, norm_f):
    assert ada_w.shape[0] == 1, "single-layer block"
    bsz, l_lat, d = x.shape
    l_ctx = ctx.shape[1]
    l_all = l_ctx + l_lat
    s5w = s5_d.shape[1] * s5_d.shape[2]
    rw = rw_g2.shape[2]
    shift_cols = rw_mu.shape[1]
    d_ff = ffn_w2.shape[1]

    c_rows = jnp.concatenate([c, c_ctx[None], jnp.zeros((8 - bsz - 1, d), F32)], axis=0)
    mod = _modulation(c_rows, ada_w[0], ada_b[0])
    tab = lambda k: mod[:, k * d:(k + 1) * d].reshape(8, 1, d)
    sh1, sc1, g1, sh2, sc2, g2 = (tab(k) for k in range(N_MOD))
    ctx_row = bsz

    tm_in = 512
    m_lat = bsz * l_lat
    x2 = x.reshape(m_lat, d)
    n_mix = s5w + shift_cols
    m_all = m_lat + bsz * l_ctx
    lat_row = lambda t: (lambda i: i // (l_lat // t))

    def mix_mod_row(i):
        return jnp.where(i >= m_lat // tm_in, ctx_row, i // (l_lat // tm_in))

    h_all, u_blk = _lnmod_proj(x2, ctx.reshape(bsz * l_ctx, d), norm1_w[0], sh1, sc1, mix_mod_row,
                               w_in[0], s5w, tm=tm_in)
    tm_all = m_all // 8
    z_rw = _wres_matmul(h_all, w_in[0], s5w, shift_cols, rows=m_all, tm=tm_all, tn=shift_cols // 3,
                        out_dtype=BF16, name="in_proj_rw")
    tm = 1024
    gates = _wres_matmul(h_all, w_in[0], n_mix, w_in.shape[2] - n_mix, rows=m_lat, tm=tm, tn=1024,
                         out_dtype=BF16, epilogue="sigmoid", name="in_proj_gates")

    y_blk = _s5_branch(u_blk, bsz, l_ctx, l_lat, s5_a_re[0], s5_a_im[0], s5_log_dt[0],
                               s5_b_re[0], s5_b_im[0], s5_c_re[0], s5_c_im[0], s5_d[0])
    s5_out = _s5_glu(y_blk, s5_glu_w[0].astype(BF16), tm=tm, tn=1024)

    lora = rw_w2.shape[2]
    zl = jnp.zeros((lora, rw), F32)
    w2bd = jnp.concatenate([jnp.concatenate([rw_w2[0, 0], zl], axis=1),
                            jnp.concatenate([zl, rw_w2[0, 1]], axis=1)], axis=0)
    a2bd = jnp.concatenate([jnp.concatenate([rw_a2[0, 0], zl], axis=1),
                            jnp.concatenate([zl, rw_a2[0, 1]], axis=1)], axis=0)
    head_of = jnp.arange(rw) // RW_HEAD
    seg = (head_of[:, None] == jnp.arange(LANES)[None, :]).astype(BF16)
    segt = seg.T
    r, v, kk, g, bonus, lw, kd, be = _rw_prep(
        z_rw, bsz, l_ctx, l_lat, rw_mu[0], w2bd, a2bd, rw_g2[0], rw_w0[0].reshape(1, 2 * rw),
        rw_a0[0].reshape(1, 2 * rw), rw_k_k[0], rw_k_a[0], rw_r_k[0].reshape(rw), seg, segt)
    y_f, y_b = _rw_scan(r, v, kk, lw, kd, be, l_ctx)

    tm_o = 256
    h1, h1n = _merge_out(y_f.reshape(m_lat, rw), y_b.reshape(m_lat, rw), bonus.reshape(m_lat, rw),
                         g.reshape(m_lat, rw), rw_ln_w[0], rw_ln_b[0], seg, segt, gates, s5_out,
                         rw_proj[0].astype(BF16), w_o[0].astype(BF16), x2, g1, norm2_w[0], sh2, sc2,
                         lat_row(tm_o), tm=tm_o)

    act = _wres_swiglu(h1n, ffn_w13[0], d_ff, tm=tm, tn=512)
    tm_dn = 512
    out = _ffn_down(act, ffn_w2[0].astype(BF16), h1, g2, lat_row(tm_dn), norm_f, tm=tm_dn, tk=d_ff // 2)
    return out.reshape(bsz, l_lat, d)
```

```python
import functools
import math

import jax
import jax.numpy as jnp
from jax import lax
from jax.experimental import pallas as pl
from jax.experimental.pallas import tpu as pltpu

F32 = jnp.float32
BF16 = jnp.bfloat16
HIGHEST = lax.Precision.HIGHEST

N_MOD = 6
NORM_EPS = 1e-6
GN_EPS = 64e-5
GRID_W = 64
S5_GROUP = 16
S5_STATE = 64
S5_CHUNK = 16
RW_HEAD = 64
RW_CHUNK = 64
LANES = 128
VMEM_LIMIT = 48 * 1024 * 1024


def _cparams(sem, vmem=VMEM_LIMIT):
    return pltpu.CompilerParams(dimension_semantics=sem, vmem_limit_bytes=vmem)


def _operands(a, b, precision):
    if precision == "bf16":
        return a.astype(BF16), b.astype(BF16), None
    return a, b, precision


def _dot(a, b, precision=None):
    a, b, precision = _operands(a, b, precision)
    return jnp.dot(a, b, preferred_element_type=F32, precision=precision)


def _dot_nt(a, b, precision=None):
    a, b, precision = _operands(a, b, precision)
    return lax.dot_general(a, b, (((1,), (1,)), ((), ())), preferred_element_type=F32, precision=precision)


def _dot_hilo(a, ind):
    hi = a.astype(BF16)
    lo = (a - hi.astype(F32)).astype(BF16)
    return _dot(hi, ind) + _dot(lo, ind)


def _sigmoid(x):
    return 1.0 / (1.0 + jnp.exp(-x))


def _silu(x):
    return x * _sigmoid(x)


def _gelu_tanh(x):
    c = math.sqrt(2.0 / math.pi)
    return 0.5 * x * (1.0 + jnp.tanh(c * (x + 0.044715 * (x * x * x))))


def _mod_kernel(c_ref, w_ref, b_ref, o_ref):
    o_ref[...] = _dot_hilo(_silu(c_ref[...]), w_ref[...].astype(BF16)) + b_ref[...]


def _modulation(c_rows, ada_w, ada_b):
    m, d = c_rows.shape
    n = ada_w.shape[1]
    tn = 1024
    return pl.pallas_call(
        _mod_kernel,
        out_shape=jax.ShapeDtypeStruct((m, n), F32),
        grid=(n // tn,),
        in_specs=[pl.BlockSpec((m, d), lambda j: (0, 0)),
                  pl.BlockSpec((d, tn), lambda j: (0, j)),
                  pl.BlockSpec((1, tn), lambda j: (0, j))],
        out_specs=pl.BlockSpec((m, tn), lambda j: (0, j)),
        compiler_params=_cparams(("arbitrary",)),
        name="modulation",
    )(c_rows, ada_w, ada_b.reshape(1, n))


def _lnmod_rows(x, nw, sh, sc):
    ms = jnp.mean(x * x, axis=-1, keepdims=True)
    y = x * lax.rsqrt(ms + NORM_EPS) * nw
    return y * (1.0 + sc) + sh


def _lnmod_proj_kernel(x_ref, c_ref, nw_ref, sh_ref, sc_ref, w_ref, h_ref, u_ref, w_scr, *, n_lat_blocks):
    @pl.when(pl.program_id(0) == 0)
    def _():
        w_scr[...] = w_ref[...].astype(BF16)

    rows = jnp.where(pl.program_id(0) < n_lat_blocks, x_ref[...], c_ref[...])
    h = _lnmod_rows(rows, nw_ref[...], sh_ref[0], sc_ref[0]).astype(BF16)
    h_ref[...] = h
    z = _dot(h, w_scr[...])
    for jb in range(u_ref.shape[0]):
        u_ref[jb] = z[:, jb * LANES:(jb + 1) * LANES]


def _lnmod_proj(x2, c2, nw, sh_tab, sc_tab, mod_row_of_block, w, n, *, tm):
    m, d = x2.shape
    mc = c2.shape[0]
    nx, nc = m // tm, mc // tm
    mod_map = lambda i: (mod_row_of_block(i), 0, 0)
    return pl.pallas_call(
        functools.partial(_lnmod_proj_kernel, n_lat_blocks=nx),
        out_shape=(jax.ShapeDtypeStruct((m + mc, d), BF16),
                   jax.ShapeDtypeStruct((n // LANES, m + mc, LANES), F32)),
        grid=(nx + nc,),
        in_specs=[pl.BlockSpec((tm, d), lambda i: (jnp.minimum(i, nx - 1), 0)),
                  pl.BlockSpec((tm, d), lambda i: (jnp.maximum(i - nx, 0), 0)),
                  pl.BlockSpec((1, d), lambda i: (0, 0)),
                  pl.BlockSpec((1, 1, d), mod_map),
                  pl.BlockSpec((1, 1, d), mod_map),
                  pl.BlockSpec((d, n), lambda i: (0, 0))],
        out_specs=(pl.BlockSpec((tm, d), lambda i: (i, 0)),
                   pl.BlockSpec((n // LANES, tm, LANES), lambda i: (0, i, 0))),
        scratch_shapes=[pltpu.VMEM((d, n), BF16)],
        compiler_params=_cparams(("arbitrary",)),
        name="lnmod_in_proj_s5",
    )(x2, c2, nw.reshape(1, d), sh_tab, sc_tab, w)


def _wres_mm_kernel(a_ref, w_ref, o_ref, w_scr, *, epilogue):
    @pl.when(pl.program_id(1) == 0)
    def _():
        w_scr[...] = w_ref[...].astype(BF16)

    z = _dot(a_ref[...], w_scr[...])
    if epilogue == "sigmoid":
        z = _sigmoid(z)
    o_ref[...] = z.astype(o_ref.dtype)


def _wres_matmul(a, w, col0, n, *, rows, tm, tn, out_dtype, epilogue=None, name):
    k = a.shape[1]
    assert col0 % LANES == 0 and n % tn == 0 and rows % tm == 0
    return pl.pallas_call(
        functools.partial(_wres_mm_kernel, epilogue=epilogue),
        out_shape=jax.ShapeDtypeStruct((rows, n), out_dtype),
        grid=(n // tn, rows // tm),
        in_specs=[pl.BlockSpec((tm, k), lambda j, i: (i, 0)),
                  pl.BlockSpec((pl.Element(k), pl.Element(tn)), lambda j, i: (0, pl.multiple_of(col0 + j * tn, LANES)))],
        out_specs=pl.BlockSpec((tm, tn), lambda j, i: (i, j)),
        scratch_shapes=[pltpu.VMEM((k, tn), BF16)],
        compiler_params=_cparams(("arbitrary", "arbitrary")),
        name=name,
    )(a, w)


def _s5_param_kernel(are_ref, aim_ref, ldt_ref, bre_ref, bim_ref, cre_ref, cim_ref,
                      e_ref, c_ref, m_ref, a16_ref):
    t_n, hg, p_n = S5_CHUNK, S5_GROUP, S5_STATE
    gl_n = are_ref.shape[1]
    tau = lax.broadcasted_iota(jnp.int32, (t_n, gl_n, p_n), 0).astype(F32)
    taps = {}
    for d in range(2):
        a_re = are_ref[d]
        a_im = aim_ref[d]
        dt = jnp.exp(ldt_ref[d])
        lam = a_re * dt
        th = a_im * dt
        er = jnp.exp(lam)
        ab_re = er * jnp.cos(th)
        ab_im = er * jnp.sin(th)
        den = a_re * a_re + a_im * a_im
        x_re = ab_re - 1.0
        co_re = (x_re * a_re + ab_im * a_im) / den
        co_im = (ab_im * a_re - x_re * a_im) / den

        def power(tv):
            mag = jnp.exp(tv * lam)
            return mag * jnp.cos(tv * th), mag * jnp.sin(tv * th)

        pw_re, pw_im = power(tau if d == 0 else (t_n - 1.0 - tau))
        pe_re, pe_im = power((t_n - 1.0 - tau) if d == 0 else tau)
        pc_re, pc_im = power((tau + 1.0) if d == 0 else (t_n - tau))
        mag16 = jnp.exp(float(t_n) * lam)
        a16_ref[d, 0] = mag16 * jnp.cos(float(t_n) * th)
        a16_ref[d, 1] = mag16 * jnp.sin(float(t_n) * th)

        lo, hi = 2 * d * p_n, (2 * d + 1) * p_n
        for gl in range(gl_n):
            of = lambda x: x[:, gl:gl + 1, :]
            bt_re = bre_ref[d, gl].T
            bt_im = bim_ref[d, gl].T
            bb_re = co_re[gl:gl + 1] * bt_re - co_im[gl:gl + 1] * bt_im
            bb_im = co_re[gl:gl + 1] * bt_im + co_im[gl:gl + 1] * bt_re
            c_re = cre_ref[d, gl]
            c_im = cim_ref[d, gl]
            cp_re = (c_re[None] * of(pw_re) - c_im[None] * of(pw_im)).reshape(t_n * hg, p_n)
            cp_im = (c_re[None] * of(pw_im) + c_im[None] * of(pw_re)).reshape(t_n * hg, p_n)
            taps[d, gl] = _dot_nt(bb_re, cp_re, HIGHEST) - _dot_nt(bb_im, cp_im, HIGHEST)
            e_ref[0, :, gl, :, lo:hi] = of(pe_re) * bb_re[None] - of(pe_im) * bb_im[None]
            e_ref[0, :, gl, :, hi:hi + p_n] = of(pe_re) * bb_im[None] + of(pe_im) * bb_re[None]
            c_ref[0, :, gl, :, lo:hi] = c_re[None] * of(pc_re) - c_im[None] * of(pc_im)
            c_ref[0, :, gl, :, hi:hi + p_n] = -(c_re[None] * of(pc_im) + c_im[None] * of(pc_re))

    width = t_n * hg
    lane = lax.broadcasted_iota(jnp.int32, (hg, width), 1)
    for gl in range(gl_n):
        tf, tb = taps[0, gl], taps[1, gl]
        for t in range(t_n):
            sf = t * hg
            sb = (t_n - 1 - t) * hg
            f = tf if sf == 0 else jnp.where(lane >= sf, pltpu.roll(tf, sf, 1), 0.0)
            b = tb if sb == 0 else jnp.where(lane < width - sb, pltpu.roll(tb, width - sb, 1), 0.0)
            m_ref[0, t, gl] = f + b


def _s5_params(a_re, a_im, log_dt, b_re, b_im, c_re, c_im, nb):
    g_n = a_re.shape[1]
    gl_n = g_n // nb
    p_n, hg, t_n = S5_STATE, S5_GROUP, S5_CHUNK
    ldt = jnp.broadcast_to(log_dt[:, :, None], (2, g_n, p_n))
    spec3 = pl.BlockSpec((2, gl_n, p_n), lambda j: (0, j, 0))
    spec4 = pl.BlockSpec((2, gl_n, hg, p_n), lambda j: (0, j, 0, 0))
    spec_b = pl.BlockSpec((2, gl_n, p_n, hg), lambda j: (0, j, 0, 0))
    wide = 4 * p_n
    comp = jax.ShapeDtypeStruct((nb, t_n, gl_n, hg, wide), F32)
    comp_spec = pl.BlockSpec((1, t_n, gl_n, hg, wide), lambda j: (j, 0, 0, 0, 0))
    return pl.pallas_call(
        _s5_param_kernel,
        out_shape=(comp, comp, comp, jax.ShapeDtypeStruct((2, 2, g_n, p_n), F32)),
        grid=(nb,),
        in_specs=[spec3, spec3, spec3, spec_b, spec_b, spec4, spec4],
        out_specs=(comp_spec, comp_spec, comp_spec, pl.BlockSpec((2, 2, gl_n, p_n), lambda j: (0, 0, j, 0))),
        compiler_params=_cparams(("parallel",)),
        name="s5_params",
    )(a_re, a_im, ldt, b_re, b_im, c_re, c_im)


def _expand_block_diag(comp, rep_ref, mask_ref, w_scr):
    k = w_scr.shape[0]
    period = mask_ref.shape[1]
    cb = comp.astype(BF16)
    step = 512
    for c0 in range(0, k, step):
        blk = _dot(cb, rep_ref[:, c0:c0 + step]).astype(BF16)
        for q0 in range(0, step, period):
            w_scr[:, c0 + q0:c0 + q0 + period] = blk[:, q0:q0 + period] * mask_ref[...]


def _gather_chunk_rows(u_ref, rows, dst):
    t_n = S5_CHUNK
    for t in range(t_n):
        dst[:, t * LANES:(t + 1) * LANES] = u_ref[0, pl.ds(t, rows, stride=t_n), :].astype(dst.dtype)


def _s5_ein_kernel(u_ref, ec_ref, rep_ref, mask_ref, o_ref, w_scr, u_scr):
    _expand_block_diag(ec_ref[0], rep_ref, mask_ref, w_scr)
    _gather_chunk_rows(u_ref, u_scr.shape[0], u_scr)
    o_ref[0] = _dot(u_scr[...], w_scr[...])


def _s5_chunk_inputs(u_blk, e_comp, rep_e, mask_e):
    nb, m_all, _ = u_blk.shape
    k, cw = e_comp.shape[1:]
    rows = m_all // S5_CHUNK
    return pl.pallas_call(
        _s5_ein_kernel,
        out_shape=jax.ShapeDtypeStruct((nb, rows, k), F32),
        grid=(nb,),
        in_specs=[pl.BlockSpec((1, m_all, LANES), lambda j: (j, 0, 0)),
                  pl.BlockSpec((1, k, cw), lambda j: (j, 0, 0)),
                  pl.BlockSpec(rep_e.shape, lambda j: (0, 0)),
                  pl.BlockSpec(mask_e.shape, lambda j: (0, 0))],
        out_specs=pl.BlockSpec((1, rows, k), lambda j: (j, 0, 0)),
        scratch_shapes=[pltpu.VMEM((k, k), BF16), pltpu.VMEM((rows, k), BF16)],
        compiler_params=_cparams(("parallel",)),
        name="s5_chunk_inputs",
    )(u_blk, e_comp, rep_e, mask_e)


def _s5_bscan_kernel(e_ref, a_ref, o_ref, *, bsz, n_ctx, n_lat):
    q = e_ref.shape[2] // 4
    planes = lambda row, d: (row[:, (2 * d) * q:(2 * d + 1) * q], row[:, (2 * d + 1) * q:(2 * d + 2) * q])
    coef = [planes(a_ref[0], d) for d in range(2)]
    ctx0 = bsz * n_lat

    def advance(state, rows):
        new = []
        for (sr, si), (b, d), row in zip(state, [(b, d) for b in range(bsz) for d in range(2)], rows):
            ar, ai = coef[d]
            er, ei = planes(e_ref[0, pl.ds(row, 1), :], d)
            new.append((ar * sr - ai * si + er, ar * si + ai * sr + ei))
        return tuple(new)

    def ctx_step(s, state):
        rows = [ctx0 + b * n_ctx + (s if d == 0 else n_ctx - 1 - s) for b in range(bsz) for d in range(2)]
        return advance(state, rows)

    def lat_step(s, state):
        rows = [b * n_lat + (s if d == 0 else n_lat - 1 - s) for b in range(bsz) for d in range(2)]
        for (sr, si), (b, d), row in zip(state, [(b, d) for b in range(bsz) for d in range(2)], rows):
            o_ref[0, pl.ds(row, 1), (2 * d) * q:(2 * d + 1) * q] = sr
            o_ref[0, pl.ds(row, 1), (2 * d + 1) * q:(2 * d + 2) * q] = si
        return advance(state, rows)

    zero = jnp.zeros((1, q), F32)
    state = tuple((zero, zero) for _ in range(2 * bsz))
    state = lax.fori_loop(0, n_ctx, ctx_step, state)
    lax.fori_loop(0, n_lat, lat_step, state)


def _s5_bscan(e_rows, a_rows, bsz, n_ctx, n_lat):
    nb, rows, k = e_rows.shape
    return pl.pallas_call(
        functools.partial(_s5_bscan_kernel, bsz=bsz, n_ctx=n_ctx, n_lat=n_lat),
        out_shape=jax.ShapeDtypeStruct((nb, bsz * n_lat, k), F32),
        grid=(nb,),
        in_specs=[pl.BlockSpec((1, rows, k), lambda j: (j, 0, 0)),
                  pl.BlockSpec((1, 1, k), lambda j: (j, 0, 0))],
        out_specs=pl.BlockSpec((1, bsz * n_lat, k), lambda j: (j, 0, 0)),
        compiler_params=_cparams(("parallel",)),
        name="s5_scan",
    )(e_rows, a_rows)


def _s5_out_kernel(u_ref, s_ref, mc_ref, cc_ref, d_ref, rep_m_ref, mask_m_ref, rep_e_ref, mask_e_ref,
                   o_ref, wm_scr, wc_scr, u_scr):
    @pl.when(pl.program_id(1) == 0)
    def _():
        _expand_block_diag(mc_ref[0], rep_m_ref, mask_m_ref, wm_scr)
        _expand_block_diag(cc_ref[0], rep_e_ref, mask_e_ref, wc_scr)

    t_n = S5_CHUNK
    rows = s_ref.shape[1]
    _gather_chunk_rows(u_ref, rows, u_scr)
    u = u_scr[...]
    y = _dot(u.astype(BF16), wm_scr[...]) + _dot_nt(s_ref[0].astype(BF16), wc_scr[...])
    y = y + u * d_ref[0]
    for t in range(t_n):
        o_ref[0, pl.ds(t, rows, stride=t_n), :] = y[:, t * LANES:(t + 1) * LANES]


def _s5_outputs(u_blk, s_rows, m_comp, c_comp, d_rows, rep_m, mask_m, rep_e, mask_e):
    nb = u_blk.shape[0]
    rows, k = s_rows.shape[1:]
    cw = m_comp.shape[2]
    tr = rows // 2
    tok = tr * S5_CHUNK
    const = lambda a: pl.BlockSpec(a.shape, lambda j, i: (0, 0))
    return pl.pallas_call(
        _s5_out_kernel,
        out_shape=jax.ShapeDtypeStruct((nb, rows * S5_CHUNK, LANES), F32),
        grid=(nb, rows // tr),
        in_specs=[pl.BlockSpec((1, tok, LANES), lambda j, i: (j, i, 0)),
                  pl.BlockSpec((1, tr, k), lambda j, i: (j, i, 0)),
                  pl.BlockSpec((1, k, cw), lambda j, i: (j, 0, 0)),
                  pl.BlockSpec((1, k, cw), lambda j, i: (j, 0, 0)),
                  pl.BlockSpec((1, 1, k), lambda j, i: (j, 0, 0)),
                  const(rep_m), const(mask_m), const(rep_e), const(mask_e)],
        out_specs=pl.BlockSpec((1, tok, LANES), lambda j, i: (j, i, 0)),
        scratch_shapes=[pltpu.VMEM((k, k), BF16), pltpu.VMEM((k, k), BF16), pltpu.VMEM((tr, k), F32)],
        compiler_params=_cparams(("parallel", "arbitrary")),
        name="s5_outputs",
    )(u_blk, s_rows, m_comp, c_comp, d_rows, rep_m, mask_m, rep_e, mask_e)


def _s5_branch(u_blk, bsz, l_ctx, l_lat, a_re, a_im, log_dt, b_re, b_im, c_re, c_im, s5_d):
    nb, m_all, _ = u_blk.shape
    hg, t_n, p_n = S5_GROUP, S5_CHUNK, S5_STATE
    g_n = a_re.shape[1]
    gl_n = g_n // nb
    k = t_n * LANES
    n_lat = l_lat // t_n
    n_ctx = l_ctx // t_n
    e_c, c_c, m_c, a16 = _s5_params(a_re, a_im, log_dt, b_re, b_im, c_re, c_im, nb)
    cw = 4 * p_n
    e_comp, c_comp, m_comp = (a.reshape(nb, k, cw) for a in (e_c, c_c, m_c))

    row_gl = (jnp.arange(k) // hg) % gl_n
    col = jnp.arange(k)
    src = jnp.arange(cw)
    rep_e = ((src[:, None] // p_n == col[None, :] // (gl_n * p_n)) & (src[:, None] % p_n == col[None, :] % p_n)).astype(BF16)
    mask_e = (row_gl[:, None] == (jnp.arange(gl_n * p_n)[None, :] // p_n)).astype(BF16)
    rep_m = ((src[:, None] // hg == col[None, :] // (gl_n * hg)) & (src[:, None] % hg == col[None, :] % hg)).astype(BF16)
    mask_m = (row_gl[:, None] == (jnp.arange(gl_n * hg)[None, :] // hg)).astype(BF16)

    d_rows = jnp.tile(s5_d.reshape(nb, 1, gl_n * hg), (1, 1, t_n))
    a_rows = jnp.transpose(a16.reshape(2, 2, nb, gl_n * p_n), (2, 0, 1, 3)).reshape(nb, 1, 4 * gl_n * p_n)

    e_rows = _s5_chunk_inputs(u_blk, e_comp, rep_e, mask_e)
    s_rows = _s5_bscan(e_rows, a_rows, bsz, n_ctx, n_lat)
    return _s5_outputs(u_blk, s_rows, m_comp, c_comp, d_rows, rep_m, mask_m, rep_e, mask_e)


def _rw_prep_kernel(z_ref, zp_ref, zn_ref, mu_ref, w2_ref, a2_ref, g2_ref, w0_ref, a0_ref,
                    kk_w_ref, ka_ref, rk_ref, seg_ref, segt_ref,
                    r_ref, v_ref, kk_ref, g_ref, bonus_ref, lw_ref, kd_ref, be_ref,
                    *, tm, l_lat, rw):
    j = pl.program_id(1)
    z = z_ref[...].astype(F32)
    lat = j > 0
    tl = lax.broadcasted_iota(jnp.int32, (tm, 1), 0)
    tok = (j - 1) * tm + tl
    col = tl % GRID_W
    m_l = jnp.where(lat, col, tl) > 0
    m_r = jnp.where(lat, col - (GRID_W - 1), tl - (tm - 1)) < 0
    m_u = jnp.logical_and(lat, tok >= GRID_W)
    m_d = jnp.logical_and(lat, tok < l_lat - GRID_W)
    z_ext = jnp.concatenate([zp_ref[...], z_ref[...], zn_ref[...]], axis=0)
    rel = (lax.broadcasted_iota(jnp.int32, (tm, tm + 2 * GRID_W), 1) - GRID_W
           - lax.broadcasted_iota(jnp.int32, (tm, tm + 2 * GRID_W), 0))
    pick = (jnp.logical_and(rel == -1, m_l) | jnp.logical_and(rel == 1, m_r)
            | jnp.logical_and(rel == -GRID_W, m_u) | jnp.logical_and(rel == GRID_W, m_d))
    s = _dot(jnp.where(pick, 1.0, 0.0).astype(z_ext.dtype), z_ext)
    cnt = (m_l.astype(F32) + m_r.astype(F32)) + (m_u.astype(F32) + m_d.astype(F32))
    zs = z + (s * (1.0 / cnt) - z) * mu_ref[...]

    r = zs[:, 0:rw]
    k = zs[:, rw:2 * rw]
    v = zs[:, 2 * rw:3 * rw]
    o = 3 * rw
    wd = zs[:, o:o + LANES]
    ad = zs[:, o + LANES:o + 2 * LANES]
    gd = zs[:, o + 2 * LANES:o + 3 * LANES]

    seg = seg_ref[...]
    segt = segt_ref[...]

    def head_sum(t):
        return _dot(_dot(t.astype(BF16), seg).astype(BF16), segt)

    g_ref[0] = _dot(_sigmoid(gd), g2_ref[...], "bf16").astype(g_ref.dtype)
    kk = k * kk_w_ref[...]
    kk = kk * lax.rsqrt(head_sum(kk * kk) + 1e-12)
    wl = w0_ref[...] + _dot(jnp.tanh(wd), w2_ref[...], "bf16")
    al = a0_ref[...] + _dot(ad, a2_ref[...], "bf16")
    r_ref[0] = r.astype(r_ref.dtype)
    v_ref[0] = v.astype(v_ref.dtype)
    kk_ref[0] = kk.astype(kk_ref.dtype)
    k_sum = jnp.zeros_like(r)
    for d in range(2):
        a = _sigmoid(al[:, d * rw:(d + 1) * rw])
        k_d = k * (1.0 + (a - 1.0) * ka_ref[...])
        k_sum = k_sum + k_d
        lw_ref[d, 0] = -math.exp(-0.5) * _sigmoid(wl[:, d * rw:(d + 1) * rw])
        kd_ref[d, 0] = k_d.astype(kd_ref.dtype)
        be_ref[d, 0] = (kk * a).astype(be_ref.dtype)
    bonus_ref[0] = (head_sum(r * rk_ref[...] * k_sum) * v).astype(bonus_ref.dtype)


def _rw_prep(z_rw, bsz, l_ctx, l_lat, mu, w2bd, a2bd, g2, w0cat, a0cat, k_k, k_a, r_k_flat, seg, segt):
    cols = z_rw.shape[1]
    tm = l_ctx
    rw = g2.shape[1]
    l_all = l_ctx + l_lat
    nblk = l_all // tm
    lat_blk = l_lat // tm
    hb = tm // GRID_W
    lat_hblk = l_lat // GRID_W

    def main_blk(b, j):
        return jnp.where(j == 0, bsz * lat_blk + b, b * lat_blk + j - 1)

    def prev_halo(b, j):
        return b * lat_hblk + jnp.maximum((j - 1) * hb - 1, 0)

    def next_halo(b, j):
        return b * lat_hblk + jnp.minimum(jnp.maximum(j, 1) * hb, lat_hblk - 1)

    full = lambda shape: pl.BlockSpec(shape, lambda b, j: (0,) * len(shape))
    shared = jax.ShapeDtypeStruct((bsz, l_all, rw), BF16)
    lat_only = jax.ShapeDtypeStruct((bsz, l_lat, rw), BF16)
    per_dir = jax.ShapeDtypeStruct((2, bsz, l_all, rw), BF16)
    per_dir_f32 = jax.ShapeDtypeStruct((2, bsz, l_all, rw), F32)
    o_shared = pl.BlockSpec((1, tm, rw), lambda b, j: (b, j, 0))
    o_lat = pl.BlockSpec((1, tm, rw), lambda b, j: (b, jnp.maximum(j - 1, 0), 0))
    o_dir = pl.BlockSpec((2, 1, tm, rw), lambda b, j: (0, b, j, 0))
    return pl.pallas_call(
        functools.partial(_rw_prep_kernel, tm=tm, l_lat=l_lat, rw=rw),
        out_shape=(shared,) * 3 + (lat_only,) * 2 + (per_dir_f32, per_dir, per_dir),
        grid=(bsz, nblk),
        in_specs=[pl.BlockSpec((tm, cols), lambda b, j: (main_blk(b, j), 0)),
                  pl.BlockSpec((GRID_W, cols), lambda b, j: (prev_halo(b, j), 0)),
                  pl.BlockSpec((GRID_W, cols), lambda b, j: (next_halo(b, j), 0)),
                  full((1, cols)), full(w2bd.shape), full(a2bd.shape), full(g2.shape),
                  full((1, 2 * rw)), full((1, 2 * rw)), full((1, rw)), full((1, rw)), full((1, rw)),
                  full(seg.shape), full(segt.shape)],
        out_specs=(o_shared,) * 3 + (o_lat,) * 2 + (o_dir,) * 3,
        compiler_params=_cparams(("parallel", "arbitrary")),
        name="rwkv_prep",
    )(z_rw, z_rw, z_rw, mu.reshape(1, cols), w2bd, a2bd, g2, w0cat, a0cat,
      k_k.reshape(1, rw), k_a.reshape(1, rw), r_k_flat.reshape(1, rw), seg, segt)


def _stack_heads(x, head0):
    return jnp.concatenate([jnp.where(head0, x, 0.0), jnp.where(head0, 0.0, x)], axis=0)


def _rw_step_kernel(rf_ref, rb_ref, vf_ref, vb_ref, kkf_ref, kkb_ref, lwf_ref, lwb_ref, kdf_ref, kdb_ref,
                    bef_ref, beb_ref, yf_ref, yb_ref,
                    z_scr, gc_s, lhs_s, kb_s, kht_s, bht_s, vs_s, rt_s, aab_s, akr_s, arb_s, t_s, x_s, xin_s,
                    yc_s, wu_s, qm_s, yn_s, *, n_pairs, bsz):
    c_n = RW_CHUNK
    n2 = 2 * c_n

    lanes_of = lambda p: slice(p * LANES, (p + 1) * LANES)
    unit = lambda d, b, p: (d * bsz + b) * n_pairs + p
    units = [(d, b, p) for d in range(2) for b in range(bsz) for p in range(n_pairs)]

    @pl.when(pl.program_id(0) == 0)
    def _():
        z_scr[...] = jnp.zeros_like(z_scr)
        qm_s[...] = jnp.zeros_like(qm_s)
        yn_s[...] = jnp.zeros_like(yn_s)

    for d, b, p in units:
        u = unit(d, b, p)
        y_ref = yf_ref if d == 0 else yb_ref
        yz = _dot(qm_s[u], z_scr[u].astype(BF16)) + yn_s[u]
        y_ref[b, :, lanes_of(p)] = yz[:c_n] + yz[c_n:n2]
        z_scr[u] = yz[n2:]

    ri = lax.broadcasted_iota(jnp.int32, (c_n, c_n), 0)
    ci = lax.broadcasted_iota(jnp.int32, (c_n, c_n), 1)
    r2 = lax.broadcasted_iota(jnp.int32, (n2, n2), 0)
    c2 = lax.broadcasted_iota(jnp.int32, (n2, n2), 1)
    t2 = r2 % c_n
    i2 = c2 % c_n
    same_head = (r2 // c_n) == (c2 // c_n)
    diag = r2 == c2
    eye = diag.astype(F32)
    head0 = lax.broadcasted_iota(jnp.int32, (1, LANES), 1) < RW_HEAD
    tri = [(ri >= ci).astype(F32), (ri <= ci).astype(F32)]
    before = [jnp.logical_and(same_head, i2 < t2), jnp.logical_and(same_head, i2 > t2)]
    upto = [jnp.logical_or(m, diag) for m in before]

    def blk(s):
        return (r2 // s) == (c2 // s)

    srcs = [(rf_ref, vf_ref, kkf_ref, lwf_ref, kdf_ref, bef_ref), (rb_ref, vb_ref, kkb_ref, lwb_ref, kdb_ref, beb_ref)]

    for d in range(2):
        r_ref, v_ref, kk_ref, lw_ref, kd_ref, be_ref = srcs[d]
        for b in range(bsz):
            lw = lw_ref[0, b]
            cum = _dot(tri[d], lw, HIGHEST)
            tot = jnp.sum(lw, axis=0, keepdims=True)
            g_inv = jnp.exp(-cum)
            g_hat = jnp.exp(tot - cum)
            kd = kd_ref[0, b].astype(F32)
            be = be_ref[0, b].astype(F32)
            at = kk_ref[b].astype(F32) * jnp.exp(cum - lw)
            rt = r_ref[b].astype(F32) * jnp.exp(cum)
            kt = kd * g_inv
            bt = be * g_inv
            kh = kd * g_hat
            bh = be * g_hat
            gc_s[d * bsz + b] = jnp.exp(tot)
            v = v_ref[b].astype(F32)
            for p in range(n_pairs):
                u = unit(d, b, p)
                st = lambda x: _stack_heads(x[:, lanes_of(p)], head0)
                rt_p = st(rt)
                at_p = st(at).astype(BF16)
                lhs_s[u, :n2] = at_p
                lhs_s[u, n2:] = rt_p.astype(BF16)
                xin_s[u, :, :LANES] = at_p
                rt_s[u] = rt_p
                kb_s[u, :n2] = st(kt).astype(BF16)
                kb_s[u, n2:] = st(bt).astype(BF16)
                kht_s[u] = st(kh).T.astype(BF16)
                bht_s[u] = st(bh).T.astype(BF16)
                vs_s[u] = st(v).astype(BF16)

    for d, b, p in units:
        u = unit(d, b, p)
        g = _dot_nt(lhs_s[u], kb_s[u])
        a_ab = jnp.where(before[d], g[:n2, n2:], 0.0)
        akr_s[u, :n2] = jnp.where(before[d], g[:n2, :n2], 0.0).astype(BF16)
        akr_s[u, n2:] = jnp.where(upto[d], g[n2:, :n2], 0.0).astype(BF16)
        arb_s[u] = jnp.where(upto[d], g[n2:, n2:], 0.0).astype(BF16)
        aab_s[u] = a_ab
        t_s[u] = eye - jnp.where(blk(2), a_ab, 0.0)

    n_units = len(units)
    s = 2
    while s < c_n:
        off = jnp.logical_and(blk(2 * s), jnp.logical_not(blk(s)))
        for u in range(n_units):
            x_s[u] = _dot(t_s[u].astype(BF16), jnp.where(off, aab_s[u], 0.0).astype(BF16)).astype(BF16)
        for u in range(n_units):
            t = t_s[u]
            t_s[u] = t - _dot(x_s[u], t.astype(BF16))
        s *= 2

    for u in range(n_units):
        av = _dot(akr_s[u], vs_s[u])
        xin_s[u, :, LANES:] = av[:n2].astype(BF16)
        yc_s[u] = av[n2:]

    for u in range(n_units):
        wu_s[u] = _dot(t_s[u].astype(BF16), xin_s[u]).astype(BF16)
    for d, b, p in units:
        u = unit(d, b, p)
        wu = wu_s[u]
        q = _dot(arb_s[u], wu)
        bwu = _dot(bht_s[u], wu)
        qm_s[u, :n2] = (rt_s[u] - q[:, :LANES]).astype(BF16)
        qm_s[u, n2:] = (eye * gc_s[d * bsz + b][:, lanes_of(p)] - bwu[:, :LANES]).astype(BF16)
        yn_s[u, :n2] = yc_s[u] - q[:, LANES:]
        yn_s[u, n2:] = _dot(kht_s[u], vs_s[u]) - bwu[:, LANES:]


def _rw_scan(r, v, kk, lw, kd, be, l_ctx):
    bsz, l_all, rw = r.shape
    c_n = RW_CHUNK
    n_all = l_all // c_n
    n_ctx = l_ctx // c_n
    n_lat = n_all - n_ctx
    n_pairs = rw // LANES
    n_units = 2 * bsz * n_pairs
    n2 = 2 * c_n
    vm = lambda nr, cols, dt: pltpu.VMEM((n_units, nr, cols), dt)

    clamp = lambda s: jnp.minimum(s, n_all - 1)
    chunk_f = lambda s: clamp(s)
    chunk_b = lambda s: jnp.where(clamp(s) < n_ctx, n_ctx - 1 - clamp(s), n_all + n_ctx - 1 - clamp(s))
    prev = lambda s: jnp.maximum(s - 1, 0)
    out_f = lambda s: jnp.maximum(prev(s) - n_ctx, 0)
    out_b = lambda s: jnp.where(prev(s) < n_ctx, n_lat - 1, chunk_b(prev(s)) - n_ctx)

    sh_f = pl.BlockSpec((bsz, c_n, rw), lambda s: (0, chunk_f(s), 0))
    sh_b = pl.BlockSpec((bsz, c_n, rw), lambda s: (0, chunk_b(s), 0))
    pd_f = pl.BlockSpec((1, bsz, c_n, rw), lambda s: (0, 0, chunk_f(s), 0))
    pd_b = pl.BlockSpec((1, bsz, c_n, rw), lambda s: (1, 0, chunk_b(s), 0))
    y_shape = jax.ShapeDtypeStruct((bsz, n_lat * c_n, rw), F32)
    return pl.pallas_call(
        functools.partial(_rw_step_kernel, n_pairs=n_pairs, bsz=bsz),
        out_shape=(y_shape, y_shape),
        grid=(n_all + 1,),
        in_specs=[sh_f, sh_b, sh_f, sh_b, sh_f, sh_b, pd_f, pd_b, pd_f, pd_b, pd_f, pd_b],
        out_specs=(pl.BlockSpec((bsz, c_n, rw), lambda s: (0, out_f(s), 0)),
                   pl.BlockSpec((bsz, c_n, rw), lambda s: (0, out_b(s), 0))),
        scratch_shapes=[vm(n2, n2, F32),
                        pltpu.VMEM((2 * bsz, 1, rw), F32),
                        vm(2 * n2, n2, BF16),
                        vm(2 * n2, n2, BF16),
                        vm(n2, n2, BF16),
                        vm(n2, n2, BF16),
                        vm(n2, n2, BF16),
                        vm(n2, n2, F32),
                        vm(n2, n2, F32),
                        vm(2 * n2, n2, BF16),
                        vm(n2, n2, BF16),
                        vm(n2, n2, F32),
                        vm(n2, n2, BF16),
                        vm(n2, 2 * n2, BF16),
                        vm(n2, n2, F32),
                        vm(n2, 2 * n2, BF16),
                        vm(2 * n2, n2, BF16),
                        vm(2 * n2, n2, F32)],
        compiler_params=_cparams(("arbitrary",)),
        name="rwkv_scan",
    )(r, r, v, v, kk, kk, lw, lw, kd, kd, be, be)


def _s5_glu_kernel(y_ref, wa_ref, wb_ref, o_ref, h_scr):
    @pl.when(pl.program_id(1) == 0)
    def _():
        for jb in range(y_ref.shape[0]):
            h_scr[:, jb * LANES:(jb + 1) * LANES] = _gelu_tanh(y_ref[jb]).astype(BF16)

    h = h_scr[...]
    o_ref[...] = (_dot(h, wa_ref[...]) * _sigmoid(_dot(h, wb_ref[...]))).astype(o_ref.dtype)


def _s5_glu(y_blk, w, *, tm, tn):
    nb, m, _ = y_blk.shape
    k = nb * LANES
    n = w.shape[1] // 2
    nj = n // tn
    return pl.pallas_call(
        _s5_glu_kernel,
        out_shape=jax.ShapeDtypeStruct((m, n), BF16),
        grid=(m // tm, nj),
        in_specs=[pl.BlockSpec((nb, tm, LANES), lambda i, j: (0, i, 0)),
                  pl.BlockSpec((k, tn), lambda i, j: (0, j)),
                  pl.BlockSpec((k, tn), lambda i, j: (0, j + nj))],
        out_specs=pl.BlockSpec((tm, tn), lambda i, j: (i, j)),
        scratch_shapes=[pltpu.VMEM((tm, k), BF16)],
        compiler_params=_cparams(("parallel", "arbitrary")),
        name="s5_glu",
    )(y_blk, w, w)


def _merge_out_kernel(yf_ref, yb_ref, bonus_ref, g_ref, lnw_ref, lnb_ref, seg_ref, segt_ref,
                      ga_ref, gb_ref, s5_ref, wp_ref, wo_ref, x_ref, g1_ref, nw_ref, sh_ref, sc_ref,
                      h_ref, hn_ref):
    seg = seg_ref[...]
    segt = segt_ref[...]
    inv_n = 1.0 / RW_HEAD
    tm = yf_ref.shape[0]
    parts = [slice(0, tm // 2), slice(tm // 2, tm)]

    def head_mean(vals, precise):
        red = _dot_hilo if precise else (lambda a, ind: _dot(a.astype(BF16), ind))
        sums = [red(v, seg) for v in vals]
        return [red(s, segt) * inv_n for s in sums]

    y = [yf_ref[p, :] + yb_ref[p, :] for p in parts]
    dy = [a - m for a, m in zip(y, head_mean(y, True))]
    var = head_mean([a * a for a in dy], False)
    y_rw = []
    for p, a, v in zip(parts, dy, var):
        t = a * lax.rsqrt(v + GN_EPS) * lnw_ref[...] + lnb_ref[...] + bonus_ref[p, :].astype(F32)
        y_rw.append((t * g_ref[p, :].astype(F32)).astype(BF16))
    rw_out = [_dot(a, wp_ref[...]) for a in y_rw]
    merged = [(ga_ref[p, :].astype(F32) * s5_ref[p, :].astype(F32) + gb_ref[p, :].astype(F32) * r).astype(BF16)
              for p, r in zip(parts, rw_out)]
    proj = [_dot(a, wo_ref[...]) for a in merged]
    for p, a in zip(parts, proj):
        h = x_ref[p, :] + g1_ref[0] * a
        h_ref[p, :] = h
        hn_ref[p, :] = _lnmod_rows(h, nw_ref[...], sh_ref[0], sc_ref[0]).astype(hn_ref.dtype)


def _merge_out(y_f, y_b, bonus, g, ln_w, ln_b, seg, segt, gates, s5_out, w_proj, w_o, x2, g_tab, nw,
               sh_tab, sc_tab, mod_row_of_block, *, tm):
    m, rw = y_f.shape
    n = w_proj.shape[1]
    full = lambda shape: pl.BlockSpec(shape, lambda i: (0,) * len(shape))
    rows = lambda width: pl.BlockSpec((tm, width), lambda i: (i, 0))
    mod_map = lambda i: (mod_row_of_block(i), 0, 0)
    return pl.pallas_call(
        _merge_out_kernel,
        out_shape=(jax.ShapeDtypeStruct((m, n), F32), jax.ShapeDtypeStruct((m, n), BF16)),
        grid=(m // tm,),
        in_specs=[rows(rw), rows(rw), rows(rw), rows(rw),
                  full((1, rw)), full((1, rw)), full(seg.shape), full(segt.shape),
                  pl.BlockSpec((tm, n), lambda i: (i, 0)), pl.BlockSpec((tm, n), lambda i: (i, 1)), rows(n),
                  full(w_proj.shape), full(w_o.shape), rows(n),
                  pl.BlockSpec((1, 1, n), mod_map), full((1, n)),
                  pl.BlockSpec((1, 1, n), mod_map), pl.BlockSpec((1, 1, n), mod_map)],
        out_specs=(rows(n), rows(n)),
        compiler_params=_cparams(("parallel",), vmem=VMEM_LIMIT + 8 * 1024 * 1024),
        name="merge_out_proj",
    )(y_f, y_b, bonus, g, ln_w.reshape(1, rw), ln_b.reshape(1, rw), seg, segt,
      gates, gates, s5_out, w_proj, w_o, x2, g_tab, nw.reshape(1, n), sh_tab, sc_tab)


def _wres_swiglu_kernel(a_ref, w1_ref, w3_ref, o_ref, w1_scr, w3_scr):
    @pl.when(pl.program_id(1) == 0)
    def _():
        w1_scr[...] = w1_ref[...].astype(BF16)
        w3_scr[...] = w3_ref[...].astype(BF16)

    a = a_ref[...]
    o_ref[...] = (_silu(_dot(a, w1_scr[...])) * _dot(a, w3_scr[...])).astype(o_ref.dtype)


def _wres_swiglu(a, w13, d_ff, *, tm, tn):
    m, k = a.shape
    nj = d_ff // tn
    return pl.pallas_call(
        _wres_swiglu_kernel,
        out_shape=jax.ShapeDtypeStruct((m, d_ff), BF16),
        grid=(nj, m // tm),
        in_specs=[pl.BlockSpec((tm, k), lambda j, i: (i, 0)),
                  pl.BlockSpec((k, tn), lambda j, i: (0, j)),
                  pl.BlockSpec((k, tn), lambda j, i: (0, j + nj))],
        out_specs=pl.BlockSpec((tm, tn), lambda j, i: (i, j)),
        scratch_shapes=[pltpu.VMEM((k, tn), BF16), pltpu.VMEM((k, tn), BF16)],
        compiler_params=_cparams(("arbitrary", "arbitrary"), vmem=VMEM_LIMIT + 8 * 1024 * 1024),
        name="ffn_up",
    )(a, w13, w13)


def _ffn_down_kernel(a_ref, w_ref, x_ref, g_ref, nf_ref, o_ref):
    kk = pl.program_id(1)
    part = _dot(a_ref[...], w_ref[...])

    @pl.when(kk == 0)
    def _():
        o_ref[...] = part

    @pl.when(jnp.logical_and(kk > 0, kk < pl.num_programs(1) - 1))
    def _():
        o_ref[...] += part

    @pl.when(kk == pl.num_programs(1) - 1)
    def _():
        h = x_ref[...] + g_ref[0] * (o_ref[...] + part)
        ms = jnp.mean(h * h, axis=-1, keepdims=True)
        o_ref[...] = h * lax.rsqrt(ms + NORM_EPS) * nf_ref[...]


def _ffn_down(a, w, x2, g_tab, mod_row_of_block, norm_f, *, tm, tk):
    m, k = a.shape
    n = w.shape[1]
    return pl.pallas_call(
        _ffn_down_kernel,
        out_shape=jax.ShapeDtypeStruct((m, n), F32),
        grid=(m // tm, k // tk),
        in_specs=[pl.BlockSpec((tm, tk), lambda i, kk: (i, kk)),
                  pl.BlockSpec((tk, n), lambda i, kk: (kk, 0)),
                  pl.BlockSpec((tm, n), lambda i, kk: (i, 0)),
                  pl.BlockSpec((1, 1, n), lambda i, kk: (mod_row_of_block(i), 0, 0)),
                  pl.BlockSpec((1, n), lambda i, kk: (0, 0))],
        out_specs=pl.BlockSpec((tm, n), lambda i, kk: (i, 0)),
        compiler_params=_cparams(("parallel", "arbitrary"), vmem=VMEM_LIMIT + 8 * 1024 * 1024),
        name="ffn_down",
    )(a, w, x2, g_tab, norm_f.reshape(1, n))


def kernel(x, c, ctx, c_ctx, ada_w, ada_b, norm1_w, w_in, rw_mu, s5_a_re, s5_a_im, s5_log_dt, s5_b_re, s5_b_im, s5_c_re, s5_c_im, s5_d, s5_glu_w, rw_w0, rw_w2, rw_a0, rw_a2, rw_g2, rw_k_k, rw_k_a, rw_r_k, rw_ln_w, rw_ln_b, rw_proj, w_o, norm2_w, ffn_w13, ffn_w2, norm_f):
    assert ada_w.shape[0] == 1, "single-layer block"
    bsz, l_lat, d = x.shape
    l_ctx = ctx.shape[1]
    l_all = l_ctx + l_lat
    s5w = s5_d.shape[1] * s5_d.shape[2]
    rw = rw_g2.shape[2]
    shift_cols = rw_mu.shape[1]
    d_ff = ffn_w2.shape[1]

    c_rows = jnp.concatenate([c, c_ctx[None], jnp.zeros((8 - bsz - 1, d), F32)], axis=0)
    mod = _modulation(c_rows, ada_w[0], ada_b[0])
    tab = lambda k: mod[:, k * d:(k + 1) * d].reshape(8, 1, d)
    sh1, sc1, g1, sh2, sc2, g2 = (tab(k) for k in range(N_MOD))
    ctx_row = bsz

    tm_in = 512
    m_lat = bsz * l_lat
    x2 = x.reshape(m_lat, d)
    n_mix = s5w + shift_cols
    m_all = m_lat + bsz * l_ctx
    lat_row = lambda t: (lambda i: i // (l_lat // t))

    def mix_mod_row(i):
        return jnp.where(i >= m_lat // tm_in, ctx_row, i // (l_lat // tm_in))

    h_all, u_blk = _lnmod_proj(x2, ctx.reshape(bsz * l_ctx, d), norm1_w[0], sh1, sc1, mix_mod_row,
                               w_in[0], s5w, tm=tm_in)
    tm_all = m_all // 8
    z_rw = _wres_matmul(h_all, w_in[0], s5w, shift_cols, rows=m_all, tm=tm_all, tn=shift_cols // 3,
                        out_dtype=BF16, name="in_proj_rw")
    tm = 1024
    gates = _wres_matmul(h_all, w_in[0], n_mix, w_in.shape[2] - n_mix, rows=m_lat, tm=tm, tn=1024,
                         out_dtype=BF16, epilogue="sigmoid", name="in_proj_gates")

    y_blk = _s5_branch(u_blk, bsz, l_ctx, l_lat, s5_a_re[0], s5_a_im[0], s5_log_dt[0],
                               s5_b_re[0], s5_b_im[0], s5_c_re[0], s5_c_im[0], s5_d[0])
    s5_out = _s5_glu(y_blk, s5_glu_w[0].astype(BF16), tm=tm, tn=1024)

    lora = rw_w2.shape[2]
    zl = jnp.zeros((lora, rw), F32)
    w2bd = jnp.concatenate([jnp.concatenate([rw_w2[0, 0], zl], axis=1),
                            jnp.concatenate([zl, rw_w2[0, 1]], axis=1)], axis=0)
    a2bd = jnp.concatenate([jnp.concatenate([rw_a2[0, 0], zl], axis=1),
                            jnp.concatenate([zl, rw_a2[0, 1]], axis=1)], axis=0)
    head_of = jnp.arange(rw) // RW_HEAD
    seg = (head_of[:, None] == jnp.arange(LANES)[None, :]).astype(BF16)
    segt = seg.T
    r, v, kk, g, bonus, lw, kd, be = _rw_prep(
        z_rw, bsz, l_ctx, l_lat, rw_mu[0], w2bd, a2bd, rw_g2[0], rw_w0[0].reshape(1, 2 * rw),
        rw_a0[0].reshape(1, 2 * rw), rw_k_k[0], rw_k_a[0], rw_r_k[0].reshape(rw), seg, segt)
    y_f, y_b = _rw_scan(r, v, kk, lw, kd, be, l_ctx)

    tm_o = 256
    h1, h1n = _merge_out(y_f.reshape(m_lat, rw), y_b.reshape(m_lat, rw), bonus.reshape(m_lat, rw),
                         g.reshape(m_lat, rw), rw_ln_w[0], rw_ln_b[0], seg, segt, gates, s5_out,
                         rw_proj[0].astype(BF16), w_o[0].astype(BF16), x2, g1, norm2_w[0], sh2, sc2,
                         lat_row(tm_o), tm=tm_o)

    act = _wres_swiglu(h1n, ffn_w13[0], d_ff, tm=2 * tm, tn=512)
    tm_dn = 512
    out = _ffn_down(act, ffn_w2[0].astype(BF16), h1, g2, lat_row(tm_dn), norm_f, tm=tm_dn, tk=d_ff // 2)
    return out.reshape(bsz, l_lat, d)
```

```python
import functools
import math

import jax
import jax.numpy as jnp
from jax import lax
from jax.experimental import pallas as pl
from jax.experimental.pallas import tpu as pltpu

F32 = jnp.float32
BF16 = jnp.bfloat16
HIGHEST = lax.Precision.HIGHEST

N_MOD = 6
NORM_EPS = 1e-6
GN_EPS = 64e-5
GRID_W = 64
S5_GROUP = 16
S5_STATE = 64
S5_CHUNK = 16
RW_HEAD = 64
RW_CHUNK = 64
LANES = 128
VMEM_LIMIT = 48 * 1024 * 1024


def _cparams(sem, vmem=VMEM_LIMIT):
    return pltpu.CompilerParams(dimension_semantics=sem, vmem_limit_bytes=vmem)


def _operands(a, b, precision):
    if precision == "bf16":
        return a.astype(BF16), b.astype(BF16), None
    return a, b, precision


def _dot(a, b, precision=None):
    a, b, precision = _operands(a, b, precision)
    return jnp.dot(a, b, preferred_element_type=F32, precision=precision)


def _dot_nt(a, b, precision=None):
    a, b, precision = _operands(a, b, precision)
    return lax.dot_general(a, b, (((1,), (1,)), ((), ())), preferred_element_type=F32, precision=precision)


def _dot_hilo(a, ind):
    hi = a.astype(BF16)
    lo = (a - hi.astype(F32)).astype(BF16)
    return _dot(hi, ind) + _dot(lo, ind)


def _sigmoid(x):
    return 1.0 / (1.0 + jnp.exp(-x))


def _silu(x):
    return x * _sigmoid(x)


def _gelu_tanh(x):
    c = math.sqrt(2.0 / math.pi)
    return 0.5 * x * (1.0 + jnp.tanh(c * (x + 0.044715 * (x * x * x))))


def _mod_kernel(c_ref, w_ref, b_ref, o_ref):
    o_ref[...] = _dot_hilo(_silu(c_ref[...]), w_ref[...].astype(BF16)) + b_ref[...]


def _modulation(c_rows, ada_w, ada_b):
    m, d = c_rows.shape
    n = ada_w.shape[1]
    tn = 1024
    return pl.pallas_call(
        _mod_kernel,
        out_shape=jax.ShapeDtypeStruct((m, n), F32),
        grid=(n // tn,),
        in_specs=[pl.BlockSpec((m, d), lambda j: (0, 0)),
                  pl.BlockSpec((d, tn), lambda j: (0, j)),
                  pl.BlockSpec((1, tn), lambda j: (0, j))],
        out_specs=pl.BlockSpec((m, tn), lambda j: (0, j)),
        compiler_params=_cparams(("arbitrary",)),
        name="modulation",
    )(c_rows, ada_w, ada_b.reshape(1, n))


def _lnmod_rows(x, nw, sh, sc):
    ms = jnp.mean(x * x, axis=-1, keepdims=True)
    y = x * lax.rsqrt(ms + NORM_EPS) * nw
    return y * (1.0 + sc) + sh


def _lnmod_proj_kernel(x_ref, c_ref, nw_ref, sh_ref, sc_ref, w_ref, h_ref, u_ref, w_scr, *, n_lat_blocks):
    @pl.when(pl.program_id(0) == 0)
    def _():
        w_scr[...] = w_ref[...].astype(BF16)

    rows = jnp.where(pl.program_id(0) < n_lat_blocks, x_ref[...], c_ref[...])
    h = _lnmod_rows(rows, nw_ref[...], sh_ref[0], sc_ref[0]).astype(BF16)
    h_ref[...] = h
    z = _dot(h, w_scr[...])
    for jb in range(u_ref.shape[0]):
        u_ref[jb] = z[:, jb * LANES:(jb + 1) * LANES]


def _lnmod_proj(x2, c2, nw, sh_tab, sc_tab, mod_row_of_block, w, n, *, tm):
    m, d = x2.shape
    mc = c2.shape[0]
    nx, nc = m // tm, mc // tm
    mod_map = lambda i: (mod_row_of_block(i), 0, 0)
    return pl.pallas_call(
        functools.partial(_lnmod_proj_kernel, n_lat_blocks=nx),
        out_shape=(jax.ShapeDtypeStruct((m + mc, d), BF16),
                   jax.ShapeDtypeStruct((n // LANES, m + mc, LANES), F32)),
        grid=(nx + nc,),
        in_specs=[pl.BlockSpec((tm, d), lambda i: (jnp.minimum(i, nx - 1), 0)),
                  pl.BlockSpec((tm, d), lambda i: (jnp.maximum(i - nx, 0), 0)),
                  pl.BlockSpec((1, d), lambda i: (0, 0)),
                  pl.BlockSpec((1, 1, d), mod_map),
                  pl.BlockSpec((1, 1, d), mod_map),
                  pl.BlockSpec((d, n), lambda i: (0, 0))],
        out_specs=(pl.BlockSpec((tm, d), lambda i: (i, 0)),
                   pl.BlockSpec((n // LANES, tm, LANES), lambda i: (0, i, 0))),
        scratch_shapes=[pltpu.VMEM((d, n), BF16)],
        compiler_params=_cparams(("arbitrary",)),
        name="lnmod_in_proj_s5",
    )(x2, c2, nw.reshape(1, d), sh_tab, sc_tab, w)


def _wres_mm_kernel(a_ref, w_ref, o_ref, w_scr, *, epilogue):
    @pl.when(pl.program_id(1) == 0)
    def _():
        w_scr[...] = w_ref[...].astype(BF16)

    z = _dot(a_ref[...], w_scr[...])
    if epilogue == "sigmoid":
        z = _sigmoid(z)
    o_ref[...] = z.astype(o_ref.dtype)


def _wres_matmul(a, w, col0, n, *, rows, tm, tn, out_dtype, epilogue=None, name):
    k = a.shape[1]
    assert col0 % LANES == 0 and n % tn == 0 and rows % tm == 0
    return pl.pallas_call(
        functools.partial(_wres_mm_kernel, epilogue=epilogue),
        out_shape=jax.ShapeDtypeStruct((rows, n), out_dtype),
        grid=(n // tn, rows // tm),
        in_specs=[pl.BlockSpec((tm, k), lambda j, i: (i, 0)),
                  pl.BlockSpec((pl.Element(k), pl.Element(tn)), lambda j, i: (0, pl.multiple_of(col0 + j * tn, LANES)))],
        out_specs=pl.BlockSpec((tm, tn), lambda j, i: (i, j)),
        scratch_shapes=[pltpu.VMEM((k, tn), BF16)],
        compiler_params=_cparams(("arbitrary", "arbitrary")),
        name=name,
    )(a, w)


def _s5_param_kernel(are_ref, aim_ref, ldt_ref, bre_ref, bim_ref, cre_ref, cim_ref,
                      e_ref, c_ref, m_ref, a16_ref):
    t_n, hg, p_n = S5_CHUNK, S5_GROUP, S5_STATE
    gl_n = are_ref.shape[1]
    tau = lax.broadcasted_iota(jnp.int32, (t_n, gl_n, p_n), 0).astype(F32)
    taps = {}
    for d in range(2):
        a_re = are_ref[d]
        a_im = aim_ref[d]
        dt = jnp.exp(ldt_ref[d])
        lam = a_re * dt
        th = a_im * dt
        er = jnp.exp(lam)
        ab_re = er * jnp.cos(th)
        ab_im = er * jnp.sin(th)
        den = a_re * a_re + a_im * a_im
        x_re = ab_re - 1.0
        co_re = (x_re * a_re + ab_im * a_im) / den
        co_im = (ab_im * a_re - x_re * a_im) / den

        def power(tv):
            mag = jnp.exp(tv * lam)
            return mag * jnp.cos(tv * th), mag * jnp.sin(tv * th)

        pw_re, pw_im = power(tau if d == 0 else (t_n - 1.0 - tau))
        pe_re, pe_im = power((t_n - 1.0 - tau) if d == 0 else tau)
        pc_re, pc_im = power((tau + 1.0) if d == 0 else (t_n - tau))
        mag16 = jnp.exp(float(t_n) * lam)
        a16_ref[d, 0] = mag16 * jnp.cos(float(t_n) * th)
        a16_ref[d, 1] = mag16 * jnp.sin(float(t_n) * th)

        lo, hi = 2 * d * p_n, (2 * d + 1) * p_n
        for gl in range(gl_n):
            of = lambda x: x[:, gl:gl + 1, :]
            bt_re = bre_ref[d, gl].T
            bt_im = bim_ref[d, gl].T
            bb_re = co_re[gl:gl + 1] * bt_re - co_im[gl:gl + 1] * bt_im
            bb_im = co_re[gl:gl + 1] * bt_im + co_im[gl:gl + 1] * bt_re
            c_re = cre_ref[d, gl]
            c_im = cim_ref[d, gl]
            cp_re = (c_re[None] * of(pw_re) - c_im[None] * of(pw_im)).reshape(t_n * hg, p_n)
            cp_im = (c_re[None] * of(pw_im) + c_im[None] * of(pw_re)).reshape(t_n * hg, p_n)
            taps[d, gl] = _dot_nt(bb_re, cp_re, HIGHEST) - _dot_nt(bb_im, cp_im, HIGHEST)
            e_ref[0, :, gl, :, lo:hi] = of(pe_re) * bb_re[None] - of(pe_im) * bb_im[None]
            e_ref[0, :, gl, :, hi:hi + p_n] = of(pe_re) * bb_im[None] + of(pe_im) * bb_re[None]
            c_ref[0, :, gl, :, lo:hi] = c_re[None] * of(pc_re) - c_im[None] * of(pc_im)
            c_ref[0, :, gl, :, hi:hi + p_n] = -(c_re[None] * of(pc_im) + c_im[None] * of(pc_re))

    width = t_n * hg
    lane = lax.broadcasted_iota(jnp.int32, (hg, width), 1)
    for gl in range(gl_n):
        tf, tb = taps[0, gl], taps[1, gl]
        for t in range(t_n):
            sf = t * hg
            sb = (t_n - 1 - t) * hg
            f = tf if sf == 0 else jnp.where(lane >= sf, pltpu.roll(tf, sf, 1), 0.0)
            b = tb if sb == 0 else jnp.where(lane < width - sb, pltpu.roll(tb, width - sb, 1), 0.0)
            m_ref[0, t, gl] = f + b


def _s5_params(a_re, a_im, log_dt, b_re, b_im, c_re, c_im, nb):
    g_n = a_re.shape[1]
    gl_n = g_n // nb
    p_n, hg, t_n = S5_STATE, S5_GROUP, S5_CHUNK
    ldt = jnp.broadcast_to(log_dt[:, :, None], (2, g_n, p_n))
    spec3 = pl.BlockSpec((2, gl_n, p_n), lambda j: (0, j, 0))
    spec4 = pl.BlockSpec((2, gl_n, hg, p_n), lambda j: (0, j, 0, 0))
    spec_b = pl.BlockSpec((2, gl_n, p_n, hg), lambda j: (0, j, 0, 0))
    wide = 4 * p_n
    comp = jax.ShapeDtypeStruct((nb, t_n, gl_n, hg, wide), F32)
    comp_spec = pl.BlockSpec((1, t_n, gl_n, hg, wide), lambda j: (j, 0, 0, 0, 0))
    return pl.pallas_call(
        _s5_param_kernel,
        out_shape=(comp, comp, comp, jax.ShapeDtypeStruct((2, 2, g_n, p_n), F32)),
        grid=(nb,),
        in_specs=[spec3, spec3, spec3, spec_b, spec_b, spec4, spec4],
        out_specs=(comp_spec, comp_spec, comp_spec, pl.BlockSpec((2, 2, gl_n, p_n), lambda j: (0, 0, j, 0))),
        compiler_params=_cparams(("parallel",)),
        name="s5_params",
    )(a_re, a_im, ldt, b_re, b_im, c_re, c_im)


def _expand_block_diag(comp, rep_ref, mask_ref, w_scr):
    k = w_scr.shape[0]
    period = mask_ref.shape[1]
    cb = comp.astype(BF16)
    step = 512
    for c0 in range(0, k, step):
        blk = _dot(cb, rep_ref[:, c0:c0 + step]).astype(BF16)
        for q0 in range(0, step, period):
            w_scr[:, c0 + q0:c0 + q0 + period] = blk[:, q0:q0 + period] * mask_ref[...]


def _gather_chunk_rows(u_ref, rows, dst):
    t_n = S5_CHUNK
    for t in range(t_n):
        dst[:, t * LANES:(t + 1) * LANES] = u_ref[0, pl.ds(t, rows, stride=t_n), :].astype(dst.dtype)


def _s5_ein_kernel(u_ref, ec_ref, rep_ref, mask_ref, o_ref, w_scr, u_scr):
    _expand_block_diag(ec_ref[0], rep_ref, mask_ref, w_scr)
    _gather_chunk_rows(u_ref, u_scr.shape[0], u_scr)
    o_ref[0] = _dot(u_scr[...], w_scr[...])


def _s5_chunk_inputs(u_blk, e_comp, rep_e, mask_e):
    nb, m_all, _ = u_blk.shape
    k, cw = e_comp.shape[1:]
    rows = m_all // S5_CHUNK
    return pl.pallas_call(
        _s5_ein_kernel,
        out_shape=jax.ShapeDtypeStruct((nb, rows, k), F32),
        grid=(nb,),
        in_specs=[pl.BlockSpec((1, m_all, LANES), lambda j: (j, 0, 0)),
                  pl.BlockSpec((1, k, cw), lambda j: (j, 0, 0)),
                  pl.BlockSpec(rep_e.shape, lambda j: (0, 0)),
                  pl.BlockSpec(mask_e.shape, lambda j: (0, 0))],
        out_specs=pl.BlockSpec((1, rows, k), lambda j: (j, 0, 0)),
        scratch_shapes=[pltpu.VMEM((k, k), BF16), pltpu.VMEM((rows, k), BF16)],
        compiler_params=_cparams(("parallel",)),
        name="s5_chunk_inputs",
    )(u_blk, e_comp, rep_e, mask_e)


def _s5_bscan_kernel(e_ref, a_ref, o_ref, *, bsz, n_ctx, n_lat):
    q = e_ref.shape[2] // 4
    planes = lambda row, d: (row[:, (2 * d) * q:(2 * d + 1) * q], row[:, (2 * d + 1) * q:(2 * d + 2) * q])
    coef = [planes(a_ref[0], d) for d in range(2)]
    ctx0 = bsz * n_lat

    def advance(state, rows):
        new = []
        for (sr, si), (b, d), row in zip(state, [(b, d) for b in range(bsz) for d in range(2)], rows):
            ar, ai = coef[d]
            er, ei = planes(e_ref[0, pl.ds(row, 1), :], d)
            new.append((ar * sr - ai * si + er, ar * si + ai * sr + ei))
        return tuple(new)

    def ctx_step(s, state):
        rows = [ctx0 + b * n_ctx + (s if d == 0 else n_ctx - 1 - s) for b in range(bsz) for d in range(2)]
        return advance(state, rows)

    def lat_step(s, state):
        rows = [b * n_lat + (s if d == 0 else n_lat - 1 - s) for b in range(bsz) for d in range(2)]
        for (sr, si), (b, d), row in zip(state, [(b, d) for b in range(bsz) for d in range(2)], rows):
            o_ref[0, pl.ds(row, 1), (2 * d) * q:(2 * d + 1) * q] = sr
            o_ref[0, pl.ds(row, 1), (2 * d + 1) * q:(2 * d + 2) * q] = si
        return advance(state, rows)

    zero = jnp.zeros((1, q), F32)
    state = tuple((zero, zero) for _ in range(2 * bsz))
    state = lax.fori_loop(0, n_ctx, ctx_step, state)
    lax.fori_loop(0, n_lat, lat_step, state)


def _s5_bscan(e_rows, a_rows, bsz, n_ctx, n_lat):
    nb, rows, k = e_rows.shape
    return pl.pallas_call(
        functools.partial(_s5_bscan_kernel, bsz=bsz, n_ctx=n_ctx, n_lat=n_lat),
        out_shape=jax.ShapeDtypeStruct((nb, bsz * n_lat, k), F32),
        grid=(nb,),
        in_specs=[pl.BlockSpec((1, rows, k), lambda j: (j, 0, 0)),
                  pl.BlockSpec((1, 1, k), lambda j: (j, 0, 0))],
        out_specs=pl.BlockSpec((1, bsz * n_lat, k), lambda j: (j, 0, 0)),
        compiler_params=_cparams(("parallel",)),
        name="s5_scan",
    )(e_rows, a_rows)


def _s5_out_kernel(u_ref, s_ref, mc_ref, cc_ref, d_ref, rep_m_ref, mask_m_ref, rep_e_ref, mask_e_ref,
                   o_ref, wm_scr, wc_scr, u_scr):
    @pl.when(pl.program_id(1) == 0)
    def _():
        _expand_block_diag(mc_ref[0], rep_m_ref, mask_m_ref, wm_scr)
        _expand_block_diag(cc_ref[0], rep_e_ref, mask_e_ref, wc_scr)

    t_n = S5_CHUNK
    rows = s_ref.shape[1]
    _gather_chunk_rows(u_ref, rows, u_scr)
    u = u_scr[...]
    y = _dot(u.astype(BF16), wm_scr[...]) + _dot_nt(s_ref[0].astype(BF16), wc_scr[...])
    y = _gelu_tanh(y + u * d_ref[0])
    for t in range(t_n):
        o_ref[0, pl.ds(t, rows, stride=t_n), :] = y[:, t * LANES:(t + 1) * LANES]


def _s5_outputs(u_blk, s_rows, m_comp, c_comp, d_rows, rep_m, mask_m, rep_e, mask_e):
    nb = u_blk.shape[0]
    rows, k = s_rows.shape[1:]
    cw = m_comp.shape[2]
    tr = rows // 2
    tok = tr * S5_CHUNK
    const = lambda a: pl.BlockSpec(a.shape, lambda j, i: (0, 0))
    return pl.pallas_call(
        _s5_out_kernel,
        out_shape=jax.ShapeDtypeStruct((nb, rows * S5_CHUNK, LANES), F32),
        grid=(nb, rows // tr),
        in_specs=[pl.BlockSpec((1, tok, LANES), lambda j, i: (j, i, 0)),
                  pl.BlockSpec((1, tr, k), lambda j, i: (j, i, 0)),
                  pl.BlockSpec((1, k, cw), lambda j, i: (j, 0, 0)),
                  pl.BlockSpec((1, k, cw), lambda j, i: (j, 0, 0)),
                  pl.BlockSpec((1, 1, k), lambda j, i: (j, 0, 0)),
                  const(rep_m), const(mask_m), const(rep_e), const(mask_e)],
        out_specs=pl.BlockSpec((1, tok, LANES), lambda j, i: (j, i, 0)),
        scratch_shapes=[pltpu.VMEM((k, k), BF16), pltpu.VMEM((k, k), BF16), pltpu.VMEM((tr, k), F32)],
        compiler_params=_cparams(("parallel", "arbitrary")),
        name="s5_outputs",
    )(u_blk, s_rows, m_comp, c_comp, d_rows, rep_m, mask_m, rep_e, mask_e)


def _s5_branch(u_blk, bsz, l_ctx, l_lat, a_re, a_im, log_dt, b_re, b_im, c_re, c_im, s5_d):
    nb, m_all, _ = u_blk.shape
    hg, t_n, p_n = S5_GROUP, S5_CHUNK, S5_STATE
    g_n = a_re.shape[1]
    gl_n = g_n // nb
    k = t_n * LANES
    n_lat = l_lat // t_n
    n_ctx = l_ctx // t_n
    e_c, c_c, m_c, a16 = _s5_params(a_re, a_im, log_dt, b_re, b_im, c_re, c_im, nb)
    cw = 4 * p_n
    e_comp, c_comp, m_comp = (a.reshape(nb, k, cw) for a in (e_c, c_c, m_c))

    row_gl = (jnp.arange(k) // hg) % gl_n
    col = jnp.arange(k)
    src = jnp.arange(cw)
    rep_e = ((src[:, None] // p_n == col[None, :] // (gl_n * p_n)) & (src[:, None] % p_n == col[None, :] % p_n)).astype(BF16)
    mask_e = (row_gl[:, None] == (jnp.arange(gl_n * p_n)[None, :] // p_n)).astype(BF16)
    rep_m = ((src[:, None] // hg == col[None, :] // (gl_n * hg)) & (src[:, None] % hg == col[None, :] % hg)).astype(BF16)
    mask_m = (row_gl[:, None] == (jnp.arange(gl_n * hg)[None, :] // hg)).astype(BF16)

    d_rows = jnp.tile(s5_d.reshape(nb, 1, gl_n * hg), (1, 1, t_n))
    a_rows = jnp.transpose(a16.reshape(2, 2, nb, gl_n * p_n), (2, 0, 1, 3)).reshape(nb, 1, 4 * gl_n * p_n)

    e_rows = _s5_chunk_inputs(u_blk, e_comp, rep_e, mask_e)
    s_rows = _s5_bscan(e_rows, a_rows, bsz, n_ctx, n_lat)
    return _s5_outputs(u_blk, s_rows, m_comp, c_comp, d_rows, rep_m, mask_m, rep_e, mask_e)


def _rw_prep_kernel(z_ref, zp_ref, zn_ref, mu_ref, w2_ref, a2_ref, g2_ref, w0_ref, a0_ref,
                    kk_w_ref, ka_ref, rk_ref, seg_ref, segt_ref,
                    r_ref, v_ref, kk_ref, g_ref, bonus_ref, lw_ref, kd_ref, be_ref,
                    *, tm, l_lat, rw):
    j = pl.program_id(1)
    z = z_ref[...].astype(F32)
    lat = j > 0
    tl = lax.broadcasted_iota(jnp.int32, (tm, 1), 0)
    tok = (j - 1) * tm + tl
    col = tl % GRID_W
    m_l = jnp.where(lat, col, tl) > 0
    m_r = jnp.where(lat, col - (GRID_W - 1), tl - (tm - 1)) < 0
    m_u = jnp.logical_and(lat, tok >= GRID_W)
    m_d = jnp.logical_and(lat, tok < l_lat - GRID_W)
    z_ext = jnp.concatenate([zp_ref[...], z_ref[...], zn_ref[...]], axis=0)
    rel = (lax.broadcasted_iota(jnp.int32, (tm, tm + 2 * GRID_W), 1) - GRID_W
           - lax.broadcasted_iota(jnp.int32, (tm, tm + 2 * GRID_W), 0))
    pick = (jnp.logical_and(rel == -1, m_l) | jnp.logical_and(rel == 1, m_r)
            | jnp.logical_and(rel == -GRID_W, m_u) | jnp.logical_and(rel == GRID_W, m_d))
    s = _dot(jnp.where(pick, 1.0, 0.0).astype(z_ext.dtype), z_ext)
    cnt = (m_l.astype(F32) + m_r.astype(F32)) + (m_u.astype(F32) + m_d.astype(F32))
    zs = z + (s * (1.0 / cnt) - z) * mu_ref[...]

    r = zs[:, 0:rw]
    k = zs[:, rw:2 * rw]
    v = zs[:, 2 * rw:3 * rw]
    o = 3 * rw
    wd = zs[:, o:o + LANES]
    ad = zs[:, o + LANES:o + 2 * LANES]
    gd = zs[:, o + 2 * LANES:o + 3 * LANES]

    seg = seg_ref[...]
    segt = segt_ref[...]

    def head_sum(t):
        return _dot(_dot(t.astype(BF16), seg).astype(BF16), segt)

    g_ref[0] = _dot(_sigmoid(gd), g2_ref[...], "bf16").astype(g_ref.dtype)
    kk = k * kk_w_ref[...]
    kk = kk * lax.rsqrt(head_sum(kk * kk) + 1e-12)
    wl = w0_ref[...] + _dot(jnp.tanh(wd), w2_ref[...], "bf16")
    al = a0_ref[...] + _dot(ad, a2_ref[...], "bf16")
    r_ref[0] = r.astype(r_ref.dtype)
    v_ref[0] = v.astype(v_ref.dtype)
    kk_ref[0] = kk.astype(kk_ref.dtype)
    k_sum = jnp.zeros_like(r)
    for d in range(2):
        a = _sigmoid(al[:, d * rw:(d + 1) * rw])
        k_d = k * (1.0 + (a - 1.0) * ka_ref[...])
        k_sum = k_sum + k_d
        lw_ref[d, 0] = -math.exp(-0.5) * _sigmoid(wl[:, d * rw:(d + 1) * rw])
        kd_ref[d, 0] = k_d.astype(kd_ref.dtype)
        be_ref[d, 0] = (kk * a).astype(be_ref.dtype)
    bonus_ref[0] = (head_sum(r * rk_ref[...] * k_sum) * v).astype(bonus_ref.dtype)


def _rw_prep(z_rw, bsz, l_ctx, l_lat, mu, w2bd, a2bd, g2, w0cat, a0cat, k_k, k_a, r_k_flat, seg, segt):
    cols = z_rw.shape[1]
    tm = l_ctx
    rw = g2.shape[1]
    l_all = l_ctx + l_lat
    nblk = l_all // tm
    lat_blk = l_lat // tm
    hb = tm // GRID_W
    lat_hblk = l_lat // GRID_W

    def main_blk(b, j):
        return jnp.where(j == 0, bsz * lat_blk + b, b * lat_blk + j - 1)

    def prev_halo(b, j):
        return b * lat_hblk + jnp.maximum((j - 1) * hb - 1, 0)

    def next_halo(b, j):
        return b * lat_hblk + jnp.minimum(jnp.maximum(j, 1) * hb, lat_hblk - 1)

    full = lambda shape: pl.BlockSpec(shape, lambda b, j: (0,) * len(shape))
    shared = jax.ShapeDtypeStruct((bsz, l_all, rw), BF16)
    lat_only = jax.ShapeDtypeStruct((bsz, l_lat, rw), BF16)
    per_dir = jax.ShapeDtypeStruct((2, bsz, l_all, rw), BF16)
    per_dir_f32 = jax.ShapeDtypeStruct((2, bsz, l_all, rw), F32)
    o_shared = pl.BlockSpec((1, tm, rw), lambda b, j: (b, j, 0))
    o_lat = pl.BlockSpec((1, tm, rw), lambda b, j: (b, jnp.maximum(j - 1, 0), 0))
    o_dir = pl.BlockSpec((2, 1, tm, rw), lambda b, j: (0, b, j, 0))
    return pl.pallas_call(
        functools.partial(_rw_prep_kernel, tm=tm, l_lat=l_lat, rw=rw),
        out_shape=(shared,) * 3 + (lat_only,) * 2 + (per_dir_f32, per_dir, per_dir),
        grid=(bsz, nblk),
        in_specs=[pl.BlockSpec((tm, cols), lambda b, j: (main_blk(b, j), 0)),
                  pl.BlockSpec((GRID_W, cols), lambda b, j: (prev_halo(b, j), 0)),
                  pl.BlockSpec((GRID_W, cols), lambda b, j: (next_halo(b, j), 0)),
                  full((1, cols)), full(w2bd.shape), full(a2bd.shape), full(g2.shape),
                  full((1, 2 * rw)), full((1, 2 * rw)), full((1, rw)), full((1, rw)), full((1, rw)),
                  full(seg.shape), full(segt.shape)],
        out_specs=(o_shared,) * 3 + (o_lat,) * 2 + (o_dir,) * 3,
        compiler_params=_cparams(("parallel", "arbitrary")),
        name="rwkv_prep",
    )(z_rw, z_rw, z_rw, mu.reshape(1, cols), w2bd, a2bd, g2, w0cat, a0cat,
      k_k.reshape(1, rw), k_a.reshape(1, rw), r_k_flat.reshape(1, rw), seg, segt)


def _stack_heads(x, head0):
    return jnp.concatenate([jnp.where(head0, x, 0.0), jnp.where(head0, 0.0, x)], axis=0)


def _rw_step_kernel(rf_ref, rb_ref, vf_ref, vb_ref, kkf_ref, kkb_ref, lwf_ref, lwb_ref, kdf_ref, kdb_ref,
                    bef_ref, beb_ref, yf_ref, yb_ref,
                    z_scr, gc_s, lhs_s, kb_s, kht_s, bht_s, vs_s, rt_s, aab_s, akr_s, arb_s, t_s, x_s, xin_s,
                    yc_s, wu_s, qm_s, yn_s, *, n_pairs, bsz):
    c_n = RW_CHUNK
    n2 = 2 * c_n

    lanes_of = lambda p: slice(p * LANES, (p + 1) * LANES)
    unit = lambda d, b, p: (d * bsz + b) * n_pairs + p
    units = [(d, b, p) for d in range(2) for b in range(bsz) for p in range(n_pairs)]

    @pl.when(pl.program_id(0) == 0)
    def _():
        z_scr[...] = jnp.zeros_like(z_scr)
        qm_s[...] = jnp.zeros_like(qm_s)
        yn_s[...] = jnp.zeros_like(yn_s)

    for d, b, p in units:
        u = unit(d, b, p)
        y_ref = yf_ref if d == 0 else yb_ref
        yz = _dot(qm_s[u], z_scr[u].astype(BF16)) + yn_s[u]
        y_ref[b, :, lanes_of(p)] = yz[:c_n] + yz[c_n:n2]
        z_scr[u] = yz[n2:]

    ri = lax.broadcasted_iota(jnp.int32, (c_n, c_n), 0)
    ci = lax.broadcasted_iota(jnp.int32, (c_n, c_n), 1)
    r2 = lax.broadcasted_iota(jnp.int32, (n2, n2), 0)
    c2 = lax.broadcasted_iota(jnp.int32, (n2, n2), 1)
    t2 = r2 % c_n
    i2 = c2 % c_n
    same_head = (r2 // c_n) == (c2 // c_n)
    diag = r2 == c2
    eye = diag.astype(F32)
    head0 = lax.broadcasted_iota(jnp.int32, (1, LANES), 1) < RW_HEAD
    tri = [(ri >= ci).astype(F32), (ri <= ci).astype(F32)]
    before = [jnp.logical_and(same_head, i2 < t2), jnp.logical_and(same_head, i2 > t2)]
    upto = [jnp.logical_or(m, diag) for m in before]

    def blk(s):
        return (r2 // s) == (c2 // s)

    srcs = [(rf_ref, vf_ref, kkf_ref, lwf_ref, kdf_ref, bef_ref), (rb_ref, vb_ref, kkb_ref, lwb_ref, kdb_ref, beb_ref)]

    for d in range(2):
        r_ref, v_ref, kk_ref, lw_ref, kd_ref, be_ref = srcs[d]
        for b in range(bsz):
            lw = lw_ref[0, b]
            cum = _dot(tri[d], lw, HIGHEST)
            tot = jnp.sum(lw, axis=0, keepdims=True)
            g_inv = jnp.exp(-cum)
            g_hat = jnp.exp(tot - cum)
            kd = kd_ref[0, b].astype(F32)
            be = be_ref[0, b].astype(F32)
            at = kk_ref[b].astype(F32) * jnp.exp(cum - lw)
            rt = r_ref[b].astype(F32) * jnp.exp(cum)
            kt = kd * g_inv
            bt = be * g_inv
            kh = kd * g_hat
            bh = be * g_hat
            gc_s[d * bsz + b] = jnp.exp(tot)
            v = v_ref[b].astype(F32)
            for p in range(n_pairs):
                u = unit(d, b, p)
                st = lambda x: _stack_heads(x[:, lanes_of(p)], head0)
                rt_p = st(rt)
                at_p = st(at).astype(BF16)
                lhs_s[u, :n2] = at_p
                lhs_s[u, n2:] = rt_p.astype(BF16)
                xin_s[u, :, :LANES] = at_p
                rt_s[u] = rt_p
                kb_s[u, :n2] = st(kt).astype(BF16)
                kb_s[u, n2:] = st(bt).astype(BF16)
                kht_s[u] = st(kh).T.astype(BF16)
                bht_s[u] = st(bh).T.astype(BF16)
                vs_s[u] = st(v).astype(BF16)

    for d, b, p in units:
        u = unit(d, b, p)
        g = _dot_nt(lhs_s[u], kb_s[u])
        a_ab = jnp.where(before[d], g[:n2, n2:], 0.0)
        akr_s[u, :n2] = jnp.where(before[d], g[:n2, :n2], 0.0).astype(BF16)
        akr_s[u, n2:] = jnp.where(upto[d], g[n2:, :n2], 0.0).astype(BF16)
        arb_s[u] = jnp.where(upto[d], g[n2:, n2:], 0.0).astype(BF16)
        aab_s[u] = a_ab
        t_s[u] = eye - jnp.where(blk(2), a_ab, 0.0)

    n_units = len(units)
    s = 2
    while s < c_n:
        off = jnp.logical_and(blk(2 * s), jnp.logical_not(blk(s)))
        for u in range(n_units):
            x_s[u] = _dot(t_s[u].astype(BF16), jnp.where(off, aab_s[u], 0.0).astype(BF16)).astype(BF16)
        for u in range(n_units):
            t = t_s[u]
            t_s[u] = t - _dot(x_s[u], t.astype(BF16))
        s *= 2

    for u in range(n_units):
        av = _dot(akr_s[u], vs_s[u])
        xin_s[u, :, LANES:] = av[:n2].astype(BF16)
        yc_s[u] = av[n2:]

    for u in range(n_units):
        wu_s[u] = _dot(t_s[u].astype(BF16), xin_s[u]).astype(BF16)
    for d, b, p in units:
        u = unit(d, b, p)
        wu = wu_s[u]
        q = _dot(arb_s[u], wu)
        bwu = _dot(bht_s[u], wu)
        qm_s[u, :n2] = (rt_s[u] - q[:, :LANES]).astype(BF16)
        qm_s[u, n2:] = (eye * gc_s[d * bsz + b][:, lanes_of(p)] - bwu[:, :LANES]).astype(BF16)
        yn_s[u, :n2] = yc_s[u] - q[:, LANES:]
        yn_s[u, n2:] = _dot(kht_s[u], vs_s[u]) - bwu[:, LANES:]


def _rw_scan(r, v, kk, lw, kd, be, l_ctx):
    bsz, l_all, rw = r.shape
    c_n = RW_CHUNK
    n_all = l_all // c_n
    n_ctx = l_ctx // c_n
    n_lat = n_all - n_ctx
    n_pairs = rw // LANES
    n_units = 2 * bsz * n_pairs
    n2 = 2 * c_n
    vm = lambda nr, cols, dt: pltpu.VMEM((n_units, nr, cols), dt)

    clamp = lambda s: jnp.minimum(s, n_all - 1)
    chunk_f = lambda s: clamp(s)
    chunk_b = lambda s: jnp.where(clamp(s) < n_ctx, n_ctx - 1 - clamp(s), n_all + n_ctx - 1 - clamp(s))
    prev = lambda s: jnp.maximum(s - 1, 0)
    out_f = lambda s: jnp.maximum(prev(s) - n_ctx, 0)
    out_b = lambda s: jnp.where(prev(s) < n_ctx, n_lat - 1, chunk_b(prev(s)) - n_ctx)

    sh_f = pl.BlockSpec((bsz, c_n, rw), lambda s: (0, chunk_f(s), 0))
    sh_b = pl.BlockSpec((bsz, c_n, rw), lambda s: (0, chunk_b(s), 0))
    pd_f = pl.BlockSpec((1, bsz, c_n, rw), lambda s: (0, 0, chunk_f(s), 0))
    pd_b = pl.BlockSpec((1, bsz, c_n, rw), lambda s: (1, 0, chunk_b(s), 0))
    y_shape = jax.ShapeDtypeStruct((bsz, n_lat * c_n, rw), F32)
    return pl.pallas_call(
        functools.partial(_rw_step_kernel, n_pairs=n_pairs, bsz=bsz),
        out_shape=(y_shape, y_shape),
        grid=(n_all + 1,),
        in_specs=[sh_f, sh_b, sh_f, sh_b, sh_f, sh_b, pd_f, pd_b, pd_f, pd_b, pd_f, pd_b],
        out_specs=(pl.BlockSpec((bsz, c_n, rw), lambda s: (0, out_f(s), 0)),
                   pl.BlockSpec((bsz, c_n, rw), lambda s: (0, out_b(s), 0))),
        scratch_shapes=[vm(n2, n2, F32),
                        pltpu.VMEM((2 * bsz, 1, rw), F32),
                        vm(2 * n2, n2, BF16),
                        vm(2 * n2, n2, BF16),
                        vm(n2, n2, BF16),
                        vm(n2, n2, BF16),
                        vm(n2, n2, BF16),
                        vm(n2, n2, F32),
                        vm(n2, n2, F32),
                        vm(2 * n2, n2, BF16),
                        vm(n2, n2, BF16),
                        vm(n2, n2, F32),
                        vm(n2, n2, BF16),
                        vm(n2, 2 * n2, BF16),
                        vm(n2, n2, F32),
                        vm(n2, 2 * n2, BF16),
                        vm(2 * n2, n2, BF16),
                        vm(2 * n2, n2, F32)],
        compiler_params=_cparams(("arbitrary",)),
        name="rwkv_scan",
    )(r, r, v, v, kk, kk, lw, lw, kd, kd, be, be)


def _s5_glu_kernel(y_ref, wa_ref, wb_ref, o_ref, h_scr):
    @pl.when(pl.program_id(1) == 0)
    def _():
        for jb in range(y_ref.shape[0]):
            h_scr[:, jb * LANES:(jb + 1) * LANES] = y_ref[jb].astype(BF16)

    h = h_scr[...]
    o_ref[...] = (_dot(h, wa_ref[...]) * _sigmoid(_dot(h, wb_ref[...]))).astype(o_ref.dtype)


def _s5_glu(y_blk, w, *, tm, tn):
    nb, m, _ = y_blk.shape
    k = nb * LANES
    n = w.shape[1] // 2
    nj = n // tn
    return pl.pallas_call(
        _s5_glu_kernel,
        out_shape=jax.ShapeDtypeStruct((m, n), BF16),
        grid=(m // tm, nj),
        in_specs=[pl.BlockSpec((nb, tm, LANES), lambda i, j: (0, i, 0)),
                  pl.BlockSpec((k, tn), lambda i, j: (0, j)),
                  pl.BlockSpec((k, tn), lambda i, j: (0, j + nj))],
        out_specs=pl.BlockSpec((tm, tn), lambda i, j: (i, j)),
        scratch_shapes=[pltpu.VMEM((tm, k), BF16)],
        compiler_params=_cparams(("parallel", "arbitrary")),
        name="s5_glu",
    )(y_blk, w, w)


def _merge_out_kernel(yf_ref, yb_ref, bonus_ref, g_ref, lnw_ref, lnb_ref, seg_ref, segt_ref,
                      ga_ref, gb_ref, s5_ref, wp_ref, wo_ref, x_ref, g1_ref, nw_ref, sh_ref, sc_ref,
                      h_ref, hn_ref):
    seg = seg_ref[...]
    segt = segt_ref[...]
    inv_n = 1.0 / RW_HEAD
    tm = yf_ref.shape[0]
    parts = [slice(0, tm // 2), slice(tm // 2, tm)]

    def head_mean(vals, precise):
        red = _dot_hilo if precise else (lambda a, ind: _dot(a.astype(BF16), ind))
        sums = [red(v, seg) for v in vals]
        return [red(s, segt) * inv_n for s in sums]

    y = [yf_ref[p, :] + yb_ref[p, :] for p in parts]
    dy = [a - m for a, m in zip(y, head_mean(y, True))]
    var = head_mean([a * a for a in dy], False)
    y_rw = []
    for p, a, v in zip(parts, dy, var):
        t = a * lax.rsqrt(v + GN_EPS) * lnw_ref[...] + lnb_ref[...] + bonus_ref[p, :].astype(F32)
        y_rw.append((t * g_ref[p, :].astype(F32)).astype(BF16))
    rw_out = [_dot(a, wp_ref[...]) for a in y_rw]
    merged = [(ga_ref[p, :].astype(F32) * s5_ref[p, :].astype(F32) + gb_ref[p, :].astype(F32) * r).astype(BF16)
              for p, r in zip(parts, rw_out)]
    proj = [_dot(a, wo_ref[...]) for a in merged]
    for p, a in zip(parts, proj):
        h = x_ref[p, :] + g1_ref[0] * a
        h_ref[p, :] = h
        hn_ref[p, :] = _lnmod_rows(h, nw_ref[...], sh_ref[0], sc_ref[0]).astype(hn_ref.dtype)


def _merge_out(y_f, y_b, bonus, g, ln_w, ln_b, seg, segt, gates, s5_out, w_proj, w_o, x2, g_tab, nw,
               sh_tab, sc_tab, mod_row_of_block, *, tm):
    m, rw = y_f.shape
    n = w_proj.shape[1]
    full = lambda shape: pl.BlockSpec(shape, lambda i: (0,) * len(shape))
    rows = lambda width: pl.BlockSpec((tm, width), lambda i: (i, 0))
    mod_map = lambda i: (mod_row_of_block(i), 0, 0)
    return pl.pallas_call(
        _merge_out_kernel,
        out_shape=(jax.ShapeDtypeStruct((m, n), F32), jax.ShapeDtypeStruct((m, n), BF16)),
        grid=(m // tm,),
        in_specs=[rows(rw), rows(rw), rows(rw), rows(rw),
                  full((1, rw)), full((1, rw)), full(seg.shape), full(segt.shape),
                  pl.BlockSpec((tm, n), lambda i: (i, 0)), pl.BlockSpec((tm, n), lambda i: (i, 1)), rows(n),
                  full(w_proj.shape), full(w_o.shape), rows(n),
                  pl.BlockSpec((1, 1, n), mod_map), full((1, n)),
                  pl.BlockSpec((1, 1, n), mod_map), pl.BlockSpec((1, 1, n), mod_map)],
        out_specs=(rows(n), rows(n)),
        compiler_params=_cparams(("parallel",), vmem=VMEM_LIMIT + 8 * 1024 * 1024),
        name="merge_out_proj",
    )(y_f, y_b, bonus, g, ln_w.reshape(1, rw), ln_b.reshape(1, rw), seg, segt,
      gates, gates, s5_out, w_proj, w_o, x2, g_tab, nw.reshape(1, n), sh_tab, sc_tab)


def _wres_swiglu_kernel(a_ref, w1_ref, w3_ref, o_ref, w1_scr, w3_scr):
    @pl.when(pl.program_id(1) == 0)
    def _():
        w1_scr[...] = w1_ref[...].astype(BF16)
        w3_scr[...] = w3_ref[...].astype(BF16)

    a = a_ref[...]
    o_ref[...] = (_silu(_dot(a, w1_scr[...])) * _dot(a, w3_scr[...])).astype(o_ref.dtype)


def _wres_swiglu(a, w13, d_ff, *, tm, tn):
    m, k = a.shape
    nj = d_ff // tn
    return pl.pallas_call(
        _wres_swiglu_kernel,
        out_shape=jax.ShapeDtypeStruct((m, d_ff), BF16),
        grid=(nj, m // tm),
        in_specs=[pl.BlockSpec((tm, k), lambda j, i: (i, 0)),
                  pl.BlockSpec((k, tn), lambda j, i: (0, j)),
                  pl.BlockSpec((k, tn), lambda j, i: (0, j + nj))],
        out_specs=pl.BlockSpec((tm, tn), lambda j, i: (i, j)),
        scratch_shapes=[pltpu.VMEM((k, tn), BF16), pltpu.VMEM((k, tn), BF16)],
        compiler_params=_cparams(("arbitrary", "arbitrary")),
        name="ffn_up",
    )(a, w13, w13)


def _ffn_down_kernel(a_ref, w_ref, x_ref, g_ref, nf_ref, o_ref):
    kk = pl.program_id(1)
    part = _dot(a_ref[...], w_ref[...])

    @pl.when(kk == 0)
    def _():
        o_ref[...] = part

    @pl.when(jnp.logical_and(kk > 0, kk < pl.num_programs(1) - 1))
    def _():
        o_ref[...] += part

    @pl.when(kk == pl.num_programs(1) - 1)
    def _():
        h = x_ref[...] + g_ref[0] * (o_ref[...] + part)
        ms = jnp.mean(h * h, axis=-1, keepdims=True)
        o_ref[...] = h * lax.rsqrt(ms + NORM_EPS) * nf_ref[...]


def _ffn_down(a, w, x2, g_tab, mod_row_of_block, norm_f, *, tm, tk):
    m, k = a.shape
    n = w.shape[1]
    return pl.pallas_call(
        _ffn_down_kernel,
        out_shape=jax.ShapeDtypeStruct((m, n), F32),
        grid=(m // tm, k // tk),
        in_specs=[pl.BlockSpec((tm, tk), lambda i, kk: (i, kk)),
                  pl.BlockSpec((tk, n), lambda i, kk: (kk, 0)),
                  pl.BlockSpec((tm, n), lambda i, kk: (i, 0)),
                  pl.BlockSpec((1, 1, n), lambda i, kk: (mod_row_of_block(i), 0, 0)),
                  pl.BlockSpec((1, n), lambda i, kk: (0, 0))],
        out_specs=pl.BlockSpec((tm, n), lambda i, kk: (i, 0)),
        compiler_params=_cparams(("parallel", "arbitrary"), vmem=VMEM_LIMIT + 8 * 1024 * 1024),
        name="ffn_down",
    )(a, w, x2, g_tab, norm_f.reshape(1, n))


def kernel(x, c, ctx, c_ctx, ada_w, ada_b, norm1_w, w_in, rw_mu, s5_a_re, s5_a_im, s5_log_dt, s5_b_re, s5_b_im, s5_c_re, s5_c_im, s5_d, s5_glu_w, rw_w0, rw_w2, rw_a0, rw_a2, rw_g2, rw_k_k, rw_k_a, rw_r_k, rw_ln_w, rw_ln_b, rw_proj, w_o, norm2_w, ffn_w13, ffn_w2, norm_f):
    assert ada_w.shape[0] == 1, "single-layer block"
    bsz, l_lat, d = x.shape
    l_ctx = ctx.shape[1]
    l_all = l_ctx + l_lat
    s5w = s5_d.shape[1] * s5_d.shape[2]
    rw = rw_g2.shape[2]
    shift_cols = rw_mu.shape[1]
    d_ff = ffn_w2.shape[1]

    c_rows = jnp.concatenate([c, c_ctx[None], jnp.zeros((8 - bsz - 1, d), F32)], axis=0)
    mod = _modulation(c_rows, ada_w[0], ada_b[0])
    tab = lambda k: mod[:, k * d:(k + 1) * d].reshape(8, 1, d)
    sh1, sc1, g1, sh2, sc2, g2 = (tab(k) for k in range(N_MOD))
    ctx_row = bsz

    tm_in = 512
    m_lat = bsz * l_lat
    x2 = x.reshape(m_lat, d)
    n_mix = s5w + shift_cols
    m_all = m_lat + bsz * l_ctx
    lat_row = lambda t: (lambda i: i // (l_lat // t))

    def mix_mod_row(i):
        return jnp.where(i >= m_lat // tm_in, ctx_row, i // (l_lat // tm_in))

    h_all, u_blk = _lnmod_proj(x2, ctx.reshape(bsz * l_ctx, d), norm1_w[0], sh1, sc1, mix_mod_row,
                               w_in[0], s5w, tm=tm_in)
    tm_all = m_all // 8
    z_rw = _wres_matmul(h_all, w_in[0], s5w, shift_cols, rows=m_all, tm=tm_all, tn=shift_cols // 3,
                        out_dtype=BF16, name="in_proj_rw")
    tm = 1024
    gates = _wres_matmul(h_all, w_in[0], n_mix, w_in.shape[2] - n_mix, rows=m_lat, tm=tm, tn=1024,
                         out_dtype=BF16, epilogue="sigmoid", name="in_proj_gates")

    y_blk = _s5_branch(u_blk, bsz, l_ctx, l_lat, s5_a_re[0], s5_a_im[0], s5_log_dt[0],
                               s5_b_re[0], s5_b_im[0], s5_c_re[0], s5_c_im[0], s5_d[0])
    s5_out = _s5_glu(y_blk, s5_glu_w[0].astype(BF16), tm=tm, tn=1024)

    lora = rw_w2.shape[2]
    zl = jnp.zeros((lora, rw), F32)
    w2bd = jnp.concatenate([jnp.concatenate([rw_w2[0, 0], zl], axis=1),
                            jnp.concatenate([zl, rw_w2[0, 1]], axis=1)], axis=0)
    a2bd = jnp.concatenate([jnp.concatenate([rw_a2[0, 0], zl], axis=1),
                            jnp.concatenate([zl, rw_a2[0, 1]], axis=1)], axis=0)
    head_of = jnp.arange(rw) // RW_HEAD
    seg = (head_of[:, None] == jnp.arange(LANES)[None, :]).astype(BF16)
    segt = seg.T
    r, v, kk, g, bonus, lw, kd, be = _rw_prep(
        z_rw, bsz, l_ctx, l_lat, rw_mu[0], w2bd, a2bd, rw_g2[0], rw_w0[0].reshape(1, 2 * rw),
        rw_a0[0].reshape(1, 2 * rw), rw_k_k[0], rw_k_a[0], rw_r_k[0].reshape(rw), seg, segt)
    y_f, y_b = _rw_scan(r, v, kk, lw, kd, be, l_ctx)

    tm_o = 256
    h1, h1n = _merge_out(y_f.reshape(m_lat, rw), y_b.reshape(m_lat, rw), bonus.reshape(m_lat, rw),
                         g.reshape(m_lat, rw), rw_ln_w[0], rw_ln_b[0], seg, segt, gates, s5_out,
                         rw_proj[0].astype(BF16), w_o[0].astype(BF16), x2, g1, norm2_w[0], sh2, sc2,
                         lat_row(tm_o), tm=tm_o)

    act = _wres_swiglu(h1n, ffn_w13[0], d_ff, tm=tm, tn=512)
    tm_dn = 512
    out = _ffn_down(act, ffn_w2[0].astype(BF16), h1, g2, lat_row(tm_dn), norm_f, tm=tm_dn, tk=d_ff // 2)
    return out.reshape(bsz, l_lat, d)
```

```python
import functools
import math

import jax
import jax.numpy as jnp
from jax import lax
from jax.experimental import pallas as pl
from jax.experimental.pallas import tpu as pltpu

F32 = jnp.float32
BF16 = jnp.bfloat16
HIGHEST = lax.Precision.HIGHEST

N_MOD = 6
NORM_EPS = 1e-6
GN_EPS = 64e-5
GRID_W = 64
S5_GROUP = 16
S5_STATE = 64
S5_CHUNK = 16
RW_HEAD = 64
RW_CHUNK = 64
LANES = 128
VMEM_LIMIT = 48 * 1024 * 1024


def _cparams(sem, vmem=VMEM_LIMIT):
    return pltpu.CompilerParams(dimension_semantics=sem, vmem_limit_bytes=vmem)


def _operands(a, b, precision):
    if precision == "bf16":
        return a.astype(BF16), b.astype(BF16), None
    return a, b, precision


def _dot(a, b, precision=None):
    a, b, precision = _operands(a, b, precision)
    return jnp.dot(a, b, preferred_element_type=F32, precision=precision)


def _dot_nt(a, b, precision=None):
    a, b, precision = _operands(a, b, precision)
    return lax.dot_general(a, b, (((1,), (1,)), ((), ())), preferred_element_type=F32, precision=precision)


def _dot_hilo(a, ind):
    hi = a.astype(BF16)
    lo = (a - hi.astype(F32)).astype(BF16)
    return _dot(hi, ind) + _dot(lo, ind)


def _sigmoid(x):
    return 1.0 / (1.0 + jnp.exp(-x))


def _silu(x):
    return x * _sigmoid(x)


def _gelu_tanh(x):
    c = math.sqrt(2.0 / math.pi)
    return 0.5 * x * (1.0 + jnp.tanh(c * (x + 0.044715 * (x * x * x))))


def _mod_kernel(c_ref, w_ref, b_ref, o_ref):
    o_ref[...] = _dot_hilo(_silu(c_ref[...]), w_ref[...].astype(BF16)) + b_ref[...]


def _modulation(c_rows, ada_w, ada_b):
    m, d = c_rows.shape
    n = ada_w.shape[1]
    tn = 1024
    return pl.pallas_call(
        _mod_kernel,
        out_shape=jax.ShapeDtypeStruct((m, n), F32),
        grid=(n // tn,),
        in_specs=[pl.BlockSpec((m, d), lambda j: (0, 0)),
                  pl.BlockSpec((d, tn), lambda j: (0, j)),
                  pl.BlockSpec((1, tn), lambda j: (0, j))],
        out_specs=pl.BlockSpec((m, tn), lambda j: (0, j)),
        compiler_params=_cparams(("arbitrary",)),
        name="modulation",
    )(c_rows, ada_w, ada_b.reshape(1, n))


def _lnmod_rows(x, nw, sh, sc):
    ms = jnp.mean(x * x, axis=-1, keepdims=True)
    y = x * lax.rsqrt(ms + NORM_EPS) * nw
    return y * (1.0 + sc) + sh


def _lnmod_proj_kernel(x_ref, c_ref, nw_ref, sh_ref, sc_ref, w_ref, h_ref, u_ref, w_scr, *, n_lat_blocks):
    @pl.when(pl.program_id(0) == 0)
    def _():
        w_scr[...] = w_ref[...].astype(BF16)

    rows = jnp.where(pl.program_id(0) < n_lat_blocks, x_ref[...], c_ref[...])
    h = _lnmod_rows(rows, nw_ref[...], sh_ref[0], sc_ref[0]).astype(BF16)
    h_ref[...] = h
    z = _dot(h, w_scr[...])
    for jb in range(u_ref.shape[0]):
        u_ref[jb] = z[:, jb * LANES:(jb + 1) * LANES]


def _lnmod_proj(x2, c2, nw, sh_tab, sc_tab, mod_row_of_block, w, n, *, tm):
    m, d = x2.shape
    mc = c2.shape[0]
    nx, nc = m // tm, mc // tm
    mod_map = lambda i: (mod_row_of_block(i), 0, 0)
    return pl.pallas_call(
        functools.partial(_lnmod_proj_kernel, n_lat_blocks=nx),
        out_shape=(jax.ShapeDtypeStruct((m + mc, d), BF16),
                   jax.ShapeDtypeStruct((n // LANES, m + mc, LANES), F32)),
        grid=(nx + nc,),
        in_specs=[pl.BlockSpec((tm, d), lambda i: (jnp.minimum(i, nx - 1), 0)),
                  pl.BlockSpec((tm, d), lambda i: (jnp.maximum(i - nx, 0), 0)),
                  pl.BlockSpec((1, d), lambda i: (0, 0)),
                  pl.BlockSpec((1, 1, d), mod_map),
                  pl.BlockSpec((1, 1, d), mod_map),
                  pl.BlockSpec((d, n), lambda i: (0, 0))],
        out_specs=(pl.BlockSpec((tm, d), lambda i: (i, 0)),
                   pl.BlockSpec((n // LANES, tm, LANES), lambda i: (0, i, 0))),
        scratch_shapes=[pltpu.VMEM((d, n), BF16)],
        compiler_params=_cparams(("arbitrary",)),
        name="lnmod_in_proj_s5",
    )(x2, c2, nw.reshape(1, d), sh_tab, sc_tab, w)


def _wres_mm_kernel(a_ref, w_ref, o_ref, w_scr, *, epilogue):
    @pl.when(pl.program_id(1) == 0)
    def _():
        w_scr[...] = w_ref[...].astype(BF16)

    z = _dot(a_ref[...], w_scr[...])
    if epilogue == "sigmoid":
        z = _sigmoid(z)
    o_ref[...] = z.astype(o_ref.dtype)


def _wres_matmul(a, w, col0, n, *, rows, tm, tn, out_dtype, epilogue=None, name):
    k = a.shape[1]
    assert col0 % LANES == 0 and n % tn == 0 and rows % tm == 0
    return pl.pallas_call(
        functools.partial(_wres_mm_kernel, epilogue=epilogue),
        out_shape=jax.ShapeDtypeStruct((rows, n), out_dtype),
        grid=(n // tn, rows // tm),
        in_specs=[pl.BlockSpec((tm, k), lambda j, i: (i, 0)),
                  pl.BlockSpec((pl.Element(k), pl.Element(tn)), lambda j, i: (0, pl.multiple_of(col0 + j * tn, LANES)))],
        out_specs=pl.BlockSpec((tm, tn), lambda j, i: (i, j)),
        scratch_shapes=[pltpu.VMEM((k, tn), BF16)],
        compiler_params=_cparams(("arbitrary", "arbitrary")),
        name=name,
    )(a, w)


def _s5_param_kernel(are_ref, aim_ref, ldt_ref, bre_ref, bim_ref, cre_ref, cim_ref,
                      e_ref, c_ref, m_ref, a16_ref):
    t_n, hg, p_n = S5_CHUNK, S5_GROUP, S5_STATE
    gl_n = are_ref.shape[1]
    tau = lax.broadcasted_iota(jnp.int32, (t_n, gl_n, p_n), 0).astype(F32)
    taps = {}
    for d in range(2):
        a_re = are_ref[d]
        a_im = aim_ref[d]
        dt = jnp.exp(ldt_ref[d])
        lam = a_re * dt
        th = a_im * dt
        er = jnp.exp(lam)
        ab_re = er * jnp.cos(th)
        ab_im = er * jnp.sin(th)
        den = a_re * a_re + a_im * a_im
        x_re = ab_re - 1.0
        co_re = (x_re * a_re + ab_im * a_im) / den
        co_im = (ab_im * a_re - x_re * a_im) / den

        def power(tv):
            mag = jnp.exp(tv * lam)
            return mag * jnp.cos(tv * th), mag * jnp.sin(tv * th)

        pw_re, pw_im = power(tau if d == 0 else (t_n - 1.0 - tau))
        pe_re, pe_im = power((t_n - 1.0 - tau) if d == 0 else tau)
        pc_re, pc_im = power((tau + 1.0) if d == 0 else (t_n - tau))
        mag16 = jnp.exp(float(t_n) * lam)
        a16_ref[d, 0] = mag16 * jnp.cos(float(t_n) * th)
        a16_ref[d, 1] = mag16 * jnp.sin(float(t_n) * th)

        lo, hi = 2 * d * p_n, (2 * d + 1) * p_n
        for gl in range(gl_n):
            of = lambda x: x[:, gl:gl + 1, :]
            bt_re = bre_ref[d, gl].T
            bt_im = bim_ref[d, gl].T
            bb_re = co_re[gl:gl + 1] * bt_re - co_im[gl:gl + 1] * bt_im
            bb_im = co_re[gl:gl + 1] * bt_im + co_im[gl:gl + 1] * bt_re
            c_re = cre_ref[d, gl]
            c_im = cim_ref[d, gl]
            cp_re = (c_re[None] * of(pw_re) - c_im[None] * of(pw_im)).reshape(t_n * hg, p_n)
            cp_im = (c_re[None] * of(pw_im) + c_im[None] * of(pw_re)).reshape(t_n * hg, p_n)
            taps[d, gl] = _dot_nt(bb_re, cp_re, HIGHEST) - _dot_nt(bb_im, cp_im, HIGHEST)
            e_ref[0, :, gl, :, lo:hi] = of(pe_re) * bb_re[None] - of(pe_im) * bb_im[None]
            e_ref[0, :, gl, :, hi:hi + p_n] = of(pe_re) * bb_im[None] + of(pe_im) * bb_re[None]
            c_ref[0, :, gl, :, lo:hi] = c_re[None] * of(pc_re) - c_im[None] * of(pc_im)
            c_ref[0, :, gl, :, hi:hi + p_n] = -(c_re[None] * of(pc_im) + c_im[None] * of(pc_re))

    width = t_n * hg
    lane = lax.broadcasted_iota(jnp.int32, (hg, width), 1)
    for gl in range(gl_n):
        tf, tb = taps[0, gl], taps[1, gl]
        for t in range(t_n):
            sf = t * hg
            sb = (t_n - 1 - t) * hg
            f = tf if sf == 0 else jnp.where(lane >= sf, pltpu.roll(tf, sf, 1), 0.0)
            b = tb if sb == 0 else jnp.where(lane < width - sb, pltpu.roll(tb, width - sb, 1), 0.0)
            m_ref[0, t, gl] = f + b


def _s5_params(a_re, a_im, log_dt, b_re, b_im, c_re, c_im, nb):
    g_n = a_re.shape[1]
    gl_n = g_n // nb
    p_n, hg, t_n = S5_STATE, S5_GROUP, S5_CHUNK
    ldt = jnp.broadcast_to(log_dt[:, :, None], (2, g_n, p_n))
    spec3 = pl.BlockSpec((2, gl_n, p_n), lambda j: (0, j, 0))
    spec4 = pl.BlockSpec((2, gl_n, hg, p_n), lambda j: (0, j, 0, 0))
    spec_b = pl.BlockSpec((2, gl_n, p_n, hg), lambda j: (0, j, 0, 0))
    wide = 4 * p_n
    comp = jax.ShapeDtypeStruct((nb, t_n, gl_n, hg, wide), F32)
    comp_spec = pl.BlockSpec((1, t_n, gl_n, hg, wide), lambda j: (j, 0, 0, 0, 0))
    return pl.pallas_call(
        _s5_param_kernel,
        out_shape=(comp, comp, comp, jax.ShapeDtypeStruct((2, 2, g_n, p_n), F32)),
        grid=(nb,),
        in_specs=[spec3, spec3, spec3, spec_b, spec_b, spec4, spec4],
        out_specs=(comp_spec, comp_spec, comp_spec, pl.BlockSpec((2, 2, gl_n, p_n), lambda j: (0, 0, j, 0))),
        compiler_params=_cparams(("parallel",)),
        name="s5_params",
    )(a_re, a_im, ldt, b_re, b_im, c_re, c_im)


def _expand_block_diag(comp, rep_ref, mask_ref, w_scr):
    k = w_scr.shape[0]
    period = mask_ref.shape[1]
    cb = comp.astype(BF16)
    step = 512
    for c0 in range(0, k, step):
        blk = _dot(cb, rep_ref[:, c0:c0 + step]).astype(BF16)
        for q0 in range(0, step, period):
            w_scr[:, c0 + q0:c0 + q0 + period] = blk[:, q0:q0 + period] * mask_ref[...]


def _gather_chunk_rows(u_ref, rows, dst):
    t_n = S5_CHUNK
    for t in range(t_n):
        dst[:, t * LANES:(t + 1) * LANES] = u_ref[0, pl.ds(t, rows, stride=t_n), :].astype(dst.dtype)


def _s5_ein_kernel(u_ref, ec_ref, rep_ref, mask_ref, o_ref, w_scr, u_scr):
    _expand_block_diag(ec_ref[0], rep_ref, mask_ref, w_scr)
    _gather_chunk_rows(u_ref, u_scr.shape[0], u_scr)
    o_ref[0] = _dot(u_scr[...], w_scr[...])


def _s5_chunk_inputs(u_blk, e_comp, rep_e, mask_e):
    nb, m_all, _ = u_blk.shape
    k, cw = e_comp.shape[1:]
    rows = m_all // S5_CHUNK
    return pl.pallas_call(
        _s5_ein_kernel,
        out_shape=jax.ShapeDtypeStruct((nb, rows, k), F32),
        grid=(nb,),
        in_specs=[pl.BlockSpec((1, m_all, LANES), lambda j: (j, 0, 0)),
                  pl.BlockSpec((1, k, cw), lambda j: (j, 0, 0)),
                  pl.BlockSpec(rep_e.shape, lambda j: (0, 0)),
                  pl.BlockSpec(mask_e.shape, lambda j: (0, 0))],
        out_specs=pl.BlockSpec((1, rows, k), lambda j: (j, 0, 0)),
        scratch_shapes=[pltpu.VMEM((k, k), BF16), pltpu.VMEM((rows, k), BF16)],
        compiler_params=_cparams(("parallel",)),
        name="s5_chunk_inputs",
    )(u_blk, e_comp, rep_e, mask_e)


def _s5_bscan_kernel(e_ref, a_ref, o_ref, *, bsz, n_ctx, n_lat):
    q = e_ref.shape[2] // 4
    planes = lambda row, d: (row[:, (2 * d) * q:(2 * d + 1) * q], row[:, (2 * d + 1) * q:(2 * d + 2) * q])
    coef = [planes(a_ref[0], d) for d in range(2)]
    ctx0 = bsz * n_lat

    def advance(state, rows):
        new = []
        for (sr, si), (b, d), row in zip(state, [(b, d) for b in range(bsz) for d in range(2)], rows):
            ar, ai = coef[d]
            er, ei = planes(e_ref[0, pl.ds(row, 1), :], d)
            new.append((ar * sr - ai * si + er, ar * si + ai * sr + ei))
        return tuple(new)

    def ctx_step(s, state):
        rows = [ctx0 + b * n_ctx + (s if d == 0 else n_ctx - 1 - s) for b in range(bsz) for d in range(2)]
        return advance(state, rows)

    def lat_step(s, state):
        rows = [b * n_lat + (s if d == 0 else n_lat - 1 - s) for b in range(bsz) for d in range(2)]
        for (sr, si), (b, d), row in zip(state, [(b, d) for b in range(bsz) for d in range(2)], rows):
            o_ref[0, pl.ds(row, 1), (2 * d) * q:(2 * d + 1) * q] = sr
            o_ref[0, pl.ds(row, 1), (2 * d + 1) * q:(2 * d + 2) * q] = si
        return advance(state, rows)

    zero = jnp.zeros((1, q), F32)
    state = tuple((zero, zero) for _ in range(2 * bsz))
    state = lax.fori_loop(0, n_ctx, ctx_step, state)
    lax.fori_loop(0, n_lat, lat_step, state)


def _s5_bscan(e_rows, a_rows, bsz, n_ctx, n_lat):
    nb, rows, k = e_rows.shape
    return pl.pallas_call(
        functools.partial(_s5_bscan_kernel, bsz=bsz, n_ctx=n_ctx, n_lat=n_lat),
        out_shape=jax.ShapeDtypeStruct((nb, bsz * n_lat, k), F32),
        grid=(nb,),
        in_specs=[pl.BlockSpec((1, rows, k), lambda j: (j, 0, 0)),
                  pl.BlockSpec((1, 1, k), lambda j: (j, 0, 0))],
        out_specs=pl.BlockSpec((1, bsz * n_lat, k), lambda j: (j, 0, 0)),
        compiler_params=_cparams(("parallel",)),
        name="s5_scan",
    )(e_rows, a_rows)


def _s5_out_kernel(u_ref, s_ref, mc_ref, cc_ref, d_ref, rep_m_ref, mask_m_ref, rep_e_ref, mask_e_ref,
                   o_ref, wm_scr, wc_scr, u_scr):
    @pl.when(pl.program_id(1) == 0)
    def _():
        _expand_block_diag(mc_ref[0], rep_m_ref, mask_m_ref, wm_scr)
        _expand_block_diag(cc_ref[0], rep_e_ref, mask_e_ref, wc_scr)

    t_n = S5_CHUNK
    rows = s_ref.shape[1]
    _gather_chunk_rows(u_ref, rows, u_scr)
    u = u_scr[...]
    y = _dot(u.astype(BF16), wm_scr[...]) + _dot_nt(s_ref[0].astype(BF16), wc_scr[...])
    y = y + u * d_ref[0]
    for t in range(t_n):
        o_ref[0, pl.ds(t, rows, stride=t_n), :] = y[:, t * LANES:(t + 1) * LANES]


def _s5_outputs(u_blk, s_rows, m_comp, c_comp, d_rows, rep_m, mask_m, rep_e, mask_e):
    nb = u_blk.shape[0]
    rows, k = s_rows.shape[1:]
    cw = m_comp.shape[2]
    tr = rows // 2
    tok = tr * S5_CHUNK
    const = lambda a: pl.BlockSpec(a.shape, lambda j, i: (0, 0))
    return pl.pallas_call(
        _s5_out_kernel,
        out_shape=jax.ShapeDtypeStruct((nb, rows * S5_CHUNK, LANES), F32),
        grid=(nb, rows // tr),
        in_specs=[pl.BlockSpec((1, tok, LANES), lambda j, i: (j, i, 0)),
                  pl.BlockSpec((1, tr, k), lambda j, i: (j, i, 0)),
                  pl.BlockSpec((1, k, cw), lambda j, i: (j, 0, 0)),
                  pl.BlockSpec((1, k, cw), lambda j, i: (j, 0, 0)),
                  pl.BlockSpec((1, 1, k), lambda j, i: (j, 0, 0)),
                  const(rep_m), const(mask_m), const(rep_e), const(mask_e)],
        out_specs=pl.BlockSpec((1, tok, LANES), lambda j, i: (j, i, 0)),
        scratch_shapes=[pltpu.VMEM((k, k), BF16), pltpu.VMEM((k, k), BF16), pltpu.VMEM((tr, k), F32)],
        compiler_params=_cparams(("parallel", "arbitrary")),
        name="s5_outputs",
    )(u_blk, s_rows, m_comp, c_comp, d_rows, rep_m, mask_m, rep_e, mask_e)


def _s5_branch(u_blk, bsz, l_ctx, l_lat, a_re, a_im, log_dt, b_re, b_im, c_re, c_im, s5_d):
    nb, m_all, _ = u_blk.shape
    hg, t_n, p_n = S5_GROUP, S5_CHUNK, S5_STATE
    g_n = a_re.shape[1]
    gl_n = g_n // nb
    k = t_n * LANES
    n_lat = l_lat // t_n
    n_ctx = l_ctx // t_n
    e_c, c_c, m_c, a16 = _s5_params(a_re, a_im, log_dt, b_re, b_im, c_re, c_im, nb)
    cw = 4 * p_n
    e_comp, c_comp, m_comp = (a.reshape(nb, k, cw) for a in (e_c, c_c, m_c))

    row_gl = (jnp.arange(k) // hg) % gl_n
    col = jnp.arange(k)
    src = jnp.arange(cw)
    rep_e = ((src[:, None] // p_n == col[None, :] // (gl_n * p_n)) & (src[:, None] % p_n == col[None, :] % p_n)).astype(BF16)
    mask_e = (row_gl[:, None] == (jnp.arange(gl_n * p_n)[None, :] // p_n)).astype(BF16)
    rep_m = ((src[:, None] // hg == col[None, :] // (gl_n * hg)) & (src[:, None] % hg == col[None, :] % hg)).astype(BF16)
    mask_m = (row_gl[:, None] == (jnp.arange(gl_n * hg)[None, :] // hg)).astype(BF16)

    d_rows = jnp.tile(s5_d.reshape(nb, 1, gl_n * hg), (1, 1, t_n))
    a_rows = jnp.transpose(a16.reshape(2, 2, nb, gl_n * p_n), (2, 0, 1, 3)).reshape(nb, 1, 4 * gl_n * p_n)

    e_rows = _s5_chunk_inputs(u_blk, e_comp, rep_e, mask_e)
    s_rows = _s5_bscan(e_rows, a_rows, bsz, n_ctx, n_lat)
    return _s5_outputs(u_blk, s_rows, m_comp, c_comp, d_rows, rep_m, mask_m, rep_e, mask_e)


def _rw_prep_kernel(z_ref, zp_ref, zn_ref, mu_ref, w2_ref, a2_ref, g2_ref, w0_ref, a0_ref,
                    kk_w_ref, ka_ref, rk_ref, seg_ref, segt_ref,
                    r_ref, v_ref, kk_ref, g_ref, bonus_ref, lw_ref, kd_ref, be_ref,
                    *, tm, l_lat, rw):
    j = pl.program_id(1)
    z = z_ref[...].astype(F32)
    lat = j > 0
    tl = lax.broadcasted_iota(jnp.int32, (tm, 1), 0)
    tok = (j - 1) * tm + tl
    col = tl % GRID_W
    m_l = jnp.where(lat, col, tl) > 0
    m_r = jnp.where(lat, col - (GRID_W - 1), tl - (tm - 1)) < 0
    m_u = jnp.logical_and(lat, tok >= GRID_W)
    m_d = jnp.logical_and(lat, tok < l_lat - GRID_W)
    z_ext = jnp.concatenate([zp_ref[...], z_ref[...], zn_ref[...]], axis=0)
    rel = (lax.broadcasted_iota(jnp.int32, (tm, tm + 2 * GRID_W), 1) - GRID_W
           - lax.broadcasted_iota(jnp.int32, (tm, tm + 2 * GRID_W), 0))
    pick = (jnp.logical_and(rel == -1, m_l) | jnp.logical_and(rel == 1, m_r)
            | jnp.logical_and(rel == -GRID_W, m_u) | jnp.logical_and(rel == GRID_W, m_d))
    s = _dot(jnp.where(pick, 1.0, 0.0).astype(z_ext.dtype), z_ext)
    cnt = (m_l.astype(F32) + m_r.astype(F32)) + (m_u.astype(F32) + m_d.astype(F32))
    zs = z + (s * (1.0 / cnt) - z) * mu_ref[...]

    r = zs[:, 0:rw]
    k = zs[:, rw:2 * rw]
    v = zs[:, 2 * rw:3 * rw]
    o = 3 * rw
    wd = zs[:, o:o + LANES]
    ad = zs[:, o + LANES:o + 2 * LANES]
    gd = zs[:, o + 2 * LANES:o + 3 * LANES]

    seg = seg_ref[...]
    segt = segt_ref[...]

    def head_sum(t):
        return _dot(_dot(t.astype(BF16), seg).astype(BF16), segt)

    g_ref[0] = _dot(_sigmoid(gd), g2_ref[...], "bf16").astype(g_ref.dtype)
    kk = k * kk_w_ref[...]
    kk = kk * lax.rsqrt(head_sum(kk * kk) + 1e-12)
    wl = w0_ref[...] + _dot(jnp.tanh(wd), w2_ref[...], "bf16")
    al = a0_ref[...] + _dot(ad, a2_ref[...], "bf16")
    r_ref[0] = r.astype(r_ref.dtype)
    v_ref[0] = v.astype(v_ref.dtype)
    kk_ref[0] = kk.astype(kk_ref.dtype)
    k_sum = jnp.zeros_like(r)
    for d in range(2):
        a = _sigmoid(al[:, d * rw:(d + 1) * rw])
        k_d = k * (1.0 + (a - 1.0) * ka_ref[...])
        k_sum = k_sum + k_d
        lw_ref[d, 0] = -math.exp(-0.5) * _sigmoid(wl[:, d * rw:(d + 1) * rw])
        kd_ref[d, 0] = k_d.astype(kd_ref.dtype)
        be_ref[d, 0] = (kk * a).astype(be_ref.dtype)
    bonus_ref[0] = (head_sum(r * rk_ref[...] * k_sum) * v).astype(bonus_ref.dtype)


def _rw_prep(z_rw, bsz, l_ctx, l_lat, mu, w2bd, a2bd, g2, w0cat, a0cat, k_k, k_a, r_k_flat, seg, segt):
    cols = z_rw.shape[1]
    tm = l_ctx
    rw = g2.shape[1]
    l_all = l_ctx + l_lat
    nblk = l_all // tm
    lat_blk = l_lat // tm
    hb = tm // GRID_W
    lat_hblk = l_lat // GRID_W

    def main_blk(b, j):
        return jnp.where(j == 0, bsz * lat_blk + b, b * lat_blk + j - 1)

    def prev_halo(b, j):
        return b * lat_hblk + jnp.maximum((j - 1) * hb - 1, 0)

    def next_halo(b, j):
        return b * lat_hblk + jnp.minimum(jnp.maximum(j, 1) * hb, lat_hblk - 1)

    full = lambda shape: pl.BlockSpec(shape, lambda b, j: (0,) * len(shape))
    shared = jax.ShapeDtypeStruct((bsz, l_all, rw), BF16)
    lat_only = jax.ShapeDtypeStruct((bsz, l_lat, rw), BF16)
    per_dir = jax.ShapeDtypeStruct((2, bsz, l_all, rw), BF16)
    per_dir_f32 = jax.ShapeDtypeStruct((2, bsz, l_all, rw), F32)
    o_shared = pl.BlockSpec((1, tm, rw), lambda b, j: (b, j, 0))
    o_lat = pl.BlockSpec((1, tm, rw), lambda b, j: (b, jnp.maximum(j - 1, 0), 0))
    o_dir = pl.BlockSpec((2, 1, tm, rw), lambda b, j: (0, b, j, 0))
    return pl.pallas_call(
        functools.partial(_rw_prep_kernel, tm=tm, l_lat=l_lat, rw=rw),
        out_shape=(shared,) * 3 + (lat_only,) * 2 + (per_dir_f32, per_dir, per_dir),
        grid=(bsz, nblk),
        in_specs=[pl.BlockSpec((tm, cols), lambda b, j: (main_blk(b, j), 0)),
                  pl.BlockSpec((GRID_W, cols), lambda b, j: (prev_halo(b, j), 0)),
                  pl.BlockSpec((GRID_W, cols), lambda b, j: (next_halo(b, j), 0)),
                  full((1, cols)), full(w2bd.shape), full(a2bd.shape), full(g2.shape),
                  full((1, 2 * rw)), full((1, 2 * rw)), full((1, rw)), full((1, rw)), full((1, rw)),
                  full(seg.shape), full(segt.shape)],
        out_specs=(o_shared,) * 3 + (o_lat,) * 2 + (o_dir,) * 3,
        compiler_params=_cparams(("parallel", "arbitrary")),
        name="rwkv_prep",
    )(z_rw, z_rw, z_rw, mu.reshape(1, cols), w2bd, a2bd, g2, w0cat, a0cat,
      k_k.reshape(1, rw), k_a.reshape(1, rw), r_k_flat.reshape(1, rw), seg, segt)


def _stack_heads(x, head0):
    return jnp.concatenate([jnp.where(head0, x, 0.0), jnp.where(head0, 0.0, x)], axis=0)


def _rw_step_kernel(rf_ref, rb_ref, vf_ref, vb_ref, kkf_ref, kkb_ref, lwf_ref, lwb_ref, kdf_ref, kdb_ref,
                    bef_ref, beb_ref, yf_ref, yb_ref,
                    z_scr, gc_s, lhs_s, kb_s, kht_s, bht_s, vs_s, rt_s, aab_s, akr_s, arb_s, t_s, x_s, xin_s,
                    yc_s, wu_s, qm_s, yn_s, *, n_pairs, bsz):
    c_n = RW_CHUNK
    n2 = 2 * c_n

    lanes_of = lambda p: slice(p * LANES, (p + 1) * LANES)
    unit = lambda d, b, p: (d * bsz + b) * n_pairs + p
    units = [(d, b, p) for d in range(2) for b in range(bsz) for p in range(n_pairs)]

    @pl.when(pl.program_id(0) == 0)
    def _():
        z_scr[...] = jnp.zeros_like(z_scr)
        qm_s[...] = jnp.zeros_like(qm_s)
        yn_s[...] = jnp.zeros_like(yn_s)

    for d, b, p in units:
        u = unit(d, b, p)
        y_ref = yf_ref if d == 0 else yb_ref
        yz = _dot(qm_s[u], z_scr[u].astype(BF16)) + yn_s[u]
        y_ref[b, :, lanes_of(p)] = yz[:c_n] + yz[c_n:n2]
        z_scr[u] = yz[n2:]

    ri = lax.broadcasted_iota(jnp.int32, (c_n, c_n), 0)
    ci = lax.broadcasted_iota(jnp.int32, (c_n, c_n), 1)
    r2 = lax.broadcasted_iota(jnp.int32, (n2, n2), 0)
    c2 = lax.broadcasted_iota(jnp.int32, (n2, n2), 1)
    t2 = r2 % c_n
    i2 = c2 % c_n
    same_head = (r2 // c_n) == (c2 // c_n)
    diag = r2 == c2
    eye = diag.astype(F32)
    head0 = lax.broadcasted_iota(jnp.int32, (1, LANES), 1) < RW_HEAD
    tri = [(ri >= ci).astype(F32), (ri <= ci).astype(F32)]
    before = [jnp.logical_and(same_head, i2 < t2), jnp.logical_and(same_head, i2 > t2)]
    upto = [jnp.logical_or(m, diag) for m in before]

    def blk(s):
        return (r2 // s) == (c2 // s)

    srcs = [(rf_ref, vf_ref, kkf_ref, lwf_ref, kdf_ref, bef_ref), (rb_ref, vb_ref, kkb_ref, lwb_ref, kdb_ref, beb_ref)]

    for d in range(2):
        r_ref, v_ref, kk_ref, lw_ref, kd_ref, be_ref = srcs[d]
        for b in range(bsz):
            lw = lw_ref[0, b]
            cum = _dot(tri[d], lw, HIGHEST)
            tot = jnp.sum(lw, axis=0, keepdims=True)
            g_inv = jnp.exp(-cum)
            g_hat = jnp.exp(tot - cum)
            kd = kd_ref[0, b].astype(F32)
            be = be_ref[0, b].astype(F32)
            at = kk_ref[b].astype(F32) * jnp.exp(cum - lw)
            rt = r_ref[b].astype(F32) * jnp.exp(cum)
            kt = kd * g_inv
            bt = be * g_inv
            kh = kd * g_hat
            bh = be * g_hat
            gc_s[d * bsz + b] = jnp.exp(tot)
            v = v_ref[b].astype(F32)
            for p in range(n_pairs):
                u = unit(d, b, p)
                st = lambda x: _stack_heads(x[:, lanes_of(p)], head0)
                rt_p = st(rt)
                at_p = st(at).astype(BF16)
                lhs_s[u, :n2] = at_p
                lhs_s[u, n2:] = rt_p.astype(BF16)
                xin_s[u, :, :LANES] = at_p
                rt_s[u] = rt_p
                kb_s[u, :n2] = st(kt).astype(BF16)
                kb_s[u, n2:] = st(bt).astype(BF16)
                kht_s[u] = st(kh).T.astype(BF16)
                bht_s[u] = st(bh).T.astype(BF16)
                vs_s[u] = st(v).astype(BF16)

    for d, b, p in units:
        u = unit(d, b, p)
        g = _dot_nt(lhs_s[u], kb_s[u])
        a_ab = jnp.where(before[d], g[:n2, n2:], 0.0)
        akr_s[u, :n2] = jnp.where(before[d], g[:n2, :n2], 0.0).astype(BF16)
        akr_s[u, n2:] = jnp.where(upto[d], g[n2:, :n2], 0.0).astype(BF16)
        arb_s[u] = jnp.where(upto[d], g[n2:, n2:], 0.0).astype(BF16)
        aab_s[u] = a_ab
        t_s[u] = eye - jnp.where(blk(2), a_ab, 0.0)

    n_units = len(units)
    s = 2
    while s < c_n:
        off = jnp.logical_and(blk(2 * s), jnp.logical_not(blk(s)))
        for u in range(n_units):
            x_s[u] = _dot(t_s[u].astype(BF16), jnp.where(off, aab_s[u], 0.0).astype(BF16)).astype(BF16)
        for u in range(n_units):
            t = t_s[u]
            t_s[u] = t - _dot(x_s[u], t.astype(BF16))
        s *= 2

    for u in range(n_units):
        av = _dot(akr_s[u], vs_s[u])
        xin_s[u, :, LANES:] = av[:n2].astype(BF16)
        yc_s[u] = av[n2:]

    for u in range(n_units):
        wu_s[u] = _dot(t_s[u].astype(BF16), xin_s[u]).astype(BF16)
    for d, b, p in units:
        u = unit(d, b, p)
        wu = wu_s[u]
        q = _dot(arb_s[u], wu)
        bwu = _dot(bht_s[u], wu)
        qm_s[u, :n2] = (rt_s[u] - q[:, :LANES]).astype(BF16)
        qm_s[u, n2:] = (eye * gc_s[d * bsz + b][:, lanes_of(p)] - bwu[:, :LANES]).astype(BF16)
        yn_s[u, :n2] = yc_s[u] - q[:, LANES:]
        yn_s[u, n2:] = _dot(kht_s[u], vs_s[u]) - bwu[:, LANES:]


def _rw_scan(r, v, kk, lw, kd, be, l_ctx):
    bsz, l_all, rw = r.shape
    c_n = RW_CHUNK
    n_all = l_all // c_n
    n_ctx = l_ctx // c_n
    n_lat = n_all - n_ctx
    n_pairs = rw // LANES
    n_units = 2 * bsz * n_pairs
    n2 = 2 * c_n
    vm = lambda nr, cols, dt: pltpu.VMEM((n_units, nr, cols), dt)

    clamp = lambda s: jnp.minimum(s, n_all - 1)
    chunk_f = lambda s: clamp(s)
    chunk_b = lambda s: jnp.where(clamp(s) < n_ctx, n_ctx - 1 - clamp(s), n_all + n_ctx - 1 - clamp(s))
    prev = lambda s: jnp.maximum(s - 1, 0)
    out_f = lambda s: jnp.maximum(prev(s) - n_ctx, 0)
    out_b = lambda s: jnp.where(prev(s) < n_ctx, n_lat - 1, chunk_b(prev(s)) - n_ctx)

    sh_f = pl.BlockSpec((bsz, c_n, rw), lambda s: (0, chunk_f(s), 0))
    sh_b = pl.BlockSpec((bsz, c_n, rw), lambda s: (0, chunk_b(s), 0))
    pd_f = pl.BlockSpec((1, bsz, c_n, rw), lambda s: (0, 0, chunk_f(s), 0))
    pd_b = pl.BlockSpec((1, bsz, c_n, rw), lambda s: (1, 0, chunk_b(s), 0))
    y_shape = jax.ShapeDtypeStruct((bsz, n_lat * c_n, rw), F32)
    return pl.pallas_call(
        functools.partial(_rw_step_kernel, n_pairs=n_pairs, bsz=bsz),
        out_shape=(y_shape, y_shape),
        grid=(n_all + 1,),
        in_specs=[sh_f, sh_b, sh_f, sh_b, sh_f, sh_b, pd_f, pd_b, pd_f, pd_b, pd_f, pd_b],
        out_specs=(pl.BlockSpec((bsz, c_n, rw), lambda s: (0, out_f(s), 0)),
                   pl.BlockSpec((bsz, c_n, rw), lambda s: (0, out_b(s), 0))),
        scratch_shapes=[vm(n2, n2, F32),
                        pltpu.VMEM((2 * bsz, 1, rw), F32),
                        vm(2 * n2, n2, BF16),
                        vm(2 * n2, n2, BF16),
                        vm(n2, n2, BF16),
                        vm(n2, n2, BF16),
                        vm(n2, n2, BF16),
                        vm(n2, n2, F32),
                        vm(n2, n2, F32),
                        vm(2 * n2, n2, BF16),
                        vm(n2, n2, BF16),
                        vm(n2, n2, F32),
                        vm(n2, n2, BF16),
                        vm(n2, 2 * n2, BF16),
                        vm(n2, n2, F32),
                        vm(n2, 2 * n2, BF16),
                        vm(2 * n2, n2, BF16),
                        vm(2 * n2, n2, F32)],
        compiler_params=_cparams(("arbitrary",)),
        name="rwkv_scan",
    )(r, r, v, v, kk, kk, lw, lw, kd, kd, be, be)


def _s5_glu_kernel(y_ref, wa_ref, wb_ref, o_ref, h_scr):
    @pl.when(pl.program_id(1) == 0)
    def _():
        for jb in range(y_ref.shape[0]):
            h_scr[:, jb * LANES:(jb + 1) * LANES] = _gelu_tanh(y_ref[jb]).astype(BF16)

    h = h_scr[...]
    o_ref[...] = (_dot(h, wa_ref[...]) * _sigmoid(_dot(h, wb_ref[...]))).astype(o_ref.dtype)


def _s5_glu(y_blk, w, *, tm, tn):
    nb, m, _ = y_blk.shape
    k = nb * LANES
    n = w.shape[1] // 2
    nj = n // tn
    return pl.pallas_call(
        _s5_glu_kernel,
        out_shape=jax.ShapeDtypeStruct((m, n), BF16),
        grid=(m // tm, nj),
        in_specs=[pl.BlockSpec((nb, tm, LANES), lambda i, j: (0, i, 0)),
                  pl.BlockSpec((k, tn), lambda i, j: (0, j)),
                  pl.BlockSpec((k, tn), lambda i, j: (0, j + nj))],
        out_specs=pl.BlockSpec((tm, tn), lambda i, j: (i, j)),
        scratch_shapes=[pltpu.VMEM((tm, k), BF16)],
        compiler_params=_cparams(("parallel", "arbitrary")),
        name="s5_glu",
    )(y_blk, w, w)


def _merge_out_kernel(yf_ref, yb_ref, bonus_ref, g_ref, lnw_ref, lnb_ref, seg_ref, segt_ref,
                      ga_ref, gb_ref, s5_ref, wp_ref, wo_ref, x_ref, g1_ref, nw_ref, sh_ref, sc_ref,
                      h_ref, hn_ref):
    seg = seg_ref[...]
    segt = segt_ref[...]
    inv_n = 1.0 / RW_HEAD
    tm = yf_ref.shape[0]
    parts = [slice(q * (tm // 4), (q + 1) * (tm // 4)) for q in range(4)]

    def head_mean(vals, precise):
        red = _dot_hilo if precise else (lambda a, ind: _dot(a.astype(BF16), ind))
        sums = [red(v, seg) for v in vals]
        return [red(s, segt) * inv_n for s in sums]

    y = [yf_ref[p, :] + yb_ref[p, :] for p in parts]
    dy = [a - m for a, m in zip(y, head_mean(y, True))]
    var = head_mean([a * a for a in dy], False)
    y_rw = []
    for p, a, v in zip(parts, dy, var):
        t = a * lax.rsqrt(v + GN_EPS) * lnw_ref[...] + lnb_ref[...] + bonus_ref[p, :].astype(F32)
        y_rw.append((t * g_ref[p, :].astype(F32)).astype(BF16))
    rw_out = [_dot(a, wp_ref[...]) for a in y_rw]
    merged = [(ga_ref[p, :].astype(F32) * s5_ref[p, :].astype(F32) + gb_ref[p, :].astype(F32) * r).astype(BF16)
              for p, r in zip(parts, rw_out)]
    proj = [_dot(a, wo_ref[...]) for a in merged]
    for p, a in zip(parts, proj):
        h = x_ref[p, :] + g1_ref[0] * a
        h_ref[p, :] = h
        hn_ref[p, :] = _lnmod_rows(h, nw_ref[...], sh_ref[0], sc_ref[0]).astype(hn_ref.dtype)


def _merge_out(y_f, y_b, bonus, g, ln_w, ln_b, seg, segt, gates, s5_out, w_proj, w_o, x2, g_tab, nw,
               sh_tab, sc_tab, mod_row_of_block, *, tm):
    m, rw = y_f.shape
    n = w_proj.shape[1]
    full = lambda shape: pl.BlockSpec(shape, lambda i: (0,) * len(shape))
    rows = lambda width: pl.BlockSpec((tm, width), lambda i: (i, 0))
    mod_map = lambda i: (mod_row_of_block(i), 0, 0)
    return pl.pallas_call(
        _merge_out_kernel,
        out_shape=(jax.ShapeDtypeStruct((m, n), F32), jax.ShapeDtypeStruct((m, n), BF16)),
        grid=(m // tm,),
        in_specs=[rows(rw), rows(rw), rows(rw), rows(rw),
                  full((1, rw)), full((1, rw)), full(seg.shape), full(segt.shape),
                  pl.BlockSpec((tm, n), lambda i: (i, 0)), pl.BlockSpec((tm, n), lambda i: (i, 1)), rows(n),
                  full(w_proj.shape), full(w_o.shape), rows(n),
                  pl.BlockSpec((1, 1, n), mod_map), full((1, n)),
                  pl.BlockSpec((1, 1, n), mod_map), pl.BlockSpec((1, 1, n), mod_map)],
        out_specs=(rows(n), rows(n)),
        compiler_params=_cparams(("parallel",), vmem=VMEM_LIMIT + 8 * 1024 * 1024),
        name="merge_out_proj",
    )(y_f, y_b, bonus, g, ln_w.reshape(1, rw), ln_b.reshape(1, rw), seg, segt,
      gates, gates, s5_out, w_proj, w_o, x2, g_tab, nw.reshape(1, n), sh_tab, sc_tab)


def _wres_swiglu_kernel(a_ref, w1_ref, w3_ref, o_ref, w1_scr, w3_scr):
    @pl.when(pl.program_id(1) == 0)
    def _():
        w1_scr[...] = w1_ref[...].astype(BF16)
        w3_scr[...] = w3_ref[...].astype(BF16)

    a = a_ref[...]
    o_ref[...] = (_silu(_dot(a, w1_scr[...])) * _dot(a, w3_scr[...])).astype(o_ref.dtype)


def _wres_swiglu(a, w13, d_ff, *, tm, tn):
    m, k = a.shape
    nj = d_ff // tn
    return pl.pallas_call(
        _wres_swiglu_kernel,
        out_shape=jax.ShapeDtypeStruct((m, d_ff), BF16),
        grid=(nj, m // tm),
        in_specs=[pl.BlockSpec((tm, k), lambda j, i: (i, 0)),
                  pl.BlockSpec((k, tn), lambda j, i: (0, j)),
                  pl.BlockSpec((k, tn), lambda j, i: (0, j + nj))],
        out_specs=pl.BlockSpec((tm, tn), lambda j, i: (i, j)),
        scratch_shapes=[pltpu.VMEM((k, tn), BF16), pltpu.VMEM((k, tn), BF16)],
        compiler_params=_cparams(("arbitrary", "arbitrary")),
        name="ffn_up",
    )(a, w13, w13)


def _ffn_down_kernel(a_ref, w_ref, x_ref, g_ref, nf_ref, o_ref):
    kk = pl.program_id(1)
    part = _dot(a_ref[...], w_ref[...])

    @pl.when(kk == 0)
    def _():
        o_ref[...] = part

    @pl.when(jnp.logical_and(kk > 0, kk < pl.num_programs(1) - 1))
    def _():
        o_ref[...] += part

    @pl.when(kk == pl.num_programs(1) - 1)
    def _():
        h = x_ref[...] + g_ref[0] * (o_ref[...] + part)
        ms = jnp.mean(h * h, axis=-1, keepdims=True)
        o_ref[...] = h * lax.rsqrt(ms + NORM_EPS) * nf_ref[...]


def _ffn_down(a, w, x2, g_tab, mod_row_of_block, norm_f, *, tm, tk):
    m, k = a.shape
    n = w.shape[1]
    return pl.pallas_call(
        _ffn_down_kernel,
        out_shape=jax.ShapeDtypeStruct((m, n), F32),
        grid=(m // tm, k // tk),
        in_specs=[pl.BlockSpec((tm, tk), lambda i, kk: (i, kk)),
                  pl.BlockSpec((tk, n), lambda i, kk: (kk, 0)),
                  pl.BlockSpec((tm, n), lambda i, kk: (i, 0)),
                  pl.BlockSpec((1, 1, n), lambda i, kk: (mod_row_of_block(i), 0, 0)),
                  pl.BlockSpec((1, n), lambda i, kk: (0, 0))],
        out_specs=pl.BlockSpec((tm, n), lambda i, kk: (i, 0)),
        compiler_params=_cparams(("parallel", "arbitrary"), vmem=VMEM_LIMIT + 8 * 1024 * 1024),
        name="ffn_down",
    )(a, w, x2, g_tab, norm_f.reshape(1, n))


def kernel(x, c, ctx, c_ctx, ada_w, ada_b, norm1_w, w_in, rw_mu, s5_a_re, s5_a_im, s5_log_dt, s5_b_re, s5_b_im, s5_c_re, s5_c_im, s5_d, s5_glu_w, rw_w0, rw_w2, rw_a0, rw_a2, rw_g2, rw_k_k, rw_k_a, rw_r_k, rw_ln_w, rw_ln_b, rw_proj, w_o, norm2_w, ffn_w13, ffn_w2, norm_f):
    assert ada_w.shape[0] == 1, "single-layer block"
    bsz, l_lat, d = x.shape
    l_ctx = ctx.shape[1]
    l_all = l_ctx + l_lat
    s5w = s5_d.shape[1] * s5_d.shape[2]
    rw = rw_g2.shape[2]
    shift_cols = rw_mu.shape[1]
    d_ff = ffn_w2.shape[1]

    c_rows = jnp.concatenate([c, c_ctx[None], jnp.zeros((8 - bsz - 1, d), F32)], axis=0)
    mod = _modulation(c_rows, ada_w[0], ada_b[0])
    tab = lambda k: mod[:, k * d:(k + 1) * d].reshape(8, 1, d)
    sh1, sc1, g1, sh2, sc2, g2 = (tab(k) for k in range(N_MOD))
    ctx_row = bsz

    tm_in = 512
    m_lat = bsz * l_lat
    x2 = x.reshape(m_lat, d)
    n_mix = s5w + shift_cols
    m_all = m_lat + bsz * l_ctx
    lat_row = lambda t: (lambda i: i // (l_lat // t))

    def mix_mod_row(i):
        return jnp.where(i >= m_lat // tm_in, ctx_row, i // (l_lat // tm_in))

    h_all, u_blk = _lnmod_proj(x2, ctx.reshape(bsz * l_ctx, d), norm1_w[0], sh1, sc1, mix_mod_row,
                               w_in[0], s5w, tm=tm_in)
    tm_all = m_all // 8
    z_rw = _wres_matmul(h_all, w_in[0], s5w, shift_cols, rows=m_all, tm=tm_all, tn=shift_cols // 3,
                        out_dtype=BF16, name="in_proj_rw")
    tm = 1024
    gates = _wres_matmul(h_all, w_in[0], n_mix, w_in.shape[2] - n_mix, rows=m_lat, tm=tm, tn=1024,
                         out_dtype=BF16, epilogue="sigmoid", name="in_proj_gates")

    y_blk = _s5_branch(u_blk, bsz, l_ctx, l_lat, s5_a_re[0], s5_a_im[0], s5_log_dt[0],
                               s5_b_re[0], s5_b_im[0], s5_c_re[0], s5_c_im[0], s5_d[0])
    s5_out = _s5_glu(y_blk, s5_glu_w[0].astype(BF16), tm=tm, tn=1024)

    lora = rw_w2.shape[2]
    zl = jnp.zeros((lora, rw), F32)
    w2bd = jnp.concatenate([jnp.concatenate([rw_w2[0, 0], zl], axis=1),
                            jnp.concatenate([zl, rw_w2[0, 1]], axis=1)], axis=0)
    a2bd = jnp.concatenate([jnp.concatenate([rw_a2[0, 0], zl], axis=1),
                            jnp.concatenate([zl, rw_a2[0, 1]], axis=1)], axis=0)
    head_of = jnp.arange(rw) // RW_HEAD
    seg = (head_of[:, None] == jnp.arange(LANES)[None, :]).astype(BF16)
    segt = seg.T
    r, v, kk, g, bonus, lw, kd, be = _rw_prep(
        z_rw, bsz, l_ctx, l_lat, rw_mu[0], w2bd, a2bd, rw_g2[0], rw_w0[0].reshape(1, 2 * rw),
        rw_a0[0].reshape(1, 2 * rw), rw_k_k[0], rw_k_a[0], rw_r_k[0].reshape(rw), seg, segt)
    y_f, y_b = _rw_scan(r, v, kk, lw, kd, be, l_ctx)

    tm_o = 256
    h1, h1n = _merge_out(y_f.reshape(m_lat, rw), y_b.reshape(m_lat, rw), bonus.reshape(m_lat, rw),
                         g.reshape(m_lat, rw), rw_ln_w[0], rw_ln_b[0], seg, segt, gates, s5_out,
                         rw_proj[0].astype(BF16), w_o[0].astype(BF16), x2, g1, norm2_w[0], sh2, sc2,
                         lat_row(tm_o), tm=tm_o)

    act = _wres_swiglu(h1n, ffn_w13[0], d_ff, tm=tm, tn=512)
    tm_dn = 512
    out = _ffn_down(act, ffn_w2[0].astype(BF16), h1, g2, lat_row(tm_dn), norm_f, tm=tm_dn, tk=d_ff // 2)
    return out.reshape(bsz, l_lat, d)
```

```python
import functools
import math

import jax
import jax.numpy as jnp
from jax import lax
from jax.experimental import pallas as pl
from jax.experimental.pallas import tpu as pltpu

F32 = jnp.float32
BF16 = jnp.bfloat16
HIGHEST = lax.Precision.HIGHEST

N_MOD = 6
NORM_EPS = 1e-6
GN_EPS = 64e-5
GRID_W = 64
S5_GROUP = 16
S5_STATE = 64
S5_CHUNK = 16
RW_HEAD = 64
RW_CHUNK = 64
LANES = 128
VMEM_LIMIT = 48 * 1024 * 1024


def _cparams(sem, vmem=VMEM_LIMIT):
    return pltpu.CompilerParams(dimension_semantics=sem, vmem_limit_bytes=vmem)


def _operands(a, b, precision):
    if precision == "bf16":
        return a.astype(BF16), b.astype(BF16), None
    return a, b, precision


def _dot(a, b, precision=None):
    a, b, precision = _operands(a, b, precision)
    return jnp.dot(a, b, preferred_element_type=F32, precision=precision)


def _dot_nt(a, b, precision=None):
    a, b, precision = _operands(a, b, precision)
    return lax.dot_general(a, b, (((1,), (1,)), ((), ())), preferred_element_type=F32, precision=precision)


def _dot_hilo(a, ind):
    hi = a.astype(BF16)
    lo = (a - hi.astype(F32)).astype(BF16)
    return _dot(hi, ind) + _dot(lo, ind)


def _sigmoid(x):
    return 1.0 / (1.0 + jnp.exp(-x))


def _silu(x):
    return x * _sigmoid(x)


def _gelu_tanh(x):
    c = math.sqrt(2.0 / math.pi)
    return 0.5 * x * (1.0 + jnp.tanh(c * (x + 0.044715 * (x * x * x))))


def _mod_kernel(c_ref, w_ref, b_ref, o_ref):
    o_ref[...] = _dot_hilo(_silu(c_ref[...]), w_ref[...].astype(BF16)) + b_ref[...]


def _modulation(c_rows, ada_w, ada_b):
    m, d = c_rows.shape
    n = ada_w.shape[1]
    tn = 1024
    return pl.pallas_call(
        _mod_kernel,
        out_shape=jax.ShapeDtypeStruct((m, n), F32),
        grid=(n // tn,),
        in_specs=[pl.BlockSpec((m, d), lambda j: (0, 0)),
                  pl.BlockSpec((d, tn), lambda j: (0, j)),
                  pl.BlockSpec((1, tn), lambda j: (0, j))],
        out_specs=pl.BlockSpec((m, tn), lambda j: (0, j)),
        compiler_params=_cparams(("arbitrary",)),
        name="modulation",
    )(c_rows, ada_w, ada_b.reshape(1, n))


def _lnmod_rows(x, nw, sh, sc):
    ms = jnp.mean(x * x, axis=-1, keepdims=True)
    y = x * lax.rsqrt(ms + NORM_EPS) * nw
    return y * (1.0 + sc) + sh


def _lnmod_proj_kernel(x_ref, c_ref, nw_ref, sh_ref, sc_ref, w_ref, h_ref, u_ref, w_scr, *, n_lat_blocks):
    @pl.when(pl.program_id(0) == 0)
    def _():
        w_scr[...] = w_ref[...].astype(BF16)

    rows = jnp.where(pl.program_id(0) < n_lat_blocks, x_ref[...], c_ref[...])
    h = _lnmod_rows(rows, nw_ref[...], sh_ref[0], sc_ref[0]).astype(BF16)
    h_ref[...] = h
    z = _dot(h, w_scr[...])
    for jb in range(u_ref.shape[0]):
        u_ref[jb] = z[:, jb * LANES:(jb + 1) * LANES]


def _lnmod_proj(x2, c2, nw, sh_tab, sc_tab, mod_row_of_block, w, n, *, tm):
    m, d = x2.shape
    mc = c2.shape[0]
    nx, nc = m // tm, mc // tm
    mod_map = lambda i: (mod_row_of_block(i), 0, 0)
    return pl.pallas_call(
        functools.partial(_lnmod_proj_kernel, n_lat_blocks=nx),
        out_shape=(jax.ShapeDtypeStruct((m + mc, d), BF16),
                   jax.ShapeDtypeStruct((n // LANES, m + mc, LANES), F32)),
        grid=(nx + nc,),
        in_specs=[pl.BlockSpec((tm, d), lambda i: (jnp.minimum(i, nx - 1), 0)),
                  pl.BlockSpec((tm, d), lambda i: (jnp.maximum(i - nx, 0), 0)),
                  pl.BlockSpec((1, d), lambda i: (0, 0)),
                  pl.BlockSpec((1, 1, d), mod_map),
                  pl.BlockSpec((1, 1, d), mod_map),
                  pl.BlockSpec((d, n), lambda i: (0, 0))],
        out_specs=(pl.BlockSpec((tm, d), lambda i: (i, 0)),
                   pl.BlockSpec((n // LANES, tm, LANES), lambda i: (0, i, 0))),
        scratch_shapes=[pltpu.VMEM((d, n), BF16)],
        compiler_params=_cparams(("arbitrary",)),
        name="lnmod_in_proj_s5",
    )(x2, c2, nw.reshape(1, d), sh_tab, sc_tab, w)


def _wres_mm_kernel(a_ref, w_ref, o_ref, w_scr, *, epilogue):
    @pl.when(pl.program_id(1) == 0)
    def _():
        w_scr[...] = w_ref[...].astype(BF16)

    z = _dot(a_ref[...], w_scr[...])
    if epilogue == "sigmoid":
        z = _sigmoid(z)
    o_ref[...] = z.astype(o_ref.dtype)


def _wres_matmul(a, w, col0, n, *, rows, tm, tn, out_dtype, epilogue=None, name):
    k = a.shape[1]
    assert col0 % LANES == 0 and n % tn == 0 and rows % tm == 0
    return pl.pallas_call(
        functools.partial(_wres_mm_kernel, epilogue=epilogue),
        out_shape=jax.ShapeDtypeStruct((rows, n), out_dtype),
        grid=(n // tn, rows // tm),
        in_specs=[pl.BlockSpec((tm, k), lambda j, i: (i, 0)),
                  pl.BlockSpec((pl.Element(k), pl.Element(tn)), lambda j, i: (0, pl.multiple_of(col0 + j * tn, LANES)))],
        out_specs=pl.BlockSpec((tm, tn), lambda j, i: (i, j)),
        scratch_shapes=[pltpu.VMEM((k, tn), BF16)],
        compiler_params=_cparams(("arbitrary", "arbitrary")),
        name=name,
    )(a, w)


def _s5_param_kernel(are_ref, aim_ref, ldt_ref, bre_ref, bim_ref, cre_ref, cim_ref,
                      e_ref, c_ref, m_ref, a16_ref):
    t_n, hg, p_n = S5_CHUNK, S5_GROUP, S5_STATE
    gl_n = are_ref.shape[1]
    tau = lax.broadcasted_iota(jnp.int32, (t_n, gl_n, p_n), 0).astype(F32)
    taps = {}
    for d in range(2):
        a_re = are_ref[d]
        a_im = aim_ref[d]
        dt = jnp.exp(ldt_ref[d])
        lam = a_re * dt
        th = a_im * dt
        er = jnp.exp(lam)
        ab_re = er * jnp.cos(th)
        ab_im = er * jnp.sin(th)
        den = a_re * a_re + a_im * a_im
        x_re = ab_re - 1.0
        co_re = (x_re * a_re + ab_im * a_im) / den
        co_im = (ab_im * a_re - x_re * a_im) / den

        def power(tv):
            mag = jnp.exp(tv * lam)
            return mag * jnp.cos(tv * th), mag * jnp.sin(tv * th)

        pw_re, pw_im = power(tau if d == 0 else (t_n - 1.0 - tau))
        pe_re, pe_im = power((t_n - 1.0 - tau) if d == 0 else tau)
        pc_re, pc_im = power((tau + 1.0) if d == 0 else (t_n - tau))
        mag16 = jnp.exp(float(t_n) * lam)
        a16_ref[d, 0] = mag16 * jnp.cos(float(t_n) * th)
        a16_ref[d, 1] = mag16 * jnp.sin(float(t_n) * th)

        lo, hi = 2 * d * p_n, (2 * d + 1) * p_n
        for gl in range(gl_n):
            of = lambda x: x[:, gl:gl + 1, :]
            bt_re = bre_ref[d, gl].T
            bt_im = bim_ref[d, gl].T
            bb_re = co_re[gl:gl + 1] * bt_re - co_im[gl:gl + 1] * bt_im
            bb_im = co_re[gl:gl + 1] * bt_im + co_im[gl:gl + 1] * bt_re
            c_re = cre_ref[d, gl]
            c_im = cim_ref[d, gl]
            cp_re = (c_re[None] * of(pw_re) - c_im[None] * of(pw_im)).reshape(t_n * hg, p_n)
            cp_im = (c_re[None] * of(pw_im) + c_im[None] * of(pw_re)).reshape(t_n * hg, p_n)
            taps[d, gl] = _dot_nt(bb_re, cp_re, HIGHEST) - _dot_nt(bb_im, cp_im, HIGHEST)
            e_ref[0, :, gl, :, lo:hi] = of(pe_re) * bb_re[None] - of(pe_im) * bb_im[None]
            e_ref[0, :, gl, :, hi:hi + p_n] = of(pe_re) * bb_im[None] + of(pe_im) * bb_re[None]
            c_ref[0, :, gl, :, lo:hi] = c_re[None] * of(pc_re) - c_im[None] * of(pc_im)
            c_ref[0, :, gl, :, hi:hi + p_n] = -(c_re[None] * of(pc_im) + c_im[None] * of(pc_re))

    width = t_n * hg
    lane = lax.broadcasted_iota(jnp.int32, (hg, width), 1)
    for gl in range(gl_n):
        tf, tb = taps[0, gl], taps[1, gl]
        for t in range(t_n):
            sf = t * hg
            sb = (t_n - 1 - t) * hg
            f = tf if sf == 0 else jnp.where(lane >= sf, pltpu.roll(tf, sf, 1), 0.0)
            b = tb if sb == 0 else jnp.where(lane < width - sb, pltpu.roll(tb, width - sb, 1), 0.0)
            m_ref[0, t, gl] = f + b


def _s5_params(a_re, a_im, log_dt, b_re, b_im, c_re, c_im, nb):
    g_n = a_re.shape[1]
    gl_n = g_n // nb
    p_n, hg, t_n = S5_STATE, S5_GROUP, S5_CHUNK
    ldt = jnp.broadcast_to(log_dt[:, :, None], (2, g_n, p_n))
    spec3 = pl.BlockSpec((2, gl_n, p_n), lambda j: (0, j, 0))
    spec4 = pl.BlockSpec((2, gl_n, hg, p_n), lambda j: (0, j, 0, 0))
    spec_b = pl.BlockSpec((2, gl_n, p_n, hg), lambda j: (0, j, 0, 0))
    wide = 4 * p_n
    comp = jax.ShapeDtypeStruct((nb, t_n, gl_n, hg, wide), F32)
    comp_spec = pl.BlockSpec((1, t_n, gl_n, hg, wide), lambda j: (j, 0, 0, 0, 0))
    return pl.pallas_call(
        _s5_param_kernel,
        out_shape=(comp, comp, comp, jax.ShapeDtypeStruct((2, 2, g_n, p_n), F32)),
        grid=(nb,),
        in_specs=[spec3, spec3, spec3, spec_b, spec_b, spec4, spec4],
        out_specs=(comp_spec, comp_spec, comp_spec, pl.BlockSpec((2, 2, gl_n, p_n), lambda j: (0, 0, j, 0))),
        compiler_params=_cparams(("parallel",)),
        name="s5_params",
    )(a_re, a_im, ldt, b_re, b_im, c_re, c_im)


def _expand_block_diag(comp, rep_ref, mask_ref, w_scr):
    k = w_scr.shape[0]
    period = mask_ref.shape[1]
    cb = comp.astype(BF16)
    step = 512
    for c0 in range(0, k, step):
        blk = _dot(cb, rep_ref[:, c0:c0 + step]).astype(BF16)
        for q0 in range(0, step, period):
            w_scr[:, c0 + q0:c0 + q0 + period] = blk[:, q0:q0 + period] * mask_ref[...]


def _gather_chunk_rows(u_ref, rows, dst):
    t_n = S5_CHUNK
    for t in range(t_n):
        dst[:, t * LANES:(t + 1) * LANES] = u_ref[0, pl.ds(t, rows, stride=t_n), :].astype(dst.dtype)


def _s5_ein_kernel(u_ref, ec_ref, rep_ref, mask_ref, o_ref, w_scr, u_scr):
    _expand_block_diag(ec_ref[0], rep_ref, mask_ref, w_scr)
    _gather_chunk_rows(u_ref, u_scr.shape[0], u_scr)
    o_ref[0] = _dot(u_scr[...], w_scr[...])


def _s5_chunk_inputs(u_blk, e_comp, rep_e, mask_e):
    nb, m_all, _ = u_blk.shape
    k, cw = e_comp.shape[1:]
    rows = m_all // S5_CHUNK
    return pl.pallas_call(
        _s5_ein_kernel,
        out_shape=jax.ShapeDtypeStruct((nb, rows, k), F32),
        grid=(nb,),
        in_specs=[pl.BlockSpec((1, m_all, LANES), lambda j: (j, 0, 0)),
                  pl.BlockSpec((1, k, cw), lambda j: (j, 0, 0)),
                  pl.BlockSpec(rep_e.shape, lambda j: (0, 0)),
                  pl.BlockSpec(mask_e.shape, lambda j: (0, 0))],
        out_specs=pl.BlockSpec((1, rows, k), lambda j: (j, 0, 0)),
        scratch_shapes=[pltpu.VMEM((k, k), BF16), pltpu.VMEM((rows, k), BF16)],
        compiler_params=_cparams(("parallel",)),
        name="s5_chunk_inputs",
    )(u_blk, e_comp, rep_e, mask_e)


def _s5_bscan_kernel(e_ref, a_ref, o_ref, *, bsz, n_ctx, n_lat):
    q = e_ref.shape[2] // 4
    planes = lambda row, d: (row[:, (2 * d) * q:(2 * d + 1) * q], row[:, (2 * d + 1) * q:(2 * d + 2) * q])
    coef = [planes(a_ref[0], d) for d in range(2)]
    ctx0 = bsz * n_lat

    def advance(state, rows):
        new = []
        for (sr, si), (b, d), row in zip(state, [(b, d) for b in range(bsz) for d in range(2)], rows):
            ar, ai = coef[d]
            er, ei = planes(e_ref[0, pl.ds(row, 1), :], d)
            new.append((ar * sr - ai * si + er, ar * si + ai * sr + ei))
        return tuple(new)

    def ctx_step(s, state):
        rows = [ctx0 + b * n_ctx + (s if d == 0 else n_ctx - 1 - s) for b in range(bsz) for d in range(2)]
        return advance(state, rows)

    def lat_step(s, state):
        rows = [b * n_lat + (s if d == 0 else n_lat - 1 - s) for b in range(bsz) for d in range(2)]
        for (sr, si), (b, d), row in zip(state, [(b, d) for b in range(bsz) for d in range(2)], rows):
            o_ref[0, pl.ds(row, 1), (2 * d) * q:(2 * d + 1) * q] = sr
            o_ref[0, pl.ds(row, 1), (2 * d + 1) * q:(2 * d + 2) * q] = si
        return advance(state, rows)

    zero = jnp.zeros((1, q), F32)
    state = tuple((zero, zero) for _ in range(2 * bsz))
    state = lax.fori_loop(0, n_ctx, ctx_step, state)
    lax.fori_loop(0, n_lat, lat_step, state)


def _s5_bscan(e_rows, a_rows, bsz, n_ctx, n_lat):
    nb, rows, k = e_rows.shape
    return pl.pallas_call(
        functools.partial(_s5_bscan_kernel, bsz=bsz, n_ctx=n_ctx, n_lat=n_lat),
        out_shape=jax.ShapeDtypeStruct((nb, bsz * n_lat, k), F32),
        grid=(nb,),
        in_specs=[pl.BlockSpec((1, rows, k), lambda j: (j, 0, 0)),
                  pl.BlockSpec((1, 1, k), lambda j: (j, 0, 0))],
        out_specs=pl.BlockSpec((1, bsz * n_lat, k), lambda j: (j, 0, 0)),
        compiler_params=_cparams(("parallel",)),
        name="s5_scan",
    )(e_rows, a_rows)


def _s5_out_kernel(u_ref, s_ref, mc_ref, cc_ref, d_ref, rep_m_ref, mask_m_ref, rep_e_ref, mask_e_ref,
                   o_ref, wm_scr, wc_scr, u_scr):
    @pl.when(pl.program_id(1) == 0)
    def _():
        _expand_block_diag(mc_ref[0], rep_m_ref, mask_m_ref, wm_scr)
        _expand_block_diag(cc_ref[0], rep_e_ref, mask_e_ref, wc_scr)

    t_n = S5_CHUNK
    rows = s_ref.shape[1]
    _gather_chunk_rows(u_ref, rows, u_scr)
    u = u_scr[...]
    y = _dot(u.astype(BF16), wm_scr[...]) + _dot_nt(s_ref[0].astype(BF16), wc_scr[...])
    y = y + u * d_ref[0]
    for t in range(t_n):
        o_ref[0, pl.ds(t, rows, stride=t_n), :] = y[:, t * LANES:(t + 1) * LANES]


def _s5_outputs(u_blk, s_rows, m_comp, c_comp, d_rows, rep_m, mask_m, rep_e, mask_e):
    nb = u_blk.shape[0]
    rows, k = s_rows.shape[1:]
    cw = m_comp.shape[2]
    tr = rows // 2
    tok = tr * S5_CHUNK
    const = lambda a: pl.BlockSpec(a.shape, lambda j, i: (0, 0))
    return pl.pallas_call(
        _s5_out_kernel,
        out_shape=jax.ShapeDtypeStruct((nb, rows * S5_CHUNK, LANES), F32),
        grid=(nb, rows // tr),
        in_specs=[pl.BlockSpec((1, tok, LANES), lambda j, i: (j, i, 0)),
                  pl.BlockSpec((1, tr, k), lambda j, i: (j, i, 0)),
                  pl.BlockSpec((1, k, cw), lambda j, i: (j, 0, 0)),
                  pl.BlockSpec((1, k, cw), lambda j, i: (j, 0, 0)),
                  pl.BlockSpec((1, 1, k), lambda j, i: (j, 0, 0)),
                  const(rep_m), const(mask_m), const(rep_e), const(mask_e)],
        out_specs=pl.BlockSpec((1, tok, LANES), lambda j, i: (j, i, 0)),
        scratch_shapes=[pltpu.VMEM((k, k), BF16), pltpu.VMEM((k, k), BF16), pltpu.VMEM((tr, k), F32)],
        compiler_params=_cparams(("parallel", "arbitrary")),
        name="s5_outputs",
    )(u_blk, s_rows, m_comp, c_comp, d_rows, rep_m, mask_m, rep_e, mask_e)


def _s5_branch(u_blk, bsz, l_ctx, l_lat, a_re, a_im, log_dt, b_re, b_im, c_re, c_im, s5_d):
    nb, m_all, _ = u_blk.shape
    hg, t_n, p_n = S5_GROUP, S5_CHUNK, S5_STATE
    g_n = a_re.shape[1]
    gl_n = g_n // nb
    k = t_n * LANES
    n_lat = l_lat // t_n
    n_ctx = l_ctx // t_n
    e_c, c_c, m_c, a16 = _s5_params(a_re, a_im, log_dt, b_re, b_im, c_re, c_im, nb)
    cw = 4 * p_n
    e_comp, c_comp, m_comp = (a.reshape(nb, k, cw) for a in (e_c, c_c, m_c))

    row_gl = (jnp.arange(k) // hg) % gl_n
    col = jnp.arange(k)
    src = jnp.arange(cw)
    rep_e = ((src[:, None] // p_n == col[None, :] // (gl_n * p_n)) & (src[:, None] % p_n == col[None, :] % p_n)).astype(BF16)
    mask_e = (row_gl[:, None] == (jnp.arange(gl_n * p_n)[None, :] // p_n)).astype(BF16)
    rep_m = ((src[:, None] // hg == col[None, :] // (gl_n * hg)) & (src[:, None] % hg == col[None, :] % hg)).astype(BF16)
    mask_m = (row_gl[:, None] == (jnp.arange(gl_n * hg)[None, :] // hg)).astype(BF16)

    d_rows = jnp.tile(s5_d.reshape(nb, 1, gl_n * hg), (1, 1, t_n))
    a_rows = jnp.transpose(a16.reshape(2, 2, nb, gl_n * p_n), (2, 0, 1, 3)).reshape(nb, 1, 4 * gl_n * p_n)

    e_rows = _s5_chunk_inputs(u_blk, e_comp, rep_e, mask_e)
    s_rows = _s5_bscan(e_rows, a_rows, bsz, n_ctx, n_lat)
    return _s5_outputs(u_blk, s_rows, m_comp, c_comp, d_rows, rep_m, mask_m, rep_e, mask_e)


def _rw_prep_kernel(z_ref, zp_ref, zn_ref, mu_ref, w2_ref, a2_ref, g2_ref, w0_ref, a0_ref,
                    kk_w_ref, ka_ref, rk_ref, seg_ref, segt_ref,
                    r_ref, v_ref, kk_ref, g_ref, bonus_ref, lw_ref, kd_ref, be_ref,
                    *, tm, l_lat, rw):
    j = pl.program_id(1)
    z = z_ref[...].astype(F32)
    lat = j > 0
    tl = lax.broadcasted_iota(jnp.int32, (tm, 1), 0)
    tok = (j - 1) * tm + tl
    col = tl % GRID_W
    m_l = jnp.where(lat, col, tl) > 0
    m_r = jnp.where(lat, col - (GRID_W - 1), tl - (tm - 1)) < 0
    m_u = jnp.logical_and(lat, tok >= GRID_W)
    m_d = jnp.logical_and(lat, tok < l_lat - GRID_W)
    z_ext = jnp.concatenate([zp_ref[...], z_ref[...], zn_ref[...]], axis=0)
    rel = (lax.broadcasted_iota(jnp.int32, (tm, tm + 2 * GRID_W), 1) - GRID_W
           - lax.broadcasted_iota(jnp.int32, (tm, tm + 2 * GRID_W), 0))
    pick = (jnp.logical_and(rel == -1, m_l) | jnp.logical_and(rel == 1, m_r)
            | jnp.logical_and(rel == -GRID_W, m_u) | jnp.logical_and(rel == GRID_W, m_d))
    s = _dot(jnp.where(pick, 1.0, 0.0).astype(z_ext.dtype), z_ext)
    cnt = (m_l.astype(F32) + m_r.astype(F32)) + (m_u.astype(F32) + m_d.astype(F32))
    zs = z + (s * (1.0 / cnt) - z) * mu_ref[...]

    r = zs[:, 0:rw]
    k = zs[:, rw:2 * rw]
    v = zs[:, 2 * rw:3 * rw]
    o = 3 * rw
    wd = zs[:, o:o + LANES]
    ad = zs[:, o + LANES:o + 2 * LANES]
    gd = zs[:, o + 2 * LANES:o + 3 * LANES]

    seg = seg_ref[...]
    segt = segt_ref[...]

    def head_sum(t):
        return _dot(_dot(t.astype(BF16), seg).astype(BF16), segt)

    g_ref[0] = _dot(_sigmoid(gd), g2_ref[...], "bf16").astype(g_ref.dtype)
    kk = k * kk_w_ref[...]
    kk = kk * lax.rsqrt(head_sum(kk * kk) + 1e-12)
    wl = w0_ref[...] + _dot(jnp.tanh(wd), w2_ref[...], "bf16")
    al = a0_ref[...] + _dot(ad, a2_ref[...], "bf16")
    r_ref[0] = r.astype(r_ref.dtype)
    v_ref[0] = v.astype(v_ref.dtype)
    kk_ref[0] = kk.astype(kk_ref.dtype)
    k_sum = jnp.zeros_like(r)
    for d in range(2):
        a = _sigmoid(al[:, d * rw:(d + 1) * rw])
        k_d = k * (1.0 + (a - 1.0) * ka_ref[...])
        k_sum = k_sum + k_d
        lw_ref[d, 0] = -math.exp(-0.5) * _sigmoid(wl[:, d * rw:(d + 1) * rw])
        kd_ref[d, 0] = k_d.astype(kd_ref.dtype)
        be_ref[d, 0] = (kk * a).astype(be_ref.dtype)
    bonus_ref[0] = (head_sum(r * rk_ref[...] * k_sum) * v).astype(bonus_ref.dtype)


def _rw_prep(z_rw, bsz, l_ctx, l_lat, mu, w2bd, a2bd, g2, w0cat, a0cat, k_k, k_a, r_k_flat, seg, segt):
    cols = z_rw.shape[1]
    tm = l_ctx
    rw = g2.shape[1]
    l_all = l_ctx + l_lat
    nblk = l_all // tm
    lat_blk = l_lat // tm
    hb = tm // GRID_W
    lat_hblk = l_lat // GRID_W

    def main_blk(b, j):
        return jnp.where(j == 0, bsz * lat_blk + b, b * lat_blk + j - 1)

    def prev_halo(b, j):
        return b * lat_hblk + jnp.maximum((j - 1) * hb - 1, 0)

    def next_halo(b, j):
        return b * lat_hblk + jnp.minimum(jnp.maximum(j, 1) * hb, lat_hblk - 1)

    full = lambda shape: pl.BlockSpec(shape, lambda b, j: (0,) * len(shape))
    shared = jax.ShapeDtypeStruct((bsz, l_all, rw), BF16)
    lat_only = jax.ShapeDtypeStruct((bsz, l_lat, rw), BF16)
    per_dir = jax.ShapeDtypeStruct((2, bsz, l_all, rw), BF16)
    per_dir_f32 = jax.ShapeDtypeStruct((2, bsz, l_all, rw), F32)
    o_shared = pl.BlockSpec((1, tm, rw), lambda b, j: (b, j, 0))
    o_lat = pl.BlockSpec((1, tm, rw), lambda b, j: (b, jnp.maximum(j - 1, 0), 0))
    o_dir = pl.BlockSpec((2, 1, tm, rw), lambda b, j: (0, b, j, 0))
    return pl.pallas_call(
        functools.partial(_rw_prep_kernel, tm=tm, l_lat=l_lat, rw=rw),
        out_shape=(shared,) * 3 + (lat_only,) * 2 + (per_dir_f32, per_dir, per_dir),
        grid=(bsz, nblk),
        in_specs=[pl.BlockSpec((tm, cols), lambda b, j: (main_blk(b, j), 0)),
                  pl.BlockSpec((GRID_W, cols), lambda b, j: (prev_halo(b, j), 0)),
                  pl.BlockSpec((GRID_W, cols), lambda b, j: (next_halo(b, j), 0)),
                  full((1, cols)), full(w2bd.shape), full(a2bd.shape), full(g2.shape),
                  full((1, 2 * rw)), full((1, 2 * rw)), full((1, rw)), full((1, rw)), full((1, rw)),
                  full(seg.shape), full(segt.shape)],
        out_specs=(o_shared,) * 3 + (o_lat,) * 2 + (o_dir,) * 3,
        compiler_params=_cparams(("parallel", "arbitrary")),
        name="rwkv_prep",
    )(z_rw, z_rw, z_rw, mu.reshape(1, cols), w2bd, a2bd, g2, w0cat, a0cat,
      k_k.reshape(1, rw), k_a.reshape(1, rw), r_k_flat.reshape(1, rw), seg, segt)


def _stack_heads(x, head0):
    return jnp.concatenate([jnp.where(head0, x, 0.0), jnp.where(head0, 0.0, x)], axis=0)


def _rw_step_kernel(rf_ref, rb_ref, vf_ref, vb_ref, kkf_ref, kkb_ref, lwf_ref, lwb_ref, kdf_ref, kdb_ref,
                    bef_ref, beb_ref, yf_ref, yb_ref,
                    z_scr, gc_s, lhs_s, kb_s, kht_s, bht_s, vs_s, rt_s, aab_s, akr_s, arb_s, t_s, x_s, xin_s,
                    yc_s, wu_s, qm_s, yn_s, *, n_pairs, bsz):
    c_n = RW_CHUNK
    n2 = 2 * c_n

    lanes_of = lambda p: slice(p * LANES, (p + 1) * LANES)
    unit = lambda d, b, p: (d * bsz + b) * n_pairs + p
    units = [(d, b, p) for d in range(2) for b in range(bsz) for p in range(n_pairs)]

    @pl.when(pl.program_id(0) == 0)
    def _():
        z_scr[...] = jnp.zeros_like(z_scr)
        qm_s[...] = jnp.zeros_like(qm_s)
        yn_s[...] = jnp.zeros_like(yn_s)

    for d, b, p in units:
        u = unit(d, b, p)
        y_ref = yf_ref if d == 0 else yb_ref
        yz = _dot(qm_s[u], z_scr[u].astype(BF16)) + yn_s[u]
        y_ref[b, :, lanes_of(p)] = yz[:c_n] + yz[c_n:n2]
        z_scr[u] = yz[n2:]

    ri = lax.broadcasted_iota(jnp.int32, (c_n, c_n), 0)
    ci = lax.broadcasted_iota(jnp.int32, (c_n, c_n), 1)
    r2 = lax.broadcasted_iota(jnp.int32, (n2, n2), 0)
    c2 = lax.broadcasted_iota(jnp.int32, (n2, n2), 1)
    t2 = r2 % c_n
    i2 = c2 % c_n
    same_head = (r2 // c_n) == (c2 // c_n)
    diag = r2 == c2
    eye = diag.astype(F32)
    head0 = lax.broadcasted_iota(jnp.int32, (1, LANES), 1) < RW_HEAD
    tri = [(ri >= ci).astype(F32), (ri <= ci).astype(F32)]
    before = [jnp.logical_and(same_head, i2 < t2), jnp.logical_and(same_head, i2 > t2)]
    upto = [jnp.logical_or(m, diag) for m in before]

    def blk(s):
        return (r2 // s) == (c2 // s)

    srcs = [(rf_ref, vf_ref, kkf_ref, lwf_ref, kdf_ref, bef_ref), (rb_ref, vb_ref, kkb_ref, lwb_ref, kdb_ref, beb_ref)]

    for d in range(2):
        r_ref, v_ref, kk_ref, lw_ref, kd_ref, be_ref = srcs[d]
        for b in range(bsz):
            lw = lw_ref[0, b]
            cum = _dot(tri[d], lw, HIGHEST)
            tot = jnp.sum(lw, axis=0, keepdims=True)
            g_inv = jnp.exp(-cum)
            g_hat = jnp.exp(tot - cum)
            kd = kd_ref[0, b].astype(F32)
            be = be_ref[0, b].astype(F32)
            at = kk_ref[b].astype(F32) * jnp.exp(cum - lw)
            rt = r_ref[b].astype(F32) * jnp.exp(cum)
            kt = kd * g_inv
            bt = be * g_inv
            kh = kd * g_hat
            bh = be * g_hat
            gc_s[d * bsz + b] = jnp.exp(tot)
            v = v_ref[b].astype(F32)
            for p in range(n_pairs):
                u = unit(d, b, p)
                st = lambda x: _stack_heads(x[:, lanes_of(p)], head0)
                rt_p = st(rt)
                at_p = st(at).astype(BF16)
                lhs_s[u, :n2] = at_p
                lhs_s[u, n2:] = rt_p.astype(BF16)
                xin_s[u, :, :LANES] = at_p
                rt_s[u] = rt_p
                kb_s[u, :n2] = st(kt).astype(BF16)
                kb_s[u, n2:] = st(bt).astype(BF16)
                kht_s[u] = st(kh).T.astype(BF16)
                bht_s[u] = st(bh).T.astype(BF16)
                vs_s[u] = st(v).astype(BF16)

    for d, b, p in units:
        u = unit(d, b, p)
        g = _dot_nt(lhs_s[u], kb_s[u])
        a_ab = jnp.where(before[d], g[:n2, n2:], 0.0)
        akr_s[u, :n2] = jnp.where(before[d], g[:n2, :n2], 0.0).astype(BF16)
        akr_s[u, n2:] = jnp.where(upto[d], g[n2:, :n2], 0.0).astype(BF16)
        arb_s[u] = jnp.where(upto[d], g[n2:, n2:], 0.0).astype(BF16)
        aab_s[u] = a_ab
        t_s[u] = eye - jnp.where(blk(2), a_ab, 0.0)

    n_units = len(units)
    s = 2
    while s < c_n:
        off = jnp.logical_and(blk(2 * s), jnp.logical_not(blk(s)))
        for u in range(n_units):
            x_s[u] = _dot(t_s[u].astype(BF16), jnp.where(off, aab_s[u], 0.0).astype(BF16)).astype(BF16)
        for u in range(n_units):
            t = t_s[u]
            t_s[u] = t - _dot(x_s[u], t.astype(BF16))
        s *= 2

    for u in range(n_units):
        av = _dot(akr_s[u], vs_s[u])
        xin_s[u, :, LANES:] = av[:n2].astype(BF16)
        yc_s[u] = av[n2:]

    for u in range(n_units):
        wu_s[u] = _dot(t_s[u].astype(BF16), xin_s[u]).astype(BF16)
    for d, b, p in units:
        u = unit(d, b, p)
        wu = wu_s[u]
        q = _dot(arb_s[u], wu)
        bwu = _dot(bht_s[u], wu)
        qm_s[u, :n2] = (rt_s[u] - q[:, :LANES]).astype(BF16)
        qm_s[u, n2:] = (eye * gc_s[d * bsz + b][:, lanes_of(p)] - bwu[:, :LANES]).astype(BF16)
        yn_s[u, :n2] = yc_s[u] - q[:, LANES:]
        yn_s[u, n2:] = _dot(kht_s[u], vs_s[u]) - bwu[:, LANES:]


def _rw_scan(r, v, kk, lw, kd, be, l_ctx):
    bsz, l_all, rw = r.shape
    c_n = RW_CHUNK
    n_all = l_all // c_n
    n_ctx = l_ctx // c_n
    n_lat = n_all - n_ctx
    n_pairs = rw // LANES
    n_units = 2 * bsz * n_pairs
    n2 = 2 * c_n
    vm = lambda nr, cols, dt: pltpu.VMEM((n_units, nr, cols), dt)

    clamp = lambda s: jnp.minimum(s, n_all - 1)
    chunk_f = lambda s: clamp(s)
    chunk_b = lambda s: jnp.where(clamp(s) < n_ctx, n_ctx - 1 - clamp(s), n_all + n_ctx - 1 - clamp(s))
    prev = lambda s: jnp.maximum(s - 1, 0)
    out_f = lambda s: jnp.maximum(prev(s) - n_ctx, 0)
    out_b = lambda s: jnp.where(prev(s) < n_ctx, n_lat - 1, chunk_b(prev(s)) - n_ctx)

    sh_f = pl.BlockSpec((bsz, c_n, rw), lambda s: (0, chunk_f(s), 0))
    sh_b = pl.BlockSpec((bsz, c_n, rw), lambda s: (0, chunk_b(s), 0))
    pd_f = pl.BlockSpec((1, bsz, c_n, rw), lambda s: (0, 0, chunk_f(s), 0))
    pd_b = pl.BlockSpec((1, bsz, c_n, rw), lambda s: (1, 0, chunk_b(s), 0))
    y_shape = jax.ShapeDtypeStruct((bsz, n_lat * c_n, rw), F32)
    return pl.pallas_call(
        functools.partial(_rw_step_kernel, n_pairs=n_pairs, bsz=bsz),
        out_shape=(y_shape, y_shape),
        grid=(n_all + 1,),
        in_specs=[sh_f, sh_b, sh_f, sh_b, sh_f, sh_b, pd_f, pd_b, pd_f, pd_b, pd_f, pd_b],
        out_specs=(pl.BlockSpec((bsz, c_n, rw), lambda s: (0, out_f(s), 0)),
                   pl.BlockSpec((bsz, c_n, rw), lambda s: (0, out_b(s), 0))),
        scratch_shapes=[vm(n2, n2, F32),
                        pltpu.VMEM((2 * bsz, 1, rw), F32),
                        vm(2 * n2, n2, BF16),
                        vm(2 * n2, n2, BF16),
                        vm(n2, n2, BF16),
                        vm(n2, n2, BF16),
                        vm(n2, n2, BF16),
                        vm(n2, n2, F32),
                        vm(n2, n2, F32),
                        vm(2 * n2, n2, BF16),
                        vm(n2, n2, BF16),
                        vm(n2, n2, F32),
                        vm(n2, n2, BF16),
                        vm(n2, 2 * n2, BF16),
                        vm(n2, n2, F32),
                        vm(n2, 2 * n2, BF16),
                        vm(2 * n2, n2, BF16),
                        vm(2 * n2, n2, F32)],
        compiler_params=_cparams(("arbitrary",)),
        name="rwkv_scan",
    )(r, r, v, v, kk, kk, lw, lw, kd, kd, be, be)


def _s5_glu_kernel(y_ref, wa_ref, wb_ref, o_ref, h_scr):
    @pl.when(pl.program_id(1) == 0)
    def _():
        for jb in range(y_ref.shape[0]):
            h_scr[:, jb * LANES:(jb + 1) * LANES] = _gelu_tanh(y_ref[jb]).astype(BF16)

    h = h_scr[...]
    o_ref[...] = (_dot(h, wa_ref[...]) * _sigmoid(_dot(h, wb_ref[...]))).astype(o_ref.dtype)


def _s5_glu(y_blk, w, *, tm, tn):
    nb, m, _ = y_blk.shape
    k = nb * LANES
    n = w.shape[1] // 2
    nj = n // tn
    return pl.pallas_call(
        _s5_glu_kernel,
        out_shape=jax.ShapeDtypeStruct((m, n), BF16),
        grid=(m // tm, nj),
        in_specs=[pl.BlockSpec((nb, tm, LANES), lambda i, j: (0, i, 0)),
                  pl.BlockSpec((k, tn), lambda i, j: (0, j)),
                  pl.BlockSpec((k, tn), lambda i, j: (0, j + nj))],
        out_specs=pl.BlockSpec((tm, tn), lambda i, j: (i, j)),
        scratch_shapes=[pltpu.VMEM((tm, k), BF16)],
        compiler_params=_cparams(("parallel", "arbitrary")),
        name="s5_glu",
    )(y_blk, w, w)


def _merge_out_kernel(yf_ref, yb_ref, bonus_ref, g_ref, lnw_ref, lnb_ref, seg_ref, segt_ref,
                      ga_ref, gb_ref, s5_ref, wp_ref, wo_ref, x_ref, g1_ref, nw_ref, sh_ref, sc_ref,
                      h_ref, hn_ref):
    seg = seg_ref[...]
    segt = segt_ref[...]
    inv_n = 1.0 / RW_HEAD
    tm = yf_ref.shape[0]
    parts = [slice(0, tm)]

    def head_mean(vals, precise):
        red = _dot_hilo if precise else (lambda a, ind: _dot(a.astype(BF16), ind))
        sums = [red(v, seg) for v in vals]
        return [red(s, segt) * inv_n for s in sums]

    y = [yf_ref[p, :] + yb_ref[p, :] for p in parts]
    dy = [a - m for a, m in zip(y, head_mean(y, True))]
    var = head_mean([a * a for a in dy], False)
    y_rw = []
    for p, a, v in zip(parts, dy, var):
        t = a * lax.rsqrt(v + GN_EPS) * lnw_ref[...] + lnb_ref[...] + bonus_ref[p, :].astype(F32)
        y_rw.append((t * g_ref[p, :].astype(F32)).astype(BF16))
    rw_out = [_dot(a, wp_ref[...]) for a in y_rw]
    merged = [(ga_ref[p, :].astype(F32) * s5_ref[p, :].astype(F32) + gb_ref[p, :].astype(F32) * r).astype(BF16)
              for p, r in zip(parts, rw_out)]
    proj = [_dot(a, wo_ref[...]) for a in merged]
    for p, a in zip(parts, proj):
        h = x_ref[p, :] + g1_ref[0] * a
        h_ref[p, :] = h
        hn_ref[p, :] = _lnmod_rows(h, nw_ref[...], sh_ref[0], sc_ref[0]).astype(hn_ref.dtype)


def _merge_out(y_f, y_b, bonus, g, ln_w, ln_b, seg, segt, gates, s5_out, w_proj, w_o, x2, g_tab, nw,
               sh_tab, sc_tab, mod_row_of_block, *, tm):
    m, rw = y_f.shape
    n = w_proj.shape[1]
    full = lambda shape: pl.BlockSpec(shape, lambda i: (0,) * len(shape))
    rows = lambda width: pl.BlockSpec((tm, width), lambda i: (i, 0))
    mod_map = lambda i: (mod_row_of_block(i), 0, 0)
    return pl.pallas_call(
        _merge_out_kernel,
        out_shape=(jax.ShapeDtypeStruct((m, n), F32), jax.ShapeDtypeStruct((m, n), BF16)),
        grid=(m // tm,),
        in_specs=[rows(rw), rows(rw), rows(rw), rows(rw),
                  full((1, rw)), full((1, rw)), full(seg.shape), full(segt.shape),
                  pl.BlockSpec((tm, n), lambda i: (i, 0)), pl.BlockSpec((tm, n), lambda i: (i, 1)), rows(n),
                  full(w_proj.shape), full(w_o.shape), rows(n),
                  pl.BlockSpec((1, 1, n), mod_map), full((1, n)),
                  pl.BlockSpec((1, 1, n), mod_map), pl.BlockSpec((1, 1, n), mod_map)],
        out_specs=(rows(n), rows(n)),
        compiler_params=_cparams(("parallel",), vmem=VMEM_LIMIT + 8 * 1024 * 1024),
        name="merge_out_proj",
    )(y_f, y_b, bonus, g, ln_w.reshape(1, rw), ln_b.reshape(1, rw), seg, segt,
      gates, gates, s5_out, w_proj, w_o, x2, g_tab, nw.reshape(1, n), sh_tab, sc_tab)


def _wres_swiglu_kernel(a_ref, w1_ref, w3_ref, o_ref, w1_scr, w3_scr):
    @pl.when(pl.program_id(1) == 0)
    def _():
        w1_scr[...] = w1_ref[...].astype(BF16)
        w3_scr[...] = w3_ref[...].astype(BF16)

    a = a_ref[...]
    o_ref[...] = (_silu(_dot(a, w1_scr[...])) * _dot(a, w3_scr[...])).astype(o_ref.dtype)


def _wres_swiglu(a, w13, d_ff, *, tm, tn):
    m, k = a.shape
    nj = d_ff // tn
    return pl.pallas_call(
        _wres_swiglu_kernel,
        out_shape=jax.ShapeDtypeStruct((m, d_ff), BF16),
        grid=(nj, m // tm),
        in_specs=[pl.BlockSpec((tm, k), lambda j, i: (i, 0)),
                  pl.BlockSpec((k, tn), lambda j, i: (0, j)),
                  pl.BlockSpec((k, tn), lambda j, i: (0, j + nj))],
        out_specs=pl.BlockSpec((tm, tn), lambda j, i: (i, j)),
        scratch_shapes=[pltpu.VMEM((k, tn), BF16), pltpu.VMEM((k, tn), BF16)],
        compiler_params=_cparams(("arbitrary", "arbitrary")),
        name="ffn_up",
    )(a, w13, w13)


def _ffn_down_kernel(a_ref, w_ref, x_ref, g_ref, nf_ref, o_ref):
    kk = pl.program_id(1)
    part = _dot(a_ref[...], w_ref[...])

    @pl.when(kk == 0)
    def _():
        o_ref[...] = part

    @pl.when(jnp.logical_and(kk > 0, kk < pl.num_programs(1) - 1))
    def _():
        o_ref[...] += part

    @pl.when(kk == pl.num_programs(1) - 1)
    def _():
        h = x_ref[...] + g_ref[0] * (o_ref[...] + part)
        ms = jnp.mean(h * h, axis=-1, keepdims=True)
        o_ref[...] = h * lax.rsqrt(ms + NORM_EPS) * nf_ref[...]


def _ffn_down(a, w, x2, g_tab, mod_row_of_block, norm_f, *, tm, tk):
    m, k = a.shape
    n = w.shape[1]
    return pl.pallas_call(
        _ffn_down_kernel,
        out_shape=jax.ShapeDtypeStruct((m, n), F32),
        grid=(m // tm, k // tk),
        in_specs=[pl.BlockSpec((tm, tk), lambda i, kk: (i, kk)),
                  pl.BlockSpec((tk, n), lambda i, kk: (kk, 0)),
                  pl.BlockSpec((tm, n), lambda i, kk: (i, 0)),
                  pl.BlockSpec((1, 1, n), lambda i, kk: (mod_row_of_block(i), 0, 0)),
                  pl.BlockSpec((1, n), lambda i, kk: (0, 0))],
        out_specs=pl.BlockSpec((tm, n), lambda i, kk: (i, 0)),
        compiler_params=_cparams(("parallel", "arbitrary"), vmem=VMEM_LIMIT + 8 * 1024 * 1024),
        name="ffn_down",
    )(a, w, x2, g_tab, norm_f.reshape(1, n))


def kernel(x, c, ctx, c_ctx, ada_w, ada_b, norm1_w, w_in, rw_mu, s5_a_re, s5_a_im, s5_log_dt, s5_b_re, s5_b_im, s5_c_re, s5_c_im, s5_d, s5_glu_w, rw_w0, rw_w2, rw_a0, rw_a2, rw_g2, rw_k_k, rw_k_a, rw_r_k, rw_ln_w, rw_ln_b, rw_proj, w_o, norm2_w, ffn_w13, ffn_w2, norm_f):
    assert ada_w.shape[0] == 1, "single-layer block"
    bsz, l_lat, d = x.shape
    l_ctx = ctx.shape[1]
    l_all = l_ctx + l_lat
    s5w = s5_d.shape[1] * s5_d.shape[2]
    rw = rw_g2.shape[2]
    shift_cols = rw_mu.shape[1]
    d_ff = ffn_w2.shape[1]

    c_rows = jnp.concatenate([c, c_ctx[None], jnp.zeros((8 - bsz - 1, d), F32)], axis=0)
    mod = _modulation(c_rows, ada_w[0], ada_b[0])
    tab = lambda k: mod[:, k * d:(k + 1) * d].reshape(8, 1, d)
    sh1, sc1, g1, sh2, sc2, g2 = (tab(k) for k in range(N_MOD))
    ctx_row = bsz

    tm_in = 512
    m_lat = bsz * l_lat
    x2 = x.reshape(m_lat, d)
    n_mix = s5w + shift_cols
    m_all = m_lat + bsz * l_ctx
    lat_row = lambda t: (lambda i: i // (l_lat // t))

    def mix_mod_row(i):
        return jnp.where(i >= m_lat // tm_in, ctx_row, i // (l_lat // tm_in))

    h_all, u_blk = _lnmod_proj(x2, ctx.reshape(bsz * l_ctx, d), norm1_w[0], sh1, sc1, mix_mod_row,
                               w_in[0], s5w, tm=tm_in)
    tm_all = m_all // 8
    z_rw = _wres_matmul(h_all, w_in[0], s5w, shift_cols, rows=m_all, tm=tm_all, tn=shift_cols // 3,
                        out_dtype=BF16, name="in_proj_rw")
    tm = 1024
    gates = _wres_matmul(h_all, w_in[0], n_mix, w_in.shape[2] - n_mix, rows=m_lat, tm=tm, tn=1024,
                         out_dtype=BF16, epilogue="sigmoid", name="in_proj_gates")

    y_blk = _s5_branch(u_blk, bsz, l_ctx, l_lat, s5_a_re[0], s5_a_im[0], s5_log_dt[0],
                               s5_b_re[0], s5_b_im[0], s5_c_re[0], s5_c_im[0], s5_d[0])
    s5_out = _s5_glu(y_blk, s5_glu_w[0].astype(BF16), tm=tm, tn=1024)

    lora = rw_w2.shape[2]
    zl = jnp.zeros((lora, rw), F32)
    w2bd = jnp.concatenate([jnp.concatenate([rw_w2[0, 0], zl], axis=1),
                            jnp.concatenate([zl, rw_w2[0, 1]], axis=1)], axis=0)
    a2bd = jnp.concatenate([jnp.concatenate([rw_a2[0, 0], zl], axis=1),
                            jnp.concatenate([zl, rw_a2[0, 1]], axis=1)], axis=0)
    head_of = jnp.arange(rw) // RW_HEAD
    seg = (head_of[:, None] == jnp.arange(LANES)[None, :]).astype(BF16)
    segt = seg.T
    r, v, kk, g, bonus, lw, kd, be = _rw_prep(
        z_rw, bsz, l_ctx, l_lat, rw_mu[0], w2bd, a2bd, rw_g2[0], rw_w0[0].reshape(1, 2 * rw),
        rw_a0[0].reshape(1, 2 * rw), rw_k_k[0], rw_k_a[0], rw_r_k[0].reshape(rw), seg, segt)
    y_f, y_b = _rw_scan(r, v, kk, lw, kd, be, l_ctx)

    tm_o = 256
    h1, h1n = _merge_out(y_f.reshape(m_lat, rw), y_b.reshape(m_lat, rw), bonus.reshape(m_lat, rw),
                         g.reshape(m_lat, rw), rw_ln_w[0], rw_ln_b[0], seg, segt, gates, s5_out,
                         rw_proj[0].astype(BF16), w_o[0].astype(BF16), x2, g1, norm2_w[0], sh2, sc2,
                         lat_row(tm_o), tm=tm_o)

    act = _wres_swiglu(h1n, ffn_w13[0], d_ff, tm=tm, tn=512)
    tm_dn = 512
    out = _ffn_down(act, ffn_w2[0].astype(BF16), h1, g2, lat_row(tm_dn), norm_f, tm=tm_dn, tk=d_ff // 2)
    return out.reshape(bsz, l_lat, d)
```
